```python
import jax, jax.numpy as jnp
from jax import lax
import numpy as np

D_MODEL = 1024
BATCH = 32
SEQ = 2048
DEPTH = 2

D_MIX = D_MODEL
W_A = D_MIX // 4
W_B = D_MIX // 4
W_C = D_MIX // 4
W_D = D_MIX // 4
HEAD_DIM = 64
HEADS_C = W_C // HEAD_DIM
HEADS_D = W_D // HEAD_DIM
K_SHORT = 3
K_CONFORMER = 31
CHUNK = 128
Q_BLOCK = 128
D_FF = 2816
K_FFN = 3
RMS_EPS = 1e-6
LN_EPS = 1e-5
IN_A = 3 * W_A
IN_B = 2 * W_B
IN_C = 2 * W_C
IN_D = 3 * W_D
IN_TOTAL = IN_A + IN_B + IN_C + IN_D

kernel_name = "hybrid_parallel_conv_sgu_stickbreaking"


def _rmsnorm(x, g):
    xf = x.astype(jnp.float32)
    y = xf * lax.rsqrt(jnp.mean(xf * xf, axis=-1, keepdims=True) + RMS_EPS)
    return (y * g.astype(jnp.float32)).astype(x.dtype)


def _layernorm(x, g, b):
    xf = x.astype(jnp.float32)
    mu = jnp.mean(xf, axis=-1, keepdims=True)
    xc = xf - mu
    var = jnp.mean(xc * xc, axis=-1, keepdims=True)
    y = xc * lax.rsqrt(var + LN_EPS) * g.astype(jnp.float32) + b.astype(jnp.float32)
    return y.astype(x.dtype)


def _causal_dwconv(x, w):
    k = w.shape[0]
    return lax.conv_general_dilated(
        x, w[:, None, :].astype(x.dtype), window_strides=(1,), padding=[(k - 1, 0)],
        dimension_numbers=("NWC", "WIO", "NWC"), feature_group_count=x.shape[-1])


def _stick_breaking(q, k, v):
    s_len = q.shape[2]
    scale = q.shape[-1] ** -0.5
    outs = []
    for i in range(s_len // Q_BLOCK):
        q0 = i * Q_BLOCK
        kv_len = q0 + Q_BLOCK
        qb = q[:, :, q0:kv_len]
        kb = k[:, :, :kv_len]
        vb = v[:, :, :kv_len]
        z = jnp.einsum("bhtd,bhsd->bhts", qb, kb).astype(jnp.float32) * scale
        t_idx = q0 + jnp.arange(Q_BLOCK)[:, None]
        s_idx = jnp.arange(kv_len)[None, :]
        mask = s_idx < t_idx
        log_beta = jax.nn.log_sigmoid(z)
        log_one_minus = jnp.where(mask, log_beta - z, 0.0)
        after = lax.cumsum(log_one_minus, axis=3, reverse=True) - log_one_minus
        a = jnp.exp(jnp.where(mask, log_beta + after, -jnp.inf))
        outs.append(jnp.einsum("bhts,bhsd->bhtd", a.astype(vb.dtype), vb))
    return jnp.concatenate(outs, axis=2)


def _fwd_setup_inputs(seed: int = 0) -> dict:
    key = jax.random.key(seed)
    ks = jax.random.split(key, 20)
    f32 = jnp.float32
    nrm = lambda k, shape, s: jax.random.normal(k, shape, f32) * s
    x = jax.random.normal(ks[0], (BATCH, SEQ, D_MODEL), f32)
    norm1_g = 1.0 + nrm(ks[1], (DEPTH, D_MODEL), 0.05)
    w_in = nrm(ks[2], (DEPTH, D_MODEL, IN_TOTAL), D_MODEL ** -0.5)
    conv_a_w = nrm(ks[3], (DEPTH, K_SHORT, W_A), K_SHORT ** -0.5)
    conv_b_w = nrm(ks[4], (DEPTH, K_CONFORMER, W_B), K_CONFORMER ** -0.5)
    conv_b_b = nrm(ks[5], (DEPTH, W_B), 0.02)
    ln_b_g = 1.0 + nrm(ks[6], (DEPTH, W_B), 0.05)
    ln_b_b = nrm(ks[7], (DEPTH, W_B), 0.02)
    ln_c_g = 1.0 + nrm(ks[8], (DEPTH, W_C), 0.05)
    ln_c_b = nrm(ks[9], (DEPTH, W_C), 0.02)
    sgu_w = nrm(ks[10], (DEPTH, HEADS_C, CHUNK, CHUNK), 0.5 * CHUNK ** -0.5)
    sgu_b = 1.0 + nrm(ks[11], (DEPTH, HEADS_C, CHUNK), 0.1)
    w_out = nrm(ks[12], (DEPTH, D_MIX, D_MODEL), D_MIX ** -0.5)
    norm2_g = 1.0 + nrm(ks[13], (DEPTH, D_MODEL), 0.05)
    w_up = nrm(ks[14], (DEPTH, D_MODEL, 2 * D_FF), D_MODEL ** -0.5)
    conv_f_w = nrm(ks[15], (DEPTH, K_FFN, 2 * D_FF), K_FFN ** -0.5)
    w_down = nrm(ks[16], (DEPTH, D_FF, D_MODEL), D_FF ** -0.5)
    final_g = 1.0 + nrm(ks[17], (D_MODEL,), 0.05)
    return {"x": x, "norm1_g": norm1_g, "w_in": w_in, "conv_a_w": conv_a_w,
            "conv_b_w": conv_b_w, "conv_b_b": conv_b_b, "ln_b_g": ln_b_g, "ln_b_b": ln_b_b,
            "ln_c_g": ln_c_g, "ln_c_b": ln_c_b, "sgu_w": sgu_w, "sgu_b": sgu_b,
            "w_out": w_out, "norm2_g": norm2_g, "w_up": w_up, "conv_f_w": conv_f_w,
            "w_down": w_down, "final_g": final_g}


def _fwd_reference(x, norm1_g, w_in, conv_a_w, conv_b_w, conv_b_b, ln_b_g, ln_b_b,
              ln_c_g, ln_c_b, sgu_w, sgu_b, w_out, norm2_g, w_up, conv_f_w,
              w_down, final_g):
    bsz, s_len, _ = x.shape
    n_chunks = s_len // CHUNK
    tri = jnp.tril(jnp.ones((CHUNK, CHUNK), dtype=x.dtype))
    for l in range(DEPTH):
        h = _rmsnorm(x, norm1_g[l])
        p = jnp.einsum("bsd,de->bse", h, w_in[l])
        p_a, p_b, p_c, p_d = jnp.split(p, [IN_A, IN_A + IN_B, IN_A + IN_B + IN_C], axis=-1)

        gate_b, gate_c, h_a = jnp.split(p_a, 3, axis=-1)
        y_a = gate_b * _causal_dwconv(gate_c * h_a, conv_a_w[l])

        val_b, gat_b = jnp.split(p_b, 2, axis=-1)
        glu = val_b * jax.nn.sigmoid(gat_b)
        cb = _causal_dwconv(glu, conv_b_w[l]) + conv_b_b[l]
        y_b = jax.nn.silu(_layernorm(cb, ln_b_g[l], ln_b_b[l]))

        uv = jax.nn.gelu(p_c, approximate=False)
        u, v_c = jnp.split(uv, 2, axis=-1)
        v_c = _layernorm(v_c, ln_c_g[l], ln_c_b[l])
        v_c = v_c.reshape(bsz, n_chunks, CHUNK, HEADS_C, HEAD_DIM)
        ws = sgu_w[l] * tri
        sp = jnp.einsum("gts,bnsgc->bntgc", ws, v_c) + sgu_b[l].T[None, None, :, :, None]
        y_c = u * sp.reshape(bsz, s_len, W_C)

        q, k, v = jnp.split(p_d, 3, axis=-1)
        to_heads = lambda t: t.reshape(bsz, s_len, HEADS_D, HEAD_DIM).transpose(0, 2, 1, 3)
        o_d = _stick_breaking(to_heads(q), to_heads(k), to_heads(v))
        y_d = o_d.transpose(0, 2, 1, 3).reshape(bsz, s_len, W_D)

        mix = jnp.concatenate([y_a, y_b, y_c, y_d], axis=-1)
        x = x + jnp.einsum("bse,ed->bsd", mix, w_out[l])

        h2 = _rmsnorm(x, norm2_g[l])
        up = _causal_dwconv(jnp.einsum("bsd,df->bsf", h2, w_up[l]), conv_f_w[l])
        g_f, v_f = jnp.split(up, 2, axis=-1)
        x = x + jnp.einsum("bsf,fd->bsd", jax.nn.silu(g_f) * v_f, w_down[l])
    return _rmsnorm(x, final_g)


import jax as _jax
import jax.numpy as _jnp

TWIN_FORMAT = 'train_step'
FWD_PARAMS = ['x', 'norm1_g', 'w_in', 'conv_a_w', 'conv_b_w', 'conv_b_b', 'ln_b_g', 'ln_b_b', 'ln_c_g', 'ln_c_b', 'sgu_w', 'sgu_b', 'w_out', 'norm2_g', 'w_up', 'conv_f_w', 'w_down', 'final_g']
TWIN_WEIGHTS = ['norm1_g', 'w_in', 'conv_a_w', 'conv_b_w', 'conv_b_b', 'ln_b_g', 'ln_b_b', 'ln_c_g', 'ln_c_b', 'sgu_w', 'sgu_b', 'w_out', 'norm2_g', 'w_up', 'conv_f_w', 'w_down', 'final_g']
TWIN_DIFF_INPUT = 'x'
TWIN_INPUTS = ['x', 'norm1_g', 'w_in', 'conv_a_w', 'conv_b_w', 'conv_b_b', 'ln_b_g', 'ln_b_b', 'ln_c_g', 'ln_c_b', 'sgu_w', 'sgu_b', 'w_out', 'norm2_g', 'w_up', 'conv_f_w', 'w_down', 'final_g', 'loss_target', 'm_norm1_g', 'm_w_in', 'm_conv_a_w', 'm_conv_b_w', 'm_conv_b_b', 'm_ln_b_g', 'm_ln_b_b', 'm_ln_c_g', 'm_ln_c_b', 'm_sgu_w', 'm_sgu_b', 'm_w_out', 'm_norm2_g', 'm_w_up', 'm_conv_f_w', 'm_w_down', 'm_final_g', 'v_norm1_g', 'v_w_in', 'v_conv_a_w', 'v_conv_b_w', 'v_conv_b_b', 'v_ln_b_g', 'v_ln_b_b', 'v_ln_c_g', 'v_ln_c_b', 'v_sgu_w', 'v_sgu_b', 'v_w_out', 'v_norm2_g', 'v_w_up', 'v_conv_f_w', 'v_w_down', 'v_final_g']
TWIN_OUTPUTS = ['loss', 'grad_x', 'grad_norm1_g', 'grad_w_in', 'grad_conv_a_w', 'grad_conv_b_w', 'grad_conv_b_b', 'grad_ln_b_g', 'grad_ln_b_b', 'grad_ln_c_g', 'grad_ln_c_b', 'grad_sgu_w', 'grad_sgu_b', 'grad_w_out', 'grad_norm2_g', 'grad_w_up', 'grad_conv_f_w', 'grad_w_down', 'grad_final_g', 'delta_norm1_g', 'delta_w_in', 'delta_conv_a_w', 'delta_conv_b_w', 'delta_conv_b_b', 'delta_ln_b_g', 'delta_ln_b_b', 'delta_ln_c_g', 'delta_ln_c_b', 'delta_sgu_w', 'delta_sgu_b', 'delta_w_out', 'delta_norm2_g', 'delta_w_up', 'delta_conv_f_w', 'delta_w_down', 'delta_final_g', 'new_m_norm1_g', 'new_m_w_in', 'new_m_conv_a_w', 'new_m_conv_b_w', 'new_m_conv_b_b', 'new_m_ln_b_g', 'new_m_ln_b_b', 'new_m_ln_c_g', 'new_m_ln_c_b', 'new_m_sgu_w', 'new_m_sgu_b', 'new_m_w_out', 'new_m_norm2_g', 'new_m_w_up', 'new_m_conv_f_w', 'new_m_w_down', 'new_m_final_g', 'new_v_norm1_g', 'new_v_w_in', 'new_v_conv_a_w', 'new_v_conv_b_w', 'new_v_conv_b_b', 'new_v_ln_b_g', 'new_v_ln_b_b', 'new_v_ln_c_g', 'new_v_ln_c_b', 'new_v_sgu_w', 'new_v_sgu_b', 'new_v_w_out', 'new_v_norm2_g', 'new_v_w_up', 'new_v_conv_f_w', 'new_v_w_down', 'new_v_final_g']
TWIN_LEAF_KINDS = {'loss': 'loss', 'grad_x': 'grad_x', 'grad_norm1_g': 'grad_w', 'grad_w_in': 'grad_w', 'grad_conv_a_w': 'grad_w', 'grad_conv_b_w': 'grad_w', 'grad_conv_b_b': 'grad_w', 'grad_ln_b_g': 'grad_w', 'grad_ln_b_b': 'grad_w', 'grad_ln_c_g': 'grad_w', 'grad_ln_c_b': 'grad_w', 'grad_sgu_w': 'grad_w', 'grad_sgu_b': 'grad_w', 'grad_w_out': 'grad_w', 'grad_norm2_g': 'grad_w', 'grad_w_up': 'grad_w', 'grad_conv_f_w': 'grad_w', 'grad_w_down': 'grad_w', 'grad_final_g': 'grad_w', 'delta_norm1_g': 'delta_w', 'delta_w_in': 'delta_w', 'delta_conv_a_w': 'delta_w', 'delta_conv_b_w': 'delta_w', 'delta_conv_b_b': 'delta_w', 'delta_ln_b_g': 'delta_w', 'delta_ln_b_b': 'delta_w', 'delta_ln_c_g': 'delta_w', 'delta_ln_c_b': 'delta_w', 'delta_sgu_w': 'delta_w', 'delta_sgu_b': 'delta_w', 'delta_w_out': 'delta_w', 'delta_norm2_g': 'delta_w', 'delta_w_up': 'delta_w', 'delta_conv_f_w': 'delta_w', 'delta_w_down': 'delta_w', 'delta_final_g': 'delta_w', 'new_m_norm1_g': 'new_m', 'new_m_w_in': 'new_m', 'new_m_conv_a_w': 'new_m', 'new_m_conv_b_w': 'new_m', 'new_m_conv_b_b': 'new_m', 'new_m_ln_b_g': 'new_m', 'new_m_ln_b_b': 'new_m', 'new_m_ln_c_g': 'new_m', 'new_m_ln_c_b': 'new_m', 'new_m_sgu_w': 'new_m', 'new_m_sgu_b': 'new_m', 'new_m_w_out': 'new_m', 'new_m_norm2_g': 'new_m', 'new_m_w_up': 'new_m', 'new_m_conv_f_w': 'new_m', 'new_m_w_down': 'new_m', 'new_m_final_g': 'new_m', 'new_v_norm1_g': 'new_v', 'new_v_w_in': 'new_v', 'new_v_conv_a_w': 'new_v', 'new_v_conv_b_w': 'new_v', 'new_v_conv_b_b': 'new_v', 'new_v_ln_b_g': 'new_v', 'new_v_ln_b_b': 'new_v', 'new_v_ln_c_g': 'new_v', 'new_v_ln_c_b': 'new_v', 'new_v_sgu_w': 'new_v', 'new_v_sgu_b': 'new_v', 'new_v_w_out': 'new_v', 'new_v_norm2_g': 'new_v', 'new_v_w_up': 'new_v', 'new_v_conv_f_w': 'new_v', 'new_v_w_down': 'new_v', 'new_v_final_g': 'new_v'}


def _forward(args):
    return _fwd_reference(*[args[k] for k in FWD_PARAMS])


def _output_shape():
    out = _jax.eval_shape(lambda: _forward(_fwd_setup_inputs(0)))
    return out.shape, out.dtype

N_MICROBATCH = 1
ADAM_LR = 0.001
ADAM_B1 = 0.9
ADAM_B2 = 0.999
ADAM_EPS = 1e-08
ADAM_WD = 0.01
ADAM_STEP = 10
PER_EXAMPLE_BATCH_AXIS = {'x': 0, 'loss_target': 0}
SHARED_INPUTS = []
_WEIGHT_DTYPES = {'norm1_g': _jnp.float32, 'w_in': _jnp.float32, 'conv_a_w': _jnp.float32, 'conv_b_w': _jnp.float32, 'conv_b_b': _jnp.float32, 'ln_b_g': _jnp.float32, 'ln_b_b': _jnp.float32, 'ln_c_g': _jnp.float32, 'ln_c_b': _jnp.float32, 'sgu_w': _jnp.float32, 'sgu_b': _jnp.float32, 'w_out': _jnp.float32, 'norm2_g': _jnp.float32, 'w_up': _jnp.float32, 'conv_f_w': _jnp.float32, 'w_down': _jnp.float32, 'final_g': _jnp.float32}
MOMENT_SCALE = {'norm1_g': 2.407345e-01, 'w_in': 1.489942e-01, 'conv_a_w': 2.623014e-01, 'conv_b_w': 1.366318e-01, 'conv_b_b': 3.353342e-01, 'ln_b_g': 1.702731e-01, 'ln_b_b': 1.711902e-01, 'ln_c_g': 5.243601e-02, 'ln_c_b': 5.279499e-02, 'sgu_w': 7.263673e-02, 'sgu_b': 1.020339e-01, 'w_out': 1.712824e-01, 'norm2_g': 1.645780e-01, 'w_up': 7.084156e-02, 'conv_f_w': 6.956920e-02, 'w_down': 1.163261e-01, 'final_g': 6.397430e+01}


def _to_microbatches(a, axis):
    t = _jnp.moveaxis(a, axis, 0)
    t = t.reshape((N_MICROBATCH, t.shape[0] // N_MICROBATCH) + t.shape[1:])
    return _jnp.moveaxis(t, 1, axis + 1)


def setup_inputs(seed: int = 0) -> dict:
    inp = _fwd_setup_inputs(seed)
    key = _jax.random.fold_in(_jax.random.key(seed), 7919)
    shape, _ = _output_shape()
    out = dict(inp)
    out["loss_target"] = _jax.random.normal(_jax.random.fold_in(key, 0), shape, _jnp.float32)
    for i, name in enumerate(TWIN_WEIGHTS):
        w = inp[name].astype(_jnp.float32)
        if MOMENT_SCALE is None:
            s = _jnp.sqrt(_jnp.mean(_jnp.square(w)) + 1e-30)
        else:
            s = MOMENT_SCALE[name]
        km, kv = _jax.random.split(_jax.random.fold_in(key, i + 1))
        out[name] = w
        out["m_" + name] = s * _jax.random.normal(km, w.shape, _jnp.float32)
        out["v_" + name] = (s * s) * _jax.random.uniform(kv, w.shape, _jnp.float32, 0.5, 1.5)
    if N_MICROBATCH > 1:
        for name, axis in PER_EXAMPLE_BATCH_AXIS.items():
            out[name] = _to_microbatches(out[name], axis)
    return {'x': out['x'], 'norm1_g': out['norm1_g'], 'w_in': out['w_in'], 'conv_a_w': out['conv_a_w'], 'conv_b_w': out['conv_b_w'], 'conv_b_b': out['conv_b_b'], 'ln_b_g': out['ln_b_g'], 'ln_b_b': out['ln_b_b'], 'ln_c_g': out['ln_c_g'], 'ln_c_b': out['ln_c_b'], 'sgu_w': out['sgu_w'], 'sgu_b': out['sgu_b'], 'w_out': out['w_out'], 'norm2_g': out['norm2_g'], 'w_up': out['w_up'], 'conv_f_w': out['conv_f_w'], 'w_down': out['w_down'], 'final_g': out['final_g'], 'loss_target': out['loss_target'], 'm_norm1_g': out['m_norm1_g'], 'm_w_in': out['m_w_in'], 'm_conv_a_w': out['m_conv_a_w'], 'm_conv_b_w': out['m_conv_b_w'], 'm_conv_b_b': out['m_conv_b_b'], 'm_ln_b_g': out['m_ln_b_g'], 'm_ln_b_b': out['m_ln_b_b'], 'm_ln_c_g': out['m_ln_c_g'], 'm_ln_c_b': out['m_ln_c_b'], 'm_sgu_w': out['m_sgu_w'], 'm_sgu_b': out['m_sgu_b'], 'm_w_out': out['m_w_out'], 'm_norm2_g': out['m_norm2_g'], 'm_w_up': out['m_w_up'], 'm_conv_f_w': out['m_conv_f_w'], 'm_w_down': out['m_w_down'], 'm_final_g': out['m_final_g'], 'v_norm1_g': out['v_norm1_g'], 'v_w_in': out['v_w_in'], 'v_conv_a_w': out['v_conv_a_w'], 'v_conv_b_w': out['v_conv_b_w'], 'v_conv_b_b': out['v_conv_b_b'], 'v_ln_b_g': out['v_ln_b_g'], 'v_ln_b_b': out['v_ln_b_b'], 'v_ln_c_g': out['v_ln_c_g'], 'v_ln_c_b': out['v_ln_c_b'], 'v_sgu_w': out['v_sgu_w'], 'v_sgu_b': out['v_sgu_b'], 'v_w_out': out['v_w_out'], 'v_norm2_g': out['v_norm2_g'], 'v_w_up': out['v_w_up'], 'v_conv_f_w': out['v_conv_f_w'], 'v_w_down': out['v_w_down'], 'v_final_g': out['v_final_g']}


def _loss(weights, diff, rest, loss_target):
    with _jax.named_scope("forward"):
        args = {**rest, TWIN_DIFF_INPUT: diff, **{k: w.astype(_WEIGHT_DTYPES[k]) for k, w in weights.items()}}
        y = _forward(args)
    with _jax.named_scope("loss_head"):
        err = _jnp.square(y.astype(_jnp.float32) - loss_target)
        return 0.5 * _jnp.sum(_jnp.mean(err, axis=-1)) if err.ndim else 0.5 * err


def _adamw(w, g, m, v):
    m = ADAM_B1 * m + (1.0 - ADAM_B1) * g
    v = ADAM_B2 * v + (1.0 - ADAM_B2) * _jnp.square(g)
    m_hat = m / (1.0 - ADAM_B1 ** ADAM_STEP)
    v_hat = v / (1.0 - ADAM_B2 ** ADAM_STEP)
    delta = -ADAM_LR * (m_hat / (_jnp.sqrt(v_hat) + ADAM_EPS) + ADAM_WD * w)
    return delta, m, v


def reference(x, norm1_g, w_in, conv_a_w, conv_b_w, conv_b_b, ln_b_g, ln_b_b, ln_c_g, ln_c_b, sgu_w, sgu_b, w_out, norm2_g, w_up, conv_f_w, w_down, final_g, loss_target, m_norm1_g, m_w_in, m_conv_a_w, m_conv_b_w, m_conv_b_b, m_ln_b_g, m_ln_b_b, m_ln_c_g, m_ln_c_b, m_sgu_w, m_sgu_b, m_w_out, m_norm2_g, m_w_up, m_conv_f_w, m_w_down, m_final_g, v_norm1_g, v_w_in, v_conv_a_w, v_conv_b_w, v_conv_b_b, v_ln_b_g, v_ln_b_b, v_ln_c_g, v_ln_c_b, v_sgu_w, v_sgu_b, v_w_out, v_norm2_g, v_w_up, v_conv_f_w, v_w_down, v_final_g):
    given = dict(x=x, norm1_g=norm1_g, w_in=w_in, conv_a_w=conv_a_w, conv_b_w=conv_b_w, conv_b_b=conv_b_b, ln_b_g=ln_b_g, ln_b_b=ln_b_b, ln_c_g=ln_c_g, ln_c_b=ln_c_b, sgu_w=sgu_w, sgu_b=sgu_b, w_out=w_out, norm2_g=norm2_g, w_up=w_up, conv_f_w=conv_f_w, w_down=w_down, final_g=final_g, loss_target=loss_target, m_norm1_g=m_norm1_g, m_w_in=m_w_in, m_conv_a_w=m_conv_a_w, m_conv_b_w=m_conv_b_w, m_conv_b_b=m_conv_b_b, m_ln_b_g=m_ln_b_g, m_ln_b_b=m_ln_b_b, m_ln_c_g=m_ln_c_g, m_ln_c_b=m_ln_c_b, m_sgu_w=m_sgu_w, m_sgu_b=m_sgu_b, m_w_out=m_w_out, m_norm2_g=m_norm2_g, m_w_up=m_w_up, m_conv_f_w=m_conv_f_w, m_w_down=m_w_down, m_final_g=m_final_g, v_norm1_g=v_norm1_g, v_w_in=v_w_in, v_conv_a_w=v_conv_a_w, v_conv_b_w=v_conv_b_w, v_conv_b_b=v_conv_b_b, v_ln_b_g=v_ln_b_g, v_ln_b_b=v_ln_b_b, v_ln_c_g=v_ln_c_g, v_ln_c_b=v_ln_c_b, v_sgu_w=v_sgu_w, v_sgu_b=v_sgu_b, v_w_out=v_w_out, v_norm2_g=v_norm2_g, v_w_up=v_w_up, v_conv_f_w=v_conv_f_w, v_w_down=v_w_down, v_final_g=v_final_g)
    weights = {n: given[n] for n in TWIN_WEIGHTS}
    shared = {n: given[n] for n in SHARED_INPUTS}
    per_example = {n: given[n] for n in ['x']}
    grad_fn = _jax.value_and_grad(_loss, argnums=(0, 1))

    def one_microbatch(ex, loss_target):
        ex = dict(ex)
        diff = ex.pop(TWIN_DIFF_INPUT)
        return grad_fn(weights, diff, {**shared, **ex}, loss_target)

    if N_MICROBATCH == 1:
        loss, (grad_w, grad_x) = one_microbatch(per_example, given["loss_target"])
    else:
        def body(carry, xs):
            loss_sum, grad_sum = carry
            l_k, (gw_k, gx_k) = one_microbatch(xs[0], xs[1])
            with _jax.named_scope("update"):
                return (loss_sum + l_k, _jax.tree.map(_jnp.add, grad_sum, gw_k)), gx_k

        init = (_jnp.zeros((), _jnp.float32), _jax.tree.map(_jnp.zeros_like, weights))
        (loss, grad_w), grad_x = _jax.lax.scan(body, init, (per_example, given["loss_target"]))
    with _jax.named_scope("update"):
        delta_w, new_m, new_v = {}, {}, {}
        for n in TWIN_WEIGHTS:
            delta_w[n], new_m[n], new_v[n] = _adamw(weights[n], grad_w[n], given["m_" + n], given["v_" + n])
    return (loss, grad_x, *[grad_w[n] for n in TWIN_WEIGHTS], *[delta_w[n] for n in TWIN_WEIGHTS],
            *[new_m[n] for n in TWIN_WEIGHTS], *[new_v[n] for n in TWIN_WEIGHTS])
```

```python
import functools

import jax
import jax.numpy as jnp
from jax import lax
from jax.experimental import pallas as pl
from jax.experimental.pallas import tpu as pltpu

_F32 = jnp.float32
_BF = jnp.bfloat16

HEAD_DIM = 64
MIX_W = 256
N_HEADS = MIX_W // HEAD_DIM
CHUNK = 128
KV_BLOCK = 128
K_SHORT = 3
K_CONF = 31
K_FFN = 3
RMS_EPS = 1e-6
LN_EPS = 1e-5
ADAM_LR = 0.001
ADAM_B1 = 0.9
ADAM_B2 = 0.999
ADAM_EPS = 1e-08
ADAM_WD = 0.01
ADAM_STEP = 10
N_DEV = 8
VMEM_LIMIT = 56 * 1024 * 1024


def _bf(x):
    return x.astype(_BF)


def _params(*sem):
    return pltpu.CompilerParams(dimension_semantics=sem, vmem_limit_bytes=VMEM_LIMIT)


def _dot(a, b):
    return jnp.dot(a, b, preferred_element_type=_F32)


def _dot_nt(a, b):
    return lax.dot_general(a, b, (((1,), (1,)), ((), ())), preferred_element_type=_F32)


def _dot_tn(a, b):
    return lax.dot_general(a, b, (((0,), (0,)), ((), ())), preferred_element_type=_F32)


def _row_tile(t, want):
    return want if t % want == 0 else t


def _pick_tile(rows, unit, max_rows):
    best = 0
    for cand in range(unit, min(rows, max_rows) + 1, unit):
        if rows % cand == 0:
            best = cand
    return best or rows


def _sigmoid(x):
    return 1.0 / (1.0 + jnp.exp(-x))


def _rms_rstd(x):
    return lax.rsqrt(jnp.mean(x * x, axis=-1, keepdims=True) + RMS_EPS)


def _norm_mm(x, g, w_t, name):
    t, d = x.shape
    n = w_t.shape[0]
    tm = _row_tile(t, 512)
    tn = _row_tile(n, 512)

    def body(x_ref, g_ref, w_ref, p_ref, h_ref, h_s):
        @pl.when(pl.program_id(1) == 0)
        def _():
            xv = x_ref[...]
            h = _bf(xv * _rms_rstd(xv) * g_ref[...])
            h_s[...] = h
            h_ref[...] = h

        p_ref[...] = _dot_nt(h_s[...], w_ref[...])

    return pl.pallas_call(
        body, name=name, grid=(t // tm, n // tn),
        in_specs=[pl.BlockSpec((tm, d), lambda i, j: (i, 0)), pl.BlockSpec((1, d), lambda i, j: (0, 0)),
                  pl.BlockSpec((tn, d), lambda i, j: (j, 0))],
        out_specs=[pl.BlockSpec((tm, tn), lambda i, j: (i, j)), pl.BlockSpec((tm, d), lambda i, j: (i, 0))],
        out_shape=[jax.ShapeDtypeStruct((t, n), _F32), jax.ShapeDtypeStruct((t, d), _BF)],
        scratch_shapes=[pltpu.VMEM((tm, d), _BF)],
        compiler_params=_params("parallel", "arbitrary"),
    )(x, g, w_t)


def _mm_nt(a, w_t, name):
    t, k = a.shape
    n = w_t.shape[0]
    tm = _row_tile(t, 512)
    tn = _row_tile(n, 512)

    def body(a_ref, w_ref, o_ref):
        o_ref[...] = _dot_nt(a_ref[...], w_ref[...])

    return pl.pallas_call(
        body, name=name, grid=(t // tm, n // tn),
        in_specs=[pl.BlockSpec((tm, k), lambda i, j: (i, 0)), pl.BlockSpec((tn, k), lambda i, j: (j, 0))],
        out_specs=pl.BlockSpec((tm, tn), lambda i, j: (i, j)),
        out_shape=jax.ShapeDtypeStruct((t, n), _F32),
        compiler_params=_params("parallel", "arbitrary"),
    )(a, w_t)


def _mm_res(a, w, x, name):
    t, k = a.shape
    d = w.shape[1]
    tm = _row_tile(t, 512)

    def body(a_ref, w_ref, x_ref, o_ref):
        o_ref[...] = x_ref[...] + _dot(a_ref[...], w_ref[...])

    return pl.pallas_call(
        body, name=name, grid=(t // tm,),
        in_specs=[pl.BlockSpec((tm, k), lambda i: (i, 0)), pl.BlockSpec((k, d), lambda i: (0, 0)),
                  pl.BlockSpec((tm, d), lambda i: (i, 0))],
        out_specs=pl.BlockSpec((tm, d), lambda i: (i, 0)),
        out_shape=jax.ShapeDtypeStruct((t, d), _F32),
        compiler_params=_params("parallel"),
    )(a, w, x)


def _mm_normbwd(a, w, x, g, dres, name):
    t, k = a.shape
    d = w.shape[1]
    tm = _row_tile(t, 256)

    def body(a_ref, w_ref, x_ref, g_ref, r_ref, dx_ref, dxb_ref, dg_ref):
        dh = _dot(a_ref[...], w_ref[...])
        xv = x_ref[...]
        rstd = _rms_rstd(xv)
        xn = xv * rstd
        u = dh * g_ref[...]
        dx = r_ref[...] + rstd * (u - xn * jnp.mean(u * xn, axis=-1, keepdims=True))
        dx_ref[...] = dx
        dxb_ref[...] = _bf(dx)

        @pl.when(pl.program_id(0) == 0)
        def _():
            dg_ref[...] = jnp.zeros_like(dg_ref)

        dg_ref[...] += jnp.sum(dh * xn, axis=0, keepdims=True)

    return pl.pallas_call(
        body, name=name, grid=(t // tm,),
        in_specs=[pl.BlockSpec((tm, k), lambda i: (i, 0)), pl.BlockSpec((k, d), lambda i: (0, 0)),
                  pl.BlockSpec((tm, d), lambda i: (i, 0)), pl.BlockSpec((1, d), lambda i: (0, 0)),
                  pl.BlockSpec((tm, d), lambda i: (i, 0))],
        out_specs=[pl.BlockSpec((tm, d), lambda i: (i, 0)), pl.BlockSpec((tm, d), lambda i: (i, 0)),
                   pl.BlockSpec((1, d), lambda i: (0, 0))],
        out_shape=[jax.ShapeDtypeStruct((t, d), _F32), jax.ShapeDtypeStruct((t, d), _BF),
                   jax.ShapeDtypeStruct((1, d), _F32)],
        compiler_params=_params("arbitrary"),
    )(a, w, x, g, dres)


def _mm_tn(a, b, name, out_dtype):
    t, m = a.shape
    n = b.shape[1]
    tm = _row_tile(m, 512) if m % 512 == 0 else _row_tile(m, 256)
    tn = _row_tile(n, 512)
    tk = _row_tile(t, 1024)
    nk = t // tk

    def body(a_ref, b_ref, o_ref, acc):
        kk = pl.program_id(2)

        @pl.when(kk == 0)
        def _():
            acc[...] = jnp.zeros_like(acc)

        acc[...] += _dot_tn(a_ref[...], b_ref[...])

        @pl.when(kk == nk - 1)
        def _():
            o_ref[...] = acc[...].astype(o_ref.dtype)

    return pl.pallas_call(
        body, name=name, grid=(m // tm, n // tn, nk),
        in_specs=[pl.BlockSpec((tk, tm), lambda i, j, kk: (kk, i)), pl.BlockSpec((tk, tn), lambda i, j, kk: (kk, j))],
        out_specs=pl.BlockSpec((tm, tn), lambda i, j, kk: (i, j)),
        out_shape=jax.ShapeDtypeStruct((m, n), out_dtype),
        scratch_shapes=[pltpu.VMEM((tm, tn), _F32)],
        compiler_params=_params("parallel", "parallel", "arbitrary"),
    )(a, b)


def _final_loss(x, g, target, name):
    t, d = x.shape
    tm = _row_tile(t, 256)

    def body(x_ref, g_ref, t_ref, dx_ref, dxb_ref, dg_ref, loss_ref):
        xv = x_ref[...]
        rstd = _rms_rstd(xv)
        xn = xv * rstd
        err = xn * g_ref[...] - t_ref[...]
        dy = err * (1.0 / d)
        u = dy * g_ref[...]
        dx = rstd * (u - xn * jnp.mean(u * xn, axis=-1, keepdims=True))
        dx_ref[...] = dx
        dxb_ref[...] = _bf(dx)

        @pl.when(pl.program_id(0) == 0)
        def _():
            dg_ref[...] = jnp.zeros_like(dg_ref)
            loss_ref[...] = jnp.zeros_like(loss_ref)

        dg_ref[...] += jnp.sum(dy * xn, axis=0, keepdims=True)
        loss_ref[...] += (0.5 / d) * jnp.sum(jnp.sum(err * err, axis=1, keepdims=True), axis=0, keepdims=True)

    return pl.pallas_call(
        body, name=name, grid=(t // tm,),
        in_specs=[pl.BlockSpec((tm, d), lambda i: (i, 0)), pl.BlockSpec((1, d), lambda i: (0, 0)),
                  pl.BlockSpec((tm, d), lambda i: (i, 0))],
        out_specs=[pl.BlockSpec((tm, d), lambda i: (i, 0)), pl.BlockSpec((tm, d), lambda i: (i, 0)),
                   pl.BlockSpec((1, d), lambda i: (0, 0)), pl.BlockSpec((1, 1), lambda i: (0, 0))],
        out_shape=[jax.ShapeDtypeStruct((t, d), _F32), jax.ShapeDtypeStruct((t, d), _BF),
                   jax.ShapeDtypeStruct((1, d), _F32), jax.ShapeDtypeStruct((1, 1), _F32)],
        compiler_params=_params("arbitrary"),
    )(x, g, target)


def _pad_rows(x, pad):
    return jnp.concatenate([x, jnp.zeros((pad, x.shape[1]), x.dtype)], axis=0)


def _shift_down(xp, s):
    return xp if s == 0 else pltpu.roll(xp, s, 0)


def _shift_up(xp, s):
    return xp if s == 0 else pltpu.roll(xp, xp.shape[0] - s, 0)


def _conv3(xp, w_ref):
    return w_ref[2:3, :] * xp + w_ref[1:2, :] * _shift_down(xp, 1) + w_ref[0:1, :] * _shift_down(xp, 2)


def _conv3_t(dyp, w_ref):
    return w_ref[2:3, :] * dyp + w_ref[1:2, :] * _shift_up(dyp, 1) + w_ref[0:1, :] * _shift_up(dyp, 2)


def _conv3_dw(dyp, xp):
    return [jnp.sum(dyp * _shift_down(xp, 2 - k), axis=0, keepdims=True) for k in range(3)]


def _ffn_mid_fwd(up_pre, wf, n_ex, name):
    t, f2 = up_pre.shape
    f = f2 // 2
    s = t // n_ex
    cb = MIX_W
    nb = f // cb

    def body(ug_ref, uv_ref, wg_ref, wv_ref, act_ref):
        gf = _conv3(_pad_rows(ug_ref[...], 8), wg_ref)[:s]
        vf = _conv3(_pad_rows(uv_ref[...], 8), wv_ref)[:s]
        act_ref[...] = _bf(gf * _sigmoid(gf) * vf)

    return pl.pallas_call(
        body, name=name, grid=(n_ex, nb),
        in_specs=[pl.BlockSpec((s, cb), lambda e, j: (e, j)), pl.BlockSpec((s, cb), lambda e, j: (e, j + nb)),
                  pl.BlockSpec((K_FFN, cb), lambda e, j: (0, j)), pl.BlockSpec((K_FFN, cb), lambda e, j: (0, j + nb))],
        out_specs=pl.BlockSpec((s, cb), lambda e, j: (e, j)),
        out_shape=jax.ShapeDtypeStruct((t, f), _BF),
        compiler_params=_params("parallel", "parallel"),
    )(up_pre, up_pre, wf, wf)


def _ffn_mid_bwd(up_pre, wf, dact, n_ex, name):
    t, f2 = up_pre.shape
    f = f2 // 2
    s = t // n_ex
    cb = MIX_W
    nb = f // cb

    def body(ug_ref, uv_ref, wg_ref, wv_ref, da_ref, du_ref, dw_ref):
        j = pl.program_id(0)
        ugp = _pad_rows(ug_ref[...], 8)
        uvp = _pad_rows(uv_ref[...], 8)
        gf = _conv3(ugp, wg_ref)[:s]
        vf = _conv3(uvp, wv_ref)[:s]
        sg = _sigmoid(gf)
        da = da_ref[...]

        @pl.when(pl.program_id(1) == 0)
        def _():
            dw_ref[...] = jnp.zeros_like(dw_ref)

        def finish(dup, w_ref, xp):
            dupp = _pad_rows(dup, 8)
            du_ref[...] = _bf(_conv3_t(dupp, w_ref)[:s])
            rows = _conv3_dw(dupp, xp)
            for k in range(3):
                dw_ref[k:k + 1, :] += rows[k]

        @pl.when(j < nb)
        def _():
            finish(da * vf * sg * (1.0 + gf * (1.0 - sg)), wg_ref, ugp)

        @pl.when(j >= nb)
        def _():
            finish(da * gf * sg, wv_ref, uvp)

    jm = lambda j: j % nb
    return pl.pallas_call(
        body, name=name, grid=(2 * nb, n_ex),
        in_specs=[pl.BlockSpec((s, cb), lambda j, e: (e, jm(j))), pl.BlockSpec((s, cb), lambda j, e: (e, jm(j) + nb)),
                  pl.BlockSpec((K_FFN, cb), lambda j, e: (0, jm(j))), pl.BlockSpec((K_FFN, cb), lambda j, e: (0, jm(j) + nb)),
                  pl.BlockSpec((s, cb), lambda j, e: (e, jm(j)))],
        out_specs=[pl.BlockSpec((s, cb), lambda j, e: (e, j)), pl.BlockSpec((K_FFN, cb), lambda j, e: (0, j))],
        out_shape=[jax.ShapeDtypeStruct((t, f2), _BF), jax.ShapeDtypeStruct((K_FFN, f2), _F32)],
        compiler_params=_params("parallel", "arbitrary"),
    )(up_pre, up_pre, wf, wf, dact)


def _pcol(s, j):
    return pl.BlockSpec((s, MIX_W), lambda e, j=j: (e, j))


def _vec(rows=1):
    return pl.BlockSpec((rows, MIX_W), lambda e: (0, 0))


def _mix_a_fwd(p, wa, n_ex, name):
    t = p.shape[0]
    s = t // n_ex

    def body(gb_ref, gc_ref, ha_ref, w_ref, y_ref):
        cv = _conv3(_pad_rows(gc_ref[...] * ha_ref[...], 8), w_ref)[:s]
        y_ref[...] = _bf(gb_ref[...] * cv)

    return pl.pallas_call(
        body, name=name, grid=(n_ex,),
        in_specs=[_pcol(s, 0), _pcol(s, 1), _pcol(s, 2), _vec(K_SHORT)],
        out_specs=pl.BlockSpec((s, MIX_W), lambda e: (e, 0)),
        out_shape=jax.ShapeDtypeStruct((t, MIX_W), _BF),
        compiler_params=_params("parallel"),
    )(p, p, p, wa)


def _mix_a_bwd(p, wa, dmix, n_ex, name):
    t = p.shape[0]
    s = t // n_ex

    def body(gb_ref, gc_ref, ha_ref, w_ref, dy_ref, dp_ref, dw_ref):
        gc = gc_ref[...]
        ha = ha_ref[...]
        up = _pad_rows(gc * ha, 8)
        cv = _conv3(up, w_ref)[:s]
        dy = dy_ref[...]
        dcvp = _pad_rows(dy * gb_ref[...], 8)
        du = _conv3_t(dcvp, w_ref)[:s]
        dp_ref[:, 0:MIX_W] = _bf(dy * cv)
        dp_ref[:, MIX_W:2 * MIX_W] = _bf(du * ha)
        dp_ref[:, 2 * MIX_W:3 * MIX_W] = _bf(du * gc)

        @pl.when(pl.program_id(0) == 0)
        def _():
            dw_ref[...] = jnp.zeros_like(dw_ref)

        rows = _conv3_dw(dcvp, up)
        for k in range(3):
            dw_ref[k:k + 1, :] += rows[k]

    return pl.pallas_call(
        body, name=name, grid=(n_ex,),
        in_specs=[_pcol(s, 0), _pcol(s, 1), _pcol(s, 2), _vec(K_SHORT), _pcol(s, 0)],
        out_specs=[pl.BlockSpec((s, 3 * MIX_W), lambda e: (e, 0)), _vec(K_SHORT)],
        out_shape=[jax.ShapeDtypeStruct((t, 3 * MIX_W), _BF), jax.ShapeDtypeStruct((K_SHORT, MIX_W), _F32)],
        compiler_params=_params("arbitrary"),
    )(p, p, p, wa, dmix)


CONF_PAD = 32


def _ln_fwd(x, g, b):
    mu = jnp.mean(x, axis=-1, keepdims=True)
    xc = x - mu
    rstd = lax.rsqrt(jnp.mean(xc * xc, axis=-1, keepdims=True) + LN_EPS)
    xhat = xc * rstd
    return xhat * g + b, xhat, rstd


def _ln_bwd(dy, xhat, rstd, g):
    dxh = dy * g
    return rstd * (dxh - jnp.mean(dxh, axis=-1, keepdims=True) - xhat * jnp.mean(dxh * xhat, axis=-1, keepdims=True))


def _mix_b_fwd(p, wb, bb, lg, lb, n_ex, name):
    t = p.shape[0]
    s = t // n_ex

    def body(val_ref, gat_ref, w_ref, bb_ref, lg_ref, lb_ref, y_ref, cb_ref):
        cur = _pad_rows(val_ref[...] * _sigmoid(gat_ref[...]), CONF_PAD)
        acc = w_ref[K_CONF - 1:K_CONF, :] * cur
        for sh in range(1, K_CONF):
            cur = pltpu.roll(cur, 1, 0)
            acc = acc + w_ref[K_CONF - 1 - sh:K_CONF - sh, :] * cur
        cb = acc[:s] + bb_ref[...]
        cb_ref[...] = cb
        yl, _, _ = _ln_fwd(cb, lg_ref[...], lb_ref[...])
        y_ref[...] = _bf(yl * _sigmoid(yl))

    return pl.pallas_call(
        body, name=name, grid=(n_ex,),
        in_specs=[_pcol(s, 3), _pcol(s, 4), _vec(K_CONF), _vec(), _vec(), _vec()],
        out_specs=[pl.BlockSpec((s, MIX_W), lambda e: (e, 0)), pl.BlockSpec((s, MIX_W), lambda e: (e, 0))],
        out_shape=[jax.ShapeDtypeStruct((t, MIX_W), _BF), jax.ShapeDtypeStruct((t, MIX_W), _F32)],
        compiler_params=_params("parallel"),
    )(p, p, wb, bb, lg, lb)


def _mix_b_bwd(p, cb, wb, lg, lb, dmix, n_ex, name):
    t = p.shape[0]
    s = t // n_ex

    def body(val_ref, gat_ref, cb_ref, w_ref, lg_ref, lb_ref, dy_ref, dp_ref, dw_ref, dbb_ref, dlg_ref, dlb_ref):
        @pl.when(pl.program_id(0) == 0)
        def _():
            for r in (dw_ref, dbb_ref, dlg_ref, dlb_ref):
                r[...] = jnp.zeros_like(r)

        yl, xhat, rstd = _ln_fwd(cb_ref[...], lg_ref[...], lb_ref[...])
        sy = _sigmoid(yl)
        dyl = dy_ref[...] * sy * (1.0 + yl * (1.0 - sy))
        dlg_ref[...] += jnp.sum(dyl * xhat, axis=0, keepdims=True)
        dlb_ref[...] += jnp.sum(dyl, axis=0, keepdims=True)
        dcb = _ln_bwd(dyl, xhat, rstd, lg_ref[...])
        dbb_ref[...] += jnp.sum(dcb, axis=0, keepdims=True)

        val = val_ref[...]
        sg = _sigmoid(gat_ref[...])
        dcbp = _pad_rows(dcb, CONF_PAD)
        cur = _pad_rows(val * sg, CONF_PAD)
        up = dcbp
        dglu = w_ref[K_CONF - 1:K_CONF, :] * up
        dw_ref[K_CONF - 1:K_CONF, :] += jnp.sum(dcbp * cur, axis=0, keepdims=True)
        n_pad = s + CONF_PAD
        for sh in range(1, K_CONF):
            cur = pltpu.roll(cur, 1, 0)
            up = pltpu.roll(up, n_pad - 1, 0)
            k = K_CONF - 1 - sh
            dglu = dglu + w_ref[k:k + 1, :] * up
            dw_ref[k:k + 1, :] += jnp.sum(dcbp * cur, axis=0, keepdims=True)
        dglu = dglu[:s]
        dp_ref[:, 0:MIX_W] = _bf(dglu * sg)
        dp_ref[:, MIX_W:2 * MIX_W] = _bf(dglu * val * sg * (1.0 - sg))

    return pl.pallas_call(
        body, name=name, grid=(n_ex,),
        in_specs=[_pcol(s, 3), _pcol(s, 4), pl.BlockSpec((s, MIX_W), lambda e: (e, 0)), _vec(K_CONF), _vec(), _vec(),
                  _pcol(s, 1)],
        out_specs=[pl.BlockSpec((s, 2 * MIX_W), lambda e: (e, 0)), _vec(K_CONF), _vec(), _vec(), _vec()],
        out_shape=[jax.ShapeDtypeStruct((t, 2 * MIX_W), _BF), jax.ShapeDtypeStruct((K_CONF, MIX_W), _F32),
                   jax.ShapeDtypeStruct((1, MIX_W), _F32), jax.ShapeDtypeStruct((1, MIX_W), _F32),
                   jax.ShapeDtypeStruct((1, MIX_W), _F32)],
        compiler_params=_params("arbitrary"),
    )(p, p, cb, wb, lg, lb, dmix)


_INV_SQRT2 = 0.7071067811865476
_INV_SQRT2PI = 0.3989422804014327


def _gelu(x):
    return 0.5 * x * (1.0 + lax.erf(x * _INV_SQRT2))


def _gelu_grad(x):
    return 0.5 * (1.0 + lax.erf(x * _INV_SQRT2)) + x * _INV_SQRT2PI * jnp.exp(-0.5 * x * x)


def _head_masks(width=MIX_W):
    lane = lax.broadcasted_iota(jnp.int32, (1, width), 1)
    return [(lane >= h * HEAD_DIM) & (lane < (h + 1) * HEAD_DIM) for h in range(N_HEADS)]


def _tril_mask():
    r = lax.broadcasted_iota(jnp.int32, (CHUNK, CHUNK), 0)
    c = lax.broadcasted_iota(jnp.int32, (CHUNK, CHUNK), 1)
    return c <= r


def _sgu_apply(ws_ref, x3, transpose):
    n = x3.shape[0]
    tril = _tril_mask()
    masks = _head_masks()
    xb = _bf(x3)
    out = jnp.zeros(x3.shape, _F32)
    for h in range(N_HEADS):
        w = _bf(jnp.where(tril, ws_ref[h], 0.0))
        wb = jnp.broadcast_to(w[None], (n, CHUNK, CHUNK))
        dims = (((1,), (1,)), ((0,), (0,))) if transpose else (((2,), (1,)), ((0,), (0,)))
        r = lax.dot_general(wb, xb, dims, preferred_element_type=_F32)
        out = out + jnp.where(masks[h][None], r, 0.0)
    return out


def _mix_c_fwd(p, lg, lb, ws, sb_full, n_ex, name):
    t = p.shape[0]
    s = t // n_ex
    nc = s // CHUNK

    def body(pu_ref, pv_ref, lg_ref, lb_ref, ws_ref, sb_ref, y_ref):
        u = _gelu(pu_ref[...])
        vl, _, _ = _ln_fwd(_gelu(pv_ref[...]), lg_ref[...], lb_ref[...])
        sp = _sgu_apply(ws_ref, vl.reshape(nc, CHUNK, MIX_W), False) + sb_ref[...][None]
        y_ref[...] = _bf(u * sp.reshape(s, MIX_W))

    return pl.pallas_call(
        body, name=name, grid=(n_ex,),
        in_specs=[_pcol(s, 5), _pcol(s, 6), _vec(), _vec(),
                  pl.BlockSpec((N_HEADS, CHUNK, CHUNK), lambda e: (0, 0, 0)), pl.BlockSpec((CHUNK, MIX_W), lambda e: (0, 0))],
        out_specs=pl.BlockSpec((s, MIX_W), lambda e: (e, 0)),
        out_shape=jax.ShapeDtypeStruct((t, MIX_W), _BF),
        compiler_params=_params("parallel"),
    )(p, p, lg, lb, ws, sb_full)


def _mix_c_bwd(p, lg, lb, ws, sb_full, dmix, n_ex, name):
    t = p.shape[0]
    s = t // n_ex
    nc = s // CHUNK

    def body(pu_ref, pv_ref, lg_ref, lb_ref, ws_ref, sb_ref, dy_ref, dp_ref, dlg_ref, dlb_ref, dws_ref, dsb_ref):
        @pl.when(pl.program_id(0) == 0)
        def _():
            for r in (dlg_ref, dlb_ref, dws_ref, dsb_ref):
                r[...] = jnp.zeros_like(r)

        pu = pu_ref[...]
        pv = pv_ref[...]
        u = _gelu(pu)
        vl, xhat, rstd = _ln_fwd(_gelu(pv), lg_ref[...], lb_ref[...])
        vl3 = vl.reshape(nc, CHUNK, MIX_W)
        sp = _sgu_apply(ws_ref, vl3, False) + sb_ref[...][None]
        dy = dy_ref[...]
        dp_ref[:, 0:MIX_W] = _bf(dy * sp.reshape(s, MIX_W) * _gelu_grad(pu))
        dsp3 = (dy * u).reshape(nc, CHUNK, MIX_W)
        dsb_full = jnp.sum(dsp3, axis=0)
        masks = _head_masks()
        tril = _tril_mask()
        dspb = _bf(dsp3)
        vlb = _bf(vl3)
        for h in range(N_HEADS):
            dsb_ref[:, h:h + 1] += jnp.sum(jnp.where(masks[h], dsb_full, 0.0), axis=1, keepdims=True)
            dm = jnp.where(masks[h][None], dspb, jnp.zeros_like(dspb))
            g3 = lax.dot_general(dm, vlb, (((2,), (2,)), ((0,), (0,))), preferred_element_type=_F32)
            dws_ref[h] += jnp.where(tril, jnp.sum(g3, axis=0), 0.0)
        dvl = _sgu_apply(ws_ref, dsp3, True).reshape(s, MIX_W)
        dlg_ref[...] += jnp.sum(dvl * xhat, axis=0, keepdims=True)
        dlb_ref[...] += jnp.sum(dvl, axis=0, keepdims=True)
        dp_ref[:, MIX_W:2 * MIX_W] = _bf(_ln_bwd(dvl, xhat, rstd, lg_ref[...]) * _gelu_grad(pv))

    return pl.pallas_call(
        body, name=name, grid=(n_ex,),
        in_specs=[_pcol(s, 5), _pcol(s, 6), _vec(), _vec(),
                  pl.BlockSpec((N_HEADS, CHUNK, CHUNK), lambda e: (0, 0, 0)), pl.BlockSpec((CHUNK, MIX_W), lambda e: (0, 0)),
                  _pcol(s, 2)],
        out_specs=[pl.BlockSpec((s, 2 * MIX_W), lambda e: (e, 0)), _vec(), _vec(),
                   pl.BlockSpec((N_HEADS, CHUNK, CHUNK), lambda e: (0, 0, 0)), pl.BlockSpec((CHUNK, N_HEADS), lambda e: (0, 0))],
        out_shape=[jax.ShapeDtypeStruct((t, 2 * MIX_W), _BF), jax.ShapeDtypeStruct((1, MIX_W), _F32),
                   jax.ShapeDtypeStruct((1, MIX_W), _F32), jax.ShapeDtypeStruct((N_HEADS, CHUNK, CHUNK), _F32),
                   jax.ShapeDtypeStruct((CHUNK, N_HEADS), _F32)],
        compiler_params=_params("arbitrary"),
    )(p, p, lg, lb, ws, sb_full, dmix)


D_QBLOCK = 128


def _split_dot(x, m):
    hi = _bf(x)
    lo = _bf(x - hi.astype(_F32))
    return _dot(hi, m) + _dot(lo, m)


def _sb_scores(qm, kj, j, t_idx):
    z = _dot_nt(qm, kj) * (HEAD_DIM ** -0.5)
    s_idx = j * KV_BLOCK + lax.broadcasted_iota(jnp.int32, (1, KV_BLOCK), 1)
    valid = s_idx < t_idx
    lb = jnp.minimum(z, 0.0) - jnp.log(1.0 + jnp.exp(-jnp.abs(z)))
    c = jnp.where(valid, lb - z, 0.0)
    return valid, lb, c


def _tri(lower):
    r = lax.broadcasted_iota(jnp.int32, (KV_BLOCK, KV_BLOCK), 0)
    c = lax.broadcasted_iota(jnp.int32, (KV_BLOCK, KV_BLOCK), 1)
    return _bf(jnp.where(r > c if lower else r < c, 1.0, 0.0))


def _mix_d_fwd(p, n_ex, name):
    t = p.shape[0]
    s = t // n_ex
    tq = D_QBLOCK
    nq = s // tq
    r = tq // KV_BLOCK

    def body(q_ref, k_ref, v_ref, y_ref, kb, vb):
        i = pl.program_id(1)

        @pl.when(i == 0)
        def _():
            kb[...] = _bf(k_ref[...])
            vb[...] = _bf(v_ref[...])

        q = q_ref[...]
        t_idx = i * tq + lax.broadcasted_iota(jnp.int32, (tq, 1), 0)
        after_m = _tri(True)
        nkb = (i + 1) * r
        total = jnp.zeros((tq, MIX_W), _F32)
        for mask in _head_masks():
            qm = _bf(jnp.where(mask, q, 0.0))

            def step(jj, carry, qm=qm):
                run, acc = carry
                j = nkb - 1 - jj
                rows = pl.ds(pl.multiple_of(j * KV_BLOCK, KV_BLOCK), KV_BLOCK)
                valid, lb, c = _sb_scores(qm, kb[rows, :], j, t_idx)
                a = jnp.where(valid, jnp.exp(lb + _split_dot(c, after_m) + run), 0.0)
                acc = acc + _dot(_bf(a), vb[rows, :])
                return run + jnp.sum(c, axis=1, keepdims=True), acc

            _, acc = lax.fori_loop(0, nkb, step, (jnp.zeros((tq, 1), _F32), jnp.zeros((tq, MIX_W), _F32)))
            total = total + jnp.where(mask, acc, 0.0)
        y_ref[...] = _bf(total)

    return pl.pallas_call(
        body, name=name, grid=(n_ex, nq),
        in_specs=[pl.BlockSpec((tq, MIX_W), lambda e, i: (e * nq + i, 7)), pl.BlockSpec((s, MIX_W), lambda e, i: (e, 8)),
                  pl.BlockSpec((s, MIX_W), lambda e, i: (e, 9))],
        out_specs=pl.BlockSpec((tq, MIX_W), lambda e, i: (e * nq + i, 0)),
        out_shape=jax.ShapeDtypeStruct((t, MIX_W), _BF),
        scratch_shapes=[pltpu.VMEM((s, MIX_W), _BF), pltpu.VMEM((s, MIX_W), _BF)],
        compiler_params=_params("parallel", "arbitrary"),
    )(p, p, p)


def _mix_d_bwd(p, dmix, n_ex, name):
    t = p.shape[0]
    s = t // n_ex
    tq = D_QBLOCK
    nq = s // tq
    r = tq // KV_BLOCK
    scale = HEAD_DIM ** -0.5

    def body(q_ref, k_ref, v_ref, do_ref, dq_ref, dk_ref, dv_ref, kb, vb, runs):
        i = pl.program_id(1)

        @pl.when(i == 0)
        def _():
            kb[...] = _bf(k_ref[...])
            vb[...] = _bf(v_ref[...])
            dk_ref[...] = jnp.zeros_like(dk_ref)
            dv_ref[...] = jnp.zeros_like(dv_ref)

        q = q_ref[...]
        do = do_ref[...]
        t_idx = i * tq + lax.broadcasted_iota(jnp.int32, (tq, 1), 0)
        after_m = _tri(True)
        before_m = _tri(False)
        nkb = (i + 1) * r
        dq_total = jnp.zeros((tq, MIX_W), _F32)
        for mask in _head_masks():
            qm = _bf(jnp.where(mask, q, 0.0))
            dom = _bf(jnp.where(mask, do, 0.0))

            def sweep(jj, run, qm=qm):
                j = nkb - 1 - jj
                rows = pl.ds(pl.multiple_of(j * KV_BLOCK, KV_BLOCK), KV_BLOCK)
                _, _, c = _sb_scores(qm, kb[rows, :], j, t_idx)
                runs[j] = run
                return run + jnp.sum(c, axis=1, keepdims=True)

            lax.fori_loop(0, nkb, sweep, jnp.zeros((tq, 1), _F32))

            def step(j, carry, qm=qm, dom=dom):
                pre, dq = carry
                rows = pl.ds(pl.multiple_of(j * KV_BLOCK, KV_BLOCK), KV_BLOCK)
                kj = kb[rows, :]
                vj = vb[rows, :]
                valid, lb, c = _sb_scores(qm, kj, j, t_idx)
                a = jnp.where(valid, jnp.exp(lb + _split_dot(c, after_m) + runs[j]), 0.0)
                g = a * _dot_nt(dom, vj)
                before = _split_dot(g, before_m) + pre
                sig = jnp.exp(lb)
                dz = _bf(jnp.where(valid, g * (1.0 - sig) - sig * before, 0.0) * scale)
                dk_ref[rows, :] += _dot_tn(dz, qm)
                dv_ref[rows, :] += _dot_tn(_bf(a), dom)
                return pre + jnp.sum(g, axis=1, keepdims=True), dq + _dot(dz, kj)

            _, dq = lax.fori_loop(0, nkb, step, (jnp.zeros((tq, 1), _F32), jnp.zeros((tq, MIX_W), _F32)))
            dq_total = dq_total + jnp.where(mask, dq, 0.0)
        dq_ref[...] = _bf(dq_total)

    return pl.pallas_call(
        body, name=name, grid=(n_ex, nq),
        in_specs=[pl.BlockSpec((tq, MIX_W), lambda e, i: (e * nq + i, 7)), pl.BlockSpec((s, MIX_W), lambda e, i: (e, 8)),
                  pl.BlockSpec((s, MIX_W), lambda e, i: (e, 9)), pl.BlockSpec((tq, MIX_W), lambda e, i: (e * nq + i, 3))],
        out_specs=[pl.BlockSpec((tq, MIX_W), lambda e, i: (e * nq + i, 0)), pl.BlockSpec((s, MIX_W), lambda e, i: (e, 0)),
                   pl.BlockSpec((s, MIX_W), lambda e, i: (e, 0))],
        out_shape=[jax.ShapeDtypeStruct((t, MIX_W), _BF), jax.ShapeDtypeStruct((t, MIX_W), _F32),
                   jax.ShapeDtypeStruct((t, MIX_W), _F32)],
        scratch_shapes=[pltpu.VMEM((s, MIX_W), _BF), pltpu.VMEM((s, MIX_W), _BF),
                        pltpu.VMEM((s // KV_BLOCK, tq, 1), _F32)],
        compiler_params=_params("parallel", "arbitrary"),
    )(p, p, p, dmix)


def _local_fwd_bwd(x, target, w, n_ex):
    depth = len(w["w_in_t"])
    saved = []
    for l in range(depth):
        p, h1 = _norm_mm(x, w["norm1_g"][l], w["w_in_t"][l], "in_proj")
        y_a = _mix_a_fwd(p, w["conv_a_w"][l], n_ex, "mix_a_fwd")
        y_b, cb = _mix_b_fwd(p, w["conv_b_w"][l], w["conv_b_b"][l], w["ln_b_g"][l], w["ln_b_b"][l], n_ex, "mix_b_fwd")
        y_c = _mix_c_fwd(p, w["ln_c_g"][l], w["ln_c_b"][l], w["sgu_w"][l], w["sgu_b_full"][l], n_ex, "mix_c_fwd")
        y_d = _mix_d_fwd(p, n_ex, "mix_d_fwd")
        mix = jnp.concatenate([y_a, y_b, y_c, y_d], axis=1)
        x1 = _mm_res(mix, w["w_out"][l], x, "out_proj")
        up_pre, h2 = _norm_mm(x1, w["norm2_g"][l], w["w_up_t"][l], "up_proj")
        act = _ffn_mid_fwd(up_pre, w["conv_f_w"][l], n_ex, "ffn_mid_fwd")
        x2 = _mm_res(act, w["w_down"][l], x1, "down_proj")
        saved.append((x, h1, p, cb, mix, x1, h2, up_pre, act))
        x = x2

    dx, dxb, d_final_g, loss = _final_loss(x, w["final_g"], target, "final_loss")
    grads = {k: [None] * depth for k in (
        "norm1_g", "w_in_t", "conv_a_w", "conv_b_w", "conv_b_b", "ln_b_g", "ln_b_b", "ln_c_g", "ln_c_b", "sgu_w",
        "sgu_b_t", "w_out", "norm2_g", "w_up_t", "conv_f_w", "w_down")}
    grads["final_g"] = d_final_g
    for l in reversed(range(depth)):
        x0, h1, p, cb, mix, x1, h2, up_pre, act = saved[l]
        dact = _mm_nt(dxb, w["w_down"][l], "down_proj_dx")
        grads["w_down"][l] = _mm_tn(act, dxb, "down_proj_dw", _BF)
        dup, grads["conv_f_w"][l] = _ffn_mid_bwd(up_pre, w["conv_f_w"][l], dact, n_ex, "ffn_mid_bwd")
        dx, dxb, grads["norm2_g"][l] = _mm_normbwd(dup, w["w_up_t"][l], x1, w["norm2_g"][l], dx, "up_proj_dx")
        grads["w_up_t"][l] = _mm_tn(dup, h2, "up_proj_dw", _BF)
        dmix = _mm_nt(dxb, w["w_out"][l], "out_proj_dx")
        grads["w_out"][l] = _mm_tn(mix, dxb, "out_proj_dw", _BF)
        dp_a, grads["conv_a_w"][l] = _mix_a_bwd(p, w["conv_a_w"][l], dmix, n_ex, "mix_a_bwd")
        dp_b, grads["conv_b_w"][l], grads["conv_b_b"][l], grads["ln_b_g"][l], grads["ln_b_b"][l] = _mix_b_bwd(
            p, cb, w["conv_b_w"][l], w["ln_b_g"][l], w["ln_b_b"][l], dmix, n_ex, "mix_b_bwd")
        dp_c, grads["ln_c_g"][l], grads["ln_c_b"][l], grads["sgu_w"][l], grads["sgu_b_t"][l] = _mix_c_bwd(
            p, w["ln_c_g"][l], w["ln_c_b"][l], w["sgu_w"][l], w["sgu_b_full"][l], dmix, n_ex, "mix_c_bwd")
        dq, dk, dv = _mix_d_bwd(p, dmix, n_ex, "mix_d_bwd")
        dp = jnp.concatenate([dp_a, dp_b, dp_c, dq, _bf(dk), _bf(dv)], axis=1)
        dx, dxb, grads["norm1_g"][l] = _mm_normbwd(dp, w["w_in_t"][l], x0, w["norm1_g"][l], dx, "in_proj_dx")
        grads["w_in_t"][l] = _mm_tn(dp, h1, "in_proj_dw", _BF)
    return loss, dx, grads


_MESH = pl.DeviceIdType.MESH
_ANY = pl.BlockSpec(memory_space=pl.ANY)


def _position():
    return lax.axis_index("x"), lax.axis_index("y"), lax.axis_index("c")


def _flat(px, py, pc):
    return 4 * px + 2 * py + pc


def _all_gather(shard, name):
    r, c_ = shard.shape

    def body(x_ref, out_ref, send_sems, recv_sems, local_sem):
        x, y, c = _position()
        me, sibling = (x, y, c), (x, y, 1 - c)
        chips = [(1 - x, y), (x, 1 - y), (1 - x, 1 - y)]

        def copy(k, block, to, src=None):
            slab = out_ref.at[_flat(*block)]
            return pltpu.make_async_remote_copy(
                src_ref=slab if src is None else src, dst_ref=slab, send_sem=send_sems.at[k], recv_sem=recv_sems.at[k],
                device_id=to, device_id_type=_MESH)

        mine = pltpu.make_async_copy(x_ref, out_ref.at[_flat(*me)], local_sem)
        mine.start()
        first = [copy(0, me, sibling, src=x_ref)]
        first += [copy(1 + j, me, (*chip, c), src=x_ref) for j, chip in enumerate(chips)]
        for cp in first:
            cp.start()
        passed = [copy(4 + j, (*chip, c), sibling) for j, chip in enumerate(chips)]
        for j, chip in enumerate(chips):
            copy(1 + j, (*chip, c), me).wait_recv()
            passed[j].start()
        copy(0, sibling, me).wait_recv()
        for j, chip in enumerate(chips):
            copy(4 + j, (*chip, 1 - c), me).wait_recv()
        for cp in first + passed:
            cp.wait_send()
        mine.wait()

    return pl.pallas_call(
        body, name=name, out_shape=jax.ShapeDtypeStruct((N_DEV, r, c_), shard.dtype),
        in_specs=[_ANY], out_specs=_ANY,
        scratch_shapes=[pltpu.SemaphoreType.DMA((7,)), pltpu.SemaphoreType.DMA((7,)), pltpu.SemaphoreType.DMA],
    )(shard)


def _all_to_all(slabs, name):
    _, r, c_ = slabs.shape

    def body(x_ref, out_ref, send_sems, recv_sems, local_sem):
        x, y, c = _position()
        my = _flat(x, y, c)
        mine = pltpu.make_async_copy(x_ref.at[my], out_ref.at[my], local_sem)
        mine.start()
        peers = [((1 - x) if (k + 1) & 4 else x, (1 - y) if (k + 1) & 2 else y, (1 - c) if (k + 1) & 1 else c)
                 for k in range(N_DEV - 1)]

        def copy(k):
            return pltpu.make_async_remote_copy(
                src_ref=x_ref.at[_flat(*peers[k])], dst_ref=out_ref.at[my], send_sem=send_sems.at[k],
                recv_sem=recv_sems.at[k], device_id=peers[k], device_id_type=_MESH)

        def landing(k):
            return pltpu.make_async_remote_copy(
                src_ref=x_ref.at[my], dst_ref=out_ref.at[_flat(*peers[k])], send_sem=send_sems.at[k],
                recv_sem=recv_sems.at[k], device_id=peers[k], device_id_type=_MESH)

        sends = [copy(k) for k in range(N_DEV - 1)]
        for cp in sends:
            cp.start()
        for k in range(N_DEV - 1):
            landing(k).wait_recv()
        for cp in sends:
            cp.wait_send()
        mine.wait()

    return pl.pallas_call(
        body, name=name, out_shape=jax.ShapeDtypeStruct(slabs.shape, slabs.dtype),
        in_specs=[_ANY], out_specs=_ANY,
        scratch_shapes=[pltpu.SemaphoreType.DMA((7,)), pltpu.SemaphoreType.DMA((7,)), pltpu.SemaphoreType.DMA],
    )(slabs)


def _sum_slabs(slabs, name):
    n, r, c_ = slabs.shape
    tr = _pick_tile(r, 16, max(16, (4 << 20) // (n * c_ * slabs.dtype.itemsize)))

    def body(x_ref, o_ref):
        acc = x_ref[0].astype(_F32)
        for k in range(1, n):
            acc = acc + x_ref[k].astype(_F32)
        o_ref[...] = acc

    return pl.pallas_call(
        body, name=name, grid=(r // tr,),
        in_specs=[pl.BlockSpec((n, tr, c_), lambda i: (0, i, 0))],
        out_specs=pl.BlockSpec((tr, c_), lambda i: (i, 0)),
        out_shape=jax.ShapeDtypeStruct((r, c_), _F32),
        compiler_params=_params("parallel"),
    )(slabs)


def _adamw(w, g, m, v, name):
    r, c_ = w.shape
    tr = _pick_tile(r, 8, 512)

    def body(w_ref, g_ref, m_ref, v_ref, d_ref, nm_ref, nv_ref):
        gv = g_ref[...]
        nm = ADAM_B1 * m_ref[...] + (1.0 - ADAM_B1) * gv
        nv = ADAM_B2 * v_ref[...] + (1.0 - ADAM_B2) * (gv * gv)
        m_hat = nm / (1.0 - ADAM_B1 ** ADAM_STEP)
        v_hat = nv / (1.0 - ADAM_B2 ** ADAM_STEP)
        d_ref[...] = -ADAM_LR * (m_hat / (jnp.sqrt(v_hat) + ADAM_EPS) + ADAM_WD * w_ref[...])
        nm_ref[...] = nm
        nv_ref[...] = nv

    spec = pl.BlockSpec((tr, c_), lambda i: (i, 0))
    shape = jax.ShapeDtypeStruct((r, c_), _F32)
    return pl.pallas_call(
        body, name=name, grid=(r // tr,), in_specs=[spec] * 4, out_specs=[spec] * 3, out_shape=[shape] * 3,
        compiler_params=_params("parallel"),
    )(w, g, m, v)


_SMALL = ("norm1_g", "conv_a_w", "conv_b_w", "conv_b_b", "ln_b_g", "ln_b_b", "ln_c_g", "ln_c_b", "sgu_w", "sgu_b",
          "norm2_g", "conv_f_w", "final_g")
_CONV_SHARDED = ("conv_a_w", "conv_b_w", "conv_f_w")
_NAMES = ("norm1_g", "w_in", "conv_a_w", "conv_b_w", "conv_b_b", "ln_b_g", "ln_b_b", "ln_c_g", "ln_c_b", "sgu_w", "sgu_b",
          "w_out", "norm2_g", "w_up", "conv_f_w", "w_down", "final_g")


def _pack_rows(parts, lanes=128, row_multiple=8):
    flat = jnp.concatenate([a.reshape(-1) for a in parts])
    rows = -(-flat.shape[0] // lanes)
    rows = -(-rows // row_multiple) * row_multiple
    return jnp.pad(flat, (0, rows * lanes - flat.shape[0])).reshape(rows, lanes)


def _unpack_rows(packed, shapes):
    flat = packed.reshape(-1)
    out, off = [], 0
    for shp in shapes:
        size = 1
        for s in shp:
            size *= s
        out.append(flat[off:off + size].reshape(shp))
        off += size
    return out


def _gather_conv_weights(conv_a_w, conv_b_w, conv_f_w):
    shards = (conv_a_w, conv_b_w, conv_f_w)
    gathered = _all_gather(_pack_rows(shards), "gather_conv_weights")
    full = []
    per_dev = [_unpack_rows(gathered[d], [s.shape for s in shards]) for d in range(N_DEV)]
    for i in range(len(shards)):
        full.append(jnp.concatenate([per_dev[d][i] for d in range(N_DEV)], axis=-1))
    return full


def kernel(x, norm1_g, w_in, conv_a_w, conv_b_w, conv_b_b, ln_b_g, ln_b_b, ln_c_g, ln_c_b, sgu_w, sgu_b, w_out, norm2_g, w_up, conv_f_w, w_down, final_g, loss_target, m_norm1_g, m_w_in, m_conv_a_w, m_conv_b_w, m_conv_b_b, m_ln_b_g, m_ln_b_b, m_ln_c_g, m_ln_c_b, m_sgu_w, m_sgu_b, m_w_out, m_norm2_g, m_w_up, m_conv_f_w, m_w_down, m_final_g, v_norm1_g, v_w_in, v_conv_a_w, v_conv_b_w, v_conv_b_b, v_ln_b_g, v_ln_b_b, v_ln_c_g, v_ln_c_b, v_sgu_w, v_sgu_b, v_w_out, v_norm2_g, v_w_up, v_conv_f_w, v_w_down, v_final_g):
    weights = dict(norm1_g=norm1_g, w_in=w_in, conv_a_w=conv_a_w, conv_b_w=conv_b_w, conv_b_b=conv_b_b, ln_b_g=ln_b_g,
                   ln_b_b=ln_b_b, ln_c_g=ln_c_g, ln_c_b=ln_c_b, sgu_w=sgu_w, sgu_b=sgu_b, w_out=w_out, norm2_g=norm2_g,
                   w_up=w_up, conv_f_w=conv_f_w, w_down=w_down, final_g=final_g)
    mom1 = dict(norm1_g=m_norm1_g, w_in=m_w_in, conv_a_w=m_conv_a_w, conv_b_w=m_conv_b_w, conv_b_b=m_conv_b_b,
                ln_b_g=m_ln_b_g, ln_b_b=m_ln_b_b, ln_c_g=m_ln_c_g, ln_c_b=m_ln_c_b, sgu_w=m_sgu_w, sgu_b=m_sgu_b,
                w_out=m_w_out, norm2_g=m_norm2_g, w_up=m_w_up, conv_f_w=m_conv_f_w, w_down=m_w_down, final_g=m_final_g)
    mom2 = dict(norm1_g=v_norm1_g, w_in=v_w_in, conv_a_w=v_conv_a_w, conv_b_w=v_conv_b_w, conv_b_b=v_conv_b_b,
                ln_b_g=v_ln_b_g, ln_b_b=v_ln_b_b, ln_c_g=v_ln_c_g, ln_c_b=v_ln_c_b, sgu_w=v_sgu_w, sgu_b=v_sgu_b,
                w_out=v_w_out, norm2_g=v_norm2_g, w_up=v_w_up, conv_f_w=v_conv_f_w, w_down=v_w_down, final_g=v_final_g)
    n_ex, seq, d = x.shape
    depth = w_in.shape[0]
    my = _flat(*_position())

    big_parts = [("w_in", jnp.swapaxes(w_in, 1, 2)), ("w_out", w_out), ("w_up", jnp.swapaxes(w_up, 1, 2)), ("w_down", w_down)]
    slab_rows = [(name, a.shape[1]) for name, a in big_parts]
    packed = jnp.concatenate([_bf(a[l]) for _, a in big_parts for l in range(depth)], axis=0)
    gathered = _all_gather(packed, "gather_weights")
    full, off = {}, 0
    for name, rows in slab_rows:
        full[name] = []
        for l in range(depth):
            full[name].append(gathered[:, off:off + rows, :].reshape(N_DEV * rows, d))
            off += rows
    conv_a_full, conv_b_full, conv_f_full = _gather_conv_weights(conv_a_w, conv_b_w, conv_f_w)

    row = lambda a, l: a[l][None]
    w = {
        "norm1_g": [row(norm1_g, l) for l in range(depth)], "w_in_t": full["w_in"],
        "conv_a_w": [conv_a_full[l] for l in range(depth)], "conv_b_w": [conv_b_full[l] for l in range(depth)],
        "conv_b_b": [row(conv_b_b, l) for l in range(depth)], "ln_b_g": [row(ln_b_g, l) for l in range(depth)],
        "ln_b_b": [row(ln_b_b, l) for l in range(depth)], "ln_c_g": [row(ln_c_g, l) for l in range(depth)],
        "ln_c_b": [row(ln_c_b, l) for l in range(depth)], "sgu_w": [sgu_w[l] for l in range(depth)],
        "sgu_b_full": [jnp.repeat(sgu_b[l].T, HEAD_DIM, axis=1) for l in range(depth)],
        "w_out": full["w_out"], "norm2_g": [row(norm2_g, l) for l in range(depth)], "w_up_t": full["w_up"],
        "conv_f_w": [conv_f_full[l] for l in range(depth)], "w_down": full["w_down"], "final_g": final_g[None],
    }

    loss, dx, g = _local_fwd_bwd(x.reshape(n_ex * seq, d), loss_target.reshape(n_ex * seq, d), w, n_ex)
    loss = lax.psum(loss[0, 0], ("x", "y", "c"))
    grad_x = dx.reshape(n_ex, seq, d)

    big_grads = [("w_in", g["w_in_t"]), ("w_out", g["w_out"]), ("w_up", g["w_up_t"]), ("w_down", g["w_down"])]
    send = jnp.concatenate([gl.reshape(N_DEV, gl.shape[0] // N_DEV, d) for _, gs in big_grads for gl in gs], axis=1)
    reduced = _sum_slabs(_all_to_all(send, "exchange_grads"), "sum_grads")
    grads, off = {}, 0
    for name, rows in slab_rows:
        per_layer = []
        for l in range(depth):
            per_layer.append(reduced[off:off + rows])
            off += rows
        stacked = jnp.stack(per_layer)
        grads[name] = jnp.swapaxes(stacked, 1, 2) if name in ("w_in", "w_up") else stacked

    small_local = {
        "norm1_g": jnp.stack([a[0] for a in g["norm1_g"]]), "conv_a_w": jnp.stack(g["conv_a_w"]),
        "conv_b_w": jnp.stack(g["conv_b_w"]), "conv_b_b": jnp.stack([a[0] for a in g["conv_b_b"]]),
        "ln_b_g": jnp.stack([a[0] for a in g["ln_b_g"]]), "ln_b_b": jnp.stack([a[0] for a in g["ln_b_b"]]),
        "ln_c_g": jnp.stack([a[0] for a in g["ln_c_g"]]), "ln_c_b": jnp.stack([a[0] for a in g["ln_c_b"]]),
        "sgu_w": jnp.stack(g["sgu_w"]), "sgu_b": jnp.stack([a.T for a in g["sgu_b_t"]]),
        "norm2_g": jnp.stack([a[0] for a in g["norm2_g"]]), "conv_f_w": jnp.stack(g["conv_f_w"]),
        "final_g": g["final_g"][0],
    }
    small_sum = _sum_slabs(_all_gather(_pack_rows([small_local[k] for k in _SMALL]), "gather_small_grads"),
                           "sum_small_grads")
    for name, total in zip(_SMALL, _unpack_rows(small_sum, [small_local[k].shape for k in _SMALL])):
        if name in _CONV_SHARDED:
            width = weights[name].shape[-1]
            total = lax.dynamic_slice_in_dim(total, my * width, width, axis=-1)
        grads[name] = total

    delta, new_m, new_v = {}, {}, {}
    for name in _NAMES:
        shp = weights[name].shape
        two_d = (-1, shp[-1]) if len(shp) > 1 else (1, shp[0])
        outs = _adamw(weights[name].reshape(two_d), grads[name].reshape(two_d), mom1[name].reshape(two_d),
                      mom2[name].reshape(two_d), "adamw_" + name)
        delta[name], new_m[name], new_v[name] = (o.reshape(shp) for o in outs)

    return (loss, grad_x, *[grads[n] for n in _NAMES], *[delta[n] for n in _NAMES], *[new_m[n] for n in _NAMES],
            *[new_v[n] for n in _NAMES])
```

```python
import functools

import jax
import jax.numpy as jnp
from jax import lax
from jax.experimental import pallas as pl
from jax.experimental.pallas import tpu as pltpu

_F32 = jnp.float32
_BF = jnp.bfloat16

HEAD_DIM = 64
MIX_W = 256
N_HEADS = MIX_W // HEAD_DIM
CHUNK = 128
KV_BLOCK = 128
K_SHORT = 3
K_CONF = 31
K_FFN = 3
RMS_EPS = 1e-6
LN_EPS = 1e-5
ADAM_LR = 0.001
ADAM_B1 = 0.9
ADAM_B2 = 0.999
ADAM_EPS = 1e-08
ADAM_WD = 0.01
ADAM_STEP = 10
N_DEV = 8
VMEM_LIMIT = 56 * 1024 * 1024


def _bf(x):
    return x.astype(_BF)


def _ld(ref):
    return ref[...].astype(_F32)


def _params(*sem):
    return pltpu.CompilerParams(dimension_semantics=sem, vmem_limit_bytes=VMEM_LIMIT)


def _dot(a, b):
    return jnp.dot(a, b, preferred_element_type=_F32)


def _dot_nt(a, b):
    return lax.dot_general(a, b, (((1,), (1,)), ((), ())), preferred_element_type=_F32)


def _dot_tn(a, b):
    return lax.dot_general(a, b, (((0,), (0,)), ((), ())), preferred_element_type=_F32)


def _row_tile(t, want):
    return want if t % want == 0 else t


def _pick_tile(rows, unit, max_rows):
    best = 0
    for cand in range(unit, min(rows, max_rows) + 1, unit):
        if rows % cand == 0:
            best = cand
    return best or rows


def _sigmoid(x):
    return 1.0 / (1.0 + jnp.exp(-x))


def _rms_rstd(x):
    return lax.rsqrt(jnp.mean(x * x, axis=-1, keepdims=True) + RMS_EPS)


def _norm_mm(x, g, w_t, name):
    t, d = x.shape
    n = w_t.shape[0]
    tm = _row_tile(t, 512)
    tn = _row_tile(n, 512)

    def body(x_ref, g_ref, w_ref, p_ref, h_ref):
        xv = x_ref[...]
        h = _bf(xv * _rms_rstd(xv) * g_ref[...])
        h_ref[...] = h
        for n0 in range(0, n, tn):
            p_ref[:, n0:n0 + tn] = _bf(_dot_nt(h, w_ref[n0:n0 + tn, :]))

    return pl.pallas_call(
        body, name=name, grid=(t // tm,),
        in_specs=[pl.BlockSpec((tm, d), lambda i: (i, 0)), pl.BlockSpec((1, d), lambda i: (0, 0)),
                  pl.BlockSpec((n, d), lambda i: (0, 0))],
        out_specs=[pl.BlockSpec((tm, n), lambda i: (i, 0)), pl.BlockSpec((tm, d), lambda i: (i, 0))],
        out_shape=[jax.ShapeDtypeStruct((t, n), _BF), jax.ShapeDtypeStruct((t, d), _BF)],
        compiler_params=_params("parallel"),
    )(x, g, w_t)


def _mm_nt(a, w_t, name):
    t, k = a.shape
    n = w_t.shape[0]
    tm = _row_tile(t, 512)
    tn = _row_tile(n, 512) if n % 512 == 0 else _row_tile(n, 256)

    def body(a_ref, w_ref, o_ref):
        av = a_ref[...]
        for n0 in range(0, n, tn):
            o_ref[:, n0:n0 + tn] = _bf(_dot_nt(av, w_ref[n0:n0 + tn, :]))

    return pl.pallas_call(
        body, name=name, grid=(t // tm,),
        in_specs=[pl.BlockSpec((tm, k), lambda i: (i, 0)), pl.BlockSpec((n, k), lambda i: (0, 0))],
        out_specs=pl.BlockSpec((tm, n), lambda i: (i, 0)),
        out_shape=jax.ShapeDtypeStruct((t, n), _BF),
        compiler_params=_params("parallel"),
    )(a, w_t)


def _mm_res(parts, w, x, name):
    t = x.shape[0]
    k, d = w.shape
    tm = _row_tile(t, 512)
    widths = [a.shape[1] for a in parts]
    n_parts = len(parts)

    def body(*refs):
        w_ref, x_ref, o_ref = refs[n_parts:]
        acc, off = x_ref[...], 0
        for a_ref, width in zip(refs[:n_parts], widths):
            acc = acc + _dot(a_ref[...], w_ref[off:off + width, :])
            off += width
        o_ref[...] = acc

    return pl.pallas_call(
        body, name=name, grid=(t // tm,),
        in_specs=[pl.BlockSpec((tm, width), lambda i: (i, 0)) for width in widths] + [
            pl.BlockSpec((k, d), lambda i: (0, 0)), pl.BlockSpec((tm, d), lambda i: (i, 0))],
        out_specs=pl.BlockSpec((tm, d), lambda i: (i, 0)),
        out_shape=jax.ShapeDtypeStruct((t, d), _F32),
        compiler_params=_params("parallel"),
    )(*parts, w, x)


def _mm_normbwd(parts, w, x, g, dres, name):
    t = x.shape[0]
    k, d = w.shape
    tm = _row_tile(t, 256)
    widths = [a.shape[1] for a in parts]
    n_parts = len(parts)

    def body(*refs):
        a_refs = refs[:n_parts]
        w_ref, x_ref, g_ref, r_ref, dx_ref, dxb_ref, dg_ref = refs[n_parts:]
        dh, off = None, 0
        for a_ref, width in zip(a_refs, widths):
            term = _dot(_bf(a_ref[...]), w_ref[off:off + width, :])
            dh = term if dh is None else dh + term
            off += width
        xv = x_ref[...]
        rstd = _rms_rstd(xv)
        xn = xv * rstd
        u = dh * g_ref[...]
        dx = r_ref[...] + rstd * (u - xn * jnp.mean(u * xn, axis=-1, keepdims=True))
        dx_ref[...] = dx
        dxb_ref[...] = _bf(dx)

        @pl.when(pl.program_id(0) == 0)
        def _():
            dg_ref[...] = jnp.zeros_like(dg_ref)

        dg_ref[...] += jnp.sum(dh * xn, axis=0, keepdims=True)

    return pl.pallas_call(
        body, name=name, grid=(t // tm,),
        in_specs=[pl.BlockSpec((tm, width), lambda i: (i, 0)) for width in widths] + [
            pl.BlockSpec((k, d), lambda i: (0, 0)),
            pl.BlockSpec((tm, d), lambda i: (i, 0)), pl.BlockSpec((1, d), lambda i: (0, 0)),
            pl.BlockSpec((tm, d), lambda i: (i, 0))],
        out_specs=[pl.BlockSpec((tm, d), lambda i: (i, 0)), pl.BlockSpec((tm, d), lambda i: (i, 0)),
                   pl.BlockSpec((1, d), lambda i: (0, 0))],
        out_shape=[jax.ShapeDtypeStruct((t, d), _F32), jax.ShapeDtypeStruct((t, d), _BF),
                   jax.ShapeDtypeStruct((1, d), _F32)],
        compiler_params=_params("arbitrary"),
    )(*parts, w, x, g, dres)


def _mm_tn(a, b, name, out_dtype):
    t, m = a.shape
    n = b.shape[1]
    tm = _pick_tile(m, 128, 1408)
    tn = _pick_tile(n, 128, 1024)
    tk = _row_tile(t, 1024)
    nk = t // tk

    def body(a_ref, b_ref, o_ref, acc):
        kk = pl.program_id(2)

        @pl.when(kk == 0)
        def _():
            acc[...] = jnp.zeros_like(acc)

        acc[...] += _dot_tn(_bf(a_ref[...]), b_ref[...])

        @pl.when(kk == nk - 1)
        def _():
            o_ref[...] = acc[...].astype(o_ref.dtype)

    return pl.pallas_call(
        body, name=name, grid=(m // tm, n // tn, nk),
        in_specs=[pl.BlockSpec((tk, tm), lambda i, j, kk: (kk, i)), pl.BlockSpec((tk, tn), lambda i, j, kk: (kk, j))],
        out_specs=pl.BlockSpec((tm, tn), lambda i, j, kk: (i, j)),
        out_shape=jax.ShapeDtypeStruct((m, n), out_dtype),
        scratch_shapes=[pltpu.VMEM((tm, tn), _F32)],
        compiler_params=_params("parallel", "parallel", "arbitrary"),
    )(a, b)


def _final_loss(x, g, target, name):
    t, d = x.shape
    tm = _row_tile(t, 256)

    def body(x_ref, g_ref, t_ref, dx_ref, dxb_ref, dg_ref, loss_ref):
        xv = x_ref[...]
        rstd = _rms_rstd(xv)
        xn = xv * rstd
        err = xn * g_ref[...] - t_ref[...]
        dy = err * (1.0 / d)
        u = dy * g_ref[...]
        dx = rstd * (u - xn * jnp.mean(u * xn, axis=-1, keepdims=True))
        dx_ref[...] = dx
        dxb_ref[...] = _bf(dx)

        @pl.when(pl.program_id(0) == 0)
        def _():
            dg_ref[...] = jnp.zeros_like(dg_ref)
            loss_ref[...] = jnp.zeros_like(loss_ref)

        dg_ref[...] += jnp.sum(dy * xn, axis=0, keepdims=True)
        loss_ref[...] += (0.5 / d) * jnp.sum(jnp.sum(err * err, axis=1, keepdims=True), axis=0, keepdims=True)

    return pl.pallas_call(
        body, name=name, grid=(t // tm,),
        in_specs=[pl.BlockSpec((tm, d), lambda i: (i, 0)), pl.BlockSpec((1, d), lambda i: (0, 0)),
                  pl.BlockSpec((tm, d), lambda i: (i, 0))],
        out_specs=[pl.BlockSpec((tm, d), lambda i: (i, 0)), pl.BlockSpec((tm, d), lambda i: (i, 0)),
                   pl.BlockSpec((1, d), lambda i: (0, 0)), pl.BlockSpec((1, 1), lambda i: (0, 0))],
        out_shape=[jax.ShapeDtypeStruct((t, d), _F32), jax.ShapeDtypeStruct((t, d), _BF),
                   jax.ShapeDtypeStruct((1, d), _F32), jax.ShapeDtypeStruct((1, 1), _F32)],
        compiler_params=_params("arbitrary"),
    )(x, g, target)


def _pad_rows(x, pad):
    return jnp.concatenate([x, jnp.zeros((pad, x.shape[1]), x.dtype)], axis=0)


def _shift_down(xp, s):
    return xp if s == 0 else pltpu.roll(xp, s, 0)


def _shift_up(xp, s):
    return xp if s == 0 else pltpu.roll(xp, xp.shape[0] - s, 0)


def _taps3(xp):
    one = _shift_down(xp, 1)
    return xp, one, _shift_down(one, 1)


def _conv3_taps(taps, w_ref):
    return w_ref[2:3, :] * taps[0] + w_ref[1:2, :] * taps[1] + w_ref[0:1, :] * taps[2]


def _conv3(xp, w_ref):
    return _conv3_taps(_taps3(xp), w_ref)


def _conv3_t(dyp, w_ref):
    one = _shift_up(dyp, 1)
    return w_ref[2:3, :] * dyp + w_ref[1:2, :] * one + w_ref[0:1, :] * _shift_up(one, 1)


def _conv3_dw(dyp, taps):
    return [jnp.sum(dyp * taps[2 - k], axis=0, keepdims=True) for k in range(3)]


def _ffn_mid_fwd(up_pre, wf, n_ex, name):
    t, f2 = up_pre.shape
    f = f2 // 2
    s = t // n_ex
    cb = MIX_W
    nb = f // cb

    def body(ug_ref, uv_ref, wg_ref, wv_ref, act_ref):
        gf = _conv3(_pad_rows(ug_ref[...].astype(_F32), 8), wg_ref)[:s]
        vf = _conv3(_pad_rows(uv_ref[...].astype(_F32), 8), wv_ref)[:s]
        act_ref[...] = _bf(gf * _sigmoid(gf) * vf)

    return pl.pallas_call(
        body, name=name, grid=(n_ex, nb),
        in_specs=[pl.BlockSpec((s, cb), lambda e, j: (e, j)), pl.BlockSpec((s, cb), lambda e, j: (e, j + nb)),
                  pl.BlockSpec((K_FFN, cb), lambda e, j: (0, j)), pl.BlockSpec((K_FFN, cb), lambda e, j: (0, j + nb))],
        out_specs=pl.BlockSpec((s, cb), lambda e, j: (e, j)),
        out_shape=jax.ShapeDtypeStruct((t, f), _BF),
        compiler_params=_params("parallel", "parallel"),
    )(up_pre, up_pre, wf, wf)


def _ffn_mid_bwd(up_pre, wf, dact, n_ex, name):
    t, f2 = up_pre.shape
    f = f2 // 2
    s = t // n_ex
    cb = MIX_W
    nb = f // cb

    def body(ug_ref, uv_ref, wg_ref, wv_ref, da_ref, dug_ref, duv_ref, dwg_ref, dwv_ref):
        g_taps = _taps3(_pad_rows(ug_ref[...].astype(_F32), 8))
        v_taps = _taps3(_pad_rows(uv_ref[...].astype(_F32), 8))
        gf = _conv3_taps(g_taps, wg_ref)[:s]
        vf = _conv3_taps(v_taps, wv_ref)[:s]
        sg = _sigmoid(gf)
        da = da_ref[...].astype(_F32)

        @pl.when(pl.program_id(1) == 0)
        def _():
            dwg_ref[...] = jnp.zeros_like(dwg_ref)
            dwv_ref[...] = jnp.zeros_like(dwv_ref)

        def finish(dup, w_ref, taps, du_ref, dw_ref):
            dupp = _pad_rows(dup, 8)
            du_ref[...] = _bf(_conv3_t(dupp, w_ref)[:s])
            rows = _conv3_dw(dupp, taps)
            for k in range(3):
                dw_ref[k:k + 1, :] += rows[k]

        finish(da * vf * sg * (1.0 + gf * (1.0 - sg)), wg_ref, g_taps, dug_ref, dwg_ref)
        finish(da * gf * sg, wv_ref, v_taps, duv_ref, dwv_ref)

    return pl.pallas_call(
        body, name=name, grid=(nb, n_ex),
        in_specs=[pl.BlockSpec((s, cb), lambda j, e: (e, j)), pl.BlockSpec((s, cb), lambda j, e: (e, j + nb)),
                  pl.BlockSpec((K_FFN, cb), lambda j, e: (0, j)), pl.BlockSpec((K_FFN, cb), lambda j, e: (0, j + nb)),
                  pl.BlockSpec((s, cb), lambda j, e: (e, j))],
        out_specs=[pl.BlockSpec((s, cb), lambda j, e: (e, j)), pl.BlockSpec((s, cb), lambda j, e: (e, j)),
                   pl.BlockSpec((K_FFN, cb), lambda j, e: (0, j)), pl.BlockSpec((K_FFN, cb), lambda j, e: (0, j))],
        out_shape=[jax.ShapeDtypeStruct((t, f), _BF), jax.ShapeDtypeStruct((t, f), _BF),
                   jax.ShapeDtypeStruct((K_FFN, f), _F32), jax.ShapeDtypeStruct((K_FFN, f), _F32)],
        compiler_params=_params("parallel", "arbitrary"),
    )(up_pre, up_pre, wf, wf, dact)


def _pcol(s, j):
    return pl.BlockSpec((s, MIX_W), lambda e, j=j: (e, j))


def _vec(rows=1):
    return pl.BlockSpec((rows, MIX_W), lambda e: (0, 0))


def _mix_a_fwd(p, wa, n_ex, name):
    t = p.shape[0]
    s = t // n_ex

    def body(gb_ref, gc_ref, ha_ref, w_ref, y_ref):
        cv = _conv3(_pad_rows(_ld(gc_ref) * _ld(ha_ref), 8), w_ref)[:s]
        y_ref[...] = _bf(_ld(gb_ref) * cv)

    return pl.pallas_call(
        body, name=name, grid=(n_ex,),
        in_specs=[_pcol(s, 0), _pcol(s, 1), _pcol(s, 2), _vec(K_SHORT)],
        out_specs=pl.BlockSpec((s, MIX_W), lambda e: (e, 0)),
        out_shape=jax.ShapeDtypeStruct((t, MIX_W), _BF),
        compiler_params=_params("parallel"),
    )(p, p, p, wa)


def _mix_a_bwd(p, wa, dmix, n_ex, name):
    t = p.shape[0]
    s = t // n_ex

    def body(gb_ref, gc_ref, ha_ref, w_ref, dy_ref, dp_ref, dw_ref):
        gc = _ld(gc_ref)
        ha = _ld(ha_ref)
        up = _taps3(_pad_rows(gc * ha, 8))
        cv = _conv3_taps(up, w_ref)[:s]
        dy = _ld(dy_ref)
        dcvp = _pad_rows(dy * _ld(gb_ref), 8)
        du = _conv3_t(dcvp, w_ref)[:s]
        dp_ref[:, 0:MIX_W] = _bf(dy * cv)
        dp_ref[:, MIX_W:2 * MIX_W] = _bf(du * ha)
        dp_ref[:, 2 * MIX_W:3 * MIX_W] = _bf(du * gc)

        @pl.when(pl.program_id(0) == 0)
        def _():
            dw_ref[...] = jnp.zeros_like(dw_ref)

        rows = _conv3_dw(dcvp, up)
        for k in range(3):
            dw_ref[k:k + 1, :] += rows[k]

    return pl.pallas_call(
        body, name=name, grid=(n_ex,),
        in_specs=[_pcol(s, 0), _pcol(s, 1), _pcol(s, 2), _vec(K_SHORT), _pcol(s, 0)],
        out_specs=[pl.BlockSpec((s, 3 * MIX_W), lambda e: (e, 0)), _vec(K_SHORT)],
        out_shape=[jax.ShapeDtypeStruct((t, 3 * MIX_W), _BF), jax.ShapeDtypeStruct((K_SHORT, MIX_W), _F32)],
        compiler_params=_params("arbitrary"),
    )(p, p, p, wa, dmix)


CONF_PAD = 32


def _ln_fwd(x, g, b):
    mu = jnp.mean(x, axis=-1, keepdims=True)
    xc = x - mu
    rstd = lax.rsqrt(jnp.mean(xc * xc, axis=-1, keepdims=True) + LN_EPS)
    xhat = xc * rstd
    return xhat * g + b, xhat, rstd


def _ln_bwd(dy, xhat, rstd, g):
    dxh = dy * g
    return rstd * (dxh - jnp.mean(dxh, axis=-1, keepdims=True) - xhat * jnp.mean(dxh * xhat, axis=-1, keepdims=True))


def _mix_b_fwd(p, wb, bb, lg, lb, n_ex, name):
    t = p.shape[0]
    s = t // n_ex

    def body(val_ref, gat_ref, w_ref, bb_ref, lg_ref, lb_ref, y_ref, cb_ref):
        cur = _pad_rows(_ld(val_ref) * _sigmoid(_ld(gat_ref)), CONF_PAD)
        acc = w_ref[K_CONF - 1:K_CONF, :] * cur
        for sh in range(1, K_CONF):
            cur = pltpu.roll(cur, 1, 0)
            acc = acc + w_ref[K_CONF - 1 - sh:K_CONF - sh, :] * cur
        cb = acc[:s] + bb_ref[...]
        cb_ref[...] = cb
        yl, _, _ = _ln_fwd(cb, lg_ref[...], lb_ref[...])
        y_ref[...] = _bf(yl * _sigmoid(yl))

    return pl.pallas_call(
        body, name=name, grid=(n_ex,),
        in_specs=[_pcol(s, 3), _pcol(s, 4), _vec(K_CONF), _vec(), _vec(), _vec()],
        out_specs=[pl.BlockSpec((s, MIX_W), lambda e: (e, 0)), pl.BlockSpec((s, MIX_W), lambda e: (e, 0))],
        out_shape=[jax.ShapeDtypeStruct((t, MIX_W), _BF), jax.ShapeDtypeStruct((t, MIX_W), _F32)],
        compiler_params=_params("parallel"),
    )(p, p, wb, bb, lg, lb)


def _mix_b_bwd(p, cb, wb, lg, lb, dmix, n_ex, name):
    t = p.shape[0]
    s = t // n_ex

    def body(val_ref, gat_ref, cb_ref, w_ref, lg_ref, lb_ref, dy_ref, dp_ref, dw_ref, dbb_ref, dlg_ref, dlb_ref):
        @pl.when(pl.program_id(0) == 0)
        def _():
            for r in (dw_ref, dbb_ref, dlg_ref, dlb_ref):
                r[...] = jnp.zeros_like(r)

        yl, xhat, rstd = _ln_fwd(cb_ref[...], lg_ref[...], lb_ref[...])
        sy = _sigmoid(yl)
        dyl = _ld(dy_ref) * sy * (1.0 + yl * (1.0 - sy))
        dlg_ref[...] += jnp.sum(dyl * xhat, axis=0, keepdims=True)
        dlb_ref[...] += jnp.sum(dyl, axis=0, keepdims=True)
        dcb = _ln_bwd(dyl, xhat, rstd, lg_ref[...])
        dbb_ref[...] += jnp.sum(dcb, axis=0, keepdims=True)

        val = _ld(val_ref)
        sg = _sigmoid(_ld(gat_ref))
        dcbp = _pad_rows(dcb, CONF_PAD)
        cur = _pad_rows(val * sg, CONF_PAD)
        up = dcbp
        dglu = w_ref[K_CONF - 1:K_CONF, :] * up
        dw_ref[K_CONF - 1:K_CONF, :] += jnp.sum(dcbp * cur, axis=0, keepdims=True)
        n_pad = s + CONF_PAD
        for sh in range(1, K_CONF):
            cur = pltpu.roll(cur, 1, 0)
            up = pltpu.roll(up, n_pad - 1, 0)
            k = K_CONF - 1 - sh
            dglu = dglu + w_ref[k:k + 1, :] * up
            dw_ref[k:k + 1, :] += jnp.sum(dcbp * cur, axis=0, keepdims=True)
        dglu = dglu[:s]
        dp_ref[:, 0:MIX_W] = _bf(dglu * sg)
        dp_ref[:, MIX_W:2 * MIX_W] = _bf(dglu * val * sg * (1.0 - sg))

    return pl.pallas_call(
        body, name=name, grid=(n_ex,),
        in_specs=[_pcol(s, 3), _pcol(s, 4), pl.BlockSpec((s, MIX_W), lambda e: (e, 0)), _vec(K_CONF), _vec(), _vec(),
                  _pcol(s, 1)],
        out_specs=[pl.BlockSpec((s, 2 * MIX_W), lambda e: (e, 0)), _vec(K_CONF), _vec(), _vec(), _vec()],
        out_shape=[jax.ShapeDtypeStruct((t, 2 * MIX_W), _BF), jax.ShapeDtypeStruct((K_CONF, MIX_W), _F32),
                   jax.ShapeDtypeStruct((1, MIX_W), _F32), jax.ShapeDtypeStruct((1, MIX_W), _F32),
                   jax.ShapeDtypeStruct((1, MIX_W), _F32)],
        compiler_params=_params("arbitrary"),
    )(p, p, cb, wb, lg, lb, dmix)


_INV_SQRT2 = 0.7071067811865476
_INV_SQRT2PI = 0.3989422804014327


def _gelu(x):
    return 0.5 * x * (1.0 + lax.erf(x * _INV_SQRT2))


def _gelu_grad(x):
    return 0.5 * (1.0 + lax.erf(x * _INV_SQRT2)) + x * _INV_SQRT2PI * jnp.exp(-0.5 * x * x)


def _head_masks(width=MIX_W):
    lane = lax.broadcasted_iota(jnp.int32, (1, width), 1)
    return [(lane >= h * HEAD_DIM) & (lane < (h + 1) * HEAD_DIM) for h in range(N_HEADS)]


def _tril_mask():
    r = lax.broadcasted_iota(jnp.int32, (CHUNK, CHUNK), 0)
    c = lax.broadcasted_iota(jnp.int32, (CHUNK, CHUNK), 1)
    return c <= r


def _sgu_apply(ws_ref, x3, transpose):
    n = x3.shape[0]
    tril = _tril_mask()
    masks = _head_masks()
    xb = _bf(x3)
    out = jnp.zeros(x3.shape, _F32)
    for h in range(N_HEADS):
        w = _bf(jnp.where(tril, ws_ref[h], 0.0))
        wb = jnp.broadcast_to(w[None], (n, CHUNK, CHUNK))
        dims = (((1,), (1,)), ((0,), (0,))) if transpose else (((2,), (1,)), ((0,), (0,)))
        r = lax.dot_general(wb, xb, dims, preferred_element_type=_F32)
        out = out + jnp.where(masks[h][None], r, 0.0)
    return out


def _mix_c_fwd(p, lg, lb, ws, sb_full, n_ex, name):
    t = p.shape[0]
    s = t // n_ex
    nc = s // CHUNK

    def body(pu_ref, pv_ref, lg_ref, lb_ref, ws_ref, sb_ref, y_ref):
        u = _gelu(_ld(pu_ref))
        vl, _, _ = _ln_fwd(_gelu(_ld(pv_ref)), lg_ref[...], lb_ref[...])
        sp = _sgu_apply(ws_ref, vl.reshape(nc, CHUNK, MIX_W), False) + sb_ref[...][None]
        y_ref[...] = _bf(u * sp.reshape(s, MIX_W))

    return pl.pallas_call(
        body, name=name, grid=(n_ex,),
        in_specs=[_pcol(s, 5), _pcol(s, 6), _vec(), _vec(),
                  pl.BlockSpec((N_HEADS, CHUNK, CHUNK), lambda e: (0, 0, 0)), pl.BlockSpec((CHUNK, MIX_W), lambda e: (0, 0))],
        out_specs=pl.BlockSpec((s, MIX_W), lambda e: (e, 0)),
        out_shape=jax.ShapeDtypeStruct((t, MIX_W), _BF),
        compiler_params=_params("parallel"),
    )(p, p, lg, lb, ws, sb_full)


def _mix_c_bwd(p, lg, lb, ws, sb_full, dmix, n_ex, name):
    t = p.shape[0]
    s = t // n_ex
    nc = s // CHUNK

    def body(pu_ref, pv_ref, lg_ref, lb_ref, ws_ref, sb_ref, dy_ref, dp_ref, dlg_ref, dlb_ref, dws_ref, dsb_ref):
        @pl.when(pl.program_id(0) == 0)
        def _():
            for r in (dlg_ref, dlb_ref, dws_ref, dsb_ref):
                r[...] = jnp.zeros_like(r)

        pu = _ld(pu_ref)
        pv = _ld(pv_ref)
        u = _gelu(pu)
        vl, xhat, rstd = _ln_fwd(_gelu(pv), lg_ref[...], lb_ref[...])
        vl3 = vl.reshape(nc, CHUNK, MIX_W)
        sp = _sgu_apply(ws_ref, vl3, False) + sb_ref[...][None]
        dy = _ld(dy_ref)
        dp_ref[:, 0:MIX_W] = _bf(dy * sp.reshape(s, MIX_W) * _gelu_grad(pu))
        dsp3 = (dy * u).reshape(nc, CHUNK, MIX_W)
        dsb_full = jnp.sum(dsp3, axis=0)
        masks = _head_masks()
        tril = _tril_mask()
        dspb = _bf(dsp3)
        vlb = _bf(vl3)
        for h in range(N_HEADS):
            dsb_ref[:, h:h + 1] += jnp.sum(jnp.where(masks[h], dsb_full, 0.0), axis=1, keepdims=True)
            dm = jnp.where(masks[h][None], dspb, jnp.zeros_like(dspb))
            g3 = lax.dot_general(dm, vlb, (((2,), (2,)), ((0,), (0,))), preferred_element_type=_F32)
            dws_ref[h] += jnp.where(tril, jnp.sum(g3, axis=0), 0.0)
        dvl = _sgu_apply(ws_ref, dsp3, True).reshape(s, MIX_W)
        dlg_ref[...] += jnp.sum(dvl * xhat, axis=0, keepdims=True)
        dlb_ref[...] += jnp.sum(dvl, axis=0, keepdims=True)
        dp_ref[:, MIX_W:2 * MIX_W] = _bf(_ln_bwd(dvl, xhat, rstd, lg_ref[...]) * _gelu_grad(pv))

    return pl.pallas_call(
        body, name=name, grid=(n_ex,),
        in_specs=[_pcol(s, 5), _pcol(s, 6), _vec(), _vec(),
                  pl.BlockSpec((N_HEADS, CHUNK, CHUNK), lambda e: (0, 0, 0)), pl.BlockSpec((CHUNK, MIX_W), lambda e: (0, 0)),
                  _pcol(s, 2)],
        out_specs=[pl.BlockSpec((s, 2 * MIX_W), lambda e: (e, 0)), _vec(), _vec(),
                   pl.BlockSpec((N_HEADS, CHUNK, CHUNK), lambda e: (0, 0, 0)), pl.BlockSpec((CHUNK, N_HEADS), lambda e: (0, 0))],
        out_shape=[jax.ShapeDtypeStruct((t, 2 * MIX_W), _BF), jax.ShapeDtypeStruct((1, MIX_W), _F32),
                   jax.ShapeDtypeStruct((1, MIX_W), _F32), jax.ShapeDtypeStruct((N_HEADS, CHUNK, CHUNK), _F32),
                   jax.ShapeDtypeStruct((CHUNK, N_HEADS), _F32)],
        compiler_params=_params("arbitrary"),
    )(p, p, lg, lb, ws, sb_full, dmix)


D_QBLOCK = 128
HEAD_COLS = N_HEADS * KV_BLOCK


def _stack_heads(x3):
    return jnp.stack([_bf(jnp.where(m[None], x3, 0.0)) for m in _head_masks()], axis=1)


def _stack_heads_rows(x):
    return jnp.concatenate([_bf(jnp.where(m, x, 0.0)) for m in _head_masks()], axis=0)


def _cols_to_rows(x):
    return jnp.concatenate([x[:, h * KV_BLOCK:(h + 1) * KV_BLOCK] for h in range(N_HEADS)], axis=0)


def _head_sums(x):
    return [jnp.sum(x[:, h * KV_BLOCK:(h + 1) * KV_BLOCK], axis=1, keepdims=True) for h in range(N_HEADS)]


def _spread(cols):
    tq = cols[0].shape[0]
    return jnp.concatenate([jnp.broadcast_to(c, (tq, KV_BLOCK)) for c in cols], axis=1)


def _pair_dot(x, m2):
    half = 2 * KV_BLOCK
    xb = _bf(x)
    return jnp.concatenate([_dot(xb[:, :half], m2), _dot(xb[:, half:], m2)], axis=1)


def _tri2(lower):
    n = 2 * KV_BLOCK
    r = lax.broadcasted_iota(jnp.int32, (n, n), 0)
    c = lax.broadcasted_iota(jnp.int32, (n, n), 1)
    same = (r >= KV_BLOCK) == (c >= KV_BLOCK)
    return _bf(jnp.where(same & (r > c if lower else r < c), 1.0, 0.0))


def _sb_scores(qb, kc, j, t_idx):
    z = _dot_nt(qb, kc) * (HEAD_DIM ** -0.5)
    lane = lax.broadcasted_iota(jnp.int32, (1, HEAD_COLS), 1)
    valid = (j * KV_BLOCK + (lane & (KV_BLOCK - 1))) < t_idx
    lb = jnp.minimum(z, 0.0) - jnp.log(1.0 + jnp.exp(-jnp.abs(z)))
    c = jnp.where(valid, lb - z, 0.0)
    return valid, lb, c


def _mix_d_fwd(p, n_ex, name):
    t = p.shape[0]
    s = t // n_ex
    tq = D_QBLOCK
    nq = s // tq
    r = tq // KV_BLOCK
    nb = s // KV_BLOCK

    def body(q_ref, k_ref, v_ref, y_ref, kc, vc):
        i = pl.program_id(1)

        @pl.when(i == 0)
        def _():
            kc[...] = _stack_heads(k_ref[...].reshape(nb, KV_BLOCK, MIX_W))
            vc[...] = _stack_heads(v_ref[...].reshape(nb, KV_BLOCK, MIX_W))

        qb = _bf(q_ref[...])
        t_idx = i * tq + lax.broadcasted_iota(jnp.int32, (tq, 1), 0)
        after_m = _tri2(True)
        nkb = (i + 1) * r

        def step(jj, carry):
            runs, acc = carry
            j = nkb - 1 - jj
            valid, lb, c = _sb_scores(qb, kc[j].reshape(HEAD_COLS, MIX_W), j, t_idx)
            a = jnp.where(valid, jnp.exp(lb + _pair_dot(c, after_m) + _spread(runs)), 0.0)
            acc = acc + _dot(_bf(a), vc[j].reshape(HEAD_COLS, MIX_W))
            return tuple(ru + cs for ru, cs in zip(runs, _head_sums(c))), acc

        zero = jnp.zeros((tq, 1), _F32)
        _, acc = lax.fori_loop(0, nkb, step, ((zero,) * N_HEADS, jnp.zeros((tq, MIX_W), _F32)))
        y_ref[...] = _bf(acc)

    return pl.pallas_call(
        body, name=name, grid=(n_ex, nq),
        in_specs=[pl.BlockSpec((tq, MIX_W), lambda e, i: (e * nq + i, 7)), pl.BlockSpec((s, MIX_W), lambda e, i: (e, 8)),
                  pl.BlockSpec((s, MIX_W), lambda e, i: (e, 9))],
        out_specs=pl.BlockSpec((tq, MIX_W), lambda e, i: (e * nq + i, 0)),
        out_shape=jax.ShapeDtypeStruct((t, MIX_W), _BF),
        scratch_shapes=[pltpu.VMEM((nb, N_HEADS, KV_BLOCK, MIX_W), _BF), pltpu.VMEM((nb, N_HEADS, KV_BLOCK, MIX_W), _BF)],
        compiler_params=_params("parallel", "arbitrary"),
    )(p, p, p)


def _mix_d_bwd(p, dmix, n_ex, name):
    t = p.shape[0]
    s = t // n_ex
    tq = D_QBLOCK
    nq = s // tq
    r = tq // KV_BLOCK
    nb = s // KV_BLOCK
    scale = HEAD_DIM ** -0.5

    def body(q_ref, k_ref, v_ref, do_ref, dq_ref, dk_ref, dv_ref, kc, vc, runs_ref):
        i = pl.program_id(1)

        @pl.when(i == 0)
        def _():
            kc[...] = _stack_heads(k_ref[...].reshape(nb, KV_BLOCK, MIX_W))
            vc[...] = _stack_heads(v_ref[...].reshape(nb, KV_BLOCK, MIX_W))
            dk_ref[...] = jnp.zeros_like(dk_ref)
            dv_ref[...] = jnp.zeros_like(dv_ref)

        q = q_ref[...]
        do = do_ref[...]
        qb = _bf(q)
        dob = _bf(do)
        q_rows = _stack_heads_rows(q)
        do_rows = _stack_heads_rows(do)
        t_idx = i * tq + lax.broadcasted_iota(jnp.int32, (tq, 1), 0)
        after_m = _tri2(True)
        before_m = _tri2(False)
        nkb = (i + 1) * r
        zero = jnp.zeros((tq, 1), _F32)

        def sweep(jj, runs):
            j = nkb - 1 - jj
            _, _, c = _sb_scores(qb, kc[j].reshape(HEAD_COLS, MIX_W), j, t_idx)
            for h in range(N_HEADS):
                runs_ref[j * N_HEADS + h] = runs[h]
            return tuple(ru + cs for ru, cs in zip(runs, _head_sums(c)))

        lax.fori_loop(0, nkb, sweep, (zero,) * N_HEADS)

        def step(j, carry):
            pres, dq = carry
            rows = pl.ds(pl.multiple_of(j * KV_BLOCK, KV_BLOCK), KV_BLOCK)
            kj = kc[j].reshape(HEAD_COLS, MIX_W)
            valid, lb, c = _sb_scores(qb, kj, j, t_idx)
            runs = [runs_ref[j * N_HEADS + h] for h in range(N_HEADS)]
            a = jnp.where(valid, jnp.exp(lb + _pair_dot(c, after_m) + _spread(runs)), 0.0)
            g = a * _dot_nt(dob, vc[j].reshape(HEAD_COLS, MIX_W))
            before = _pair_dot(g, before_m) + _spread(pres)
            sig = jnp.exp(lb)
            dz = _bf(jnp.where(valid, g * (1.0 - sig) - sig * before, 0.0) * scale)
            dk_ref[rows, :] += _dot_tn(_cols_to_rows(dz), q_rows)
            dv_ref[rows, :] += _dot_tn(_cols_to_rows(_bf(a)), do_rows)
            return tuple(pr + gs for pr, gs in zip(pres, _head_sums(g))), dq + _dot(dz, kj)

        _, dq = lax.fori_loop(0, nkb, step, ((zero,) * N_HEADS, jnp.zeros((tq, MIX_W), _F32)))
        dq_ref[...] = _bf(dq)

    return pl.pallas_call(
        body, name=name, grid=(n_ex, nq),
        in_specs=[pl.BlockSpec((tq, MIX_W), lambda e, i: (e * nq + i, 7)), pl.BlockSpec((s, MIX_W), lambda e, i: (e, 8)),
                  pl.BlockSpec((s, MIX_W), lambda e, i: (e, 9)), pl.BlockSpec((tq, MIX_W), lambda e, i: (e * nq + i, 3))],
        out_specs=[pl.BlockSpec((tq, MIX_W), lambda e, i: (e * nq + i, 0)), pl.BlockSpec((s, MIX_W), lambda e, i: (e, 0)),
                   pl.BlockSpec((s, MIX_W), lambda e, i: (e, 0))],
        out_shape=[jax.ShapeDtypeStruct((t, MIX_W), _BF), jax.ShapeDtypeStruct((t, MIX_W), _F32),
                   jax.ShapeDtypeStruct((t, MIX_W), _F32)],
        scratch_shapes=[pltpu.VMEM((nb, N_HEADS, KV_BLOCK, MIX_W), _BF), pltpu.VMEM((nb, N_HEADS, KV_BLOCK, MIX_W), _BF),
                        pltpu.VMEM((nb * N_HEADS, tq, 1), _F32)],
        compiler_params=_params("parallel", "arbitrary"),
    )(p, p, p, dmix)


def _local_fwd_bwd(x, target, w, n_ex):
    depth = len(w["w_in_t"])
    saved = []
    for l in range(depth):
        p, h1 = _norm_mm(x, w["norm1_g"][l], w["w_in_t"][l], "in_proj")
        y_a = _mix_a_fwd(p, w["conv_a_w"][l], n_ex, "mix_a_fwd")
        y_b, cb = _mix_b_fwd(p, w["conv_b_w"][l], w["conv_b_b"][l], w["ln_b_g"][l], w["ln_b_b"][l], n_ex, "mix_b_fwd")
        y_c = _mix_c_fwd(p, w["ln_c_g"][l], w["ln_c_b"][l], w["sgu_w"][l], w["sgu_b_full"][l], n_ex, "mix_c_fwd")
        y_d = _mix_d_fwd(p, n_ex, "mix_d_fwd")
        mix = (y_a, y_b, y_c, y_d)
        x1 = _mm_res(mix, w["w_out"][l], x, "out_proj")
        up_pre, h2 = _norm_mm(x1, w["norm2_g"][l], w["w_up_t"][l], "up_proj")
        act = _ffn_mid_fwd(up_pre, w["conv_f_w"][l], n_ex, "ffn_mid_fwd")
        x2 = _mm_res((act,), w["w_down"][l], x1, "down_proj")
        saved.append((x, h1, p, cb, mix, x1, h2, up_pre, act))
        x = x2

    dx, dxb, d_final_g, loss = _final_loss(x, w["final_g"], target, "final_loss")
    grads = {k: [None] * depth for k in (
        "norm1_g", "w_in_t", "conv_a_w", "conv_b_w", "conv_b_b", "ln_b_g", "ln_b_b", "ln_c_g", "ln_c_b", "sgu_w",
        "sgu_b_t", "w_out", "norm2_g", "w_up_t", "conv_f_w", "w_down")}
    grads["final_g"] = d_final_g
    for l in reversed(range(depth)):
        x0, h1, p, cb, mix, x1, h2, up_pre, act = saved[l]
        dact = _mm_nt(dxb, w["w_down"][l], "down_proj_dx")
        grads["w_down"][l] = _mm_tn(act, dxb, "down_proj_dw", _BF)
        dup_g, dup_v, dwf_g, dwf_v = _ffn_mid_bwd(up_pre, w["conv_f_w"][l], dact, n_ex, "ffn_mid_bwd")
        grads["conv_f_w"][l] = jnp.concatenate([dwf_g, dwf_v], axis=1)
        dx, dxb, grads["norm2_g"][l] = _mm_normbwd((dup_g, dup_v), w["w_up_t"][l], x1, w["norm2_g"][l], dx, "up_proj_dx")
        grads["w_up_t"][l] = jnp.concatenate([_mm_tn(part, h2, "up_proj_dw", _BF) for part in (dup_g, dup_v)], axis=0)
        dmix = _mm_nt(dxb, w["w_out"][l], "out_proj_dx")
        grads["w_out"][l] = jnp.concatenate([_mm_tn(part, dxb, "out_proj_dw", _BF) for part in mix], axis=0)
        dp_a, grads["conv_a_w"][l] = _mix_a_bwd(p, w["conv_a_w"][l], dmix, n_ex, "mix_a_bwd")
        dp_b, grads["conv_b_w"][l], grads["conv_b_b"][l], grads["ln_b_g"][l], grads["ln_b_b"][l] = _mix_b_bwd(
            p, cb, w["conv_b_w"][l], w["ln_b_g"][l], w["ln_b_b"][l], dmix, n_ex, "mix_b_bwd")
        dp_c, grads["ln_c_g"][l], grads["ln_c_b"][l], grads["sgu_w"][l], grads["sgu_b_t"][l] = _mix_c_bwd(
            p, w["ln_c_g"][l], w["ln_c_b"][l], w["sgu_w"][l], w["sgu_b_full"][l], dmix, n_ex, "mix_c_bwd")
        dq, dk, dv = _mix_d_bwd(p, dmix, n_ex, "mix_d_bwd")
        dp = (dp_a, dp_b, dp_c, dq, dk, dv)
        dx, dxb, grads["norm1_g"][l] = _mm_normbwd(dp, w["w_in_t"][l], x0, w["norm1_g"][l], dx, "in_proj_dx")
        grads["w_in_t"][l] = jnp.concatenate([_mm_tn(part, h1, "in_proj_dw", _BF) for part in dp], axis=0)
    return loss, dx, grads


_MESH = pl.DeviceIdType.MESH
_ANY = pl.BlockSpec(memory_space=pl.ANY)


def _position():
    return lax.axis_index("x"), lax.axis_index("y"), lax.axis_index("c")


def _flat(px, py, pc):
    return 4 * px + 2 * py + pc


def _all_gather(shard, name):
    r, c_ = shard.shape

    def body(x_ref, out_ref, send_sems, recv_sems, local_sem):
        x, y, c = _position()
        me, sibling = (x, y, c), (x, y, 1 - c)
        chips = [(1 - x, y), (x, 1 - y), (1 - x, 1 - y)]

        def copy(k, block, to, src=None):
            slab = out_ref.at[_flat(*block)]
            return pltpu.make_async_remote_copy(
                src_ref=slab if src is None else src, dst_ref=slab, send_sem=send_sems.at[k], recv_sem=recv_sems.at[k],
                device_id=to, device_id_type=_MESH)

        mine = pltpu.make_async_copy(x_ref, out_ref.at[_flat(*me)], local_sem)
        mine.start()
        first = [copy(0, me, sibling, src=x_ref)]
        first += [copy(1 + j, me, (*chip, c), src=x_ref) for j, chip in enumerate(chips)]
        for cp in first:
            cp.start()
        passed = [copy(4 + j, (*chip, c), sibling) for j, chip in enumerate(chips)]
        for j, chip in enumerate(chips):
            copy(1 + j, (*chip, c), me).wait_recv()
            passed[j].start()
        copy(0, sibling, me).wait_recv()
        for j, chip in enumerate(chips):
            copy(4 + j, (*chip, 1 - c), me).wait_recv()
        for cp in first + passed:
            cp.wait_send()
        mine.wait()

    return pl.pallas_call(
        body, name=name, out_shape=jax.ShapeDtypeStruct((N_DEV, r, c_), shard.dtype),
        in_specs=[_ANY], out_specs=_ANY,
        scratch_shapes=[pltpu.SemaphoreType.DMA((7,)), pltpu.SemaphoreType.DMA((7,)), pltpu.SemaphoreType.DMA],
    )(shard)


def _all_to_all(slabs, name):
    _, r, c_ = slabs.shape

    def body(x_ref, out_ref, send_sems, recv_sems, local_sem):
        x, y, c = _position()
        my = _flat(x, y, c)
        mine = pltpu.make_async_copy(x_ref.at[my], out_ref.at[my], local_sem)
        mine.start()
        peers = [((1 - x) if (k + 1) & 4 else x, (1 - y) if (k + 1) & 2 else y, (1 - c) if (k + 1) & 1 else c)
                 for k in range(N_DEV - 1)]

        def copy(k):
            return pltpu.make_async_remote_copy(
                src_ref=x_ref.at[_flat(*peers[k])], dst_ref=out_ref.at[my], send_sem=send_sems.at[k],
                recv_sem=recv_sems.at[k], device_id=peers[k], device_id_type=_MESH)

        def landing(k):
            return pltpu.make_async_remote_copy(
                src_ref=x_ref.at[my], dst_ref=out_ref.at[_flat(*peers[k])], send_sem=send_sems.at[k],
                recv_sem=recv_sems.at[k], device_id=peers[k], device_id_type=_MESH)

        sends = [copy(k) for k in range(N_DEV - 1)]
        for cp in sends:
            cp.start()
        for k in range(N_DEV - 1):
            landing(k).wait_recv()
        for cp in sends:
            cp.wait_send()
        mine.wait()

    return pl.pallas_call(
        body, name=name, out_shape=jax.ShapeDtypeStruct(slabs.shape, slabs.dtype),
        in_specs=[_ANY], out_specs=_ANY,
        scratch_shapes=[pltpu.SemaphoreType.DMA((7,)), pltpu.SemaphoreType.DMA((7,)), pltpu.SemaphoreType.DMA],
    )(slabs)


def _sum_slabs(slabs, name):
    n, r, c_ = slabs.shape
    tr = _pick_tile(r, 16, max(16, (4 << 20) // (n * c_ * slabs.dtype.itemsize)))

    def body(x_ref, o_ref):
        acc = x_ref[0].astype(_F32)
        for k in range(1, n):
            acc = acc + x_ref[k].astype(_F32)
        o_ref[...] = acc

    return pl.pallas_call(
        body, name=name, grid=(r // tr,),
        in_specs=[pl.BlockSpec((n, tr, c_), lambda i: (0, i, 0))],
        out_specs=pl.BlockSpec((tr, c_), lambda i: (i, 0)),
        out_shape=jax.ShapeDtypeStruct((r, c_), _F32),
        compiler_params=_params("parallel"),
    )(slabs)


def _adamw(w, g, m, v, name):
    r, c_ = w.shape
    tr = _pick_tile(r, 8, 512)

    def body(w_ref, g_ref, m_ref, v_ref, d_ref, nm_ref, nv_ref):
        gv = g_ref[...]
        nm = ADAM_B1 * m_ref[...] + (1.0 - ADAM_B1) * gv
        nv = ADAM_B2 * v_ref[...] + (1.0 - ADAM_B2) * (gv * gv)
        m_hat = nm / (1.0 - ADAM_B1 ** ADAM_STEP)
        v_hat = nv / (1.0 - ADAM_B2 ** ADAM_STEP)
        d_ref[...] = -ADAM_LR * (m_hat / (jnp.sqrt(v_hat) + ADAM_EPS) + ADAM_WD * w_ref[...])
        nm_ref[...] = nm
        nv_ref[...] = nv

    spec = pl.BlockSpec((tr, c_), lambda i: (i, 0))
    shape = jax.ShapeDtypeStruct((r, c_), _F32)
    return pl.pallas_call(
        body, name=name, grid=(r // tr,), in_specs=[spec] * 4, out_specs=[spec] * 3, out_shape=[shape] * 3,
        compiler_params=_params("parallel"),
    )(w, g, m, v)


_SMALL = ("norm1_g", "conv_a_w", "conv_b_w", "conv_b_b", "ln_b_g", "ln_b_b", "ln_c_g", "ln_c_b", "sgu_w", "sgu_b",
          "norm2_g", "conv_f_w", "final_g")
_CONV_SHARDED = ("conv_a_w", "conv_b_w", "conv_f_w")
_NAMES = ("norm1_g", "w_in", "conv_a_w", "conv_b_w", "conv_b_b", "ln_b_g", "ln_b_b", "ln_c_g", "ln_c_b", "sgu_w", "sgu_b",
          "w_out", "norm2_g", "w_up", "conv_f_w", "w_down", "final_g")


def _pack_rows(parts, lanes=128, row_multiple=8):
    flat = jnp.concatenate([a.reshape(-1) for a in parts])
    rows = -(-flat.shape[0] // lanes)
    rows = -(-rows // row_multiple) * row_multiple
    return jnp.pad(flat, (0, rows * lanes - flat.shape[0])).reshape(rows, lanes)


def _unpack_rows(packed, shapes):
    flat = packed.reshape(-1)
    out, off = [], 0
    for shp in shapes:
        size = 1
        for s in shp:
            size *= s
        out.append(flat[off:off + size].reshape(shp))
        off += size
    return out


def _gather_conv_weights(conv_a_w, conv_b_w, conv_f_w):
    shards = (conv_a_w, conv_b_w, conv_f_w)
    gathered = _all_gather(_pack_rows(shards), "gather_conv_weights")
    full = []
    per_dev = [_unpack_rows(gathered[d], [s.shape for s in shards]) for d in range(N_DEV)]
    for i in range(len(shards)):
        full.append(jnp.concatenate([per_dev[d][i] for d in range(N_DEV)], axis=-1))
    return full


def kernel(x, norm1_g, w_in, conv_a_w, conv_b_w, conv_b_b, ln_b_g, ln_b_b, ln_c_g, ln_c_b, sgu_w, sgu_b, w_out, norm2_g, w_up, conv_f_w, w_down, final_g, loss_target, m_norm1_g, m_w_in, m_conv_a_w, m_conv_b_w, m_conv_b_b, m_ln_b_g, m_ln_b_b, m_ln_c_g, m_ln_c_b, m_sgu_w, m_sgu_b, m_w_out, m_norm2_g, m_w_up, m_conv_f_w, m_w_down, m_final_g, v_norm1_g, v_w_in, v_conv_a_w, v_conv_b_w, v_conv_b_b, v_ln_b_g, v_ln_b_b, v_ln_c_g, v_ln_c_b, v_sgu_w, v_sgu_b, v_w_out, v_norm2_g, v_w_up, v_conv_f_w, v_w_down, v_final_g):
    weights = dict(norm1_g=norm1_g, w_in=w_in, conv_a_w=conv_a_w, conv_b_w=conv_b_w, conv_b_b=conv_b_b, ln_b_g=ln_b_g,
                   ln_b_b=ln_b_b, ln_c_g=ln_c_g, ln_c_b=ln_c_b, sgu_w=sgu_w, sgu_b=sgu_b, w_out=w_out, norm2_g=norm2_g,
                   w_up=w_up, conv_f_w=conv_f_w, w_down=w_down, final_g=final_g)
    mom1 = dict(norm1_g=m_norm1_g, w_in=m_w_in, conv_a_w=m_conv_a_w, conv_b_w=m_conv_b_w, conv_b_b=m_conv_b_b,
                ln_b_g=m_ln_b_g, ln_b_b=m_ln_b_b, ln_c_g=m_ln_c_g, ln_c_b=m_ln_c_b, sgu_w=m_sgu_w, sgu_b=m_sgu_b,
                w_out=m_w_out, norm2_g=m_norm2_g, w_up=m_w_up, conv_f_w=m_conv_f_w, w_down=m_w_down, final_g=m_final_g)
    mom2 = dict(norm1_g=v_norm1_g, w_in=v_w_in, conv_a_w=v_conv_a_w, conv_b_w=v_conv_b_w, conv_b_b=v_conv_b_b,
                ln_b_g=v_ln_b_g, ln_b_b=v_ln_b_b, ln_c_g=v_ln_c_g, ln_c_b=v_ln_c_b, sgu_w=v_sgu_w, sgu_b=v_sgu_b,
                w_out=v_w_out, norm2_g=v_norm2_g, w_up=v_w_up, conv_f_w=v_conv_f_w, w_down=v_w_down, final_g=v_final_g)
    n_ex, seq, d = x.shape
    depth = w_in.shape[0]
    my = _flat(*_position())

    big_parts = [("w_in", jnp.swapaxes(w_in, 1, 2)), ("w_out", w_out), ("w_up", jnp.swapaxes(w_up, 1, 2)), ("w_down", w_down)]
    slab_rows = [(name, a.shape[1]) for name, a in big_parts]
    packed = jnp.concatenate([_bf(a[l]) for _, a in big_parts for l in range(depth)], axis=0)
    gathered = _all_gather(packed, "gather_weights")
    full, off = {}, 0
    for name, rows in slab_rows:
        full[name] = []
        for l in range(depth):
            full[name].append(gathered[:, off:off + rows, :].reshape(N_DEV * rows, d))
            off += rows
    conv_a_full, conv_b_full, conv_f_full = _gather_conv_weights(conv_a_w, conv_b_w, conv_f_w)

    row = lambda a, l: a[l][None]
    w = {
        "norm1_g": [row(norm1_g, l) for l in range(depth)], "w_in_t": full["w_in"],
        "conv_a_w": [conv_a_full[l] for l in range(depth)], "conv_b_w": [conv_b_full[l] for l in range(depth)],
        "conv_b_b": [row(conv_b_b, l) for l in range(depth)], "ln_b_g": [row(ln_b_g, l) for l in range(depth)],
        "ln_b_b": [row(ln_b_b, l) for l in range(depth)], "ln_c_g": [row(ln_c_g, l) for l in range(depth)],
        "ln_c_b": [row(ln_c_b, l) for l in range(depth)], "sgu_w": [sgu_w[l] for l in range(depth)],
        "sgu_b_full": [jnp.repeat(sgu_b[l].T, HEAD_DIM, axis=1) for l in range(depth)],
        "w_out": full["w_out"], "norm2_g": [row(norm2_g, l) for l in range(depth)], "w_up_t": full["w_up"],
        "conv_f_w": [conv_f_full[l] for l in range(depth)], "w_down": full["w_down"], "final_g": final_g[None],
    }

    loss, dx, g = _local_fwd_bwd(x.reshape(n_ex * seq, d), loss_target.reshape(n_ex * seq, d), w, n_ex)
    loss = lax.psum(loss[0, 0], ("x", "y", "c"))
    grad_x = dx.reshape(n_ex, seq, d)

    big_grads = [("w_in", g["w_in_t"]), ("w_out", g["w_out"]), ("w_up", g["w_up_t"]), ("w_down", g["w_down"])]
    send = jnp.concatenate([gl.reshape(N_DEV, gl.shape[0] // N_DEV, d) for _, gs in big_grads for gl in gs], axis=1)
    reduced = _sum_slabs(_all_to_all(send, "exchange_grads"), "sum_grads")
    grads, off = {}, 0
    for name, rows in slab_rows:
        per_layer = []
        for l in range(depth):
            per_layer.append(reduced[off:off + rows])
            off += rows
        stacked = jnp.stack(per_layer)
        grads[name] = jnp.swapaxes(stacked, 1, 2) if name in ("w_in", "w_up") else stacked

    small_local = {
        "norm1_g": jnp.stack([a[0] for a in g["norm1_g"]]), "conv_a_w": jnp.stack(g["conv_a_w"]),
        "conv_b_w": jnp.stack(g["conv_b_w"]), "conv_b_b": jnp.stack([a[0] for a in g["conv_b_b"]]),
        "ln_b_g": jnp.stack([a[0] for a in g["ln_b_g"]]), "ln_b_b": jnp.stack([a[0] for a in g["ln_b_b"]]),
        "ln_c_g": jnp.stack([a[0] for a in g["ln_c_g"]]), "ln_c_b": jnp.stack([a[0] for a in g["ln_c_b"]]),
        "sgu_w": jnp.stack(g["sgu_w"]), "sgu_b": jnp.stack([a.T for a in g["sgu_b_t"]]),
        "norm2_g": jnp.stack([a[0] for a in g["norm2_g"]]), "conv_f_w": jnp.stack(g["conv_f_w"]),
        "final_g": g["final_g"][0],
    }
    small_sum = _sum_slabs(_all_gather(_pack_rows([small_local[k] for k in _SMALL]), "gather_small_grads"),
                           "sum_small_grads")
    for name, total in zip(_SMALL, _unpack_rows(small_sum, [small_local[k].shape for k in _SMALL])):
        if name in _CONV_SHARDED:
            width = weights[name].shape[-1]
            total = lax.dynamic_slice_in_dim(total, my * width, width, axis=-1)
        grads[name] = total

    delta, new_m, new_v = {}, {}, {}
    for name in _NAMES:
        shp = weights[name].shape
        two_d = (-1, shp[-1]) if len(shp) > 1 else (1, shp[0])
        outs = _adamw(weights[name].reshape(two_d), grads[name].reshape(two_d), mom1[name].reshape(two_d),
                      mom2[name].reshape(two_d), "adamw_" + name)
        delta[name], new_m[name], new_v[name] = (o.reshape(shp) for o in outs)

    return (loss, grad_x, *[grads[n] for n in _NAMES], *[delta[n] for n in _NAMES], *[new_m[n] for n in _NAMES],
            *[new_v[n] for n in _NAMES])
```

```python
import functools

import jax
import jax.numpy as jnp
from jax import lax
from jax.experimental import pallas as pl
from jax.experimental.pallas import tpu as pltpu

_F32 = jnp.float32
_BF = jnp.bfloat16

HEAD_DIM = 64
MIX_W = 256
N_HEADS = MIX_W // HEAD_DIM
CHUNK = 128
KV_BLOCK = 128
K_SHORT = 3
K_CONF = 31
K_FFN = 3
RMS_EPS = 1e-6
LN_EPS = 1e-5
ADAM_LR = 0.001
ADAM_B1 = 0.9
ADAM_B2 = 0.999
ADAM_EPS = 1e-08
ADAM_WD = 0.01
ADAM_STEP = 10
N_DEV = 8
VMEM_LIMIT = 56 * 1024 * 1024


def _bf(x):
    return x.astype(_BF)


def _ld(ref):
    return ref[...].astype(_F32)


def _params(*sem):
    return pltpu.CompilerParams(dimension_semantics=sem, vmem_limit_bytes=VMEM_LIMIT)


def _dot(a, b):
    return jnp.dot(a, b, preferred_element_type=_F32)


def _dot_nt(a, b):
    return lax.dot_general(a, b, (((1,), (1,)), ((), ())), preferred_element_type=_F32)


def _dot_tn(a, b):
    return lax.dot_general(a, b, (((0,), (0,)), ((), ())), preferred_element_type=_F32)


def _row_tile(t, want):
    return want if t % want == 0 else t


def _pick_tile(rows, unit, max_rows):
    best = 0
    for cand in range(unit, min(rows, max_rows) + 1, unit):
        if rows % cand == 0:
            best = cand
    return best or rows


def _sigmoid(x):
    return 1.0 / (1.0 + jnp.exp(-x))


def _rms_rstd(x):
    return lax.rsqrt(jnp.mean(x * x, axis=-1, keepdims=True) + RMS_EPS)


def _norm_mm(x, g, w_t, name):
    t, d = x.shape
    n = w_t.shape[0]
    tm = _row_tile(t, 512)
    tn = _row_tile(n, 512)

    def body(x_ref, g_ref, w_ref, p_ref, h_ref):
        xv = x_ref[...]
        h = _bf(xv * _rms_rstd(xv) * g_ref[...])
        h_ref[...] = h
        for n0 in range(0, n, tn):
            p_ref[:, n0:n0 + tn] = _bf(_dot_nt(h, w_ref[n0:n0 + tn, :]))

    return pl.pallas_call(
        body, name=name, grid=(t // tm,),
        in_specs=[pl.BlockSpec((tm, d), lambda i: (i, 0)), pl.BlockSpec((1, d), lambda i: (0, 0)),
                  pl.BlockSpec((n, d), lambda i: (0, 0))],
        out_specs=[pl.BlockSpec((tm, n), lambda i: (i, 0)), pl.BlockSpec((tm, d), lambda i: (i, 0))],
        out_shape=[jax.ShapeDtypeStruct((t, n), _BF), jax.ShapeDtypeStruct((t, d), _BF)],
        compiler_params=_params("parallel"),
    )(x, g, w_t)


def _mm_nt(a, w_t, name):
    t, k = a.shape
    n = w_t.shape[0]
    tm = _row_tile(t, 512)
    tn = _row_tile(n, 512) if n % 512 == 0 else _row_tile(n, 256)

    def body(a_ref, w_ref, o_ref):
        av = a_ref[...]
        for n0 in range(0, n, tn):
            o_ref[:, n0:n0 + tn] = _bf(_dot_nt(av, w_ref[n0:n0 + tn, :]))

    return pl.pallas_call(
        body, name=name, grid=(t // tm,),
        in_specs=[pl.BlockSpec((tm, k), lambda i: (i, 0)), pl.BlockSpec((n, k), lambda i: (0, 0))],
        out_specs=pl.BlockSpec((tm, n), lambda i: (i, 0)),
        out_shape=jax.ShapeDtypeStruct((t, n), _BF),
        compiler_params=_params("parallel"),
    )(a, w_t)


def _mm_res(parts, w, x, name):
    t = x.shape[0]
    k, d = w.shape
    tm = _row_tile(t, 512)
    widths = [a.shape[1] for a in parts]
    n_parts = len(parts)

    def body(*refs):
        w_ref, x_ref, o_ref = refs[n_parts:]
        acc, off = x_ref[...], 0
        for a_ref, width in zip(refs[:n_parts], widths):
            acc = acc + _dot(a_ref[...], w_ref[off:off + width, :])
            off += width
        o_ref[...] = acc

    return pl.pallas_call(
        body, name=name, grid=(t // tm,),
        in_specs=[pl.BlockSpec((tm, width), lambda i: (i, 0)) for width in widths] + [
            pl.BlockSpec((k, d), lambda i: (0, 0)), pl.BlockSpec((tm, d), lambda i: (i, 0))],
        out_specs=pl.BlockSpec((tm, d), lambda i: (i, 0)),
        out_shape=jax.ShapeDtypeStruct((t, d), _F32),
        compiler_params=_params("parallel"),
    )(*parts, w, x)


def _mm_normbwd(parts, w, x, g, dres, name):
    t = x.shape[0]
    k, d = w.shape
    tm = _row_tile(t, 256)
    widths = [a.shape[1] for a in parts]
    n_parts = len(parts)

    def body(*refs):
        a_refs = refs[:n_parts]
        w_ref, x_ref, g_ref, r_ref, dx_ref, dxb_ref, dg_ref = refs[n_parts:]
        dh, off = None, 0
        for a_ref, width in zip(a_refs, widths):
            term = _dot(_bf(a_ref[...]), w_ref[off:off + width, :])
            dh = term if dh is None else dh + term
            off += width
        xv = x_ref[...]
        rstd = _rms_rstd(xv)
        xn = xv * rstd
        u = dh * g_ref[...]
        dx = r_ref[...] + rstd * (u - xn * jnp.mean(u * xn, axis=-1, keepdims=True))
        dx_ref[...] = dx
        dxb_ref[...] = _bf(dx)

        @pl.when(pl.program_id(0) == 0)
        def _():
            dg_ref[...] = jnp.zeros_like(dg_ref)

        dg_ref[...] += jnp.sum(dh * xn, axis=0, keepdims=True)

    return pl.pallas_call(
        body, name=name, grid=(t // tm,),
        in_specs=[pl.BlockSpec((tm, width), lambda i: (i, 0)) for width in widths] + [
            pl.BlockSpec((k, d), lambda i: (0, 0)),
            pl.BlockSpec((tm, d), lambda i: (i, 0)), pl.BlockSpec((1, d), lambda i: (0, 0)),
            pl.BlockSpec((tm, d), lambda i: (i, 0))],
        out_specs=[pl.BlockSpec((tm, d), lambda i: (i, 0)), pl.BlockSpec((tm, d), lambda i: (i, 0)),
                   pl.BlockSpec((1, d), lambda i: (0, 0))],
        out_shape=[jax.ShapeDtypeStruct((t, d), _F32), jax.ShapeDtypeStruct((t, d), _BF),
                   jax.ShapeDtypeStruct((1, d), _F32)],
        compiler_params=_params("arbitrary"),
    )(*parts, w, x, g, dres)


def _mm_tn(a, b, name, out_dtype):
    t, m = a.shape
    n = b.shape[1]
    tm = _pick_tile(m, 128, 1408)
    tn = _pick_tile(n, 128, 1024)
    tk = _row_tile(t, 1024)
    nk = t // tk

    def body(a_ref, b_ref, o_ref, acc):
        kk = pl.program_id(2)

        @pl.when(kk == 0)
        def _():
            acc[...] = jnp.zeros_like(acc)

        acc[...] += _dot_tn(_bf(a_ref[...]), b_ref[...])

        @pl.when(kk == nk - 1)
        def _():
            o_ref[...] = acc[...].astype(o_ref.dtype)

    return pl.pallas_call(
        body, name=name, grid=(m // tm, n // tn, nk),
        in_specs=[pl.BlockSpec((tk, tm), lambda i, j, kk: (kk, i)), pl.BlockSpec((tk, tn), lambda i, j, kk: (kk, j))],
        out_specs=pl.BlockSpec((tm, tn), lambda i, j, kk: (i, j)),
        out_shape=jax.ShapeDtypeStruct((m, n), out_dtype),
        scratch_shapes=[pltpu.VMEM((tm, tn), _F32)],
        compiler_params=_params("parallel", "parallel", "arbitrary"),
    )(a, b)


def _final_loss(x, g, target, name):
    t, d = x.shape
    tm = _row_tile(t, 256)

    def body(x_ref, g_ref, t_ref, dx_ref, dxb_ref, dg_ref, loss_ref):
        xv = x_ref[...]
        rstd = _rms_rstd(xv)
        xn = xv * rstd
        err = xn * g_ref[...] - t_ref[...]
        dy = err * (1.0 / d)
        u = dy * g_ref[...]
        dx = rstd * (u - xn * jnp.mean(u * xn, axis=-1, keepdims=True))
        dx_ref[...] = dx
        dxb_ref[...] = _bf(dx)

        @pl.when(pl.program_id(0) == 0)
        def _():
            dg_ref[...] = jnp.zeros_like(dg_ref)
            loss_ref[...] = jnp.zeros_like(loss_ref)

        dg_ref[...] += jnp.sum(dy * xn, axis=0, keepdims=True)
        loss_ref[...] += (0.5 / d) * jnp.sum(jnp.sum(err * err, axis=1, keepdims=True), axis=0, keepdims=True)

    return pl.pallas_call(
        body, name=name, grid=(t // tm,),
        in_specs=[pl.BlockSpec((tm, d), lambda i: (i, 0)), pl.BlockSpec((1, d), lambda i: (0, 0)),
                  pl.BlockSpec((tm, d), lambda i: (i, 0))],
        out_specs=[pl.BlockSpec((tm, d), lambda i: (i, 0)), pl.BlockSpec((tm, d), lambda i: (i, 0)),
                   pl.BlockSpec((1, d), lambda i: (0, 0)), pl.BlockSpec((1, 1), lambda i: (0, 0))],
        out_shape=[jax.ShapeDtypeStruct((t, d), _F32), jax.ShapeDtypeStruct((t, d), _BF),
                   jax.ShapeDtypeStruct((1, d), _F32), jax.ShapeDtypeStruct((1, 1), _F32)],
        compiler_params=_params("arbitrary"),
    )(x, g, target)


def _pad_rows(x, pad):
    return jnp.concatenate([x, jnp.zeros((pad, x.shape[1]), x.dtype)], axis=0)


def _shift_down(xp, s):
    return xp if s == 0 else pltpu.roll(xp, s, 0)


def _shift_up(xp, s):
    return xp if s == 0 else pltpu.roll(xp, xp.shape[0] - s, 0)


def _taps3(xp):
    one = _shift_down(xp, 1)
    return xp, one, _shift_down(one, 1)


def _conv3_taps(taps, w_ref):
    return w_ref[2:3, :] * taps[0] + w_ref[1:2, :] * taps[1] + w_ref[0:1, :] * taps[2]


def _conv3(xp, w_ref):
    return _conv3_taps(_taps3(xp), w_ref)


def _conv3_t(dyp, w_ref):
    one = _shift_up(dyp, 1)
    return w_ref[2:3, :] * dyp + w_ref[1:2, :] * one + w_ref[0:1, :] * _shift_up(one, 1)


def _conv3_dw(dyp, taps):
    return [jnp.sum(dyp * taps[2 - k], axis=0, keepdims=True) for k in range(3)]


def _ffn_mid_fwd(up_pre, wf, n_ex, name):
    t, f2 = up_pre.shape
    f = f2 // 2
    s = t // n_ex
    cb = MIX_W
    nb = f // cb

    def body(ug_ref, uv_ref, wg_ref, wv_ref, act_ref):
        gf = _conv3(_pad_rows(ug_ref[...].astype(_F32), 8), wg_ref)[:s]
        vf = _conv3(_pad_rows(uv_ref[...].astype(_F32), 8), wv_ref)[:s]
        act_ref[...] = _bf(gf * _sigmoid(gf) * vf)

    return pl.pallas_call(
        body, name=name, grid=(n_ex, nb),
        in_specs=[pl.BlockSpec((s, cb), lambda e, j: (e, j)), pl.BlockSpec((s, cb), lambda e, j: (e, j + nb)),
                  pl.BlockSpec((K_FFN, cb), lambda e, j: (0, j)), pl.BlockSpec((K_FFN, cb), lambda e, j: (0, j + nb))],
        out_specs=pl.BlockSpec((s, cb), lambda e, j: (e, j)),
        out_shape=jax.ShapeDtypeStruct((t, f), _BF),
        compiler_params=_params("parallel", "parallel"),
    )(up_pre, up_pre, wf, wf)


def _ffn_mid_bwd(up_pre, wf, dact, n_ex, name):
    t, f2 = up_pre.shape
    f = f2 // 2
    s = t // n_ex
    cb = MIX_W
    nb = f // cb

    def body(ug_ref, uv_ref, wg_ref, wv_ref, da_ref, dug_ref, duv_ref, dwg_ref, dwv_ref):
        g_taps = _taps3(_pad_rows(ug_ref[...].astype(_F32), 8))
        v_taps = _taps3(_pad_rows(uv_ref[...].astype(_F32), 8))
        gf = _conv3_taps(g_taps, wg_ref)[:s]
        vf = _conv3_taps(v_taps, wv_ref)[:s]
        sg = _sigmoid(gf)
        da = da_ref[...].astype(_F32)

        @pl.when(pl.program_id(1) == 0)
        def _():
            dwg_ref[...] = jnp.zeros_like(dwg_ref)
            dwv_ref[...] = jnp.zeros_like(dwv_ref)

        def finish(dup, w_ref, taps, du_ref, dw_ref):
            dupp = _pad_rows(dup, 8)
            du_ref[...] = _bf(_conv3_t(dupp, w_ref)[:s])
            rows = _conv3_dw(dupp, taps)
            for k in range(3):
                dw_ref[k:k + 1, :] += rows[k]

        finish(da * vf * sg * (1.0 + gf * (1.0 - sg)), wg_ref, g_taps, dug_ref, dwg_ref)
        finish(da * gf * sg, wv_ref, v_taps, duv_ref, dwv_ref)

    return pl.pallas_call(
        body, name=name, grid=(nb, n_ex),
        in_specs=[pl.BlockSpec((s, cb), lambda j, e: (e, j)), pl.BlockSpec((s, cb), lambda j, e: (e, j + nb)),
                  pl.BlockSpec((K_FFN, cb), lambda j, e: (0, j)), pl.BlockSpec((K_FFN, cb), lambda j, e: (0, j + nb)),
                  pl.BlockSpec((s, cb), lambda j, e: (e, j))],
        out_specs=[pl.BlockSpec((s, cb), lambda j, e: (e, j)), pl.BlockSpec((s, cb), lambda j, e: (e, j)),
                   pl.BlockSpec((K_FFN, cb), lambda j, e: (0, j)), pl.BlockSpec((K_FFN, cb), lambda j, e: (0, j))],
        out_shape=[jax.ShapeDtypeStruct((t, f), _BF), jax.ShapeDtypeStruct((t, f), _BF),
                   jax.ShapeDtypeStruct((K_FFN, f), _F32), jax.ShapeDtypeStruct((K_FFN, f), _F32)],
        compiler_params=_params("parallel", "arbitrary"),
    )(up_pre, up_pre, wf, wf, dact)


def _pcol(s, j):
    return pl.BlockSpec((s, MIX_W), lambda e, j=j: (e, j))


def _vec(rows=1):
    return pl.BlockSpec((rows, MIX_W), lambda e: (0, 0))


def _mix_a_fwd(p, wa, n_ex, name):
    t = p.shape[0]
    s = t // n_ex

    def body(gb_ref, gc_ref, ha_ref, w_ref, y_ref):
        cv = _conv3(_pad_rows(_ld(gc_ref) * _ld(ha_ref), 8), w_ref)[:s]
        y_ref[...] = _bf(_ld(gb_ref) * cv)

    return pl.pallas_call(
        body, name=name, grid=(n_ex,),
        in_specs=[_pcol(s, 0), _pcol(s, 1), _pcol(s, 2), _vec(K_SHORT)],
        out_specs=pl.BlockSpec((s, MIX_W), lambda e: (e, 0)),
        out_shape=jax.ShapeDtypeStruct((t, MIX_W), _BF),
        compiler_params=_params("parallel"),
    )(p, p, p, wa)


def _mix_a_bwd(p, wa, dmix, n_ex, name):
    t = p.shape[0]
    s = t // n_ex

    def body(gb_ref, gc_ref, ha_ref, w_ref, dy_ref, dp_ref, dw_ref):
        gc = _ld(gc_ref)
        ha = _ld(ha_ref)
        up = _taps3(_pad_rows(gc * ha, 8))
        cv = _conv3_taps(up, w_ref)[:s]
        dy = _ld(dy_ref)
        dcvp = _pad_rows(dy * _ld(gb_ref), 8)
        du = _conv3_t(dcvp, w_ref)[:s]
        dp_ref[:, 0:MIX_W] = _bf(dy * cv)
        dp_ref[:, MIX_W:2 * MIX_W] = _bf(du * ha)
        dp_ref[:, 2 * MIX_W:3 * MIX_W] = _bf(du * gc)

        @pl.when(pl.program_id(0) == 0)
        def _():
            dw_ref[...] = jnp.zeros_like(dw_ref)

        rows = _conv3_dw(dcvp, up)
        for k in range(3):
            dw_ref[k:k + 1, :] += rows[k]

    return pl.pallas_call(
        body, name=name, grid=(n_ex,),
        in_specs=[_pcol(s, 0), _pcol(s, 1), _pcol(s, 2), _vec(K_SHORT), _pcol(s, 0)],
        out_specs=[pl.BlockSpec((s, 3 * MIX_W), lambda e: (e, 0)), _vec(K_SHORT)],
        out_shape=[jax.ShapeDtypeStruct((t, 3 * MIX_W), _BF), jax.ShapeDtypeStruct((K_SHORT, MIX_W), _F32)],
        compiler_params=_params("arbitrary"),
    )(p, p, p, wa, dmix)


CONF_PAD = 32


def _ln_fwd(x, g, b):
    mu = jnp.mean(x, axis=-1, keepdims=True)
    xc = x - mu
    rstd = lax.rsqrt(jnp.mean(xc * xc, axis=-1, keepdims=True) + LN_EPS)
    xhat = xc * rstd
    return xhat * g + b, xhat, rstd


def _ln_bwd(dy, xhat, rstd, g):
    dxh = dy * g
    return rstd * (dxh - jnp.mean(dxh, axis=-1, keepdims=True) - xhat * jnp.mean(dxh * xhat, axis=-1, keepdims=True))


def _mix_b_fwd(p, wb, bb, lg, lb, n_ex, name):
    t = p.shape[0]
    s = t // n_ex

    def body(val_ref, gat_ref, w_ref, bb_ref, lg_ref, lb_ref, y_ref, cb_ref):
        cur = _pad_rows(_ld(val_ref) * _sigmoid(_ld(gat_ref)), CONF_PAD)
        acc = w_ref[K_CONF - 1:K_CONF, :] * cur
        for sh in range(1, K_CONF):
            cur = pltpu.roll(cur, 1, 0)
            acc = acc + w_ref[K_CONF - 1 - sh:K_CONF - sh, :] * cur
        cb = acc[:s] + bb_ref[...]
        cb_ref[...] = cb
        yl, _, _ = _ln_fwd(cb, lg_ref[...], lb_ref[...])
        y_ref[...] = _bf(yl * _sigmoid(yl))

    return pl.pallas_call(
        body, name=name, grid=(n_ex,),
        in_specs=[_pcol(s, 3), _pcol(s, 4), _vec(K_CONF), _vec(), _vec(), _vec()],
        out_specs=[pl.BlockSpec((s, MIX_W), lambda e: (e, 0)), pl.BlockSpec((s, MIX_W), lambda e: (e, 0))],
        out_shape=[jax.ShapeDtypeStruct((t, MIX_W), _BF), jax.ShapeDtypeStruct((t, MIX_W), _F32)],
        compiler_params=_params("parallel"),
    )(p, p, wb, bb, lg, lb)


def _mix_b_bwd(p, cb, wb, lg, lb, dmix, n_ex, name):
    t = p.shape[0]
    s = t // n_ex

    def body(val_ref, gat_ref, cb_ref, w_ref, lg_ref, lb_ref, dy_ref, dp_ref, dw_ref, dbb_ref, dlg_ref, dlb_ref):
        @pl.when(pl.program_id(0) == 0)
        def _():
            for r in (dw_ref, dbb_ref, dlg_ref, dlb_ref):
                r[...] = jnp.zeros_like(r)

        yl, xhat, rstd = _ln_fwd(cb_ref[...], lg_ref[...], lb_ref[...])
        sy = _sigmoid(yl)
        dyl = _ld(dy_ref) * sy * (1.0 + yl * (1.0 - sy))
        dlg_ref[...] += jnp.sum(dyl * xhat, axis=0, keepdims=True)
        dlb_ref[...] += jnp.sum(dyl, axis=0, keepdims=True)
        dcb = _ln_bwd(dyl, xhat, rstd, lg_ref[...])
        dbb_ref[...] += jnp.sum(dcb, axis=0, keepdims=True)

        val = _ld(val_ref)
        sg = _sigmoid(_ld(gat_ref))
        dcbp = _pad_rows(dcb, CONF_PAD)
        cur = _pad_rows(val * sg, CONF_PAD)
        up = dcbp
        dglu = w_ref[K_CONF - 1:K_CONF, :] * up
        dw_ref[K_CONF - 1:K_CONF, :] += jnp.sum(dcbp * cur, axis=0, keepdims=True)
        n_pad = s + CONF_PAD
        for sh in range(1, K_CONF):
            cur = pltpu.roll(cur, 1, 0)
            up = pltpu.roll(up, n_pad - 1, 0)
            k = K_CONF - 1 - sh
            dglu = dglu + w_ref[k:k + 1, :] * up
            dw_ref[k:k + 1, :] += jnp.sum(dcbp * cur, axis=0, keepdims=True)
        dglu = dglu[:s]
        dp_ref[:, 0:MIX_W] = _bf(dglu * sg)
        dp_ref[:, MIX_W:2 * MIX_W] = _bf(dglu * val * sg * (1.0 - sg))

    return pl.pallas_call(
        body, name=name, grid=(n_ex,),
        in_specs=[_pcol(s, 3), _pcol(s, 4), pl.BlockSpec((s, MIX_W), lambda e: (e, 0)), _vec(K_CONF), _vec(), _vec(),
                  _pcol(s, 1)],
        out_specs=[pl.BlockSpec((s, 2 * MIX_W), lambda e: (e, 0)), _vec(K_CONF), _vec(), _vec(), _vec()],
        out_shape=[jax.ShapeDtypeStruct((t, 2 * MIX_W), _BF), jax.ShapeDtypeStruct((K_CONF, MIX_W), _F32),
                   jax.ShapeDtypeStruct((1, MIX_W), _F32), jax.ShapeDtypeStruct((1, MIX_W), _F32),
                   jax.ShapeDtypeStruct((1, MIX_W), _F32)],
        compiler_params=_params("arbitrary"),
    )(p, p, cb, wb, lg, lb, dmix)


_INV_SQRT2 = 0.7071067811865476
_INV_SQRT2PI = 0.3989422804014327


def _gelu(x):
    return 0.5 * x * (1.0 + lax.erf(x * _INV_SQRT2))


def _gelu_grad(x):
    return 0.5 * (1.0 + lax.erf(x * _INV_SQRT2)) + x * _INV_SQRT2PI * jnp.exp(-0.5 * x * x)


def _head_masks(width=MIX_W):
    lane = lax.broadcasted_iota(jnp.int32, (1, width), 1)
    return [(lane >= h * HEAD_DIM) & (lane < (h + 1) * HEAD_DIM) for h in range(N_HEADS)]


def _tril_mask():
    r = lax.broadcasted_iota(jnp.int32, (CHUNK, CHUNK), 0)
    c = lax.broadcasted_iota(jnp.int32, (CHUNK, CHUNK), 1)
    return c <= r


def _sgu_apply(ws_ref, x3, transpose):
    n = x3.shape[0]
    tril = _tril_mask()
    masks = _head_masks()
    xb = _bf(x3)
    out = jnp.zeros(x3.shape, _F32)
    for h in range(N_HEADS):
        w = _bf(jnp.where(tril, ws_ref[h], 0.0))
        wb = jnp.broadcast_to(w[None], (n, CHUNK, CHUNK))
        dims = (((1,), (1,)), ((0,), (0,))) if transpose else (((2,), (1,)), ((0,), (0,)))
        r = lax.dot_general(wb, xb, dims, preferred_element_type=_F32)
        out = out + jnp.where(masks[h][None], r, 0.0)
    return out


def _mix_c_fwd(p, lg, lb, ws, sb_full, n_ex, name):
    t = p.shape[0]
    s = t // n_ex
    nc = s // CHUNK

    def body(pu_ref, pv_ref, lg_ref, lb_ref, ws_ref, sb_ref, y_ref):
        u = _gelu(_ld(pu_ref))
        vl, _, _ = _ln_fwd(_gelu(_ld(pv_ref)), lg_ref[...], lb_ref[...])
        sp = _sgu_apply(ws_ref, vl.reshape(nc, CHUNK, MIX_W), False) + sb_ref[...][None]
        y_ref[...] = _bf(u * sp.reshape(s, MIX_W))

    return pl.pallas_call(
        body, name=name, grid=(n_ex,),
        in_specs=[_pcol(s, 5), _pcol(s, 6), _vec(), _vec(),
                  pl.BlockSpec((N_HEADS, CHUNK, CHUNK), lambda e: (0, 0, 0)), pl.BlockSpec((CHUNK, MIX_W), lambda e: (0, 0))],
        out_specs=pl.BlockSpec((s, MIX_W), lambda e: (e, 0)),
        out_shape=jax.ShapeDtypeStruct((t, MIX_W), _BF),
        compiler_params=_params("parallel"),
    )(p, p, lg, lb, ws, sb_full)


def _mix_c_bwd(p, lg, lb, ws, sb_full, dmix, n_ex, name):
    t = p.shape[0]
    s = t // n_ex
    nc = s // CHUNK

    def body(pu_ref, pv_ref, lg_ref, lb_ref, ws_ref, sb_ref, dy_ref, dp_ref, dlg_ref, dlb_ref, dws_ref, dsb_ref):
        @pl.when(pl.program_id(0) == 0)
        def _():
            for r in (dlg_ref, dlb_ref, dws_ref, dsb_ref):
                r[...] = jnp.zeros_like(r)

        pu = _ld(pu_ref)
        pv = _ld(pv_ref)
        u = _gelu(pu)
        vl, xhat, rstd = _ln_fwd(_gelu(pv), lg_ref[...], lb_ref[...])
        vl3 = vl.reshape(nc, CHUNK, MIX_W)
        sp = _sgu_apply(ws_ref, vl3, False) + sb_ref[...][None]
        dy = _ld(dy_ref)
        dp_ref[:, 0:MIX_W] = _bf(dy * sp.reshape(s, MIX_W) * _gelu_grad(pu))
        dsp3 = (dy * u).reshape(nc, CHUNK, MIX_W)
        dsb_full = jnp.sum(dsp3, axis=0)
        masks = _head_masks()
        tril = _tril_mask()
        dspb = _bf(dsp3)
        vlb = _bf(vl3)
        for h in range(N_HEADS):
            dsb_ref[:, h:h + 1] += jnp.sum(jnp.where(masks[h], dsb_full, 0.0), axis=1, keepdims=True)
            dm = jnp.where(masks[h][None], dspb, jnp.zeros_like(dspb))
            g3 = lax.dot_general(dm, vlb, (((2,), (2,)), ((0,), (0,))), preferred_element_type=_F32)
            dws_ref[h] += jnp.where(tril, jnp.sum(g3, axis=0), 0.0)
        dvl = _sgu_apply(ws_ref, dsp3, True).reshape(s, MIX_W)
        dlg_ref[...] += jnp.sum(dvl * xhat, axis=0, keepdims=True)
        dlb_ref[...] += jnp.sum(dvl, axis=0, keepdims=True)
        dp_ref[:, MIX_W:2 * MIX_W] = _bf(_ln_bwd(dvl, xhat, rstd, lg_ref[...]) * _gelu_grad(pv))

    return pl.pallas_call(
        body, name=name, grid=(n_ex,),
        in_specs=[_pcol(s, 5), _pcol(s, 6), _vec(), _vec(),
                  pl.BlockSpec((N_HEADS, CHUNK, CHUNK), lambda e: (0, 0, 0)), pl.BlockSpec((CHUNK, MIX_W), lambda e: (0, 0)),
                  _pcol(s, 2)],
        out_specs=[pl.BlockSpec((s, 2 * MIX_W), lambda e: (e, 0)), _vec(), _vec(),
                   pl.BlockSpec((N_HEADS, CHUNK, CHUNK), lambda e: (0, 0, 0)), pl.BlockSpec((CHUNK, N_HEADS), lambda e: (0, 0))],
        out_shape=[jax.ShapeDtypeStruct((t, 2 * MIX_W), _BF), jax.ShapeDtypeStruct((1, MIX_W), _F32),
                   jax.ShapeDtypeStruct((1, MIX_W), _F32), jax.ShapeDtypeStruct((N_HEADS, CHUNK, CHUNK), _F32),
                   jax.ShapeDtypeStruct((CHUNK, N_HEADS), _F32)],
        compiler_params=_params("arbitrary"),
    )(p, p, lg, lb, ws, sb_full, dmix)


D_QBLOCK = 256
HEAD_COLS = N_HEADS * KV_BLOCK


def _stack_heads(x3):
    return jnp.stack([_bf(jnp.where(m[None], x3, 0.0)) for m in _head_masks()], axis=1)


def _stack_heads_rows(x):
    return jnp.concatenate([_bf(jnp.where(m, x, 0.0)) for m in _head_masks()], axis=0)


def _cols_to_rows(x):
    return jnp.concatenate([x[:, h * KV_BLOCK:(h + 1) * KV_BLOCK] for h in range(N_HEADS)], axis=0)


def _head_sums(x):
    return [jnp.sum(x[:, h * KV_BLOCK:(h + 1) * KV_BLOCK], axis=1, keepdims=True) for h in range(N_HEADS)]


def _spread(cols):
    tq = cols[0].shape[0]
    return jnp.concatenate([jnp.broadcast_to(c, (tq, KV_BLOCK)) for c in cols], axis=1)


def _pair_dot(x, m2):
    half = 2 * KV_BLOCK
    xb = _bf(x)
    return jnp.concatenate([_dot(xb[:, :half], m2), _dot(xb[:, half:], m2)], axis=1)


def _tri2(lower):
    n = 2 * KV_BLOCK
    r = lax.broadcasted_iota(jnp.int32, (n, n), 0)
    c = lax.broadcasted_iota(jnp.int32, (n, n), 1)
    same = (r >= KV_BLOCK) == (c >= KV_BLOCK)
    return _bf(jnp.where(same & (r > c if lower else r < c), 1.0, 0.0))


def _sb_scores(qs, kc, j, t_idx):
    z = _dot_nt(qs, kc)
    lane = lax.broadcasted_iota(jnp.int32, (1, HEAD_COLS), 1)
    valid = (j * KV_BLOCK + (lane & (KV_BLOCK - 1))) < t_idx
    lb = jnp.minimum(z, 0.0) - jnp.log(1.0 + jnp.exp(-jnp.abs(z)))
    c = jnp.where(valid, lb - z, 0.0)
    return valid, lb, c


RUN_LANES = 128


def _run_lane(j, h):
    return lax.broadcasted_iota(jnp.int32, (1, RUN_LANES), 1) == j * N_HEADS + h


def _d_qblock(s):
    return D_QBLOCK if s % D_QBLOCK == 0 else KV_BLOCK


def _mix_d_fwd(p, n_ex, name):
    t = p.shape[0]
    s = t // n_ex
    tq = _d_qblock(s)
    nq = s // tq
    r = tq // KV_BLOCK
    nb = s // KV_BLOCK
    assert nb * N_HEADS <= RUN_LANES

    def body(q_ref, k_ref, v_ref, y_ref, runs_ref, kc, vc):
        i = pl.program_id(1)

        @pl.when(i == 0)
        def _():
            kc[...] = _stack_heads(k_ref[...].reshape(nb, KV_BLOCK, MIX_W))
            vc[...] = _stack_heads(v_ref[...].reshape(nb, KV_BLOCK, MIX_W))

        qs = _bf(_ld(q_ref) * (HEAD_DIM ** -0.5))
        t_idx = i * tq + lax.broadcasted_iota(jnp.int32, (tq, 1), 0)
        after_m = _tri2(True)
        nkb = (i + 1) * r

        runs_ref[...] = jnp.zeros_like(runs_ref)

        def step(jj, carry):
            runs, acc = carry
            j = nkb - 1 - jj
            valid, lb, c = _sb_scores(qs, kc[j].reshape(HEAD_COLS, MIX_W), j, t_idx)
            a = jnp.where(valid, jnp.exp(lb + _pair_dot(c, after_m) + _spread(runs)), 0.0)
            acc = acc + _dot(_bf(a), vc[j].reshape(HEAD_COLS, MIX_W))
            kept = runs_ref[...]
            for h in range(N_HEADS):
                kept = jnp.where(_run_lane(j, h), runs[h], kept)
            runs_ref[...] = kept
            return tuple(ru + cs for ru, cs in zip(runs, _head_sums(c))), acc

        zero = jnp.zeros((tq, 1), _F32)
        _, acc = lax.fori_loop(0, nkb, step, ((zero,) * N_HEADS, jnp.zeros((tq, MIX_W), _F32)))
        y_ref[...] = _bf(acc)

    return pl.pallas_call(
        body, name=name, grid=(n_ex, nq),
        in_specs=[pl.BlockSpec((tq, MIX_W), lambda e, i: (e * nq + i, 7)), pl.BlockSpec((s, MIX_W), lambda e, i: (e, 8)),
                  pl.BlockSpec((s, MIX_W), lambda e, i: (e, 9))],
        out_specs=[pl.BlockSpec((tq, MIX_W), lambda e, i: (e * nq + i, 0)),
                   pl.BlockSpec((tq, RUN_LANES), lambda e, i: (e * nq + i, 0))],
        out_shape=[jax.ShapeDtypeStruct((t, MIX_W), _BF), jax.ShapeDtypeStruct((t, RUN_LANES), _F32)],
        scratch_shapes=[pltpu.VMEM((nb, N_HEADS, KV_BLOCK, MIX_W), _BF), pltpu.VMEM((nb, N_HEADS, KV_BLOCK, MIX_W), _BF)],
        compiler_params=_params("parallel", "arbitrary"),
    )(p, p, p)


def _mix_d_bwd(p, kept_runs, dmix, n_ex, name):
    t = p.shape[0]
    s = t // n_ex
    tq = _d_qblock(s)
    nq = s // tq
    r = tq // KV_BLOCK
    nb = s // KV_BLOCK
    scale = HEAD_DIM ** -0.5

    def body(q_ref, k_ref, v_ref, runs_ref, do_ref, dq_ref, dk_ref, dv_ref, kc, vc):
        i = pl.program_id(1)

        @pl.when(i == 0)
        def _():
            kc[...] = _stack_heads(k_ref[...].reshape(nb, KV_BLOCK, MIX_W))
            vc[...] = _stack_heads(v_ref[...].reshape(nb, KV_BLOCK, MIX_W))
            dk_ref[...] = jnp.zeros_like(dk_ref)
            dv_ref[...] = jnp.zeros_like(dv_ref)

        q_scaled = _ld(q_ref) * scale
        qs = _bf(q_scaled)
        do = do_ref[...]
        dob = _bf(do)
        q_rows = _stack_heads_rows(q_scaled)
        do_rows = _stack_heads_rows(do)
        kept = runs_ref[...]
        t_idx = i * tq + lax.broadcasted_iota(jnp.int32, (tq, 1), 0)
        after_m = _tri2(True)
        before_m = _tri2(False)
        nkb = (i + 1) * r
        zero = jnp.zeros((tq, 1), _F32)

        def step(j, carry):
            pres, dq = carry
            rows = pl.ds(pl.multiple_of(j * KV_BLOCK, KV_BLOCK), KV_BLOCK)
            kj = kc[j].reshape(HEAD_COLS, MIX_W)
            valid, lb, c = _sb_scores(qs, kj, j, t_idx)
            runs = [jnp.sum(jnp.where(_run_lane(j, h), kept, 0.0), axis=1, keepdims=True) for h in range(N_HEADS)]
            a = jnp.where(valid, jnp.exp(lb + _pair_dot(c, after_m) + _spread(runs)), 0.0)
            g = a * _dot_nt(dob, vc[j].reshape(HEAD_COLS, MIX_W))
            before = _pair_dot(g, before_m) + _spread(pres)
            sig = jnp.exp(lb)
            dz = _bf(jnp.where(valid, g * (1.0 - sig) - sig * before, 0.0))
            dk_ref[rows, :] += _dot_tn(_cols_to_rows(dz), q_rows)
            dv_ref[rows, :] += _dot_tn(_cols_to_rows(_bf(a)), do_rows)
            return tuple(pr + gs for pr, gs in zip(pres, _head_sums(g))), dq + _dot(dz, kj)

        _, dq = lax.fori_loop(0, nkb, step, ((zero,) * N_HEADS, jnp.zeros((tq, MIX_W), _F32)))
        dq_ref[...] = _bf(dq * scale)

    return pl.pallas_call(
        body, name=name, grid=(n_ex, nq),
        in_specs=[pl.BlockSpec((tq, MIX_W), lambda e, i: (e * nq + i, 7)), pl.BlockSpec((s, MIX_W), lambda e, i: (e, 8)),
                  pl.BlockSpec((s, MIX_W), lambda e, i: (e, 9)), pl.BlockSpec((tq, RUN_LANES), lambda e, i: (e * nq + i, 0)),
                  pl.BlockSpec((tq, MIX_W), lambda e, i: (e * nq + i, 3))],
        out_specs=[pl.BlockSpec((tq, MIX_W), lambda e, i: (e * nq + i, 0)), pl.BlockSpec((s, MIX_W), lambda e, i: (e, 0)),
                   pl.BlockSpec((s, MIX_W), lambda e, i: (e, 0))],
        out_shape=[jax.ShapeDtypeStruct((t, MIX_W), _BF), jax.ShapeDtypeStruct((t, MIX_W), _F32),
                   jax.ShapeDtypeStruct((t, MIX_W), _F32)],
        scratch_shapes=[pltpu.VMEM((nb, N_HEADS, KV_BLOCK, MIX_W), _BF), pltpu.VMEM((nb, N_HEADS, KV_BLOCK, MIX_W), _BF)],
        compiler_params=_params("parallel", "arbitrary"),
    )(p, p, p, kept_runs, dmix)


def _fwd_mix(x, w, l, n_ex):
    p, h1 = _norm_mm(x, w["norm1_g"][l], w["w_in_t"][l], "in_proj")
    y_a = _mix_a_fwd(p, w["conv_a_w"][l], n_ex, "mix_a_fwd")
    y_b, cb = _mix_b_fwd(p, w["conv_b_w"][l], w["conv_b_b"][l], w["ln_b_g"][l], w["ln_b_b"][l], n_ex, "mix_b_fwd")
    y_c = _mix_c_fwd(p, w["ln_c_g"][l], w["ln_c_b"][l], w["sgu_w"][l], w["sgu_b_full"][l], n_ex, "mix_c_fwd")
    y_d, runs_d = _mix_d_fwd(p, n_ex, "mix_d_fwd")
    return dict(x=x, h1=h1, p=p, cb=cb, runs_d=runs_d, mix=(y_a, y_b, y_c, y_d))


def _fwd_ffn(st, w, l, n_ex):
    x1 = _mm_res(st["mix"], w["w_out"][l], st["x"], "out_proj")
    up_pre, h2 = _norm_mm(x1, w["norm2_g"][l], w["w_up_t"][l], "up_proj")
    act = _ffn_mid_fwd(up_pre, w["conv_f_w"][l], n_ex, "ffn_mid_fwd")
    st.update(x1=x1, h2=h2, up_pre=up_pre, act=act)
    return _mm_res((act,), w["w_down"][l], x1, "down_proj")


def _bwd_ffn(st, w, l, dx, dxb, n_ex):
    g = {}
    dact = _mm_nt(dxb, w["w_down"][l], "down_proj_dx")
    g["w_down"] = _mm_tn(st["act"], dxb, "down_proj_dw", _BF)
    dup_g, dup_v, dwf_g, dwf_v = _ffn_mid_bwd(st["up_pre"], w["conv_f_w"][l], dact, n_ex, "ffn_mid_bwd")
    g["conv_f_w"] = jnp.concatenate([dwf_g, dwf_v], axis=1)
    dx, dxb, g["norm2_g"] = _mm_normbwd((dup_g, dup_v), w["w_up_t"][l], st["x1"], w["norm2_g"][l], dx, "up_proj_dx")
    g["w_up_t"] = jnp.concatenate([_mm_tn(part, st["h2"], "up_proj_dw", _BF) for part in (dup_g, dup_v)], axis=0)
    return dx, dxb, g


def _bwd_mix(st, w, l, dx, dxb, n_ex):
    g = {}
    p = st["p"]
    dmix = _mm_nt(dxb, w["w_out"][l], "out_proj_dx")
    g["w_out"] = jnp.concatenate([_mm_tn(part, dxb, "out_proj_dw", _BF) for part in st["mix"]], axis=0)
    dp_a, g["conv_a_w"] = _mix_a_bwd(p, w["conv_a_w"][l], dmix, n_ex, "mix_a_bwd")
    dp_b, g["conv_b_w"], g["conv_b_b"], g["ln_b_g"], g["ln_b_b"] = _mix_b_bwd(
        p, st["cb"], w["conv_b_w"][l], w["ln_b_g"][l], w["ln_b_b"][l], dmix, n_ex, "mix_b_bwd")
    dp_c, g["ln_c_g"], g["ln_c_b"], g["sgu_w"], g["sgu_b_t"] = _mix_c_bwd(
        p, w["ln_c_g"][l], w["ln_c_b"][l], w["sgu_w"][l], w["sgu_b_full"][l], dmix, n_ex, "mix_c_bwd")
    dq, dk, dv = _mix_d_bwd(p, st["runs_d"], dmix, n_ex, "mix_d_bwd")
    dp = (dp_a, dp_b, dp_c, dq, dk, dv)
    dx, dxb, g["norm1_g"] = _mm_normbwd(dp, w["w_in_t"][l], st["x"], w["norm1_g"][l], dx, "in_proj_dx")
    g["w_in_t"] = jnp.concatenate([_mm_tn(part, st["h1"], "in_proj_dw", _BF) for part in dp], axis=0)
    return dx, dxb, g


def _local_fwd_bwd(x, target, w, n_ex):
    depth = len(w["w_in_t"])
    saved = []
    for l in range(depth):
        st = _fwd_mix(x, w, l, n_ex)
        x = _fwd_ffn(st, w, l, n_ex)
        saved.append(st)
    dx, dxb, d_final_g, loss = _final_loss(x, w["final_g"], target, "final_loss")
    grads = {}
    for l in reversed(range(depth)):
        dx, dxb, g_ffn = _bwd_ffn(saved[l], w, l, dx, dxb, n_ex)
        dx, dxb, g_mix = _bwd_mix(saved[l], w, l, dx, dxb, n_ex)
        for k, v in {**g_ffn, **g_mix}.items():
            grads.setdefault(k, [None] * depth)[l] = v
    grads["final_g"] = d_final_g
    return loss, dx, grads


_MESH = pl.DeviceIdType.MESH
_ANY = pl.BlockSpec(memory_space=pl.ANY)


def _position():
    return lax.axis_index("x"), lax.axis_index("y"), lax.axis_index("c")


def _flat(px, py, pc):
    return 4 * px + 2 * py + pc


def _all_gather(shard, name):
    r, c_ = shard.shape

    def body(x_ref, out_ref, send_sems, recv_sems, local_sem):
        x, y, c = _position()
        me, sibling = (x, y, c), (x, y, 1 - c)
        chips = [(1 - x, y), (x, 1 - y), (1 - x, 1 - y)]

        def copy(k, block, to, src=None):
            slab = out_ref.at[_flat(*block)]
            return pltpu.make_async_remote_copy(
                src_ref=slab if src is None else src, dst_ref=slab, send_sem=send_sems.at[k], recv_sem=recv_sems.at[k],
                device_id=to, device_id_type=_MESH)

        mine = pltpu.make_async_copy(x_ref, out_ref.at[_flat(*me)], local_sem)
        mine.start()
        first = [copy(0, me, sibling, src=x_ref)]
        first += [copy(1 + j, me, (*chip, c), src=x_ref) for j, chip in enumerate(chips)]
        for cp in first:
            cp.start()
        passed = [copy(4 + j, (*chip, c), sibling) for j, chip in enumerate(chips)]
        for j, chip in enumerate(chips):
            copy(1 + j, (*chip, c), me).wait_recv()
            passed[j].start()
        copy(0, sibling, me).wait_recv()
        for j, chip in enumerate(chips):
            copy(4 + j, (*chip, 1 - c), me).wait_recv()
        for cp in first + passed:
            cp.wait_send()
        mine.wait()

    return pl.pallas_call(
        body, name=name, out_shape=jax.ShapeDtypeStruct((N_DEV, r, c_), shard.dtype),
        in_specs=[_ANY], out_specs=_ANY,
        scratch_shapes=[pltpu.SemaphoreType.DMA((7,)), pltpu.SemaphoreType.DMA((7,)), pltpu.SemaphoreType.DMA],
    )(shard)


_HBM = pl.BlockSpec(memory_space=pltpu.HBM)
_SEM = pl.BlockSpec(memory_space=pltpu.SEMAPHORE)
_DATAFLOW = pltpu.SideEffectType.DATAFLOW_SIDE_EFFECTING


def _peers(x, y, c):
    return [((1 - x) if (k + 1) & 4 else x, (1 - y) if (k + 1) & 2 else y, (1 - c) if (k + 1) & 1 else c)
            for k in range(N_DEV - 1)]


def _direct_copies(src_ref, land_ref, send_sems, recv_sems, to_all):
    x, y, c = _position()
    my = _flat(x, y, c)
    out, back = [], []
    for k, peer in enumerate(_peers(x, y, c)):
        src = src_ref if to_all else src_ref.at[_flat(*peer)]
        sems = dict(send_sem=send_sems.at[k], recv_sem=recv_sems.at[k], device_id=peer, device_id_type=_MESH)
        out.append(pltpu.make_async_remote_copy(src_ref=src, dst_ref=land_ref.at[my], **sems))
        back.append(pltpu.make_async_remote_copy(src_ref=src, dst_ref=land_ref.at[_flat(*peer)], **sems))
    return out, back


def _exchange_start(src, to_all, after, name):
    r, c_ = src.shape[-2:]

    def body(src_ref, land_ref, after_ref, send_sems, recv_sems, src_thru, land_thru, token):
        for cp in _direct_copies(src_ref, land_ref, send_sems, recv_sems, to_all)[0]:
            cp.start()
        token[...] = jnp.zeros_like(token)

    land = pltpu.with_memory_space_constraint(lax.empty((N_DEV, r, c_), src.dtype), pltpu.HBM)
    send_sems, recv_sems, src_thru, land_thru, token = pl.pallas_call(
        body, name=name,
        out_shape=(pltpu.SemaphoreType.DMA((N_DEV - 1,)), pltpu.SemaphoreType.DMA((N_DEV - 1,)),
                   pltpu.HBM(src.shape, src.dtype), pltpu.HBM((N_DEV, r, c_), src.dtype), jax.ShapeDtypeStruct((8, 128), _F32)),
        in_specs=(_HBM, _HBM, _ANY), out_specs=(_SEM, _SEM, _HBM, _HBM, pl.BlockSpec(memory_space=pltpu.VMEM)),
        input_output_aliases={0: 2, 1: 3},
        compiler_params=pltpu.CompilerParams(has_side_effects=_DATAFLOW),
    )(pltpu.with_memory_space_constraint(src, pltpu.HBM), land, after)
    return (send_sems, recv_sems, src_thru, land_thru, to_all), token


def _exchange_wait(handle, after, name):
    send_sems, recv_sems, src_thru, land_thru, to_all = handle

    def body(src_ref, land_ref, send_sems, recv_sems, after_ref, src_dead, got_ref):
        out, back = _direct_copies(src_ref, land_ref, send_sems, recv_sems, to_all)
        for cp in out:
            cp.wait_send()
        for cp in back:
            cp.wait_recv()

    return pl.pallas_call(
        body, name=name,
        out_shape=(pltpu.HBM(src_thru.shape, src_thru.dtype), pltpu.HBM(land_thru.shape, land_thru.dtype)),
        in_specs=(_HBM, _HBM, _SEM, _SEM, _ANY), out_specs=(_HBM, _HBM), input_output_aliases={0: 0, 1: 1},
        compiler_params=pltpu.CompilerParams(has_side_effects=_DATAFLOW),
    )(src_thru, land_thru, send_sems, recv_sems, after)


def _with_own(landed, own):
    my = _flat(*_position())
    return lax.dynamic_update_slice(landed, own[None], (my, 0, 0))


def _sum_slabs(slabs, name):
    n, r, c_ = slabs.shape
    tr = _pick_tile(r, 16, max(16, (4 << 20) // (n * c_ * slabs.dtype.itemsize)))

    def body(x_ref, o_ref):
        acc = x_ref[0].astype(_F32)
        for k in range(1, n):
            acc = acc + x_ref[k].astype(_F32)
        o_ref[...] = acc

    return pl.pallas_call(
        body, name=name, grid=(r // tr,),
        in_specs=[pl.BlockSpec((n, tr, c_), lambda i: (0, i, 0))],
        out_specs=pl.BlockSpec((tr, c_), lambda i: (i, 0)),
        out_shape=jax.ShapeDtypeStruct((r, c_), _F32),
        compiler_params=_params("parallel"),
    )(slabs)


def _adamw(w, g, m, v, name):
    r, c_ = w.shape
    tr = _pick_tile(r, 8, 512)

    def body(w_ref, g_ref, m_ref, v_ref, d_ref, nm_ref, nv_ref):
        gv = g_ref[...]
        nm = ADAM_B1 * m_ref[...] + (1.0 - ADAM_B1) * gv
        nv = ADAM_B2 * v_ref[...] + (1.0 - ADAM_B2) * (gv * gv)
        m_hat = nm / (1.0 - ADAM_B1 ** ADAM_STEP)
        v_hat = nv / (1.0 - ADAM_B2 ** ADAM_STEP)
        d_ref[...] = -ADAM_LR * (m_hat / (jnp.sqrt(v_hat) + ADAM_EPS) + ADAM_WD * w_ref[...])
        nm_ref[...] = nm
        nv_ref[...] = nv

    spec = pl.BlockSpec((tr, c_), lambda i: (i, 0))
    shape = jax.ShapeDtypeStruct((r, c_), _F32)
    return pl.pallas_call(
        body, name=name, grid=(r // tr,), in_specs=[spec] * 4, out_specs=[spec] * 3, out_shape=[shape] * 3,
        compiler_params=_params("parallel"),
    )(w, g, m, v)


_SMALL = ("norm1_g", "conv_a_w", "conv_b_w", "conv_b_b", "ln_b_g", "ln_b_b", "ln_c_g", "ln_c_b", "sgu_w", "sgu_b",
          "norm2_g", "conv_f_w", "final_g")
_CONV_SHARDED = ("conv_a_w", "conv_b_w", "conv_f_w")
_NAMES = ("norm1_g", "w_in", "conv_a_w", "conv_b_w", "conv_b_b", "ln_b_g", "ln_b_b", "ln_c_g", "ln_c_b", "sgu_w", "sgu_b",
          "w_out", "norm2_g", "w_up", "conv_f_w", "w_down", "final_g")


def _pack_rows(parts, lanes=128, row_multiple=8):
    flat = jnp.concatenate([a.reshape(-1) for a in parts])
    rows = -(-flat.shape[0] // lanes)
    rows = -(-rows // row_multiple) * row_multiple
    return jnp.pad(flat, (0, rows * lanes - flat.shape[0])).reshape(rows, lanes)


def _unpack_rows(packed, shapes):
    flat = packed.reshape(-1)
    out, off = [], 0
    for shp in shapes:
        size = 1
        for s in shp:
            size *= s
        out.append(flat[off:off + size].reshape(shp))
        off += size
    return out


def _gather_conv_weights(conv_a_w, conv_b_w, conv_f_w):
    shards = (conv_a_w, conv_b_w, conv_f_w)
    gathered = _all_gather(_pack_rows(shards), "gather_conv_weights")
    full = []
    per_dev = [_unpack_rows(gathered[d], [s.shape for s in shards]) for d in range(N_DEV)]
    for i in range(len(shards)):
        full.append(jnp.concatenate([per_dev[d][i] for d in range(N_DEV)], axis=-1))
    return full


def kernel(x, norm1_g, w_in, conv_a_w, conv_b_w, conv_b_b, ln_b_g, ln_b_b, ln_c_g, ln_c_b, sgu_w, sgu_b, w_out, norm2_g, w_up, conv_f_w, w_down, final_g, loss_target, m_norm1_g, m_w_in, m_conv_a_w, m_conv_b_w, m_conv_b_b, m_ln_b_g, m_ln_b_b, m_ln_c_g, m_ln_c_b, m_sgu_w, m_sgu_b, m_w_out, m_norm2_g, m_w_up, m_conv_f_w, m_w_down, m_final_g, v_norm1_g, v_w_in, v_conv_a_w, v_conv_b_w, v_conv_b_b, v_ln_b_g, v_ln_b_b, v_ln_c_g, v_ln_c_b, v_sgu_w, v_sgu_b, v_w_out, v_norm2_g, v_w_up, v_conv_f_w, v_w_down, v_final_g):
    weights = dict(norm1_g=norm1_g, w_in=w_in, conv_a_w=conv_a_w, conv_b_w=conv_b_w, conv_b_b=conv_b_b, ln_b_g=ln_b_g,
                   ln_b_b=ln_b_b, ln_c_g=ln_c_g, ln_c_b=ln_c_b, sgu_w=sgu_w, sgu_b=sgu_b, w_out=w_out, norm2_g=norm2_g,
                   w_up=w_up, conv_f_w=conv_f_w, w_down=w_down, final_g=final_g)
    mom1 = dict(norm1_g=m_norm1_g, w_in=m_w_in, conv_a_w=m_conv_a_w, conv_b_w=m_conv_b_w, conv_b_b=m_conv_b_b,
                ln_b_g=m_ln_b_g, ln_b_b=m_ln_b_b, ln_c_g=m_ln_c_g, ln_c_b=m_ln_c_b, sgu_w=m_sgu_w, sgu_b=m_sgu_b,
                w_out=m_w_out, norm2_g=m_norm2_g, w_up=m_w_up, conv_f_w=m_conv_f_w, w_down=m_w_down, final_g=m_final_g)
    mom2 = dict(norm1_g=v_norm1_g, w_in=v_w_in, conv_a_w=v_conv_a_w, conv_b_w=v_conv_b_w, conv_b_b=v_conv_b_b,
                ln_b_g=v_ln_b_g, ln_b_b=v_ln_b_b, ln_c_g=v_ln_c_g, ln_c_b=v_ln_c_b, sgu_w=v_sgu_w, sgu_b=v_sgu_b,
                w_out=v_w_out, norm2_g=v_norm2_g, w_up=v_w_up, conv_f_w=v_conv_f_w, w_down=v_w_down, final_g=v_final_g)
    n_ex, seq, d = x.shape
    depth = w_in.shape[0]
    assert depth == 2
    my = _flat(*_position())
    row = lambda a, l: a[l][None]
    tied = lambda a, token: a + token[0:1, 0:1]

    slab = {"w_in": [_bf(jnp.swapaxes(w_in, 1, 2)[l]) for l in range(depth)], "w_out": [_bf(w_out[l]) for l in range(depth)],
            "w_up": [_bf(jnp.swapaxes(w_up, 1, 2)[l]) for l in range(depth)], "w_down": [_bf(w_down[l]) for l in range(depth)]}
    rows = {name: parts[0].shape[0] for name, parts in slab.items()}
    key_of = {"w_in": "w_in_t", "w_out": "w_out", "w_up": "w_up_t", "w_down": "w_down"}
    rest_layer0 = [("w_out", 0), ("w_up", 0), ("w_down", 0)]
    all_layer1 = [("w_in", 1), ("w_out", 1), ("w_up", 1), ("w_down", 1)]

    def split_rows(a, which, merge):
        out, off = {}, 0
        for name, l in which:
            part = a[..., off:off + rows[name], :]
            out[(name, l)] = part.reshape(N_DEV * rows[name], d) if merge else part
            off += rows[name]
        return out

    send_w0 = jnp.concatenate([slab[n][l] for n, l in rest_layer0], axis=0)
    send_w1 = jnp.concatenate([slab[n][l] for n, l in all_layer1], axis=0)
    gather0, token = _exchange_start(send_w0, True, norm1_g, "gather_layer0_start")
    conv_a_full, conv_b_full, conv_f_full = _gather_conv_weights(conv_a_w, conv_b_w, conv_f_w)
    w = {
        "norm1_g": [row(norm1_g, l) for l in range(depth)], "w_in_t": [None] * depth,
        "conv_a_w": [conv_a_full[l] for l in range(depth)], "conv_b_w": [conv_b_full[l] for l in range(depth)],
        "conv_b_b": [row(conv_b_b, l) for l in range(depth)], "ln_b_g": [row(ln_b_g, l) for l in range(depth)],
        "ln_b_b": [row(ln_b_b, l) for l in range(depth)], "ln_c_g": [row(ln_c_g, l) for l in range(depth)],
        "ln_c_b": [row(ln_c_b, l) for l in range(depth)], "sgu_w": [sgu_w[l] for l in range(depth)],
        "sgu_b_full": [jnp.repeat(sgu_b[l].T, HEAD_DIM, axis=1) for l in range(depth)],
        "w_out": [None] * depth, "norm2_g": [row(norm2_g, l) for l in range(depth)], "w_up_t": [None] * depth,
        "conv_f_w": [conv_f_full[l] for l in range(depth)], "w_down": [None] * depth, "final_g": final_g[None],
    }
    w["w_in_t"][0] = _all_gather(slab["w_in"][0], "gather_w_in0").reshape(N_DEV * rows["w_in"], d)
    w["norm1_g"][0] = tied(row(norm1_g, 0), token)

    def land_weights(handle, after, which, name):
        own, landed = _exchange_wait(handle, after, name)
        for (n, l), mat in split_rows(_with_own(landed, own), which, True).items():
            w[key_of[n]][l] = mat
        return landed

    st0 = _fwd_mix(x.reshape(n_ex * seq, d), w, 0, n_ex)
    landed0 = land_weights(gather0, st0["mix"][3], rest_layer0, "gather_layer0_wait")
    gather1, token = _exchange_start(send_w1, True, landed0, "gather_layer1_start")
    w["norm2_g"][0] = tied(row(norm2_g, 0), token)
    x_mid = _fwd_ffn(st0, w, 0, n_ex)
    land_weights(gather1, x_mid, all_layer1, "gather_layer1_wait")
    st1 = _fwd_mix(x_mid, w, 1, n_ex)
    x_out = _fwd_ffn(st1, w, 1, n_ex)
    dx, dxb, d_final_g, loss = _final_loss(x_out, w["final_g"], loss_target.reshape(n_ex * seq, d), "final_loss")
    loss = lax.psum(loss[0, 0], ("x", "y", "c"))

    def send_grads(g, which, after, name):
        slabs = jnp.concatenate([g[key_of[n]].reshape(N_DEV, rows[n], d) for n, _ in which], axis=1)
        return _exchange_start(slabs, False, after, name)

    dx, dxb, g_ffn1 = _bwd_ffn(st1, w, 1, dx, dxb, n_ex)
    dx, dxb, g_mix1 = _bwd_mix(st1, w, 1, dx, dxb, n_ex)
    grads1, token = send_grads({**g_ffn1, **g_mix1}, all_layer1, dx, "exchange_layer1_start")
    w["norm2_g"][0] = tied(row(norm2_g, 0), token)
    dx, dxb, g_ffn0 = _bwd_ffn(st0, w, 0, dx, dxb, n_ex)
    ffn_layer0 = [("w_up", 0), ("w_down", 0)]
    grads0a, token = send_grads(g_ffn0, ffn_layer0, dx, "exchange_ffn0_start")
    w["norm1_g"][0] = tied(row(norm1_g, 0), token)
    dx, dxb, g_mix0 = _bwd_mix(st0, w, 0, dx, dxb, n_ex)
    mix_layer0 = [("w_in", 0), ("w_out", 0)]
    grads0b, _ = send_grads(g_mix0, mix_layer0, dx, "exchange_mix0_start")
    grad_x = dx.reshape(n_ex, seq, d)
    g = {k: [{**g_ffn0, **g_mix0}[k], {**g_ffn1, **g_mix1}[k]] for k in g_mix0.keys() | g_ffn0.keys()}
    g["final_g"] = d_final_g

    reduced = {}

    def land_grads(handle, after, which, name):
        sent, landed = _exchange_wait(handle, after, name + "_wait")
        own = lax.dynamic_index_in_dim(sent, my, 0, keepdims=False)
        total = _sum_slabs(_with_own(landed, own), name + "_sum")
        reduced.update(split_rows(total, which, False))
        return total

    done = land_grads(grads1, dx, all_layer1, "exchange_layer1")
    done = land_grads(grads0a, done, ffn_layer0, "exchange_ffn0")
    land_grads(grads0b, done, mix_layer0, "exchange_mix0")
    grads = {}
    for name in rows:
        stacked = jnp.stack([reduced[(name, l)] for l in range(depth)])
        grads[name] = jnp.swapaxes(stacked, 1, 2) if name in ("w_in", "w_up") else stacked

    small_local = {
        "norm1_g": jnp.stack([a[0] for a in g["norm1_g"]]), "conv_a_w": jnp.stack(g["conv_a_w"]),
        "conv_b_w": jnp.stack(g["conv_b_w"]), "conv_b_b": jnp.stack([a[0] for a in g["conv_b_b"]]),
        "ln_b_g": jnp.stack([a[0] for a in g["ln_b_g"]]), "ln_b_b": jnp.stack([a[0] for a in g["ln_b_b"]]),
        "ln_c_g": jnp.stack([a[0] for a in g["ln_c_g"]]), "ln_c_b": jnp.stack([a[0] for a in g["ln_c_b"]]),
        "sgu_w": jnp.stack(g["sgu_w"]), "sgu_b": jnp.stack([a.T for a in g["sgu_b_t"]]),
        "norm2_g": jnp.stack([a[0] for a in g["norm2_g"]]), "conv_f_w": jnp.stack(g["conv_f_w"]),
        "final_g": g["final_g"][0],
    }
    small_sum = _sum_slabs(_all_gather(_pack_rows([small_local[k] for k in _SMALL]), "gather_small_grads"),
                           "sum_small_grads")
    for name, total in zip(_SMALL, _unpack_rows(small_sum, [small_local[k].shape for k in _SMALL])):
        if name in _CONV_SHARDED:
            width = weights[name].shape[-1]
            total = lax.dynamic_slice_in_dim(total, my * width, width, axis=-1)
        grads[name] = total

    delta, new_m, new_v = {}, {}, {}
    for name in _NAMES:
        shp = weights[name].shape
        two_d = (-1, shp[-1]) if len(shp) > 1 else (1, shp[0])
        outs = _adamw(weights[name].reshape(two_d), grads[name].reshape(two_d), mom1[name].reshape(two_d),
                      mom2[name].reshape(two_d), "adamw_" + name)
        delta[name], new_m[name], new_v[name] = (o.reshape(shp) for o in outs)

    return (loss, grad_x, *[grads[n] for n in _NAMES], *[delta[n] for n in _NAMES], *[new_m[n] for n in _NAMES],
            *[new_v[n] for n in _NAMES])
```

```python
import functools

import jax
import jax.numpy as jnp
from jax import lax
from jax.experimental import pallas as pl
from jax.experimental.pallas import tpu as pltpu

_F32 = jnp.float32
_BF = jnp.bfloat16

HEAD_DIM = 64
MIX_W = 256
N_HEADS = MIX_W // HEAD_DIM
CHUNK = 128
KV_BLOCK = 128
K_SHORT = 3
K_CONF = 31
K_FFN = 3
RMS_EPS = 1e-6
LN_EPS = 1e-5
ADAM_LR = 0.001
ADAM_B1 = 0.9
ADAM_B2 = 0.999
ADAM_EPS = 1e-08
ADAM_WD = 0.01
ADAM_STEP = 10
N_DEV = 8
VMEM_LIMIT = 56 * 1024 * 1024


def _bf(x):
    return x.astype(_BF)


def _ld(ref):
    return ref[...].astype(_F32)


def _params(*sem):
    return pltpu.CompilerParams(dimension_semantics=sem, vmem_limit_bytes=VMEM_LIMIT)


def _dot(a, b):
    return jnp.dot(a, b, preferred_element_type=_F32)


def _dot_nt(a, b):
    return lax.dot_general(a, b, (((1,), (1,)), ((), ())), preferred_element_type=_F32)


def _dot_tn(a, b):
    return lax.dot_general(a, b, (((0,), (0,)), ((), ())), preferred_element_type=_F32)


def _row_tile(t, want):
    return want if t % want == 0 else t


def _pick_tile(rows, unit, max_rows):
    best = 0
    for cand in range(unit, min(rows, max_rows) + 1, unit):
        if rows % cand == 0:
            best = cand
    return best or rows


def _sigmoid(x):
    return 1.0 / (1.0 + jnp.exp(-x))


def _rms_rstd(x):
    return lax.rsqrt(jnp.mean(x * x, axis=-1, keepdims=True) + RMS_EPS)


def _norm_mm(x, g, w_t, name):
    t, d = x.shape
    n = w_t.shape[0]
    tm = _row_tile(t, 512)
    tn = _row_tile(n, 512)

    def body(x_ref, g_ref, w_ref, p_ref, h_ref):
        xv = x_ref[...]
        h = _bf(xv * _rms_rstd(xv) * g_ref[...])
        h_ref[...] = h
        for n0 in range(0, n, tn):
            p_ref[:, n0:n0 + tn] = _bf(_dot_nt(h, w_ref[n0:n0 + tn, :]))

    return pl.pallas_call(
        body, name=name, grid=(t // tm,),
        in_specs=[pl.BlockSpec((tm, d), lambda i: (i, 0)), pl.BlockSpec((1, d), lambda i: (0, 0)),
                  pl.BlockSpec((n, d), lambda i: (0, 0))],
        out_specs=[pl.BlockSpec((tm, n), lambda i: (i, 0)), pl.BlockSpec((tm, d), lambda i: (i, 0))],
        out_shape=[jax.ShapeDtypeStruct((t, n), _BF), jax.ShapeDtypeStruct((t, d), _BF)],
        compiler_params=_params("parallel"),
    )(x, g, w_t)


def _mm_nt(a, w_t, name):
    t, k = a.shape
    n = w_t.shape[0]
    tm = _row_tile(t, 512)
    tn = _row_tile(n, 512) if n % 512 == 0 else _row_tile(n, 256)

    def body(a_ref, w_ref, o_ref):
        av = a_ref[...]
        for n0 in range(0, n, tn):
            o_ref[:, n0:n0 + tn] = _bf(_dot_nt(av, w_ref[n0:n0 + tn, :]))

    return pl.pallas_call(
        body, name=name, grid=(t // tm,),
        in_specs=[pl.BlockSpec((tm, k), lambda i: (i, 0)), pl.BlockSpec((n, k), lambda i: (0, 0))],
        out_specs=pl.BlockSpec((tm, n), lambda i: (i, 0)),
        out_shape=jax.ShapeDtypeStruct((t, n), _BF),
        compiler_params=_params("parallel"),
    )(a, w_t)


def _mm_res(parts, w, x, name):
    t = x.shape[0]
    k, d = w.shape
    tm = _row_tile(t, 512)
    widths = [a.shape[1] for a in parts]
    n_parts = len(parts)

    def body(*refs):
        w_ref, x_ref, o_ref = refs[n_parts:]
        acc, off = x_ref[...], 0
        for a_ref, width in zip(refs[:n_parts], widths):
            acc = acc + _dot(a_ref[...], w_ref[off:off + width, :])
            off += width
        o_ref[...] = acc

    return pl.pallas_call(
        body, name=name, grid=(t // tm,),
        in_specs=[pl.BlockSpec((tm, width), lambda i: (i, 0)) for width in widths] + [
            pl.BlockSpec((k, d), lambda i: (0, 0)), pl.BlockSpec((tm, d), lambda i: (i, 0))],
        out_specs=pl.BlockSpec((tm, d), lambda i: (i, 0)),
        out_shape=jax.ShapeDtypeStruct((t, d), _F32),
        compiler_params=_params("parallel"),
    )(*parts, w, x)


def _mm_normbwd(parts, w, x, g, dres, name):
    t = x.shape[0]
    k, d = w.shape
    tm = _row_tile(t, 256)
    widths = [a.shape[1] for a in parts]
    n_parts = len(parts)

    def body(*refs):
        a_refs = refs[:n_parts]
        w_ref, x_ref, g_ref, r_ref, dx_ref, dxb_ref, dg_ref = refs[n_parts:]
        dh, off = None, 0
        for a_ref, width in zip(a_refs, widths):
            term = _dot(_bf(a_ref[...]), w_ref[off:off + width, :])
            dh = term if dh is None else dh + term
            off += width
        xv = x_ref[...]
        rstd = _rms_rstd(xv)
        xn = xv * rstd
        u = dh * g_ref[...]
        dx = r_ref[...] + rstd * (u - xn * jnp.mean(u * xn, axis=-1, keepdims=True))
        dx_ref[...] = dx
        dxb_ref[...] = _bf(dx)

        @pl.when(pl.program_id(0) == 0)
        def _():
            dg_ref[...] = jnp.zeros_like(dg_ref)

        dg_ref[...] += jnp.sum(dh * xn, axis=0, keepdims=True)

    return pl.pallas_call(
        body, name=name, grid=(t // tm,),
        in_specs=[pl.BlockSpec((tm, width), lambda i: (i, 0)) for width in widths] + [
            pl.BlockSpec((k, d), lambda i: (0, 0)),
            pl.BlockSpec((tm, d), lambda i: (i, 0)), pl.BlockSpec((1, d), lambda i: (0, 0)),
            pl.BlockSpec((tm, d), lambda i: (i, 0))],
        out_specs=[pl.BlockSpec((tm, d), lambda i: (i, 0)), pl.BlockSpec((tm, d), lambda i: (i, 0)),
                   pl.BlockSpec((1, d), lambda i: (0, 0))],
        out_shape=[jax.ShapeDtypeStruct((t, d), _F32), jax.ShapeDtypeStruct((t, d), _BF),
                   jax.ShapeDtypeStruct((1, d), _F32)],
        compiler_params=_params("arbitrary"),
    )(*parts, w, x, g, dres)


def _mm_tn(a, b, name, out_dtype):
    t, m = a.shape
    n = b.shape[1]
    tm = _pick_tile(m, 128, 1408)
    tn = _pick_tile(n, 128, 1024)
    tk = _row_tile(t, 1024)
    nk = t // tk

    def body(a_ref, b_ref, o_ref, acc):
        kk = pl.program_id(2)

        @pl.when(kk == 0)
        def _():
            acc[...] = jnp.zeros_like(acc)

        acc[...] += _dot_tn(_bf(a_ref[...]), b_ref[...])

        @pl.when(kk == nk - 1)
        def _():
            o_ref[...] = acc[...].astype(o_ref.dtype)

    return pl.pallas_call(
        body, name=name, grid=(m // tm, n // tn, nk),
        in_specs=[pl.BlockSpec((tk, tm), lambda i, j, kk: (kk, i)), pl.BlockSpec((tk, tn), lambda i, j, kk: (kk, j))],
        out_specs=pl.BlockSpec((tm, tn), lambda i, j, kk: (i, j)),
        out_shape=jax.ShapeDtypeStruct((m, n), out_dtype),
        scratch_shapes=[pltpu.VMEM((tm, tn), _F32)],
        compiler_params=_params("parallel", "parallel", "arbitrary"),
    )(a, b)


def _mm_tn_parts(parts, b, name):
    t, n = b.shape
    widths = [a.shape[1] for a in parts]
    m = sum(widths)
    n_parts = len(parts)
    tk = _row_tile(t, 1024)
    nk = t // tk

    def body(*refs):
        b_ref, o_ref, acc = refs[n_parts:]
        kk = pl.program_id(0)

        @pl.when(kk == 0)
        def _():
            acc[...] = jnp.zeros_like(acc)

        bv = b_ref[...]
        off = 0
        for a_ref, width in zip(refs[:n_parts], widths):
            acc[off:off + width, :] += _dot_tn(_bf(a_ref[...]), bv)
            off += width

        @pl.when(kk == nk - 1)
        def _():
            o_ref[...] = _bf(acc[...])

    return pl.pallas_call(
        body, name=name, grid=(nk,),
        in_specs=[pl.BlockSpec((tk, width), lambda kk: (kk, 0)) for width in widths] + [pl.BlockSpec((tk, n), lambda kk: (kk, 0))],
        out_specs=pl.BlockSpec((m, n), lambda kk: (0, 0)),
        out_shape=jax.ShapeDtypeStruct((m, n), _BF),
        scratch_shapes=[pltpu.VMEM((m, n), _F32)],
        compiler_params=_params("arbitrary"),
    )(*parts, b)


def _final_loss(x, g, target, name):
    t, d = x.shape
    tm = _row_tile(t, 256)

    def body(x_ref, g_ref, t_ref, dx_ref, dxb_ref, dg_ref, loss_ref):
        xv = x_ref[...]
        rstd = _rms_rstd(xv)
        xn = xv * rstd
        err = xn * g_ref[...] - t_ref[...]
        dy = err * (1.0 / d)
        u = dy * g_ref[...]
        dx = rstd * (u - xn * jnp.mean(u * xn, axis=-1, keepdims=True))
        dx_ref[...] = dx
        dxb_ref[...] = _bf(dx)

        @pl.when(pl.program_id(0) == 0)
        def _():
            dg_ref[...] = jnp.zeros_like(dg_ref)
            loss_ref[...] = jnp.zeros_like(loss_ref)

        dg_ref[...] += jnp.sum(dy * xn, axis=0, keepdims=True)
        loss_ref[...] += (0.5 / d) * jnp.sum(jnp.sum(err * err, axis=1, keepdims=True), axis=0, keepdims=True)

    return pl.pallas_call(
        body, name=name, grid=(t // tm,),
        in_specs=[pl.BlockSpec((tm, d), lambda i: (i, 0)), pl.BlockSpec((1, d), lambda i: (0, 0)),
                  pl.BlockSpec((tm, d), lambda i: (i, 0))],
        out_specs=[pl.BlockSpec((tm, d), lambda i: (i, 0)), pl.BlockSpec((tm, d), lambda i: (i, 0)),
                   pl.BlockSpec((1, d), lambda i: (0, 0)), pl.BlockSpec((1, 1), lambda i: (0, 0))],
        out_shape=[jax.ShapeDtypeStruct((t, d), _F32), jax.ShapeDtypeStruct((t, d), _BF),
                   jax.ShapeDtypeStruct((1, d), _F32), jax.ShapeDtypeStruct((1, 1), _F32)],
        compiler_params=_params("arbitrary"),
    )(x, g, target)


def _pad_rows(x, pad):
    return jnp.concatenate([x, jnp.zeros((pad, x.shape[1]), x.dtype)], axis=0)


def _shift_down(xp, s):
    return xp if s == 0 else pltpu.roll(xp, s, 0)


def _shift_up(xp, s):
    return xp if s == 0 else pltpu.roll(xp, xp.shape[0] - s, 0)


def _taps3(xp):
    one = _shift_down(xp, 1)
    return xp, one, _shift_down(one, 1)


def _conv3_taps(taps, w_ref):
    return w_ref[2:3, :] * taps[0] + w_ref[1:2, :] * taps[1] + w_ref[0:1, :] * taps[2]


def _conv3(xp, w_ref):
    return _conv3_taps(_taps3(xp), w_ref)


def _conv3_t(dyp, w_ref):
    one = _shift_up(dyp, 1)
    return w_ref[2:3, :] * dyp + w_ref[1:2, :] * one + w_ref[0:1, :] * _shift_up(one, 1)


def _conv3_dw(dyp, taps):
    return [jnp.sum(dyp * taps[2 - k], axis=0, keepdims=True) for k in range(3)]


def _ffn_mid_fwd(up_pre, wf, n_ex, name):
    t, f2 = up_pre.shape
    f = f2 // 2
    s = t // n_ex
    cb = MIX_W
    nb = f // cb

    def body(ug_ref, uv_ref, wg_ref, wv_ref, act_ref):
        gf = _conv3(_pad_rows(ug_ref[...].astype(_F32), 8), wg_ref)[:s]
        vf = _conv3(_pad_rows(uv_ref[...].astype(_F32), 8), wv_ref)[:s]
        act_ref[...] = _bf(gf * _sigmoid(gf) * vf)

    return pl.pallas_call(
        body, name=name, grid=(n_ex, nb),
        in_specs=[pl.BlockSpec((s, cb), lambda e, j: (e, j)), pl.BlockSpec((s, cb), lambda e, j: (e, j + nb)),
                  pl.BlockSpec((K_FFN, cb), lambda e, j: (0, j)), pl.BlockSpec((K_FFN, cb), lambda e, j: (0, j + nb))],
        out_specs=pl.BlockSpec((s, cb), lambda e, j: (e, j)),
        out_shape=jax.ShapeDtypeStruct((t, f), _BF),
        compiler_params=_params("parallel", "parallel"),
    )(up_pre, up_pre, wf, wf)


def _ffn_mid_bwd(up_pre, wf, dact, n_ex, name):
    t, f2 = up_pre.shape
    f = f2 // 2
    s = t // n_ex
    cb = MIX_W
    nb = f // cb

    def body(ug_ref, uv_ref, wg_ref, wv_ref, da_ref, dug_ref, duv_ref, dwg_ref, dwv_ref):
        g_taps = _taps3(_pad_rows(ug_ref[...].astype(_F32), 8))
        v_taps = _taps3(_pad_rows(uv_ref[...].astype(_F32), 8))
        gf = _conv3_taps(g_taps, wg_ref)[:s]
        vf = _conv3_taps(v_taps, wv_ref)[:s]
        sg = _sigmoid(gf)
        da = da_ref[...].astype(_F32)

        @pl.when(pl.program_id(1) == 0)
        def _():
            dwg_ref[...] = jnp.zeros_like(dwg_ref)
            dwv_ref[...] = jnp.zeros_like(dwv_ref)

        def finish(dup, w_ref, taps, du_ref, dw_ref):
            dupp = _pad_rows(dup, 8)
            du_ref[...] = _bf(_conv3_t(dupp, w_ref)[:s])
            rows = _conv3_dw(dupp, taps)
            for k in range(3):
                dw_ref[k:k + 1, :] += rows[k]

        finish(da * vf * sg * (1.0 + gf * (1.0 - sg)), wg_ref, g_taps, dug_ref, dwg_ref)
        finish(da * gf * sg, wv_ref, v_taps, duv_ref, dwv_ref)

    return pl.pallas_call(
        body, name=name, grid=(nb, n_ex),
        in_specs=[pl.BlockSpec((s, cb), lambda j, e: (e, j)), pl.BlockSpec((s, cb), lambda j, e: (e, j + nb)),
                  pl.BlockSpec((K_FFN, cb), lambda j, e: (0, j)), pl.BlockSpec((K_FFN, cb), lambda j, e: (0, j + nb)),
                  pl.BlockSpec((s, cb), lambda j, e: (e, j))],
        out_specs=[pl.BlockSpec((s, cb), lambda j, e: (e, j)), pl.BlockSpec((s, cb), lambda j, e: (e, j)),
                   pl.BlockSpec((K_FFN, cb), lambda j, e: (0, j)), pl.BlockSpec((K_FFN, cb), lambda j, e: (0, j))],
        out_shape=[jax.ShapeDtypeStruct((t, f), _BF), jax.ShapeDtypeStruct((t, f), _BF),
                   jax.ShapeDtypeStruct((K_FFN, f), _F32), jax.ShapeDtypeStruct((K_FFN, f), _F32)],
        compiler_params=_params("parallel", "arbitrary"),
    )(up_pre, up_pre, wf, wf, dact)


def _pcol(s, j):
    return pl.BlockSpec((s, MIX_W), lambda e, j=j: (e, j))


def _vec(rows=1):
    return pl.BlockSpec((rows, MIX_W), lambda e: (0, 0))


def _mix_a_fwd(p, wa, n_ex, name):
    t = p.shape[0]
    s = t // n_ex

    def body(gb_ref, gc_ref, ha_ref, w_ref, y_ref):
        cv = _conv3(_pad_rows(_ld(gc_ref) * _ld(ha_ref), 8), w_ref)[:s]
        y_ref[...] = _bf(_ld(gb_ref) * cv)

    return pl.pallas_call(
        body, name=name, grid=(n_ex,),
        in_specs=[_pcol(s, 0), _pcol(s, 1), _pcol(s, 2), _vec(K_SHORT)],
        out_specs=pl.BlockSpec((s, MIX_W), lambda e: (e, 0)),
        out_shape=jax.ShapeDtypeStruct((t, MIX_W), _BF),
        compiler_params=_params("parallel"),
    )(p, p, p, wa)


def _mix_a_bwd(p, wa, dmix, n_ex, name):
    t = p.shape[0]
    s = t // n_ex

    def body(gb_ref, gc_ref, ha_ref, w_ref, dy_ref, dp_ref, dw_ref):
        gc = _ld(gc_ref)
        ha = _ld(ha_ref)
        up = _taps3(_pad_rows(gc * ha, 8))
        cv = _conv3_taps(up, w_ref)[:s]
        dy = _ld(dy_ref)
        dcvp = _pad_rows(dy * _ld(gb_ref), 8)
        du = _conv3_t(dcvp, w_ref)[:s]
        dp_ref[:, 0:MIX_W] = _bf(dy * cv)
        dp_ref[:, MIX_W:2 * MIX_W] = _bf(du * ha)
        dp_ref[:, 2 * MIX_W:3 * MIX_W] = _bf(du * gc)

        @pl.when(pl.program_id(0) == 0)
        def _():
            dw_ref[...] = jnp.zeros_like(dw_ref)

        rows = _conv3_dw(dcvp, up)
        for k in range(3):
            dw_ref[k:k + 1, :] += rows[k]

    return pl.pallas_call(
        body, name=name, grid=(n_ex,),
        in_specs=[_pcol(s, 0), _pcol(s, 1), _pcol(s, 2), _vec(K_SHORT), _pcol(s, 0)],
        out_specs=[pl.BlockSpec((s, 3 * MIX_W), lambda e: (e, 0)), _vec(K_SHORT)],
        out_shape=[jax.ShapeDtypeStruct((t, 3 * MIX_W), _BF), jax.ShapeDtypeStruct((K_SHORT, MIX_W), _F32)],
        compiler_params=_params("arbitrary"),
    )(p, p, p, wa, dmix)


CONF_PAD = 32


def _ln_fwd(x, g, b):
    mu = jnp.mean(x, axis=-1, keepdims=True)
    xc = x - mu
    rstd = lax.rsqrt(jnp.mean(xc * xc, axis=-1, keepdims=True) + LN_EPS)
    xhat = xc * rstd
    return xhat * g + b, xhat, rstd


def _ln_bwd(dy, xhat, rstd, g):
    dxh = dy * g
    return rstd * (dxh - jnp.mean(dxh, axis=-1, keepdims=True) - xhat * jnp.mean(dxh * xhat, axis=-1, keepdims=True))


def _mix_b_fwd(p, wb, bb, lg, lb, n_ex, name):
    t = p.shape[0]
    s = t // n_ex

    def body(val_ref, gat_ref, w_ref, bb_ref, lg_ref, lb_ref, y_ref, cb_ref):
        cur = _pad_rows(_ld(val_ref) * _sigmoid(_ld(gat_ref)), CONF_PAD)
        acc = w_ref[K_CONF - 1:K_CONF, :] * cur
        for sh in range(1, K_CONF):
            cur = pltpu.roll(cur, 1, 0)
            acc = acc + w_ref[K_CONF - 1 - sh:K_CONF - sh, :] * cur
        cb = acc[:s] + bb_ref[...]
        cb_ref[...] = cb
        yl, _, _ = _ln_fwd(cb, lg_ref[...], lb_ref[...])
        y_ref[...] = _bf(yl * _sigmoid(yl))

    return pl.pallas_call(
        body, name=name, grid=(n_ex,),
        in_specs=[_pcol(s, 3), _pcol(s, 4), _vec(K_CONF), _vec(), _vec(), _vec()],
        out_specs=[pl.BlockSpec((s, MIX_W), lambda e: (e, 0)), pl.BlockSpec((s, MIX_W), lambda e: (e, 0))],
        out_shape=[jax.ShapeDtypeStruct((t, MIX_W), _BF), jax.ShapeDtypeStruct((t, MIX_W), _F32)],
        compiler_params=_params("parallel"),
    )(p, p, wb, bb, lg, lb)


def _mix_b_bwd(p, cb, wb, lg, lb, dmix, n_ex, name):
    t = p.shape[0]
    s = t // n_ex

    def body(val_ref, gat_ref, cb_ref, w_ref, lg_ref, lb_ref, dy_ref, dp_ref, dw_ref, dbb_ref, dlg_ref, dlb_ref):
        @pl.when(pl.program_id(0) == 0)
        def _():
            for r in (dw_ref, dbb_ref, dlg_ref, dlb_ref):
                r[...] = jnp.zeros_like(r)

        yl, xhat, rstd = _ln_fwd(cb_ref[...], lg_ref[...], lb_ref[...])
        sy = _sigmoid(yl)
        dyl = _ld(dy_ref) * sy * (1.0 + yl * (1.0 - sy))
        dlg_ref[...] += jnp.sum(dyl * xhat, axis=0, keepdims=True)
        dlb_ref[...] += jnp.sum(dyl, axis=0, keepdims=True)
        dcb = _ln_bwd(dyl, xhat, rstd, lg_ref[...])
        dbb_ref[...] += jnp.sum(dcb, axis=0, keepdims=True)

        val = _ld(val_ref)
        sg = _sigmoid(_ld(gat_ref))
        dcbp = _pad_rows(dcb, CONF_PAD)
        cur = _pad_rows(val * sg, CONF_PAD)
        up = dcbp
        dglu = w_ref[K_CONF - 1:K_CONF, :] * up
        dw_ref[K_CONF - 1:K_CONF, :] += jnp.sum(dcbp * cur, axis=0, keepdims=True)
        n_pad = s + CONF_PAD
        for sh in range(1, K_CONF):
            cur = pltpu.roll(cur, 1, 0)
            up = pltpu.roll(up, n_pad - 1, 0)
            k = K_CONF - 1 - sh
            dglu = dglu + w_ref[k:k + 1, :] * up
            dw_ref[k:k + 1, :] += jnp.sum(dcbp * cur, axis=0, keepdims=True)
        dglu = dglu[:s]
        dp_ref[:, 0:MIX_W] = _bf(dglu * sg)
        dp_ref[:, MIX_W:2 * MIX_W] = _bf(dglu * val * sg * (1.0 - sg))

    return pl.pallas_call(
        body, name=name, grid=(n_ex,),
        in_specs=[_pcol(s, 3), _pcol(s, 4), pl.BlockSpec((s, MIX_W), lambda e: (e, 0)), _vec(K_CONF), _vec(), _vec(),
                  _pcol(s, 1)],
        out_specs=[pl.BlockSpec((s, 2 * MIX_W), lambda e: (e, 0)), _vec(K_CONF), _vec(), _vec(), _vec()],
        out_shape=[jax.ShapeDtypeStruct((t, 2 * MIX_W), _BF), jax.ShapeDtypeStruct((K_CONF, MIX_W), _F32),
                   jax.ShapeDtypeStruct((1, MIX_W), _F32), jax.ShapeDtypeStruct((1, MIX_W), _F32),
                   jax.ShapeDtypeStruct((1, MIX_W), _F32)],
        compiler_params=_params("arbitrary"),
    )(p, p, cb, wb, lg, lb, dmix)


_INV_SQRT2 = 0.7071067811865476
_INV_SQRT2PI = 0.3989422804014327


def _gelu(x):
    return 0.5 * x * (1.0 + lax.erf(x * _INV_SQRT2))


def _gelu_grad(x):
    return 0.5 * (1.0 + lax.erf(x * _INV_SQRT2)) + x * _INV_SQRT2PI * jnp.exp(-0.5 * x * x)


def _head_masks(width=MIX_W):
    lane = lax.broadcasted_iota(jnp.int32, (1, width), 1)
    return [(lane >= h * HEAD_DIM) & (lane < (h + 1) * HEAD_DIM) for h in range(N_HEADS)]


def _tril_mask():
    r = lax.broadcasted_iota(jnp.int32, (CHUNK, CHUNK), 0)
    c = lax.broadcasted_iota(jnp.int32, (CHUNK, CHUNK), 1)
    return c <= r


def _sgu_apply(ws_ref, x3, transpose):
    n = x3.shape[0]
    tril = _tril_mask()
    masks = _head_masks()
    xb = _bf(x3)
    out = jnp.zeros(x3.shape, _F32)
    for h in range(N_HEADS):
        w = _bf(jnp.where(tril, ws_ref[h], 0.0))
        wb = jnp.broadcast_to(w[None], (n, CHUNK, CHUNK))
        dims = (((1,), (1,)), ((0,), (0,))) if transpose else (((2,), (1,)), ((0,), (0,)))
        r = lax.dot_general(wb, xb, dims, preferred_element_type=_F32)
        out = out + jnp.where(masks[h][None], r, 0.0)
    return out


def _mix_c_fwd(p, lg, lb, ws, sb_full, n_ex, name):
    t = p.shape[0]
    s = t // n_ex
    nc = s // CHUNK

    def body(pu_ref, pv_ref, lg_ref, lb_ref, ws_ref, sb_ref, y_ref):
        u = _gelu(_ld(pu_ref))
        vl, _, _ = _ln_fwd(_gelu(_ld(pv_ref)), lg_ref[...], lb_ref[...])
        sp = _sgu_apply(ws_ref, vl.reshape(nc, CHUNK, MIX_W), False) + sb_ref[...][None]
        y_ref[...] = _bf(u * sp.reshape(s, MIX_W))

    return pl.pallas_call(
        body, name=name, grid=(n_ex,),
        in_specs=[_pcol(s, 5), _pcol(s, 6), _vec(), _vec(),
                  pl.BlockSpec((N_HEADS, CHUNK, CHUNK), lambda e: (0, 0, 0)), pl.BlockSpec((CHUNK, MIX_W), lambda e: (0, 0))],
        out_specs=pl.BlockSpec((s, MIX_W), lambda e: (e, 0)),
        out_shape=jax.ShapeDtypeStruct((t, MIX_W), _BF),
        compiler_params=_params("parallel"),
    )(p, p, lg, lb, ws, sb_full)


def _mix_c_bwd(p, lg, lb, ws, sb_full, dmix, n_ex, name):
    t = p.shape[0]
    s = t // n_ex
    nc = s // CHUNK

    def body(pu_ref, pv_ref, lg_ref, lb_ref, ws_ref, sb_ref, dy_ref, dp_ref, dlg_ref, dlb_ref, dws_ref, dsb_ref):
        @pl.when(pl.program_id(0) == 0)
        def _():
            for r in (dlg_ref, dlb_ref, dws_ref, dsb_ref):
                r[...] = jnp.zeros_like(r)

        pu = _ld(pu_ref)
        pv = _ld(pv_ref)
        u = _gelu(pu)
        vl, xhat, rstd = _ln_fwd(_gelu(pv), lg_ref[...], lb_ref[...])
        vl3 = vl.reshape(nc, CHUNK, MIX_W)
        sp = _sgu_apply(ws_ref, vl3, False) + sb_ref[...][None]
        dy = _ld(dy_ref)
        dp_ref[:, 0:MIX_W] = _bf(dy * sp.reshape(s, MIX_W) * _gelu_grad(pu))
        dsp3 = (dy * u).reshape(nc, CHUNK, MIX_W)
        dsb_full = jnp.sum(dsp3, axis=0)
        masks = _head_masks()
        tril = _tril_mask()
        dspb = _bf(dsp3)
        vlb = _bf(vl3)
        for h in range(N_HEADS):
            dsb_ref[:, h:h + 1] += jnp.sum(jnp.where(masks[h], dsb_full, 0.0), axis=1, keepdims=True)
            dm = jnp.where(masks[h][None], dspb, jnp.zeros_like(dspb))
            g3 = lax.dot_general(dm, vlb, (((2,), (2,)), ((0,), (0,))), preferred_element_type=_F32)
            dws_ref[h] += jnp.where(tril, jnp.sum(g3, axis=0), 0.0)
        dvl = _sgu_apply(ws_ref, dsp3, True).reshape(s, MIX_W)
        dlg_ref[...] += jnp.sum(dvl * xhat, axis=0, keepdims=True)
        dlb_ref[...] += jnp.sum(dvl, axis=0, keepdims=True)
        dp_ref[:, MIX_W:2 * MIX_W] = _bf(_ln_bwd(dvl, xhat, rstd, lg_ref[...]) * _gelu_grad(pv))

    return pl.pallas_call(
        body, name=name, grid=(n_ex,),
        in_specs=[_pcol(s, 5), _pcol(s, 6), _vec(), _vec(),
                  pl.BlockSpec((N_HEADS, CHUNK, CHUNK), lambda e: (0, 0, 0)), pl.BlockSpec((CHUNK, MIX_W), lambda e: (0, 0)),
                  _pcol(s, 2)],
        out_specs=[pl.BlockSpec((s, 2 * MIX_W), lambda e: (e, 0)), _vec(), _vec(),
                   pl.BlockSpec((N_HEADS, CHUNK, CHUNK), lambda e: (0, 0, 0)), pl.BlockSpec((CHUNK, N_HEADS), lambda e: (0, 0))],
        out_shape=[jax.ShapeDtypeStruct((t, 2 * MIX_W), _BF), jax.ShapeDtypeStruct((1, MIX_W), _F32),
                   jax.ShapeDtypeStruct((1, MIX_W), _F32), jax.ShapeDtypeStruct((N_HEADS, CHUNK, CHUNK), _F32),
                   jax.ShapeDtypeStruct((CHUNK, N_HEADS), _F32)],
        compiler_params=_params("arbitrary"),
    )(p, p, lg, lb, ws, sb_full, dmix)


D_QBLOCK = 256
HEAD_COLS = N_HEADS * KV_BLOCK


def _stack_heads(x3):
    return jnp.stack([_bf(jnp.where(m[None], x3, 0.0)) for m in _head_masks()], axis=1)


def _stack_heads_rows(x):
    return jnp.concatenate([_bf(jnp.where(m, x, 0.0)) for m in _head_masks()], axis=0)


def _cols_to_rows(x):
    return jnp.concatenate([x[:, h * KV_BLOCK:(h + 1) * KV_BLOCK] for h in range(N_HEADS)], axis=0)


def _head_sums(x):
    return [jnp.sum(x[:, h * KV_BLOCK:(h + 1) * KV_BLOCK], axis=1, keepdims=True) for h in range(N_HEADS)]


def _spread(cols):
    tq = cols[0].shape[0]
    return jnp.concatenate([jnp.broadcast_to(c, (tq, KV_BLOCK)) for c in cols], axis=1)


def _pair_dot(x, m2):
    half = 2 * KV_BLOCK
    xb = _bf(x)
    return jnp.concatenate([_dot(xb[:, :half], m2), _dot(xb[:, half:], m2)], axis=1)


def _tri2(lower):
    n = 2 * KV_BLOCK
    r = lax.broadcasted_iota(jnp.int32, (n, n), 0)
    c = lax.broadcasted_iota(jnp.int32, (n, n), 1)
    same = (r >= KV_BLOCK) == (c >= KV_BLOCK)
    return _bf(jnp.where(same & (r > c if lower else r < c), 1.0, 0.0))


def _sb_scores(qs, kc, j, t_idx):
    z = _dot_nt(qs, kc)
    lane = lax.broadcasted_iota(jnp.int32, (1, HEAD_COLS), 1)
    valid = (j * KV_BLOCK + (lane & (KV_BLOCK - 1))) < t_idx
    lb = jnp.minimum(z, 0.0) - jnp.log(1.0 + jnp.exp(-jnp.abs(z)))
    c = jnp.where(valid, lb - z, 0.0)
    return valid, lb, c


RUN_LANES = 128


def _run_lane(j, h):
    return lax.broadcasted_iota(jnp.int32, (1, RUN_LANES), 1) == j * N_HEADS + h


def _d_qblock(s):
    return D_QBLOCK if s % D_QBLOCK == 0 else KV_BLOCK


def _mix_d_fwd(p, n_ex, name):
    t = p.shape[0]
    s = t // n_ex
    tq = _d_qblock(s)
    nq = s // tq
    r = tq // KV_BLOCK
    nb = s // KV_BLOCK
    assert nb * N_HEADS <= RUN_LANES

    def body(q_ref, k_ref, v_ref, y_ref, runs_ref, kc, vc):
        i = pl.program_id(1)

        @pl.when(i == 0)
        def _():
            kc[...] = _stack_heads(k_ref[...].reshape(nb, KV_BLOCK, MIX_W))
            vc[...] = _stack_heads(v_ref[...].reshape(nb, KV_BLOCK, MIX_W))

        qs = _bf(_ld(q_ref) * (HEAD_DIM ** -0.5))
        t_idx = i * tq + lax.broadcasted_iota(jnp.int32, (tq, 1), 0)
        after_m = _tri2(True)
        nkb = (i + 1) * r

        runs_ref[...] = jnp.zeros_like(runs_ref)

        def one_block(j, runs, acc):
            valid, lb, c = _sb_scores(qs, kc[j].reshape(HEAD_COLS, MIX_W), j, t_idx)
            a = jnp.where(valid, jnp.exp(lb + _pair_dot(c, after_m) + _spread(runs)), 0.0)
            acc = acc + _dot(_bf(a), vc[j].reshape(HEAD_COLS, MIX_W))
            kept = runs_ref[...]
            for h in range(N_HEADS):
                kept = jnp.where(_run_lane(j, h), runs[h], kept)
            runs_ref[...] = kept
            return tuple(ru + cs for ru, cs in zip(runs, _head_sums(c))), acc

        def step(trip, carry):
            runs, acc = carry
            for sub in range(r):
                runs, acc = one_block(nkb - 1 - trip * r - sub, runs, acc)
            return runs, acc

        zero = jnp.zeros((tq, 1), _F32)
        _, acc = lax.fori_loop(0, i + 1, step, ((zero,) * N_HEADS, jnp.zeros((tq, MIX_W), _F32)))
        y_ref[...] = _bf(acc)

    return pl.pallas_call(
        body, name=name, grid=(n_ex, nq),
        in_specs=[pl.BlockSpec((tq, MIX_W), lambda e, i: (e * nq + i, 7)), pl.BlockSpec((s, MIX_W), lambda e, i: (e, 8)),
                  pl.BlockSpec((s, MIX_W), lambda e, i: (e, 9))],
        out_specs=[pl.BlockSpec((tq, MIX_W), lambda e, i: (e * nq + i, 0)),
                   pl.BlockSpec((tq, RUN_LANES), lambda e, i: (e * nq + i, 0))],
        out_shape=[jax.ShapeDtypeStruct((t, MIX_W), _BF), jax.ShapeDtypeStruct((t, RUN_LANES), _F32)],
        scratch_shapes=[pltpu.VMEM((nb, N_HEADS, KV_BLOCK, MIX_W), _BF), pltpu.VMEM((nb, N_HEADS, KV_BLOCK, MIX_W), _BF)],
        compiler_params=_params("parallel", "arbitrary"),
    )(p, p, p)


def _mix_d_bwd(p, kept_runs, dmix, n_ex, name):
    t = p.shape[0]
    s = t // n_ex
    tq = _d_qblock(s)
    nq = s // tq
    r = tq // KV_BLOCK
    nb = s // KV_BLOCK
    scale = HEAD_DIM ** -0.5

    def body(q_ref, k_ref, v_ref, runs_ref, do_ref, dq_ref, dk_ref, dv_ref, kc, vc):
        i = pl.program_id(1)

        @pl.when(i == 0)
        def _():
            kc[...] = _stack_heads(k_ref[...].reshape(nb, KV_BLOCK, MIX_W))
            vc[...] = _stack_heads(v_ref[...].reshape(nb, KV_BLOCK, MIX_W))
            dk_ref[...] = jnp.zeros_like(dk_ref)
            dv_ref[...] = jnp.zeros_like(dv_ref)

        q_scaled = _ld(q_ref) * scale
        qs = _bf(q_scaled)
        do = do_ref[...]
        dob = _bf(do)
        q_rows = _stack_heads_rows(q_scaled)
        do_rows = _stack_heads_rows(do)
        kept = runs_ref[...]
        t_idx = i * tq + lax.broadcasted_iota(jnp.int32, (tq, 1), 0)
        after_m = _tri2(True)
        before_m = _tri2(False)
        nkb = (i + 1) * r
        zero = jnp.zeros((tq, 1), _F32)

        def step(trip, carry):
            for sub in range(r):
                carry = one_block(trip * r + sub, carry)
            return carry

        def one_block(j, carry):
            pres, dq = carry
            rows = pl.ds(pl.multiple_of(j * KV_BLOCK, KV_BLOCK), KV_BLOCK)
            kj = kc[j].reshape(HEAD_COLS, MIX_W)
            valid, lb, c = _sb_scores(qs, kj, j, t_idx)
            runs = [jnp.sum(jnp.where(_run_lane(j, h), kept, 0.0), axis=1, keepdims=True) for h in range(N_HEADS)]
            a = jnp.where(valid, jnp.exp(lb + _pair_dot(c, after_m) + _spread(runs)), 0.0)
            g = a * _dot_nt(dob, vc[j].reshape(HEAD_COLS, MIX_W))
            before = _pair_dot(g, before_m) + _spread(pres)
            sig = jnp.exp(lb)
            dz = _bf(jnp.where(valid, g * (1.0 - sig) - sig * before, 0.0))
            dk_ref[rows, :] += _dot_tn(_cols_to_rows(dz), q_rows)
            dv_ref[rows, :] += _dot_tn(_cols_to_rows(_bf(a)), do_rows)
            return tuple(pr + gs for pr, gs in zip(pres, _head_sums(g))), dq + _dot(dz, kj)

        _, dq = lax.fori_loop(0, i + 1, step, ((zero,) * N_HEADS, jnp.zeros((tq, MIX_W), _F32)))
        dq_ref[...] = _bf(dq * scale)

    return pl.pallas_call(
        body, name=name, grid=(n_ex, nq),
        in_specs=[pl.BlockSpec((tq, MIX_W), lambda e, i: (e * nq + i, 7)), pl.BlockSpec((s, MIX_W), lambda e, i: (e, 8)),
                  pl.BlockSpec((s, MIX_W), lambda e, i: (e, 9)), pl.BlockSpec((tq, RUN_LANES), lambda e, i: (e * nq + i, 0)),
                  pl.BlockSpec((tq, MIX_W), lambda e, i: (e * nq + i, 3))],
        out_specs=[pl.BlockSpec((tq, MIX_W), lambda e, i: (e * nq + i, 0)), pl.BlockSpec((s, MIX_W), lambda e, i: (e, 0)),
                   pl.BlockSpec((s, MIX_W), lambda e, i: (e, 0))],
        out_shape=[jax.ShapeDtypeStruct((t, MIX_W), _BF), jax.ShapeDtypeStruct((t, MIX_W), _F32),
                   jax.ShapeDtypeStruct((t, MIX_W), _F32)],
        scratch_shapes=[pltpu.VMEM((nb, N_HEADS, KV_BLOCK, MIX_W), _BF), pltpu.VMEM((nb, N_HEADS, KV_BLOCK, MIX_W), _BF)],
        compiler_params=_params("parallel", "arbitrary"),
    )(p, p, p, kept_runs, dmix)


def _fwd_mix(x, w, l, n_ex):
    p, h1 = _norm_mm(x, w["norm1_g"][l], w["w_in_t"][l], "in_proj")
    y_a = _mix_a_fwd(p, w["conv_a_w"][l], n_ex, "mix_a_fwd")
    y_b, cb = _mix_b_fwd(p, w["conv_b_w"][l], w["conv_b_b"][l], w["ln_b_g"][l], w["ln_b_b"][l], n_ex, "mix_b_fwd")
    y_c = _mix_c_fwd(p, w["ln_c_g"][l], w["ln_c_b"][l], w["sgu_w"][l], w["sgu_b_full"][l], n_ex, "mix_c_fwd")
    y_d, runs_d = _mix_d_fwd(p, n_ex, "mix_d_fwd")
    return dict(x=x, h1=h1, p=p, cb=cb, runs_d=runs_d, mix=(y_a, y_b, y_c, y_d))


def _fwd_ffn(st, w, l, n_ex):
    x1 = _mm_res(st["mix"], w["w_out"][l], st["x"], "out_proj")
    up_pre, h2 = _norm_mm(x1, w["norm2_g"][l], w["w_up_t"][l], "up_proj")
    act = _ffn_mid_fwd(up_pre, w["conv_f_w"][l], n_ex, "ffn_mid_fwd")
    st.update(x1=x1, h2=h2, up_pre=up_pre, act=act)
    return _mm_res((act,), w["w_down"][l], x1, "down_proj")


def _bwd_ffn(st, w, l, dx, dxb, n_ex):
    g = {}
    dact = _mm_nt(dxb, w["w_down"][l], "down_proj_dx")
    g["w_down"] = _mm_tn(st["act"], dxb, "down_proj_dw", _BF)
    dup_g, dup_v, dwf_g, dwf_v = _ffn_mid_bwd(st["up_pre"], w["conv_f_w"][l], dact, n_ex, "ffn_mid_bwd")
    g["conv_f_w"] = jnp.concatenate([dwf_g, dwf_v], axis=1)
    dx, dxb, g["norm2_g"] = _mm_normbwd((dup_g, dup_v), w["w_up_t"][l], st["x1"], w["norm2_g"][l], dx, "up_proj_dx")
    g["w_up_t"] = jnp.concatenate([_mm_tn(part, st["h2"], "up_proj_dw", _BF) for part in (dup_g, dup_v)], axis=0)
    return dx, dxb, g


def _bwd_out_proj(st, w, l, dxb):
    return _mm_nt(dxb, w["w_out"][l], "out_proj_dx"), _mm_tn_parts(st["mix"], dxb, "out_proj_dw")


def _bwd_mixers(st, w, l, dx, dmix, n_ex):
    g = {}
    p = st["p"]
    dp_a, g["conv_a_w"] = _mix_a_bwd(p, w["conv_a_w"][l], dmix, n_ex, "mix_a_bwd")
    dp_b, g["conv_b_w"], g["conv_b_b"], g["ln_b_g"], g["ln_b_b"] = _mix_b_bwd(
        p, st["cb"], w["conv_b_w"][l], w["ln_b_g"][l], w["ln_b_b"][l], dmix, n_ex, "mix_b_bwd")
    dp_c, g["ln_c_g"], g["ln_c_b"], g["sgu_w"], g["sgu_b_t"] = _mix_c_bwd(
        p, w["ln_c_g"][l], w["ln_c_b"][l], w["sgu_w"][l], w["sgu_b_full"][l], dmix, n_ex, "mix_c_bwd")
    dq, dk, dv = _mix_d_bwd(p, st["runs_d"], dmix, n_ex, "mix_d_bwd")
    dp = (dp_a, dp_b, dp_c, dq, dk, dv)
    dx, dxb, g["norm1_g"] = _mm_normbwd(dp, w["w_in_t"][l], st["x"], w["norm1_g"][l], dx, "in_proj_dx")
    g["w_in_t"] = _mm_tn_parts(dp, st["h1"], "in_proj_dw")
    return dx, dxb, g


def _bwd_mix(st, w, l, dx, dxb, n_ex):
    dmix, dw_out = _bwd_out_proj(st, w, l, dxb)
    dx, dxb, g = _bwd_mixers(st, w, l, dx, dmix, n_ex)
    g["w_out"] = dw_out
    return dx, dxb, g


def _local_fwd_bwd(x, target, w, n_ex):
    depth = len(w["w_in_t"])
    saved = []
    for l in range(depth):
        st = _fwd_mix(x, w, l, n_ex)
        x = _fwd_ffn(st, w, l, n_ex)
        saved.append(st)
    dx, dxb, d_final_g, loss = _final_loss(x, w["final_g"], target, "final_loss")
    grads = {}
    for l in reversed(range(depth)):
        dx, dxb, g_ffn = _bwd_ffn(saved[l], w, l, dx, dxb, n_ex)
        dx, dxb, g_mix = _bwd_mix(saved[l], w, l, dx, dxb, n_ex)
        for k, v in {**g_ffn, **g_mix}.items():
            grads.setdefault(k, [None] * depth)[l] = v
    grads["final_g"] = d_final_g
    return loss, dx, grads


_MESH = pl.DeviceIdType.MESH
_ANY = pl.BlockSpec(memory_space=pl.ANY)


def _position():
    return lax.axis_index("x"), lax.axis_index("y"), lax.axis_index("c")


def _flat(px, py, pc):
    return 4 * px + 2 * py + pc


def _all_gather(shard, name, after):
    r, c_ = shard.shape

    def body(x_ref, after_ref, out_ref, send_sems, recv_sems, local_sem):
        x, y, c = _position()
        me, sibling = (x, y, c), (x, y, 1 - c)
        chips = [(1 - x, y), (x, 1 - y), (1 - x, 1 - y)]

        def copy(k, block, to, src=None):
            slab = out_ref.at[_flat(*block)]
            return pltpu.make_async_remote_copy(
                src_ref=slab if src is None else src, dst_ref=slab, send_sem=send_sems.at[k], recv_sem=recv_sems.at[k],
                device_id=to, device_id_type=_MESH)

        mine = pltpu.make_async_copy(x_ref, out_ref.at[_flat(*me)], local_sem)
        mine.start()
        first = [copy(0, me, sibling, src=x_ref)]
        first += [copy(1 + j, me, (*chip, c), src=x_ref) for j, chip in enumerate(chips)]
        for cp in first:
            cp.start()
        passed = [copy(4 + j, (*chip, c), sibling) for j, chip in enumerate(chips)]
        for j, chip in enumerate(chips):
            copy(1 + j, (*chip, c), me).wait_recv()
            passed[j].start()
        copy(0, sibling, me).wait_recv()
        for j, chip in enumerate(chips):
            copy(4 + j, (*chip, 1 - c), me).wait_recv()
        for cp in first + passed:
            cp.wait_send()
        mine.wait()

    return pl.pallas_call(
        body, name=name, out_shape=jax.ShapeDtypeStruct((N_DEV, r, c_), shard.dtype),
        in_specs=[_ANY, _ANY], out_specs=_ANY,
        scratch_shapes=[pltpu.SemaphoreType.DMA((7,)), pltpu.SemaphoreType.DMA((7,)), pltpu.SemaphoreType.DMA],
    )(shard, after)


_HBM = pl.BlockSpec(memory_space=pltpu.HBM)
_SEM = pl.BlockSpec(memory_space=pltpu.SEMAPHORE)
_DATAFLOW = pltpu.SideEffectType.DATAFLOW_SIDE_EFFECTING


def _peers(x, y, c):
    return [((1 - x) if (k + 1) & 4 else x, (1 - y) if (k + 1) & 2 else y, (1 - c) if (k + 1) & 1 else c)
            for k in range(N_DEV - 1)]


def _direct_copies(src_ref, land_ref, send_sems, recv_sems, to_all):
    x, y, c = _position()
    my = _flat(x, y, c)
    out, back = [], []
    for k, peer in enumerate(_peers(x, y, c)):
        src = src_ref if to_all else src_ref.at[_flat(*peer)]
        sems = dict(send_sem=send_sems.at[k], recv_sem=recv_sems.at[k], device_id=peer, device_id_type=_MESH)
        out.append(pltpu.make_async_remote_copy(src_ref=src, dst_ref=land_ref.at[my], **sems))
        back.append(pltpu.make_async_remote_copy(src_ref=src, dst_ref=land_ref.at[_flat(*peer)], **sems))
    return out, back


def _exchange_start(src, to_all, after, name):
    r, c_ = src.shape[-2:]

    def body(src_ref, land_ref, after_ref, send_sems, recv_sems, src_thru, land_thru, token):
        for cp in _direct_copies(src_ref, land_ref, send_sems, recv_sems, to_all)[0]:
            cp.start()
        token[...] = jnp.zeros_like(token)

    land = pltpu.with_memory_space_constraint(lax.empty((N_DEV, r, c_), src.dtype), pltpu.HBM)
    send_sems, recv_sems, src_thru, land_thru, token = pl.pallas_call(
        body, name=name,
        out_shape=(pltpu.SemaphoreType.DMA((N_DEV - 1,)), pltpu.SemaphoreType.DMA((N_DEV - 1,)),
                   pltpu.HBM(src.shape, src.dtype), pltpu.HBM((N_DEV, r, c_), src.dtype), jax.ShapeDtypeStruct((8, 128), _F32)),
        in_specs=(_HBM, _HBM, _ANY), out_specs=(_SEM, _SEM, _HBM, _HBM, pl.BlockSpec(memory_space=pltpu.VMEM)),
        input_output_aliases={0: 2, 1: 3},
        compiler_params=pltpu.CompilerParams(has_side_effects=_DATAFLOW),
    )(pltpu.with_memory_space_constraint(src, pltpu.HBM), land, after)
    return (send_sems, recv_sems, src_thru, land_thru, to_all), token


def _exchange_wait(handle, after, name):
    send_sems, recv_sems, src_thru, land_thru, to_all = handle

    def body(src_ref, land_ref, send_sems, recv_sems, after_ref, src_dead, got_ref):
        out, back = _direct_copies(src_ref, land_ref, send_sems, recv_sems, to_all)
        for cp in out:
            cp.wait_send()
        for cp in back:
            cp.wait_recv()

    return pl.pallas_call(
        body, name=name,
        out_shape=(pltpu.HBM(src_thru.shape, src_thru.dtype), pltpu.HBM(land_thru.shape, land_thru.dtype)),
        in_specs=(_HBM, _HBM, _SEM, _SEM, _ANY), out_specs=(_HBM, _HBM), input_output_aliases={0: 0, 1: 1},
        compiler_params=pltpu.CompilerParams(has_side_effects=_DATAFLOW),
    )(src_thru, land_thru, send_sems, recv_sems, after)


def _with_own(landed, own):
    my = _flat(*_position())
    return lax.dynamic_update_slice(landed, own[None], (my, 0, 0))


def _sum_slabs(slabs, name):
    n, r, c_ = slabs.shape
    tr = _pick_tile(r, 16, max(16, (4 << 20) // (n * c_ * slabs.dtype.itemsize)))

    def body(x_ref, o_ref):
        acc = x_ref[0].astype(_F32)
        for k in range(1, n):
            acc = acc + x_ref[k].astype(_F32)
        o_ref[...] = acc

    return pl.pallas_call(
        body, name=name, grid=(r // tr,),
        in_specs=[pl.BlockSpec((n, tr, c_), lambda i: (0, i, 0))],
        out_specs=pl.BlockSpec((tr, c_), lambda i: (i, 0)),
        out_shape=jax.ShapeDtypeStruct((r, c_), _F32),
        compiler_params=_params("parallel"),
    )(slabs)


def _adamw(w, g, m, v, name):
    r, c_ = w.shape
    tr = _pick_tile(r, 8, 512)

    def body(w_ref, g_ref, m_ref, v_ref, d_ref, nm_ref, nv_ref):
        gv = g_ref[...]
        nm = ADAM_B1 * m_ref[...] + (1.0 - ADAM_B1) * gv
        nv = ADAM_B2 * v_ref[...] + (1.0 - ADAM_B2) * (gv * gv)
        m_hat = nm / (1.0 - ADAM_B1 ** ADAM_STEP)
        v_hat = nv / (1.0 - ADAM_B2 ** ADAM_STEP)
        d_ref[...] = -ADAM_LR * (m_hat / (jnp.sqrt(v_hat) + ADAM_EPS) + ADAM_WD * w_ref[...])
        nm_ref[...] = nm
        nv_ref[...] = nv

    spec = pl.BlockSpec((tr, c_), lambda i: (i, 0))
    shape = jax.ShapeDtypeStruct((r, c_), _F32)
    return pl.pallas_call(
        body, name=name, grid=(r // tr,), in_specs=[spec] * 4, out_specs=[spec] * 3, out_shape=[shape] * 3,
        compiler_params=_params("parallel"),
    )(w, g, m, v)


_SMALL = ("norm1_g", "conv_a_w", "conv_b_w", "conv_b_b", "ln_b_g", "ln_b_b", "ln_c_g", "ln_c_b", "sgu_w", "sgu_b",
          "norm2_g", "conv_f_w", "final_g")
_CONV_SHARDED = ("conv_a_w", "conv_b_w", "conv_f_w")
_NAMES = ("norm1_g", "w_in", "conv_a_w", "conv_b_w", "conv_b_b", "ln_b_g", "ln_b_b", "ln_c_g", "ln_c_b", "sgu_w", "sgu_b",
          "w_out", "norm2_g", "w_up", "conv_f_w", "w_down", "final_g")


def _pack_rows(parts, lanes=128, row_multiple=8):
    flat = jnp.concatenate([a.reshape(-1) for a in parts])
    rows = -(-flat.shape[0] // lanes)
    rows = -(-rows // row_multiple) * row_multiple
    return jnp.pad(flat, (0, rows * lanes - flat.shape[0])).reshape(rows, lanes)


def _unpack_rows(packed, shapes):
    flat = packed.reshape(-1)
    out, off = [], 0
    for shp in shapes:
        size = 1
        for s in shp:
            size *= s
        out.append(flat[off:off + size].reshape(shp))
        off += size
    return out


def _gather_conv_weights(conv_a_w, conv_b_w, conv_f_w, after):
    shards = (conv_a_w, conv_b_w, conv_f_w)
    gathered = _all_gather(_pack_rows(shards), "gather_conv_weights", after)
    full = []
    per_dev = [_unpack_rows(gathered[d], [s.shape for s in shards]) for d in range(N_DEV)]
    for i in range(len(shards)):
        full.append(jnp.concatenate([per_dev[d][i] for d in range(N_DEV)], axis=-1))
    return full


def kernel(x, norm1_g, w_in, conv_a_w, conv_b_w, conv_b_b, ln_b_g, ln_b_b, ln_c_g, ln_c_b, sgu_w, sgu_b, w_out, norm2_g, w_up, conv_f_w, w_down, final_g, loss_target, m_norm1_g, m_w_in, m_conv_a_w, m_conv_b_w, m_conv_b_b, m_ln_b_g, m_ln_b_b, m_ln_c_g, m_ln_c_b, m_sgu_w, m_sgu_b, m_w_out, m_norm2_g, m_w_up, m_conv_f_w, m_w_down, m_final_g, v_norm1_g, v_w_in, v_conv_a_w, v_conv_b_w, v_conv_b_b, v_ln_b_g, v_ln_b_b, v_ln_c_g, v_ln_c_b, v_sgu_w, v_sgu_b, v_w_out, v_norm2_g, v_w_up, v_conv_f_w, v_w_down, v_final_g):
    weights = dict(norm1_g=norm1_g, w_in=w_in, conv_a_w=conv_a_w, conv_b_w=conv_b_w, conv_b_b=conv_b_b, ln_b_g=ln_b_g,
                   ln_b_b=ln_b_b, ln_c_g=ln_c_g, ln_c_b=ln_c_b, sgu_w=sgu_w, sgu_b=sgu_b, w_out=w_out, norm2_g=norm2_g,
                   w_up=w_up, conv_f_w=conv_f_w, w_down=w_down, final_g=final_g)
    mom1 = dict(norm1_g=m_norm1_g, w_in=m_w_in, conv_a_w=m_conv_a_w, conv_b_w=m_conv_b_w, conv_b_b=m_conv_b_b,
                ln_b_g=m_ln_b_g, ln_b_b=m_ln_b_b, ln_c_g=m_ln_c_g, ln_c_b=m_ln_c_b, sgu_w=m_sgu_w, sgu_b=m_sgu_b,
                w_out=m_w_out, norm2_g=m_norm2_g, w_up=m_w_up, conv_f_w=m_conv_f_w, w_down=m_w_down, final_g=m_final_g)
    mom2 = dict(norm1_g=v_norm1_g, w_in=v_w_in, conv_a_w=v_conv_a_w, conv_b_w=v_conv_b_w, conv_b_b=v_conv_b_b,
                ln_b_g=v_ln_b_g, ln_b_b=v_ln_b_b, ln_c_g=v_ln_c_g, ln_c_b=v_ln_c_b, sgu_w=v_sgu_w, sgu_b=v_sgu_b,
                w_out=v_w_out, norm2_g=v_norm2_g, w_up=v_w_up, conv_f_w=v_conv_f_w, w_down=v_w_down, final_g=v_final_g)
    n_ex, seq, d = x.shape
    depth = w_in.shape[0]
    assert depth == 2
    my = _flat(*_position())
    row = lambda a, l: a[l][None]
    tied = lambda a, token: a + token[0:1, 0:1]

    slab = {"w_in": [_bf(jnp.swapaxes(w_in, 1, 2)[l]) for l in range(depth)], "w_out": [_bf(w_out[l]) for l in range(depth)],
            "w_up": [_bf(jnp.swapaxes(w_up, 1, 2)[l]) for l in range(depth)], "w_down": [_bf(w_down[l]) for l in range(depth)]}
    rows = {name: parts[0].shape[0] for name, parts in slab.items()}
    key_of = {"w_in": "w_in_t", "w_out": "w_out", "w_up": "w_up_t", "w_down": "w_down"}
    rest_layer0 = [("w_out", 0), ("w_up", 0), ("w_down", 0)]
    all_layer1 = [("w_in", 1), ("w_out", 1), ("w_up", 1), ("w_down", 1)]

    def split_rows(a, which, merge):
        out, off = {}, 0
        for name, l in which:
            part = a[..., off:off + rows[name], :]
            out[(name, l)] = part.reshape(N_DEV * rows[name], d) if merge else part
            off += rows[name]
        return out

    send_w0 = jnp.concatenate([slab[n][l] for n, l in rest_layer0], axis=0)
    send_w1 = jnp.concatenate([slab[n][l] for n, l in all_layer1], axis=0)
    w_in0 = _all_gather(slab["w_in"][0], "gather_w_in0", norm1_g)
    conv_a_full, conv_b_full, conv_f_full = _gather_conv_weights(conv_a_w, conv_b_w, conv_f_w, w_in0)
    gather0, token = _exchange_start(send_w0, True, conv_f_full, "gather_layer0_start")
    w = {
        "norm1_g": [row(norm1_g, l) for l in range(depth)], "w_in_t": [None] * depth,
        "conv_a_w": [conv_a_full[l] for l in range(depth)], "conv_b_w": [conv_b_full[l] for l in range(depth)],
        "conv_b_b": [row(conv_b_b, l) for l in range(depth)], "ln_b_g": [row(ln_b_g, l) for l in range(depth)],
        "ln_b_b": [row(ln_b_b, l) for l in range(depth)], "ln_c_g": [row(ln_c_g, l) for l in range(depth)],
        "ln_c_b": [row(ln_c_b, l) for l in range(depth)], "sgu_w": [sgu_w[l] for l in range(depth)],
        "sgu_b_full": [jnp.repeat(sgu_b[l].T, HEAD_DIM, axis=1) for l in range(depth)],
        "w_out": [None] * depth, "norm2_g": [row(norm2_g, l) for l in range(depth)], "w_up_t": [None] * depth,
        "conv_f_w": [conv_f_full[l] for l in range(depth)], "w_down": [None] * depth, "final_g": final_g[None],
    }
    w["w_in_t"][0] = w_in0.reshape(N_DEV * rows["w_in"], d)
    w["norm1_g"][0] = tied(row(norm1_g, 0), token)

    def land_weights(handle, after, which, name):
        own, landed = _exchange_wait(handle, after, name)
        for (n, l), mat in split_rows(_with_own(landed, own), which, True).items():
            w[key_of[n]][l] = mat
        return landed

    st0 = _fwd_mix(x.reshape(n_ex * seq, d), w, 0, n_ex)
    landed0 = land_weights(gather0, st0["mix"][3], rest_layer0, "gather_layer0_wait")
    gather1, token = _exchange_start(send_w1, True, landed0, "gather_layer1_start")
    w["norm2_g"][0] = tied(row(norm2_g, 0), token)
    x_mid = _fwd_ffn(st0, w, 0, n_ex)
    land_weights(gather1, x_mid, all_layer1, "gather_layer1_wait")
    st1 = _fwd_mix(x_mid, w, 1, n_ex)
    x_out = _fwd_ffn(st1, w, 1, n_ex)
    dx, dxb, d_final_g, loss = _final_loss(x_out, w["final_g"], loss_target.reshape(n_ex * seq, d), "final_loss")
    loss = lax.psum(loss[0, 0], ("x", "y", "c"))

    def send_grads(g, which, after, name):
        slabs = jnp.concatenate([g[key_of[n]].reshape(N_DEV, rows[n], d) for n, _ in which], axis=1)
        return _exchange_start(slabs, False, after, name)

    dx, dxb, g_ffn1 = _bwd_ffn(st1, w, 1, dx, dxb, n_ex)
    dx, dxb, g_mix1 = _bwd_mix(st1, w, 1, dx, dxb, n_ex)
    grads1, token = send_grads({**g_ffn1, **g_mix1}, all_layer1, dx, "exchange_layer1_start")
    w["norm2_g"][0] = tied(row(norm2_g, 0), token)
    dx, dxb, g_ffn0 = _bwd_ffn(st0, w, 0, dx, dxb, n_ex)
    dmix, g_ffn0["w_out"] = _bwd_out_proj(st0, w, 0, dxb)
    ffn_layer0 = [("w_out", 0), ("w_up", 0), ("w_down", 0)]
    grads0a, token = send_grads(g_ffn0, ffn_layer0, dmix, "exchange_ffn0_start")
    w["conv_a_w"][0] = tied(w["conv_a_w"][0], token)
    dx, dxb, g_mix0 = _bwd_mixers(st0, w, 0, dx, dmix, n_ex)
    mix_layer0 = [("w_in", 0)]
    grads0b, _ = send_grads(g_mix0, mix_layer0, dx, "exchange_mix0_start")
    grad_x = dx.reshape(n_ex, seq, d)
    g = {k: [{**g_ffn0, **g_mix0}[k], {**g_ffn1, **g_mix1}[k]] for k in g_mix0.keys() | g_ffn0.keys()}
    g["final_g"] = d_final_g

    reduced = {}

    def land_grads(handle, after, which, name):
        sent, landed = _exchange_wait(handle, after, name + "_wait")
        own = lax.dynamic_index_in_dim(sent, my, 0, keepdims=False)
        total = _sum_slabs(_with_own(landed, own), name + "_sum")
        reduced.update(split_rows(total, which, False))
        return total

    def stacked_grad(name):
        stacked = jnp.stack([reduced[(name, l)] for l in range(depth)])
        return jnp.swapaxes(stacked, 1, 2) if name in ("w_in", "w_up") else stacked

    done = land_grads(grads1, dx, all_layer1, "exchange_layer1")
    land_grads(grads0a, done, ffn_layer0, "exchange_ffn0")
    grads = {name: stacked_grad(name) for name in ("w_out", "w_up", "w_down")}

    small_local = {
        "norm1_g": jnp.stack([a[0] for a in g["norm1_g"]]), "conv_a_w": jnp.stack(g["conv_a_w"]),
        "conv_b_w": jnp.stack(g["conv_b_w"]), "conv_b_b": jnp.stack([a[0] for a in g["conv_b_b"]]),
        "ln_b_g": jnp.stack([a[0] for a in g["ln_b_g"]]), "ln_b_b": jnp.stack([a[0] for a in g["ln_b_b"]]),
        "ln_c_g": jnp.stack([a[0] for a in g["ln_c_g"]]), "ln_c_b": jnp.stack([a[0] for a in g["ln_c_b"]]),
        "sgu_w": jnp.stack(g["sgu_w"]), "sgu_b": jnp.stack([a.T for a in g["sgu_b_t"]]),
        "norm2_g": jnp.stack([a[0] for a in g["norm2_g"]]), "conv_f_w": jnp.stack(g["conv_f_w"]),
        "final_g": g["final_g"][0],
    }
    small_sum = _sum_slabs(_all_gather(_pack_rows([small_local[k] for k in _SMALL]), "gather_small_grads", dx),
                           "sum_small_grads")
    for name, total in zip(_SMALL, _unpack_rows(small_sum, [small_local[k].shape for k in _SMALL])):
        if name in _CONV_SHARDED:
            width = weights[name].shape[-1]
            total = lax.dynamic_slice_in_dim(total, my * width, width, axis=-1)
        grads[name] = total

    delta, new_m, new_v = {}, {}, {}
    for name in [n for n in _NAMES if n != "w_in"] + ["w_in"]:
        if name == "w_in":
            land_grads(grads0b, new_v["w_up"], mix_layer0, "exchange_mix0")
            grads["w_in"] = stacked_grad("w_in")
        shp = weights[name].shape
        two_d = (-1, shp[-1]) if len(shp) > 1 else (1, shp[0])
        outs = _adamw(weights[name].reshape(two_d), grads[name].reshape(two_d), mom1[name].reshape(two_d),
                      mom2[name].reshape(two_d), "adamw_" + name)
        delta[name], new_m[name], new_v[name] = (o.reshape(shp) for o in outs)

    return (loss, grad_x, *[grads[n] for n in _NAMES], *[delta[n] for n in _NAMES], *[new_m[n] for n in _NAMES],
            *[new_v[n] for n in _NAMES])
```

```python
import functools

import jax
import jax.numpy as jnp
from jax import lax
from jax.experimental import pallas as pl
from jax.experimental.pallas import tpu as pltpu

_F32 = jnp.float32
_BF = jnp.bfloat16

HEAD_DIM = 64
MIX_W = 256
N_HEADS = MIX_W // HEAD_DIM
CHUNK = 128
KV_BLOCK = 128
K_SHORT = 3
K_CONF = 31
K_FFN = 3
RMS_EPS = 1e-6
LN_EPS = 1e-5
ADAM_LR = 0.001
ADAM_B1 = 0.9
ADAM_B2 = 0.999
ADAM_EPS = 1e-08
ADAM_WD = 0.01
ADAM_STEP = 10
N_DEV = 8
VMEM_LIMIT = 56 * 1024 * 1024


def _bf(x):
    return x.astype(_BF)


def _ld(ref):
    return ref[...].astype(_F32)


_ANY_SPEC = pl.BlockSpec(memory_space=pl.ANY)


def _params(*sem):
    return pltpu.CompilerParams(dimension_semantics=sem, vmem_limit_bytes=VMEM_LIMIT)


def _dot(a, b):
    return jnp.dot(a, b, preferred_element_type=_F32)


def _dot_nt(a, b):
    return lax.dot_general(a, b, (((1,), (1,)), ((), ())), preferred_element_type=_F32)


def _dot_tn(a, b):
    return lax.dot_general(a, b, (((0,), (0,)), ((), ())), preferred_element_type=_F32)


def _row_tile(t, want):
    return want if t % want == 0 else t


def _pick_tile(rows, unit, max_rows):
    best = 0
    for cand in range(unit, min(rows, max_rows) + 1, unit):
        if rows % cand == 0:
            best = cand
    return best or rows


def _sigmoid(x):
    return 1.0 / (1.0 + jnp.exp(-x))


def _rms_rstd(x):
    return lax.rsqrt(jnp.mean(x * x, axis=-1, keepdims=True) + RMS_EPS)


def _norm_mm(x, g, w_t, name):
    t, d = x.shape
    n = w_t.shape[0]
    tm = _row_tile(t, 512)
    tn = _row_tile(n, 512)

    def body(x_ref, g_ref, w_ref, p_ref, h_ref):
        xv = x_ref[...]
        h = _bf(xv * _rms_rstd(xv) * g_ref[...])
        h_ref[...] = h
        for n0 in range(0, n, tn):
            p_ref[:, n0:n0 + tn] = _bf(_dot_nt(h, w_ref[n0:n0 + tn, :]))

    return pl.pallas_call(
        body, name=name, grid=(t // tm,),
        in_specs=[pl.BlockSpec((tm, d), lambda i: (i, 0)), pl.BlockSpec((1, d), lambda i: (0, 0)),
                  pl.BlockSpec((n, d), lambda i: (0, 0))],
        out_specs=[pl.BlockSpec((tm, n), lambda i: (i, 0)), pl.BlockSpec((tm, d), lambda i: (i, 0))],
        out_shape=[jax.ShapeDtypeStruct((t, n), _BF), jax.ShapeDtypeStruct((t, d), _BF)],
        compiler_params=_params("parallel"),
    )(x, g, w_t)


def _mm_nt(a, w_t, name, after=None):
    t, k = a.shape
    n = w_t.shape[0]
    tm = _row_tile(t, 512)
    tn = _row_tile(n, 512) if n % 512 == 0 else _row_tile(n, 256)

    def body(a_ref, w_ref, *rest):
        o_ref = rest[-1]
        av = a_ref[...]
        for n0 in range(0, n, tn):
            o_ref[:, n0:n0 + tn] = _bf(_dot_nt(av, w_ref[n0:n0 + tn, :]))

    extra = () if after is None else (after,)
    return pl.pallas_call(
        body, name=name, grid=(t // tm,),
        in_specs=[pl.BlockSpec((tm, k), lambda i: (i, 0)), pl.BlockSpec((n, k), lambda i: (0, 0))] + [_ANY_SPEC] * len(extra),
        out_specs=pl.BlockSpec((tm, n), lambda i: (i, 0)),
        out_shape=jax.ShapeDtypeStruct((t, n), _BF),
        compiler_params=_params("parallel"),
    )(a, w_t, *extra)


def _mm_res(parts, w, x, name):
    t = x.shape[0]
    k, d = w.shape
    tm = _row_tile(t, 512)
    widths = [a.shape[1] for a in parts]
    n_parts = len(parts)

    def body(*refs):
        w_ref, x_ref, o_ref = refs[n_parts:]
        acc, off = x_ref[...], 0
        for a_ref, width in zip(refs[:n_parts], widths):
            acc = acc + _dot(a_ref[...], w_ref[off:off + width, :])
            off += width
        o_ref[...] = acc

    return pl.pallas_call(
        body, name=name, grid=(t // tm,),
        in_specs=[pl.BlockSpec((tm, width), lambda i: (i, 0)) for width in widths] + [
            pl.BlockSpec((k, d), lambda i: (0, 0)), pl.BlockSpec((tm, d), lambda i: (i, 0))],
        out_specs=pl.BlockSpec((tm, d), lambda i: (i, 0)),
        out_shape=jax.ShapeDtypeStruct((t, d), _F32),
        compiler_params=_params("parallel"),
    )(*parts, w, x)


def _mm_normbwd(parts, w, x, g, dres, name):
    t = x.shape[0]
    k, d = w.shape
    tm = _row_tile(t, 256)
    widths = [a.shape[1] for a in parts]
    n_parts = len(parts)

    def body(*refs):
        a_refs = refs[:n_parts]
        w_ref, x_ref, g_ref, r_ref, dx_ref, dxb_ref, dg_ref = refs[n_parts:]
        dh, off = None, 0
        for a_ref, width in zip(a_refs, widths):
            term = _dot(_bf(a_ref[...]), w_ref[off:off + width, :])
            dh = term if dh is None else dh + term
            off += width
        xv = x_ref[...]
        rstd = _rms_rstd(xv)
        xn = xv * rstd
        u = dh * g_ref[...]
        dx = r_ref[...] + rstd * (u - xn * jnp.mean(u * xn, axis=-1, keepdims=True))
        dx_ref[...] = dx
        dxb_ref[...] = _bf(dx)

        @pl.when(pl.program_id(0) == 0)
        def _():
            dg_ref[...] = jnp.zeros_like(dg_ref)

        dg_ref[...] += jnp.sum(dh * xn, axis=0, keepdims=True)

    return pl.pallas_call(
        body, name=name, grid=(t // tm,),
        in_specs=[pl.BlockSpec((tm, width), lambda i: (i, 0)) for width in widths] + [
            pl.BlockSpec((k, d), lambda i: (0, 0)),
            pl.BlockSpec((tm, d), lambda i: (i, 0)), pl.BlockSpec((1, d), lambda i: (0, 0)),
            pl.BlockSpec((tm, d), lambda i: (i, 0))],
        out_specs=[pl.BlockSpec((tm, d), lambda i: (i, 0)), pl.BlockSpec((tm, d), lambda i: (i, 0)),
                   pl.BlockSpec((1, d), lambda i: (0, 0))],
        out_shape=[jax.ShapeDtypeStruct((t, d), _F32), jax.ShapeDtypeStruct((t, d), _BF),
                   jax.ShapeDtypeStruct((1, d), _F32)],
        compiler_params=_params("arbitrary"),
    )(*parts, w, x, g, dres)


def _mm_tn(a, b, name, out_dtype):
    t, m = a.shape
    n = b.shape[1]
    tm = _pick_tile(m, 128, 1408)
    tn = _pick_tile(n, 128, 1024)
    tk = _row_tile(t, 1024)
    nk = t // tk

    def body(a_ref, b_ref, o_ref, acc):
        kk = pl.program_id(2)

        @pl.when(kk == 0)
        def _():
            acc[...] = jnp.zeros_like(acc)

        acc[...] += _dot_tn(_bf(a_ref[...]), b_ref[...])

        @pl.when(kk == nk - 1)
        def _():
            o_ref[...] = acc[...].astype(o_ref.dtype)

    return pl.pallas_call(
        body, name=name, grid=(m // tm, n // tn, nk),
        in_specs=[pl.BlockSpec((tk, tm), lambda i, j, kk: (kk, i)), pl.BlockSpec((tk, tn), lambda i, j, kk: (kk, j))],
        out_specs=pl.BlockSpec((tm, tn), lambda i, j, kk: (i, j)),
        out_shape=jax.ShapeDtypeStruct((m, n), out_dtype),
        scratch_shapes=[pltpu.VMEM((tm, tn), _F32)],
        compiler_params=_params("parallel", "parallel", "arbitrary"),
    )(a, b)


def _mm_tn_parts(parts, b, name):
    t, n = b.shape
    widths = [a.shape[1] for a in parts]
    m = sum(widths)
    n_parts = len(parts)
    tk = _row_tile(t, 1024)
    nk = t // tk

    def body(*refs):
        b_ref, o_ref, acc = refs[n_parts:]
        kk = pl.program_id(0)

        @pl.when(kk == 0)
        def _():
            acc[...] = jnp.zeros_like(acc)

        bv = b_ref[...]
        off = 0
        for a_ref, width in zip(refs[:n_parts], widths):
            acc[off:off + width, :] += _dot_tn(_bf(a_ref[...]), bv)
            off += width

        @pl.when(kk == nk - 1)
        def _():
            o_ref[...] = _bf(acc[...])

    return pl.pallas_call(
        body, name=name, grid=(nk,),
        in_specs=[pl.BlockSpec((tk, width), lambda kk: (kk, 0)) for width in widths] + [pl.BlockSpec((tk, n), lambda kk: (kk, 0))],
        out_specs=pl.BlockSpec((m, n), lambda kk: (0, 0)),
        out_shape=jax.ShapeDtypeStruct((m, n), _BF),
        scratch_shapes=[pltpu.VMEM((m, n), _F32)],
        compiler_params=_params("arbitrary"),
    )(*parts, b)


def _final_loss(x, g, target, name):
    t, d = x.shape
    tm = _row_tile(t, 256)

    def body(x_ref, g_ref, t_ref, dx_ref, dxb_ref, dg_ref, loss_ref):
        xv = x_ref[...]
        rstd = _rms_rstd(xv)
        xn = xv * rstd
        err = xn * g_ref[...] - t_ref[...]
        dy = err * (1.0 / d)
        u = dy * g_ref[...]
        dx = rstd * (u - xn * jnp.mean(u * xn, axis=-1, keepdims=True))
        dx_ref[...] = dx
        dxb_ref[...] = _bf(dx)

        @pl.when(pl.program_id(0) == 0)
        def _():
            dg_ref[...] = jnp.zeros_like(dg_ref)
            loss_ref[...] = jnp.zeros_like(loss_ref)

        dg_ref[...] += jnp.sum(dy * xn, axis=0, keepdims=True)
        loss_ref[...] += (0.5 / d) * jnp.sum(jnp.sum(err * err, axis=1, keepdims=True), axis=0, keepdims=True)

    return pl.pallas_call(
        body, name=name, grid=(t // tm,),
        in_specs=[pl.BlockSpec((tm, d), lambda i: (i, 0)), pl.BlockSpec((1, d), lambda i: (0, 0)),
                  pl.BlockSpec((tm, d), lambda i: (i, 0))],
        out_specs=[pl.BlockSpec((tm, d), lambda i: (i, 0)), pl.BlockSpec((tm, d), lambda i: (i, 0)),
                   pl.BlockSpec((1, d), lambda i: (0, 0)), pl.BlockSpec((1, 1), lambda i: (0, 0))],
        out_shape=[jax.ShapeDtypeStruct((t, d), _F32), jax.ShapeDtypeStruct((t, d), _BF),
                   jax.ShapeDtypeStruct((1, d), _F32), jax.ShapeDtypeStruct((1, 1), _F32)],
        compiler_params=_params("arbitrary"),
    )(x, g, target)


def _pad_rows(x, pad):
    return jnp.concatenate([x, jnp.zeros((pad, x.shape[1]), x.dtype)], axis=0)


def _shift_down(xp, s):
    return xp if s == 0 else pltpu.roll(xp, s, 0)


def _shift_up(xp, s):
    return xp if s == 0 else pltpu.roll(xp, xp.shape[0] - s, 0)


def _taps3(xp):
    one = _shift_down(xp, 1)
    return xp, one, _shift_down(one, 1)


def _conv3_taps(taps, w_ref):
    return w_ref[2:3, :] * taps[0] + w_ref[1:2, :] * taps[1] + w_ref[0:1, :] * taps[2]


def _conv3(xp, w_ref):
    return _conv3_taps(_taps3(xp), w_ref)


def _conv3_t(dyp, w_ref):
    one = _shift_up(dyp, 1)
    return w_ref[2:3, :] * dyp + w_ref[1:2, :] * one + w_ref[0:1, :] * _shift_up(one, 1)


def _conv3_dw(dyp, taps):
    return [jnp.sum(dyp * taps[2 - k], axis=0, keepdims=True) for k in range(3)]


def _ffn_mid_fwd(up_pre, wf, n_ex, name):
    t, f2 = up_pre.shape
    f = f2 // 2
    s = t // n_ex
    cb = MIX_W
    nb = f // cb

    def body(ug_ref, uv_ref, wg_ref, wv_ref, act_ref):
        gf = _conv3(_pad_rows(ug_ref[...].astype(_F32), 8), wg_ref)[:s]
        vf = _conv3(_pad_rows(uv_ref[...].astype(_F32), 8), wv_ref)[:s]
        act_ref[...] = _bf(gf * _sigmoid(gf) * vf)

    return pl.pallas_call(
        body, name=name, grid=(n_ex, nb),
        in_specs=[pl.BlockSpec((s, cb), lambda e, j: (e, j)), pl.BlockSpec((s, cb), lambda e, j: (e, j + nb)),
                  pl.BlockSpec((K_FFN, cb), lambda e, j: (0, j)), pl.BlockSpec((K_FFN, cb), lambda e, j: (0, j + nb))],
        out_specs=pl.BlockSpec((s, cb), lambda e, j: (e, j)),
        out_shape=jax.ShapeDtypeStruct((t, f), _BF),
        compiler_params=_params("parallel", "parallel"),
    )(up_pre, up_pre, wf, wf)


def _ffn_mid_bwd(up_pre, wf, dact, n_ex, name):
    t, f2 = up_pre.shape
    f = f2 // 2
    s = t // n_ex
    cb = MIX_W
    nb = f // cb

    def body(ug_ref, uv_ref, wg_ref, wv_ref, da_ref, dug_ref, duv_ref, dwg_ref, dwv_ref):
        g_taps = _taps3(_pad_rows(ug_ref[...].astype(_F32), 8))
        v_taps = _taps3(_pad_rows(uv_ref[...].astype(_F32), 8))
        gf = _conv3_taps(g_taps, wg_ref)[:s]
        vf = _conv3_taps(v_taps, wv_ref)[:s]
        sg = _sigmoid(gf)
        da = da_ref[...].astype(_F32)

        @pl.when(pl.program_id(1) == 0)
        def _():
            dwg_ref[...] = jnp.zeros_like(dwg_ref)
            dwv_ref[...] = jnp.zeros_like(dwv_ref)

        def finish(dup, w_ref, taps, du_ref, dw_ref):
            dupp = _pad_rows(dup, 8)
            du_ref[...] = _bf(_conv3_t(dupp, w_ref)[:s])
            rows = _conv3_dw(dupp, taps)
            for k in range(3):
                dw_ref[k:k + 1, :] += rows[k]

        finish(da * vf * sg * (1.0 + gf * (1.0 - sg)), wg_ref, g_taps, dug_ref, dwg_ref)
        finish(da * gf * sg, wv_ref, v_taps, duv_ref, dwv_ref)

    return pl.pallas_call(
        body, name=name, grid=(nb, n_ex),
        in_specs=[pl.BlockSpec((s, cb), lambda j, e: (e, j)), pl.BlockSpec((s, cb), lambda j, e: (e, j + nb)),
                  pl.BlockSpec((K_FFN, cb), lambda j, e: (0, j)), pl.BlockSpec((K_FFN, cb), lambda j, e: (0, j + nb)),
                  pl.BlockSpec((s, cb), lambda j, e: (e, j))],
        out_specs=[pl.BlockSpec((s, cb), lambda j, e: (e, j)), pl.BlockSpec((s, cb), lambda j, e: (e, j)),
                   pl.BlockSpec((K_FFN, cb), lambda j, e: (0, j)), pl.BlockSpec((K_FFN, cb), lambda j, e: (0, j))],
        out_shape=[jax.ShapeDtypeStruct((t, f), _BF), jax.ShapeDtypeStruct((t, f), _BF),
                   jax.ShapeDtypeStruct((K_FFN, f), _F32), jax.ShapeDtypeStruct((K_FFN, f), _F32)],
        compiler_params=_params("parallel", "arbitrary"),
    )(up_pre, up_pre, wf, wf, dact)


FFN_HALO = 8
FFN_TILE = 512
FFN_TILE_BWD = 256


def _resident(shape):
    return pl.BlockSpec(shape, lambda i: (0,) * len(shape), pipeline_mode=pl.Buffered(1))


def _ffn_fwd(x1, g2, w_up_t, wf, w_down, n_ex, name):
    t, d = x1.shape
    f2 = w_up_t.shape[0]
    f = f2 // 2
    s = t // n_ex
    tm = _row_tile(s, FFN_TILE)
    tiles_per_ex = s // tm
    cb = MIX_W
    nb = f // cb

    def body(x_ref, g_ref, wu_ref, wf_ref, wd_ref, x2_ref, h_ref, up_ref, act_ref, carry):
        @pl.when(pl.program_id(0) % tiles_per_ex == 0)
        def _():
            carry[...] = jnp.zeros_like(carry)

        xv = x_ref[...]
        h = _bf(xv * _rms_rstd(xv) * g_ref[...])
        h_ref[...] = h
        acc = xv
        for j in range(nb):
            conv = []
            for half in range(2):
                cols = slice(half * f + j * cb, half * f + (j + 1) * cb)
                u = _bf(_dot_nt(h, wu_ref[cols, :]))
                up_ref[:, cols] = u
                ext = jnp.concatenate([carry[:, cols], u.astype(_F32)], axis=0)
                carry[:, cols] = ext[tm:, :]
                one = pltpu.roll(ext, 1, 0)
                two = pltpu.roll(one, 1, 0)
                conv.append((wf_ref[2:3, cols] * ext + wf_ref[1:2, cols] * one + wf_ref[0:1, cols] * two)[FFN_HALO:])
            a = _bf(conv[0] * _sigmoid(conv[0]) * conv[1])
            act_ref[:, j * cb:(j + 1) * cb] = a
            acc = acc + _dot(a, wd_ref[j * cb:(j + 1) * cb, :])
        x2_ref[...] = acc

    row = lambda width: pl.BlockSpec((tm, width), lambda i: (i, 0))
    return pl.pallas_call(
        body, name=name, grid=(t // tm,),
        in_specs=[row(d), pl.BlockSpec((1, d), lambda i: (0, 0)), _resident((f2, d)),
                  pl.BlockSpec((K_FFN, f2), lambda i: (0, 0)), _resident((f, d))],
        out_specs=[row(d), row(d), row(f2), row(f)],
        out_shape=[jax.ShapeDtypeStruct((t, d), _F32), jax.ShapeDtypeStruct((t, d), _BF),
                   jax.ShapeDtypeStruct((t, f2), _BF), jax.ShapeDtypeStruct((t, f), _BF)],
        scratch_shapes=[pltpu.VMEM((FFN_HALO, f2), _F32)],
        compiler_params=_params("arbitrary"),
    )(x1, g2, w_up_t, wf, w_down)


def _ffn_bwd(dxb, dres, w_down, up, wf, w_up_t, x1, g2, n_ex, name):
    t, d = x1.shape
    f2 = w_up_t.shape[0]
    f = f2 // 2
    s = t // n_ex
    tm = _row_tile(s, FFN_TILE_BWD)
    tiles_per_ex = s // tm
    n_tiles = t // tm
    cb = MIX_W
    nb = f // cb
    n_ext = tm + FFN_HALO

    def body(dxb_ref, dres_ref, wd_ref, up_ref, halo_ref, wf_ref, wu_ref, x_ref, g_ref,
             dx_ref, dxo_ref, dup_ref, dwf_ref, dg_ref, carry):
        i = pl.program_id(0)
        r = n_tiles - 1 - i
        first_of_example = r % tiles_per_ex == 0

        @pl.when(r % tiles_per_ex == tiles_per_ex - 1)
        def _():
            carry[...] = jnp.zeros_like(carry)

        @pl.when(i == 0)
        def _():
            dwf_ref[...] = jnp.zeros_like(dwf_ref)
            dg_ref[...] = jnp.zeros_like(dg_ref)

        dxv = dxb_ref[...]
        dh = jnp.zeros((tm, d), _F32)
        for j in range(nb):
            da = _bf(_dot_nt(dxv, wd_ref[j * cb:(j + 1) * cb, :])).astype(_F32)
            taps, conv = [], []
            for half in range(2):
                cols = slice(half * f + j * cb, half * f + (j + 1) * cb)
                halo = jnp.where(first_of_example, 0.0, halo_ref[:, cols].astype(_F32))
                ext = jnp.concatenate([halo, up_ref[:, cols].astype(_F32)], axis=0)
                one = pltpu.roll(ext, 1, 0)
                two = pltpu.roll(one, 1, 0)
                taps.append((ext[FFN_HALO:], one[FFN_HALO:], two[FFN_HALO:]))
                conv.append(wf_ref[2:3, cols] * taps[half][0] + wf_ref[1:2, cols] * taps[half][1]
                            + wf_ref[0:1, cols] * taps[half][2])
            gf, vf = conv
            sg = _sigmoid(gf)
            for half, dpost in enumerate((da * vf * sg * (1.0 + gf * (1.0 - sg)), da * gf * sg)):
                cols = slice(half * f + j * cb, half * f + (j + 1) * cb)
                ext = jnp.concatenate([dpost, carry[:, cols]], axis=0)
                carry[:, cols] = dpost[:FFN_HALO]
                one = pltpu.roll(ext, n_ext - 1, 0)
                two = pltpu.roll(one, n_ext - 1, 0)
                dpre = _bf((wf_ref[2:3, cols] * ext + wf_ref[1:2, cols] * one + wf_ref[0:1, cols] * two)[:tm])
                dup_ref[:, cols] = dpre
                dh = dh + _dot(dpre, wu_ref[cols, :])
                for k in range(K_FFN):
                    dwf_ref[k:k + 1, cols] += jnp.sum(dpost * taps[half][2 - k], axis=0, keepdims=True)

        xv = x_ref[...]
        rstd = _rms_rstd(xv)
        xn = xv * rstd
        u = dh * g_ref[...]
        dx = dres_ref[...] + rstd * (u - xn * jnp.mean(u * xn, axis=-1, keepdims=True))
        dx_ref[...] = dx
        dxo_ref[...] = _bf(dx)
        dg_ref[...] += jnp.sum(dh * xn, axis=0, keepdims=True)

    rev = lambda width: pl.BlockSpec((tm, width), lambda i: (n_tiles - 1 - i, 0))
    halo_blocks = tm // FFN_HALO
    halo_spec = pl.BlockSpec((FFN_HALO, f2), lambda i: (jnp.maximum((n_tiles - 1 - i) * halo_blocks - 1, 0), 0))
    const = lambda shape: pl.BlockSpec(shape, lambda i: (0, 0))
    return pl.pallas_call(
        body, name=name, grid=(n_tiles,),
        in_specs=[rev(d), rev(d), _resident((f, d)), rev(f2), halo_spec, const((K_FFN, f2)), _resident((f2, d)), rev(d),
                  const((1, d))],
        out_specs=[rev(d), rev(d), rev(f2), const((K_FFN, f2)), const((1, d))],
        out_shape=[jax.ShapeDtypeStruct((t, d), _F32), jax.ShapeDtypeStruct((t, d), _BF), jax.ShapeDtypeStruct((t, f2), _BF),
                   jax.ShapeDtypeStruct((K_FFN, f2), _F32), jax.ShapeDtypeStruct((1, d), _F32)],
        scratch_shapes=[pltpu.VMEM((FFN_HALO, f2), _F32)],
        compiler_params=_params("arbitrary"),
    )(dxb, dres, w_down, up, up, wf, w_up_t, x1, g2)


def _pcol(s, j):
    return pl.BlockSpec((s, MIX_W), lambda e, j=j: (e, j))


def _vec(rows=1):
    return pl.BlockSpec((rows, MIX_W), lambda e: (0, 0))


def _mix_a_fwd(p, wa, n_ex, name):
    t = p.shape[0]
    s = t // n_ex

    def body(gb_ref, gc_ref, ha_ref, w_ref, y_ref):
        cv = _conv3(_pad_rows(_ld(gc_ref) * _ld(ha_ref), 8), w_ref)[:s]
        y_ref[...] = _bf(_ld(gb_ref) * cv)

    return pl.pallas_call(
        body, name=name, grid=(n_ex,),
        in_specs=[_pcol(s, 0), _pcol(s, 1), _pcol(s, 2), _vec(K_SHORT)],
        out_specs=pl.BlockSpec((s, MIX_W), lambda e: (e, 0)),
        out_shape=jax.ShapeDtypeStruct((t, MIX_W), _BF),
        compiler_params=_params("parallel"),
    )(p, p, p, wa)


def _mix_a_bwd(p, wa, dmix, n_ex, name):
    t = p.shape[0]
    s = t // n_ex

    def body(gb_ref, gc_ref, ha_ref, w_ref, dy_ref, dp_ref, dw_ref):
        gc = _ld(gc_ref)
        ha = _ld(ha_ref)
        up = _taps3(_pad_rows(gc * ha, 8))
        cv = _conv3_taps(up, w_ref)[:s]
        dy = _ld(dy_ref)
        dcvp = _pad_rows(dy * _ld(gb_ref), 8)
        du = _conv3_t(dcvp, w_ref)[:s]
        dp_ref[:, 0:MIX_W] = _bf(dy * cv)
        dp_ref[:, MIX_W:2 * MIX_W] = _bf(du * ha)
        dp_ref[:, 2 * MIX_W:3 * MIX_W] = _bf(du * gc)

        @pl.when(pl.program_id(0) == 0)
        def _():
            dw_ref[...] = jnp.zeros_like(dw_ref)

        rows = _conv3_dw(dcvp, up)
        for k in range(3):
            dw_ref[k:k + 1, :] += rows[k]

    return pl.pallas_call(
        body, name=name, grid=(n_ex,),
        in_specs=[_pcol(s, 0), _pcol(s, 1), _pcol(s, 2), _vec(K_SHORT), _pcol(s, 0)],
        out_specs=[pl.BlockSpec((s, 3 * MIX_W), lambda e: (e, 0)), _vec(K_SHORT)],
        out_shape=[jax.ShapeDtypeStruct((t, 3 * MIX_W), _BF), jax.ShapeDtypeStruct((K_SHORT, MIX_W), _F32)],
        compiler_params=_params("arbitrary"),
    )(p, p, p, wa, dmix)


CONF_PAD = 32
CONF_ROWS = 64
_CONF_LANES = (slice(0, 128), slice(128, 256))


def _conf_taps(win, ahead):
    n = CONF_ROWS + CONF_PAD
    for b in range(8):
        rot = win if b == 0 else pltpu.roll(win, (n - b) if ahead else b, 0)
        for a in range(4):
            if 8 * a + b < K_CONF:
                yield rot, 8 * a + b, (8 * a) if ahead else (CONF_PAD - 8 * a)


def _ln_fwd(x, g, b):
    mu = jnp.mean(x, axis=-1, keepdims=True)
    xc = x - mu
    rstd = lax.rsqrt(jnp.mean(xc * xc, axis=-1, keepdims=True) + LN_EPS)
    xhat = xc * rstd
    return xhat * g + b, xhat, rstd


def _ln_bwd(dy, xhat, rstd, g):
    dxh = dy * g
    return rstd * (dxh - jnp.mean(dxh, axis=-1, keepdims=True) - xhat * jnp.mean(dxh * xhat, axis=-1, keepdims=True))


def _mix_b_fwd(p, wb, bb, lg, lb, n_ex, name):
    t = p.shape[0]
    s = t // n_ex

    def body(val_ref, gat_ref, w_ref, bb_ref, lg_ref, lb_ref, y_ref, cb_ref, xpad):
        xpad[0:CONF_PAD, :] = jnp.zeros((CONF_PAD, MIX_W), _F32)
        xpad[CONF_PAD:, :] = _ld(val_ref) * _sigmoid(_ld(gat_ref))

        def chunk(c, carry):
            r0 = pl.multiple_of(c * CONF_ROWS, CONF_ROWS)
            for lanes in _CONF_LANES:
                acc = None
                for rot, sh, lo in _conf_taps(xpad[pl.ds(r0, CONF_ROWS + CONF_PAD), lanes], False):
                    term = w_ref[K_CONF - 1 - sh:K_CONF - sh, lanes] * rot[lo:lo + CONF_ROWS]
                    acc = term if acc is None else acc + term
                cb_ref[pl.ds(r0, CONF_ROWS), lanes] = acc + bb_ref[:, lanes]
            return carry

        lax.fori_loop(0, s // CONF_ROWS, chunk, 0)
        yl, _, _ = _ln_fwd(cb_ref[...], lg_ref[...], lb_ref[...])
        y_ref[...] = _bf(yl * _sigmoid(yl))

    return pl.pallas_call(
        body, name=name, grid=(n_ex,),
        in_specs=[_pcol(s, 3), _pcol(s, 4), _vec(K_CONF), _vec(), _vec(), _vec()],
        out_specs=[pl.BlockSpec((s, MIX_W), lambda e: (e, 0)), pl.BlockSpec((s, MIX_W), lambda e: (e, 0))],
        out_shape=[jax.ShapeDtypeStruct((t, MIX_W), _BF), jax.ShapeDtypeStruct((t, MIX_W), _F32)],
        scratch_shapes=[pltpu.VMEM((CONF_PAD + s, MIX_W), _F32)],
        compiler_params=_params("parallel"),
    )(p, p, wb, bb, lg, lb)


def _mix_b_bwd(p, cb, wb, lg, lb, dmix, n_ex, name):
    t = p.shape[0]
    s = t // n_ex

    def body(val_ref, gat_ref, cb_ref, w_ref, lg_ref, lb_ref, dy_ref, dp_ref, dw_ref, dbb_ref, dlg_ref, dlb_ref,
             xpad, dpad, dglu_s, dw_acc):
        @pl.when(pl.program_id(0) == 0)
        def _():
            for r in (dw_ref, dbb_ref, dlg_ref, dlb_ref):
                r[...] = jnp.zeros_like(r)

        yl, xhat, rstd = _ln_fwd(cb_ref[...], lg_ref[...], lb_ref[...])
        sy = _sigmoid(yl)
        dyl = _ld(dy_ref) * sy * (1.0 + yl * (1.0 - sy))
        dlg_ref[...] += jnp.sum(dyl * xhat, axis=0, keepdims=True)
        dlb_ref[...] += jnp.sum(dyl, axis=0, keepdims=True)
        dcb = _ln_bwd(dyl, xhat, rstd, lg_ref[...])
        dbb_ref[...] += jnp.sum(dcb, axis=0, keepdims=True)

        val = _ld(val_ref)
        sg = _sigmoid(_ld(gat_ref))
        xpad[0:CONF_PAD, :] = jnp.zeros((CONF_PAD, MIX_W), _F32)
        xpad[CONF_PAD:, :] = val * sg
        dpad[0:s, :] = dcb
        dpad[s:, :] = jnp.zeros((CONF_PAD, MIX_W), _F32)
        dw_acc[...] = jnp.zeros_like(dw_acc)

        def chunk(c, carry):
            r0 = pl.multiple_of(c * CONF_ROWS, CONF_ROWS)
            for lanes in _CONF_LANES:
                d_win = dpad[pl.ds(r0, CONF_ROWS + CONF_PAD), lanes]
                d_rows = d_win[0:CONF_ROWS]
                acc = None
                for rot, sh, lo in _conf_taps(d_win, True):
                    term = w_ref[K_CONF - 1 - sh:K_CONF - sh, lanes] * rot[lo:lo + CONF_ROWS]
                    acc = term if acc is None else acc + term
                dglu_s[pl.ds(r0, CONF_ROWS), lanes] = acc
                for rot, sh, lo in _conf_taps(xpad[pl.ds(r0, CONF_ROWS + CONF_PAD), lanes], False):
                    prod = d_rows * rot[lo:lo + CONF_ROWS]
                    dw_acc[K_CONF - 1 - sh, :, lanes] += jnp.sum(prod.reshape(CONF_ROWS // 8, 8, 128), axis=0)
            return carry

        lax.fori_loop(0, s // CONF_ROWS, chunk, 0)
        dw_ref[...] += jnp.sum(dw_acc[...], axis=1)
        dglu = dglu_s[...]
        dp_ref[:, 0:MIX_W] = _bf(dglu * sg)
        dp_ref[:, MIX_W:2 * MIX_W] = _bf(dglu * val * sg * (1.0 - sg))

    return pl.pallas_call(
        body, name=name, grid=(n_ex,),
        in_specs=[_pcol(s, 3), _pcol(s, 4), pl.BlockSpec((s, MIX_W), lambda e: (e, 0)), _vec(K_CONF), _vec(), _vec(),
                  _pcol(s, 1)],
        out_specs=[pl.BlockSpec((s, 2 * MIX_W), lambda e: (e, 0)), _vec(K_CONF), _vec(), _vec(), _vec()],
        out_shape=[jax.ShapeDtypeStruct((t, 2 * MIX_W), _BF), jax.ShapeDtypeStruct((K_CONF, MIX_W), _F32),
                   jax.ShapeDtypeStruct((1, MIX_W), _F32), jax.ShapeDtypeStruct((1, MIX_W), _F32),
                   jax.ShapeDtypeStruct((1, MIX_W), _F32)],
        scratch_shapes=[pltpu.VMEM((CONF_PAD + s, MIX_W), _F32), pltpu.VMEM((s + CONF_PAD, MIX_W), _F32),
                        pltpu.VMEM((s, MIX_W), _F32), pltpu.VMEM((K_CONF, 8, MIX_W), _F32)],
        compiler_params=_params("arbitrary"),
    )(p, p, cb, wb, lg, lb, dmix)


_INV_SQRT2 = 0.7071067811865476
_INV_SQRT2PI = 0.3989422804014327


def _gelu(x):
    return 0.5 * x * (1.0 + lax.erf(x * _INV_SQRT2))


def _gelu_grad(x):
    return 0.5 * (1.0 + lax.erf(x * _INV_SQRT2)) + x * _INV_SQRT2PI * jnp.exp(-0.5 * x * x)


def _head_masks(width=MIX_W):
    lane = lax.broadcasted_iota(jnp.int32, (1, width), 1)
    return [(lane >= h * HEAD_DIM) & (lane < (h + 1) * HEAD_DIM) for h in range(N_HEADS)]


def _tril_mask():
    r = lax.broadcasted_iota(jnp.int32, (CHUNK, CHUNK), 0)
    c = lax.broadcasted_iota(jnp.int32, (CHUNK, CHUNK), 1)
    return c <= r


def _sgu_apply(ws_ref, x3, transpose):
    n = x3.shape[0]
    tril = _tril_mask()
    masks = _head_masks()
    xb = _bf(x3)
    out = jnp.zeros(x3.shape, _F32)
    for h in range(N_HEADS):
        w = _bf(jnp.where(tril, ws_ref[h], 0.0))
        wb = jnp.broadcast_to(w[None], (n, CHUNK, CHUNK))
        dims = (((1,), (1,)), ((0,), (0,))) if transpose else (((2,), (1,)), ((0,), (0,)))
        r = lax.dot_general(wb, xb, dims, preferred_element_type=_F32)
        out = out + jnp.where(masks[h][None], r, 0.0)
    return out


def _mix_c_fwd(p, lg, lb, ws, sb_full, n_ex, name):
    t = p.shape[0]
    s = t // n_ex
    nc = s // CHUNK

    def body(pu_ref, pv_ref, lg_ref, lb_ref, ws_ref, sb_ref, y_ref):
        u = _gelu(_ld(pu_ref))
        vl, _, _ = _ln_fwd(_gelu(_ld(pv_ref)), lg_ref[...], lb_ref[...])
        sp = _sgu_apply(ws_ref, vl.reshape(nc, CHUNK, MIX_W), False) + sb_ref[...][None]
        y_ref[...] = _bf(u * sp.reshape(s, MIX_W))

    return pl.pallas_call(
        body, name=name, grid=(n_ex,),
        in_specs=[_pcol(s, 5), _pcol(s, 6), _vec(), _vec(),
                  pl.BlockSpec((N_HEADS, CHUNK, CHUNK), lambda e: (0, 0, 0)), pl.BlockSpec((CHUNK, MIX_W), lambda e: (0, 0))],
        out_specs=pl.BlockSpec((s, MIX_W), lambda e: (e, 0)),
        out_shape=jax.ShapeDtypeStruct((t, MIX_W), _BF),
        compiler_params=_params("parallel"),
    )(p, p, lg, lb, ws, sb_full)


def _mix_c_bwd(p, lg, lb, ws, sb_full, dmix, n_ex, name):
    t = p.shape[0]
    s = t // n_ex
    nc = s // CHUNK

    def body(pu_ref, pv_ref, lg_ref, lb_ref, ws_ref, sb_ref, dy_ref, dp_ref, dlg_ref, dlb_ref, dws_ref, dsb_ref):
        @pl.when(pl.program_id(0) == 0)
        def _():
            for r in (dlg_ref, dlb_ref, dws_ref, dsb_ref):
                r[...] = jnp.zeros_like(r)

        pu = _ld(pu_ref)
        pv = _ld(pv_ref)
        u = _gelu(pu)
        vl, xhat, rstd = _ln_fwd(_gelu(pv), lg_ref[...], lb_ref[...])
        vl3 = vl.reshape(nc, CHUNK, MIX_W)
        sp = _sgu_apply(ws_ref, vl3, False) + sb_ref[...][None]
        dy = _ld(dy_ref)
        dp_ref[:, 0:MIX_W] = _bf(dy * sp.reshape(s, MIX_W) * _gelu_grad(pu))
        dsp3 = (dy * u).reshape(nc, CHUNK, MIX_W)
        dsb_full = jnp.sum(dsp3, axis=0)
        masks = _head_masks()
        tril = _tril_mask()
        dspb = _bf(dsp3)
        vlb = _bf(vl3)
        for h in range(N_HEADS):
            dsb_ref[:, h:h + 1] += jnp.sum(jnp.where(masks[h], dsb_full, 0.0), axis=1, keepdims=True)
            dm = jnp.where(masks[h][None], dspb, jnp.zeros_like(dspb))
            g3 = lax.dot_general(dm, vlb, (((2,), (2,)), ((0,), (0,))), preferred_element_type=_F32)
            dws_ref[h] += jnp.where(tril, jnp.sum(g3, axis=0), 0.0)
        dvl = _sgu_apply(ws_ref, dsp3, True).reshape(s, MIX_W)
        dlg_ref[...] += jnp.sum(dvl * xhat, axis=0, keepdims=True)
        dlb_ref[...] += jnp.sum(dvl, axis=0, keepdims=True)
        dp_ref[:, MIX_W:2 * MIX_W] = _bf(_ln_bwd(dvl, xhat, rstd, lg_ref[...]) * _gelu_grad(pv))

    return pl.pallas_call(
        body, name=name, grid=(n_ex,),
        in_specs=[_pcol(s, 5), _pcol(s, 6), _vec(), _vec(),
                  pl.BlockSpec((N_HEADS, CHUNK, CHUNK), lambda e: (0, 0, 0)), pl.BlockSpec((CHUNK, MIX_W), lambda e: (0, 0)),
                  _pcol(s, 2)],
        out_specs=[pl.BlockSpec((s, 2 * MIX_W), lambda e: (e, 0)), _vec(), _vec(),
                   pl.BlockSpec((N_HEADS, CHUNK, CHUNK), lambda e: (0, 0, 0)), pl.BlockSpec((CHUNK, N_HEADS), lambda e: (0, 0))],
        out_shape=[jax.ShapeDtypeStruct((t, 2 * MIX_W), _BF), jax.ShapeDtypeStruct((1, MIX_W), _F32),
                   jax.ShapeDtypeStruct((1, MIX_W), _F32), jax.ShapeDtypeStruct((N_HEADS, CHUNK, CHUNK), _F32),
                   jax.ShapeDtypeStruct((CHUNK, N_HEADS), _F32)],
        compiler_params=_params("arbitrary"),
    )(p, p, lg, lb, ws, sb_full, dmix)


D_QBLOCK = 256
HEAD_COLS = N_HEADS * KV_BLOCK


def _stack_heads(x3):
    return jnp.stack([_bf(jnp.where(m[None], x3, 0.0)) for m in _head_masks()], axis=1)


def _stack_heads_rows(x):
    return jnp.concatenate([_bf(jnp.where(m, x, 0.0)) for m in _head_masks()], axis=0)


def _cols_to_rows(x):
    return jnp.concatenate([x[:, h * KV_BLOCK:(h + 1) * KV_BLOCK] for h in range(N_HEADS)], axis=0)


def _head_sums(x):
    return [jnp.sum(x[:, h * KV_BLOCK:(h + 1) * KV_BLOCK], axis=1, keepdims=True) for h in range(N_HEADS)]


def _spread(cols):
    tq = cols[0].shape[0]
    return jnp.concatenate([jnp.broadcast_to(c, (tq, KV_BLOCK)) for c in cols], axis=1)


def _pair_dot(x, m2):
    half = 2 * KV_BLOCK
    xb = _bf(x)
    return jnp.concatenate([_dot(xb[:, :half], m2), _dot(xb[:, half:], m2)], axis=1)


def _tri2(lower):
    n = 2 * KV_BLOCK
    r = lax.broadcasted_iota(jnp.int32, (n, n), 0)
    c = lax.broadcasted_iota(jnp.int32, (n, n), 1)
    same = (r >= KV_BLOCK) == (c >= KV_BLOCK)
    return _bf(jnp.where(same & (r > c if lower else r < c), 1.0, 0.0))


def _sb_scores(qs, kc, j, t_idx):
    z = _dot_nt(qs, kc)
    lane = lax.broadcasted_iota(jnp.int32, (1, HEAD_COLS), 1)
    valid = (j * KV_BLOCK + (lane & (KV_BLOCK - 1))) < t_idx
    lb = jnp.minimum(z, 0.0) - jnp.log(1.0 + jnp.exp(-jnp.abs(z)))
    c = jnp.where(valid, lb - z, 0.0)
    return valid, lb, c


RUN_LANES = 128


def _run_lane(j, h):
    return lax.broadcasted_iota(jnp.int32, (1, RUN_LANES), 1) == j * N_HEADS + h


def _d_qblock(s):
    return D_QBLOCK if s % D_QBLOCK == 0 else KV_BLOCK


def _mix_d_fwd(p, n_ex, name):
    t = p.shape[0]
    s = t // n_ex
    tq = _d_qblock(s)
    nq = s // tq
    r = tq // KV_BLOCK
    nb = s // KV_BLOCK
    assert nb * N_HEADS <= RUN_LANES

    def body(q_ref, k_ref, v_ref, y_ref, runs_ref, kc, vc):
        i = pl.program_id(1)

        @pl.when(i == 0)
        def _():
            kc[...] = _stack_heads(k_ref[...].reshape(nb, KV_BLOCK, MIX_W))
            vc[...] = _stack_heads(v_ref[...].reshape(nb, KV_BLOCK, MIX_W))

        qs = _bf(_ld(q_ref) * (HEAD_DIM ** -0.5))
        t_idx = i * tq + lax.broadcasted_iota(jnp.int32, (tq, 1), 0)
        after_m = _tri2(True)
        nkb = (i + 1) * r

        runs_ref[...] = jnp.zeros_like(runs_ref)

        def one_block(j, runs, acc):
            valid, lb, c = _sb_scores(qs, kc[j].reshape(HEAD_COLS, MIX_W), j, t_idx)
            a = jnp.where(valid, jnp.exp(lb + _pair_dot(c, after_m) + _spread(runs)), 0.0)
            acc = acc + _dot(_bf(a), vc[j].reshape(HEAD_COLS, MIX_W))
            kept = runs_ref[...]
            for h in range(N_HEADS):
                kept = jnp.where(_run_lane(j, h), runs[h], kept)
            runs_ref[...] = kept
            return tuple(ru + cs for ru, cs in zip(runs, _head_sums(c))), acc

        def step(trip, carry):
            runs, acc = carry
            for sub in range(r):
                runs, acc = one_block(nkb - 1 - trip * r - sub, runs, acc)
            return runs, acc

        zero = jnp.zeros((tq, 1), _F32)
        _, acc = lax.fori_loop(0, i + 1, step, ((zero,) * N_HEADS, jnp.zeros((tq, MIX_W), _F32)))
        y_ref[...] = _bf(acc)

    return pl.pallas_call(
        body, name=name, grid=(n_ex, nq),
        in_specs=[pl.BlockSpec((tq, MIX_W), lambda e, i: (e * nq + i, 7)), pl.BlockSpec((s, MIX_W), lambda e, i: (e, 8)),
                  pl.BlockSpec((s, MIX_W), lambda e, i: (e, 9))],
        out_specs=[pl.BlockSpec((tq, MIX_W), lambda e, i: (e * nq + i, 0)),
                   pl.BlockSpec((tq, RUN_LANES), lambda e, i: (e * nq + i, 0))],
        out_shape=[jax.ShapeDtypeStruct((t, MIX_W), _BF), jax.ShapeDtypeStruct((t, RUN_LANES), _F32)],
        scratch_shapes=[pltpu.VMEM((nb, N_HEADS, KV_BLOCK, MIX_W), _BF), pltpu.VMEM((nb, N_HEADS, KV_BLOCK, MIX_W), _BF)],
        compiler_params=_params("parallel", "arbitrary"),
    )(p, p, p)


def _mix_d_bwd(p, kept_runs, dmix, n_ex, name):
    t = p.shape[0]
    s = t // n_ex
    tq = _d_qblock(s)
    nq = s // tq
    r = tq // KV_BLOCK
    nb = s // KV_BLOCK
    scale = HEAD_DIM ** -0.5

    def body(q_ref, k_ref, v_ref, runs_ref, do_ref, dq_ref, dk_ref, dv_ref, kc, vc):
        i = pl.program_id(1)

        @pl.when(i == 0)
        def _():
            kc[...] = _stack_heads(k_ref[...].reshape(nb, KV_BLOCK, MIX_W))
            vc[...] = _stack_heads(v_ref[...].reshape(nb, KV_BLOCK, MIX_W))
            dk_ref[...] = jnp.zeros_like(dk_ref)
            dv_ref[...] = jnp.zeros_like(dv_ref)

        q_scaled = _ld(q_ref) * scale
        qs = _bf(q_scaled)
        do = do_ref[...]
        dob = _bf(do)
        q_rows = _stack_heads_rows(q_scaled)
        do_rows = _stack_heads_rows(do)
        kept = runs_ref[...]
        t_idx = i * tq + lax.broadcasted_iota(jnp.int32, (tq, 1), 0)
        after_m = _tri2(True)
        before_m = _tri2(False)
        nkb = (i + 1) * r
        zero = jnp.zeros((tq, 1), _F32)

        def step(trip, carry):
            for sub in range(r):
                carry = one_block(trip * r + sub, carry)
            return carry

        def one_block(j, carry):
            pres, dq = carry
            rows = pl.ds(pl.multiple_of(j * KV_BLOCK, KV_BLOCK), KV_BLOCK)
            kj = kc[j].reshape(HEAD_COLS, MIX_W)
            valid, lb, c = _sb_scores(qs, kj, j, t_idx)
            runs = [jnp.sum(jnp.where(_run_lane(j, h), kept, 0.0), axis=1, keepdims=True) for h in range(N_HEADS)]
            a = jnp.where(valid, jnp.exp(lb + _pair_dot(c, after_m) + _spread(runs)), 0.0)
            g = a * _dot_nt(dob, vc[j].reshape(HEAD_COLS, MIX_W))
            before = _pair_dot(g, before_m) + _spread(pres)
            sig = jnp.exp(lb)
            dz = _bf(jnp.where(valid, g * (1.0 - sig) - sig * before, 0.0))
            dk_ref[rows, :] += _dot_tn(_cols_to_rows(dz), q_rows)
            dv_ref[rows, :] += _dot_tn(_cols_to_rows(_bf(a)), do_rows)
            return tuple(pr + gs for pr, gs in zip(pres, _head_sums(g))), dq + _dot(dz, kj)

        _, dq = lax.fori_loop(0, i + 1, step, ((zero,) * N_HEADS, jnp.zeros((tq, MIX_W), _F32)))
        dq_ref[...] = _bf(dq * scale)

    return pl.pallas_call(
        body, name=name, grid=(n_ex, nq),
        in_specs=[pl.BlockSpec((tq, MIX_W), lambda e, i: (e * nq + i, 7)), pl.BlockSpec((s, MIX_W), lambda e, i: (e, 8)),
                  pl.BlockSpec((s, MIX_W), lambda e, i: (e, 9)), pl.BlockSpec((tq, RUN_LANES), lambda e, i: (e * nq + i, 0)),
                  pl.BlockSpec((tq, MIX_W), lambda e, i: (e * nq + i, 3))],
        out_specs=[pl.BlockSpec((tq, MIX_W), lambda e, i: (e * nq + i, 0)), pl.BlockSpec((s, MIX_W), lambda e, i: (e, 0)),
                   pl.BlockSpec((s, MIX_W), lambda e, i: (e, 0))],
        out_shape=[jax.ShapeDtypeStruct((t, MIX_W), _BF), jax.ShapeDtypeStruct((t, MIX_W), _F32),
                   jax.ShapeDtypeStruct((t, MIX_W), _F32)],
        scratch_shapes=[pltpu.VMEM((nb, N_HEADS, KV_BLOCK, MIX_W), _BF), pltpu.VMEM((nb, N_HEADS, KV_BLOCK, MIX_W), _BF)],
        compiler_params=_params("parallel", "arbitrary"),
    )(p, p, p, kept_runs, dmix)


def _fwd_mix(x, w, l, n_ex):
    p, h1 = _norm_mm(x, w["norm1_g"][l], w["w_in_t"][l], "in_proj")
    y_a = _mix_a_fwd(p, w["conv_a_w"][l], n_ex, "mix_a_fwd")
    y_b, cb = _mix_b_fwd(p, w["conv_b_w"][l], w["conv_b_b"][l], w["ln_b_g"][l], w["ln_b_b"][l], n_ex, "mix_b_fwd")
    y_c = _mix_c_fwd(p, w["ln_c_g"][l], w["ln_c_b"][l], w["sgu_w"][l], w["sgu_b_full"][l], n_ex, "mix_c_fwd")
    y_d, runs_d = _mix_d_fwd(p, n_ex, "mix_d_fwd")
    return dict(x=x, h1=h1, p=p, cb=cb, runs_d=runs_d, mix=(y_a, y_b, y_c, y_d))


def _fwd_ffn(st, w, l, n_ex):
    x1 = _mm_res(st["mix"], w["w_out"][l], st["x"], "out_proj")
    up_pre, h2 = _norm_mm(x1, w["norm2_g"][l], w["w_up_t"][l], "up_proj")
    act = _ffn_mid_fwd(up_pre, w["conv_f_w"][l], n_ex, "ffn_mid_fwd")
    st.update(x1=x1, h2=h2, up_pre=up_pre, act=act)
    return _mm_res((act,), w["w_down"][l], x1, "down_proj")


def _bwd_ffn(st, w, l, dx, dxb, n_ex):
    g = {}
    dact = _mm_nt(dxb, w["w_down"][l], "down_proj_dx")
    g["w_down"] = _mm_tn(st["act"], dxb, "down_proj_dw", _BF)
    dup_g, dup_v, dwf_g, dwf_v = _ffn_mid_bwd(st["up_pre"], w["conv_f_w"][l], dact, n_ex, "ffn_mid_bwd")
    g["conv_f_w"] = jnp.concatenate([dwf_g, dwf_v], axis=1)
    dx, dxb, g["norm2_g"] = _mm_normbwd((dup_g, dup_v), w["w_up_t"][l], st["x1"], w["norm2_g"][l], dx, "up_proj_dx")
    g["w_up_t"] = jnp.concatenate([_mm_tn(part, st["h2"], "up_proj_dw", _BF) for part in (dup_g, dup_v)], axis=0)
    return dx, dxb, g


def _bwd_out_proj(st, w, l, dxb):
    return _mm_nt(dxb, w["w_out"][l], "out_proj_dx"), _mm_tn_parts(st["mix"], dxb, "out_proj_dw")


def _bwd_mixers(st, w, l, dx, dmix, n_ex):
    g = {}
    p = st["p"]
    dp_a, g["conv_a_w"] = _mix_a_bwd(p, w["conv_a_w"][l], dmix, n_ex, "mix_a_bwd")
    dp_b, g["conv_b_w"], g["conv_b_b"], g["ln_b_g"], g["ln_b_b"] = _mix_b_bwd(
        p, st["cb"], w["conv_b_w"][l], w["ln_b_g"][l], w["ln_b_b"][l], dmix, n_ex, "mix_b_bwd")
    dp_c, g["ln_c_g"], g["ln_c_b"], g["sgu_w"], g["sgu_b_t"] = _mix_c_bwd(
        p, w["ln_c_g"][l], w["ln_c_b"][l], w["sgu_w"][l], w["sgu_b_full"][l], dmix, n_ex, "mix_c_bwd")
    dq, dk, dv = _mix_d_bwd(p, st["runs_d"], dmix, n_ex, "mix_d_bwd")
    dp = (dp_a, dp_b, dp_c, dq, dk, dv)
    dx, dxb, g["norm1_g"] = _mm_normbwd(dp, w["w_in_t"][l], st["x"], w["norm1_g"][l], dx, "in_proj_dx")
    g["w_in_t"] = _mm_tn_parts(dp, st["h1"], "in_proj_dw")
    return dx, dxb, g


def _bwd_mix(st, w, l, dx, dxb, n_ex):
    dmix, dw_out = _bwd_out_proj(st, w, l, dxb)
    dx, dxb, g = _bwd_mixers(st, w, l, dx, dmix, n_ex)
    g["w_out"] = dw_out
    return dx, dxb, g


def _local_fwd_bwd(x, target, w, n_ex):
    depth = len(w["w_in_t"])
    saved = []
    for l in range(depth):
        st = _fwd_mix(x, w, l, n_ex)
        x = _fwd_ffn(st, w, l, n_ex)
        saved.append(st)
    dx, dxb, d_final_g, loss = _final_loss(x, w["final_g"], target, "final_loss")
    grads = {}
    for l in reversed(range(depth)):
        dx, dxb, g_ffn = _bwd_ffn(saved[l], w, l, dx, dxb, n_ex)
        dx, dxb, g_mix = _bwd_mix(saved[l], w, l, dx, dxb, n_ex)
        for k, v in {**g_ffn, **g_mix}.items():
            grads.setdefault(k, [None] * depth)[l] = v
    grads["final_g"] = d_final_g
    return loss, dx, grads


_MESH = pl.DeviceIdType.MESH
_ANY = pl.BlockSpec(memory_space=pl.ANY)


def _position():
    return lax.axis_index("x"), lax.axis_index("y"), lax.axis_index("c")


def _flat(px, py, pc):
    return 4 * px + 2 * py + pc


def _all_gather(shard, name, after):
    r, c_ = shard.shape

    def body(x_ref, after_ref, out_ref, send_sems, recv_sems, local_sem):
        x, y, c = _position()
        me, sibling = (x, y, c), (x, y, 1 - c)
        chips = [(1 - x, y), (x, 1 - y), (1 - x, 1 - y)]

        def copy(k, block, to, src=None):
            slab = out_ref.at[_flat(*block)]
            return pltpu.make_async_remote_copy(
                src_ref=slab if src is None else src, dst_ref=slab, send_sem=send_sems.at[k], recv_sem=recv_sems.at[k],
                device_id=to, device_id_type=_MESH)

        mine = pltpu.make_async_copy(x_ref, out_ref.at[_flat(*me)], local_sem)
        mine.start()
        first = [copy(0, me, sibling, src=x_ref)]
        first += [copy(1 + j, me, (*chip, c), src=x_ref) for j, chip in enumerate(chips)]
        for cp in first:
            cp.start()
        passed = [copy(4 + j, (*chip, c), sibling) for j, chip in enumerate(chips)]
        for j, chip in enumerate(chips):
            copy(1 + j, (*chip, c), me).wait_recv()
            passed[j].start()
        copy(0, sibling, me).wait_recv()
        for j, chip in enumerate(chips):
            copy(4 + j, (*chip, 1 - c), me).wait_recv()
        for cp in first + passed:
            cp.wait_send()
        mine.wait()

    return pl.pallas_call(
        body, name=name, out_shape=jax.ShapeDtypeStruct((N_DEV, r, c_), shard.dtype),
        in_specs=[_ANY, _ANY], out_specs=_ANY,
        scratch_shapes=[pltpu.SemaphoreType.DMA((7,)), pltpu.SemaphoreType.DMA((7,)), pltpu.SemaphoreType.DMA],
    )(shard, after)


_HBM = pl.BlockSpec(memory_space=pltpu.HBM)
_SEM = pl.BlockSpec(memory_space=pltpu.SEMAPHORE)
_DATAFLOW = pltpu.SideEffectType.DATAFLOW_SIDE_EFFECTING


def _peers(x, y, c):
    return [((1 - x) if (k + 1) & 4 else x, (1 - y) if (k + 1) & 2 else y, (1 - c) if (k + 1) & 1 else c)
            for k in range(N_DEV - 1)]


def _direct_copies(src_ref, land_ref, send_sems, recv_sems, to_all):
    x, y, c = _position()
    my = _flat(x, y, c)
    out, back = [], []
    for k, peer in enumerate(_peers(x, y, c)):
        src = src_ref if to_all else src_ref.at[_flat(*peer)]
        sems = dict(send_sem=send_sems.at[k], recv_sem=recv_sems.at[k], device_id=peer, device_id_type=_MESH)
        out.append(pltpu.make_async_remote_copy(src_ref=src, dst_ref=land_ref.at[my], **sems))
        back.append(pltpu.make_async_remote_copy(src_ref=src, dst_ref=land_ref.at[_flat(*peer)], **sems))
    return out, back


def _exchange_start(src, to_all, after, name):
    r, c_ = src.shape[-2:]

    def body(src_ref, land_ref, after_ref, send_sems, recv_sems, src_thru, land_thru, token):
        for cp in _direct_copies(src_ref, land_ref, send_sems, recv_sems, to_all)[0]:
            cp.start()
        token[...] = jnp.zeros_like(token)

    land = pltpu.with_memory_space_constraint(lax.empty((N_DEV, r, c_), src.dtype), pltpu.HBM)
    send_sems, recv_sems, src_thru, land_thru, token = pl.pallas_call(
        body, name=name,
        out_shape=(pltpu.SemaphoreType.DMA((N_DEV - 1,)), pltpu.SemaphoreType.DMA((N_DEV - 1,)),
                   pltpu.HBM(src.shape, src.dtype), pltpu.HBM((N_DEV, r, c_), src.dtype), jax.ShapeDtypeStruct((8, 128), _F32)),
        in_specs=(_HBM, _HBM, _ANY), out_specs=(_SEM, _SEM, _HBM, _HBM, pl.BlockSpec(memory_space=pltpu.VMEM)),
        input_output_aliases={0: 2, 1: 3},
        compiler_params=pltpu.CompilerParams(has_side_effects=_DATAFLOW),
    )(pltpu.with_memory_space_constraint(src, pltpu.HBM), land, after)
    return (send_sems, recv_sems, src_thru, land_thru, to_all), token


def _exchange_wait(handle, after, name):
    send_sems, recv_sems, src_thru, land_thru, to_all = handle

    def body(src_ref, land_ref, send_sems, recv_sems, after_ref, src_dead, got_ref):
        out, back = _direct_copies(src_ref, land_ref, send_sems, recv_sems, to_all)
        for cp in out:
            cp.wait_send()
        for cp in back:
            cp.wait_recv()

    return pl.pallas_call(
        body, name=name,
        out_shape=(pltpu.HBM(src_thru.shape, src_thru.dtype), pltpu.HBM(land_thru.shape, land_thru.dtype)),
        in_specs=(_HBM, _HBM, _SEM, _SEM, _ANY), out_specs=(_HBM, _HBM), input_output_aliases={0: 0, 1: 1},
        compiler_params=pltpu.CompilerParams(has_side_effects=_DATAFLOW),
    )(src_thru, land_thru, send_sems, recv_sems, after)


def _with_own(landed, own):
    my = _flat(*_position())
    return lax.dynamic_update_slice(landed, own[None], (my, 0, 0))


def _sum_slabs(slabs, name):
    n, r, c_ = slabs.shape
    tr = _pick_tile(r, 16, max(16, (4 << 20) // (n * c_ * slabs.dtype.itemsize)))

    def body(x_ref, o_ref):
        acc = x_ref[0].astype(_F32)
        for k in range(1, n):
            acc = acc + x_ref[k].astype(_F32)
        o_ref[...] = acc

    return pl.pallas_call(
        body, name=name, grid=(r // tr,),
        in_specs=[pl.BlockSpec((n, tr, c_), lambda i: (0, i, 0))],
        out_specs=pl.BlockSpec((tr, c_), lambda i: (i, 0)),
        out_shape=jax.ShapeDtypeStruct((r, c_), _F32),
        compiler_params=_params("parallel"),
    )(slabs)


def _adamw(w, g, m, v, name):
    r, c_ = w.shape
    tr = _pick_tile(r, 8, 512)

    def body(w_ref, g_ref, m_ref, v_ref, d_ref, nm_ref, nv_ref):
        gv = g_ref[...]
        nm = ADAM_B1 * m_ref[...] + (1.0 - ADAM_B1) * gv
        nv = ADAM_B2 * v_ref[...] + (1.0 - ADAM_B2) * (gv * gv)
        m_hat = nm / (1.0 - ADAM_B1 ** ADAM_STEP)
        v_hat = nv / (1.0 - ADAM_B2 ** ADAM_STEP)
        d_ref[...] = -ADAM_LR * (m_hat / (jnp.sqrt(v_hat) + ADAM_EPS) + ADAM_WD * w_ref[...])
        nm_ref[...] = nm
        nv_ref[...] = nv

    spec = pl.BlockSpec((tr, c_), lambda i: (i, 0))
    shape = jax.ShapeDtypeStruct((r, c_), _F32)
    return pl.pallas_call(
        body, name=name, grid=(r // tr,), in_specs=[spec] * 4, out_specs=[spec] * 3, out_shape=[shape] * 3,
        compiler_params=_params("parallel"),
    )(w, g, m, v)


_SMALL = ("norm1_g", "conv_a_w", "conv_b_w", "conv_b_b", "ln_b_g", "ln_b_b", "ln_c_g", "ln_c_b", "sgu_w", "sgu_b",
          "norm2_g", "conv_f_w", "final_g")
_CONV_SHARDED = ("conv_a_w", "conv_b_w", "conv_f_w")
_NAMES = ("norm1_g", "w_in", "conv_a_w", "conv_b_w", "conv_b_b", "ln_b_g", "ln_b_b", "ln_c_g", "ln_c_b", "sgu_w", "sgu_b",
          "w_out", "norm2_g", "w_up", "conv_f_w", "w_down", "final_g")


def _pack_rows(parts, lanes=128, row_multiple=8):
    flat = jnp.concatenate([a.reshape(-1) for a in parts])
    rows = -(-flat.shape[0] // lanes)
    rows = -(-rows // row_multiple) * row_multiple
    return jnp.pad(flat, (0, rows * lanes - flat.shape[0])).reshape(rows, lanes)


def _unpack_rows(packed, shapes):
    flat = packed.reshape(-1)
    out, off = [], 0
    for shp in shapes:
        size = 1
        for s in shp:
            size *= s
        out.append(flat[off:off + size].reshape(shp))
        off += size
    return out


def _gather_conv_weights(conv_a_w, conv_b_w, conv_f_w, after):
    shards = (conv_a_w, conv_b_w, conv_f_w)
    gathered = _all_gather(_pack_rows(shards), "gather_conv_weights", after)
    full = []
    per_dev = [_unpack_rows(gathered[d], [s.shape for s in shards]) for d in range(N_DEV)]
    for i in range(len(shards)):
        full.append(jnp.concatenate([per_dev[d][i] for d in range(N_DEV)], axis=-1))
    return full


def kernel(x, norm1_g, w_in, conv_a_w, conv_b_w, conv_b_b, ln_b_g, ln_b_b, ln_c_g, ln_c_b, sgu_w, sgu_b, w_out, norm2_g, w_up, conv_f_w, w_down, final_g, loss_target, m_norm1_g, m_w_in, m_conv_a_w, m_conv_b_w, m_conv_b_b, m_ln_b_g, m_ln_b_b, m_ln_c_g, m_ln_c_b, m_sgu_w, m_sgu_b, m_w_out, m_norm2_g, m_w_up, m_conv_f_w, m_w_down, m_final_g, v_norm1_g, v_w_in, v_conv_a_w, v_conv_b_w, v_conv_b_b, v_ln_b_g, v_ln_b_b, v_ln_c_g, v_ln_c_b, v_sgu_w, v_sgu_b, v_w_out, v_norm2_g, v_w_up, v_conv_f_w, v_w_down, v_final_g):
    weights = dict(norm1_g=norm1_g, w_in=w_in, conv_a_w=conv_a_w, conv_b_w=conv_b_w, conv_b_b=conv_b_b, ln_b_g=ln_b_g,
                   ln_b_b=ln_b_b, ln_c_g=ln_c_g, ln_c_b=ln_c_b, sgu_w=sgu_w, sgu_b=sgu_b, w_out=w_out, norm2_g=norm2_g,
                   w_up=w_up, conv_f_w=conv_f_w, w_down=w_down, final_g=final_g)
    mom1 = dict(norm1_g=m_norm1_g, w_in=m_w_in, conv_a_w=m_conv_a_w, conv_b_w=m_conv_b_w, conv_b_b=m_conv_b_b,
                ln_b_g=m_ln_b_g, ln_b_b=m_ln_b_b, ln_c_g=m_ln_c_g, ln_c_b=m_ln_c_b, sgu_w=m_sgu_w, sgu_b=m_sgu_b,
                w_out=m_w_out, norm2_g=m_norm2_g, w_up=m_w_up, conv_f_w=m_conv_f_w, w_down=m_w_down, final_g=m_final_g)
    mom2 = dict(norm1_g=v_norm1_g, w_in=v_w_in, conv_a_w=v_conv_a_w, conv_b_w=v_conv_b_w, conv_b_b=v_conv_b_b,
                ln_b_g=v_ln_b_g, ln_b_b=v_ln_b_b, ln_c_g=v_ln_c_g, ln_c_b=v_ln_c_b, sgu_w=v_sgu_w, sgu_b=v_sgu_b,
                w_out=v_w_out, norm2_g=v_norm2_g, w_up=v_w_up, conv_f_w=v_conv_f_w, w_down=v_w_down, final_g=v_final_g)
    n_ex, seq, d = x.shape
    depth = w_in.shape[0]
    assert depth == 2
    my = _flat(*_position())
    row = lambda a, l: a[l][None]
    tied = lambda a, token: a + token[0:1, 0:1]

    slab = {"w_in": [_bf(jnp.swapaxes(w_in, 1, 2)[l]) for l in range(depth)], "w_out": [_bf(w_out[l]) for l in range(depth)],
            "w_up": [_bf(jnp.swapaxes(w_up, 1, 2)[l]) for l in range(depth)], "w_down": [_bf(w_down[l]) for l in range(depth)]}
    rows = {name: parts[0].shape[0] for name, parts in slab.items()}
    key_of = {"w_in": "w_in_t", "w_out": "w_out", "w_up": "w_up_t", "w_down": "w_down"}
    rest_layer0 = [("w_out", 0), ("w_up", 0), ("w_down", 0)]
    all_layer1 = [("w_in", 1), ("w_out", 1), ("w_up", 1), ("w_down", 1)]

    def split_rows(a, which, merge):
        out, off = {}, 0
        for name, l in which:
            part = a[..., off:off + rows[name], :]
            out[(name, l)] = part.reshape(N_DEV * rows[name], d) if merge else part
            off += rows[name]
        return out

    send_w0 = jnp.concatenate([slab[n][l] for n, l in rest_layer0], axis=0)
    send_w1 = jnp.concatenate([slab[n][l] for n, l in all_layer1], axis=0)
    w_in0 = _all_gather(slab["w_in"][0], "gather_w_in0", norm1_g)
    conv_a_full, conv_b_full, conv_f_full = _gather_conv_weights(conv_a_w, conv_b_w, conv_f_w, w_in0)
    gather0, token = _exchange_start(send_w0, True, conv_f_full, "gather_layer0_start")
    w = {
        "norm1_g": [row(norm1_g, l) for l in range(depth)], "w_in_t": [None] * depth,
        "conv_a_w": [conv_a_full[l] for l in range(depth)], "conv_b_w": [conv_b_full[l] for l in range(depth)],
        "conv_b_b": [row(conv_b_b, l) for l in range(depth)], "ln_b_g": [row(ln_b_g, l) for l in range(depth)],
        "ln_b_b": [row(ln_b_b, l) for l in range(depth)], "ln_c_g": [row(ln_c_g, l) for l in range(depth)],
        "ln_c_b": [row(ln_c_b, l) for l in range(depth)], "sgu_w": [sgu_w[l] for l in range(depth)],
        "sgu_b_full": [jnp.repeat(sgu_b[l].T, HEAD_DIM, axis=1) for l in range(depth)],
        "w_out": [None] * depth, "norm2_g": [row(norm2_g, l) for l in range(depth)], "w_up_t": [None] * depth,
        "conv_f_w": [conv_f_full[l] for l in range(depth)], "w_down": [None] * depth, "final_g": final_g[None],
    }
    w["w_in_t"][0] = w_in0.reshape(N_DEV * rows["w_in"], d)
    w["norm1_g"][0] = tied(row(norm1_g, 0), token)

    def land_weights(handle, after, which, name):
        own, landed = _exchange_wait(handle, after, name)
        for (n, l), mat in split_rows(_with_own(landed, own), which, True).items():
            w[key_of[n]][l] = mat
        return landed

    st0 = _fwd_mix(x.reshape(n_ex * seq, d), w, 0, n_ex)
    landed0 = land_weights(gather0, st0["mix"][3], rest_layer0, "gather_layer0_wait")
    gather1, token = _exchange_start(send_w1, True, landed0, "gather_layer1_start")
    w["norm2_g"][0] = tied(row(norm2_g, 0), token)
    x_mid = _fwd_ffn(st0, w, 0, n_ex)
    land_weights(gather1, x_mid, all_layer1, "gather_layer1_wait")
    st1 = _fwd_mix(x_mid, w, 1, n_ex)
    x_out = _fwd_ffn(st1, w, 1, n_ex)
    dx, dxb, d_final_g, loss = _final_loss(x_out, w["final_g"], loss_target.reshape(n_ex * seq, d), "final_loss")
    loss = lax.psum(loss[0, 0], ("x", "y", "c"))

    def send_grads(g, which, after, name):
        slabs = jnp.concatenate([g[key_of[n]].reshape(N_DEV, rows[n], d) for n, _ in which], axis=1)
        return _exchange_start(slabs, False, after, name)

    dx, dxb, g_ffn1 = _bwd_ffn(st1, w, 1, dx, dxb, n_ex)
    dx, dxb, g_mix1 = _bwd_mix(st1, w, 1, dx, dxb, n_ex)
    grads1, token = send_grads({**g_ffn1, **g_mix1}, all_layer1, dx, "exchange_layer1_start")
    w["norm2_g"][0] = tied(row(norm2_g, 0), token)
    dx, dxb, g_ffn0 = _bwd_ffn(st0, w, 0, dx, dxb, n_ex)
    g_ffn0["w_out"] = _mm_tn_parts(st0["mix"], dxb, "out_proj_dw")
    ffn_layer0 = [("w_out", 0), ("w_up", 0), ("w_down", 0)]
    grads0a, token = send_grads(g_ffn0, ffn_layer0, g_ffn0["w_out"], "exchange_ffn0_start")
    dmix = _mm_nt(dxb, w["w_out"][0], "out_proj_dx", after=token)
    dx, dxb, g_mix0 = _bwd_mixers(st0, w, 0, dx, dmix, n_ex)
    mix_layer0 = [("w_in", 0)]
    grads0b, _ = send_grads(g_mix0, mix_layer0, dx, "exchange_mix0_start")
    grad_x = dx.reshape(n_ex, seq, d)
    g = {k: [{**g_ffn0, **g_mix0}[k], {**g_ffn1, **g_mix1}[k]] for k in g_mix0.keys() | g_ffn0.keys()}
    g["final_g"] = d_final_g

    reduced = {}

    def land_grads(handle, after, which, name):
        sent, landed = _exchange_wait(handle, after, name + "_wait")
        own = lax.dynamic_index_in_dim(sent, my, 0, keepdims=False)
        total = _sum_slabs(_with_own(landed, own), name + "_sum")
        reduced.update(split_rows(total, which, False))
        return total

    def stacked_grad(name):
        stacked = jnp.stack([reduced[(name, l)] for l in range(depth)])
        return jnp.swapaxes(stacked, 1, 2) if name in ("w_in", "w_up") else stacked

    done = land_grads(grads1, dx, all_layer1, "exchange_layer1")
    land_grads(grads0a, done, ffn_layer0, "exchange_ffn0")
    grads = {name: stacked_grad(name) for name in ("w_out", "w_up", "w_down")}

    small_local = {
        "norm1_g": jnp.stack([a[0] for a in g["norm1_g"]]), "conv_a_w": jnp.stack(g["conv_a_w"]),
        "conv_b_w": jnp.stack(g["conv_b_w"]), "conv_b_b": jnp.stack([a[0] for a in g["conv_b_b"]]),
        "ln_b_g": jnp.stack([a[0] for a in g["ln_b_g"]]), "ln_b_b": jnp.stack([a[0] for a in g["ln_b_b"]]),
        "ln_c_g": jnp.stack([a[0] for a in g["ln_c_g"]]), "ln_c_b": jnp.stack([a[0] for a in g["ln_c_b"]]),
        "sgu_w": jnp.stack(g["sgu_w"]), "sgu_b": jnp.stack([a.T for a in g["sgu_b_t"]]),
        "norm2_g": jnp.stack([a[0] for a in g["norm2_g"]]), "conv_f_w": jnp.stack(g["conv_f_w"]),
        "final_g": g["final_g"][0],
    }
    small_sum = _sum_slabs(_all_gather(_pack_rows([small_local[k] for k in _SMALL]), "gather_small_grads", dx),
                           "sum_small_grads")
    for name, total in zip(_SMALL, _unpack_rows(small_sum, [small_local[k].shape for k in _SMALL])):
        if name in _CONV_SHARDED:
            width = weights[name].shape[-1]
            total = lax.dynamic_slice_in_dim(total, my * width, width, axis=-1)
        grads[name] = total

    delta, new_m, new_v = {}, {}, {}
    for name in [n for n in _NAMES if n != "w_in"] + ["w_in"]:
        if name == "w_in":
            land_grads(grads0b, new_v["w_up"], mix_layer0, "exchange_mix0")
            grads["w_in"] = stacked_grad("w_in")
        shp = weights[name].shape
        two_d = (-1, shp[-1]) if len(shp) > 1 else (1, shp[0])
        outs = _adamw(weights[name].reshape(two_d), grads[name].reshape(two_d), mom1[name].reshape(two_d),
                      mom2[name].reshape(two_d), "adamw_" + name)
        delta[name], new_m[name], new_v[name] = (o.reshape(shp) for o in outs)

    return (loss, grad_x, *[grads[n] for n in _NAMES], *[delta[n] for n in _NAMES], *[new_m[n] for n in _NAMES],
            *[new_v[n] for n in _NAMES])
```

```python
import functools

import jax
import jax.numpy as jnp
from jax import lax
from jax.experimental import pallas as pl
from jax.experimental.pallas import tpu as pltpu

_F32 = jnp.float32
_BF = jnp.bfloat16

HEAD_DIM = 64
MIX_W = 256
N_HEADS = MIX_W // HEAD_DIM
CHUNK = 128
KV_BLOCK = 128
K_SHORT = 3
K_CONF = 31
K_FFN = 3
RMS_EPS = 1e-6
LN_EPS = 1e-5
ADAM_LR = 0.001
ADAM_B1 = 0.9
ADAM_B2 = 0.999
ADAM_EPS = 1e-08
ADAM_WD = 0.01
ADAM_STEP = 10
N_DEV = 8
VMEM_LIMIT = 56 * 1024 * 1024


def _bf(x):
    return x.astype(_BF)


def _ld(ref):
    return ref[...].astype(_F32)


_ANY_SPEC = pl.BlockSpec(memory_space=pl.ANY)


def _params(*sem):
    return pltpu.CompilerParams(dimension_semantics=sem, vmem_limit_bytes=VMEM_LIMIT)


def _dot(a, b):
    return jnp.dot(a, b, preferred_element_type=_F32)


def _dot_nt(a, b):
    return lax.dot_general(a, b, (((1,), (1,)), ((), ())), preferred_element_type=_F32)


def _dot_tn(a, b):
    return lax.dot_general(a, b, (((0,), (0,)), ((), ())), preferred_element_type=_F32)


def _row_tile(t, want):
    return want if t % want == 0 else t


def _pick_tile(rows, unit, max_rows):
    best = 0
    for cand in range(unit, min(rows, max_rows) + 1, unit):
        if rows % cand == 0:
            best = cand
    return best or rows


def _sigmoid(x):
    return 1.0 / (1.0 + jnp.exp(-x))


def _rms_rstd(x):
    return lax.rsqrt(jnp.mean(x * x, axis=-1, keepdims=True) + RMS_EPS)


def _norm_mm(x, g, w_t, name):
    t, d = x.shape
    n = w_t.shape[0]
    tm = _row_tile(t, 512)
    tn = _row_tile(n, 512)

    def body(x_ref, g_ref, w_ref, p_ref, h_ref):
        xv = x_ref[...]
        h = _bf(xv * _rms_rstd(xv) * g_ref[...])
        h_ref[...] = h
        for n0 in range(0, n, tn):
            p_ref[:, n0:n0 + tn] = _bf(_dot_nt(h, w_ref[n0:n0 + tn, :]))

    return pl.pallas_call(
        body, name=name, grid=(t // tm,),
        in_specs=[pl.BlockSpec((tm, d), lambda i: (i, 0)), pl.BlockSpec((1, d), lambda i: (0, 0)),
                  pl.BlockSpec((n, d), lambda i: (0, 0))],
        out_specs=[pl.BlockSpec((tm, n), lambda i: (i, 0)), pl.BlockSpec((tm, d), lambda i: (i, 0))],
        out_shape=[jax.ShapeDtypeStruct((t, n), _BF), jax.ShapeDtypeStruct((t, d), _BF)],
        compiler_params=_params("parallel"),
    )(x, g, w_t)


def _mm_nt(a, w_t, name, after=None):
    t, k = a.shape
    n = w_t.shape[0]
    tm = _row_tile(t, 512)
    tn = _row_tile(n, 512) if n % 512 == 0 else _row_tile(n, 256)

    def body(a_ref, w_ref, *rest):
        o_ref = rest[-1]
        av = a_ref[...]
        for n0 in range(0, n, tn):
            o_ref[:, n0:n0 + tn] = _bf(_dot_nt(av, w_ref[n0:n0 + tn, :]))

    extra = () if after is None else (after,)
    return pl.pallas_call(
        body, name=name, grid=(t // tm,),
        in_specs=[pl.BlockSpec((tm, k), lambda i: (i, 0)), pl.BlockSpec((n, k), lambda i: (0, 0))] + [_ANY_SPEC] * len(extra),
        out_specs=pl.BlockSpec((tm, n), lambda i: (i, 0)),
        out_shape=jax.ShapeDtypeStruct((t, n), _BF),
        compiler_params=_params("parallel"),
    )(a, w_t, *extra)


def _mm_res(parts, w, x, name):
    t = x.shape[0]
    k, d = w.shape
    tm = _row_tile(t, 512)
    widths = [a.shape[1] for a in parts]
    n_parts = len(parts)

    def body(*refs):
        w_ref, x_ref, o_ref = refs[n_parts:]
        acc, off = x_ref[...], 0
        for a_ref, width in zip(refs[:n_parts], widths):
            acc = acc + _dot(a_ref[...], w_ref[off:off + width, :])
            off += width
        o_ref[...] = acc

    return pl.pallas_call(
        body, name=name, grid=(t // tm,),
        in_specs=[pl.BlockSpec((tm, width), lambda i: (i, 0)) for width in widths] + [
            pl.BlockSpec((k, d), lambda i: (0, 0)), pl.BlockSpec((tm, d), lambda i: (i, 0))],
        out_specs=pl.BlockSpec((tm, d), lambda i: (i, 0)),
        out_shape=jax.ShapeDtypeStruct((t, d), _F32),
        compiler_params=_params("parallel"),
    )(*parts, w, x)


def _mm_normbwd(parts, w, x, g, dres, name):
    t = x.shape[0]
    k, d = w.shape
    tm = _row_tile(t, 512)
    widths = [a.shape[1] for a in parts]
    n_parts = len(parts)

    def body(*refs):
        a_refs = refs[:n_parts]
        w_ref, x_ref, g_ref, r_ref, dx_ref, dxb_ref, dg_ref = refs[n_parts:]
        dh, off = None, 0
        for a_ref, width in zip(a_refs, widths):
            term = _dot(_bf(a_ref[...]), w_ref[off:off + width, :])
            dh = term if dh is None else dh + term
            off += width
        xv = x_ref[...]
        rstd = _rms_rstd(xv)
        xn = xv * rstd
        u = dh * g_ref[...]
        dx = r_ref[...] + rstd * (u - xn * jnp.mean(u * xn, axis=-1, keepdims=True))
        dx_ref[...] = dx
        dxb_ref[...] = _bf(dx)

        @pl.when(pl.program_id(0) == 0)
        def _():
            dg_ref[...] = jnp.zeros_like(dg_ref)

        dg_ref[...] += jnp.sum(dh * xn, axis=0, keepdims=True)

    return pl.pallas_call(
        body, name=name, grid=(t // tm,),
        in_specs=[pl.BlockSpec((tm, width), lambda i: (i, 0)) for width in widths] + [
            pl.BlockSpec((k, d), lambda i: (0, 0)),
            pl.BlockSpec((tm, d), lambda i: (i, 0)), pl.BlockSpec((1, d), lambda i: (0, 0)),
            pl.BlockSpec((tm, d), lambda i: (i, 0))],
        out_specs=[pl.BlockSpec((tm, d), lambda i: (i, 0)), pl.BlockSpec((tm, d), lambda i: (i, 0)),
                   pl.BlockSpec((1, d), lambda i: (0, 0))],
        out_shape=[jax.ShapeDtypeStruct((t, d), _F32), jax.ShapeDtypeStruct((t, d), _BF),
                   jax.ShapeDtypeStruct((1, d), _F32)],
        compiler_params=_params("arbitrary"),
    )(*parts, w, x, g, dres)


def _mm_tn(a, b, name, out_dtype):
    t, m = a.shape
    n = b.shape[1]
    tm = _pick_tile(m, 128, 1408)
    tn = _pick_tile(n, 128, 1024)
    tk = _row_tile(t, 1024)
    nk = t // tk

    def body(a_ref, b_ref, o_ref, acc):
        kk = pl.program_id(2)

        @pl.when(kk == 0)
        def _():
            acc[...] = jnp.zeros_like(acc)

        acc[...] += _dot_tn(_bf(a_ref[...]), b_ref[...])

        @pl.when(kk == nk - 1)
        def _():
            o_ref[...] = acc[...].astype(o_ref.dtype)

    return pl.pallas_call(
        body, name=name, grid=(m // tm, n // tn, nk),
        in_specs=[pl.BlockSpec((tk, tm), lambda i, j, kk: (kk, i)), pl.BlockSpec((tk, tn), lambda i, j, kk: (kk, j))],
        out_specs=pl.BlockSpec((tm, tn), lambda i, j, kk: (i, j)),
        out_shape=jax.ShapeDtypeStruct((m, n), out_dtype),
        scratch_shapes=[pltpu.VMEM((tm, tn), _F32)],
        compiler_params=_params("parallel", "parallel", "arbitrary"),
    )(a, b)


def _mm_tn_parts(parts, b, name):
    t, n = b.shape
    widths = [a.shape[1] for a in parts]
    m = sum(widths)
    n_parts = len(parts)
    tk = _row_tile(t, 1024)
    nk = t // tk

    def body(*refs):
        b_ref, o_ref, acc = refs[n_parts:]
        kk = pl.program_id(0)

        @pl.when(kk == 0)
        def _():
            acc[...] = jnp.zeros_like(acc)

        bv = b_ref[...]
        off = 0
        for a_ref, width in zip(refs[:n_parts], widths):
            acc[off:off + width, :] += _dot_tn(_bf(a_ref[...]), bv)
            off += width

        @pl.when(kk == nk - 1)
        def _():
            o_ref[...] = _bf(acc[...])

    return pl.pallas_call(
        body, name=name, grid=(nk,),
        in_specs=[pl.BlockSpec((tk, width), lambda kk: (kk, 0)) for width in widths] + [pl.BlockSpec((tk, n), lambda kk: (kk, 0))],
        out_specs=pl.BlockSpec((m, n), lambda kk: (0, 0)),
        out_shape=jax.ShapeDtypeStruct((m, n), _BF),
        scratch_shapes=[pltpu.VMEM((m, n), _F32)],
        compiler_params=_params("arbitrary"),
    )(*parts, b)


def _final_loss(x, g, target, name):
    t, d = x.shape
    tm = _row_tile(t, 256)

    def body(x_ref, g_ref, t_ref, dx_ref, dxb_ref, dg_ref, loss_ref):
        xv = x_ref[...]
        rstd = _rms_rstd(xv)
        xn = xv * rstd
        err = xn * g_ref[...] - t_ref[...]
        dy = err * (1.0 / d)
        u = dy * g_ref[...]
        dx = rstd * (u - xn * jnp.mean(u * xn, axis=-1, keepdims=True))
        dx_ref[...] = dx
        dxb_ref[...] = _bf(dx)

        @pl.when(pl.program_id(0) == 0)
        def _():
            dg_ref[...] = jnp.zeros_like(dg_ref)
            loss_ref[...] = jnp.zeros_like(loss_ref)

        dg_ref[...] += jnp.sum(dy * xn, axis=0, keepdims=True)
        loss_ref[...] += (0.5 / d) * jnp.sum(jnp.sum(err * err, axis=1, keepdims=True), axis=0, keepdims=True)

    return pl.pallas_call(
        body, name=name, grid=(t // tm,),
        in_specs=[pl.BlockSpec((tm, d), lambda i: (i, 0)), pl.BlockSpec((1, d), lambda i: (0, 0)),
                  pl.BlockSpec((tm, d), lambda i: (i, 0))],
        out_specs=[pl.BlockSpec((tm, d), lambda i: (i, 0)), pl.BlockSpec((tm, d), lambda i: (i, 0)),
                   pl.BlockSpec((1, d), lambda i: (0, 0)), pl.BlockSpec((1, 1), lambda i: (0, 0))],
        out_shape=[jax.ShapeDtypeStruct((t, d), _F32), jax.ShapeDtypeStruct((t, d), _BF),
                   jax.ShapeDtypeStruct((1, d), _F32), jax.ShapeDtypeStruct((1, 1), _F32)],
        compiler_params=_params("arbitrary"),
    )(x, g, target)


def _pad_rows(x, pad):
    return jnp.concatenate([x, jnp.zeros((pad, x.shape[1]), x.dtype)], axis=0)


def _shift_down(xp, s):
    return xp if s == 0 else pltpu.roll(xp, s, 0)


def _shift_up(xp, s):
    return xp if s == 0 else pltpu.roll(xp, xp.shape[0] - s, 0)


def _taps3(xp):
    one = _shift_down(xp, 1)
    return xp, one, _shift_down(one, 1)


def _conv3_taps(taps, w_ref):
    return w_ref[2:3, :] * taps[0] + w_ref[1:2, :] * taps[1] + w_ref[0:1, :] * taps[2]


def _conv3(xp, w_ref):
    return _conv3_taps(_taps3(xp), w_ref)


def _conv3_t(dyp, w_ref):
    one = _shift_up(dyp, 1)
    return w_ref[2:3, :] * dyp + w_ref[1:2, :] * one + w_ref[0:1, :] * _shift_up(one, 1)


def _conv3_dw(dyp, taps):
    return [jnp.sum(dyp * taps[2 - k], axis=0, keepdims=True) for k in range(3)]


def _ffn_mid_fwd(up_pre, wf, n_ex, name):
    t, f2 = up_pre.shape
    f = f2 // 2
    s = t // n_ex
    cb = MIX_W
    nb = f // cb

    def body(ug_ref, uv_ref, wg_ref, wv_ref, act_ref, gf_ref, vf_ref):
        gf = _conv3(_pad_rows(ug_ref[...].astype(_F32), 8), wg_ref)[:s]
        vf = _conv3(_pad_rows(uv_ref[...].astype(_F32), 8), wv_ref)[:s]
        act_ref[...] = _bf(gf * _sigmoid(gf) * vf)
        gf_ref[...] = _bf(gf)
        vf_ref[...] = _bf(vf)

    out = pl.BlockSpec((s, cb), lambda e, j: (e, j))
    return pl.pallas_call(
        body, name=name, grid=(n_ex, nb),
        in_specs=[pl.BlockSpec((s, cb), lambda e, j: (e, j)), pl.BlockSpec((s, cb), lambda e, j: (e, j + nb)),
                  pl.BlockSpec((K_FFN, cb), lambda e, j: (0, j)), pl.BlockSpec((K_FFN, cb), lambda e, j: (0, j + nb))],
        out_specs=[out, out, out],
        out_shape=[jax.ShapeDtypeStruct((t, f), _BF)] * 3,
        compiler_params=_params("parallel", "parallel"),
    )(up_pre, up_pre, wf, wf)


def _ffn_mid_bwd(up_pre, conv_g, conv_v, wf, dact, n_ex, name):
    t, f2 = up_pre.shape
    f = f2 // 2
    s = t // n_ex
    cb = MIX_W
    nb = f // cb

    def body(ug_ref, uv_ref, gf_ref, vf_ref, wg_ref, wv_ref, da_ref, dug_ref, duv_ref, dwg_ref, dwv_ref):
        gf = _ld(gf_ref)
        vf = _ld(vf_ref)
        sg = _sigmoid(gf)
        da = _ld(da_ref)

        @pl.when(pl.program_id(1) == 0)
        def _():
            dwg_ref[...] = jnp.zeros_like(dwg_ref)
            dwv_ref[...] = jnp.zeros_like(dwv_ref)

        def finish(dpost, w_ref, x_ref, du_ref, dw_ref):
            ahead = [_pad_rows(dpost, 8)]
            ahead.append(_shift_up(ahead[0], 1))
            ahead.append(_shift_up(ahead[1], 1))
            du_ref[...] = _bf((w_ref[2:3, :] * ahead[0] + w_ref[1:2, :] * ahead[1] + w_ref[0:1, :] * ahead[2])[:s])
            x = _ld(x_ref)
            for k in range(K_FFN):
                dw_ref[k:k + 1, :] += jnp.sum(ahead[2 - k][:s] * x, axis=0, keepdims=True)

        finish(da * vf * sg * (1.0 + gf * (1.0 - sg)), wg_ref, ug_ref, dug_ref, dwg_ref)
        finish(da * gf * sg, wv_ref, uv_ref, duv_ref, dwv_ref)

    return pl.pallas_call(
        body, name=name, grid=(nb, n_ex),
        in_specs=[pl.BlockSpec((s, cb), lambda j, e: (e, j)), pl.BlockSpec((s, cb), lambda j, e: (e, j + nb)),
                  pl.BlockSpec((s, cb), lambda j, e: (e, j)), pl.BlockSpec((s, cb), lambda j, e: (e, j)),
                  pl.BlockSpec((K_FFN, cb), lambda j, e: (0, j)), pl.BlockSpec((K_FFN, cb), lambda j, e: (0, j + nb)),
                  pl.BlockSpec((s, cb), lambda j, e: (e, j))],
        out_specs=[pl.BlockSpec((s, cb), lambda j, e: (e, j)), pl.BlockSpec((s, cb), lambda j, e: (e, j)),
                   pl.BlockSpec((K_FFN, cb), lambda j, e: (0, j)), pl.BlockSpec((K_FFN, cb), lambda j, e: (0, j))],
        out_shape=[jax.ShapeDtypeStruct((t, f), _BF), jax.ShapeDtypeStruct((t, f), _BF),
                   jax.ShapeDtypeStruct((K_FFN, f), _F32), jax.ShapeDtypeStruct((K_FFN, f), _F32)],
        compiler_params=_params("parallel", "arbitrary"),
    )(up_pre, up_pre, conv_g, conv_v, wf, wf, dact)


FFN_HALO = 8
FFN_TILE = 512
FFN_TILE_BWD = 256


def _resident(shape):
    return pl.BlockSpec(shape, lambda i: (0,) * len(shape), pipeline_mode=pl.Buffered(1))


def _ffn_fwd(x1, g2, w_up_t, wf, w_down, n_ex, name):
    t, d = x1.shape
    f2 = w_up_t.shape[0]
    f = f2 // 2
    s = t // n_ex
    tm = _row_tile(s, FFN_TILE)
    tiles_per_ex = s // tm
    cb = MIX_W
    nb = f // cb

    def body(x_ref, g_ref, wu_ref, wf_ref, wd_ref, x2_ref, h_ref, up_ref, act_ref, carry):
        @pl.when(pl.program_id(0) % tiles_per_ex == 0)
        def _():
            carry[...] = jnp.zeros_like(carry)

        xv = x_ref[...]
        h = _bf(xv * _rms_rstd(xv) * g_ref[...])
        h_ref[...] = h
        acc = xv
        for j in range(nb):
            conv = []
            for half in range(2):
                cols = slice(half * f + j * cb, half * f + (j + 1) * cb)
                u = _bf(_dot_nt(h, wu_ref[cols, :]))
                up_ref[:, cols] = u
                ext = jnp.concatenate([carry[:, cols], u.astype(_F32)], axis=0)
                carry[:, cols] = ext[tm:, :]
                one = pltpu.roll(ext, 1, 0)
                two = pltpu.roll(one, 1, 0)
                conv.append((wf_ref[2:3, cols] * ext + wf_ref[1:2, cols] * one + wf_ref[0:1, cols] * two)[FFN_HALO:])
            a = _bf(conv[0] * _sigmoid(conv[0]) * conv[1])
            act_ref[:, j * cb:(j + 1) * cb] = a
            acc = acc + _dot(a, wd_ref[j * cb:(j + 1) * cb, :])
        x2_ref[...] = acc

    row = lambda width: pl.BlockSpec((tm, width), lambda i: (i, 0))
    return pl.pallas_call(
        body, name=name, grid=(t // tm,),
        in_specs=[row(d), pl.BlockSpec((1, d), lambda i: (0, 0)), _resident((f2, d)),
                  pl.BlockSpec((K_FFN, f2), lambda i: (0, 0)), _resident((f, d))],
        out_specs=[row(d), row(d), row(f2), row(f)],
        out_shape=[jax.ShapeDtypeStruct((t, d), _F32), jax.ShapeDtypeStruct((t, d), _BF),
                   jax.ShapeDtypeStruct((t, f2), _BF), jax.ShapeDtypeStruct((t, f), _BF)],
        scratch_shapes=[pltpu.VMEM((FFN_HALO, f2), _F32)],
        compiler_params=_params("arbitrary"),
    )(x1, g2, w_up_t, wf, w_down)


def _ffn_bwd(dxb, dres, w_down, up, wf, w_up_t, x1, g2, n_ex, name):
    t, d = x1.shape
    f2 = w_up_t.shape[0]
    f = f2 // 2
    s = t // n_ex
    tm = _row_tile(s, FFN_TILE_BWD)
    tiles_per_ex = s // tm
    n_tiles = t // tm
    cb = MIX_W
    nb = f // cb
    n_ext = tm + FFN_HALO

    def body(dxb_ref, dres_ref, wd_ref, up_ref, halo_ref, wf_ref, wu_ref, x_ref, g_ref,
             dx_ref, dxo_ref, dup_ref, dwf_ref, dg_ref, carry):
        i = pl.program_id(0)
        r = n_tiles - 1 - i
        first_of_example = r % tiles_per_ex == 0

        @pl.when(r % tiles_per_ex == tiles_per_ex - 1)
        def _():
            carry[...] = jnp.zeros_like(carry)

        @pl.when(i == 0)
        def _():
            dwf_ref[...] = jnp.zeros_like(dwf_ref)
            dg_ref[...] = jnp.zeros_like(dg_ref)

        dxv = dxb_ref[...]
        dh = jnp.zeros((tm, d), _F32)
        for j in range(nb):
            da = _bf(_dot_nt(dxv, wd_ref[j * cb:(j + 1) * cb, :])).astype(_F32)
            taps, conv = [], []
            for half in range(2):
                cols = slice(half * f + j * cb, half * f + (j + 1) * cb)
                halo = jnp.where(first_of_example, 0.0, halo_ref[:, cols].astype(_F32))
                ext = jnp.concatenate([halo, up_ref[:, cols].astype(_F32)], axis=0)
                one = pltpu.roll(ext, 1, 0)
                two = pltpu.roll(one, 1, 0)
                taps.append((ext[FFN_HALO:], one[FFN_HALO:], two[FFN_HALO:]))
                conv.append(wf_ref[2:3, cols] * taps[half][0] + wf_ref[1:2, cols] * taps[half][1]
                            + wf_ref[0:1, cols] * taps[half][2])
            gf, vf = conv
            sg = _sigmoid(gf)
            for half, dpost in enumerate((da * vf * sg * (1.0 + gf * (1.0 - sg)), da * gf * sg)):
                cols = slice(half * f + j * cb, half * f + (j + 1) * cb)
                ext = jnp.concatenate([dpost, carry[:, cols]], axis=0)
                carry[:, cols] = dpost[:FFN_HALO]
                one = pltpu.roll(ext, n_ext - 1, 0)
                two = pltpu.roll(one, n_ext - 1, 0)
                dpre = _bf((wf_ref[2:3, cols] * ext + wf_ref[1:2, cols] * one + wf_ref[0:1, cols] * two)[:tm])
                dup_ref[:, cols] = dpre
                dh = dh + _dot(dpre, wu_ref[cols, :])
                for k in range(K_FFN):
                    dwf_ref[k:k + 1, cols] += jnp.sum(dpost * taps[half][2 - k], axis=0, keepdims=True)

        xv = x_ref[...]
        rstd = _rms_rstd(xv)
        xn = xv * rstd
        u = dh * g_ref[...]
        dx = dres_ref[...] + rstd * (u - xn * jnp.mean(u * xn, axis=-1, keepdims=True))
        dx_ref[...] = dx
        dxo_ref[...] = _bf(dx)
        dg_ref[...] += jnp.sum(dh * xn, axis=0, keepdims=True)

    rev = lambda width: pl.BlockSpec((tm, width), lambda i: (n_tiles - 1 - i, 0))
    halo_blocks = tm // FFN_HALO
    halo_spec = pl.BlockSpec((FFN_HALO, f2), lambda i: (jnp.maximum((n_tiles - 1 - i) * halo_blocks - 1, 0), 0))
    const = lambda shape: pl.BlockSpec(shape, lambda i: (0, 0))
    return pl.pallas_call(
        body, name=name, grid=(n_tiles,),
        in_specs=[rev(d), rev(d), _resident((f, d)), rev(f2), halo_spec, const((K_FFN, f2)), _resident((f2, d)), rev(d),
                  const((1, d))],
        out_specs=[rev(d), rev(d), rev(f2), const((K_FFN, f2)), const((1, d))],
        out_shape=[jax.ShapeDtypeStruct((t, d), _F32), jax.ShapeDtypeStruct((t, d), _BF), jax.ShapeDtypeStruct((t, f2), _BF),
                   jax.ShapeDtypeStruct((K_FFN, f2), _F32), jax.ShapeDtypeStruct((1, d), _F32)],
        scratch_shapes=[pltpu.VMEM((FFN_HALO, f2), _F32)],
        compiler_params=_params("arbitrary"),
    )(dxb, dres, w_down, up, up, wf, w_up_t, x1, g2)


def _pcol(s, j):
    return pl.BlockSpec((s, MIX_W), lambda e, j=j: (e, j))


def _vec(rows=1):
    return pl.BlockSpec((rows, MIX_W), lambda e: (0, 0))


def _mix_a_fwd(p, wa, n_ex, name):
    t = p.shape[0]
    s = t // n_ex

    def body(gb_ref, gc_ref, ha_ref, w_ref, y_ref):
        cv = _conv3(_pad_rows(_ld(gc_ref) * _ld(ha_ref), 8), w_ref)[:s]
        y_ref[...] = _bf(_ld(gb_ref) * cv)

    return pl.pallas_call(
        body, name=name, grid=(n_ex,),
        in_specs=[_pcol(s, 0), _pcol(s, 1), _pcol(s, 2), _vec(K_SHORT)],
        out_specs=pl.BlockSpec((s, MIX_W), lambda e: (e, 0)),
        out_shape=jax.ShapeDtypeStruct((t, MIX_W), _BF),
        compiler_params=_params("parallel"),
    )(p, p, p, wa)


def _mix_a_bwd(p, wa, dmix, n_ex, name):
    t = p.shape[0]
    s = t // n_ex

    def body(gb_ref, gc_ref, ha_ref, w_ref, dy_ref, dp_ref, dw_ref):
        gc = _ld(gc_ref)
        ha = _ld(ha_ref)
        up = _taps3(_pad_rows(gc * ha, 8))
        cv = _conv3_taps(up, w_ref)[:s]
        dy = _ld(dy_ref)
        dcvp = _pad_rows(dy * _ld(gb_ref), 8)
        du = _conv3_t(dcvp, w_ref)[:s]
        dp_ref[:, 0:MIX_W] = _bf(dy * cv)
        dp_ref[:, MIX_W:2 * MIX_W] = _bf(du * ha)
        dp_ref[:, 2 * MIX_W:3 * MIX_W] = _bf(du * gc)

        @pl.when(pl.program_id(0) == 0)
        def _():
            dw_ref[...] = jnp.zeros_like(dw_ref)

        rows = _conv3_dw(dcvp, up)
        for k in range(3):
            dw_ref[k:k + 1, :] += rows[k]

    return pl.pallas_call(
        body, name=name, grid=(n_ex,),
        in_specs=[_pcol(s, 0), _pcol(s, 1), _pcol(s, 2), _vec(K_SHORT), _pcol(s, 0)],
        out_specs=[pl.BlockSpec((s, 3 * MIX_W), lambda e: (e, 0)), _vec(K_SHORT)],
        out_shape=[jax.ShapeDtypeStruct((t, 3 * MIX_W), _BF), jax.ShapeDtypeStruct((K_SHORT, MIX_W), _F32)],
        compiler_params=_params("arbitrary"),
    )(p, p, p, wa, dmix)


CONF_PAD = 32
CONF_ROWS = 64
_CONF_LANES = (slice(0, 128), slice(128, 256))


def _conf_taps(win, ahead):
    n = CONF_ROWS + CONF_PAD
    for b in range(8):
        rot = win if b == 0 else pltpu.roll(win, (n - b) if ahead else b, 0)
        for a in range(4):
            if 8 * a + b < K_CONF:
                yield rot, 8 * a + b, (8 * a) if ahead else (CONF_PAD - 8 * a)


def _ln_fwd(x, g, b):
    mu = jnp.mean(x, axis=-1, keepdims=True)
    xc = x - mu
    rstd = lax.rsqrt(jnp.mean(xc * xc, axis=-1, keepdims=True) + LN_EPS)
    xhat = xc * rstd
    return xhat * g + b, xhat, rstd


def _ln_bwd(dy, xhat, rstd, g):
    dxh = dy * g
    return rstd * (dxh - jnp.mean(dxh, axis=-1, keepdims=True) - xhat * jnp.mean(dxh * xhat, axis=-1, keepdims=True))


def _mix_b_fwd(p, wb, bb, lg, lb, n_ex, name):
    t = p.shape[0]
    s = t // n_ex

    def body(val_ref, gat_ref, w_ref, bb_ref, lg_ref, lb_ref, y_ref, cb_ref, xpad):
        xpad[0:CONF_PAD, :] = jnp.zeros((CONF_PAD, MIX_W), _F32)
        xpad[CONF_PAD:, :] = _ld(val_ref) * _sigmoid(_ld(gat_ref))

        def chunk(c, carry):
            r0 = pl.multiple_of(c * CONF_ROWS, CONF_ROWS)
            for lanes in _CONF_LANES:
                acc = None
                for rot, sh, lo in _conf_taps(xpad[pl.ds(r0, CONF_ROWS + CONF_PAD), lanes], False):
                    term = w_ref[K_CONF - 1 - sh:K_CONF - sh, lanes] * rot[lo:lo + CONF_ROWS]
                    acc = term if acc is None else acc + term
                cb_ref[pl.ds(r0, CONF_ROWS), lanes] = acc + bb_ref[:, lanes]
            return carry

        lax.fori_loop(0, s // CONF_ROWS, chunk, 0)
        yl, _, _ = _ln_fwd(cb_ref[...], lg_ref[...], lb_ref[...])
        y_ref[...] = _bf(yl * _sigmoid(yl))

    return pl.pallas_call(
        body, name=name, grid=(n_ex,),
        in_specs=[_pcol(s, 3), _pcol(s, 4), _vec(K_CONF), _vec(), _vec(), _vec()],
        out_specs=[pl.BlockSpec((s, MIX_W), lambda e: (e, 0)), pl.BlockSpec((s, MIX_W), lambda e: (e, 0))],
        out_shape=[jax.ShapeDtypeStruct((t, MIX_W), _BF), jax.ShapeDtypeStruct((t, MIX_W), _F32)],
        scratch_shapes=[pltpu.VMEM((CONF_PAD + s, MIX_W), _F32)],
        compiler_params=_params("parallel"),
    )(p, p, wb, bb, lg, lb)


def _mix_b_bwd(p, cb, wb, lg, lb, dmix, n_ex, name):
    t = p.shape[0]
    s = t // n_ex

    def body(val_ref, gat_ref, cb_ref, w_ref, lg_ref, lb_ref, dy_ref, dp_ref, dw_ref, dbb_ref, dlg_ref, dlb_ref,
             xpad, dpad, dglu_s, dw_acc):
        @pl.when(pl.program_id(0) == 0)
        def _():
            for r in (dw_ref, dbb_ref, dlg_ref, dlb_ref):
                r[...] = jnp.zeros_like(r)

        yl, xhat, rstd = _ln_fwd(cb_ref[...], lg_ref[...], lb_ref[...])
        sy = _sigmoid(yl)
        dyl = _ld(dy_ref) * sy * (1.0 + yl * (1.0 - sy))
        dlg_ref[...] += jnp.sum(dyl * xhat, axis=0, keepdims=True)
        dlb_ref[...] += jnp.sum(dyl, axis=0, keepdims=True)
        dcb = _ln_bwd(dyl, xhat, rstd, lg_ref[...])
        dbb_ref[...] += jnp.sum(dcb, axis=0, keepdims=True)

        val = _ld(val_ref)
        sg = _sigmoid(_ld(gat_ref))
        xpad[0:CONF_PAD, :] = jnp.zeros((CONF_PAD, MIX_W), _F32)
        xpad[CONF_PAD:, :] = val * sg
        dpad[0:s, :] = dcb
        dpad[s:, :] = jnp.zeros((CONF_PAD, MIX_W), _F32)
        dw_acc[...] = jnp.zeros_like(dw_acc)

        def chunk(c, carry):
            r0 = pl.multiple_of(c * CONF_ROWS, CONF_ROWS)
            for lanes in _CONF_LANES:
                d_win = dpad[pl.ds(r0, CONF_ROWS + CONF_PAD), lanes]
                d_rows = d_win[0:CONF_ROWS]
                acc = None
                for rot, sh, lo in _conf_taps(d_win, True):
                    term = w_ref[K_CONF - 1 - sh:K_CONF - sh, lanes] * rot[lo:lo + CONF_ROWS]
                    acc = term if acc is None else acc + term
                dglu_s[pl.ds(r0, CONF_ROWS), lanes] = acc
                for rot, sh, lo in _conf_taps(xpad[pl.ds(r0, CONF_ROWS + CONF_PAD), lanes], False):
                    prod = d_rows * rot[lo:lo + CONF_ROWS]
                    dw_acc[K_CONF - 1 - sh, :, lanes] += jnp.sum(prod.reshape(CONF_ROWS // 8, 8, 128), axis=0)
            return carry

        lax.fori_loop(0, s // CONF_ROWS, chunk, 0)
        dw_ref[...] += jnp.sum(dw_acc[...], axis=1)
        dglu = dglu_s[...]
        dp_ref[:, 0:MIX_W] = _bf(dglu * sg)
        dp_ref[:, MIX_W:2 * MIX_W] = _bf(dglu * val * sg * (1.0 - sg))

    return pl.pallas_call(
        body, name=name, grid=(n_ex,),
        in_specs=[_pcol(s, 3), _pcol(s, 4), pl.BlockSpec((s, MIX_W), lambda e: (e, 0)), _vec(K_CONF), _vec(), _vec(),
                  _pcol(s, 1)],
        out_specs=[pl.BlockSpec((s, 2 * MIX_W), lambda e: (e, 0)), _vec(K_CONF), _vec(), _vec(), _vec()],
        out_shape=[jax.ShapeDtypeStruct((t, 2 * MIX_W), _BF), jax.ShapeDtypeStruct((K_CONF, MIX_W), _F32),
                   jax.ShapeDtypeStruct((1, MIX_W), _F32), jax.ShapeDtypeStruct((1, MIX_W), _F32),
                   jax.ShapeDtypeStruct((1, MIX_W), _F32)],
        scratch_shapes=[pltpu.VMEM((CONF_PAD + s, MIX_W), _F32), pltpu.VMEM((s + CONF_PAD, MIX_W), _F32),
                        pltpu.VMEM((s, MIX_W), _F32), pltpu.VMEM((K_CONF, 8, MIX_W), _F32)],
        compiler_params=_params("arbitrary"),
    )(p, p, cb, wb, lg, lb, dmix)


_INV_SQRT2 = 0.7071067811865476
_INV_SQRT2PI = 0.3989422804014327


def _gelu(x):
    return 0.5 * x * (1.0 + lax.erf(x * _INV_SQRT2))


def _gelu_grad(x):
    return 0.5 * (1.0 + lax.erf(x * _INV_SQRT2)) + x * _INV_SQRT2PI * jnp.exp(-0.5 * x * x)


def _head_masks(width=MIX_W):
    lane = lax.broadcasted_iota(jnp.int32, (1, width), 1)
    return [(lane >= h * HEAD_DIM) & (lane < (h + 1) * HEAD_DIM) for h in range(N_HEADS)]


def _tril_mask():
    r = lax.broadcasted_iota(jnp.int32, (CHUNK, CHUNK), 0)
    c = lax.broadcasted_iota(jnp.int32, (CHUNK, CHUNK), 1)
    return c <= r


def _sgu_apply(ws_ref, x3, transpose):
    n = x3.shape[0]
    tril = _tril_mask()
    masks = _head_masks()
    xb = _bf(x3)
    out = jnp.zeros(x3.shape, _F32)
    for h in range(N_HEADS):
        w = _bf(jnp.where(tril, ws_ref[h], 0.0))
        wb = jnp.broadcast_to(w[None], (n, CHUNK, CHUNK))
        dims = (((1,), (1,)), ((0,), (0,))) if transpose else (((2,), (1,)), ((0,), (0,)))
        r = lax.dot_general(wb, xb, dims, preferred_element_type=_F32)
        out = out + jnp.where(masks[h][None], r, 0.0)
    return out


def _mix_c_fwd(p, lg, lb, ws, sb_full, n_ex, name):
    t = p.shape[0]
    s = t // n_ex
    nc = s // CHUNK

    def body(pu_ref, pv_ref, lg_ref, lb_ref, ws_ref, sb_ref, y_ref):
        u = _gelu(_ld(pu_ref))
        vl, _, _ = _ln_fwd(_gelu(_ld(pv_ref)), lg_ref[...], lb_ref[...])
        sp = _sgu_apply(ws_ref, vl.reshape(nc, CHUNK, MIX_W), False) + sb_ref[...][None]
        y_ref[...] = _bf(u * sp.reshape(s, MIX_W))

    return pl.pallas_call(
        body, name=name, grid=(n_ex,),
        in_specs=[_pcol(s, 5), _pcol(s, 6), _vec(), _vec(),
                  pl.BlockSpec((N_HEADS, CHUNK, CHUNK), lambda e: (0, 0, 0)), pl.BlockSpec((CHUNK, MIX_W), lambda e: (0, 0))],
        out_specs=pl.BlockSpec((s, MIX_W), lambda e: (e, 0)),
        out_shape=jax.ShapeDtypeStruct((t, MIX_W), _BF),
        compiler_params=_params("parallel"),
    )(p, p, lg, lb, ws, sb_full)


def _mix_c_bwd(p, lg, lb, ws, sb_full, dmix, n_ex, name):
    t = p.shape[0]
    s = t // n_ex
    nc = s // CHUNK

    def body(pu_ref, pv_ref, lg_ref, lb_ref, ws_ref, sb_ref, dy_ref, dp_ref, dlg_ref, dlb_ref, dws_ref, dsb_ref):
        @pl.when(pl.program_id(0) == 0)
        def _():
            for r in (dlg_ref, dlb_ref, dws_ref, dsb_ref):
                r[...] = jnp.zeros_like(r)

        pu = _ld(pu_ref)
        pv = _ld(pv_ref)
        u = _gelu(pu)
        vl, xhat, rstd = _ln_fwd(_gelu(pv), lg_ref[...], lb_ref[...])
        vl3 = vl.reshape(nc, CHUNK, MIX_W)
        sp = _sgu_apply(ws_ref, vl3, False) + sb_ref[...][None]
        dy = _ld(dy_ref)
        dp_ref[:, 0:MIX_W] = _bf(dy * sp.reshape(s, MIX_W) * _gelu_grad(pu))
        dsp3 = (dy * u).reshape(nc, CHUNK, MIX_W)
        dsb_full = jnp.sum(dsp3, axis=0)
        masks = _head_masks()
        tril = _tril_mask()
        dspb = _bf(dsp3)
        vlb = _bf(vl3)
        for h in range(N_HEADS):
            dsb_ref[:, h:h + 1] += jnp.sum(jnp.where(masks[h], dsb_full, 0.0), axis=1, keepdims=True)
            dm = jnp.where(masks[h][None], dspb, jnp.zeros_like(dspb))
            g3 = lax.dot_general(dm, vlb, (((2,), (2,)), ((0,), (0,))), preferred_element_type=_F32)
            dws_ref[h] += jnp.where(tril, jnp.sum(g3, axis=0), 0.0)
        dvl = _sgu_apply(ws_ref, dsp3, True).reshape(s, MIX_W)
        dlg_ref[...] += jnp.sum(dvl * xhat, axis=0, keepdims=True)
        dlb_ref[...] += jnp.sum(dvl, axis=0, keepdims=True)
        dp_ref[:, MIX_W:2 * MIX_W] = _bf(_ln_bwd(dvl, xhat, rstd, lg_ref[...]) * _gelu_grad(pv))

    return pl.pallas_call(
        body, name=name, grid=(n_ex,),
        in_specs=[_pcol(s, 5), _pcol(s, 6), _vec(), _vec(),
                  pl.BlockSpec((N_HEADS, CHUNK, CHUNK), lambda e: (0, 0, 0)), pl.BlockSpec((CHUNK, MIX_W), lambda e: (0, 0)),
                  _pcol(s, 2)],
        out_specs=[pl.BlockSpec((s, 2 * MIX_W), lambda e: (e, 0)), _vec(), _vec(),
                   pl.BlockSpec((N_HEADS, CHUNK, CHUNK), lambda e: (0, 0, 0)), pl.BlockSpec((CHUNK, N_HEADS), lambda e: (0, 0))],
        out_shape=[jax.ShapeDtypeStruct((t, 2 * MIX_W), _BF), jax.ShapeDtypeStruct((1, MIX_W), _F32),
                   jax.ShapeDtypeStruct((1, MIX_W), _F32), jax.ShapeDtypeStruct((N_HEADS, CHUNK, CHUNK), _F32),
                   jax.ShapeDtypeStruct((CHUNK, N_HEADS), _F32)],
        compiler_params=_params("arbitrary"),
    )(p, p, lg, lb, ws, sb_full, dmix)


D_QBLOCK = 256
HEAD_COLS = N_HEADS * KV_BLOCK


def _stack_heads(x3):
    return jnp.stack([_bf(jnp.where(m[None], x3, 0.0)) for m in _head_masks()], axis=1)


def _stack_heads_rows(x):
    return jnp.concatenate([_bf(jnp.where(m, x, 0.0)) for m in _head_masks()], axis=0)


def _cols_to_rows(x):
    return jnp.concatenate([x[:, h * KV_BLOCK:(h + 1) * KV_BLOCK] for h in range(N_HEADS)], axis=0)


def _head_sums(x):
    return [jnp.sum(x[:, h * KV_BLOCK:(h + 1) * KV_BLOCK], axis=1, keepdims=True) for h in range(N_HEADS)]


def _spread(cols):
    tq = cols[0].shape[0]
    return jnp.concatenate([jnp.broadcast_to(c, (tq, KV_BLOCK)) for c in cols], axis=1)


def _pair_dot(x, m2):
    half = 2 * KV_BLOCK
    xb = _bf(x)
    return jnp.concatenate([_dot(xb[:, :half], m2), _dot(xb[:, half:], m2)], axis=1)


def _tri2(lower):
    n = 2 * KV_BLOCK
    r = lax.broadcasted_iota(jnp.int32, (n, n), 0)
    c = lax.broadcasted_iota(jnp.int32, (n, n), 1)
    same = (r >= KV_BLOCK) == (c >= KV_BLOCK)
    return _bf(jnp.where(same & (r > c if lower else r < c), 1.0, 0.0))


def _sb_scores(qs, kc, j, t_idx, on_diagonal):
    z = _dot_nt(qs, kc)
    lb = jnp.minimum(z, 0.0) - jnp.log(1.0 + jnp.exp(-jnp.abs(z)))
    if not on_diagonal:
        return (lambda x: x), lb, lb - z
    lane = lax.broadcasted_iota(jnp.int32, (1, HEAD_COLS), 1)
    valid = (j * KV_BLOCK + (lane & (KV_BLOCK - 1))) < t_idx
    keep = lambda x: jnp.where(valid, x, 0.0)
    return keep, lb, keep(lb - z)


RUN_LANES = 128


def _run_lane(j, h):
    return lax.broadcasted_iota(jnp.int32, (1, RUN_LANES), 1) == j * N_HEADS + h


def _d_qblock(s):
    return D_QBLOCK if s % D_QBLOCK == 0 else KV_BLOCK


def _mix_d_fwd(p, n_ex, name):
    t = p.shape[0]
    s = t // n_ex
    tq = _d_qblock(s)
    nq = s // tq
    r = tq // KV_BLOCK
    nb = s // KV_BLOCK
    assert nb * N_HEADS <= RUN_LANES

    def body(q_ref, k_ref, v_ref, y_ref, runs_ref, kc, vc):
        i = pl.program_id(1)

        @pl.when(i == 0)
        def _():
            kc[...] = _stack_heads(k_ref[...].reshape(nb, KV_BLOCK, MIX_W))
            vc[...] = _stack_heads(v_ref[...].reshape(nb, KV_BLOCK, MIX_W))

        qs = _bf(_ld(q_ref) * (HEAD_DIM ** -0.5))
        t_idx = i * tq + lax.broadcasted_iota(jnp.int32, (tq, 1), 0)
        after_m = _tri2(True)
        nkb = (i + 1) * r

        runs_ref[...] = jnp.zeros_like(runs_ref)

        def one_block(j, runs, acc, on_diagonal):
            keep, lb, c = _sb_scores(qs, kc[j].reshape(HEAD_COLS, MIX_W), j, t_idx, on_diagonal)
            a = keep(jnp.exp(lb + _pair_dot(c, after_m) + _spread(runs)))
            acc = acc + _dot(_bf(a), vc[j].reshape(HEAD_COLS, MIX_W))
            kept = runs_ref[...]
            for h in range(N_HEADS):
                kept = jnp.where(_run_lane(j, h), runs[h], kept)
            runs_ref[...] = kept
            return tuple(ru + cs for ru, cs in zip(runs, _head_sums(c))), acc

        def trip(last, carry, on_diagonal):
            runs, acc = carry
            for sub in range(r):
                runs, acc = one_block(last - sub, runs, acc, on_diagonal)
            return runs, acc

        zero = jnp.zeros((tq, 1), _F32)
        carry = trip(nkb - 1, ((zero,) * N_HEADS, jnp.zeros((tq, MIX_W), _F32)), True)
        _, acc = lax.fori_loop(0, i, lambda m, carry: trip(nkb - 1 - (m + 1) * r, carry, False), carry)
        y_ref[...] = _bf(acc)

    return pl.pallas_call(
        body, name=name, grid=(n_ex, nq),
        in_specs=[pl.BlockSpec((tq, MIX_W), lambda e, i: (e * nq + i, 7)), pl.BlockSpec((s, MIX_W), lambda e, i: (e, 8)),
                  pl.BlockSpec((s, MIX_W), lambda e, i: (e, 9))],
        out_specs=[pl.BlockSpec((tq, MIX_W), lambda e, i: (e * nq + i, 0)),
                   pl.BlockSpec((tq, RUN_LANES), lambda e, i: (e * nq + i, 0))],
        out_shape=[jax.ShapeDtypeStruct((t, MIX_W), _BF), jax.ShapeDtypeStruct((t, RUN_LANES), _F32)],
        scratch_shapes=[pltpu.VMEM((nb, N_HEADS, KV_BLOCK, MIX_W), _BF), pltpu.VMEM((nb, N_HEADS, KV_BLOCK, MIX_W), _BF)],
        compiler_params=_params("parallel", "arbitrary"),
    )(p, p, p)


def _mix_d_bwd(p, kept_runs, dmix, n_ex, name):
    t = p.shape[0]
    s = t // n_ex
    tq = _d_qblock(s)
    nq = s // tq
    r = tq // KV_BLOCK
    nb = s // KV_BLOCK
    scale = HEAD_DIM ** -0.5

    def body(q_ref, k_ref, v_ref, runs_ref, do_ref, dq_ref, dk_ref, dv_ref, kc, vc):
        i = pl.program_id(1)

        @pl.when(i == 0)
        def _():
            kc[...] = _stack_heads(k_ref[...].reshape(nb, KV_BLOCK, MIX_W))
            vc[...] = _stack_heads(v_ref[...].reshape(nb, KV_BLOCK, MIX_W))
            dk_ref[...] = jnp.zeros_like(dk_ref)
            dv_ref[...] = jnp.zeros_like(dv_ref)

        q_scaled = _ld(q_ref) * scale
        qs = _bf(q_scaled)
        do = do_ref[...]
        dob = _bf(do)
        q_rows = _stack_heads_rows(q_scaled)
        do_rows = _stack_heads_rows(do)
        kept = runs_ref[...]
        t_idx = i * tq + lax.broadcasted_iota(jnp.int32, (tq, 1), 0)
        after_m = _tri2(True)
        before_m = _tri2(False)
        nkb = (i + 1) * r
        zero = jnp.zeros((tq, 1), _F32)

        def trip(first, carry, on_diagonal):
            for sub in range(r):
                carry = one_block(first + sub, carry, on_diagonal)
            return carry

        def one_block(j, carry, on_diagonal):
            pres, dq = carry
            rows = pl.ds(pl.multiple_of(j * KV_BLOCK, KV_BLOCK), KV_BLOCK)
            kj = kc[j].reshape(HEAD_COLS, MIX_W)
            keep, lb, c = _sb_scores(qs, kj, j, t_idx, on_diagonal)
            runs = [jnp.sum(jnp.where(_run_lane(j, h), kept, 0.0), axis=1, keepdims=True) for h in range(N_HEADS)]
            a = keep(jnp.exp(lb + _pair_dot(c, after_m) + _spread(runs)))
            g = a * _dot_nt(dob, vc[j].reshape(HEAD_COLS, MIX_W))
            before = _pair_dot(g, before_m) + _spread(pres)
            sig = jnp.exp(lb)
            dz = _bf(keep(g * (1.0 - sig) - sig * before))
            dk_ref[rows, :] += _dot_tn(_cols_to_rows(dz), q_rows)
            dv_ref[rows, :] += _dot_tn(_cols_to_rows(_bf(a)), do_rows)
            return tuple(pr + gs for pr, gs in zip(pres, _head_sums(g))), dq + _dot(dz, kj)

        init = ((zero,) * N_HEADS, jnp.zeros((tq, MIX_W), _F32))
        carry = lax.fori_loop(0, i, lambda m, carry: trip(m * r, carry, False), init)
        _, dq = trip(i * r, carry, True)
        dq_ref[...] = _bf(dq * scale)

    return pl.pallas_call(
        body, name=name, grid=(n_ex, nq),
        in_specs=[pl.BlockSpec((tq, MIX_W), lambda e, i: (e * nq + i, 7)), pl.BlockSpec((s, MIX_W), lambda e, i: (e, 8)),
                  pl.BlockSpec((s, MIX_W), lambda e, i: (e, 9)), pl.BlockSpec((tq, RUN_LANES), lambda e, i: (e * nq + i, 0)),
                  pl.BlockSpec((tq, MIX_W), lambda e, i: (e * nq + i, 3))],
        out_specs=[pl.BlockSpec((tq, MIX_W), lambda e, i: (e * nq + i, 0)), pl.BlockSpec((s, MIX_W), lambda e, i: (e, 0)),
                   pl.BlockSpec((s, MIX_W), lambda e, i: (e, 0))],
        out_shape=[jax.ShapeDtypeStruct((t, MIX_W), _BF), jax.ShapeDtypeStruct((t, MIX_W), _F32),
                   jax.ShapeDtypeStruct((t, MIX_W), _F32)],
        scratch_shapes=[pltpu.VMEM((nb, N_HEADS, KV_BLOCK, MIX_W), _BF), pltpu.VMEM((nb, N_HEADS, KV_BLOCK, MIX_W), _BF)],
        compiler_params=_params("parallel", "arbitrary"),
    )(p, p, p, kept_runs, dmix)


def _fwd_mix(x, w, l, n_ex):
    p, h1 = _norm_mm(x, w["norm1_g"][l], w["w_in_t"][l], "in_proj")
    y_a = _mix_a_fwd(p, w["conv_a_w"][l], n_ex, "mix_a_fwd")
    y_b, cb = _mix_b_fwd(p, w["conv_b_w"][l], w["conv_b_b"][l], w["ln_b_g"][l], w["ln_b_b"][l], n_ex, "mix_b_fwd")
    y_c = _mix_c_fwd(p, w["ln_c_g"][l], w["ln_c_b"][l], w["sgu_w"][l], w["sgu_b_full"][l], n_ex, "mix_c_fwd")
    y_d, runs_d = _mix_d_fwd(p, n_ex, "mix_d_fwd")
    return dict(x=x, h1=h1, p=p, cb=cb, runs_d=runs_d, mix=(y_a, y_b, y_c, y_d))


def _fwd_ffn(st, w, l, n_ex):
    x1 = _mm_res(st["mix"], w["w_out"][l], st["x"], "out_proj")
    up_pre, h2 = _norm_mm(x1, w["norm2_g"][l], w["w_up_t"][l], "up_proj")
    act, conv_g, conv_v = _ffn_mid_fwd(up_pre, w["conv_f_w"][l], n_ex, "ffn_mid_fwd")
    st.update(x1=x1, h2=h2, up_pre=up_pre, act=act, conv_g=conv_g, conv_v=conv_v)
    return _mm_res((act,), w["w_down"][l], x1, "down_proj")


def _bwd_ffn(st, w, l, dx, dxb, n_ex):
    g = {}
    dact = _mm_nt(dxb, w["w_down"][l], "down_proj_dx")
    g["w_down"] = _mm_tn(st["act"], dxb, "down_proj_dw", _BF)
    dup_g, dup_v, dwf_g, dwf_v = _ffn_mid_bwd(
        st["up_pre"], st["conv_g"], st["conv_v"], w["conv_f_w"][l], dact, n_ex, "ffn_mid_bwd")
    g["conv_f_w"] = jnp.concatenate([dwf_g, dwf_v], axis=1)
    dx, dxb, g["norm2_g"] = _mm_normbwd((dup_g, dup_v), w["w_up_t"][l], st["x1"], w["norm2_g"][l], dx, "up_proj_dx")
    g["w_up_t"] = jnp.concatenate([_mm_tn(part, st["h2"], "up_proj_dw", _BF) for part in (dup_g, dup_v)], axis=0)
    return dx, dxb, g


def _bwd_out_proj(st, w, l, dxb):
    return _mm_nt(dxb, w["w_out"][l], "out_proj_dx"), _mm_tn_parts(st["mix"], dxb, "out_proj_dw")


def _bwd_mixers(st, w, l, dx, dmix, n_ex):
    g = {}
    p = st["p"]
    dp_a, g["conv_a_w"] = _mix_a_bwd(p, w["conv_a_w"][l], dmix, n_ex, "mix_a_bwd")
    dp_b, g["conv_b_w"], g["conv_b_b"], g["ln_b_g"], g["ln_b_b"] = _mix_b_bwd(
        p, st["cb"], w["conv_b_w"][l], w["ln_b_g"][l], w["ln_b_b"][l], dmix, n_ex, "mix_b_bwd")
    dp_c, g["ln_c_g"], g["ln_c_b"], g["sgu_w"], g["sgu_b_t"] = _mix_c_bwd(
        p, w["ln_c_g"][l], w["ln_c_b"][l], w["sgu_w"][l], w["sgu_b_full"][l], dmix, n_ex, "mix_c_bwd")
    dq, dk, dv = _mix_d_bwd(p, st["runs_d"], dmix, n_ex, "mix_d_bwd")
    dp = (dp_a, dp_b, dp_c, dq, dk, dv)
    dx, dxb, g["norm1_g"] = _mm_normbwd(dp, w["w_in_t"][l], st["x"], w["norm1_g"][l], dx, "in_proj_dx")
    g["w_in_t"] = _mm_tn_parts(dp, st["h1"], "in_proj_dw")
    return dx, dxb, g


def _bwd_mix(st, w, l, dx, dxb, n_ex):
    dmix, dw_out = _bwd_out_proj(st, w, l, dxb)
    dx, dxb, g = _bwd_mixers(st, w, l, dx, dmix, n_ex)
    g["w_out"] = dw_out
    return dx, dxb, g


def _local_fwd_bwd(x, target, w, n_ex):
    depth = len(w["w_in_t"])
    saved = []
    for l in range(depth):
        st = _fwd_mix(x, w, l, n_ex)
        x = _fwd_ffn(st, w, l, n_ex)
        saved.append(st)
    dx, dxb, d_final_g, loss = _final_loss(x, w["final_g"], target, "final_loss")
    grads = {}
    for l in reversed(range(depth)):
        dx, dxb, g_ffn = _bwd_ffn(saved[l], w, l, dx, dxb, n_ex)
        dx, dxb, g_mix = _bwd_mix(saved[l], w, l, dx, dxb, n_ex)
        for k, v in {**g_ffn, **g_mix}.items():
            grads.setdefault(k, [None] * depth)[l] = v
    grads["final_g"] = d_final_g
    return loss, dx, grads


_MESH = pl.DeviceIdType.MESH
_ANY = pl.BlockSpec(memory_space=pl.ANY)


def _position():
    return lax.axis_index("x"), lax.axis_index("y"), lax.axis_index("c")


def _flat(px, py, pc):
    return 4 * px + 2 * py + pc


def _all_gather(shard, name, after):
    r, c_ = shard.shape

    def body(x_ref, after_ref, out_ref, send_sems, recv_sems, local_sem):
        x, y, c = _position()
        me, sibling = (x, y, c), (x, y, 1 - c)
        chips = [(1 - x, y), (x, 1 - y), (1 - x, 1 - y)]

        def copy(k, block, to, src=None):
            slab = out_ref.at[_flat(*block)]
            return pltpu.make_async_remote_copy(
                src_ref=slab if src is None else src, dst_ref=slab, send_sem=send_sems.at[k], recv_sem=recv_sems.at[k],
                device_id=to, device_id_type=_MESH)

        mine = pltpu.make_async_copy(x_ref, out_ref.at[_flat(*me)], local_sem)
        mine.start()
        first = [copy(0, me, sibling, src=x_ref)]
        first += [copy(1 + j, me, (*chip, c), src=x_ref) for j, chip in enumerate(chips)]
        for cp in first:
            cp.start()
        passed = [copy(4 + j, (*chip, c), sibling) for j, chip in enumerate(chips)]
        for j, chip in enumerate(chips):
            copy(1 + j, (*chip, c), me).wait_recv()
            passed[j].start()
        copy(0, sibling, me).wait_recv()
        for j, chip in enumerate(chips):
            copy(4 + j, (*chip, 1 - c), me).wait_recv()
        for cp in first + passed:
            cp.wait_send()
        mine.wait()

    return pl.pallas_call(
        body, name=name, out_shape=jax.ShapeDtypeStruct((N_DEV, r, c_), shard.dtype),
        in_specs=[_ANY, _ANY], out_specs=_ANY,
        scratch_shapes=[pltpu.SemaphoreType.DMA((7,)), pltpu.SemaphoreType.DMA((7,)), pltpu.SemaphoreType.DMA],
    )(shard, after)


_HBM = pl.BlockSpec(memory_space=pltpu.HBM)
_SEM = pl.BlockSpec(memory_space=pltpu.SEMAPHORE)
_DATAFLOW = pltpu.SideEffectType.DATAFLOW_SIDE_EFFECTING


def _peers(x, y, c):
    return [((1 - x) if (k + 1) & 4 else x, (1 - y) if (k + 1) & 2 else y, (1 - c) if (k + 1) & 1 else c)
            for k in range(N_DEV - 1)]


def _direct_copies(src_ref, land_ref, send_sems, recv_sems, to_all):
    x, y, c = _position()
    my = _flat(x, y, c)
    out, back = [], []
    for k, peer in enumerate(_peers(x, y, c)):
        src = src_ref if to_all else src_ref.at[_flat(*peer)]
        sems = dict(send_sem=send_sems.at[k], recv_sem=recv_sems.at[k], device_id=peer, device_id_type=_MESH)
        out.append(pltpu.make_async_remote_copy(src_ref=src, dst_ref=land_ref.at[my], **sems))
        back.append(pltpu.make_async_remote_copy(src_ref=src, dst_ref=land_ref.at[_flat(*peer)], **sems))
    return out, back


def _exchange_start(src, to_all, after, name):
    r, c_ = src.shape[-2:]

    def body(src_ref, land_ref, after_ref, send_sems, recv_sems, src_thru, land_thru, token):
        for cp in _direct_copies(src_ref, land_ref, send_sems, recv_sems, to_all)[0]:
            cp.start()
        token[...] = jnp.zeros_like(token)

    land = pltpu.with_memory_space_constraint(lax.empty((N_DEV, r, c_), src.dtype), pltpu.HBM)
    send_sems, recv_sems, src_thru, land_thru, token = pl.pallas_call(
        body, name=name,
        out_shape=(pltpu.SemaphoreType.DMA((N_DEV - 1,)), pltpu.SemaphoreType.DMA((N_DEV - 1,)),
                   pltpu.HBM(src.shape, src.dtype), pltpu.HBM((N_DEV, r, c_), src.dtype), jax.ShapeDtypeStruct((8, 128), _F32)),
        in_specs=(_HBM, _HBM, _ANY), out_specs=(_SEM, _SEM, _HBM, _HBM, pl.BlockSpec(memory_space=pltpu.VMEM)),
        input_output_aliases={0: 2, 1: 3},
        compiler_params=pltpu.CompilerParams(has_side_effects=_DATAFLOW),
    )(pltpu.with_memory_space_constraint(src, pltpu.HBM), land, after)
    return (send_sems, recv_sems, src_thru, land_thru, to_all), token


def _exchange_wait(handle, after, name):
    send_sems, recv_sems, src_thru, land_thru, to_all = handle

    def body(src_ref, land_ref, send_sems, recv_sems, after_ref, src_dead, got_ref):
        out, back = _direct_copies(src_ref, land_ref, send_sems, recv_sems, to_all)
        for cp in out:
            cp.wait_send()
        for cp in back:
            cp.wait_recv()

    return pl.pallas_call(
        body, name=name,
        out_shape=(pltpu.HBM(src_thru.shape, src_thru.dtype), pltpu.HBM(land_thru.shape, land_thru.dtype)),
        in_specs=(_HBM, _HBM, _SEM, _SEM, _ANY), out_specs=(_HBM, _HBM), input_output_aliases={0: 0, 1: 1},
        compiler_params=pltpu.CompilerParams(has_side_effects=_DATAFLOW),
    )(src_thru, land_thru, send_sems, recv_sems, after)


def _with_own(landed, own):
    my = _flat(*_position())
    return lax.dynamic_update_slice(landed, own[None], (my, 0, 0))


def _sum_slabs(slabs, name):
    n, r, c_ = slabs.shape
    tr = _pick_tile(r, 16, max(16, (4 << 20) // (n * c_ * slabs.dtype.itemsize)))

    def body(x_ref, o_ref):
        acc = x_ref[0].astype(_F32)
        for k in range(1, n):
            acc = acc + x_ref[k].astype(_F32)
        o_ref[...] = acc

    return pl.pallas_call(
        body, name=name, grid=(r // tr,),
        in_specs=[pl.BlockSpec((n, tr, c_), lambda i: (0, i, 0))],
        out_specs=pl.BlockSpec((tr, c_), lambda i: (i, 0)),
        out_shape=jax.ShapeDtypeStruct((r, c_), _F32),
        compiler_params=_params("parallel"),
    )(slabs)


def _adamw(w, g, m, v, name):
    r, c_ = w.shape
    tr = _pick_tile(r, 8, 512)

    def body(w_ref, g_ref, m_ref, v_ref, d_ref, nm_ref, nv_ref):
        gv = g_ref[...]
        nm = ADAM_B1 * m_ref[...] + (1.0 - ADAM_B1) * gv
        nv = ADAM_B2 * v_ref[...] + (1.0 - ADAM_B2) * (gv * gv)
        m_hat = nm / (1.0 - ADAM_B1 ** ADAM_STEP)
        v_hat = nv / (1.0 - ADAM_B2 ** ADAM_STEP)
        d_ref[...] = -ADAM_LR * (m_hat / (jnp.sqrt(v_hat) + ADAM_EPS) + ADAM_WD * w_ref[...])
        nm_ref[...] = nm
        nv_ref[...] = nv

    spec = pl.BlockSpec((tr, c_), lambda i: (i, 0))
    shape = jax.ShapeDtypeStruct((r, c_), _F32)
    return pl.pallas_call(
        body, name=name, grid=(r // tr,), in_specs=[spec] * 4, out_specs=[spec] * 3, out_shape=[shape] * 3,
        compiler_params=_params("parallel"),
    )(w, g, m, v)


_SMALL = ("norm1_g", "conv_a_w", "conv_b_w", "conv_b_b", "ln_b_g", "ln_b_b", "ln_c_g", "ln_c_b", "sgu_w", "sgu_b",
          "norm2_g", "conv_f_w", "final_g")
_CONV_SHARDED = ("conv_a_w", "conv_b_w", "conv_f_w")
_NAMES = ("norm1_g", "w_in", "conv_a_w", "conv_b_w", "conv_b_b", "ln_b_g", "ln_b_b", "ln_c_g", "ln_c_b", "sgu_w", "sgu_b",
          "w_out", "norm2_g", "w_up", "conv_f_w", "w_down", "final_g")


def _pack_rows(parts, lanes=128, row_multiple=8):
    flat = jnp.concatenate([a.reshape(-1) for a in parts])
    rows = -(-flat.shape[0] // lanes)
    rows = -(-rows // row_multiple) * row_multiple
    return jnp.pad(flat, (0, rows * lanes - flat.shape[0])).reshape(rows, lanes)


def _unpack_rows(packed, shapes):
    flat = packed.reshape(-1)
    out, off = [], 0
    for shp in shapes:
        size = 1
        for s in shp:
            size *= s
        out.append(flat[off:off + size].reshape(shp))
        off += size
    return out


def _gather_conv_weights(conv_a_w, conv_b_w, conv_f_w, after):
    shards = (conv_a_w, conv_b_w, conv_f_w)
    gathered = _all_gather(_pack_rows(shards), "gather_conv_weights", after)
    full = []
    per_dev = [_unpack_rows(gathered[d], [s.shape for s in shards]) for d in range(N_DEV)]
    for i in range(len(shards)):
        full.append(jnp.concatenate([per_dev[d][i] for d in range(N_DEV)], axis=-1))
    return full


def kernel(x, norm1_g, w_in, conv_a_w, conv_b_w, conv_b_b, ln_b_g, ln_b_b, ln_c_g, ln_c_b, sgu_w, sgu_b, w_out, norm2_g, w_up, conv_f_w, w_down, final_g, loss_target, m_norm1_g, m_w_in, m_conv_a_w, m_conv_b_w, m_conv_b_b, m_ln_b_g, m_ln_b_b, m_ln_c_g, m_ln_c_b, m_sgu_w, m_sgu_b, m_w_out, m_norm2_g, m_w_up, m_conv_f_w, m_w_down, m_final_g, v_norm1_g, v_w_in, v_conv_a_w, v_conv_b_w, v_conv_b_b, v_ln_b_g, v_ln_b_b, v_ln_c_g, v_ln_c_b, v_sgu_w, v_sgu_b, v_w_out, v_norm2_g, v_w_up, v_conv_f_w, v_w_down, v_final_g):
    weights = dict(norm1_g=norm1_g, w_in=w_in, conv_a_w=conv_a_w, conv_b_w=conv_b_w, conv_b_b=conv_b_b, ln_b_g=ln_b_g,
                   ln_b_b=ln_b_b, ln_c_g=ln_c_g, ln_c_b=ln_c_b, sgu_w=sgu_w, sgu_b=sgu_b, w_out=w_out, norm2_g=norm2_g,
                   w_up=w_up, conv_f_w=conv_f_w, w_down=w_down, final_g=final_g)
    mom1 = dict(norm1_g=m_norm1_g, w_in=m_w_in, conv_a_w=m_conv_a_w, conv_b_w=m_conv_b_w, conv_b_b=m_conv_b_b,
                ln_b_g=m_ln_b_g, ln_b_b=m_ln_b_b, ln_c_g=m_ln_c_g, ln_c_b=m_ln_c_b, sgu_w=m_sgu_w, sgu_b=m_sgu_b,
                w_out=m_w_out, norm2_g=m_norm2_g, w_up=m_w_up, conv_f_w=m_conv_f_w, w_down=m_w_down, final_g=m_final_g)
    mom2 = dict(norm1_g=v_norm1_g, w_in=v_w_in, conv_a_w=v_conv_a_w, conv_b_w=v_conv_b_w, conv_b_b=v_conv_b_b,
                ln_b_g=v_ln_b_g, ln_b_b=v_ln_b_b, ln_c_g=v_ln_c_g, ln_c_b=v_ln_c_b, sgu_w=v_sgu_w, sgu_b=v_sgu_b,
                w_out=v_w_out, norm2_g=v_norm2_g, w_up=v_w_up, conv_f_w=v_conv_f_w, w_down=v_w_down, final_g=v_final_g)
    n_ex, seq, d = x.shape
    depth = w_in.shape[0]
    assert depth == 2
    my = _flat(*_position())
    row = lambda a, l: a[l][None]
    tied = lambda a, token: a + token[0:1, 0:1]

    slab = {"w_in": [_bf(jnp.swapaxes(w_in, 1, 2)[l]) for l in range(depth)], "w_out": [_bf(w_out[l]) for l in range(depth)],
            "w_up": [_bf(jnp.swapaxes(w_up, 1, 2)[l]) for l in range(depth)], "w_down": [_bf(w_down[l]) for l in range(depth)]}
    rows = {name: parts[0].shape[0] for name, parts in slab.items()}
    key_of = {"w_in": "w_in_t", "w_out": "w_out", "w_up": "w_up_t", "w_down": "w_down"}
    rest_layer0 = [("w_out", 0), ("w_up", 0), ("w_down", 0)]
    all_layer1 = [("w_in", 1), ("w_out", 1), ("w_up", 1), ("w_down", 1)]

    def split_rows(a, which, merge):
        out, off = {}, 0
        for name, l in which:
            part = a[..., off:off + rows[name], :]
            out[(name, l)] = part.reshape(N_DEV * rows[name], d) if merge else part
            off += rows[name]
        return out

    send_w0 = jnp.concatenate([slab[n][l] for n, l in rest_layer0], axis=0)
    send_w1 = jnp.concatenate([slab[n][l] for n, l in all_layer1], axis=0)
    w_in0 = _all_gather(slab["w_in"][0], "gather_w_in0", norm1_g)
    conv_a_full, conv_b_full, conv_f_full = _gather_conv_weights(conv_a_w, conv_b_w, conv_f_w, w_in0)
    gather0, token = _exchange_start(send_w0, True, conv_f_full, "gather_layer0_start")
    w = {
        "norm1_g": [row(norm1_g, l) for l in range(depth)], "w_in_t": [None] * depth,
        "conv_a_w": [conv_a_full[l] for l in range(depth)], "conv_b_w": [conv_b_full[l] for l in range(depth)],
        "conv_b_b": [row(conv_b_b, l) for l in range(depth)], "ln_b_g": [row(ln_b_g, l) for l in range(depth)],
        "ln_b_b": [row(ln_b_b, l) for l in range(depth)], "ln_c_g": [row(ln_c_g, l) for l in range(depth)],
        "ln_c_b": [row(ln_c_b, l) for l in range(depth)], "sgu_w": [sgu_w[l] for l in range(depth)],
        "sgu_b_full": [jnp.repeat(sgu_b[l].T, HEAD_DIM, axis=1) for l in range(depth)],
        "w_out": [None] * depth, "norm2_g": [row(norm2_g, l) for l in range(depth)], "w_up_t": [None] * depth,
        "conv_f_w": [conv_f_full[l] for l in range(depth)], "w_down": [None] * depth, "final_g": final_g[None],
    }
    w["w_in_t"][0] = w_in0.reshape(N_DEV * rows["w_in"], d)
    w["norm1_g"][0] = tied(row(norm1_g, 0), token)

    def land_weights(handle, after, which, name):
        own, landed = _exchange_wait(handle, after, name)
        for (n, l), mat in split_rows(_with_own(landed, own), which, True).items():
            w[key_of[n]][l] = mat
        return landed

    st0 = _fwd_mix(x.reshape(n_ex * seq, d), w, 0, n_ex)
    landed0 = land_weights(gather0, st0["mix"][3], rest_layer0, "gather_layer0_wait")
    gather1, token = _exchange_start(send_w1, True, landed0, "gather_layer1_start")
    w["norm2_g"][0] = tied(row(norm2_g, 0), token)
    x_mid = _fwd_ffn(st0, w, 0, n_ex)
    land_weights(gather1, x_mid, all_layer1, "gather_layer1_wait")
    st1 = _fwd_mix(x_mid, w, 1, n_ex)
    x_out = _fwd_ffn(st1, w, 1, n_ex)
    dx, dxb, d_final_g, loss = _final_loss(x_out, w["final_g"], loss_target.reshape(n_ex * seq, d), "final_loss")
    loss = lax.psum(loss[0, 0], ("x", "y", "c"))

    def send_grads(g, which, after, name):
        slabs = jnp.concatenate([g[key_of[n]].reshape(N_DEV, rows[n], d) for n, _ in which], axis=1)
        return _exchange_start(slabs, False, after, name)

    dx, dxb, g_ffn1 = _bwd_ffn(st1, w, 1, dx, dxb, n_ex)
    dx, dxb, g_mix1 = _bwd_mix(st1, w, 1, dx, dxb, n_ex)
    grads1, token = send_grads({**g_ffn1, **g_mix1}, all_layer1, dx, "exchange_layer1_start")
    w["norm2_g"][0] = tied(row(norm2_g, 0), token)
    dx, dxb, g_ffn0 = _bwd_ffn(st0, w, 0, dx, dxb, n_ex)
    g_ffn0["w_out"] = _mm_tn_parts(st0["mix"], dxb, "out_proj_dw")
    ffn_layer0 = [("w_out", 0), ("w_up", 0), ("w_down", 0)]
    grads0a, token = send_grads(g_ffn0, ffn_layer0, g_ffn0["w_out"], "exchange_ffn0_start")
    dmix = _mm_nt(dxb, w["w_out"][0], "out_proj_dx", after=token)
    dx, dxb, g_mix0 = _bwd_mixers(st0, w, 0, dx, dmix, n_ex)
    mix_layer0 = [("w_in", 0)]
    grads0b, _ = send_grads(g_mix0, mix_layer0, dx, "exchange_mix0_start")
    grad_x = dx.reshape(n_ex, seq, d)
    g = {k: [{**g_ffn0, **g_mix0}[k], {**g_ffn1, **g_mix1}[k]] for k in g_mix0.keys() | g_ffn0.keys()}
    g["final_g"] = d_final_g

    reduced = {}

    def land_grads(handle, after, which, name):
        sent, landed = _exchange_wait(handle, after, name + "_wait")
        own = lax.dynamic_index_in_dim(sent, my, 0, keepdims=False)
        total = _sum_slabs(_with_own(landed, own), name + "_sum")
        reduced.update(split_rows(total, which, False))
        return total

    def stacked_grad(name):
        stacked = jnp.stack([reduced[(name, l)] for l in range(depth)])
        return jnp.swapaxes(stacked, 1, 2) if name in ("w_in", "w_up") else stacked

    done = land_grads(grads1, dx, all_layer1, "exchange_layer1")
    land_grads(grads0a, done, ffn_layer0, "exchange_ffn0")
    grads = {name: stacked_grad(name) for name in ("w_out", "w_up", "w_down")}

    small_local = {
        "norm1_g": jnp.stack([a[0] for a in g["norm1_g"]]), "conv_a_w": jnp.stack(g["conv_a_w"]),
        "conv_b_w": jnp.stack(g["conv_b_w"]), "conv_b_b": jnp.stack([a[0] for a in g["conv_b_b"]]),
        "ln_b_g": jnp.stack([a[0] for a in g["ln_b_g"]]), "ln_b_b": jnp.stack([a[0] for a in g["ln_b_b"]]),
        "ln_c_g": jnp.stack([a[0] for a in g["ln_c_g"]]), "ln_c_b": jnp.stack([a[0] for a in g["ln_c_b"]]),
        "sgu_w": jnp.stack(g["sgu_w"]), "sgu_b": jnp.stack([a.T for a in g["sgu_b_t"]]),
        "norm2_g": jnp.stack([a[0] for a in g["norm2_g"]]), "conv_f_w": jnp.stack(g["conv_f_w"]),
        "final_g": g["final_g"][0],
    }
    small_sum = _sum_slabs(_all_gather(_pack_rows([small_local[k] for k in _SMALL]), "gather_small_grads", dx),
                           "sum_small_grads")
    for name, total in zip(_SMALL, _unpack_rows(small_sum, [small_local[k].shape for k in _SMALL])):
        if name in _CONV_SHARDED:
            width = weights[name].shape[-1]
            total = lax.dynamic_slice_in_dim(total, my * width, width, axis=-1)
        grads[name] = total

    delta, new_m, new_v = {}, {}, {}
    for name in [n for n in _NAMES if n != "w_in"] + ["w_in"]:
        if name == "w_in":
            land_grads(grads0b, new_v["w_up"], mix_layer0, "exchange_mix0")
            grads["w_in"] = stacked_grad("w_in")
        shp = weights[name].shape
        two_d = (-1, shp[-1]) if len(shp) > 1 else (1, shp[0])
        outs = _adamw(weights[name].reshape(two_d), grads[name].reshape(two_d), mom1[name].reshape(two_d),
                      mom2[name].reshape(two_d), "adamw_" + name)
        delta[name], new_m[name], new_v[name] = (o.reshape(shp) for o in outs)

    return (loss, grad_x, *[grads[n] for n in _NAMES], *[delta[n] for n in _NAMES], *[new_m[n] for n in _NAMES],
            *[new_v[n] for n in _NAMES])
```

```python
import functools

import jax
import jax.numpy as jnp
from jax import lax
from jax.experimental import pallas as pl
from jax.experimental.pallas import tpu as pltpu

_F32 = jnp.float32
_BF = jnp.bfloat16

HEAD_DIM = 64
MIX_W = 256
N_HEADS = MIX_W // HEAD_DIM
CHUNK = 128
KV_BLOCK = 128
K_SHORT = 3
K_CONF = 31
K_FFN = 3
RMS_EPS = 1e-6
LN_EPS = 1e-5
ADAM_LR = 0.001
ADAM_B1 = 0.9
ADAM_B2 = 0.999
ADAM_EPS = 1e-08
ADAM_WD = 0.01
ADAM_STEP = 10
N_DEV = 8
VMEM_LIMIT = 56 * 1024 * 1024


def _bf(x):
    return x.astype(_BF)


def _ld(ref):
    return ref[...].astype(_F32)


_ANY_SPEC = pl.BlockSpec(memory_space=pl.ANY)


def _params(*sem):
    return pltpu.CompilerParams(dimension_semantics=sem, vmem_limit_bytes=VMEM_LIMIT)


def _dot(a, b):
    return jnp.dot(a, b, preferred_element_type=_F32)


def _dot_nt(a, b):
    return lax.dot_general(a, b, (((1,), (1,)), ((), ())), preferred_element_type=_F32)


def _dot_tn(a, b):
    return lax.dot_general(a, b, (((0,), (0,)), ((), ())), preferred_element_type=_F32)


def _row_tile(t, want):
    return want if t % want == 0 else t


def _pick_tile(rows, unit, max_rows):
    best = 0
    for cand in range(unit, min(rows, max_rows) + 1, unit):
        if rows % cand == 0:
            best = cand
    return best or rows


def _sigmoid(x):
    return 1.0 / (1.0 + jnp.exp(-x))


def _rms_rstd(x):
    return lax.rsqrt(jnp.mean(x * x, axis=-1, keepdims=True) + RMS_EPS)


def _norm_mm(x, g, w_t, name):
    t, d = x.shape
    n = w_t.shape[0]
    tm = _row_tile(t, 512)
    tn = _row_tile(n, 512)

    def body(x_ref, g_ref, w_ref, p_ref, h_ref):
        xv = x_ref[...]
        h = _bf(xv * _rms_rstd(xv) * g_ref[...])
        h_ref[...] = h
        for n0 in range(0, n, tn):
            p_ref[:, n0:n0 + tn] = _bf(_dot_nt(h, w_ref[n0:n0 + tn, :]))

    return pl.pallas_call(
        body, name=name, grid=(t // tm,),
        in_specs=[pl.BlockSpec((tm, d), lambda i: (i, 0)), pl.BlockSpec((1, d), lambda i: (0, 0)),
                  pl.BlockSpec((n, d), lambda i: (0, 0))],
        out_specs=[pl.BlockSpec((tm, n), lambda i: (i, 0)), pl.BlockSpec((tm, d), lambda i: (i, 0))],
        out_shape=[jax.ShapeDtypeStruct((t, n), _BF), jax.ShapeDtypeStruct((t, d), _BF)],
        compiler_params=_params("parallel"),
    )(x, g, w_t)


def _mm_nt(a, w_t, name, after=None):
    t, k = a.shape
    n = w_t.shape[0]
    tm = _row_tile(t, 512)
    tn = _row_tile(n, 512) if n % 512 == 0 else _row_tile(n, 256)

    def body(a_ref, w_ref, *rest):
        o_ref = rest[-1]
        av = a_ref[...]
        for n0 in range(0, n, tn):
            o_ref[:, n0:n0 + tn] = _bf(_dot_nt(av, w_ref[n0:n0 + tn, :]))

    extra = () if after is None else (after,)
    return pl.pallas_call(
        body, name=name, grid=(t // tm,),
        in_specs=[pl.BlockSpec((tm, k), lambda i: (i, 0)), pl.BlockSpec((n, k), lambda i: (0, 0))] + [_ANY_SPEC] * len(extra),
        out_specs=pl.BlockSpec((tm, n), lambda i: (i, 0)),
        out_shape=jax.ShapeDtypeStruct((t, n), _BF),
        compiler_params=_params("parallel"),
    )(a, w_t, *extra)


def _mm_res(parts, w, x, name):
    t = x.shape[0]
    k, d = w.shape
    tm = _row_tile(t, 512)
    widths = [a.shape[1] for a in parts]
    n_parts = len(parts)

    def body(*refs):
        w_ref, x_ref, o_ref = refs[n_parts:]
        acc, off = x_ref[...], 0
        for a_ref, width in zip(refs[:n_parts], widths):
            acc = acc + _dot(a_ref[...], w_ref[off:off + width, :])
            off += width
        o_ref[...] = acc

    return pl.pallas_call(
        body, name=name, grid=(t // tm,),
        in_specs=[pl.BlockSpec((tm, width), lambda i: (i, 0)) for width in widths] + [
            pl.BlockSpec((k, d), lambda i: (0, 0)), pl.BlockSpec((tm, d), lambda i: (i, 0))],
        out_specs=pl.BlockSpec((tm, d), lambda i: (i, 0)),
        out_shape=jax.ShapeDtypeStruct((t, d), _F32),
        compiler_params=_params("parallel"),
    )(*parts, w, x)


def _mm_normbwd(parts, w, x, g, dres, name):
    t = x.shape[0]
    k, d = w.shape
    tm = _row_tile(t, 512)
    widths = [a.shape[1] for a in parts]
    n_parts = len(parts)

    def body(*refs):
        a_refs = refs[:n_parts]
        w_ref, x_ref, g_ref, r_ref, dx_ref, dxb_ref, dg_ref = refs[n_parts:]
        dh, off = None, 0
        for a_ref, width in zip(a_refs, widths):
            term = _dot(_bf(a_ref[...]), w_ref[off:off + width, :])
            dh = term if dh is None else dh + term
            off += width
        xv = x_ref[...]
        rstd = _rms_rstd(xv)
        xn = xv * rstd
        u = dh * g_ref[...]
        dx = r_ref[...] + rstd * (u - xn * jnp.mean(u * xn, axis=-1, keepdims=True))
        dx_ref[...] = dx
        dxb_ref[...] = _bf(dx)

        @pl.when(pl.program_id(0) == 0)
        def _():
            dg_ref[...] = jnp.zeros_like(dg_ref)

        dg_ref[...] += jnp.sum(dh * xn, axis=0, keepdims=True)

    return pl.pallas_call(
        body, name=name, grid=(t // tm,),
        in_specs=[pl.BlockSpec((tm, width), lambda i: (i, 0)) for width in widths] + [
            pl.BlockSpec((k, d), lambda i: (0, 0)),
            pl.BlockSpec((tm, d), lambda i: (i, 0)), pl.BlockSpec((1, d), lambda i: (0, 0)),
            pl.BlockSpec((tm, d), lambda i: (i, 0))],
        out_specs=[pl.BlockSpec((tm, d), lambda i: (i, 0)), pl.BlockSpec((tm, d), lambda i: (i, 0)),
                   pl.BlockSpec((1, d), lambda i: (0, 0))],
        out_shape=[jax.ShapeDtypeStruct((t, d), _F32), jax.ShapeDtypeStruct((t, d), _BF),
                   jax.ShapeDtypeStruct((1, d), _F32)],
        compiler_params=_params("arbitrary"),
    )(*parts, w, x, g, dres)


def _mm_tn(a, b, name, out_dtype):
    t, m = a.shape
    n = b.shape[1]
    tm = _pick_tile(m, 128, 1408)
    tn = _pick_tile(n, 128, 1024)
    tk = _row_tile(t, 1024)
    nk = t // tk

    def body(a_ref, b_ref, o_ref, acc):
        kk = pl.program_id(2)

        @pl.when(kk == 0)
        def _():
            acc[...] = jnp.zeros_like(acc)

        acc[...] += _dot_tn(_bf(a_ref[...]), b_ref[...])

        @pl.when(kk == nk - 1)
        def _():
            o_ref[...] = acc[...].astype(o_ref.dtype)

    return pl.pallas_call(
        body, name=name, grid=(m // tm, n // tn, nk),
        in_specs=[pl.BlockSpec((tk, tm), lambda i, j, kk: (kk, i)), pl.BlockSpec((tk, tn), lambda i, j, kk: (kk, j))],
        out_specs=pl.BlockSpec((tm, tn), lambda i, j, kk: (i, j)),
        out_shape=jax.ShapeDtypeStruct((m, n), out_dtype),
        scratch_shapes=[pltpu.VMEM((tm, tn), _F32)],
        compiler_params=_params("parallel", "parallel", "arbitrary"),
    )(a, b)


def _mm_tn_parts(parts, b, name):
    t, n = b.shape
    widths = [a.shape[1] for a in parts]
    m = sum(widths)
    n_parts = len(parts)
    tk = _row_tile(t, 1024)
    nk = t // tk

    def body(*refs):
        b_ref, o_ref, acc = refs[n_parts:]
        kk = pl.program_id(0)

        @pl.when(kk == 0)
        def _():
            acc[...] = jnp.zeros_like(acc)

        bv = b_ref[...]
        off = 0
        for a_ref, width in zip(refs[:n_parts], widths):
            acc[off:off + width, :] += _dot_tn(_bf(a_ref[...]), bv)
            off += width

        @pl.when(kk == nk - 1)
        def _():
            o_ref[...] = _bf(acc[...])

    return pl.pallas_call(
        body, name=name, grid=(nk,),
        in_specs=[pl.BlockSpec((tk, width), lambda kk: (kk, 0)) for width in widths] + [pl.BlockSpec((tk, n), lambda kk: (kk, 0))],
        out_specs=pl.BlockSpec((m, n), lambda kk: (0, 0)),
        out_shape=jax.ShapeDtypeStruct((m, n), _BF),
        scratch_shapes=[pltpu.VMEM((m, n), _F32)],
        compiler_params=_params("arbitrary"),
    )(*parts, b)


def _final_loss(x, g, target, name):
    t, d = x.shape
    tm = _row_tile(t, 256)

    def body(x_ref, g_ref, t_ref, dx_ref, dxb_ref, dg_ref, loss_ref):
        xv = x_ref[...]
        rstd = _rms_rstd(xv)
        xn = xv * rstd
        err = xn * g_ref[...] - t_ref[...]
        dy = err * (1.0 / d)
        u = dy * g_ref[...]
        dx = rstd * (u - xn * jnp.mean(u * xn, axis=-1, keepdims=True))
        dx_ref[...] = dx
        dxb_ref[...] = _bf(dx)

        @pl.when(pl.program_id(0) == 0)
        def _():
            dg_ref[...] = jnp.zeros_like(dg_ref)
            loss_ref[...] = jnp.zeros_like(loss_ref)

        dg_ref[...] += jnp.sum(dy * xn, axis=0, keepdims=True)
        loss_ref[...] += (0.5 / d) * jnp.sum(jnp.sum(err * err, axis=1, keepdims=True), axis=0, keepdims=True)

    return pl.pallas_call(
        body, name=name, grid=(t // tm,),
        in_specs=[pl.BlockSpec((tm, d), lambda i: (i, 0)), pl.BlockSpec((1, d), lambda i: (0, 0)),
                  pl.BlockSpec((tm, d), lambda i: (i, 0))],
        out_specs=[pl.BlockSpec((tm, d), lambda i: (i, 0)), pl.BlockSpec((tm, d), lambda i: (i, 0)),
                   pl.BlockSpec((1, d), lambda i: (0, 0)), pl.BlockSpec((1, 1), lambda i: (0, 0))],
        out_shape=[jax.ShapeDtypeStruct((t, d), _F32), jax.ShapeDtypeStruct((t, d), _BF),
                   jax.ShapeDtypeStruct((1, d), _F32), jax.ShapeDtypeStruct((1, 1), _F32)],
        compiler_params=_params("arbitrary"),
    )(x, g, target)


def _pad_rows(x, pad):
    return jnp.concatenate([x, jnp.zeros((pad, x.shape[1]), x.dtype)], axis=0)


def _shift_down(xp, s):
    return xp if s == 0 else pltpu.roll(xp, s, 0)


def _shift_up(xp, s):
    return xp if s == 0 else pltpu.roll(xp, xp.shape[0] - s, 0)


def _taps3(xp):
    one = _shift_down(xp, 1)
    return xp, one, _shift_down(one, 1)


def _conv3_taps(taps, w_ref):
    return w_ref[2:3, :] * taps[0] + w_ref[1:2, :] * taps[1] + w_ref[0:1, :] * taps[2]


def _conv3(xp, w_ref):
    return _conv3_taps(_taps3(xp), w_ref)


def _conv3_t(dyp, w_ref):
    one = _shift_up(dyp, 1)
    return w_ref[2:3, :] * dyp + w_ref[1:2, :] * one + w_ref[0:1, :] * _shift_up(one, 1)


def _conv3_dw(dyp, taps):
    return [jnp.sum(dyp * taps[2 - k], axis=0, keepdims=True) for k in range(3)]


def _ffn_mid_fwd(up_pre, wf, n_ex, name):
    t, f2 = up_pre.shape
    f = f2 // 2
    s = t // n_ex
    cb = MIX_W
    nb = f // cb

    def body(ug_ref, uv_ref, wg_ref, wv_ref, act_ref, gf_ref, vf_ref):
        gf = _conv3(_pad_rows(ug_ref[...].astype(_F32), 8), wg_ref)[:s]
        vf = _conv3(_pad_rows(uv_ref[...].astype(_F32), 8), wv_ref)[:s]
        act_ref[...] = _bf(gf * _sigmoid(gf) * vf)
        gf_ref[...] = _bf(gf)
        vf_ref[...] = _bf(vf)

    out = pl.BlockSpec((s, cb), lambda e, j: (e, j))
    return pl.pallas_call(
        body, name=name, grid=(n_ex, nb),
        in_specs=[pl.BlockSpec((s, cb), lambda e, j: (e, j)), pl.BlockSpec((s, cb), lambda e, j: (e, j + nb)),
                  pl.BlockSpec((K_FFN, cb), lambda e, j: (0, j)), pl.BlockSpec((K_FFN, cb), lambda e, j: (0, j + nb))],
        out_specs=[out, out, out],
        out_shape=[jax.ShapeDtypeStruct((t, f), _BF)] * 3,
        compiler_params=_params("parallel", "parallel"),
    )(up_pre, up_pre, wf, wf)


def _ffn_mid_bwd(up_pre, conv_g, conv_v, wf, dact, n_ex, name):
    t, f2 = up_pre.shape
    f = f2 // 2
    s = t // n_ex
    cb = MIX_W
    nb = f // cb

    def body(ug_ref, uv_ref, gf_ref, vf_ref, wg_ref, wv_ref, da_ref, dug_ref, duv_ref, dwg_ref, dwv_ref):
        gf = _ld(gf_ref)
        vf = _ld(vf_ref)
        sg = _sigmoid(gf)
        da = _ld(da_ref)

        @pl.when(pl.program_id(1) == 0)
        def _():
            dwg_ref[...] = jnp.zeros_like(dwg_ref)
            dwv_ref[...] = jnp.zeros_like(dwv_ref)

        def finish(dpost, w_ref, x_ref, du_ref, dw_ref):
            ahead = [_pad_rows(dpost, 8)]
            ahead.append(_shift_up(ahead[0], 1))
            ahead.append(_shift_up(ahead[1], 1))
            du_ref[...] = _bf((w_ref[2:3, :] * ahead[0] + w_ref[1:2, :] * ahead[1] + w_ref[0:1, :] * ahead[2])[:s])
            x = _ld(x_ref)
            for k in range(K_FFN):
                dw_ref[k:k + 1, :] += jnp.sum(ahead[2 - k][:s] * x, axis=0, keepdims=True)

        finish(da * vf * sg * (1.0 + gf * (1.0 - sg)), wg_ref, ug_ref, dug_ref, dwg_ref)
        finish(da * gf * sg, wv_ref, uv_ref, duv_ref, dwv_ref)

    return pl.pallas_call(
        body, name=name, grid=(nb, n_ex),
        in_specs=[pl.BlockSpec((s, cb), lambda j, e: (e, j)), pl.BlockSpec((s, cb), lambda j, e: (e, j + nb)),
                  pl.BlockSpec((s, cb), lambda j, e: (e, j)), pl.BlockSpec((s, cb), lambda j, e: (e, j)),
                  pl.BlockSpec((K_FFN, cb), lambda j, e: (0, j)), pl.BlockSpec((K_FFN, cb), lambda j, e: (0, j + nb)),
                  pl.BlockSpec((s, cb), lambda j, e: (e, j))],
        out_specs=[pl.BlockSpec((s, cb), lambda j, e: (e, j)), pl.BlockSpec((s, cb), lambda j, e: (e, j)),
                   pl.BlockSpec((K_FFN, cb), lambda j, e: (0, j)), pl.BlockSpec((K_FFN, cb), lambda j, e: (0, j))],
        out_shape=[jax.ShapeDtypeStruct((t, f), _BF), jax.ShapeDtypeStruct((t, f), _BF),
                   jax.ShapeDtypeStruct((K_FFN, f), _F32), jax.ShapeDtypeStruct((K_FFN, f), _F32)],
        compiler_params=_params("parallel", "arbitrary"),
    )(up_pre, up_pre, conv_g, conv_v, wf, wf, dact)


FFN_HALO = 8
FFN_TILE = 512
FFN_TILE_BWD = 256


def _resident(shape):
    return pl.BlockSpec(shape, lambda i: (0,) * len(shape), pipeline_mode=pl.Buffered(1))


def _ffn_fwd(x1, g2, w_up_t, wf, w_down, n_ex, name):
    t, d = x1.shape
    f2 = w_up_t.shape[0]
    f = f2 // 2
    s = t // n_ex
    tm = _row_tile(s, FFN_TILE)
    tiles_per_ex = s // tm
    cb = MIX_W
    nb = f // cb

    def body(x_ref, g_ref, wu_ref, wf_ref, wd_ref, x2_ref, h_ref, up_ref, act_ref, carry):
        @pl.when(pl.program_id(0) % tiles_per_ex == 0)
        def _():
            carry[...] = jnp.zeros_like(carry)

        xv = x_ref[...]
        h = _bf(xv * _rms_rstd(xv) * g_ref[...])
        h_ref[...] = h
        acc = xv
        for j in range(nb):
            conv = []
            for half in range(2):
                cols = slice(half * f + j * cb, half * f + (j + 1) * cb)
                u = _bf(_dot_nt(h, wu_ref[cols, :]))
                up_ref[:, cols] = u
                ext = jnp.concatenate([carry[:, cols], u.astype(_F32)], axis=0)
                carry[:, cols] = ext[tm:, :]
                one = pltpu.roll(ext, 1, 0)
                two = pltpu.roll(one, 1, 0)
                conv.append((wf_ref[2:3, cols] * ext + wf_ref[1:2, cols] * one + wf_ref[0:1, cols] * two)[FFN_HALO:])
            a = _bf(conv[0] * _sigmoid(conv[0]) * conv[1])
            act_ref[:, j * cb:(j + 1) * cb] = a
            acc = acc + _dot(a, wd_ref[j * cb:(j + 1) * cb, :])
        x2_ref[...] = acc

    row = lambda width: pl.BlockSpec((tm, width), lambda i: (i, 0))
    return pl.pallas_call(
        body, name=name, grid=(t // tm,),
        in_specs=[row(d), pl.BlockSpec((1, d), lambda i: (0, 0)), _resident((f2, d)),
                  pl.BlockSpec((K_FFN, f2), lambda i: (0, 0)), _resident((f, d))],
        out_specs=[row(d), row(d), row(f2), row(f)],
        out_shape=[jax.ShapeDtypeStruct((t, d), _F32), jax.ShapeDtypeStruct((t, d), _BF),
                   jax.ShapeDtypeStruct((t, f2), _BF), jax.ShapeDtypeStruct((t, f), _BF)],
        scratch_shapes=[pltpu.VMEM((FFN_HALO, f2), _F32)],
        compiler_params=_params("arbitrary"),
    )(x1, g2, w_up_t, wf, w_down)


def _ffn_bwd(dxb, dres, w_down, up, wf, w_up_t, x1, g2, n_ex, name):
    t, d = x1.shape
    f2 = w_up_t.shape[0]
    f = f2 // 2
    s = t // n_ex
    tm = _row_tile(s, FFN_TILE_BWD)
    tiles_per_ex = s // tm
    n_tiles = t // tm
    cb = MIX_W
    nb = f // cb
    n_ext = tm + FFN_HALO

    def body(dxb_ref, dres_ref, wd_ref, up_ref, halo_ref, wf_ref, wu_ref, x_ref, g_ref,
             dx_ref, dxo_ref, dup_ref, dwf_ref, dg_ref, carry):
        i = pl.program_id(0)
        r = n_tiles - 1 - i
        first_of_example = r % tiles_per_ex == 0

        @pl.when(r % tiles_per_ex == tiles_per_ex - 1)
        def _():
            carry[...] = jnp.zeros_like(carry)

        @pl.when(i == 0)
        def _():
            dwf_ref[...] = jnp.zeros_like(dwf_ref)
            dg_ref[...] = jnp.zeros_like(dg_ref)

        dxv = dxb_ref[...]
        dh = jnp.zeros((tm, d), _F32)
        for j in range(nb):
            da = _bf(_dot_nt(dxv, wd_ref[j * cb:(j + 1) * cb, :])).astype(_F32)
            taps, conv = [], []
            for half in range(2):
                cols = slice(half * f + j * cb, half * f + (j + 1) * cb)
                halo = jnp.where(first_of_example, 0.0, halo_ref[:, cols].astype(_F32))
                ext = jnp.concatenate([halo, up_ref[:, cols].astype(_F32)], axis=0)
                one = pltpu.roll(ext, 1, 0)
                two = pltpu.roll(one, 1, 0)
                taps.append((ext[FFN_HALO:], one[FFN_HALO:], two[FFN_HALO:]))
                conv.append(wf_ref[2:3, cols] * taps[half][0] + wf_ref[1:2, cols] * taps[half][1]
                            + wf_ref[0:1, cols] * taps[half][2])
            gf, vf = conv
            sg = _sigmoid(gf)
            for half, dpost in enumerate((da * vf * sg * (1.0 + gf * (1.0 - sg)), da * gf * sg)):
                cols = slice(half * f + j * cb, half * f + (j + 1) * cb)
                ext = jnp.concatenate([dpost, carry[:, cols]], axis=0)
                carry[:, cols] = dpost[:FFN_HALO]
                one = pltpu.roll(ext, n_ext - 1, 0)
                two = pltpu.roll(one, n_ext - 1, 0)
                dpre = _bf((wf_ref[2:3, cols] * ext + wf_ref[1:2, cols] * one + wf_ref[0:1, cols] * two)[:tm])
                dup_ref[:, cols] = dpre
                dh = dh + _dot(dpre, wu_ref[cols, :])
                for k in range(K_FFN):
                    dwf_ref[k:k + 1, cols] += jnp.sum(dpost * taps[half][2 - k], axis=0, keepdims=True)

        xv = x_ref[...]
        rstd = _rms_rstd(xv)
        xn = xv * rstd
        u = dh * g_ref[...]
        dx = dres_ref[...] + rstd * (u - xn * jnp.mean(u * xn, axis=-1, keepdims=True))
        dx_ref[...] = dx
        dxo_ref[...] = _bf(dx)
        dg_ref[...] += jnp.sum(dh * xn, axis=0, keepdims=True)

    rev = lambda width: pl.BlockSpec((tm, width), lambda i: (n_tiles - 1 - i, 0))
    halo_blocks = tm // FFN_HALO
    halo_spec = pl.BlockSpec((FFN_HALO, f2), lambda i: (jnp.maximum((n_tiles - 1 - i) * halo_blocks - 1, 0), 0))
    const = lambda shape: pl.BlockSpec(shape, lambda i: (0, 0))
    return pl.pallas_call(
        body, name=name, grid=(n_tiles,),
        in_specs=[rev(d), rev(d), _resident((f, d)), rev(f2), halo_spec, const((K_FFN, f2)), _resident((f2, d)), rev(d),
                  const((1, d))],
        out_specs=[rev(d), rev(d), rev(f2), const((K_FFN, f2)), const((1, d))],
        out_shape=[jax.ShapeDtypeStruct((t, d), _F32), jax.ShapeDtypeStruct((t, d), _BF), jax.ShapeDtypeStruct((t, f2), _BF),
                   jax.ShapeDtypeStruct((K_FFN, f2), _F32), jax.ShapeDtypeStruct((1, d), _F32)],
        scratch_shapes=[pltpu.VMEM((FFN_HALO, f2), _F32)],
        compiler_params=_params("arbitrary"),
    )(dxb, dres, w_down, up, up, wf, w_up_t, x1, g2)


def _pcol(s, j):
    return pl.BlockSpec((s, MIX_W), lambda e, j=j: (e, j))


def _vec(rows=1):
    return pl.BlockSpec((rows, MIX_W), lambda e: (0, 0))


def _mix_a_fwd(p, wa, n_ex, name):
    t = p.shape[0]
    s = t // n_ex

    def body(gb_ref, gc_ref, ha_ref, w_ref, y_ref):
        cv = _conv3(_pad_rows(_ld(gc_ref) * _ld(ha_ref), 8), w_ref)[:s]
        y_ref[...] = _bf(_ld(gb_ref) * cv)

    return pl.pallas_call(
        body, name=name, grid=(n_ex,),
        in_specs=[_pcol(s, 0), _pcol(s, 1), _pcol(s, 2), _vec(K_SHORT)],
        out_specs=pl.BlockSpec((s, MIX_W), lambda e: (e, 0)),
        out_shape=jax.ShapeDtypeStruct((t, MIX_W), _BF),
        compiler_params=_params("parallel"),
    )(p, p, p, wa)


def _mix_a_bwd(p, wa, dmix, n_ex, name):
    t = p.shape[0]
    s = t // n_ex

    def body(gb_ref, gc_ref, ha_ref, w_ref, dy_ref, dp_ref, dw_ref):
        gc = _ld(gc_ref)
        ha = _ld(ha_ref)
        up = _taps3(_pad_rows(gc * ha, 8))
        cv = _conv3_taps(up, w_ref)[:s]
        dy = _ld(dy_ref)
        dcvp = _pad_rows(dy * _ld(gb_ref), 8)
        du = _conv3_t(dcvp, w_ref)[:s]
        dp_ref[:, 0:MIX_W] = _bf(dy * cv)
        dp_ref[:, MIX_W:2 * MIX_W] = _bf(du * ha)
        dp_ref[:, 2 * MIX_W:3 * MIX_W] = _bf(du * gc)

        @pl.when(pl.program_id(0) == 0)
        def _():
            dw_ref[...] = jnp.zeros_like(dw_ref)

        rows = _conv3_dw(dcvp, up)
        for k in range(3):
            dw_ref[k:k + 1, :] += rows[k]

    return pl.pallas_call(
        body, name=name, grid=(n_ex,),
        in_specs=[_pcol(s, 0), _pcol(s, 1), _pcol(s, 2), _vec(K_SHORT), _pcol(s, 0)],
        out_specs=[pl.BlockSpec((s, 3 * MIX_W), lambda e: (e, 0)), _vec(K_SHORT)],
        out_shape=[jax.ShapeDtypeStruct((t, 3 * MIX_W), _BF), jax.ShapeDtypeStruct((K_SHORT, MIX_W), _F32)],
        compiler_params=_params("arbitrary"),
    )(p, p, p, wa, dmix)


CONF_PAD = 32
CONF_ROWS = 64
_CONF_LANES = (slice(0, 128), slice(128, 256))


def _conf_taps(win, ahead):
    n = CONF_ROWS + CONF_PAD
    for b in range(8):
        rot = win if b == 0 else pltpu.roll(win, (n - b) if ahead else b, 0)
        for a in range(4):
            if 8 * a + b < K_CONF:
                yield rot, 8 * a + b, (8 * a) if ahead else (CONF_PAD - 8 * a)


def _ln_fwd(x, g, b):
    mu = jnp.mean(x, axis=-1, keepdims=True)
    xc = x - mu
    rstd = lax.rsqrt(jnp.mean(xc * xc, axis=-1, keepdims=True) + LN_EPS)
    xhat = xc * rstd
    return xhat * g + b, xhat, rstd


def _ln_bwd(dy, xhat, rstd, g):
    dxh = dy * g
    return rstd * (dxh - jnp.mean(dxh, axis=-1, keepdims=True) - xhat * jnp.mean(dxh * xhat, axis=-1, keepdims=True))


def _mix_b_fwd(p, wb, bb, lg, lb, n_ex, name):
    t = p.shape[0]
    s = t // n_ex

    def body(val_ref, gat_ref, w_ref, bb_ref, lg_ref, lb_ref, y_ref, cb_ref, xpad):
        xpad[0:CONF_PAD, :] = jnp.zeros((CONF_PAD, MIX_W), _F32)
        xpad[CONF_PAD:, :] = _ld(val_ref) * _sigmoid(_ld(gat_ref))

        def chunk(c, carry):
            r0 = pl.multiple_of(c * CONF_ROWS, CONF_ROWS)
            for lanes in _CONF_LANES:
                acc = None
                for rot, sh, lo in _conf_taps(xpad[pl.ds(r0, CONF_ROWS + CONF_PAD), lanes], False):
                    term = w_ref[K_CONF - 1 - sh:K_CONF - sh, lanes] * rot[lo:lo + CONF_ROWS]
                    acc = term if acc is None else acc + term
                cb_ref[pl.ds(r0, CONF_ROWS), lanes] = acc + bb_ref[:, lanes]
            return carry

        lax.fori_loop(0, s // CONF_ROWS, chunk, 0)
        yl, _, _ = _ln_fwd(cb_ref[...], lg_ref[...], lb_ref[...])
        y_ref[...] = _bf(yl * _sigmoid(yl))

    return pl.pallas_call(
        body, name=name, grid=(n_ex,),
        in_specs=[_pcol(s, 3), _pcol(s, 4), _vec(K_CONF), _vec(), _vec(), _vec()],
        out_specs=[pl.BlockSpec((s, MIX_W), lambda e: (e, 0)), pl.BlockSpec((s, MIX_W), lambda e: (e, 0))],
        out_shape=[jax.ShapeDtypeStruct((t, MIX_W), _BF), jax.ShapeDtypeStruct((t, MIX_W), _F32)],
        scratch_shapes=[pltpu.VMEM((CONF_PAD + s, MIX_W), _F32)],
        compiler_params=_params("parallel"),
    )(p, p, wb, bb, lg, lb)


def _mix_b_bwd(p, cb, wb, lg, lb, dmix, n_ex, name):
    t = p.shape[0]
    s = t // n_ex

    def body(val_ref, gat_ref, cb_ref, w_ref, lg_ref, lb_ref, dy_ref, dp_ref, dw_ref, dbb_ref, dlg_ref, dlb_ref,
             xpad, dpad, dglu_s, dw_acc):
        @pl.when(pl.program_id(0) == 0)
        def _():
            for r in (dw_ref, dbb_ref, dlg_ref, dlb_ref):
                r[...] = jnp.zeros_like(r)

        yl, xhat, rstd = _ln_fwd(cb_ref[...], lg_ref[...], lb_ref[...])
        sy = _sigmoid(yl)
        dyl = _ld(dy_ref) * sy * (1.0 + yl * (1.0 - sy))
        dlg_ref[...] += jnp.sum(dyl * xhat, axis=0, keepdims=True)
        dlb_ref[...] += jnp.sum(dyl, axis=0, keepdims=True)
        dcb = _ln_bwd(dyl, xhat, rstd, lg_ref[...])
        dbb_ref[...] += jnp.sum(dcb, axis=0, keepdims=True)

        val = _ld(val_ref)
        sg = _sigmoid(_ld(gat_ref))
        xpad[0:CONF_PAD, :] = jnp.zeros((CONF_PAD, MIX_W), _F32)
        xpad[CONF_PAD:, :] = val * sg
        dpad[0:s, :] = dcb
        dpad[s:, :] = jnp.zeros((CONF_PAD, MIX_W), _F32)
        dw_acc[...] = jnp.zeros_like(dw_acc)

        def chunk(c, carry):
            r0 = pl.multiple_of(c * CONF_ROWS, CONF_ROWS)
            for lanes in _CONF_LANES:
                d_win = dpad[pl.ds(r0, CONF_ROWS + CONF_PAD), lanes]
                d_rows = d_win[0:CONF_ROWS]
                acc = None
                for rot, sh, lo in _conf_taps(d_win, True):
                    term = w_ref[K_CONF - 1 - sh:K_CONF - sh, lanes] * rot[lo:lo + CONF_ROWS]
                    acc = term if acc is None else acc + term
                dglu_s[pl.ds(r0, CONF_ROWS), lanes] = acc
                for rot, sh, lo in _conf_taps(xpad[pl.ds(r0, CONF_ROWS + CONF_PAD), lanes], False):
                    prod = d_rows * rot[lo:lo + CONF_ROWS]
                    dw_acc[K_CONF - 1 - sh, :, lanes] += jnp.sum(prod.reshape(CONF_ROWS // 8, 8, 128), axis=0)
            return carry

        lax.fori_loop(0, s // CONF_ROWS, chunk, 0)
        dw_ref[...] += jnp.sum(dw_acc[...], axis=1)
        dglu = dglu_s[...]
        dp_ref[:, 0:MIX_W] = _bf(dglu * sg)
        dp_ref[:, MIX_W:2 * MIX_W] = _bf(dglu * val * sg * (1.0 - sg))

    return pl.pallas_call(
        body, name=name, grid=(n_ex,),
        in_specs=[_pcol(s, 3), _pcol(s, 4), pl.BlockSpec((s, MIX_W), lambda e: (e, 0)), _vec(K_CONF), _vec(), _vec(),
                  _pcol(s, 1)],
        out_specs=[pl.BlockSpec((s, 2 * MIX_W), lambda e: (e, 0)), _vec(K_CONF), _vec(), _vec(), _vec()],
        out_shape=[jax.ShapeDtypeStruct((t, 2 * MIX_W), _BF), jax.ShapeDtypeStruct((K_CONF, MIX_W), _F32),
                   jax.ShapeDtypeStruct((1, MIX_W), _F32), jax.ShapeDtypeStruct((1, MIX_W), _F32),
                   jax.ShapeDtypeStruct((1, MIX_W), _F32)],
        scratch_shapes=[pltpu.VMEM((CONF_PAD + s, MIX_W), _F32), pltpu.VMEM((s + CONF_PAD, MIX_W), _F32),
                        pltpu.VMEM((s, MIX_W), _F32), pltpu.VMEM((K_CONF, 8, MIX_W), _F32)],
        compiler_params=_params("arbitrary"),
    )(p, p, cb, wb, lg, lb, dmix)


_INV_SQRT2 = 0.7071067811865476
_INV_SQRT2PI = 0.3989422804014327


def _gelu(x):
    return 0.5 * x * (1.0 + lax.erf(x * _INV_SQRT2))


def _gelu_grad(x):
    return 0.5 * (1.0 + lax.erf(x * _INV_SQRT2)) + x * _INV_SQRT2PI * jnp.exp(-0.5 * x * x)


def _head_masks(width=MIX_W):
    lane = lax.broadcasted_iota(jnp.int32, (1, width), 1)
    return [(lane >= h * HEAD_DIM) & (lane < (h + 1) * HEAD_DIM) for h in range(N_HEADS)]


def _tril_mask():
    r = lax.broadcasted_iota(jnp.int32, (CHUNK, CHUNK), 0)
    c = lax.broadcasted_iota(jnp.int32, (CHUNK, CHUNK), 1)
    return c <= r


def _sgu_apply(ws_ref, x3, transpose):
    n = x3.shape[0]
    tril = _tril_mask()
    masks = _head_masks()
    xb = _bf(x3)
    out = jnp.zeros(x3.shape, _F32)
    for h in range(N_HEADS):
        w = _bf(jnp.where(tril, ws_ref[h], 0.0))
        wb = jnp.broadcast_to(w[None], (n, CHUNK, CHUNK))
        dims = (((1,), (1,)), ((0,), (0,))) if transpose else (((2,), (1,)), ((0,), (0,)))
        r = lax.dot_general(wb, xb, dims, preferred_element_type=_F32)
        out = out + jnp.where(masks[h][None], r, 0.0)
    return out


def _mix_c_fwd(p, lg, lb, ws, sb_full, n_ex, name):
    t = p.shape[0]
    s = t // n_ex
    nc = s // CHUNK

    def body(pu_ref, pv_ref, lg_ref, lb_ref, ws_ref, sb_ref, y_ref):
        u = _gelu(_ld(pu_ref))
        vl, _, _ = _ln_fwd(_gelu(_ld(pv_ref)), lg_ref[...], lb_ref[...])
        sp = _sgu_apply(ws_ref, vl.reshape(nc, CHUNK, MIX_W), False) + sb_ref[...][None]
        y_ref[...] = _bf(u * sp.reshape(s, MIX_W))

    return pl.pallas_call(
        body, name=name, grid=(n_ex,),
        in_specs=[_pcol(s, 5), _pcol(s, 6), _vec(), _vec(),
                  pl.BlockSpec((N_HEADS, CHUNK, CHUNK), lambda e: (0, 0, 0)), pl.BlockSpec((CHUNK, MIX_W), lambda e: (0, 0))],
        out_specs=pl.BlockSpec((s, MIX_W), lambda e: (e, 0)),
        out_shape=jax.ShapeDtypeStruct((t, MIX_W), _BF),
        compiler_params=_params("parallel"),
    )(p, p, lg, lb, ws, sb_full)


def _mix_c_bwd(p, lg, lb, ws, sb_full, dmix, n_ex, name):
    t = p.shape[0]
    s = t // n_ex
    nc = s // CHUNK

    def body(pu_ref, pv_ref, lg_ref, lb_ref, ws_ref, sb_ref, dy_ref, dp_ref, dlg_ref, dlb_ref, dws_ref, dsb_ref):
        @pl.when(pl.program_id(0) == 0)
        def _():
            for r in (dlg_ref, dlb_ref, dws_ref, dsb_ref):
                r[...] = jnp.zeros_like(r)

        pu = _ld(pu_ref)
        pv = _ld(pv_ref)
        u = _gelu(pu)
        vl, xhat, rstd = _ln_fwd(_gelu(pv), lg_ref[...], lb_ref[...])
        vl3 = vl.reshape(nc, CHUNK, MIX_W)
        sp = _sgu_apply(ws_ref, vl3, False) + sb_ref[...][None]
        dy = _ld(dy_ref)
        dp_ref[:, 0:MIX_W] = _bf(dy * sp.reshape(s, MIX_W) * _gelu_grad(pu))
        dsp3 = (dy * u).reshape(nc, CHUNK, MIX_W)
        dsb_full = jnp.sum(dsp3, axis=0)
        masks = _head_masks()
        tril = _tril_mask()
        dspb = _bf(dsp3)
        vlb = _bf(vl3)
        for h in range(N_HEADS):
            dsb_ref[:, h:h + 1] += jnp.sum(jnp.where(masks[h], dsb_full, 0.0), axis=1, keepdims=True)
            dm = jnp.where(masks[h][None], dspb, jnp.zeros_like(dspb))
            g3 = lax.dot_general(dm, vlb, (((2,), (2,)), ((0,), (0,))), preferred_element_type=_F32)
            dws_ref[h] += jnp.where(tril, jnp.sum(g3, axis=0), 0.0)
        dvl = _sgu_apply(ws_ref, dsp3, True).reshape(s, MIX_W)
        dlg_ref[...] += jnp.sum(dvl * xhat, axis=0, keepdims=True)
        dlb_ref[...] += jnp.sum(dvl, axis=0, keepdims=True)
        dp_ref[:, MIX_W:2 * MIX_W] = _bf(_ln_bwd(dvl, xhat, rstd, lg_ref[...]) * _gelu_grad(pv))

    return pl.pallas_call(
        body, name=name, grid=(n_ex,),
        in_specs=[_pcol(s, 5), _pcol(s, 6), _vec(), _vec(),
                  pl.BlockSpec((N_HEADS, CHUNK, CHUNK), lambda e: (0, 0, 0)), pl.BlockSpec((CHUNK, MIX_W), lambda e: (0, 0)),
                  _pcol(s, 2)],
        out_specs=[pl.BlockSpec((s, 2 * MIX_W), lambda e: (e, 0)), _vec(), _vec(),
                   pl.BlockSpec((N_HEADS, CHUNK, CHUNK), lambda e: (0, 0, 0)), pl.BlockSpec((CHUNK, N_HEADS), lambda e: (0, 0))],
        out_shape=[jax.ShapeDtypeStruct((t, 2 * MIX_W), _BF), jax.ShapeDtypeStruct((1, MIX_W), _F32),
                   jax.ShapeDtypeStruct((1, MIX_W), _F32), jax.ShapeDtypeStruct((N_HEADS, CHUNK, CHUNK), _F32),
                   jax.ShapeDtypeStruct((CHUNK, N_HEADS), _F32)],
        compiler_params=_params("arbitrary"),
    )(p, p, lg, lb, ws, sb_full, dmix)


D_QBLOCK = 256
HEAD_COLS = N_HEADS * KV_BLOCK


def _stack_heads(x3):
    return jnp.stack([_bf(jnp.where(m[None], x3, 0.0)) for m in _head_masks()], axis=1)


def _stack_heads_rows(x):
    return jnp.concatenate([_bf(jnp.where(m, x, 0.0)) for m in _head_masks()], axis=0)


def _cols_to_rows(x):
    return jnp.concatenate([x[:, h * KV_BLOCK:(h + 1) * KV_BLOCK] for h in range(N_HEADS)], axis=0)


def _head_sums(x):
    return [jnp.sum(x[:, h * KV_BLOCK:(h + 1) * KV_BLOCK], axis=1, keepdims=True) for h in range(N_HEADS)]


def _spread(cols):
    tq = cols[0].shape[0]
    return jnp.concatenate([jnp.broadcast_to(c, (tq, KV_BLOCK)) for c in cols], axis=1)


def _pair_dot(x, m2):
    half = 2 * KV_BLOCK
    xb = _bf(x)
    return jnp.concatenate([_dot(xb[:, :half], m2), _dot(xb[:, half:], m2)], axis=1)


def _tri2(lower):
    n = 2 * KV_BLOCK
    r = lax.broadcasted_iota(jnp.int32, (n, n), 0)
    c = lax.broadcasted_iota(jnp.int32, (n, n), 1)
    same = (r >= KV_BLOCK) == (c >= KV_BLOCK)
    return _bf(jnp.where(same & (r > c if lower else r < c), 1.0, 0.0))


def _sb_scores(qs, kc, j, t_idx, on_diagonal):
    z = _dot_nt(qs, kc)
    lb = jnp.minimum(z, 0.0) - jnp.log(1.0 + jnp.exp(-jnp.abs(z)))
    if not on_diagonal:
        return (lambda x: x), lb, lb - z
    lane = lax.broadcasted_iota(jnp.int32, (1, HEAD_COLS), 1)
    valid = (j * KV_BLOCK + (lane & (KV_BLOCK - 1))) < t_idx
    keep = lambda x: jnp.where(valid, x, 0.0)
    return keep, lb, keep(lb - z)


RUN_LANES = 128


def _run_lane(j, h):
    return lax.broadcasted_iota(jnp.int32, (1, RUN_LANES), 1) == j * N_HEADS + h


def _d_qblock(s):
    return D_QBLOCK if s % D_QBLOCK == 0 else KV_BLOCK


def _mix_d_fwd(p, n_ex, name):
    t = p.shape[0]
    s = t // n_ex
    tq = _d_qblock(s)
    nq = s // tq
    r = tq // KV_BLOCK
    nb = s // KV_BLOCK
    assert nb * N_HEADS <= RUN_LANES

    def body(q_ref, k_ref, v_ref, y_ref, runs_ref, kc, vc):
        i = pl.program_id(1)

        @pl.when(i == 0)
        def _():
            kc[...] = _stack_heads(k_ref[...].reshape(nb, KV_BLOCK, MIX_W))
            vc[...] = _stack_heads(v_ref[...].reshape(nb, KV_BLOCK, MIX_W))

        qs = _bf(_ld(q_ref) * (HEAD_DIM ** -0.5))
        t_idx = i * tq + lax.broadcasted_iota(jnp.int32, (tq, 1), 0)
        after_m = _tri2(True)
        nkb = (i + 1) * r

        runs_ref[...] = jnp.zeros_like(runs_ref)

        def one_block(j, runs, acc, on_diagonal):
            keep, lb, c = _sb_scores(qs, kc[j].reshape(HEAD_COLS, MIX_W), j, t_idx, on_diagonal)
            a = keep(jnp.exp(lb + _pair_dot(c, after_m) + _spread(runs)))
            acc = acc + _dot(_bf(a), vc[j].reshape(HEAD_COLS, MIX_W))
            kept = runs_ref[...]
            for h in range(N_HEADS):
                kept = jnp.where(_run_lane(j, h), runs[h], kept)
            runs_ref[...] = kept
            return tuple(ru + cs for ru, cs in zip(runs, _head_sums(c))), acc

        def trip(last, carry, on_diagonal):
            runs, acc = carry
            for sub in range(r):
                runs, acc = one_block(last - sub, runs, acc, on_diagonal)
            return runs, acc

        zero = jnp.zeros((tq, 1), _F32)
        carry = trip(nkb - 1, ((zero,) * N_HEADS, jnp.zeros((tq, MIX_W), _F32)), True)
        _, acc = lax.fori_loop(0, i, lambda m, carry: trip(nkb - 1 - (m + 1) * r, carry, False), carry)
        y_ref[...] = _bf(acc)

    return pl.pallas_call(
        body, name=name, grid=(n_ex, nq),
        in_specs=[pl.BlockSpec((tq, MIX_W), lambda e, i: (e * nq + i, 7)), pl.BlockSpec((s, MIX_W), lambda e, i: (e, 8)),
                  pl.BlockSpec((s, MIX_W), lambda e, i: (e, 9))],
        out_specs=[pl.BlockSpec((tq, MIX_W), lambda e, i: (e * nq + i, 0)),
                   pl.BlockSpec((tq, RUN_LANES), lambda e, i: (e * nq + i, 0))],
        out_shape=[jax.ShapeDtypeStruct((t, MIX_W), _BF), jax.ShapeDtypeStruct((t, RUN_LANES), _F32)],
        scratch_shapes=[pltpu.VMEM((nb, N_HEADS, KV_BLOCK, MIX_W), _BF), pltpu.VMEM((nb, N_HEADS, KV_BLOCK, MIX_W), _BF)],
        compiler_params=_params("parallel", "arbitrary"),
    )(p, p, p)


def _mix_d_bwd(p, kept_runs, dmix, n_ex, name):
    t = p.shape[0]
    s = t // n_ex
    tq = _d_qblock(s)
    nq = s // tq
    r = tq // KV_BLOCK
    nb = s // KV_BLOCK
    scale = HEAD_DIM ** -0.5

    def body(q_ref, k_ref, v_ref, runs_ref, do_ref, dq_ref, dk_ref, dv_ref, kc, vc):
        i = pl.program_id(1)

        @pl.when(i == 0)
        def _():
            kc[...] = _stack_heads(k_ref[...].reshape(nb, KV_BLOCK, MIX_W))
            vc[...] = _stack_heads(v_ref[...].reshape(nb, KV_BLOCK, MIX_W))
            dk_ref[...] = jnp.zeros_like(dk_ref)
            dv_ref[...] = jnp.zeros_like(dv_ref)

        q_scaled = _ld(q_ref) * scale
        qs = _bf(q_scaled)
        do = do_ref[...]
        dob = _bf(do)
        q_rows = _stack_heads_rows(q_scaled)
        do_rows = _stack_heads_rows(do)
        kept = runs_ref[...]
        t_idx = i * tq + lax.broadcasted_iota(jnp.int32, (tq, 1), 0)
        after_m = _tri2(True)
        before_m = _tri2(False)
        nkb = (i + 1) * r
        zero = jnp.zeros((tq, 1), _F32)

        def trip(first, carry, on_diagonal):
            for sub in range(r):
                carry = one_block(first + sub, carry, on_diagonal)
            return carry

        def one_block(j, carry, on_diagonal):
            pres, dq = carry
            rows = pl.ds(pl.multiple_of(j * KV_BLOCK, KV_BLOCK), KV_BLOCK)
            kj = kc[j].reshape(HEAD_COLS, MIX_W)
            keep, lb, c = _sb_scores(qs, kj, j, t_idx, on_diagonal)
            runs = [jnp.sum(jnp.where(_run_lane(j, h), kept, 0.0), axis=1, keepdims=True) for h in range(N_HEADS)]
            a = keep(jnp.exp(lb + _pair_dot(c, after_m) + _spread(runs)))
            g = a * _dot_nt(dob, vc[j].reshape(HEAD_COLS, MIX_W))
            before = _pair_dot(g, before_m) + _spread(pres)
            sig = jnp.exp(lb)
            dz = _bf(keep(g * (1.0 - sig) - sig * before))
            dk_ref[rows, :] += _dot_tn(_cols_to_rows(dz), q_rows)
            dv_ref[rows, :] += _dot_tn(_cols_to_rows(_bf(a)), do_rows)
            return tuple(pr + gs for pr, gs in zip(pres, _head_sums(g))), dq + _dot(dz, kj)

        init = ((zero,) * N_HEADS, jnp.zeros((tq, MIX_W), _F32))
        carry = lax.fori_loop(0, i, lambda m, carry: trip(m * r, carry, False), init)
        _, dq = trip(i * r, carry, True)
        dq_ref[...] = _bf(dq * scale)

    return pl.pallas_call(
        body, name=name, grid=(n_ex, nq),
        in_specs=[pl.BlockSpec((tq, MIX_W), lambda e, i: (e * nq + i, 7)), pl.BlockSpec((s, MIX_W), lambda e, i: (e, 8)),
                  pl.BlockSpec((s, MIX_W), lambda e, i: (e, 9)), pl.BlockSpec((tq, RUN_LANES), lambda e, i: (e * nq + i, 0)),
                  pl.BlockSpec((tq, MIX_W), lambda e, i: (e * nq + i, 3))],
        out_specs=[pl.BlockSpec((tq, MIX_W), lambda e, i: (e * nq + i, 0)), pl.BlockSpec((s, MIX_W), lambda e, i: (e, 0)),
                   pl.BlockSpec((s, MIX_W), lambda e, i: (e, 0))],
        out_shape=[jax.ShapeDtypeStruct((t, MIX_W), _BF), jax.ShapeDtypeStruct((t, MIX_W), _F32),
                   jax.ShapeDtypeStruct((t, MIX_W), _F32)],
        scratch_shapes=[pltpu.VMEM((nb, N_HEADS, KV_BLOCK, MIX_W), _BF), pltpu.VMEM((nb, N_HEADS, KV_BLOCK, MIX_W), _BF)],
        compiler_params=_params("parallel", "arbitrary"),
    )(p, p, p, kept_runs, dmix)


def _fwd_mix(x, w, l, n_ex):
    p, h1 = _norm_mm(x, w["norm1_g"][l], w["w_in_t"][l], "in_proj")
    y_a = _mix_a_fwd(p, w["conv_a_w"][l], n_ex, "mix_a_fwd")
    y_b, cb = _mix_b_fwd(p, w["conv_b_w"][l], w["conv_b_b"][l], w["ln_b_g"][l], w["ln_b_b"][l], n_ex, "mix_b_fwd")
    y_c = _mix_c_fwd(p, w["ln_c_g"][l], w["ln_c_b"][l], w["sgu_w"][l], w["sgu_b_full"][l], n_ex, "mix_c_fwd")
    y_d, runs_d = _mix_d_fwd(p, n_ex, "mix_d_fwd")
    return dict(x=x, h1=h1, p=p, cb=cb, runs_d=runs_d, mix=(y_a, y_b, y_c, y_d))


def _fwd_ffn(st, w, l, n_ex):
    x1 = _mm_res(st["mix"], w["w_out"][l], st["x"], "out_proj")
    up_pre, h2 = _norm_mm(x1, w["norm2_g"][l], w["w_up_t"][l], "up_proj")
    act, conv_g, conv_v = _ffn_mid_fwd(up_pre, w["conv_f_w"][l], n_ex, "ffn_mid_fwd")
    st.update(x1=x1, h2=h2, up_pre=up_pre, act=act, conv_g=conv_g, conv_v=conv_v)
    return _mm_res((act,), w["w_down"][l], x1, "down_proj")


def _bwd_ffn(st, w, l, dx, dxb, n_ex):
    g = {}
    dact = _mm_nt(dxb, w["w_down"][l], "down_proj_dx")
    g["w_down"] = _mm_tn(st["act"], dxb, "down_proj_dw", _BF)
    dup_g, dup_v, dwf_g, dwf_v = _ffn_mid_bwd(
        st["up_pre"], st["conv_g"], st["conv_v"], w["conv_f_w"][l], dact, n_ex, "ffn_mid_bwd")
    g["conv_f_w"] = jnp.concatenate([dwf_g, dwf_v], axis=1)
    dx, dxb, g["norm2_g"] = _mm_normbwd((dup_g, dup_v), w["w_up_t"][l], st["x1"], w["norm2_g"][l], dx, "up_proj_dx")
    g["w_up_t"] = jnp.concatenate([_mm_tn(part, st["h2"], "up_proj_dw", _BF) for part in (dup_g, dup_v)], axis=0)
    return dx, dxb, g


def _bwd_out_proj(st, w, l, dxb):
    return _mm_nt(dxb, w["w_out"][l], "out_proj_dx"), _mm_tn_parts(st["mix"], dxb, "out_proj_dw")


def _bwd_mixers(st, w, l, dx, dmix, n_ex):
    g = {}
    p = st["p"]
    dp_a, g["conv_a_w"] = _mix_a_bwd(p, w["conv_a_w"][l], dmix, n_ex, "mix_a_bwd")
    dp_b, g["conv_b_w"], g["conv_b_b"], g["ln_b_g"], g["ln_b_b"] = _mix_b_bwd(
        p, st["cb"], w["conv_b_w"][l], w["ln_b_g"][l], w["ln_b_b"][l], dmix, n_ex, "mix_b_bwd")
    dp_c, g["ln_c_g"], g["ln_c_b"], g["sgu_w"], g["sgu_b_t"] = _mix_c_bwd(
        p, w["ln_c_g"][l], w["ln_c_b"][l], w["sgu_w"][l], w["sgu_b_full"][l], dmix, n_ex, "mix_c_bwd")
    dq, dk, dv = _mix_d_bwd(p, st["runs_d"], dmix, n_ex, "mix_d_bwd")
    dp = (dp_a, dp_b, dp_c, dq, dk, dv)
    dx, dxb, g["norm1_g"] = _mm_normbwd(dp, w["w_in_t"][l], st["x"], w["norm1_g"][l], dx, "in_proj_dx")
    g["w_in_t"] = _mm_tn_parts(dp, st["h1"], "in_proj_dw")
    return dx, dxb, g


def _bwd_mix(st, w, l, dx, dxb, n_ex):
    dmix, dw_out = _bwd_out_proj(st, w, l, dxb)
    dx, dxb, g = _bwd_mixers(st, w, l, dx, dmix, n_ex)
    g["w_out"] = dw_out
    return dx, dxb, g


def _local_fwd_bwd(x, target, w, n_ex):
    depth = len(w["w_in_t"])
    saved = []
    for l in range(depth):
        st = _fwd_mix(x, w, l, n_ex)
        x = _fwd_ffn(st, w, l, n_ex)
        saved.append(st)
    dx, dxb, d_final_g, loss = _final_loss(x, w["final_g"], target, "final_loss")
    grads = {}
    for l in reversed(range(depth)):
        dx, dxb, g_ffn = _bwd_ffn(saved[l], w, l, dx, dxb, n_ex)
        dx, dxb, g_mix = _bwd_mix(saved[l], w, l, dx, dxb, n_ex)
        for k, v in {**g_ffn, **g_mix}.items():
            grads.setdefault(k, [None] * depth)[l] = v
    grads["final_g"] = d_final_g
    return loss, dx, grads


_MESH = pl.DeviceIdType.MESH
_ANY = pl.BlockSpec(memory_space=pl.ANY)


def _position():
    return lax.axis_index("x"), lax.axis_index("y"), lax.axis_index("c")


def _flat(px, py, pc):
    return 4 * px + 2 * py + pc


def _all_gather(shard, name, after):
    r, c_ = shard.shape

    def body(x_ref, after_ref, out_ref, send_sems, recv_sems, local_sem):
        x, y, c = _position()
        me, sibling = (x, y, c), (x, y, 1 - c)
        chips = [(1 - x, y), (x, 1 - y), (1 - x, 1 - y)]

        def copy(k, block, to, src=None):
            slab = out_ref.at[_flat(*block)]
            return pltpu.make_async_remote_copy(
                src_ref=slab if src is None else src, dst_ref=slab, send_sem=send_sems.at[k], recv_sem=recv_sems.at[k],
                device_id=to, device_id_type=_MESH)

        mine = pltpu.make_async_copy(x_ref, out_ref.at[_flat(*me)], local_sem)
        mine.start()
        first = [copy(0, me, sibling, src=x_ref)]
        first += [copy(1 + j, me, (*chip, c), src=x_ref) for j, chip in enumerate(chips)]
        for cp in first:
            cp.start()
        passed = [copy(4 + j, (*chip, c), sibling) for j, chip in enumerate(chips)]
        for j, chip in enumerate(chips):
            copy(1 + j, (*chip, c), me).wait_recv()
            passed[j].start()
        copy(0, sibling, me).wait_recv()
        for j, chip in enumerate(chips):
            copy(4 + j, (*chip, 1 - c), me).wait_recv()
        for cp in first + passed:
            cp.wait_send()
        mine.wait()

    return pl.pallas_call(
        body, name=name, out_shape=jax.ShapeDtypeStruct((N_DEV, r, c_), shard.dtype),
        in_specs=[_ANY, _ANY], out_specs=_ANY,
        scratch_shapes=[pltpu.SemaphoreType.DMA((7,)), pltpu.SemaphoreType.DMA((7,)), pltpu.SemaphoreType.DMA],
    )(shard, after)


_HBM = pl.BlockSpec(memory_space=pltpu.HBM)
_SEM = pl.BlockSpec(memory_space=pltpu.SEMAPHORE)
_DATAFLOW = pltpu.SideEffectType.DATAFLOW_SIDE_EFFECTING


def _peers(x, y, c):
    return [((1 - x) if (k + 1) & 4 else x, (1 - y) if (k + 1) & 2 else y, (1 - c) if (k + 1) & 1 else c)
            for k in range(N_DEV - 1)]


def _direct_copies(src_refs, land_refs, send_sems, recv_sems, to_all):
    x, y, c = _position()
    my = _flat(x, y, c)
    out, back = [], []
    for m, (src_ref, land_ref) in enumerate(zip(src_refs, land_refs)):
        for k, peer in enumerate(_peers(x, y, c)):
            src = src_ref if to_all else src_ref.at[_flat(*peer)]
            n = m * (N_DEV - 1) + k
            sems = dict(send_sem=send_sems.at[n], recv_sem=recv_sems.at[n], device_id=peer, device_id_type=_MESH)
            out.append(pltpu.make_async_remote_copy(src_ref=src, dst_ref=land_ref.at[my], **sems))
            back.append(pltpu.make_async_remote_copy(src_ref=src, dst_ref=land_ref.at[_flat(*peer)], **sems))
    return out, back


def _exchange_start(srcs, to_all, after, name):
    n = len(srcs)
    n_sems = n * (N_DEV - 1)
    land_shapes = [(N_DEV,) + tuple(a.shape[-2:]) for a in srcs]

    def body(*refs):
        src_refs, land_refs = refs[:n], refs[n:2 * n]
        send_sems, recv_sems = refs[2 * n + 1], refs[2 * n + 2]
        token = refs[-1]
        for cp in _direct_copies(src_refs, land_refs, send_sems, recv_sems, to_all)[0]:
            cp.start()
        token[...] = jnp.zeros_like(token)

    lands = [pltpu.with_memory_space_constraint(lax.empty(shp, a.dtype), pltpu.HBM) for shp, a in zip(land_shapes, srcs)]
    outs = pl.pallas_call(
        body, name=name,
        out_shape=(pltpu.SemaphoreType.DMA((n_sems,)), pltpu.SemaphoreType.DMA((n_sems,)),
                   *[pltpu.HBM(a.shape, a.dtype) for a in srcs], *[pltpu.HBM(shp, a.dtype) for shp, a in zip(land_shapes, srcs)],
                   jax.ShapeDtypeStruct((8, 128), _F32)),
        in_specs=(_HBM,) * (2 * n) + (_ANY,),
        out_specs=(_SEM, _SEM) + (_HBM,) * (2 * n) + (pl.BlockSpec(memory_space=pltpu.VMEM),),
        input_output_aliases={i: 2 + i for i in range(2 * n)},
        compiler_params=pltpu.CompilerParams(has_side_effects=_DATAFLOW),
    )(*[pltpu.with_memory_space_constraint(a, pltpu.HBM) for a in srcs], *lands, after)
    return (outs[0], outs[1], outs[2:2 + n], outs[2 + n:2 + 2 * n], to_all), outs[-1]


def _exchange_wait(handle, after, name):
    send_sems, recv_sems, srcs, lands, to_all = handle
    n = len(srcs)

    def body(*refs):
        out, back = _direct_copies(refs[:n], refs[n:2 * n], refs[2 * n], refs[2 * n + 1], to_all)
        for cp in out:
            cp.wait_send()
        for cp in back:
            cp.wait_recv()

    outs = pl.pallas_call(
        body, name=name,
        out_shape=tuple(pltpu.HBM(a.shape, a.dtype) for a in (*srcs, *lands)),
        in_specs=(_HBM,) * (2 * n) + (_SEM, _SEM, _ANY), out_specs=(_HBM,) * (2 * n),
        input_output_aliases={i: i for i in range(2 * n)},
        compiler_params=pltpu.CompilerParams(has_side_effects=_DATAFLOW),
    )(*srcs, *lands, send_sems, recv_sems, after)
    return outs[:n], outs[n:]


def _with_own(landed, own):
    my = _flat(*_position())
    return lax.dynamic_update_slice(landed, own[None], (my, 0, 0))


def _sum_slabs(slabs, name):
    n, r, c_ = slabs.shape
    tr = _pick_tile(r, 16, max(16, (12 << 20) // (n * c_ * slabs.dtype.itemsize)))

    def body(x_ref, o_ref):
        acc = x_ref[0].astype(_F32)
        for k in range(1, n):
            acc = acc + x_ref[k].astype(_F32)
        o_ref[...] = acc

    return pl.pallas_call(
        body, name=name, grid=(r // tr,),
        in_specs=[pl.BlockSpec((n, tr, c_), lambda i: (0, i, 0))],
        out_specs=pl.BlockSpec((tr, c_), lambda i: (i, 0)),
        out_shape=jax.ShapeDtypeStruct((r, c_), _F32),
        compiler_params=_params("parallel"),
    )(slabs)


def _adamw(w, g, m, v, name):
    r, c_ = w.shape
    tr = _pick_tile(r, 8, 512)

    def body(w_ref, g_ref, m_ref, v_ref, d_ref, nm_ref, nv_ref):
        _adamw_refs(w_ref, g_ref, m_ref, v_ref, d_ref, nm_ref, nv_ref)

    spec = pl.BlockSpec((tr, c_), lambda i: (i, 0))
    shape = jax.ShapeDtypeStruct((r, c_), _F32)
    return pl.pallas_call(
        body, name=name, grid=(r // tr,), in_specs=[spec] * 4, out_specs=[spec] * 3, out_shape=[shape] * 3,
        compiler_params=_params("parallel"),
    )(w, g, m, v)


def _adamw_refs(w_ref, g_ref, m_ref, v_ref, d_ref, nm_ref, nv_ref):
    gv = g_ref[...]
    nm = ADAM_B1 * m_ref[...] + (1.0 - ADAM_B1) * gv
    nv = ADAM_B2 * v_ref[...] + (1.0 - ADAM_B2) * (gv * gv)
    m_hat = nm / (1.0 - ADAM_B1 ** ADAM_STEP)
    v_hat = nv / (1.0 - ADAM_B2 ** ADAM_STEP)
    d_ref[...] = -ADAM_LR * (m_hat / (jnp.sqrt(v_hat) + ADAM_EPS) + ADAM_WD * w_ref[...])
    nm_ref[...] = nm
    nv_ref[...] = nv


def _adamw_small(params, name):
    n = len(params)

    def body(*refs):
        for i in range(n):
            _adamw_refs(*refs[4 * i:4 * i + 4], *refs[4 * n + 3 * i:4 * n + 3 * i + 3])

    outs = pl.pallas_call(
        body, name=name,
        out_shape=[jax.ShapeDtypeStruct(p[0].shape, _F32) for p in params for _ in range(3)],
        compiler_params=pltpu.CompilerParams(vmem_limit_bytes=VMEM_LIMIT),
    )(*[a for p in params for a in p])
    return [tuple(outs[3 * i:3 * i + 3]) for i in range(n)]


_SMALL = ("norm1_g", "conv_a_w", "conv_b_w", "conv_b_b", "ln_b_g", "ln_b_b", "ln_c_g", "ln_c_b", "sgu_w", "sgu_b",
          "norm2_g", "conv_f_w", "final_g")
_CONV_SHARDED = ("conv_a_w", "conv_b_w", "conv_f_w")
_NAMES = ("norm1_g", "w_in", "conv_a_w", "conv_b_w", "conv_b_b", "ln_b_g", "ln_b_b", "ln_c_g", "ln_c_b", "sgu_w", "sgu_b",
          "w_out", "norm2_g", "w_up", "conv_f_w", "w_down", "final_g")


def _pack_rows(parts, lanes=128, row_multiple=8):
    flat = jnp.concatenate([a.reshape(-1) for a in parts])
    rows = -(-flat.shape[0] // lanes)
    rows = -(-rows // row_multiple) * row_multiple
    return jnp.pad(flat, (0, rows * lanes - flat.shape[0])).reshape(rows, lanes)


def _unpack_rows(packed, shapes):
    flat = packed.reshape(-1)
    out, off = [], 0
    for shp in shapes:
        size = 1
        for s in shp:
            size *= s
        out.append(flat[off:off + size].reshape(shp))
        off += size
    return out


def _gather_conv_weights(conv_a_w, conv_b_w, conv_f_w, after):
    shards = (conv_a_w, conv_b_w, conv_f_w)
    flat = _all_gather(_pack_rows(shards), "gather_conv_weights", after).reshape(N_DEV, -1)
    full, off = [], 0
    for s in shards:
        layers, taps, width = s.shape
        per_dev = flat[:, off:off + s.size].reshape(N_DEV, layers, taps, width)
        full.append(jnp.moveaxis(per_dev, 0, 2).reshape(layers, taps, N_DEV * width))
        off += s.size
    return full


def kernel(x, norm1_g, w_in, conv_a_w, conv_b_w, conv_b_b, ln_b_g, ln_b_b, ln_c_g, ln_c_b, sgu_w, sgu_b, w_out, norm2_g, w_up, conv_f_w, w_down, final_g, loss_target, m_norm1_g, m_w_in, m_conv_a_w, m_conv_b_w, m_conv_b_b, m_ln_b_g, m_ln_b_b, m_ln_c_g, m_ln_c_b, m_sgu_w, m_sgu_b, m_w_out, m_norm2_g, m_w_up, m_conv_f_w, m_w_down, m_final_g, v_norm1_g, v_w_in, v_conv_a_w, v_conv_b_w, v_conv_b_b, v_ln_b_g, v_ln_b_b, v_ln_c_g, v_ln_c_b, v_sgu_w, v_sgu_b, v_w_out, v_norm2_g, v_w_up, v_conv_f_w, v_w_down, v_final_g):
    weights = dict(norm1_g=norm1_g, w_in=w_in, conv_a_w=conv_a_w, conv_b_w=conv_b_w, conv_b_b=conv_b_b, ln_b_g=ln_b_g,
                   ln_b_b=ln_b_b, ln_c_g=ln_c_g, ln_c_b=ln_c_b, sgu_w=sgu_w, sgu_b=sgu_b, w_out=w_out, norm2_g=norm2_g,
                   w_up=w_up, conv_f_w=conv_f_w, w_down=w_down, final_g=final_g)
    mom1 = dict(norm1_g=m_norm1_g, w_in=m_w_in, conv_a_w=m_conv_a_w, conv_b_w=m_conv_b_w, conv_b_b=m_conv_b_b,
                ln_b_g=m_ln_b_g, ln_b_b=m_ln_b_b, ln_c_g=m_ln_c_g, ln_c_b=m_ln_c_b, sgu_w=m_sgu_w, sgu_b=m_sgu_b,
                w_out=m_w_out, norm2_g=m_norm2_g, w_up=m_w_up, conv_f_w=m_conv_f_w, w_down=m_w_down, final_g=m_final_g)
    mom2 = dict(norm1_g=v_norm1_g, w_in=v_w_in, conv_a_w=v_conv_a_w, conv_b_w=v_conv_b_w, conv_b_b=v_conv_b_b,
                ln_b_g=v_ln_b_g, ln_b_b=v_ln_b_b, ln_c_g=v_ln_c_g, ln_c_b=v_ln_c_b, sgu_w=v_sgu_w, sgu_b=v_sgu_b,
                w_out=v_w_out, norm2_g=v_norm2_g, w_up=v_w_up, conv_f_w=v_conv_f_w, w_down=v_w_down, final_g=v_final_g)
    n_ex, seq, d = x.shape
    depth = w_in.shape[0]
    assert depth == 2
    my = _flat(*_position())
    row = lambda a, l: a[l][None]
    tied = lambda a, token: a + token[0:1, 0:1]

    slab = {"w_in": [_bf(jnp.swapaxes(w_in, 1, 2)[l]) for l in range(depth)], "w_out": [_bf(w_out[l]) for l in range(depth)],
            "w_up": [_bf(jnp.swapaxes(w_up, 1, 2)[l]) for l in range(depth)], "w_down": [_bf(w_down[l]) for l in range(depth)]}
    rows = {name: parts[0].shape[0] for name, parts in slab.items()}
    key_of = {"w_in": "w_in_t", "w_out": "w_out", "w_up": "w_up_t", "w_down": "w_down"}
    rest_layer0 = [("w_out", 0), ("w_up", 0), ("w_down", 0)]
    all_layer1 = [("w_in", 1), ("w_out", 1), ("w_up", 1), ("w_down", 1)]

    w_in0 = _all_gather(slab["w_in"][0], "gather_w_in0", norm1_g)
    conv_a_full, conv_b_full, conv_f_full = _gather_conv_weights(conv_a_w, conv_b_w, conv_f_w, w_in0)
    gather0, token = _exchange_start([slab[n][l] for n, l in rest_layer0], True, conv_f_full, "gather_layer0_start")
    w = {
        "norm1_g": [row(norm1_g, l) for l in range(depth)], "w_in_t": [None] * depth,
        "conv_a_w": [conv_a_full[l] for l in range(depth)], "conv_b_w": [conv_b_full[l] for l in range(depth)],
        "conv_b_b": [row(conv_b_b, l) for l in range(depth)], "ln_b_g": [row(ln_b_g, l) for l in range(depth)],
        "ln_b_b": [row(ln_b_b, l) for l in range(depth)], "ln_c_g": [row(ln_c_g, l) for l in range(depth)],
        "ln_c_b": [row(ln_c_b, l) for l in range(depth)], "sgu_w": [sgu_w[l] for l in range(depth)],
        "sgu_b_full": [jnp.repeat(sgu_b[l].T, HEAD_DIM, axis=1) for l in range(depth)],
        "w_out": [None] * depth, "norm2_g": [row(norm2_g, l) for l in range(depth)], "w_up_t": [None] * depth,
        "conv_f_w": [conv_f_full[l] for l in range(depth)], "w_down": [None] * depth, "final_g": final_g[None],
    }
    w["w_in_t"][0] = w_in0.reshape(N_DEV * rows["w_in"], d)
    w["norm1_g"][0] = tied(row(norm1_g, 0), token)

    def land_weights(handle, after, which, name):
        owns, landed = _exchange_wait(handle, after, name)
        for (n, l), own, got in zip(which, owns, landed):
            w[key_of[n]][l] = _with_own(got, own).reshape(N_DEV * rows[n], d)
        return landed[0]

    st0 = _fwd_mix(x.reshape(n_ex * seq, d), w, 0, n_ex)
    landed0 = land_weights(gather0, st0["mix"][3], rest_layer0, "gather_layer0_wait")
    gather1, token = _exchange_start([slab[n][l] for n, l in all_layer1], True, landed0, "gather_layer1_start")
    w["norm2_g"][0] = tied(row(norm2_g, 0), token)
    x_mid = _fwd_ffn(st0, w, 0, n_ex)
    land_weights(gather1, x_mid, all_layer1, "gather_layer1_wait")
    st1 = _fwd_mix(x_mid, w, 1, n_ex)
    x_out = _fwd_ffn(st1, w, 1, n_ex)
    dx, dxb, d_final_g, loss = _final_loss(x_out, w["final_g"], loss_target.reshape(n_ex * seq, d), "final_loss")
    loss = lax.psum(loss[0, 0], ("x", "y", "c"))

    def send_grads(g, which, after, name):
        return _exchange_start([g[key_of[n]].reshape(N_DEV, rows[n], d) for n, _ in which], False, after, name)

    dx, dxb, g_ffn1 = _bwd_ffn(st1, w, 1, dx, dxb, n_ex)
    dx, dxb, g_mix1 = _bwd_mix(st1, w, 1, dx, dxb, n_ex)
    grads1, token = send_grads({**g_ffn1, **g_mix1}, all_layer1, dx, "exchange_layer1_start")
    w["norm2_g"][0] = tied(row(norm2_g, 0), token)
    dx, dxb, g_ffn0 = _bwd_ffn(st0, w, 0, dx, dxb, n_ex)
    g_ffn0["w_out"] = _mm_tn_parts(st0["mix"], dxb, "out_proj_dw")
    ffn_layer0 = [("w_out", 0), ("w_up", 0), ("w_down", 0)]
    grads0a, token = send_grads(g_ffn0, ffn_layer0, dxb, "exchange_ffn0_start")
    dmix = _mm_nt(dxb, w["w_out"][0], "out_proj_dx", after=token)
    dx, dxb, g_mix0 = _bwd_mixers(st0, w, 0, dx, dmix, n_ex)
    mix_layer0 = [("w_in", 0)]
    grads0b, _ = send_grads(g_mix0, mix_layer0, dx, "exchange_mix0_start")
    grad_x = dx.reshape(n_ex, seq, d)
    g = {k: [{**g_ffn0, **g_mix0}[k], {**g_ffn1, **g_mix1}[k]] for k in g_mix0.keys() | g_ffn0.keys()}
    g["final_g"] = d_final_g

    reduced = {}

    def land_grads(handle, after, which, name):
        sent, landed = _exchange_wait(handle, after, name + "_wait")
        for (n, l), src, got in zip(which, sent, landed):
            own = lax.dynamic_index_in_dim(src, my, 0, keepdims=False)
            reduced[(n, l)] = _sum_slabs(_with_own(got, own), name + "_sum_" + n)
        return reduced[which[-1]]

    def stacked_grad(name):
        stacked = jnp.stack([reduced[(name, l)] for l in range(depth)])
        return jnp.swapaxes(stacked, 1, 2) if name in ("w_in", "w_up") else stacked

    done = land_grads(grads1, dx, all_layer1, "exchange_layer1")
    land_grads(grads0a, done, ffn_layer0, "exchange_ffn0")
    grads = {name: stacked_grad(name) for name in ("w_out", "w_up", "w_down")}

    small_local = {
        "norm1_g": jnp.stack([a[0] for a in g["norm1_g"]]), "conv_a_w": jnp.stack(g["conv_a_w"]),
        "conv_b_w": jnp.stack(g["conv_b_w"]), "conv_b_b": jnp.stack([a[0] for a in g["conv_b_b"]]),
        "ln_b_g": jnp.stack([a[0] for a in g["ln_b_g"]]), "ln_b_b": jnp.stack([a[0] for a in g["ln_b_b"]]),
        "ln_c_g": jnp.stack([a[0] for a in g["ln_c_g"]]), "ln_c_b": jnp.stack([a[0] for a in g["ln_c_b"]]),
        "sgu_w": jnp.stack(g["sgu_w"]), "sgu_b": jnp.stack([a.T for a in g["sgu_b_t"]]),
        "norm2_g": jnp.stack([a[0] for a in g["norm2_g"]]), "conv_f_w": jnp.stack(g["conv_f_w"]),
        "final_g": g["final_g"][0],
    }
    small_sum = _sum_slabs(_all_gather(_pack_rows([small_local[k] for k in _SMALL]), "gather_small_grads", dx),
                           "sum_small_grads")
    for name, total in zip(_SMALL, _unpack_rows(small_sum, [small_local[k].shape for k in _SMALL])):
        if name in _CONV_SHARDED:
            width = weights[name].shape[-1]
            total = lax.dynamic_slice_in_dim(total, my * width, width, axis=-1)
        grads[name] = total

    delta, new_m, new_v = {}, {}, {}

    def as_2d(name):
        shp = weights[name].shape
        two_d = (-1, shp[-1]) if len(shp) > 1 else (1, shp[0])
        return tuple(a.reshape(two_d) for a in (weights[name], grads[name], mom1[name], mom2[name]))

    def keep(name, outs):
        delta[name], new_m[name], new_v[name] = (o.reshape(weights[name].shape) for o in outs)

    for name, outs in zip(_SMALL, _adamw_small([as_2d(name) for name in _SMALL], "adamw_small")):
        keep(name, outs)
    for name in ("w_up", "w_down", "w_out", "w_in"):
        if name == "w_in":
            land_grads(grads0b, new_v["w_out"], mix_layer0, "exchange_mix0")
            grads["w_in"] = stacked_grad("w_in")
        keep(name, _adamw(*as_2d(name), "adamw_" + name))

    return (loss, grad_x, *[grads[n] for n in _NAMES], *[delta[n] for n in _NAMES], *[new_m[n] for n in _NAMES],
            *[new_v[n] for n in _NAMES])
```

```python
import jax
import jax.numpy as jnp
from jax import lax
from jax.experimental import pallas as pl
from jax.experimental.pallas import tpu as pltpu

_F32 = jnp.float32
_BF = jnp.bfloat16

HEAD_DIM = 64
MIX_W = 256
N_HEADS = MIX_W // HEAD_DIM
CHUNK = 128
KV_BLOCK = 128
K_SHORT = 3
K_CONF = 31
K_FFN = 3
RMS_EPS = 1e-6
LN_EPS = 1e-5
ADAM_LR = 0.001
ADAM_B1 = 0.9
ADAM_B2 = 0.999
ADAM_EPS = 1e-08
ADAM_WD = 0.01
ADAM_STEP = 10
N_DEV = 8
VMEM_LIMIT = 56 * 1024 * 1024


def _bf(x):
    return x.astype(_BF)


def _ld(ref):
    return ref[...].astype(_F32)


_ANY_SPEC = pl.BlockSpec(memory_space=pl.ANY)


def _params(*sem):
    return pltpu.CompilerParams(dimension_semantics=sem, vmem_limit_bytes=VMEM_LIMIT)


def _dot(a, b):
    return jnp.dot(a, b, preferred_element_type=_F32)


def _dot_nt(a, b):
    return lax.dot_general(a, b, (((1,), (1,)), ((), ())), preferred_element_type=_F32)


def _dot_tn(a, b):
    return lax.dot_general(a, b, (((0,), (0,)), ((), ())), preferred_element_type=_F32)


def _row_tile(t, want):
    return want if t % want == 0 else t


def _pick_tile(rows, unit, max_rows):
    best = 0
    for cand in range(unit, min(rows, max_rows) + 1, unit):
        if rows % cand == 0:
            best = cand
    return best or rows


def _sigmoid(x):
    return 1.0 / (1.0 + jnp.exp(-x))


def _rms_rstd(x):
    return lax.rsqrt(jnp.mean(x * x, axis=-1, keepdims=True) + RMS_EPS)


def _norm_mm(x, g, w_t, name):
    t, d = x.shape
    n = w_t.shape[0]
    tm = _row_tile(t, 512)
    tn = _row_tile(n, 512)

    def body(x_ref, g_ref, w_ref, p_ref, h_ref):
        xv = x_ref[...]
        h = _bf(xv * _rms_rstd(xv) * g_ref[...])
        h_ref[...] = h
        for n0 in range(0, n, tn):
            p_ref[:, n0:n0 + tn] = _bf(_dot_nt(h, w_ref[n0:n0 + tn, :]))

    return pl.pallas_call(
        body, name=name, grid=(t // tm,),
        in_specs=[pl.BlockSpec((tm, d), lambda i: (i, 0)), pl.BlockSpec((1, d), lambda i: (0, 0)),
                  pl.BlockSpec((n, d), lambda i: (0, 0))],
        out_specs=[pl.BlockSpec((tm, n), lambda i: (i, 0)), pl.BlockSpec((tm, d), lambda i: (i, 0))],
        out_shape=[jax.ShapeDtypeStruct((t, n), _BF), jax.ShapeDtypeStruct((t, d), _BF)],
        compiler_params=_params("parallel"),
    )(x, g, w_t)


def _mm_nt(a, w_t, name, after=None):
    t, k = a.shape
    n = w_t.shape[0]
    tm = _row_tile(t, 512)
    tn = _row_tile(n, 512) if n % 512 == 0 else _row_tile(n, 256)

    def body(a_ref, w_ref, *rest):
        o_ref = rest[-1]
        av = a_ref[...]
        for n0 in range(0, n, tn):
            o_ref[:, n0:n0 + tn] = _bf(_dot_nt(av, w_ref[n0:n0 + tn, :]))

    extra = () if after is None else (after,)
    return pl.pallas_call(
        body, name=name, grid=(t // tm,),
        in_specs=[pl.BlockSpec((tm, k), lambda i: (i, 0)), pl.BlockSpec((n, k), lambda i: (0, 0))] + [_ANY_SPEC] * len(extra),
        out_specs=pl.BlockSpec((tm, n), lambda i: (i, 0)),
        out_shape=jax.ShapeDtypeStruct((t, n), _BF),
        compiler_params=_params("parallel"),
    )(a, w_t, *extra)


def _mm_res(parts, w, x, name):
    t = x.shape[0]
    k, d = w.shape
    tm = _row_tile(t, 512)
    widths = [a.shape[1] for a in parts]
    n_parts = len(parts)

    def body(*refs):
        w_ref, x_ref, o_ref = refs[n_parts:]
        acc, off = x_ref[...], 0
        for a_ref, width in zip(refs[:n_parts], widths):
            acc = acc + _dot(a_ref[...], w_ref[off:off + width, :])
            off += width
        o_ref[...] = acc

    return pl.pallas_call(
        body, name=name, grid=(t // tm,),
        in_specs=[pl.BlockSpec((tm, width), lambda i: (i, 0)) for width in widths] + [
            pl.BlockSpec((k, d), lambda i: (0, 0)), pl.BlockSpec((tm, d), lambda i: (i, 0))],
        out_specs=pl.BlockSpec((tm, d), lambda i: (i, 0)),
        out_shape=jax.ShapeDtypeStruct((t, d), _F32),
        compiler_params=_params("parallel"),
    )(*parts, w, x)


def _mm_normbwd(parts, w, x, g, dres, name):
    t = x.shape[0]
    k, d = w.shape
    tm = _row_tile(t, 512)
    widths = [a.shape[1] for a in parts]
    n_parts = len(parts)

    def body(*refs):
        a_refs = refs[:n_parts]
        w_ref, x_ref, g_ref, r_ref, dx_ref, dxb_ref, dg_ref = refs[n_parts:]
        dh, off = None, 0
        for a_ref, width in zip(a_refs, widths):
            term = _dot(_bf(a_ref[...]), w_ref[off:off + width, :])
            dh = term if dh is None else dh + term
            off += width
        xv = x_ref[...]
        rstd = _rms_rstd(xv)
        xn = xv * rstd
        u = dh * g_ref[...]
        dx = r_ref[...] + rstd * (u - xn * jnp.mean(u * xn, axis=-1, keepdims=True))
        dx_ref[...] = dx
        dxb_ref[...] = _bf(dx)

        @pl.when(pl.program_id(0) == 0)
        def _():
            dg_ref[...] = jnp.zeros_like(dg_ref)

        dg_ref[...] += jnp.sum(dh * xn, axis=0, keepdims=True)

    return pl.pallas_call(
        body, name=name, grid=(t // tm,),
        in_specs=[pl.BlockSpec((tm, width), lambda i: (i, 0)) for width in widths] + [
            pl.BlockSpec((k, d), lambda i: (0, 0)),
            pl.BlockSpec((tm, d), lambda i: (i, 0)), pl.BlockSpec((1, d), lambda i: (0, 0)),
            pl.BlockSpec((tm, d), lambda i: (i, 0))],
        out_specs=[pl.BlockSpec((tm, d), lambda i: (i, 0)), pl.BlockSpec((tm, d), lambda i: (i, 0)),
                   pl.BlockSpec((1, d), lambda i: (0, 0))],
        out_shape=[jax.ShapeDtypeStruct((t, d), _F32), jax.ShapeDtypeStruct((t, d), _BF),
                   jax.ShapeDtypeStruct((1, d), _F32)],
        compiler_params=_params("arbitrary"),
    )(*parts, w, x, g, dres)


def _mm_tn(a, b, name, out_dtype):
    t, m = a.shape
    n = b.shape[1]
    tm = _pick_tile(m, 128, 1408)
    tn = _pick_tile(n, 128, 1024)
    tk = _row_tile(t, 1024)
    nk = t // tk

    def body(a_ref, b_ref, o_ref, acc):
        kk = pl.program_id(2)

        @pl.when(kk == 0)
        def _():
            acc[...] = jnp.zeros_like(acc)

        acc[...] += _dot_tn(_bf(a_ref[...]), b_ref[...])

        @pl.when(kk == nk - 1)
        def _():
            o_ref[...] = acc[...].astype(o_ref.dtype)

    return pl.pallas_call(
        body, name=name, grid=(m // tm, n // tn, nk),
        in_specs=[pl.BlockSpec((tk, tm), lambda i, j, kk: (kk, i)), pl.BlockSpec((tk, tn), lambda i, j, kk: (kk, j))],
        out_specs=pl.BlockSpec((tm, tn), lambda i, j, kk: (i, j)),
        out_shape=jax.ShapeDtypeStruct((m, n), out_dtype),
        scratch_shapes=[pltpu.VMEM((tm, tn), _F32)],
        compiler_params=_params("parallel", "parallel", "arbitrary"),
    )(a, b)


def _mm_tn_halves(a0, a1, b, name):
    t, m = a0.shape
    n = b.shape[1]
    tm = _pick_tile(m, 128, 1408)
    tn = _pick_tile(n, 128, 1024)
    tk = _row_tile(t, 1024)
    nk = t // tk
    half = m // tm

    def body(a0_ref, a1_ref, b_ref, o_ref, acc):
        i = pl.program_id(0)
        kk = pl.program_id(2)

        @pl.when(kk == 0)
        def _():
            acc[...] = jnp.zeros_like(acc)

        @pl.when(i < half)
        def _():
            acc[...] += _dot_tn(a0_ref[...], b_ref[...])

        @pl.when(i >= half)
        def _():
            acc[...] += _dot_tn(a1_ref[...], b_ref[...])

        @pl.when(kk == nk - 1)
        def _():
            o_ref[...] = _bf(acc[...])

    return pl.pallas_call(
        body, name=name, grid=(2 * half, n // tn, nk),
        in_specs=[pl.BlockSpec((tk, tm), lambda i, j, kk: (jnp.where(i < half, kk, 0), jnp.minimum(i, half - 1))),
                  pl.BlockSpec((tk, tm), lambda i, j, kk: (jnp.where(i >= half, kk, 0), jnp.maximum(i - half, 0))),
                  pl.BlockSpec((tk, tn), lambda i, j, kk: (kk, j))],
        out_specs=pl.BlockSpec((tm, tn), lambda i, j, kk: (i, j)),
        out_shape=jax.ShapeDtypeStruct((2 * m, n), _BF),
        scratch_shapes=[pltpu.VMEM((tm, tn), _F32)],
        compiler_params=_params("parallel", "parallel", "arbitrary"),
    )(a0, a1, b)


def _mm_tn_parts(parts, b, name, after=None):
    t, n = b.shape
    widths = [a.shape[1] for a in parts]
    m = sum(widths)
    n_parts = len(parts)
    tk = _row_tile(t, 1024)
    nk = t // tk
    extra = () if after is None else (after,)

    def body(*refs):
        b_ref = refs[n_parts]
        o_ref, acc = refs[-2:]
        kk = pl.program_id(0)

        @pl.when(kk == 0)
        def _():
            acc[...] = jnp.zeros_like(acc)

        bv = b_ref[...]
        off = 0
        for a_ref, width in zip(refs[:n_parts], widths):
            acc[off:off + width, :] += _dot_tn(_bf(a_ref[...]), bv)
            off += width

        @pl.when(kk == nk - 1)
        def _():
            o_ref[...] = _bf(acc[...])

    return pl.pallas_call(
        body, name=name, grid=(nk,),
        in_specs=[pl.BlockSpec((tk, width), lambda kk: (kk, 0)) for width in widths] + [pl.BlockSpec((tk, n), lambda kk: (kk, 0))]
        + [_ANY_SPEC] * len(extra),
        out_specs=pl.BlockSpec((m, n), lambda kk: (0, 0)),
        out_shape=jax.ShapeDtypeStruct((m, n), _BF),
        scratch_shapes=[pltpu.VMEM((m, n), _F32)],
        compiler_params=_params("arbitrary"),
    )(*parts, b, *extra)


def _final_loss(x, g, target, name):
    t, d = x.shape
    tm = _row_tile(t, 256)

    def body(x_ref, g_ref, t_ref, dx_ref, dxb_ref, dg_ref, loss_ref):
        xv = x_ref[...]
        rstd = _rms_rstd(xv)
        xn = xv * rstd
        err = xn * g_ref[...] - t_ref[...]
        dy = err * (1.0 / d)
        u = dy * g_ref[...]
        dx = rstd * (u - xn * jnp.mean(u * xn, axis=-1, keepdims=True))
        dx_ref[...] = dx
        dxb_ref[...] = _bf(dx)

        @pl.when(pl.program_id(0) == 0)
        def _():
            dg_ref[...] = jnp.zeros_like(dg_ref)
            loss_ref[...] = jnp.zeros_like(loss_ref)

        dg_ref[...] += jnp.sum(dy * xn, axis=0, keepdims=True)
        loss_ref[...] += (0.5 / d) * jnp.sum(jnp.sum(err * err, axis=1, keepdims=True), axis=0, keepdims=True)

    return pl.pallas_call(
        body, name=name, grid=(t // tm,),
        in_specs=[pl.BlockSpec((tm, d), lambda i: (i, 0)), pl.BlockSpec((1, d), lambda i: (0, 0)),
                  pl.BlockSpec((tm, d), lambda i: (i, 0))],
        out_specs=[pl.BlockSpec((tm, d), lambda i: (i, 0)), pl.BlockSpec((tm, d), lambda i: (i, 0)),
                   pl.BlockSpec((1, d), lambda i: (0, 0)), pl.BlockSpec((1, 1), lambda i: (0, 0))],
        out_shape=[jax.ShapeDtypeStruct((t, d), _F32), jax.ShapeDtypeStruct((t, d), _BF),
                   jax.ShapeDtypeStruct((1, d), _F32), jax.ShapeDtypeStruct((1, 1), _F32)],
        compiler_params=_params("arbitrary"),
    )(x, g, target)


def _pad_rows(x, pad):
    return jnp.concatenate([x, jnp.zeros((pad, x.shape[1]), x.dtype)], axis=0)


def _shift_down(xp, s):
    return xp if s == 0 else pltpu.roll(xp, s, 0)


def _shift_up(xp, s):
    return xp if s == 0 else pltpu.roll(xp, xp.shape[0] - s, 0)


def _taps3(xp):
    one = _shift_down(xp, 1)
    return xp, one, _shift_down(one, 1)


def _conv3_taps(taps, w_ref):
    return w_ref[2:3, :] * taps[0] + w_ref[1:2, :] * taps[1] + w_ref[0:1, :] * taps[2]


def _conv3(xp, w_ref):
    return _conv3_taps(_taps3(xp), w_ref)


def _conv3_t(dyp, w_ref):
    one = _shift_up(dyp, 1)
    return w_ref[2:3, :] * dyp + w_ref[1:2, :] * one + w_ref[0:1, :] * _shift_up(one, 1)


def _conv3_dw(dyp, taps):
    return [jnp.sum(dyp * taps[2 - k], axis=0, keepdims=True) for k in range(3)]


def _ffn_mid_fwd(up_pre, wf, n_ex, name):
    t, f2 = up_pre.shape
    f = f2 // 2
    s = t // n_ex
    cb = MIX_W
    nb = f // cb

    def body(ug_ref, uv_ref, wg_ref, wv_ref, act_ref, gf_ref, vf_ref):
        gf = _conv3(_pad_rows(ug_ref[...].astype(_F32), 8), wg_ref)[:s]
        vf = _conv3(_pad_rows(uv_ref[...].astype(_F32), 8), wv_ref)[:s]
        act_ref[...] = _bf(gf * _sigmoid(gf) * vf)
        gf_ref[...] = _bf(gf)
        vf_ref[...] = _bf(vf)

    out = pl.BlockSpec((s, cb), lambda e, j: (e, j))
    return pl.pallas_call(
        body, name=name, grid=(n_ex, nb),
        in_specs=[pl.BlockSpec((s, cb), lambda e, j: (e, j)), pl.BlockSpec((s, cb), lambda e, j: (e, j + nb)),
                  pl.BlockSpec((K_FFN, cb), lambda e, j: (0, j)), pl.BlockSpec((K_FFN, cb), lambda e, j: (0, j + nb))],
        out_specs=[out, out, out],
        out_shape=[jax.ShapeDtypeStruct((t, f), _BF)] * 3,
        compiler_params=_params("parallel", "parallel"),
    )(up_pre, up_pre, wf, wf)


def _ffn_mid_bwd(up_pre, conv_g, conv_v, wf, dact, n_ex, name):
    t, f2 = up_pre.shape
    f = f2 // 2
    s = t // n_ex
    cb = MIX_W
    nb = f // cb

    def body(ug_ref, uv_ref, gf_ref, vf_ref, wg_ref, wv_ref, da_ref, dug_ref, duv_ref, dwg_ref, dwv_ref):
        gf = _ld(gf_ref)
        vf = _ld(vf_ref)
        sg = _sigmoid(gf)
        da = _ld(da_ref)

        @pl.when(pl.program_id(1) == 0)
        def _():
            dwg_ref[...] = jnp.zeros_like(dwg_ref)
            dwv_ref[...] = jnp.zeros_like(dwv_ref)

        def finish(dpost, w_ref, x_ref, du_ref, dw_ref):
            ahead = [_pad_rows(dpost, 8)]
            ahead.append(_shift_up(ahead[0], 1))
            ahead.append(_shift_up(ahead[1], 1))
            du_ref[...] = _bf((w_ref[2:3, :] * ahead[0] + w_ref[1:2, :] * ahead[1] + w_ref[0:1, :] * ahead[2])[:s])
            x = _ld(x_ref)
            for k in range(K_FFN):
                dw_ref[k:k + 1, :] += jnp.sum(ahead[2 - k][:s] * x, axis=0, keepdims=True)

        finish(da * vf * sg * (1.0 + gf * (1.0 - sg)), wg_ref, ug_ref, dug_ref, dwg_ref)
        finish(da * gf * sg, wv_ref, uv_ref, duv_ref, dwv_ref)

    return pl.pallas_call(
        body, name=name, grid=(nb, n_ex),
        in_specs=[pl.BlockSpec((s, cb), lambda j, e: (e, j)), pl.BlockSpec((s, cb), lambda j, e: (e, j + nb)),
                  pl.BlockSpec((s, cb), lambda j, e: (e, j)), pl.BlockSpec((s, cb), lambda j, e: (e, j)),
                  pl.BlockSpec((K_FFN, cb), lambda j, e: (0, j)), pl.BlockSpec((K_FFN, cb), lambda j, e: (0, j + nb)),
                  pl.BlockSpec((s, cb), lambda j, e: (e, j))],
        out_specs=[pl.BlockSpec((s, cb), lambda j, e: (e, j)), pl.BlockSpec((s, cb), lambda j, e: (e, j)),
                   pl.BlockSpec((K_FFN, cb), lambda j, e: (0, j)), pl.BlockSpec((K_FFN, cb), lambda j, e: (0, j))],
        out_shape=[jax.ShapeDtypeStruct((t, f), _BF), jax.ShapeDtypeStruct((t, f), _BF),
                   jax.ShapeDtypeStruct((K_FFN, f), _F32), jax.ShapeDtypeStruct((K_FFN, f), _F32)],
        compiler_params=_params("parallel", "arbitrary"),
    )(up_pre, up_pre, conv_g, conv_v, wf, wf, dact)


def _pcol(s, j):
    return pl.BlockSpec((s, MIX_W), lambda e, j=j: (e, j))


def _vec(rows=1):
    return pl.BlockSpec((rows, MIX_W), lambda e: (0, 0))


def _mix_a_fwd(p, wa, n_ex, name):
    t = p.shape[0]
    s = t // n_ex

    def body(gb_ref, gc_ref, ha_ref, w_ref, y_ref):
        cv = _conv3(_pad_rows(_ld(gc_ref) * _ld(ha_ref), 8), w_ref)[:s]
        y_ref[...] = _bf(_ld(gb_ref) * cv)

    return pl.pallas_call(
        body, name=name, grid=(n_ex,),
        in_specs=[_pcol(s, 0), _pcol(s, 1), _pcol(s, 2), _vec(K_SHORT)],
        out_specs=pl.BlockSpec((s, MIX_W), lambda e: (e, 0)),
        out_shape=jax.ShapeDtypeStruct((t, MIX_W), _BF),
        compiler_params=_params("parallel"),
    )(p, p, p, wa)


def _mix_a_bwd(p, wa, dmix, n_ex, name):
    t = p.shape[0]
    s = t // n_ex

    def body(gb_ref, gc_ref, ha_ref, w_ref, dy_ref, dp_ref, dw_ref):
        gc = _ld(gc_ref)
        ha = _ld(ha_ref)
        up = _taps3(_pad_rows(gc * ha, 8))
        cv = _conv3_taps(up, w_ref)[:s]
        dy = _ld(dy_ref)
        dcvp = _pad_rows(dy * _ld(gb_ref), 8)
        du = _conv3_t(dcvp, w_ref)[:s]
        dp_ref[:, 0:MIX_W] = _bf(dy * cv)
        dp_ref[:, MIX_W:2 * MIX_W] = _bf(du * ha)
        dp_ref[:, 2 * MIX_W:3 * MIX_W] = _bf(du * gc)

        @pl.when(pl.program_id(0) == 0)
        def _():
            dw_ref[...] = jnp.zeros_like(dw_ref)

        rows = _conv3_dw(dcvp, up)
        for k in range(3):
            dw_ref[k:k + 1, :] += rows[k]

    return pl.pallas_call(
        body, name=name, grid=(n_ex,),
        in_specs=[_pcol(s, 0), _pcol(s, 1), _pcol(s, 2), _vec(K_SHORT), _pcol(s, 0)],
        out_specs=[pl.BlockSpec((s, 3 * MIX_W), lambda e: (e, 0)), _vec(K_SHORT)],
        out_shape=[jax.ShapeDtypeStruct((t, 3 * MIX_W), _BF), jax.ShapeDtypeStruct((K_SHORT, MIX_W), _F32)],
        compiler_params=_params("arbitrary"),
    )(p, p, p, wa, dmix)


CONF_PAD = 32
CONF_ROWS = 64
_CONF_LANES = (slice(0, 128), slice(128, 256))


def _conf_taps(win, ahead):
    n = CONF_ROWS + CONF_PAD
    for b in range(8):
        rot = win if b == 0 else pltpu.roll(win, (n - b) if ahead else b, 0)
        for a in range(4):
            if 8 * a + b < K_CONF:
                yield rot, 8 * a + b, (8 * a) if ahead else (CONF_PAD - 8 * a)


def _ln_fwd(x, g, b):
    mu = jnp.mean(x, axis=-1, keepdims=True)
    xc = x - mu
    rstd = lax.rsqrt(jnp.mean(xc * xc, axis=-1, keepdims=True) + LN_EPS)
    xhat = xc * rstd
    return xhat * g + b, xhat, rstd


def _ln_bwd(dy, xhat, rstd, g):
    dxh = dy * g
    return rstd * (dxh - jnp.mean(dxh, axis=-1, keepdims=True) - xhat * jnp.mean(dxh * xhat, axis=-1, keepdims=True))


def _mix_b_fwd(p, wb, bb, lg, lb, n_ex, name):
    t = p.shape[0]
    s = t // n_ex

    def body(val_ref, gat_ref, w_ref, bb_ref, lg_ref, lb_ref, y_ref, cb_ref, xpad):
        xpad[0:CONF_PAD, :] = jnp.zeros((CONF_PAD, MIX_W), _F32)
        xpad[CONF_PAD:, :] = _ld(val_ref) * _sigmoid(_ld(gat_ref))

        def chunk(c, carry):
            r0 = pl.multiple_of(c * CONF_ROWS, CONF_ROWS)
            for lanes in _CONF_LANES:
                acc = None
                for rot, sh, lo in _conf_taps(xpad[pl.ds(r0, CONF_ROWS + CONF_PAD), lanes], False):
                    term = w_ref[K_CONF - 1 - sh:K_CONF - sh, lanes] * rot[lo:lo + CONF_ROWS]
                    acc = term if acc is None else acc + term
                cb_ref[pl.ds(r0, CONF_ROWS), lanes] = acc + bb_ref[:, lanes]
            return carry

        lax.fori_loop(0, s // CONF_ROWS, chunk, 0)
        yl, _, _ = _ln_fwd(cb_ref[...], lg_ref[...], lb_ref[...])
        y_ref[...] = _bf(yl * _sigmoid(yl))

    return pl.pallas_call(
        body, name=name, grid=(n_ex,),
        in_specs=[_pcol(s, 3), _pcol(s, 4), _vec(K_CONF), _vec(), _vec(), _vec()],
        out_specs=[pl.BlockSpec((s, MIX_W), lambda e: (e, 0)), pl.BlockSpec((s, MIX_W), lambda e: (e, 0))],
        out_shape=[jax.ShapeDtypeStruct((t, MIX_W), _BF), jax.ShapeDtypeStruct((t, MIX_W), _F32)],
        scratch_shapes=[pltpu.VMEM((CONF_PAD + s, MIX_W), _F32)],
        compiler_params=_params("parallel"),
    )(p, p, wb, bb, lg, lb)


def _mix_b_bwd(p, cb, wb, lg, lb, dmix, n_ex, name):
    t = p.shape[0]
    s = t // n_ex

    def body(val_ref, gat_ref, cb_ref, w_ref, lg_ref, lb_ref, dy_ref, dp_ref, dw_ref, dbb_ref, dlg_ref, dlb_ref,
             xpad, dpad, dglu_s, dw_acc):
        @pl.when(pl.program_id(0) == 0)
        def _():
            for r in (dw_ref, dbb_ref, dlg_ref, dlb_ref):
                r[...] = jnp.zeros_like(r)

        yl, xhat, rstd = _ln_fwd(cb_ref[...], lg_ref[...], lb_ref[...])
        sy = _sigmoid(yl)
        dyl = _ld(dy_ref) * sy * (1.0 + yl * (1.0 - sy))
        dlg_ref[...] += jnp.sum(dyl * xhat, axis=0, keepdims=True)
        dlb_ref[...] += jnp.sum(dyl, axis=0, keepdims=True)
        dcb = _ln_bwd(dyl, xhat, rstd, lg_ref[...])
        dbb_ref[...] += jnp.sum(dcb, axis=0, keepdims=True)

        val = _ld(val_ref)
        sg = _sigmoid(_ld(gat_ref))
        xpad[0:CONF_PAD, :] = jnp.zeros((CONF_PAD, MIX_W), _F32)
        xpad[CONF_PAD:, :] = val * sg
        dpad[0:s, :] = dcb
        dpad[s:, :] = jnp.zeros((CONF_PAD, MIX_W), _F32)
        dw_acc[...] = jnp.zeros_like(dw_acc)

        def chunk(c, carry):
            r0 = pl.multiple_of(c * CONF_ROWS, CONF_ROWS)
            for lanes in _CONF_LANES:
                d_win = dpad[pl.ds(r0, CONF_ROWS + CONF_PAD), lanes]
                d_rows = d_win[0:CONF_ROWS]
                acc = None
                for rot, sh, lo in _conf_taps(d_win, True):
                    term = w_ref[K_CONF - 1 - sh:K_CONF - sh, lanes] * rot[lo:lo + CONF_ROWS]
                    acc = term if acc is None else acc + term
                dglu_s[pl.ds(r0, CONF_ROWS), lanes] = acc
                for rot, sh, lo in _conf_taps(xpad[pl.ds(r0, CONF_ROWS + CONF_PAD), lanes], False):
                    prod = d_rows * rot[lo:lo + CONF_ROWS]
                    dw_acc[K_CONF - 1 - sh, :, lanes] += jnp.sum(prod.reshape(CONF_ROWS // 8, 8, 128), axis=0)
            return carry

        lax.fori_loop(0, s // CONF_ROWS, chunk, 0)
        dw_ref[...] += jnp.sum(dw_acc[...], axis=1)
        dglu = dglu_s[...]
        dp_ref[:, 0:MIX_W] = _bf(dglu * sg)
        dp_ref[:, MIX_W:2 * MIX_W] = _bf(dglu * val * sg * (1.0 - sg))

    return pl.pallas_call(
        body, name=name, grid=(n_ex,),
        in_specs=[_pcol(s, 3), _pcol(s, 4), pl.BlockSpec((s, MIX_W), lambda e: (e, 0)), _vec(K_CONF), _vec(), _vec(),
                  _pcol(s, 1)],
        out_specs=[pl.BlockSpec((s, 2 * MIX_W), lambda e: (e, 0)), _vec(K_CONF), _vec(), _vec(), _vec()],
        out_shape=[jax.ShapeDtypeStruct((t, 2 * MIX_W), _BF), jax.ShapeDtypeStruct((K_CONF, MIX_W), _F32),
                   jax.ShapeDtypeStruct((1, MIX_W), _F32), jax.ShapeDtypeStruct((1, MIX_W), _F32),
                   jax.ShapeDtypeStruct((1, MIX_W), _F32)],
        scratch_shapes=[pltpu.VMEM((CONF_PAD + s, MIX_W), _F32), pltpu.VMEM((s + CONF_PAD, MIX_W), _F32),
                        pltpu.VMEM((s, MIX_W), _F32), pltpu.VMEM((K_CONF, 8, MIX_W), _F32)],
        compiler_params=_params("arbitrary"),
    )(p, p, cb, wb, lg, lb, dmix)


_INV_SQRT2 = 0.7071067811865476
_INV_SQRT2PI = 0.3989422804014327


def _gelu(x):
    return 0.5 * x * (1.0 + lax.erf(x * _INV_SQRT2))


def _gelu_grad(x):
    return 0.5 * (1.0 + lax.erf(x * _INV_SQRT2)) + x * _INV_SQRT2PI * jnp.exp(-0.5 * x * x)


def _head_masks(width=MIX_W):
    lane = lax.broadcasted_iota(jnp.int32, (1, width), 1)
    return [(lane >= h * HEAD_DIM) & (lane < (h + 1) * HEAD_DIM) for h in range(N_HEADS)]


def _tril_mask():
    r = lax.broadcasted_iota(jnp.int32, (CHUNK, CHUNK), 0)
    c = lax.broadcasted_iota(jnp.int32, (CHUNK, CHUNK), 1)
    return c <= r


def _sgu_apply(ws_ref, x3, transpose):
    n = x3.shape[0]
    tril = _tril_mask()
    masks = _head_masks()
    xb = _bf(x3)
    out = jnp.zeros(x3.shape, _F32)
    for h in range(N_HEADS):
        w = _bf(jnp.where(tril, ws_ref[h], 0.0))
        wb = jnp.broadcast_to(w[None], (n, CHUNK, CHUNK))
        dims = (((1,), (1,)), ((0,), (0,))) if transpose else (((2,), (1,)), ((0,), (0,)))
        r = lax.dot_general(wb, xb, dims, preferred_element_type=_F32)
        out = out + jnp.where(masks[h][None], r, 0.0)
    return out


def _mix_c_fwd(p, lg, lb, ws, sb_full, n_ex, name):
    t = p.shape[0]
    s = t // n_ex
    nc = s // CHUNK

    def body(pu_ref, pv_ref, lg_ref, lb_ref, ws_ref, sb_ref, y_ref):
        u = _gelu(_ld(pu_ref))
        vl, _, _ = _ln_fwd(_gelu(_ld(pv_ref)), lg_ref[...], lb_ref[...])
        sp = _sgu_apply(ws_ref, vl.reshape(nc, CHUNK, MIX_W), False) + sb_ref[...][None]
        y_ref[...] = _bf(u * sp.reshape(s, MIX_W))

    return pl.pallas_call(
        body, name=name, grid=(n_ex,),
        in_specs=[_pcol(s, 5), _pcol(s, 6), _vec(), _vec(),
                  pl.BlockSpec((N_HEADS, CHUNK, CHUNK), lambda e: (0, 0, 0)), pl.BlockSpec((CHUNK, MIX_W), lambda e: (0, 0))],
        out_specs=pl.BlockSpec((s, MIX_W), lambda e: (e, 0)),
        out_shape=jax.ShapeDtypeStruct((t, MIX_W), _BF),
        compiler_params=_params("parallel"),
    )(p, p, lg, lb, ws, sb_full)


def _mix_c_bwd(p, lg, lb, ws, sb_full, dmix, n_ex, name):
    t = p.shape[0]
    s = t // n_ex
    nc = s // CHUNK

    def body(pu_ref, pv_ref, lg_ref, lb_ref, ws_ref, sb_ref, dy_ref, dp_ref, dlg_ref, dlb_ref, dws_ref, dsb_ref):
        @pl.when(pl.program_id(0) == 0)
        def _():
            for r in (dlg_ref, dlb_ref, dws_ref, dsb_ref):
                r[...] = jnp.zeros_like(r)

        pu = _ld(pu_ref)
        pv = _ld(pv_ref)
        u = _gelu(pu)
        vl, xhat, rstd = _ln_fwd(_gelu(pv), lg_ref[...], lb_ref[...])
        vl3 = vl.reshape(nc, CHUNK, MIX_W)
        sp = _sgu_apply(ws_ref, vl3, False) + sb_ref[...][None]
        dy = _ld(dy_ref)
        dp_ref[:, 0:MIX_W] = _bf(dy * sp.reshape(s, MIX_W) * _gelu_grad(pu))
        dsp3 = (dy * u).reshape(nc, CHUNK, MIX_W)
        dsb_full = jnp.sum(dsp3, axis=0)
        masks = _head_masks()
        tril = _tril_mask()
        dspb = _bf(dsp3)
        vlb = _bf(vl3)
        for h in range(N_HEADS):
            dsb_ref[:, h:h + 1] += jnp.sum(jnp.where(masks[h], dsb_full, 0.0), axis=1, keepdims=True)
            dm = jnp.where(masks[h][None], dspb, jnp.zeros_like(dspb))
            g3 = lax.dot_general(dm, vlb, (((2,), (2,)), ((0,), (0,))), preferred_element_type=_F32)
            dws_ref[h] += jnp.where(tril, jnp.sum(g3, axis=0), 0.0)
        dvl = _sgu_apply(ws_ref, dsp3, True).reshape(s, MIX_W)
        dlg_ref[...] += jnp.sum(dvl * xhat, axis=0, keepdims=True)
        dlb_ref[...] += jnp.sum(dvl, axis=0, keepdims=True)
        dp_ref[:, MIX_W:2 * MIX_W] = _bf(_ln_bwd(dvl, xhat, rstd, lg_ref[...]) * _gelu_grad(pv))

    return pl.pallas_call(
        body, name=name, grid=(n_ex,),
        in_specs=[_pcol(s, 5), _pcol(s, 6), _vec(), _vec(),
                  pl.BlockSpec((N_HEADS, CHUNK, CHUNK), lambda e: (0, 0, 0)), pl.BlockSpec((CHUNK, MIX_W), lambda e: (0, 0)),
                  _pcol(s, 2)],
        out_specs=[pl.BlockSpec((s, 2 * MIX_W), lambda e: (e, 0)), _vec(), _vec(),
                   pl.BlockSpec((N_HEADS, CHUNK, CHUNK), lambda e: (0, 0, 0)), pl.BlockSpec((CHUNK, N_HEADS), lambda e: (0, 0))],
        out_shape=[jax.ShapeDtypeStruct((t, 2 * MIX_W), _BF), jax.ShapeDtypeStruct((1, MIX_W), _F32),
                   jax.ShapeDtypeStruct((1, MIX_W), _F32), jax.ShapeDtypeStruct((N_HEADS, CHUNK, CHUNK), _F32),
                   jax.ShapeDtypeStruct((CHUNK, N_HEADS), _F32)],
        compiler_params=_params("arbitrary"),
    )(p, p, lg, lb, ws, sb_full, dmix)


D_QBLOCK = 256
HEAD_COLS = N_HEADS * KV_BLOCK


def _stack_heads(x3):
    return jnp.stack([_bf(jnp.where(m[None], x3, 0.0)) for m in _head_masks()], axis=1)


def _stack_heads_rows(x):
    return jnp.concatenate([_bf(jnp.where(m, x, 0.0)) for m in _head_masks()], axis=0)


def _cols_to_rows(x):
    return jnp.concatenate([x[:, h * KV_BLOCK:(h + 1) * KV_BLOCK] for h in range(N_HEADS)], axis=0)


def _head_sums(x):
    return [jnp.sum(x[:, h * KV_BLOCK:(h + 1) * KV_BLOCK], axis=1, keepdims=True) for h in range(N_HEADS)]


def _spread(cols):
    tq = cols[0].shape[0]
    return jnp.concatenate([jnp.broadcast_to(c, (tq, KV_BLOCK)) for c in cols], axis=1)


def _pair_dot(x, m2):
    half = 2 * KV_BLOCK
    xb = _bf(x)
    return jnp.concatenate([_dot(xb[:, :half], m2), _dot(xb[:, half:], m2)], axis=1)


def _tri2(lower):
    n = 2 * KV_BLOCK
    r = lax.broadcasted_iota(jnp.int32, (n, n), 0)
    c = lax.broadcasted_iota(jnp.int32, (n, n), 1)
    same = (r >= KV_BLOCK) == (c >= KV_BLOCK)
    return _bf(jnp.where(same & (r > c if lower else r < c), 1.0, 0.0))


def _sb_scores(qs, kc, j, t_idx, on_diagonal):
    z = _dot_nt(qs, kc)
    lb = jnp.minimum(z, 0.0) - jnp.log(1.0 + jnp.exp(-jnp.abs(z)))
    if not on_diagonal:
        return (lambda x: x), lb, lb - z
    lane = lax.broadcasted_iota(jnp.int32, (1, HEAD_COLS), 1)
    valid = (j * KV_BLOCK + (lane & (KV_BLOCK - 1))) < t_idx
    keep = lambda x: jnp.where(valid, x, 0.0)
    return keep, lb, keep(lb - z)


RUN_LANES = 128


def _run_lane(j, h):
    return lax.broadcasted_iota(jnp.int32, (1, RUN_LANES), 1) == j * N_HEADS + h


def _d_qblock(s):
    return D_QBLOCK if s % D_QBLOCK == 0 else KV_BLOCK


def _mix_d_fwd(p, n_ex, name):
    t = p.shape[0]
    s = t // n_ex
    tq = _d_qblock(s)
    nq = s // tq
    r = tq // KV_BLOCK
    nb = s // KV_BLOCK
    assert nb * N_HEADS <= RUN_LANES

    def body(q_ref, k_ref, v_ref, y_ref, runs_ref, kc, vc):
        i = pl.program_id(1)

        @pl.when(i == 0)
        def _():
            kc[...] = _stack_heads(k_ref[...].reshape(nb, KV_BLOCK, MIX_W))
            vc[...] = _stack_heads(v_ref[...].reshape(nb, KV_BLOCK, MIX_W))

        qs = _bf(_ld(q_ref) * (HEAD_DIM ** -0.5))
        t_idx = i * tq + lax.broadcasted_iota(jnp.int32, (tq, 1), 0)
        after_m = _tri2(True)
        nkb = (i + 1) * r

        runs_ref[...] = jnp.zeros_like(runs_ref)

        def one_block(j, runs, acc, on_diagonal):
            keep, lb, c = _sb_scores(qs, kc[j].reshape(HEAD_COLS, MIX_W), j, t_idx, on_diagonal)
            a = keep(jnp.exp(lb + _pair_dot(c, after_m) + _spread(runs)))
            acc = acc + _dot(_bf(a), vc[j].reshape(HEAD_COLS, MIX_W))
            kept = runs_ref[...]
            for h in range(N_HEADS):
                kept = jnp.where(_run_lane(j, h), runs[h], kept)
            runs_ref[...] = kept
            return tuple(ru + cs for ru, cs in zip(runs, _head_sums(c))), acc

        def trip(last, carry, on_diagonal):
            runs, acc = carry
            for sub in range(r):
                runs, acc = one_block(last - sub, runs, acc, on_diagonal)
            return runs, acc

        zero = jnp.zeros((tq, 1), _F32)
        carry = trip(nkb - 1, ((zero,) * N_HEADS, jnp.zeros((tq, MIX_W), _F32)), True)
        _, acc = lax.fori_loop(0, i, lambda m, carry: trip(nkb - 1 - (m + 1) * r, carry, False), carry)
        y_ref[...] = _bf(acc)

    return pl.pallas_call(
        body, name=name, grid=(n_ex, nq),
        in_specs=[pl.BlockSpec((tq, MIX_W), lambda e, i: (e * nq + i, 7)), pl.BlockSpec((s, MIX_W), lambda e, i: (e, 8)),
                  pl.BlockSpec((s, MIX_W), lambda e, i: (e, 9))],
        out_specs=[pl.BlockSpec((tq, MIX_W), lambda e, i: (e * nq + i, 0)),
                   pl.BlockSpec((tq, RUN_LANES), lambda e, i: (e * nq + i, 0))],
        out_shape=[jax.ShapeDtypeStruct((t, MIX_W), _BF), jax.ShapeDtypeStruct((t, RUN_LANES), _F32)],
        scratch_shapes=[pltpu.VMEM((nb, N_HEADS, KV_BLOCK, MIX_W), _BF), pltpu.VMEM((nb, N_HEADS, KV_BLOCK, MIX_W), _BF)],
        compiler_params=_params("parallel", "arbitrary"),
    )(p, p, p)


def _mix_d_bwd(p, kept_runs, dmix, n_ex, name):
    t = p.shape[0]
    s = t // n_ex
    tq = _d_qblock(s)
    nq = s // tq
    r = tq // KV_BLOCK
    nb = s // KV_BLOCK
    scale = HEAD_DIM ** -0.5

    def body(q_ref, k_ref, v_ref, runs_ref, do_ref, dq_ref, dk_ref, dv_ref, kc, vc):
        i = pl.program_id(1)

        @pl.when(i == 0)
        def _():
            kc[...] = _stack_heads(k_ref[...].reshape(nb, KV_BLOCK, MIX_W))
            vc[...] = _stack_heads(v_ref[...].reshape(nb, KV_BLOCK, MIX_W))
            dk_ref[...] = jnp.zeros_like(dk_ref)
            dv_ref[...] = jnp.zeros_like(dv_ref)

        q_scaled = _ld(q_ref) * scale
        qs = _bf(q_scaled)
        do = do_ref[...]
        dob = _bf(do)
        q_rows = _stack_heads_rows(q_scaled)
        do_rows = _stack_heads_rows(do)
        kept = runs_ref[...]
        t_idx = i * tq + lax.broadcasted_iota(jnp.int32, (tq, 1), 0)
        after_m = _tri2(True)
        before_m = _tri2(False)
        nkb = (i + 1) * r
        zero = jnp.zeros((tq, 1), _F32)

        def trip(first, carry, on_diagonal):
            for sub in range(r):
                carry = one_block(first + sub, carry, on_diagonal)
            return carry

        def one_block(j, carry, on_diagonal):
            pres, dq = carry
            rows = pl.ds(pl.multiple_of(j * KV_BLOCK, KV_BLOCK), KV_BLOCK)
            kj = kc[j].reshape(HEAD_COLS, MIX_W)
            keep, lb, c = _sb_scores(qs, kj, j, t_idx, on_diagonal)
            runs = [jnp.sum(jnp.where(_run_lane(j, h), kept, 0.0), axis=1, keepdims=True) for h in range(N_HEADS)]
            a = keep(jnp.exp(lb + _pair_dot(c, after_m) + _spread(runs)))
            g = a * _dot_nt(dob, vc[j].reshape(HEAD_COLS, MIX_W))
            before = _pair_dot(g, before_m) + _spread(pres)
            sig = jnp.exp(lb)
            dz = _bf(keep(g * (1.0 - sig) - sig * before))
            dk_ref[rows, :] += _dot_tn(_cols_to_rows(dz), q_rows)
            dv_ref[rows, :] += _dot_tn(_cols_to_rows(_bf(a)), do_rows)
            return tuple(pr + gs for pr, gs in zip(pres, _head_sums(g))), dq + _dot(dz, kj)

        init = ((zero,) * N_HEADS, jnp.zeros((tq, MIX_W), _F32))
        carry = lax.fori_loop(0, i, lambda m, carry: trip(m * r, carry, False), init)
        _, dq = trip(i * r, carry, True)
        dq_ref[...] = _bf(dq * scale)

    return pl.pallas_call(
        body, name=name, grid=(n_ex, nq),
        in_specs=[pl.BlockSpec((tq, MIX_W), lambda e, i: (e * nq + i, 7)), pl.BlockSpec((s, MIX_W), lambda e, i: (e, 8)),
                  pl.BlockSpec((s, MIX_W), lambda e, i: (e, 9)), pl.BlockSpec((tq, RUN_LANES), lambda e, i: (e * nq + i, 0)),
                  pl.BlockSpec((tq, MIX_W), lambda e, i: (e * nq + i, 3))],
        out_specs=[pl.BlockSpec((tq, MIX_W), lambda e, i: (e * nq + i, 0)), pl.BlockSpec((s, MIX_W), lambda e, i: (e, 0)),
                   pl.BlockSpec((s, MIX_W), lambda e, i: (e, 0))],
        out_shape=[jax.ShapeDtypeStruct((t, MIX_W), _BF), jax.ShapeDtypeStruct((t, MIX_W), _F32),
                   jax.ShapeDtypeStruct((t, MIX_W), _F32)],
        scratch_shapes=[pltpu.VMEM((nb, N_HEADS, KV_BLOCK, MIX_W), _BF), pltpu.VMEM((nb, N_HEADS, KV_BLOCK, MIX_W), _BF)],
        compiler_params=_params("parallel", "arbitrary"),
    )(p, p, p, kept_runs, dmix)


def _fwd_mix(x, w, l, n_ex):
    p, h1 = _norm_mm(x, w["norm1_g"][l], w["w_in_t"][l], "in_proj")
    y_a = _mix_a_fwd(p, w["conv_a_w"][l], n_ex, "mix_a_fwd")
    y_b, cb = _mix_b_fwd(p, w["conv_b_w"][l], w["conv_b_b"][l], w["ln_b_g"][l], w["ln_b_b"][l], n_ex, "mix_b_fwd")
    y_c = _mix_c_fwd(p, w["ln_c_g"][l], w["ln_c_b"][l], w["sgu_w"][l], w["sgu_b_full"][l], n_ex, "mix_c_fwd")
    y_d, runs_d = _mix_d_fwd(p, n_ex, "mix_d_fwd")
    return dict(x=x, h1=h1, p=p, cb=cb, runs_d=runs_d, mix=(y_a, y_b, y_c, y_d))


def _fwd_ffn(st, w, l, n_ex):
    x1 = _mm_res(st["mix"], w["w_out"][l], st["x"], "out_proj")
    up_pre, h2 = _norm_mm(x1, w["norm2_g"][l], w["w_up_t"][l], "up_proj")
    act, conv_g, conv_v = _ffn_mid_fwd(up_pre, w["conv_f_w"][l], n_ex, "ffn_mid_fwd")
    st.update(x1=x1, h2=h2, up_pre=up_pre, act=act, conv_g=conv_g, conv_v=conv_v)
    return _mm_res((act,), w["w_down"][l], x1, "down_proj")


def _bwd_ffn(st, w, l, dx, dxb, n_ex):
    g = {}
    dact = _mm_nt(dxb, w["w_down"][l], "down_proj_dx")
    g["w_down"] = _mm_tn(st["act"], dxb, "down_proj_dw", _BF)
    dup_g, dup_v, dwf_g, dwf_v = _ffn_mid_bwd(
        st["up_pre"], st["conv_g"], st["conv_v"], w["conv_f_w"][l], dact, n_ex, "ffn_mid_bwd")
    g["conv_f_w"] = jnp.concatenate([dwf_g, dwf_v], axis=1)
    dx, dxb, g["norm2_g"] = _mm_normbwd((dup_g, dup_v), w["w_up_t"][l], st["x1"], w["norm2_g"][l], dx, "up_proj_dx")
    g["w_up_t"] = _mm_tn_halves(dup_g, dup_v, st["h2"], "up_proj_dw")
    return dx, dxb, g


def _bwd_out_proj(st, w, l, dxb):
    return _mm_nt(dxb, w["w_out"][l], "out_proj_dx"), _mm_tn_parts(st["mix"], dxb, "out_proj_dw")


def _bwd_mixers(st, w, l, dx, dmix, n_ex):
    g = {}
    p = st["p"]
    dp_a, g["conv_a_w"] = _mix_a_bwd(p, w["conv_a_w"][l], dmix, n_ex, "mix_a_bwd")
    dp_b, g["conv_b_w"], g["conv_b_b"], g["ln_b_g"], g["ln_b_b"] = _mix_b_bwd(
        p, st["cb"], w["conv_b_w"][l], w["ln_b_g"][l], w["ln_b_b"][l], dmix, n_ex, "mix_b_bwd")
    dp_c, g["ln_c_g"], g["ln_c_b"], g["sgu_w"], g["sgu_b_t"] = _mix_c_bwd(
        p, w["ln_c_g"][l], w["ln_c_b"][l], w["sgu_w"][l], w["sgu_b_full"][l], dmix, n_ex, "mix_c_bwd")
    dq, dk, dv = _mix_d_bwd(p, st["runs_d"], dmix, n_ex, "mix_d_bwd")
    dp = (dp_a, dp_b, dp_c, dq, dk, dv)
    dx, dxb, g["norm1_g"] = _mm_normbwd(dp, w["w_in_t"][l], st["x"], w["norm1_g"][l], dx, "in_proj_dx")
    return dx, dxb, g, dp


def _bwd_mix(st, w, l, dx, dxb, n_ex):
    dmix, dw_out = _bwd_out_proj(st, w, l, dxb)
    dx, dxb, g, dp = _bwd_mixers(st, w, l, dx, dmix, n_ex)
    g["w_out"] = dw_out
    g["w_in_t"] = _mm_tn_parts(dp, st["h1"], "in_proj_dw")
    return dx, dxb, g


_MESH = pl.DeviceIdType.MESH
_ANY = pl.BlockSpec(memory_space=pl.ANY)


def _position():
    return lax.axis_index("x"), lax.axis_index("y"), lax.axis_index("c")


def _flat(px, py, pc):
    return 4 * px + 2 * py + pc


def _all_gather(shard, name, after):
    r, c_ = shard.shape

    def body(x_ref, after_ref, out_ref, send_sems, recv_sems, local_sem):
        x, y, c = _position()
        me, sibling = (x, y, c), (x, y, 1 - c)
        chips = [(1 - x, y), (x, 1 - y), (1 - x, 1 - y)]

        def copy(k, block, to, src=None):
            slab = out_ref.at[_flat(*block)]
            return pltpu.make_async_remote_copy(
                src_ref=slab if src is None else src, dst_ref=slab, send_sem=send_sems.at[k], recv_sem=recv_sems.at[k],
                device_id=to, device_id_type=_MESH)

        mine = pltpu.make_async_copy(x_ref, out_ref.at[_flat(*me)], local_sem)
        mine.start()
        first = [copy(0, me, sibling, src=x_ref)]
        first += [copy(1 + j, me, (*chip, c), src=x_ref) for j, chip in enumerate(chips)]
        for cp in first:
            cp.start()
        passed = [copy(4 + j, (*chip, c), sibling) for j, chip in enumerate(chips)]
        for j, chip in enumerate(chips):
            copy(1 + j, (*chip, c), me).wait_recv()
            passed[j].start()
        copy(0, sibling, me).wait_recv()
        for j, chip in enumerate(chips):
            copy(4 + j, (*chip, 1 - c), me).wait_recv()
        for cp in first + passed:
            cp.wait_send()
        mine.wait()

    return pl.pallas_call(
        body, name=name, out_shape=jax.ShapeDtypeStruct((N_DEV, r, c_), shard.dtype),
        in_specs=[_ANY, _ANY], out_specs=_ANY,
        scratch_shapes=[pltpu.SemaphoreType.DMA((7,)), pltpu.SemaphoreType.DMA((7,)), pltpu.SemaphoreType.DMA],
    )(shard, after)


_HBM = pl.BlockSpec(memory_space=pltpu.HBM)
_SEM = pl.BlockSpec(memory_space=pltpu.SEMAPHORE)
_DATAFLOW = pltpu.SideEffectType.DATAFLOW_SIDE_EFFECTING


def _peers(x, y, c):
    return [((1 - x) if (k + 1) & 4 else x, (1 - y) if (k + 1) & 2 else y, (1 - c) if (k + 1) & 1 else c)
            for k in range(N_DEV - 1)]


def _direct_copies(src_refs, land_refs, send_sems, recv_sems, to_all):
    x, y, c = _position()
    my = _flat(x, y, c)
    out, back = [], []
    for m, (src_ref, land_ref) in enumerate(zip(src_refs, land_refs)):
        for k, peer in enumerate(_peers(x, y, c)):
            src = src_ref if to_all else src_ref.at[_flat(*peer)]
            n = m * (N_DEV - 1) + k
            sems = dict(send_sem=send_sems.at[n], recv_sem=recv_sems.at[n], device_id=peer, device_id_type=_MESH)
            out.append(pltpu.make_async_remote_copy(src_ref=src, dst_ref=land_ref.at[my], **sems))
            back.append(pltpu.make_async_remote_copy(src_ref=src, dst_ref=land_ref.at[_flat(*peer)], **sems))
    return out, back


def _exchange_start(srcs, to_all, after, name):
    n = len(srcs)
    n_sems = n * (N_DEV - 1)
    land_shapes = [(N_DEV,) + tuple(a.shape[-2:]) for a in srcs]

    def body(*refs):
        src_refs, land_refs = refs[:n], refs[n:2 * n]
        send_sems, recv_sems = refs[2 * n + 1], refs[2 * n + 2]
        token = refs[-1]
        for cp in _direct_copies(src_refs, land_refs, send_sems, recv_sems, to_all)[0]:
            cp.start()
        token[...] = jnp.zeros_like(token)

    lands = [pltpu.with_memory_space_constraint(lax.empty(shp, a.dtype), pltpu.HBM) for shp, a in zip(land_shapes, srcs)]
    outs = pl.pallas_call(
        body, name=name,
        out_shape=(pltpu.SemaphoreType.DMA((n_sems,)), pltpu.SemaphoreType.DMA((n_sems,)),
                   *[pltpu.HBM(a.shape, a.dtype) for a in srcs], *[pltpu.HBM(shp, a.dtype) for shp, a in zip(land_shapes, srcs)],
                   jax.ShapeDtypeStruct((8, 128), _F32)),
        in_specs=(_HBM,) * (2 * n) + (_ANY,),
        out_specs=(_SEM, _SEM) + (_HBM,) * (2 * n) + (pl.BlockSpec(memory_space=pltpu.VMEM),),
        input_output_aliases={i: 2 + i for i in range(2 * n)},
        compiler_params=pltpu.CompilerParams(has_side_effects=_DATAFLOW),
    )(*[pltpu.with_memory_space_constraint(a, pltpu.HBM) for a in srcs], *lands, after)
    return (outs[0], outs[1], outs[2:2 + n], outs[2 + n:2 + 2 * n], to_all), outs[-1]


def _exchange_wait(handle, after, name):
    send_sems, recv_sems, srcs, lands, to_all = handle
    n = len(srcs)

    def body(*refs):
        out, back = _direct_copies(refs[:n], refs[n:2 * n], refs[2 * n], refs[2 * n + 1], to_all)
        for cp in out:
            cp.wait_send()
        for cp in back:
            cp.wait_recv()

    outs = pl.pallas_call(
        body, name=name,
        out_shape=tuple(pltpu.HBM(a.shape, a.dtype) for a in (*srcs, *lands)),
        in_specs=(_HBM,) * (2 * n) + (_SEM, _SEM, _ANY), out_specs=(_HBM,) * (2 * n),
        input_output_aliases={i: i for i in range(2 * n)},
        compiler_params=pltpu.CompilerParams(has_side_effects=_DATAFLOW),
    )(*srcs, *lands, send_sems, recv_sems, after)
    return outs[:n], outs[n:]


def _with_own(landed, own):
    my = _flat(*_position())
    return lax.dynamic_update_slice(landed, own[None], (my, 0, 0))


def _sum_slabs(slabs, name):
    n, r, c_ = slabs.shape
    tr = _pick_tile(r, 16, max(16, (12 << 20) // (n * c_ * slabs.dtype.itemsize)))

    def body(x_ref, o_ref):
        acc = x_ref[0].astype(_F32)
        for k in range(1, n):
            acc = acc + x_ref[k].astype(_F32)
        o_ref[...] = acc

    return pl.pallas_call(
        body, name=name, grid=(r // tr,),
        in_specs=[pl.BlockSpec((n, tr, c_), lambda i: (0, i, 0))],
        out_specs=pl.BlockSpec((tr, c_), lambda i: (i, 0)),
        out_shape=jax.ShapeDtypeStruct((r, c_), _F32),
        compiler_params=_params("parallel"),
    )(slabs)


def _adamw(w, g, m, v, name):
    r, c_ = w.shape
    tr = _pick_tile(r, 8, 512)

    def body(w_ref, g_ref, m_ref, v_ref, d_ref, nm_ref, nv_ref):
        _adamw_refs(w_ref, g_ref, m_ref, v_ref, d_ref, nm_ref, nv_ref)

    spec = pl.BlockSpec((tr, c_), lambda i: (i, 0))
    shape = jax.ShapeDtypeStruct((r, c_), _F32)
    return pl.pallas_call(
        body, name=name, grid=(r // tr,), in_specs=[spec] * 4, out_specs=[spec] * 3, out_shape=[shape] * 3,
        compiler_params=_params("parallel"),
    )(w, g, m, v)


def _adamw_refs(w_ref, g_ref, m_ref, v_ref, d_ref, nm_ref, nv_ref):
    gv = g_ref[...]
    nm = ADAM_B1 * m_ref[...] + (1.0 - ADAM_B1) * gv
    nv = ADAM_B2 * v_ref[...] + (1.0 - ADAM_B2) * (gv * gv)
    m_hat = nm / (1.0 - ADAM_B1 ** ADAM_STEP)
    v_hat = nv / (1.0 - ADAM_B2 ** ADAM_STEP)
    d_ref[...] = -ADAM_LR * (m_hat / (jnp.sqrt(v_hat) + ADAM_EPS) + ADAM_WD * w_ref[...])
    nm_ref[...] = nm
    nv_ref[...] = nv


def _adamw_small(params, name):
    n = len(params)

    def body(*refs):
        for i in range(n):
            _adamw_refs(*refs[4 * i:4 * i + 4], *refs[4 * n + 3 * i:4 * n + 3 * i + 3])

    outs = pl.pallas_call(
        body, name=name,
        out_shape=[jax.ShapeDtypeStruct(p[0].shape, _F32) for p in params for _ in range(3)],
        compiler_params=pltpu.CompilerParams(vmem_limit_bytes=VMEM_LIMIT),
    )(*[a for p in params for a in p])
    return [tuple(outs[3 * i:3 * i + 3]) for i in range(n)]


_SMALL = ("norm1_g", "conv_a_w", "conv_b_w", "conv_b_b", "ln_b_g", "ln_b_b", "ln_c_g", "ln_c_b", "sgu_w", "sgu_b",
          "norm2_g", "conv_f_w", "final_g")
_CONV_SHARDED = ("conv_a_w", "conv_b_w", "conv_f_w")
_NAMES = ("norm1_g", "w_in", "conv_a_w", "conv_b_w", "conv_b_b", "ln_b_g", "ln_b_b", "ln_c_g", "ln_c_b", "sgu_w", "sgu_b",
          "w_out", "norm2_g", "w_up", "conv_f_w", "w_down", "final_g")


def _pack_rows(parts, lanes=128, row_multiple=8):
    flat = jnp.concatenate([a.reshape(-1) for a in parts])
    rows = -(-flat.shape[0] // lanes)
    rows = -(-rows // row_multiple) * row_multiple
    return jnp.pad(flat, (0, rows * lanes - flat.shape[0])).reshape(rows, lanes)


def _unpack_rows(packed, shapes):
    flat = packed.reshape(-1)
    out, off = [], 0
    for shp in shapes:
        size = 1
        for s in shp:
            size *= s
        out.append(flat[off:off + size].reshape(shp))
        off += size
    return out


def _gather_conv_weights(conv_a_w, conv_b_w, conv_f_w, after):
    shards = (conv_a_w, conv_b_w, conv_f_w)
    flat = _all_gather(_pack_rows(shards), "gather_conv_weights", after).reshape(N_DEV, -1)
    full, off = [], 0
    for s in shards:
        layers, taps, width = s.shape
        per_dev = flat[:, off:off + s.size].reshape(N_DEV, layers, taps, width)
        full.append(jnp.moveaxis(per_dev, 0, 2).reshape(layers, taps, N_DEV * width))
        off += s.size
    return full


def kernel(x, norm1_g, w_in, conv_a_w, conv_b_w, conv_b_b, ln_b_g, ln_b_b, ln_c_g, ln_c_b, sgu_w, sgu_b, w_out, norm2_g, w_up, conv_f_w, w_down, final_g, loss_target, m_norm1_g, m_w_in, m_conv_a_w, m_conv_b_w, m_conv_b_b, m_ln_b_g, m_ln_b_b, m_ln_c_g, m_ln_c_b, m_sgu_w, m_sgu_b, m_w_out, m_norm2_g, m_w_up, m_conv_f_w, m_w_down, m_final_g, v_norm1_g, v_w_in, v_conv_a_w, v_conv_b_w, v_conv_b_b, v_ln_b_g, v_ln_b_b, v_ln_c_g, v_ln_c_b, v_sgu_w, v_sgu_b, v_w_out, v_norm2_g, v_w_up, v_conv_f_w, v_w_down, v_final_g):
    weights = dict(norm1_g=norm1_g, w_in=w_in, conv_a_w=conv_a_w, conv_b_w=conv_b_w, conv_b_b=conv_b_b, ln_b_g=ln_b_g,
                   ln_b_b=ln_b_b, ln_c_g=ln_c_g, ln_c_b=ln_c_b, sgu_w=sgu_w, sgu_b=sgu_b, w_out=w_out, norm2_g=norm2_g,
                   w_up=w_up, conv_f_w=conv_f_w, w_down=w_down, final_g=final_g)
    mom1 = dict(norm1_g=m_norm1_g, w_in=m_w_in, conv_a_w=m_conv_a_w, conv_b_w=m_conv_b_w, conv_b_b=m_conv_b_b,
                ln_b_g=m_ln_b_g, ln_b_b=m_ln_b_b, ln_c_g=m_ln_c_g, ln_c_b=m_ln_c_b, sgu_w=m_sgu_w, sgu_b=m_sgu_b,
                w_out=m_w_out, norm2_g=m_norm2_g, w_up=m_w_up, conv_f_w=m_conv_f_w, w_down=m_w_down, final_g=m_final_g)
    mom2 = dict(norm1_g=v_norm1_g, w_in=v_w_in, conv_a_w=v_conv_a_w, conv_b_w=v_conv_b_w, conv_b_b=v_conv_b_b,
                ln_b_g=v_ln_b_g, ln_b_b=v_ln_b_b, ln_c_g=v_ln_c_g, ln_c_b=v_ln_c_b, sgu_w=v_sgu_w, sgu_b=v_sgu_b,
                w_out=v_w_out, norm2_g=v_norm2_g, w_up=v_w_up, conv_f_w=v_conv_f_w, w_down=v_w_down, final_g=v_final_g)
    n_ex, seq, d = x.shape
    depth = w_in.shape[0]
    assert depth == 2
    my = _flat(*_position())
    row = lambda a, l: a[l][None]
    tied = lambda a, token: a + token[0:1, 0:1]

    slab = {"w_in": [_bf(jnp.swapaxes(w_in, 1, 2)[l]) for l in range(depth)], "w_out": [_bf(w_out[l]) for l in range(depth)],
            "w_up": [_bf(jnp.swapaxes(w_up, 1, 2)[l]) for l in range(depth)], "w_down": [_bf(w_down[l]) for l in range(depth)]}
    rows = {name: parts[0].shape[0] for name, parts in slab.items()}
    key_of = {"w_in": "w_in_t", "w_out": "w_out", "w_up": "w_up_t", "w_down": "w_down"}
    rest_layer0 = [("w_out", 0), ("w_up", 0), ("w_down", 0)]
    all_layer1 = [("w_in", 1), ("w_out", 1), ("w_up", 1), ("w_down", 1)]

    w_in0 = _all_gather(slab["w_in"][0], "gather_w_in0", norm1_g)
    conv_a_full, conv_b_full, conv_f_full = _gather_conv_weights(conv_a_w, conv_b_w, conv_f_w, w_in0)
    gather0, token = _exchange_start([slab[n][l] for n, l in rest_layer0], True, conv_f_full, "gather_layer0_start")
    w = {
        "norm1_g": [row(norm1_g, l) for l in range(depth)], "w_in_t": [None] * depth,
        "conv_a_w": [conv_a_full[l] for l in range(depth)], "conv_b_w": [conv_b_full[l] for l in range(depth)],
        "conv_b_b": [row(conv_b_b, l) for l in range(depth)], "ln_b_g": [row(ln_b_g, l) for l in range(depth)],
        "ln_b_b": [row(ln_b_b, l) for l in range(depth)], "ln_c_g": [row(ln_c_g, l) for l in range(depth)],
        "ln_c_b": [row(ln_c_b, l) for l in range(depth)], "sgu_w": [sgu_w[l] for l in range(depth)],
        "sgu_b_full": [jnp.repeat(sgu_b[l].T, HEAD_DIM, axis=1) for l in range(depth)],
        "w_out": [None] * depth, "norm2_g": [row(norm2_g, l) for l in range(depth)], "w_up_t": [None] * depth,
        "conv_f_w": [conv_f_full[l] for l in range(depth)], "w_down": [None] * depth, "final_g": final_g[None],
    }
    w["w_in_t"][0] = w_in0.reshape(N_DEV * rows["w_in"], d)
    w["norm1_g"][0] = tied(row(norm1_g, 0), token)

    def land_weights(handle, after, which, name):
        owns, landed = _exchange_wait(handle, after, name)
        for (n, l), own, got in zip(which, owns, landed):
            w[key_of[n]][l] = _with_own(got, own).reshape(N_DEV * rows[n], d)
        return landed[0]

    st0 = _fwd_mix(x.reshape(n_ex * seq, d), w, 0, n_ex)
    landed0 = land_weights(gather0, st0["mix"][3], rest_layer0, "gather_layer0_wait")
    gather1, token = _exchange_start([slab[n][l] for n, l in all_layer1], True, landed0, "gather_layer1_start")
    w["norm2_g"][0] = tied(row(norm2_g, 0), token)
    x_mid = _fwd_ffn(st0, w, 0, n_ex)
    land_weights(gather1, x_mid, all_layer1, "gather_layer1_wait")
    st1 = _fwd_mix(x_mid, w, 1, n_ex)
    x_out = _fwd_ffn(st1, w, 1, n_ex)
    dx, dxb, d_final_g, loss = _final_loss(x_out, w["final_g"], loss_target.reshape(n_ex * seq, d), "final_loss")
    loss = lax.psum(loss[0, 0], ("x", "y", "c"))

    def send_grads(g, which, after, name):
        return _exchange_start([g[key_of[n]].reshape(N_DEV, rows[n], d) for n, _ in which], False, after, name)

    dx, dxb, g_ffn1 = _bwd_ffn(st1, w, 1, dx, dxb, n_ex)
    dx, dxb, g_mix1 = _bwd_mix(st1, w, 1, dx, dxb, n_ex)
    grads1, token = send_grads({**g_ffn1, **g_mix1}, all_layer1, dx, "exchange_layer1_start")
    w["norm2_g"][0] = tied(row(norm2_g, 0), token)
    dx, dxb, g_ffn0 = _bwd_ffn(st0, w, 0, dx, dxb, n_ex)
    g_ffn0["w_out"] = _mm_tn_parts(st0["mix"], dxb, "out_proj_dw")
    ffn_layer0 = [("w_out", 0), ("w_up", 0), ("w_down", 0)]
    grads0a, token = send_grads(g_ffn0, ffn_layer0, dxb, "exchange_ffn0_start")
    dmix = _mm_nt(dxb, w["w_out"][0], "out_proj_dx", after=token)
    dx, dxb, g_mix0, dp0 = _bwd_mixers(st0, w, 0, dx, dmix, n_ex)
    grad_x = dx.reshape(n_ex, seq, d)
    g = {k: [{**g_ffn0, **g_mix0}[k], {**g_ffn1, **g_mix1}[k]] for k in g_mix0.keys() | g_ffn0.keys()}
    g["final_g"] = d_final_g

    small_local = {
        "norm1_g": jnp.stack([a[0] for a in g["norm1_g"]]), "conv_a_w": jnp.stack(g["conv_a_w"]),
        "conv_b_w": jnp.stack(g["conv_b_w"]), "conv_b_b": jnp.stack([a[0] for a in g["conv_b_b"]]),
        "ln_b_g": jnp.stack([a[0] for a in g["ln_b_g"]]), "ln_b_b": jnp.stack([a[0] for a in g["ln_b_b"]]),
        "ln_c_g": jnp.stack([a[0] for a in g["ln_c_g"]]), "ln_c_b": jnp.stack([a[0] for a in g["ln_c_b"]]),
        "sgu_w": jnp.stack(g["sgu_w"]), "sgu_b": jnp.stack([a.T for a in g["sgu_b_t"]]),
        "norm2_g": jnp.stack([a[0] for a in g["norm2_g"]]), "conv_f_w": jnp.stack(g["conv_f_w"]),
        "final_g": g["final_g"][0],
    }
    small, token = _exchange_start([_pack_rows([small_local[k] for k in _SMALL])], True, dx, "gather_small_start")
    g_mix0["w_in_t"] = _mm_tn_parts(dp0, st0["h1"], "in_proj_dw", after=token)
    mix_layer0 = [("w_in", 0)]
    grads0b, token = send_grads(g_mix0, mix_layer0, dx, "exchange_mix0_start")

    reduced = {}

    def land_grads(handle, after, which, name):
        sent, landed = _exchange_wait(handle, after, name + "_wait")
        for (n, l), src, got in zip(which, sent, landed):
            own = lax.dynamic_index_in_dim(src, my, 0, keepdims=False)
            reduced[(n, l)] = _sum_slabs(_with_own(got, own), name + "_sum_" + n)
        return reduced[which[-1]]

    def stacked_grad(name):
        stacked = jnp.stack([reduced[(name, l)] for l in range(depth)])
        return jnp.swapaxes(stacked, 1, 2) if name in ("w_in", "w_up") else stacked

    done = land_grads(grads1, token, all_layer1, "exchange_layer1")
    land_grads(grads0a, done, ffn_layer0, "exchange_ffn0")
    grads = {name: stacked_grad(name) for name in ("w_out", "w_up", "w_down")}

    delta, new_m, new_v = {}, {}, {}

    def as_2d(name):
        shp = weights[name].shape
        two_d = (-1, shp[-1]) if len(shp) > 1 else (1, shp[0])
        return tuple(a.reshape(two_d) for a in (weights[name], grads[name], mom1[name], mom2[name]))

    def keep(name, outs):
        delta[name], new_m[name], new_v[name] = (o.reshape(weights[name].shape) for o in outs)

    for name in ("w_up", "w_down", "w_out"):
        keep(name, _adamw(*as_2d(name), "adamw_" + name))

    (own,), (landed,) = _exchange_wait(small, new_v["w_out"], "gather_small_wait")
    small_sum = _sum_slabs(_with_own(landed, own), "sum_small_grads")
    for name, total in zip(_SMALL, _unpack_rows(small_sum, [small_local[k].shape for k in _SMALL])):
        if name in _CONV_SHARDED:
            width = weights[name].shape[-1]
            total = lax.dynamic_slice_in_dim(total, my * width, width, axis=-1)
        grads[name] = total
    for name, outs in zip(_SMALL, _adamw_small([as_2d(name) for name in _SMALL], "adamw_small")):
        keep(name, outs)

    land_grads(grads0b, new_v["final_g"], mix_layer0, "exchange_mix0")
    grads["w_in"] = stacked_grad("w_in")
    keep("w_in", _adamw(*as_2d("w_in"), "adamw_w_in"))

    return (loss, grad_x, *[grads[n] for n in _NAMES], *[delta[n] for n in _NAMES], *[new_m[n] for n in _NAMES],
            *[new_v[n] for n in _NAMES])
```

```python
import jax
import jax.numpy as jnp
from jax import lax
from jax.experimental import pallas as pl
from jax.experimental.pallas import tpu as pltpu

_F32 = jnp.float32
_BF = jnp.bfloat16

HEAD_DIM = 64
MIX_W = 256
N_HEADS = MIX_W // HEAD_DIM
CHUNK = 128
KV_BLOCK = 128
K_SHORT = 3
K_CONF = 31
K_FFN = 3
RMS_EPS = 1e-6
LN_EPS = 1e-5
ADAM_LR = 0.001
ADAM_B1 = 0.9
ADAM_B2 = 0.999
ADAM_EPS = 1e-08
ADAM_WD = 0.01
ADAM_STEP = 10
N_DEV = 8
VMEM_LIMIT = 56 * 1024 * 1024


def _bf(x):
    return x.astype(_BF)


def _ld(ref):
    return ref[...].astype(_F32)


_ANY_SPEC = pl.BlockSpec(memory_space=pl.ANY)


def _params(*sem):
    return pltpu.CompilerParams(dimension_semantics=sem, vmem_limit_bytes=VMEM_LIMIT)


def _dot(a, b):
    return jnp.dot(a, b, preferred_element_type=_F32)


def _dot_nt(a, b):
    return lax.dot_general(a, b, (((1,), (1,)), ((), ())), preferred_element_type=_F32)


def _dot_tn(a, b):
    return lax.dot_general(a, b, (((0,), (0,)), ((), ())), preferred_element_type=_F32)


def _row_tile(t, want):
    return want if t % want == 0 else t


def _pick_tile(rows, unit, max_rows):
    best = 0
    for cand in range(unit, min(rows, max_rows) + 1, unit):
        if rows % cand == 0:
            best = cand
    return best or rows


def _sigmoid(x):
    return 1.0 / (1.0 + jnp.exp(-x))


def _rms_rstd(x):
    return lax.rsqrt(jnp.mean(x * x, axis=-1, keepdims=True) + RMS_EPS)


def _norm_mm(x, g, w_t, name):
    t, d = x.shape
    n = w_t.shape[0]
    tm = _row_tile(t, 512)
    tn = _row_tile(n, 512)

    def body(x_ref, g_ref, w_ref, p_ref, h_ref):
        xv = x_ref[...]
        h = _bf(xv * _rms_rstd(xv) * g_ref[...])
        h_ref[...] = h
        for n0 in range(0, n, tn):
            p_ref[:, n0:n0 + tn] = _bf(_dot_nt(h, w_ref[n0:n0 + tn, :]))

    return pl.pallas_call(
        body, name=name, grid=(t // tm,),
        in_specs=[pl.BlockSpec((tm, d), lambda i: (i, 0)), pl.BlockSpec((1, d), lambda i: (0, 0)),
                  pl.BlockSpec((n, d), lambda i: (0, 0))],
        out_specs=[pl.BlockSpec((tm, n), lambda i: (i, 0)), pl.BlockSpec((tm, d), lambda i: (i, 0))],
        out_shape=[jax.ShapeDtypeStruct((t, n), _BF), jax.ShapeDtypeStruct((t, d), _BF)],
        compiler_params=_params("parallel"),
    )(x, g, w_t)


def _mm_nt(a, w_t, name, after=None):
    t, k = a.shape
    n = w_t.shape[0]
    tm = _row_tile(t, 512)
    tn = _row_tile(n, 512) if n % 512 == 0 else _row_tile(n, 256)

    def body(a_ref, w_ref, *rest):
        o_ref = rest[-1]
        av = a_ref[...]
        for n0 in range(0, n, tn):
            o_ref[:, n0:n0 + tn] = _bf(_dot_nt(av, w_ref[n0:n0 + tn, :]))

    extra = () if after is None else (after,)
    return pl.pallas_call(
        body, name=name, grid=(t // tm,),
        in_specs=[pl.BlockSpec((tm, k), lambda i: (i, 0)), pl.BlockSpec((n, k), lambda i: (0, 0))] + [_ANY_SPEC] * len(extra),
        out_specs=pl.BlockSpec((tm, n), lambda i: (i, 0)),
        out_shape=jax.ShapeDtypeStruct((t, n), _BF),
        compiler_params=_params("parallel"),
    )(a, w_t, *extra)


def _mm_res(parts, w, x, name):
    t = x.shape[0]
    k, d = w.shape
    tm = _row_tile(t, 512)
    widths = [a.shape[1] for a in parts]
    n_parts = len(parts)

    def body(*refs):
        w_ref, x_ref, o_ref = refs[n_parts:]
        acc, off = x_ref[...], 0
        for a_ref, width in zip(refs[:n_parts], widths):
            acc = acc + _dot(a_ref[...], w_ref[off:off + width, :])
            off += width
        o_ref[...] = acc

    return pl.pallas_call(
        body, name=name, grid=(t // tm,),
        in_specs=[pl.BlockSpec((tm, width), lambda i: (i, 0)) for width in widths] + [
            pl.BlockSpec((k, d), lambda i: (0, 0)), pl.BlockSpec((tm, d), lambda i: (i, 0))],
        out_specs=pl.BlockSpec((tm, d), lambda i: (i, 0)),
        out_shape=jax.ShapeDtypeStruct((t, d), _F32),
        compiler_params=_params("parallel"),
    )(*parts, w, x)


def _mm_normbwd(parts, w, x, g, dres, name):
    t = x.shape[0]
    k, d = w.shape
    tm = _row_tile(t, 512)
    widths = [a.shape[1] for a in parts]
    n_parts = len(parts)

    def body(*refs):
        a_refs = refs[:n_parts]
        w_ref, x_ref, g_ref, r_ref, dx_ref, dxb_ref, dg_ref = refs[n_parts:]
        dh, off = None, 0
        for a_ref, width in zip(a_refs, widths):
            term = _dot(_bf(a_ref[...]), w_ref[off:off + width, :])
            dh = term if dh is None else dh + term
            off += width
        xv = x_ref[...]
        rstd = _rms_rstd(xv)
        xn = xv * rstd
        u = dh * g_ref[...]
        dx = r_ref[...] + rstd * (u - xn * jnp.mean(u * xn, axis=-1, keepdims=True))
        dx_ref[...] = dx
        dxb_ref[...] = _bf(dx)

        @pl.when(pl.program_id(0) == 0)
        def _():
            dg_ref[...] = jnp.zeros_like(dg_ref)

        dg_ref[...] += jnp.sum(dh * xn, axis=0, keepdims=True)

    return pl.pallas_call(
        body, name=name, grid=(t // tm,),
        in_specs=[pl.BlockSpec((tm, width), lambda i: (i, 0)) for width in widths] + [
            pl.BlockSpec((k, d), lambda i: (0, 0)),
            pl.BlockSpec((tm, d), lambda i: (i, 0)), pl.BlockSpec((1, d), lambda i: (0, 0)),
            pl.BlockSpec((tm, d), lambda i: (i, 0))],
        out_specs=[pl.BlockSpec((tm, d), lambda i: (i, 0)), pl.BlockSpec((tm, d), lambda i: (i, 0)),
                   pl.BlockSpec((1, d), lambda i: (0, 0))],
        out_shape=[jax.ShapeDtypeStruct((t, d), _F32), jax.ShapeDtypeStruct((t, d), _BF),
                   jax.ShapeDtypeStruct((1, d), _F32)],
        compiler_params=_params("arbitrary"),
    )(*parts, w, x, g, dres)


def _mm_tn(a, b, name, out_dtype):
    t, m = a.shape
    n = b.shape[1]
    tm = _pick_tile(m, 128, 1408)
    tn = _pick_tile(n, 128, 1024)
    tk = _row_tile(t, 1024)
    nk = t // tk

    def body(a_ref, b_ref, o_ref, acc):
        kk = pl.program_id(2)

        @pl.when(kk == 0)
        def _():
            acc[...] = jnp.zeros_like(acc)

        acc[...] += _dot_tn(_bf(a_ref[...]), b_ref[...])

        @pl.when(kk == nk - 1)
        def _():
            o_ref[...] = acc[...].astype(o_ref.dtype)

    return pl.pallas_call(
        body, name=name, grid=(m // tm, n // tn, nk),
        in_specs=[pl.BlockSpec((tk, tm), lambda i, j, kk: (kk, i)), pl.BlockSpec((tk, tn), lambda i, j, kk: (kk, j))],
        out_specs=pl.BlockSpec((tm, tn), lambda i, j, kk: (i, j)),
        out_shape=jax.ShapeDtypeStruct((m, n), out_dtype),
        scratch_shapes=[pltpu.VMEM((tm, tn), _F32)],
        compiler_params=_params("parallel", "parallel", "arbitrary"),
    )(a, b)


def _mm_tn_halves(a0, a1, b, name):
    t, m = a0.shape
    n = b.shape[1]
    tm = _pick_tile(m, 128, 1408)
    tn = _pick_tile(n, 128, 1024)
    tk = _row_tile(t, 1024)
    nk = t // tk
    half = m // tm

    def body(a0_ref, a1_ref, b_ref, o_ref, acc):
        i = pl.program_id(0)
        kk = pl.program_id(2)

        @pl.when(kk == 0)
        def _():
            acc[...] = jnp.zeros_like(acc)

        @pl.when(i < half)
        def _():
            acc[...] += _dot_tn(a0_ref[...], b_ref[...])

        @pl.when(i >= half)
        def _():
            acc[...] += _dot_tn(a1_ref[...], b_ref[...])

        @pl.when(kk == nk - 1)
        def _():
            o_ref[...] = _bf(acc[...])

    return pl.pallas_call(
        body, name=name, grid=(2 * half, n // tn, nk),
        in_specs=[pl.BlockSpec((tk, tm), lambda i, j, kk: (jnp.where(i < half, kk, 0), jnp.minimum(i, half - 1))),
                  pl.BlockSpec((tk, tm), lambda i, j, kk: (jnp.where(i >= half, kk, 0), jnp.maximum(i - half, 0))),
                  pl.BlockSpec((tk, tn), lambda i, j, kk: (kk, j))],
        out_specs=pl.BlockSpec((tm, tn), lambda i, j, kk: (i, j)),
        out_shape=jax.ShapeDtypeStruct((2 * m, n), _BF),
        scratch_shapes=[pltpu.VMEM((tm, tn), _F32)],
        compiler_params=_params("parallel", "parallel", "arbitrary"),
    )(a0, a1, b)


def _mm_tn_parts(parts, b, name, after=None):
    t, n = b.shape
    widths = [a.shape[1] for a in parts]
    m = sum(widths)
    n_parts = len(parts)
    tk = _row_tile(t, 1024)
    nk = t // tk
    extra = () if after is None else (after,)

    def body(*refs):
        b_ref = refs[n_parts]
        o_ref, acc = refs[-2:]
        kk = pl.program_id(0)

        @pl.when(kk == 0)
        def _():
            acc[...] = jnp.zeros_like(acc)

        bv = b_ref[...]
        off = 0
        for a_ref, width in zip(refs[:n_parts], widths):
            acc[off:off + width, :] += _dot_tn(_bf(a_ref[...]), bv)
            off += width

        @pl.when(kk == nk - 1)
        def _():
            o_ref[...] = _bf(acc[...])

    return pl.pallas_call(
        body, name=name, grid=(nk,),
        in_specs=[pl.BlockSpec((tk, width), lambda kk: (kk, 0)) for width in widths] + [pl.BlockSpec((tk, n), lambda kk: (kk, 0))]
        + [_ANY_SPEC] * len(extra),
        out_specs=pl.BlockSpec((m, n), lambda kk: (0, 0)),
        out_shape=jax.ShapeDtypeStruct((m, n), _BF),
        scratch_shapes=[pltpu.VMEM((m, n), _F32)],
        compiler_params=_params("arbitrary"),
    )(*parts, b, *extra)


def _final_loss(x, g, target, name):
    t, d = x.shape
    tm = _row_tile(t, 256)

    def body(x_ref, g_ref, t_ref, dx_ref, dxb_ref, dg_ref, loss_ref):
        xv = x_ref[...]
        rstd = _rms_rstd(xv)
        xn = xv * rstd
        err = xn * g_ref[...] - t_ref[...]
        dy = err * (1.0 / d)
        u = dy * g_ref[...]
        dx = rstd * (u - xn * jnp.mean(u * xn, axis=-1, keepdims=True))
        dx_ref[...] = dx
        dxb_ref[...] = _bf(dx)

        @pl.when(pl.program_id(0) == 0)
        def _():
            dg_ref[...] = jnp.zeros_like(dg_ref)
            loss_ref[...] = jnp.zeros_like(loss_ref)

        dg_ref[...] += jnp.sum(dy * xn, axis=0, keepdims=True)
        loss_ref[...] += (0.5 / d) * jnp.sum(jnp.sum(err * err, axis=1, keepdims=True), axis=0, keepdims=True)

    return pl.pallas_call(
        body, name=name, grid=(t // tm,),
        in_specs=[pl.BlockSpec((tm, d), lambda i: (i, 0)), pl.BlockSpec((1, d), lambda i: (0, 0)),
                  pl.BlockSpec((tm, d), lambda i: (i, 0))],
        out_specs=[pl.BlockSpec((tm, d), lambda i: (i, 0)), pl.BlockSpec((tm, d), lambda i: (i, 0)),
                   pl.BlockSpec((1, d), lambda i: (0, 0)), pl.BlockSpec((1, 1), lambda i: (0, 0))],
        out_shape=[jax.ShapeDtypeStruct((t, d), _F32), jax.ShapeDtypeStruct((t, d), _BF),
                   jax.ShapeDtypeStruct((1, d), _F32), jax.ShapeDtypeStruct((1, 1), _F32)],
        compiler_params=_params("arbitrary"),
    )(x, g, target)


def _pad_rows(x, pad):
    return jnp.concatenate([x, jnp.zeros((pad, x.shape[1]), x.dtype)], axis=0)


def _shift_down(xp, s):
    return xp if s == 0 else pltpu.roll(xp, s, 0)


def _shift_up(xp, s):
    return xp if s == 0 else pltpu.roll(xp, xp.shape[0] - s, 0)


def _taps3(xp):
    one = _shift_down(xp, 1)
    return xp, one, _shift_down(one, 1)


def _conv3_taps(taps, w_ref):
    return w_ref[2:3, :] * taps[0] + w_ref[1:2, :] * taps[1] + w_ref[0:1, :] * taps[2]


def _conv3(xp, w_ref):
    return _conv3_taps(_taps3(xp), w_ref)


def _conv3_t(dyp, w_ref):
    one = _shift_up(dyp, 1)
    return w_ref[2:3, :] * dyp + w_ref[1:2, :] * one + w_ref[0:1, :] * _shift_up(one, 1)


def _conv3_dw(dyp, taps):
    return [jnp.sum(dyp * taps[2 - k], axis=0, keepdims=True) for k in range(3)]


def _ffn_mid_fwd(up_pre, wf, n_ex, name):
    t, f2 = up_pre.shape
    f = f2 // 2
    s = t // n_ex
    cb = MIX_W
    nb = f // cb

    def body(ug_ref, uv_ref, wg_ref, wv_ref, act_ref, gf_ref, vf_ref):
        gf = _conv3(_pad_rows(ug_ref[...].astype(_F32), 8), wg_ref)[:s]
        vf = _conv3(_pad_rows(uv_ref[...].astype(_F32), 8), wv_ref)[:s]
        act_ref[...] = _bf(gf * _sigmoid(gf) * vf)
        gf_ref[...] = _bf(gf)
        vf_ref[...] = _bf(vf)

    out = pl.BlockSpec((s, cb), lambda e, j: (e, j))
    return pl.pallas_call(
        body, name=name, grid=(n_ex, nb),
        in_specs=[pl.BlockSpec((s, cb), lambda e, j: (e, j)), pl.BlockSpec((s, cb), lambda e, j: (e, j + nb)),
                  pl.BlockSpec((K_FFN, cb), lambda e, j: (0, j)), pl.BlockSpec((K_FFN, cb), lambda e, j: (0, j + nb))],
        out_specs=[out, out, out],
        out_shape=[jax.ShapeDtypeStruct((t, f), _BF)] * 3,
        compiler_params=_params("parallel", "parallel"),
    )(up_pre, up_pre, wf, wf)


def _ffn_mid_bwd(up_pre, conv_g, conv_v, wf, dact, n_ex, name):
    t, f2 = up_pre.shape
    f = f2 // 2
    s = t // n_ex
    cb = MIX_W
    nb = f // cb

    def body(ug_ref, uv_ref, gf_ref, vf_ref, wg_ref, wv_ref, da_ref, dug_ref, duv_ref, dwg_ref, dwv_ref):
        gf = _ld(gf_ref)
        vf = _ld(vf_ref)
        sg = _sigmoid(gf)
        da = _ld(da_ref)

        @pl.when(pl.program_id(1) == 0)
        def _():
            dwg_ref[...] = jnp.zeros_like(dwg_ref)
            dwv_ref[...] = jnp.zeros_like(dwv_ref)

        def finish(dpost, w_ref, x_ref, du_ref, dw_ref):
            ahead = [_pad_rows(dpost, 8)]
            ahead.append(_shift_up(ahead[0], 1))
            ahead.append(_shift_up(ahead[1], 1))
            du_ref[...] = _bf((w_ref[2:3, :] * ahead[0] + w_ref[1:2, :] * ahead[1] + w_ref[0:1, :] * ahead[2])[:s])
            x = _ld(x_ref)
            for k in range(K_FFN):
                dw_ref[k:k + 1, :] += jnp.sum(ahead[2 - k][:s] * x, axis=0, keepdims=True)

        finish(da * vf * sg * (1.0 + gf * (1.0 - sg)), wg_ref, ug_ref, dug_ref, dwg_ref)
        finish(da * gf * sg, wv_ref, uv_ref, duv_ref, dwv_ref)

    return pl.pallas_call(
        body, name=name, grid=(nb, n_ex),
        in_specs=[pl.BlockSpec((s, cb), lambda j, e: (e, j)), pl.BlockSpec((s, cb), lambda j, e: (e, j + nb)),
                  pl.BlockSpec((s, cb), lambda j, e: (e, j)), pl.BlockSpec((s, cb), lambda j, e: (e, j)),
                  pl.BlockSpec((K_FFN, cb), lambda j, e: (0, j)), pl.BlockSpec((K_FFN, cb), lambda j, e: (0, j + nb)),
                  pl.BlockSpec((s, cb), lambda j, e: (e, j))],
        out_specs=[pl.BlockSpec((s, cb), lambda j, e: (e, j)), pl.BlockSpec((s, cb), lambda j, e: (e, j)),
                   pl.BlockSpec((K_FFN, cb), lambda j, e: (0, j)), pl.BlockSpec((K_FFN, cb), lambda j, e: (0, j))],
        out_shape=[jax.ShapeDtypeStruct((t, f), _BF), jax.ShapeDtypeStruct((t, f), _BF),
                   jax.ShapeDtypeStruct((K_FFN, f), _F32), jax.ShapeDtypeStruct((K_FFN, f), _F32)],
        compiler_params=_params("parallel", "arbitrary"),
    )(up_pre, up_pre, conv_g, conv_v, wf, wf, dact)


def _pcol(s, j):
    return pl.BlockSpec((s, MIX_W), lambda e, j=j: (e, j))


def _vec(rows=1):
    return pl.BlockSpec((rows, MIX_W), lambda e: (0, 0))


def _mix_a_fwd(p, wa, n_ex, name):
    t = p.shape[0]
    s = t // n_ex

    def body(gb_ref, gc_ref, ha_ref, w_ref, y_ref):
        cv = _conv3(_pad_rows(_ld(gc_ref) * _ld(ha_ref), 8), w_ref)[:s]
        y_ref[...] = _bf(_ld(gb_ref) * cv)

    return pl.pallas_call(
        body, name=name, grid=(n_ex,),
        in_specs=[_pcol(s, 0), _pcol(s, 1), _pcol(s, 2), _vec(K_SHORT)],
        out_specs=pl.BlockSpec((s, MIX_W), lambda e: (e, 0)),
        out_shape=jax.ShapeDtypeStruct((t, MIX_W), _BF),
        compiler_params=_params("parallel"),
    )(p, p, p, wa)


def _mix_a_bwd(p, wa, dmix, n_ex, name):
    t = p.shape[0]
    s = t // n_ex

    def body(gb_ref, gc_ref, ha_ref, w_ref, dy_ref, dp_ref, dw_ref):
        gc = _ld(gc_ref)
        ha = _ld(ha_ref)
        up = _taps3(_pad_rows(gc * ha, 8))
        cv = _conv3_taps(up, w_ref)[:s]
        dy = _ld(dy_ref)
        dcvp = _pad_rows(dy * _ld(gb_ref), 8)
        du = _conv3_t(dcvp, w_ref)[:s]
        dp_ref[:, 0:MIX_W] = _bf(dy * cv)
        dp_ref[:, MIX_W:2 * MIX_W] = _bf(du * ha)
        dp_ref[:, 2 * MIX_W:3 * MIX_W] = _bf(du * gc)

        @pl.when(pl.program_id(0) == 0)
        def _():
            dw_ref[...] = jnp.zeros_like(dw_ref)

        rows = _conv3_dw(dcvp, up)
        for k in range(3):
            dw_ref[k:k + 1, :] += rows[k]

    return pl.pallas_call(
        body, name=name, grid=(n_ex,),
        in_specs=[_pcol(s, 0), _pcol(s, 1), _pcol(s, 2), _vec(K_SHORT), _pcol(s, 0)],
        out_specs=[pl.BlockSpec((s, 3 * MIX_W), lambda e: (e, 0)), _vec(K_SHORT)],
        out_shape=[jax.ShapeDtypeStruct((t, 3 * MIX_W), _BF), jax.ShapeDtypeStruct((K_SHORT, MIX_W), _F32)],
        compiler_params=_params("arbitrary"),
    )(p, p, p, wa, dmix)


CONF_PAD = 32
CONF_ROWS = 64
_CONF_LANES = (slice(0, 128), slice(128, 256))


def _conf_taps(win, ahead):
    n = CONF_ROWS + CONF_PAD
    for b in range(8):
        rot = win if b == 0 else pltpu.roll(win, (n - b) if ahead else b, 0)
        for a in range(4):
            if 8 * a + b < K_CONF:
                yield rot, 8 * a + b, (8 * a) if ahead else (CONF_PAD - 8 * a)


def _ln_fwd(x, g, b):
    mu = jnp.mean(x, axis=-1, keepdims=True)
    xc = x - mu
    rstd = lax.rsqrt(jnp.mean(xc * xc, axis=-1, keepdims=True) + LN_EPS)
    xhat = xc * rstd
    return xhat * g + b, xhat, rstd


def _ln_bwd(dy, xhat, rstd, g):
    dxh = dy * g
    return rstd * (dxh - jnp.mean(dxh, axis=-1, keepdims=True) - xhat * jnp.mean(dxh * xhat, axis=-1, keepdims=True))


def _mix_b_fwd(p, wb, bb, lg, lb, n_ex, name):
    t = p.shape[0]
    s = t // n_ex

    def body(val_ref, gat_ref, w_ref, bb_ref, lg_ref, lb_ref, y_ref, cb_ref, xpad):
        xpad[0:CONF_PAD, :] = jnp.zeros((CONF_PAD, MIX_W), _F32)
        xpad[CONF_PAD:, :] = _ld(val_ref) * _sigmoid(_ld(gat_ref))

        def chunk(c, carry):
            r0 = pl.multiple_of(c * CONF_ROWS, CONF_ROWS)
            for lanes in _CONF_LANES:
                acc = None
                for rot, sh, lo in _conf_taps(xpad[pl.ds(r0, CONF_ROWS + CONF_PAD), lanes], False):
                    term = w_ref[K_CONF - 1 - sh:K_CONF - sh, lanes] * rot[lo:lo + CONF_ROWS]
                    acc = term if acc is None else acc + term
                cb_ref[pl.ds(r0, CONF_ROWS), lanes] = acc + bb_ref[:, lanes]
            return carry

        lax.fori_loop(0, s // CONF_ROWS, chunk, 0)
        yl, _, _ = _ln_fwd(cb_ref[...], lg_ref[...], lb_ref[...])
        y_ref[...] = _bf(yl * _sigmoid(yl))

    return pl.pallas_call(
        body, name=name, grid=(n_ex,),
        in_specs=[_pcol(s, 3), _pcol(s, 4), _vec(K_CONF), _vec(), _vec(), _vec()],
        out_specs=[pl.BlockSpec((s, MIX_W), lambda e: (e, 0)), pl.BlockSpec((s, MIX_W), lambda e: (e, 0))],
        out_shape=[jax.ShapeDtypeStruct((t, MIX_W), _BF), jax.ShapeDtypeStruct((t, MIX_W), _F32)],
        scratch_shapes=[pltpu.VMEM((CONF_PAD + s, MIX_W), _F32)],
        compiler_params=_params("parallel"),
    )(p, p, wb, bb, lg, lb)


def _mix_b_bwd(p, cb, wb, lg, lb, dmix, n_ex, name):
    t = p.shape[0]
    s = t // n_ex

    def body(val_ref, gat_ref, cb_ref, w_ref, lg_ref, lb_ref, dy_ref, dp_ref, dw_ref, dbb_ref, dlg_ref, dlb_ref,
             xpad, dpad, dglu_s, dw_acc):
        @pl.when(pl.program_id(0) == 0)
        def _():
            for r in (dw_ref, dbb_ref, dlg_ref, dlb_ref):
                r[...] = jnp.zeros_like(r)

        yl, xhat, rstd = _ln_fwd(cb_ref[...], lg_ref[...], lb_ref[...])
        sy = _sigmoid(yl)
        dyl = _ld(dy_ref) * sy * (1.0 + yl * (1.0 - sy))
        dlg_ref[...] += jnp.sum(dyl * xhat, axis=0, keepdims=True)
        dlb_ref[...] += jnp.sum(dyl, axis=0, keepdims=True)
        dcb = _ln_bwd(dyl, xhat, rstd, lg_ref[...])
        dbb_ref[...] += jnp.sum(dcb, axis=0, keepdims=True)

        val = _ld(val_ref)
        sg = _sigmoid(_ld(gat_ref))
        xpad[0:CONF_PAD, :] = jnp.zeros((CONF_PAD, MIX_W), _F32)
        xpad[CONF_PAD:, :] = val * sg
        dpad[0:s, :] = dcb
        dpad[s:, :] = jnp.zeros((CONF_PAD, MIX_W), _F32)
        dw_acc[...] = jnp.zeros_like(dw_acc)

        def chunk(c, carry):
            r0 = pl.multiple_of(c * CONF_ROWS, CONF_ROWS)
            for lanes in _CONF_LANES:
                d_win = dpad[pl.ds(r0, CONF_ROWS + CONF_PAD), lanes]
                d_rows = d_win[0:CONF_ROWS]
                acc = None
                for rot, sh, lo in _conf_taps(d_win, True):
                    term = w_ref[K_CONF - 1 - sh:K_CONF - sh, lanes] * rot[lo:lo + CONF_ROWS]
                    acc = term if acc is None else acc + term
                dglu_s[pl.ds(r0, CONF_ROWS), lanes] = acc
                for rot, sh, lo in _conf_taps(xpad[pl.ds(r0, CONF_ROWS + CONF_PAD), lanes], False):
                    prod = d_rows * rot[lo:lo + CONF_ROWS]
                    dw_acc[K_CONF - 1 - sh, :, lanes] += jnp.sum(prod.reshape(CONF_ROWS // 8, 8, 128), axis=0)
            return carry

        lax.fori_loop(0, s // CONF_ROWS, chunk, 0)
        dw_ref[...] += jnp.sum(dw_acc[...], axis=1)
        dglu = dglu_s[...]
        dp_ref[:, 0:MIX_W] = _bf(dglu * sg)
        dp_ref[:, MIX_W:2 * MIX_W] = _bf(dglu * val * sg * (1.0 - sg))

    return pl.pallas_call(
        body, name=name, grid=(n_ex,),
        in_specs=[_pcol(s, 3), _pcol(s, 4), pl.BlockSpec((s, MIX_W), lambda e: (e, 0)), _vec(K_CONF), _vec(), _vec(),
                  _pcol(s, 1)],
        out_specs=[pl.BlockSpec((s, 2 * MIX_W), lambda e: (e, 0)), _vec(K_CONF), _vec(), _vec(), _vec()],
        out_shape=[jax.ShapeDtypeStruct((t, 2 * MIX_W), _BF), jax.ShapeDtypeStruct((K_CONF, MIX_W), _F32),
                   jax.ShapeDtypeStruct((1, MIX_W), _F32), jax.ShapeDtypeStruct((1, MIX_W), _F32),
                   jax.ShapeDtypeStruct((1, MIX_W), _F32)],
        scratch_shapes=[pltpu.VMEM((CONF_PAD + s, MIX_W), _F32), pltpu.VMEM((s + CONF_PAD, MIX_W), _F32),
                        pltpu.VMEM((s, MIX_W), _F32), pltpu.VMEM((K_CONF, 8, MIX_W), _F32)],
        compiler_params=_params("arbitrary"),
    )(p, p, cb, wb, lg, lb, dmix)


_INV_SQRT2 = 0.7071067811865476
_INV_SQRT2PI = 0.3989422804014327


def _gelu(x):
    return 0.5 * x * (1.0 + lax.erf(x * _INV_SQRT2))


def _gelu_grad(x):
    return 0.5 * (1.0 + lax.erf(x * _INV_SQRT2)) + x * _INV_SQRT2PI * jnp.exp(-0.5 * x * x)


def _head_masks(width=MIX_W):
    lane = lax.broadcasted_iota(jnp.int32, (1, width), 1)
    return [(lane >= h * HEAD_DIM) & (lane < (h + 1) * HEAD_DIM) for h in range(N_HEADS)]


def _tril_mask():
    r = lax.broadcasted_iota(jnp.int32, (CHUNK, CHUNK), 0)
    c = lax.broadcasted_iota(jnp.int32, (CHUNK, CHUNK), 1)
    return c <= r


def _sgu_apply(ws_ref, x3, transpose):
    n = x3.shape[0]
    tril = _tril_mask()
    masks = _head_masks()
    xb = _bf(x3)
    out = jnp.zeros(x3.shape, _F32)
    for h in range(N_HEADS):
        w = _bf(jnp.where(tril, ws_ref[h], 0.0))
        wb = jnp.broadcast_to(w[None], (n, CHUNK, CHUNK))
        dims = (((1,), (1,)), ((0,), (0,))) if transpose else (((2,), (1,)), ((0,), (0,)))
        r = lax.dot_general(wb, xb, dims, preferred_element_type=_F32)
        out = out + jnp.where(masks[h][None], r, 0.0)
    return out


def _mix_c_fwd(p, lg, lb, ws, sb_full, n_ex, name):
    t = p.shape[0]
    s = t // n_ex
    nc = s // CHUNK

    def body(pu_ref, pv_ref, lg_ref, lb_ref, ws_ref, sb_ref, y_ref):
        u = _gelu(_ld(pu_ref))
        vl, _, _ = _ln_fwd(_gelu(_ld(pv_ref)), lg_ref[...], lb_ref[...])
        sp = _sgu_apply(ws_ref, vl.reshape(nc, CHUNK, MIX_W), False) + sb_ref[...][None]
        y_ref[...] = _bf(u * sp.reshape(s, MIX_W))

    return pl.pallas_call(
        body, name=name, grid=(n_ex,),
        in_specs=[_pcol(s, 5), _pcol(s, 6), _vec(), _vec(),
                  pl.BlockSpec((N_HEADS, CHUNK, CHUNK), lambda e: (0, 0, 0)), pl.BlockSpec((CHUNK, MIX_W), lambda e: (0, 0))],
        out_specs=pl.BlockSpec((s, MIX_W), lambda e: (e, 0)),
        out_shape=jax.ShapeDtypeStruct((t, MIX_W), _BF),
        compiler_params=_params("parallel"),
    )(p, p, lg, lb, ws, sb_full)


def _mix_c_bwd(p, lg, lb, ws, sb_full, dmix, n_ex, name):
    t = p.shape[0]
    s = t // n_ex
    nc = s // CHUNK

    def body(pu_ref, pv_ref, lg_ref, lb_ref, ws_ref, sb_ref, dy_ref, dp_ref, dlg_ref, dlb_ref, dws_ref, dsb_ref):
        @pl.when(pl.program_id(0) == 0)
        def _():
            for r in (dlg_ref, dlb_ref, dws_ref, dsb_ref):
                r[...] = jnp.zeros_like(r)

        pu = _ld(pu_ref)
        pv = _ld(pv_ref)
        u = _gelu(pu)
        vl, xhat, rstd = _ln_fwd(_gelu(pv), lg_ref[...], lb_ref[...])
        vl3 = vl.reshape(nc, CHUNK, MIX_W)
        sp = _sgu_apply(ws_ref, vl3, False) + sb_ref[...][None]
        dy = _ld(dy_ref)
        dp_ref[:, 0:MIX_W] = _bf(dy * sp.reshape(s, MIX_W) * _gelu_grad(pu))
        dsp3 = (dy * u).reshape(nc, CHUNK, MIX_W)
        dsb_full = jnp.sum(dsp3, axis=0)
        masks = _head_masks()
        tril = _tril_mask()
        dspb = _bf(dsp3)
        vlb = _bf(vl3)
        for h in range(N_HEADS):
            dsb_ref[:, h:h + 1] += jnp.sum(jnp.where(masks[h], dsb_full, 0.0), axis=1, keepdims=True)
            dm = jnp.where(masks[h][None], dspb, jnp.zeros_like(dspb))
            g3 = lax.dot_general(dm, vlb, (((2,), (2,)), ((0,), (0,))), preferred_element_type=_F32)
            dws_ref[h] += jnp.where(tril, jnp.sum(g3, axis=0), 0.0)
        dvl = _sgu_apply(ws_ref, dsp3, True).reshape(s, MIX_W)
        dlg_ref[...] += jnp.sum(dvl * xhat, axis=0, keepdims=True)
        dlb_ref[...] += jnp.sum(dvl, axis=0, keepdims=True)
        dp_ref[:, MIX_W:2 * MIX_W] = _bf(_ln_bwd(dvl, xhat, rstd, lg_ref[...]) * _gelu_grad(pv))

    return pl.pallas_call(
        body, name=name, grid=(n_ex,),
        in_specs=[_pcol(s, 5), _pcol(s, 6), _vec(), _vec(),
                  pl.BlockSpec((N_HEADS, CHUNK, CHUNK), lambda e: (0, 0, 0)), pl.BlockSpec((CHUNK, MIX_W), lambda e: (0, 0)),
                  _pcol(s, 2)],
        out_specs=[pl.BlockSpec((s, 2 * MIX_W), lambda e: (e, 0)), _vec(), _vec(),
                   pl.BlockSpec((N_HEADS, CHUNK, CHUNK), lambda e: (0, 0, 0)), pl.BlockSpec((CHUNK, N_HEADS), lambda e: (0, 0))],
        out_shape=[jax.ShapeDtypeStruct((t, 2 * MIX_W), _BF), jax.ShapeDtypeStruct((1, MIX_W), _F32),
                   jax.ShapeDtypeStruct((1, MIX_W), _F32), jax.ShapeDtypeStruct((N_HEADS, CHUNK, CHUNK), _F32),
                   jax.ShapeDtypeStruct((CHUNK, N_HEADS), _F32)],
        compiler_params=_params("arbitrary"),
    )(p, p, lg, lb, ws, sb_full, dmix)


D_QBLOCK = 256
HEAD_COLS = N_HEADS * KV_BLOCK


def _stack_heads(x3):
    return jnp.stack([_bf(jnp.where(m[None], x3, 0.0)) for m in _head_masks()], axis=1)


def _stack_heads_rows(x):
    return jnp.concatenate([_bf(jnp.where(m, x, 0.0)) for m in _head_masks()], axis=0)


def _cols_to_rows(x):
    return jnp.concatenate([x[:, h * KV_BLOCK:(h + 1) * KV_BLOCK] for h in range(N_HEADS)], axis=0)


def _head_sums(x):
    return [jnp.sum(x[:, h * KV_BLOCK:(h + 1) * KV_BLOCK], axis=1, keepdims=True) for h in range(N_HEADS)]


def _spread(cols):
    tq = cols[0].shape[0]
    return jnp.concatenate([jnp.broadcast_to(c, (tq, KV_BLOCK)) for c in cols], axis=1)


def _pair_dot(x, m2):
    half = 2 * KV_BLOCK
    xb = _bf(x)
    return jnp.concatenate([_dot(xb[:, :half], m2), _dot(xb[:, half:], m2)], axis=1)


def _tri2(lower):
    n = 2 * KV_BLOCK
    r = lax.broadcasted_iota(jnp.int32, (n, n), 0)
    c = lax.broadcasted_iota(jnp.int32, (n, n), 1)
    same = (r >= KV_BLOCK) == (c >= KV_BLOCK)
    return _bf(jnp.where(same & (r > c if lower else r < c), 1.0, 0.0))


def _sb_scores(qs, kc, j, t_idx, on_diagonal):
    z = _dot_nt(qs, kc)
    lb = jnp.minimum(z, 0.0) - jnp.log(1.0 + jnp.exp(-jnp.abs(z)))
    if not on_diagonal:
        return (lambda x: x), lb, lb - z
    lane = lax.broadcasted_iota(jnp.int32, (1, HEAD_COLS), 1)
    valid = (j * KV_BLOCK + (lane & (KV_BLOCK - 1))) < t_idx
    keep = lambda x: jnp.where(valid, x, 0.0)
    return keep, lb, keep(lb - z)


RUN_LANES = 128


def _run_lane(j, h):
    return lax.broadcasted_iota(jnp.int32, (1, RUN_LANES), 1) == j * N_HEADS + h


def _d_qblock(s):
    return D_QBLOCK if s % D_QBLOCK == 0 else KV_BLOCK


def _mix_d_fwd(p, n_ex, name):
    t = p.shape[0]
    s = t // n_ex
    tq = _d_qblock(s)
    nq = s // tq
    r = tq // KV_BLOCK
    nb = s // KV_BLOCK
    assert nb * N_HEADS <= RUN_LANES

    def body(q_ref, k_ref, v_ref, y_ref, runs_ref, kc, vc):
        i = pl.program_id(1)

        @pl.when(i == 0)
        def _():
            kc[...] = _stack_heads(k_ref[...].reshape(nb, KV_BLOCK, MIX_W))
            vc[...] = _stack_heads(v_ref[...].reshape(nb, KV_BLOCK, MIX_W))

        qs = _bf(_ld(q_ref) * (HEAD_DIM ** -0.5))
        t_idx = i * tq + lax.broadcasted_iota(jnp.int32, (tq, 1), 0)
        after_m = _tri2(True)
        nkb = (i + 1) * r

        runs_ref[...] = jnp.zeros_like(runs_ref)

        def one_block(j, runs, acc, on_diagonal):
            keep, lb, c = _sb_scores(qs, kc[j].reshape(HEAD_COLS, MIX_W), j, t_idx, on_diagonal)
            a = keep(jnp.exp(lb + _pair_dot(c, after_m) + _spread(runs)))
            acc = acc + _dot(_bf(a), vc[j].reshape(HEAD_COLS, MIX_W))
            kept = runs_ref[...]
            for h in range(N_HEADS):
                kept = jnp.where(_run_lane(j, h), runs[h], kept)
            runs_ref[...] = kept
            return tuple(ru + cs for ru, cs in zip(runs, _head_sums(c))), acc

        def trip(last, carry, on_diagonal):
            runs, acc = carry
            for sub in range(r):
                runs, acc = one_block(last - sub, runs, acc, on_diagonal)
            return runs, acc

        zero = jnp.zeros((tq, 1), _F32)
        carry = trip(nkb - 1, ((zero,) * N_HEADS, jnp.zeros((tq, MIX_W), _F32)), True)
        below = lambda m: nkb - 1 - (m + 1) * r
        carry = lax.fori_loop(0, i // 2, lambda m, carry: trip(below(2 * m + 1), trip(below(2 * m), carry, False), False), carry)
        _, acc = lax.fori_loop(0, i % 2, lambda m, carry: trip(below(i - 1), carry, False), carry)
        y_ref[...] = _bf(acc)

    return pl.pallas_call(
        body, name=name, grid=(n_ex, nq),
        in_specs=[pl.BlockSpec((tq, MIX_W), lambda e, i: (e * nq + i, 7)), pl.BlockSpec((s, MIX_W), lambda e, i: (e, 8)),
                  pl.BlockSpec((s, MIX_W), lambda e, i: (e, 9))],
        out_specs=[pl.BlockSpec((tq, MIX_W), lambda e, i: (e * nq + i, 0)),
                   pl.BlockSpec((tq, RUN_LANES), lambda e, i: (e * nq + i, 0))],
        out_shape=[jax.ShapeDtypeStruct((t, MIX_W), _BF), jax.ShapeDtypeStruct((t, RUN_LANES), _F32)],
        scratch_shapes=[pltpu.VMEM((nb, N_HEADS, KV_BLOCK, MIX_W), _BF), pltpu.VMEM((nb, N_HEADS, KV_BLOCK, MIX_W), _BF)],
        compiler_params=_params("parallel", "arbitrary"),
    )(p, p, p)


def _mix_d_bwd(p, kept_runs, dmix, n_ex, name):
    t = p.shape[0]
    s = t // n_ex
    tq = _d_qblock(s)
    nq = s // tq
    r = tq // KV_BLOCK
    nb = s // KV_BLOCK
    scale = HEAD_DIM ** -0.5

    def body(q_ref, k_ref, v_ref, runs_ref, do_ref, dq_ref, dk_ref, dv_ref, kc, vc):
        i = pl.program_id(1)

        @pl.when(i == 0)
        def _():
            kc[...] = _stack_heads(k_ref[...].reshape(nb, KV_BLOCK, MIX_W))
            vc[...] = _stack_heads(v_ref[...].reshape(nb, KV_BLOCK, MIX_W))
            dk_ref[...] = jnp.zeros_like(dk_ref)
            dv_ref[...] = jnp.zeros_like(dv_ref)

        q_scaled = _ld(q_ref) * scale
        qs = _bf(q_scaled)
        do = do_ref[...]
        dob = _bf(do)
        q_rows = _stack_heads_rows(q_scaled)
        do_rows = _stack_heads_rows(do)
        kept = runs_ref[...]
        t_idx = i * tq + lax.broadcasted_iota(jnp.int32, (tq, 1), 0)
        after_m = _tri2(True)
        before_m = _tri2(False)
        nkb = (i + 1) * r
        zero = jnp.zeros((tq, 1), _F32)

        def trip(first, carry, on_diagonal):
            for sub in range(r):
                carry = one_block(first + sub, carry, on_diagonal)
            return carry

        def one_block(j, carry, on_diagonal):
            pres, dq = carry
            rows = pl.ds(pl.multiple_of(j * KV_BLOCK, KV_BLOCK), KV_BLOCK)
            kj = kc[j].reshape(HEAD_COLS, MIX_W)
            keep, lb, c = _sb_scores(qs, kj, j, t_idx, on_diagonal)
            runs = [jnp.sum(jnp.where(_run_lane(j, h), kept, 0.0), axis=1, keepdims=True) for h in range(N_HEADS)]
            a = keep(jnp.exp(lb + _pair_dot(c, after_m) + _spread(runs)))
            g = a * _dot_nt(dob, vc[j].reshape(HEAD_COLS, MIX_W))
            before = _pair_dot(g, before_m) + _spread(pres)
            sig = jnp.exp(lb)
            dz = _bf(keep(g * (1.0 - sig) - sig * before))
            dk_ref[rows, :] += _dot_tn(_cols_to_rows(dz), q_rows)
            dv_ref[rows, :] += _dot_tn(_cols_to_rows(_bf(a)), do_rows)
            return tuple(pr + gs for pr, gs in zip(pres, _head_sums(g))), dq + _dot(dz, kj)

        init = ((zero,) * N_HEADS, jnp.zeros((tq, MIX_W), _F32))
        carry = lax.fori_loop(0, i // 2, lambda m, carry: trip((2 * m + 1) * r, trip(2 * m * r, carry, False), False), init)
        carry = lax.fori_loop(0, i % 2, lambda m, carry: trip((i - 1) * r, carry, False), carry)
        _, dq = trip(i * r, carry, True)
        dq_ref[...] = _bf(dq * scale)

    return pl.pallas_call(
        body, name=name, grid=(n_ex, nq),
        in_specs=[pl.BlockSpec((tq, MIX_W), lambda e, i: (e * nq + i, 7)), pl.BlockSpec((s, MIX_W), lambda e, i: (e, 8)),
                  pl.BlockSpec((s, MIX_W), lambda e, i: (e, 9)), pl.BlockSpec((tq, RUN_LANES), lambda e, i: (e * nq + i, 0)),
                  pl.BlockSpec((tq, MIX_W), lambda e, i: (e * nq + i, 3))],
        out_specs=[pl.BlockSpec((tq, MIX_W), lambda e, i: (e * nq + i, 0)), pl.BlockSpec((s, MIX_W), lambda e, i: (e, 0)),
                   pl.BlockSpec((s, MIX_W), lambda e, i: (e, 0))],
        out_shape=[jax.ShapeDtypeStruct((t, MIX_W), _BF), jax.ShapeDtypeStruct((t, MIX_W), _F32),
                   jax.ShapeDtypeStruct((t, MIX_W), _F32)],
        scratch_shapes=[pltpu.VMEM((nb, N_HEADS, KV_BLOCK, MIX_W), _BF), pltpu.VMEM((nb, N_HEADS, KV_BLOCK, MIX_W), _BF)],
        compiler_params=_params("parallel", "arbitrary"),
    )(p, p, p, kept_runs, dmix)


def _fwd_mix(x, w, l, n_ex):
    p, h1 = _norm_mm(x, w["norm1_g"][l], w["w_in_t"][l], "in_proj")
    y_a = _mix_a_fwd(p, w["conv_a_w"][l], n_ex, "mix_a_fwd")
    y_b, cb = _mix_b_fwd(p, w["conv_b_w"][l], w["conv_b_b"][l], w["ln_b_g"][l], w["ln_b_b"][l], n_ex, "mix_b_fwd")
    y_c = _mix_c_fwd(p, w["ln_c_g"][l], w["ln_c_b"][l], w["sgu_w"][l], w["sgu_b_full"][l], n_ex, "mix_c_fwd")
    y_d, runs_d = _mix_d_fwd(p, n_ex, "mix_d_fwd")
    return dict(x=x, h1=h1, p=p, cb=cb, runs_d=runs_d, mix=(y_a, y_b, y_c, y_d))


def _fwd_ffn(st, w, l, n_ex):
    x1 = _mm_res(st["mix"], w["w_out"][l], st["x"], "out_proj")
    up_pre, h2 = _norm_mm(x1, w["norm2_g"][l], w["w_up_t"][l], "up_proj")
    act, conv_g, conv_v = _ffn_mid_fwd(up_pre, w["conv_f_w"][l], n_ex, "ffn_mid_fwd")
    st.update(x1=x1, h2=h2, up_pre=up_pre, act=act, conv_g=conv_g, conv_v=conv_v)
    return _mm_res((act,), w["w_down"][l], x1, "down_proj")


def _bwd_ffn(st, w, l, dx, dxb, n_ex):
    g = {}
    dact = _mm_nt(dxb, w["w_down"][l], "down_proj_dx")
    g["w_down"] = _mm_tn(st["act"], dxb, "down_proj_dw", _BF)
    dup_g, dup_v, dwf_g, dwf_v = _ffn_mid_bwd(
        st["up_pre"], st["conv_g"], st["conv_v"], w["conv_f_w"][l], dact, n_ex, "ffn_mid_bwd")
    g["conv_f_w"] = jnp.concatenate([dwf_g, dwf_v], axis=1)
    dx, dxb, g["norm2_g"] = _mm_normbwd((dup_g, dup_v), w["w_up_t"][l], st["x1"], w["norm2_g"][l], dx, "up_proj_dx")
    g["w_up_t"] = _mm_tn_halves(dup_g, dup_v, st["h2"], "up_proj_dw")
    return dx, dxb, g


def _bwd_out_proj(st, w, l, dxb):
    return _mm_nt(dxb, w["w_out"][l], "out_proj_dx"), _mm_tn_parts(st["mix"], dxb, "out_proj_dw")


def _bwd_mixers(st, w, l, dx, dmix, n_ex):
    g = {}
    p = st["p"]
    dp_a, g["conv_a_w"] = _mix_a_bwd(p, w["conv_a_w"][l], dmix, n_ex, "mix_a_bwd")
    dp_b, g["conv_b_w"], g["conv_b_b"], g["ln_b_g"], g["ln_b_b"] = _mix_b_bwd(
        p, st["cb"], w["conv_b_w"][l], w["ln_b_g"][l], w["ln_b_b"][l], dmix, n_ex, "mix_b_bwd")
    dp_c, g["ln_c_g"], g["ln_c_b"], g["sgu_w"], g["sgu_b_t"] = _mix_c_bwd(
        p, w["ln_c_g"][l], w["ln_c_b"][l], w["sgu_w"][l], w["sgu_b_full"][l], dmix, n_ex, "mix_c_bwd")
    dq, dk, dv = _mix_d_bwd(p, st["runs_d"], dmix, n_ex, "mix_d_bwd")
    dp = (dp_a, dp_b, dp_c, dq, dk, dv)
    dx, dxb, g["norm1_g"] = _mm_normbwd(dp, w["w_in_t"][l], st["x"], w["norm1_g"][l], dx, "in_proj_dx")
    return dx, dxb, g, dp


def _bwd_mix(st, w, l, dx, dxb, n_ex):
    dmix, dw_out = _bwd_out_proj(st, w, l, dxb)
    dx, dxb, g, dp = _bwd_mixers(st, w, l, dx, dmix, n_ex)
    g["w_out"] = dw_out
    g["w_in_t"] = _mm_tn_parts(dp, st["h1"], "in_proj_dw")
    return dx, dxb, g


_MESH = pl.DeviceIdType.MESH
_ANY = pl.BlockSpec(memory_space=pl.ANY)


def _position():
    return lax.axis_index("x"), lax.axis_index("y"), lax.axis_index("c")


def _flat(px, py, pc):
    return 4 * px + 2 * py + pc


def _all_gather(shard, name, after):
    r, c_ = shard.shape

    def body(x_ref, after_ref, out_ref, send_sems, recv_sems, local_sem):
        x, y, c = _position()
        me, sibling = (x, y, c), (x, y, 1 - c)
        chips = [(1 - x, y), (x, 1 - y), (1 - x, 1 - y)]

        def copy(k, block, to, src=None):
            slab = out_ref.at[_flat(*block)]
            return pltpu.make_async_remote_copy(
                src_ref=slab if src is None else src, dst_ref=slab, send_sem=send_sems.at[k], recv_sem=recv_sems.at[k],
                device_id=to, device_id_type=_MESH)

        mine = pltpu.make_async_copy(x_ref, out_ref.at[_flat(*me)], local_sem)
        mine.start()
        first = [copy(0, me, sibling, src=x_ref)]
        first += [copy(1 + j, me, (*chip, c), src=x_ref) for j, chip in enumerate(chips)]
        for cp in first:
            cp.start()
        passed = [copy(4 + j, (*chip, c), sibling) for j, chip in enumerate(chips)]
        for j, chip in enumerate(chips):
            copy(1 + j, (*chip, c), me).wait_recv()
            passed[j].start()
        copy(0, sibling, me).wait_recv()
        for j, chip in enumerate(chips):
            copy(4 + j, (*chip, 1 - c), me).wait_recv()
        for cp in first + passed:
            cp.wait_send()
        mine.wait()

    return pl.pallas_call(
        body, name=name, out_shape=jax.ShapeDtypeStruct((N_DEV, r, c_), shard.dtype),
        in_specs=[_ANY, _ANY], out_specs=_ANY,
        scratch_shapes=[pltpu.SemaphoreType.DMA((7,)), pltpu.SemaphoreType.DMA((7,)), pltpu.SemaphoreType.DMA],
    )(shard, after)


_HBM = pl.BlockSpec(memory_space=pltpu.HBM)
_SEM = pl.BlockSpec(memory_space=pltpu.SEMAPHORE)
_DATAFLOW = pltpu.SideEffectType.DATAFLOW_SIDE_EFFECTING


def _peers(x, y, c):
    return [((1 - x) if (k + 1) & 4 else x, (1 - y) if (k + 1) & 2 else y, (1 - c) if (k + 1) & 1 else c)
            for k in range(N_DEV - 1)]


def _direct_copies(src_refs, land_refs, send_sems, recv_sems, to_all):
    x, y, c = _position()
    my = _flat(x, y, c)
    out, back = [], []
    for m, (src_ref, land_ref) in enumerate(zip(src_refs, land_refs)):
        for k, peer in enumerate(_peers(x, y, c)):
            src = src_ref if to_all else src_ref.at[_flat(*peer)]
            n = m * (N_DEV - 1) + k
            sems = dict(send_sem=send_sems.at[n], recv_sem=recv_sems.at[n], device_id=peer, device_id_type=_MESH)
            out.append(pltpu.make_async_remote_copy(src_ref=src, dst_ref=land_ref.at[my], **sems))
            back.append(pltpu.make_async_remote_copy(src_ref=src, dst_ref=land_ref.at[_flat(*peer)], **sems))
    return out, back


def _exchange_start(srcs, to_all, after, name):
    n = len(srcs)
    n_sems = n * (N_DEV - 1)
    land_shapes = [(N_DEV,) + tuple(a.shape[-2:]) for a in srcs]

    def body(*refs):
        src_refs, land_refs = refs[:n], refs[n:2 * n]
        send_sems, recv_sems = refs[2 * n + 1], refs[2 * n + 2]
        token = refs[-1]
        for cp in _direct_copies(src_refs, land_refs, send_sems, recv_sems, to_all)[0]:
            cp.start()
        token[...] = jnp.zeros_like(token)

    lands = [pltpu.with_memory_space_constraint(lax.empty(shp, a.dtype), pltpu.HBM) for shp, a in zip(land_shapes, srcs)]
    outs = pl.pallas_call(
        body, name=name,
        out_shape=(pltpu.SemaphoreType.DMA((n_sems,)), pltpu.SemaphoreType.DMA((n_sems,)),
                   *[pltpu.HBM(a.shape, a.dtype) for a in srcs], *[pltpu.HBM(shp, a.dtype) for shp, a in zip(land_shapes, srcs)],
                   jax.ShapeDtypeStruct((8, 128), _F32)),
        in_specs=(_HBM,) * (2 * n) + (_ANY,),
        out_specs=(_SEM, _SEM) + (_HBM,) * (2 * n) + (pl.BlockSpec(memory_space=pltpu.VMEM),),
        input_output_aliases={i: 2 + i for i in range(2 * n)},
        compiler_params=pltpu.CompilerParams(has_side_effects=_DATAFLOW),
    )(*[pltpu.with_memory_space_constraint(a, pltpu.HBM) for a in srcs], *lands, after)
    return (outs[0], outs[1], outs[2:2 + n], outs[2 + n:2 + 2 * n], to_all), outs[-1]


def _exchange_wait(handle, after, name):
    send_sems, recv_sems, srcs, lands, to_all = handle
    n = len(srcs)

    def body(*refs):
        out, back = _direct_copies(refs[:n], refs[n:2 * n], refs[2 * n], refs[2 * n + 1], to_all)
        for cp in out:
            cp.wait_send()
        for cp in back:
            cp.wait_recv()

    outs = pl.pallas_call(
        body, name=name,
        out_shape=tuple(pltpu.HBM(a.shape, a.dtype) for a in (*srcs, *lands)),
        in_specs=(_HBM,) * (2 * n) + (_SEM, _SEM, _ANY), out_specs=(_HBM,) * (2 * n),
        input_output_aliases={i: i for i in range(2 * n)},
        compiler_params=pltpu.CompilerParams(has_side_effects=_DATAFLOW),
    )(*srcs, *lands, send_sems, recv_sems, after)
    return outs[:n], outs[n:]


def _with_own(landed, own):
    my = _flat(*_position())
    return lax.dynamic_update_slice(landed, own[None], (my, 0, 0))


def _sum_slabs(slabs, name):
    n, r, c_ = slabs.shape
    tr = _pick_tile(r, 16, max(16, (12 << 20) // (n * c_ * slabs.dtype.itemsize)))

    def body(x_ref, o_ref):
        acc = x_ref[0].astype(_F32)
        for k in range(1, n):
            acc = acc + x_ref[k].astype(_F32)
        o_ref[...] = acc

    return pl.pallas_call(
        body, name=name, grid=(r // tr,),
        in_specs=[pl.BlockSpec((n, tr, c_), lambda i: (0, i, 0))],
        out_specs=pl.BlockSpec((tr, c_), lambda i: (i, 0)),
        out_shape=jax.ShapeDtypeStruct((r, c_), _F32),
        compiler_params=_params("parallel"),
    )(slabs)


def _adamw(w, g, m, v, name):
    r, c_ = w.shape
    tr = _pick_tile(r, 8, 512)

    def body(w_ref, g_ref, m_ref, v_ref, d_ref, nm_ref, nv_ref):
        _adamw_refs(w_ref, g_ref, m_ref, v_ref, d_ref, nm_ref, nv_ref)

    spec = pl.BlockSpec((tr, c_), lambda i: (i, 0))
    shape = jax.ShapeDtypeStruct((r, c_), _F32)
    return pl.pallas_call(
        body, name=name, grid=(r // tr,), in_specs=[spec] * 4, out_specs=[spec] * 3, out_shape=[shape] * 3,
        compiler_params=_params("parallel"),
    )(w, g, m, v)


def _adamw_refs(w_ref, g_ref, m_ref, v_ref, d_ref, nm_ref, nv_ref):
    gv = g_ref[...]
    nm = ADAM_B1 * m_ref[...] + (1.0 - ADAM_B1) * gv
    nv = ADAM_B2 * v_ref[...] + (1.0 - ADAM_B2) * (gv * gv)
    m_hat = nm / (1.0 - ADAM_B1 ** ADAM_STEP)
    v_hat = nv / (1.0 - ADAM_B2 ** ADAM_STEP)
    d_ref[...] = -ADAM_LR * (m_hat / (jnp.sqrt(v_hat) + ADAM_EPS) + ADAM_WD * w_ref[...])
    nm_ref[...] = nm
    nv_ref[...] = nv


def _adamw_small(params, name):
    n = len(params)

    def body(*refs):
        for i in range(n):
            _adamw_refs(*refs[4 * i:4 * i + 4], *refs[4 * n + 3 * i:4 * n + 3 * i + 3])

    outs = pl.pallas_call(
        body, name=name,
        out_shape=[jax.ShapeDtypeStruct(p[0].shape, _F32) for p in params for _ in range(3)],
        compiler_params=pltpu.CompilerParams(vmem_limit_bytes=VMEM_LIMIT),
    )(*[a for p in params for a in p])
    return [tuple(outs[3 * i:3 * i + 3]) for i in range(n)]


_SMALL = ("norm1_g", "conv_a_w", "conv_b_w", "conv_b_b", "ln_b_g", "ln_b_b", "ln_c_g", "ln_c_b", "sgu_w", "sgu_b",
          "norm2_g", "conv_f_w", "final_g")
_CONV_SHARDED = ("conv_a_w", "conv_b_w", "conv_f_w")
_NAMES = ("norm1_g", "w_in", "conv_a_w", "conv_b_w", "conv_b_b", "ln_b_g", "ln_b_b", "ln_c_g", "ln_c_b", "sgu_w", "sgu_b",
          "w_out", "norm2_g", "w_up", "conv_f_w", "w_down", "final_g")


def _pack_rows(parts, lanes=128, row_multiple=8):
    flat = jnp.concatenate([a.reshape(-1) for a in parts])
    rows = -(-flat.shape[0] // lanes)
    rows = -(-rows // row_multiple) * row_multiple
    return jnp.pad(flat, (0, rows * lanes - flat.shape[0])).reshape(rows, lanes)


def _unpack_rows(packed, shapes):
    flat = packed.reshape(-1)
    out, off = [], 0
    for shp in shapes:
        size = 1
        for s in shp:
            size *= s
        out.append(flat[off:off + size].reshape(shp))
        off += size
    return out


def _gather_conv_weights(conv_a_w, conv_b_w, conv_f_w, after):
    shards = (conv_a_w, conv_b_w, conv_f_w)
    flat = _all_gather(_pack_rows(shards), "gather_conv_weights", after).reshape(N_DEV, -1)
    full, off = [], 0
    for s in shards:
        layers, taps, width = s.shape
        per_dev = flat[:, off:off + s.size].reshape(N_DEV, layers, taps, width)
        full.append(jnp.moveaxis(per_dev, 0, 2).reshape(layers, taps, N_DEV * width))
        off += s.size
    return full


def kernel(x, norm1_g, w_in, conv_a_w, conv_b_w, conv_b_b, ln_b_g, ln_b_b, ln_c_g, ln_c_b, sgu_w, sgu_b, w_out, norm2_g, w_up, conv_f_w, w_down, final_g, loss_target, m_norm1_g, m_w_in, m_conv_a_w, m_conv_b_w, m_conv_b_b, m_ln_b_g, m_ln_b_b, m_ln_c_g, m_ln_c_b, m_sgu_w, m_sgu_b, m_w_out, m_norm2_g, m_w_up, m_conv_f_w, m_w_down, m_final_g, v_norm1_g, v_w_in, v_conv_a_w, v_conv_b_w, v_conv_b_b, v_ln_b_g, v_ln_b_b, v_ln_c_g, v_ln_c_b, v_sgu_w, v_sgu_b, v_w_out, v_norm2_g, v_w_up, v_conv_f_w, v_w_down, v_final_g):
    weights = dict(norm1_g=norm1_g, w_in=w_in, conv_a_w=conv_a_w, conv_b_w=conv_b_w, conv_b_b=conv_b_b, ln_b_g=ln_b_g,
                   ln_b_b=ln_b_b, ln_c_g=ln_c_g, ln_c_b=ln_c_b, sgu_w=sgu_w, sgu_b=sgu_b, w_out=w_out, norm2_g=norm2_g,
                   w_up=w_up, conv_f_w=conv_f_w, w_down=w_down, final_g=final_g)
    mom1 = dict(norm1_g=m_norm1_g, w_in=m_w_in, conv_a_w=m_conv_a_w, conv_b_w=m_conv_b_w, conv_b_b=m_conv_b_b,
                ln_b_g=m_ln_b_g, ln_b_b=m_ln_b_b, ln_c_g=m_ln_c_g, ln_c_b=m_ln_c_b, sgu_w=m_sgu_w, sgu_b=m_sgu_b,
                w_out=m_w_out, norm2_g=m_norm2_g, w_up=m_w_up, conv_f_w=m_conv_f_w, w_down=m_w_down, final_g=m_final_g)
    mom2 = dict(norm1_g=v_norm1_g, w_in=v_w_in, conv_a_w=v_conv_a_w, conv_b_w=v_conv_b_w, conv_b_b=v_conv_b_b,
                ln_b_g=v_ln_b_g, ln_b_b=v_ln_b_b, ln_c_g=v_ln_c_g, ln_c_b=v_ln_c_b, sgu_w=v_sgu_w, sgu_b=v_sgu_b,
                w_out=v_w_out, norm2_g=v_norm2_g, w_up=v_w_up, conv_f_w=v_conv_f_w, w_down=v_w_down, final_g=v_final_g)
    n_ex, seq, d = x.shape
    depth = w_in.shape[0]
    assert depth == 2
    my = _flat(*_position())
    row = lambda a, l: a[l][None]
    tied = lambda a, token: a + token[0:1, 0:1]

    slab = {"w_in": [_bf(jnp.swapaxes(w_in, 1, 2)[l]) for l in range(depth)], "w_out": [_bf(w_out[l]) for l in range(depth)],
            "w_up": [_bf(jnp.swapaxes(w_up, 1, 2)[l]) for l in range(depth)], "w_down": [_bf(w_down[l]) for l in range(depth)]}
    rows = {name: parts[0].shape[0] for name, parts in slab.items()}
    key_of = {"w_in": "w_in_t", "w_out": "w_out", "w_up": "w_up_t", "w_down": "w_down"}
    rest_layer0 = [("w_out", 0), ("w_up", 0), ("w_down", 0)]
    all_layer1 = [("w_in", 1), ("w_out", 1), ("w_up", 1), ("w_down", 1)]

    w_in0 = _all_gather(slab["w_in"][0], "gather_w_in0", norm1_g)
    conv_a_full, conv_b_full, conv_f_full = _gather_conv_weights(conv_a_w, conv_b_w, conv_f_w, w_in0)
    gather0, token = _exchange_start([slab[n][l] for n, l in rest_layer0], True, conv_f_full, "gather_layer0_start")
    w = {
        "norm1_g": [row(norm1_g, l) for l in range(depth)], "w_in_t": [None] * depth,
        "conv_a_w": [conv_a_full[l] for l in range(depth)], "conv_b_w": [conv_b_full[l] for l in range(depth)],
        "conv_b_b": [row(conv_b_b, l) for l in range(depth)], "ln_b_g": [row(ln_b_g, l) for l in range(depth)],
        "ln_b_b": [row(ln_b_b, l) for l in range(depth)], "ln_c_g": [row(ln_c_g, l) for l in range(depth)],
        "ln_c_b": [row(ln_c_b, l) for l in range(depth)], "sgu_w": [sgu_w[l] for l in range(depth)],
        "sgu_b_full": [jnp.repeat(sgu_b[l].T, HEAD_DIM, axis=1) for l in range(depth)],
        "w_out": [None] * depth, "norm2_g": [row(norm2_g, l) for l in range(depth)], "w_up_t": [None] * depth,
        "conv_f_w": [conv_f_full[l] for l in range(depth)], "w_down": [None] * depth, "final_g": final_g[None],
    }
    w["w_in_t"][0] = w_in0.reshape(N_DEV * rows["w_in"], d)
    w["norm1_g"][0] = tied(row(norm1_g, 0), token)

    def land_weights(handle, after, which, name):
        owns, landed = _exchange_wait(handle, after, name)
        for (n, l), own, got in zip(which, owns, landed):
            w[key_of[n]][l] = _with_own(got, own).reshape(N_DEV * rows[n], d)
        return landed[0]

    st0 = _fwd_mix(x.reshape(n_ex * seq, d), w, 0, n_ex)
    landed0 = land_weights(gather0, st0["mix"][3], rest_layer0, "gather_layer0_wait")
    gather1, token = _exchange_start([slab[n][l] for n, l in all_layer1], True, landed0, "gather_layer1_start")
    w["norm2_g"][0] = tied(row(norm2_g, 0), token)
    x_mid = _fwd_ffn(st0, w, 0, n_ex)
    land_weights(gather1, x_mid, all_layer1, "gather_layer1_wait")
    st1 = _fwd_mix(x_mid, w, 1, n_ex)
    x_out = _fwd_ffn(st1, w, 1, n_ex)
    dx, dxb, d_final_g, loss = _final_loss(x_out, w["final_g"], loss_target.reshape(n_ex * seq, d), "final_loss")
    loss = lax.psum(loss[0, 0], ("x", "y", "c"))

    def send_grads(g, which, after, name):
        return _exchange_start([g[key_of[n]].reshape(N_DEV, rows[n], d) for n, _ in which], False, after, name)

    dx, dxb, g_ffn1 = _bwd_ffn(st1, w, 1, dx, dxb, n_ex)
    dx, dxb, g_mix1 = _bwd_mix(st1, w, 1, dx, dxb, n_ex)
    grads1, token = send_grads({**g_ffn1, **g_mix1}, all_layer1, dx, "exchange_layer1_start")
    w["norm2_g"][0] = tied(row(norm2_g, 0), token)
    dx, dxb, g_ffn0 = _bwd_ffn(st0, w, 0, dx, dxb, n_ex)
    g_ffn0["w_out"] = _mm_tn_parts(st0["mix"], dxb, "out_proj_dw")
    ffn_layer0 = [("w_out", 0), ("w_up", 0), ("w_down", 0)]
    grads0a, token = send_grads(g_ffn0, ffn_layer0, dxb, "exchange_ffn0_start")
    dmix = _mm_nt(dxb, w["w_out"][0], "out_proj_dx", after=token)
    dx, dxb, g_mix0, dp0 = _bwd_mixers(st0, w, 0, dx, dmix, n_ex)
    grad_x = dx.reshape(n_ex, seq, d)
    g = {k: [{**g_ffn0, **g_mix0}[k], {**g_ffn1, **g_mix1}[k]] for k in g_mix0.keys() | g_ffn0.keys()}
    g["final_g"] = d_final_g

    small_local = {
        "norm1_g": jnp.stack([a[0] for a in g["norm1_g"]]), "conv_a_w": jnp.stack(g["conv_a_w"]),
        "conv_b_w": jnp.stack(g["conv_b_w"]), "conv_b_b": jnp.stack([a[0] for a in g["conv_b_b"]]),
        "ln_b_g": jnp.stack([a[0] for a in g["ln_b_g"]]), "ln_b_b": jnp.stack([a[0] for a in g["ln_b_b"]]),
        "ln_c_g": jnp.stack([a[0] for a in g["ln_c_g"]]), "ln_c_b": jnp.stack([a[0] for a in g["ln_c_b"]]),
        "sgu_w": jnp.stack(g["sgu_w"]), "sgu_b": jnp.stack([a.T for a in g["sgu_b_t"]]),
        "norm2_g": jnp.stack([a[0] for a in g["norm2_g"]]), "conv_f_w": jnp.stack(g["conv_f_w"]),
        "final_g": g["final_g"][0],
    }
    small, token = _exchange_start([_pack_rows([small_local[k] for k in _SMALL])], True, dx, "gather_small_start")
    g_mix0["w_in_t"] = _mm_tn_parts(dp0, st0["h1"], "in_proj_dw", after=token)
    mix_layer0 = [("w_in", 0)]
    grads0b, token = send_grads(g_mix0, mix_layer0, dx, "exchange_mix0_start")

    reduced = {}

    def land_grads(handle, after, which, name):
        sent, landed = _exchange_wait(handle, after, name + "_wait")
        for (n, l), src, got in zip(which, sent, landed):
            own = lax.dynamic_index_in_dim(src, my, 0, keepdims=False)
            reduced[(n, l)] = _sum_slabs(_with_own(got, own), name + "_sum_" + n)
        return reduced[which[-1]]

    def stacked_grad(name):
        stacked = jnp.stack([reduced[(name, l)] for l in range(depth)])
        return jnp.swapaxes(stacked, 1, 2) if name in ("w_in", "w_up") else stacked

    done = land_grads(grads1, token, all_layer1, "exchange_layer1")
    land_grads(grads0a, done, ffn_layer0, "exchange_ffn0")
    grads = {name: stacked_grad(name) for name in ("w_out", "w_up", "w_down")}

    delta, new_m, new_v = {}, {}, {}

    def as_2d(name):
        shp = weights[name].shape
        two_d = (-1, shp[-1]) if len(shp) > 1 else (1, shp[0])
        return tuple(a.reshape(two_d) for a in (weights[name], grads[name], mom1[name], mom2[name]))

    def keep(name, outs):
        delta[name], new_m[name], new_v[name] = (o.reshape(weights[name].shape) for o in outs)

    for name in ("w_up", "w_down", "w_out"):
        keep(name, _adamw(*as_2d(name), "adamw_" + name))

    (own,), (landed,) = _exchange_wait(small, new_v["w_out"], "gather_small_wait")
    small_sum = _sum_slabs(_with_own(landed, own), "sum_small_grads")
    for name, total in zip(_SMALL, _unpack_rows(small_sum, [small_local[k].shape for k in _SMALL])):
        if name in _CONV_SHARDED:
            width = weights[name].shape[-1]
            total = lax.dynamic_slice_in_dim(total, my * width, width, axis=-1)
        grads[name] = total
    for name, outs in zip(_SMALL, _adamw_small([as_2d(name) for name in _SMALL], "adamw_small")):
        keep(name, outs)

    land_grads(grads0b, new_v["final_g"], mix_layer0, "exchange_mix0")
    grads["w_in"] = stacked_grad("w_in")
    keep("w_in", _adamw(*as_2d("w_in"), "adamw_w_in"))

    return (loss, grad_x, *[grads[n] for n in _NAMES], *[delta[n] for n in _NAMES], *[new_m[n] for n in _NAMES],
            *[new_v[n] for n in _NAMES])
```

```python
import jax
import jax.numpy as jnp
from jax import lax
from jax.experimental import pallas as pl
from jax.experimental.pallas import tpu as pltpu

_F32 = jnp.float32
_BF = jnp.bfloat16

HEAD_DIM = 64
MIX_W = 256
N_HEADS = MIX_W // HEAD_DIM
CHUNK = 128
KV_BLOCK = 128
K_SHORT = 3
K_CONF = 31
K_FFN = 3
RMS_EPS = 1e-6
LN_EPS = 1e-5
ADAM_LR = 0.001
ADAM_B1 = 0.9
ADAM_B2 = 0.999
ADAM_EPS = 1e-08
ADAM_WD = 0.01
ADAM_STEP = 10
N_DEV = 8
VMEM_LIMIT = 56 * 1024 * 1024


def _bf(x):
    return x.astype(_BF)


def _ld(ref):
    return ref[...].astype(_F32)


_ANY_SPEC = pl.BlockSpec(memory_space=pl.ANY)


def _params(*sem):
    return pltpu.CompilerParams(dimension_semantics=sem, vmem_limit_bytes=VMEM_LIMIT)


def _dot(a, b):
    return jnp.dot(a, b, preferred_element_type=_F32)


def _dot_nt(a, b):
    return lax.dot_general(a, b, (((1,), (1,)), ((), ())), preferred_element_type=_F32)


def _dot_tn(a, b):
    return lax.dot_general(a, b, (((0,), (0,)), ((), ())), preferred_element_type=_F32)


def _row_tile(t, want):
    return want if t % want == 0 else t


def _pick_tile(rows, unit, max_rows):
    best = 0
    for cand in range(unit, min(rows, max_rows) + 1, unit):
        if rows % cand == 0:
            best = cand
    return best or rows


def _sigmoid(x):
    return 1.0 / (1.0 + jnp.exp(-x))


def _rms_rstd(x):
    return lax.rsqrt(jnp.mean(x * x, axis=-1, keepdims=True) + RMS_EPS)


def _norm_mm(x, g, w_t, name):
    t, d = x.shape
    n = w_t.shape[0]
    tm = _row_tile(t, 512)
    tn = _row_tile(n, 512)

    def body(x_ref, g_ref, w_ref, p_ref, h_ref):
        xv = x_ref[...]
        h = _bf(xv * _rms_rstd(xv) * g_ref[...])
        h_ref[...] = h
        for n0 in range(0, n, tn):
            p_ref[:, n0:n0 + tn] = _bf(_dot_nt(h, w_ref[n0:n0 + tn, :]))

    return pl.pallas_call(
        body, name=name, grid=(t // tm,),
        in_specs=[pl.BlockSpec((tm, d), lambda i: (i, 0)), pl.BlockSpec((1, d), lambda i: (0, 0)),
                  pl.BlockSpec((n, d), lambda i: (0, 0))],
        out_specs=[pl.BlockSpec((tm, n), lambda i: (i, 0)), pl.BlockSpec((tm, d), lambda i: (i, 0))],
        out_shape=[jax.ShapeDtypeStruct((t, n), _BF), jax.ShapeDtypeStruct((t, d), _BF)],
        compiler_params=_params("parallel"),
    )(x, g, w_t)


def _mm_nt(a, w_t, name, after=None):
    t, k = a.shape
    n = w_t.shape[0]
    tm = _row_tile(t, 512)
    tn = _row_tile(n, 512) if n % 512 == 0 else _row_tile(n, 256)

    def body(a_ref, w_ref, *rest):
        o_ref = rest[-1]
        av = a_ref[...]
        for n0 in range(0, n, tn):
            o_ref[:, n0:n0 + tn] = _bf(_dot_nt(av, w_ref[n0:n0 + tn, :]))

    extra = () if after is None else (after,)
    return pl.pallas_call(
        body, name=name, grid=(t // tm,),
        in_specs=[pl.BlockSpec((tm, k), lambda i: (i, 0)), pl.BlockSpec((n, k), lambda i: (0, 0))] + [_ANY_SPEC] * len(extra),
        out_specs=pl.BlockSpec((tm, n), lambda i: (i, 0)),
        out_shape=jax.ShapeDtypeStruct((t, n), _BF),
        compiler_params=_params("parallel"),
    )(a, w_t, *extra)


def _mm_res(parts, w, x, name):
    t = x.shape[0]
    k, d = w.shape
    tm = _row_tile(t, 512)
    widths = [a.shape[1] for a in parts]
    n_parts = len(parts)

    def body(*refs):
        w_ref, x_ref, o_ref = refs[n_parts:]
        acc, off = x_ref[...], 0
        for a_ref, width in zip(refs[:n_parts], widths):
            acc = acc + _dot(a_ref[...], w_ref[off:off + width, :])
            off += width
        o_ref[...] = acc

    return pl.pallas_call(
        body, name=name, grid=(t // tm,),
        in_specs=[pl.BlockSpec((tm, width), lambda i: (i, 0)) for width in widths] + [
            pl.BlockSpec((k, d), lambda i: (0, 0)), pl.BlockSpec((tm, d), lambda i: (i, 0))],
        out_specs=pl.BlockSpec((tm, d), lambda i: (i, 0)),
        out_shape=jax.ShapeDtypeStruct((t, d), _F32),
        compiler_params=_params("parallel"),
    )(*parts, w, x)


def _mm_normbwd(parts, w, x, g, dres, name):
    t = x.shape[0]
    k, d = w.shape
    tm = _row_tile(t, 512)
    widths = [a.shape[1] for a in parts]
    n_parts = len(parts)

    def body(*refs):
        a_refs = refs[:n_parts]
        w_ref, x_ref, g_ref, r_ref, dx_ref, dxb_ref, dg_ref = refs[n_parts:]
        dh, off = None, 0
        for a_ref, width in zip(a_refs, widths):
            term = _dot(_bf(a_ref[...]), w_ref[off:off + width, :])
            dh = term if dh is None else dh + term
            off += width
        xv = x_ref[...]
        rstd = _rms_rstd(xv)
        xn = xv * rstd
        u = dh * g_ref[...]
        dx = r_ref[...] + rstd * (u - xn * jnp.mean(u * xn, axis=-1, keepdims=True))
        dx_ref[...] = dx
        dxb_ref[...] = _bf(dx)

        @pl.when(pl.program_id(0) == 0)
        def _():
            dg_ref[...] = jnp.zeros_like(dg_ref)

        dg_ref[...] += jnp.sum(dh * xn, axis=0, keepdims=True)

    return pl.pallas_call(
        body, name=name, grid=(t // tm,),
        in_specs=[pl.BlockSpec((tm, width), lambda i: (i, 0)) for width in widths] + [
            pl.BlockSpec((k, d), lambda i: (0, 0)),
            pl.BlockSpec((tm, d), lambda i: (i, 0)), pl.BlockSpec((1, d), lambda i: (0, 0)),
            pl.BlockSpec((tm, d), lambda i: (i, 0))],
        out_specs=[pl.BlockSpec((tm, d), lambda i: (i, 0)), pl.BlockSpec((tm, d), lambda i: (i, 0)),
                   pl.BlockSpec((1, d), lambda i: (0, 0))],
        out_shape=[jax.ShapeDtypeStruct((t, d), _F32), jax.ShapeDtypeStruct((t, d), _BF),
                   jax.ShapeDtypeStruct((1, d), _F32)],
        compiler_params=_params("arbitrary"),
    )(*parts, w, x, g, dres)


def _mm_tn(a, b, name, out_dtype):
    t, m = a.shape
    n = b.shape[1]
    tm = _pick_tile(m, 128, 1408)
    tn = _pick_tile(n, 128, 1024)
    tk = _row_tile(t, 1024)
    nk = t // tk

    def body(a_ref, b_ref, o_ref, acc):
        kk = pl.program_id(2)

        @pl.when(kk == 0)
        def _():
            acc[...] = jnp.zeros_like(acc)

        acc[...] += _dot_tn(_bf(a_ref[...]), b_ref[...])

        @pl.when(kk == nk - 1)
        def _():
            o_ref[...] = acc[...].astype(o_ref.dtype)

    return pl.pallas_call(
        body, name=name, grid=(m // tm, n // tn, nk),
        in_specs=[pl.BlockSpec((tk, tm), lambda i, j, kk: (kk, i)), pl.BlockSpec((tk, tn), lambda i, j, kk: (kk, j))],
        out_specs=pl.BlockSpec((tm, tn), lambda i, j, kk: (i, j)),
        out_shape=jax.ShapeDtypeStruct((m, n), out_dtype),
        scratch_shapes=[pltpu.VMEM((tm, tn), _F32)],
        compiler_params=_params("parallel", "parallel", "arbitrary"),
    )(a, b)


def _mm_tn_halves(a0, a1, b, name):
    t, m = a0.shape
    n = b.shape[1]
    tm = _pick_tile(m, 128, 1408)
    tn = _pick_tile(n, 128, 1024)
    tk = _row_tile(t, 1024)
    nk = t // tk
    half = m // tm

    def body(a0_ref, a1_ref, b_ref, o_ref, acc):
        i = pl.program_id(0)
        kk = pl.program_id(2)

        @pl.when(kk == 0)
        def _():
            acc[...] = jnp.zeros_like(acc)

        @pl.when(i < half)
        def _():
            acc[...] += _dot_tn(a0_ref[...], b_ref[...])

        @pl.when(i >= half)
        def _():
            acc[...] += _dot_tn(a1_ref[...], b_ref[...])

        @pl.when(kk == nk - 1)
        def _():
            o_ref[...] = _bf(acc[...])

    return pl.pallas_call(
        body, name=name, grid=(2 * half, n // tn, nk),
        in_specs=[pl.BlockSpec((tk, tm), lambda i, j, kk: (jnp.where(i < half, kk, 0), jnp.minimum(i, half - 1))),
                  pl.BlockSpec((tk, tm), lambda i, j, kk: (jnp.where(i >= half, kk, 0), jnp.maximum(i - half, 0))),
                  pl.BlockSpec((tk, tn), lambda i, j, kk: (kk, j))],
        out_specs=pl.BlockSpec((tm, tn), lambda i, j, kk: (i, j)),
        out_shape=jax.ShapeDtypeStruct((2 * m, n), _BF),
        scratch_shapes=[pltpu.VMEM((tm, tn), _F32)],
        compiler_params=_params("parallel", "parallel", "arbitrary"),
    )(a0, a1, b)


def _mm_tn_parts(parts, b, name, after=None):
    t, n = b.shape
    widths = [a.shape[1] for a in parts]
    m = sum(widths)
    n_parts = len(parts)
    tk = _row_tile(t, 1024)
    nk = t // tk
    extra = () if after is None else (after,)

    def body(*refs):
        b_ref = refs[n_parts]
        o_ref, acc = refs[-2:]
        kk = pl.program_id(0)

        @pl.when(kk == 0)
        def _():
            acc[...] = jnp.zeros_like(acc)

        bv = b_ref[...]
        off = 0
        for a_ref, width in zip(refs[:n_parts], widths):
            acc[off:off + width, :] += _dot_tn(_bf(a_ref[...]), bv)
            off += width

        @pl.when(kk == nk - 1)
        def _():
            o_ref[...] = _bf(acc[...])

    return pl.pallas_call(
        body, name=name, grid=(nk,),
        in_specs=[pl.BlockSpec((tk, width), lambda kk: (kk, 0)) for width in widths] + [pl.BlockSpec((tk, n), lambda kk: (kk, 0))]
        + [_ANY_SPEC] * len(extra),
        out_specs=pl.BlockSpec((m, n), lambda kk: (0, 0)),
        out_shape=jax.ShapeDtypeStruct((m, n), _BF),
        scratch_shapes=[pltpu.VMEM((m, n), _F32)],
        compiler_params=_params("arbitrary"),
    )(*parts, b, *extra)


def _final_loss(x, g, target, name):
    t, d = x.shape
    tm = _row_tile(t, 256)

    def body(x_ref, g_ref, t_ref, dx_ref, dxb_ref, dg_ref, loss_ref):
        xv = x_ref[...]
        rstd = _rms_rstd(xv)
        xn = xv * rstd
        err = xn * g_ref[...] - t_ref[...]
        dy = err * (1.0 / d)
        u = dy * g_ref[...]
        dx = rstd * (u - xn * jnp.mean(u * xn, axis=-1, keepdims=True))
        dx_ref[...] = dx
        dxb_ref[...] = _bf(dx)

        @pl.when(pl.program_id(0) == 0)
        def _():
            dg_ref[...] = jnp.zeros_like(dg_ref)
            loss_ref[...] = jnp.zeros_like(loss_ref)

        dg_ref[...] += jnp.sum(dy * xn, axis=0, keepdims=True)
        loss_ref[...] += (0.5 / d) * jnp.sum(jnp.sum(err * err, axis=1, keepdims=True), axis=0, keepdims=True)

    return pl.pallas_call(
        body, name=name, grid=(t // tm,),
        in_specs=[pl.BlockSpec((tm, d), lambda i: (i, 0)), pl.BlockSpec((1, d), lambda i: (0, 0)),
                  pl.BlockSpec((tm, d), lambda i: (i, 0))],
        out_specs=[pl.BlockSpec((tm, d), lambda i: (i, 0)), pl.BlockSpec((tm, d), lambda i: (i, 0)),
                   pl.BlockSpec((1, d), lambda i: (0, 0)), pl.BlockSpec((1, 1), lambda i: (0, 0))],
        out_shape=[jax.ShapeDtypeStruct((t, d), _F32), jax.ShapeDtypeStruct((t, d), _BF),
                   jax.ShapeDtypeStruct((1, d), _F32), jax.ShapeDtypeStruct((1, 1), _F32)],
        compiler_params=_params("arbitrary"),
    )(x, g, target)


def _pad_rows(x, pad):
    return jnp.concatenate([x, jnp.zeros((pad, x.shape[1]), x.dtype)], axis=0)


def _shift_down(xp, s):
    return xp if s == 0 else pltpu.roll(xp, s, 0)


def _shift_up(xp, s):
    return xp if s == 0 else pltpu.roll(xp, xp.shape[0] - s, 0)


def _taps3(xp):
    one = _shift_down(xp, 1)
    return xp, one, _shift_down(one, 1)


def _conv3_taps(taps, w_ref):
    return w_ref[2:3, :] * taps[0] + w_ref[1:2, :] * taps[1] + w_ref[0:1, :] * taps[2]


def _conv3(xp, w_ref):
    return _conv3_taps(_taps3(xp), w_ref)


def _conv3_t(dyp, w_ref):
    one = _shift_up(dyp, 1)
    return w_ref[2:3, :] * dyp + w_ref[1:2, :] * one + w_ref[0:1, :] * _shift_up(one, 1)


def _conv3_dw(dyp, taps):
    return [jnp.sum(dyp * taps[2 - k], axis=0, keepdims=True) for k in range(3)]


def _ffn_mid_fwd(up_pre, wf, n_ex, name):
    t, f2 = up_pre.shape
    f = f2 // 2
    s = t // n_ex
    cb = MIX_W
    nb = f // cb

    def body(ug_ref, uv_ref, wg_ref, wv_ref, act_ref, gf_ref, vf_ref):
        gf = _conv3(_pad_rows(ug_ref[...].astype(_F32), 8), wg_ref)[:s]
        vf = _conv3(_pad_rows(uv_ref[...].astype(_F32), 8), wv_ref)[:s]
        act_ref[...] = _bf(gf * _sigmoid(gf) * vf)
        gf_ref[...] = _bf(gf)
        vf_ref[...] = _bf(vf)

    out = pl.BlockSpec((s, cb), lambda e, j: (e, j))
    return pl.pallas_call(
        body, name=name, grid=(n_ex, nb),
        in_specs=[pl.BlockSpec((s, cb), lambda e, j: (e, j)), pl.BlockSpec((s, cb), lambda e, j: (e, j + nb)),
                  pl.BlockSpec((K_FFN, cb), lambda e, j: (0, j)), pl.BlockSpec((K_FFN, cb), lambda e, j: (0, j + nb))],
        out_specs=[out, out, out],
        out_shape=[jax.ShapeDtypeStruct((t, f), _BF)] * 3,
        compiler_params=_params("parallel", "parallel"),
    )(up_pre, up_pre, wf, wf)


def _ffn_mid_bwd(up_pre, conv_g, conv_v, wf, dact, n_ex, name):
    t, f2 = up_pre.shape
    f = f2 // 2
    s = t // n_ex
    cb = MIX_W
    nb = f // cb

    def body(ug_ref, uv_ref, gf_ref, vf_ref, wg_ref, wv_ref, da_ref, dug_ref, duv_ref, dwg_ref, dwv_ref):
        gf = _ld(gf_ref)
        vf = _ld(vf_ref)
        sg = _sigmoid(gf)
        da = _ld(da_ref)

        @pl.when(pl.program_id(1) == 0)
        def _():
            dwg_ref[...] = jnp.zeros_like(dwg_ref)
            dwv_ref[...] = jnp.zeros_like(dwv_ref)

        def finish(dpost, w_ref, x_ref, du_ref, dw_ref):
            ahead = [_pad_rows(dpost, 8)]
            ahead.append(_shift_up(ahead[0], 1))
            ahead.append(_shift_up(ahead[1], 1))
            du_ref[...] = _bf((w_ref[2:3, :] * ahead[0] + w_ref[1:2, :] * ahead[1] + w_ref[0:1, :] * ahead[2])[:s])
            x = _ld(x_ref)
            for k in range(K_FFN):
                dw_ref[k:k + 1, :] += jnp.sum(ahead[2 - k][:s] * x, axis=0, keepdims=True)

        finish(da * vf * sg * (1.0 + gf * (1.0 - sg)), wg_ref, ug_ref, dug_ref, dwg_ref)
        finish(da * gf * sg, wv_ref, uv_ref, duv_ref, dwv_ref)

    return pl.pallas_call(
        body, name=name, grid=(nb, n_ex),
        in_specs=[pl.BlockSpec((s, cb), lambda j, e: (e, j)), pl.BlockSpec((s, cb), lambda j, e: (e, j + nb)),
                  pl.BlockSpec((s, cb), lambda j, e: (e, j)), pl.BlockSpec((s, cb), lambda j, e: (e, j)),
                  pl.BlockSpec((K_FFN, cb), lambda j, e: (0, j)), pl.BlockSpec((K_FFN, cb), lambda j, e: (0, j + nb)),
                  pl.BlockSpec((s, cb), lambda j, e: (e, j))],
        out_specs=[pl.BlockSpec((s, cb), lambda j, e: (e, j)), pl.BlockSpec((s, cb), lambda j, e: (e, j)),
                   pl.BlockSpec((K_FFN, cb), lambda j, e: (0, j)), pl.BlockSpec((K_FFN, cb), lambda j, e: (0, j))],
        out_shape=[jax.ShapeDtypeStruct((t, f), _BF), jax.ShapeDtypeStruct((t, f), _BF),
                   jax.ShapeDtypeStruct((K_FFN, f), _F32), jax.ShapeDtypeStruct((K_FFN, f), _F32)],
        compiler_params=_params("parallel", "arbitrary"),
    )(up_pre, up_pre, conv_g, conv_v, wf, wf, dact)


def _pcol(s, j):
    return pl.BlockSpec((s, MIX_W), lambda e, j=j: (e, j))


def _vec(rows=1):
    return pl.BlockSpec((rows, MIX_W), lambda e: (0, 0))


def _mix_a_fwd(p, wa, n_ex, name):
    t = p.shape[0]
    s = t // n_ex

    def body(gb_ref, gc_ref, ha_ref, w_ref, y_ref):
        cv = _conv3(_pad_rows(_ld(gc_ref) * _ld(ha_ref), 8), w_ref)[:s]
        y_ref[...] = _bf(_ld(gb_ref) * cv)

    return pl.pallas_call(
        body, name=name, grid=(n_ex,),
        in_specs=[_pcol(s, 0), _pcol(s, 1), _pcol(s, 2), _vec(K_SHORT)],
        out_specs=pl.BlockSpec((s, MIX_W), lambda e: (e, 0)),
        out_shape=jax.ShapeDtypeStruct((t, MIX_W), _BF),
        compiler_params=_params("parallel"),
    )(p, p, p, wa)


def _mix_a_bwd(p, wa, dmix, n_ex, name):
    t = p.shape[0]
    s = t // n_ex

    def body(gb_ref, gc_ref, ha_ref, w_ref, dy_ref, dp_ref, dw_ref):
        gc = _ld(gc_ref)
        ha = _ld(ha_ref)
        up = _taps3(_pad_rows(gc * ha, 8))
        cv = _conv3_taps(up, w_ref)[:s]
        dy = _ld(dy_ref)
        dcvp = _pad_rows(dy * _ld(gb_ref), 8)
        du = _conv3_t(dcvp, w_ref)[:s]
        dp_ref[:, 0:MIX_W] = _bf(dy * cv)
        dp_ref[:, MIX_W:2 * MIX_W] = _bf(du * ha)
        dp_ref[:, 2 * MIX_W:3 * MIX_W] = _bf(du * gc)

        @pl.when(pl.program_id(0) == 0)
        def _():
            dw_ref[...] = jnp.zeros_like(dw_ref)

        rows = _conv3_dw(dcvp, up)
        for k in range(3):
            dw_ref[k:k + 1, :] += rows[k]

    return pl.pallas_call(
        body, name=name, grid=(n_ex,),
        in_specs=[_pcol(s, 0), _pcol(s, 1), _pcol(s, 2), _vec(K_SHORT), _pcol(s, 0)],
        out_specs=[pl.BlockSpec((s, 3 * MIX_W), lambda e: (e, 0)), _vec(K_SHORT)],
        out_shape=[jax.ShapeDtypeStruct((t, 3 * MIX_W), _BF), jax.ShapeDtypeStruct((K_SHORT, MIX_W), _F32)],
        compiler_params=_params("arbitrary"),
    )(p, p, p, wa, dmix)


CONF_PAD = 32
CONF_ROWS = 64
_CONF_LANES = (slice(0, 128), slice(128, 256))


def _conf_taps(win, ahead):
    n = CONF_ROWS + CONF_PAD
    for b in range(8):
        rot = win if b == 0 else pltpu.roll(win, (n - b) if ahead else b, 0)
        for a in range(4):
            if 8 * a + b < K_CONF:
                yield rot, 8 * a + b, (8 * a) if ahead else (CONF_PAD - 8 * a)


def _ln_fwd(x, g, b):
    mu = jnp.mean(x, axis=-1, keepdims=True)
    xc = x - mu
    rstd = lax.rsqrt(jnp.mean(xc * xc, axis=-1, keepdims=True) + LN_EPS)
    xhat = xc * rstd
    return xhat * g + b, xhat, rstd


def _ln_bwd(dy, xhat, rstd, g):
    dxh = dy * g
    return rstd * (dxh - jnp.mean(dxh, axis=-1, keepdims=True) - xhat * jnp.mean(dxh * xhat, axis=-1, keepdims=True))


def _mix_b_fwd(p, wb, bb, lg, lb, n_ex, name):
    t = p.shape[0]
    s = t // n_ex

    def body(val_ref, gat_ref, w_ref, bb_ref, lg_ref, lb_ref, y_ref, cb_ref, xpad):
        xpad[0:CONF_PAD, :] = jnp.zeros((CONF_PAD, MIX_W), _F32)
        xpad[CONF_PAD:, :] = _ld(val_ref) * _sigmoid(_ld(gat_ref))

        def chunk(c, carry):
            r0 = pl.multiple_of(c * CONF_ROWS, CONF_ROWS)
            for lanes in _CONF_LANES:
                acc = None
                for rot, sh, lo in _conf_taps(xpad[pl.ds(r0, CONF_ROWS + CONF_PAD), lanes], False):
                    term = w_ref[K_CONF - 1 - sh:K_CONF - sh, lanes] * rot[lo:lo + CONF_ROWS]
                    acc = term if acc is None else acc + term
                cb_ref[pl.ds(r0, CONF_ROWS), lanes] = acc + bb_ref[:, lanes]
            return carry

        lax.fori_loop(0, s // CONF_ROWS, chunk, 0)
        yl, _, _ = _ln_fwd(cb_ref[...], lg_ref[...], lb_ref[...])
        y_ref[...] = _bf(yl * _sigmoid(yl))

    return pl.pallas_call(
        body, name=name, grid=(n_ex,),
        in_specs=[_pcol(s, 3), _pcol(s, 4), _vec(K_CONF), _vec(), _vec(), _vec()],
        out_specs=[pl.BlockSpec((s, MIX_W), lambda e: (e, 0)), pl.BlockSpec((s, MIX_W), lambda e: (e, 0))],
        out_shape=[jax.ShapeDtypeStruct((t, MIX_W), _BF), jax.ShapeDtypeStruct((t, MIX_W), _F32)],
        scratch_shapes=[pltpu.VMEM((CONF_PAD + s, MIX_W), _F32)],
        compiler_params=_params("parallel"),
    )(p, p, wb, bb, lg, lb)


def _mix_b_bwd(p, cb, wb, lg, lb, dmix, n_ex, name):
    t = p.shape[0]
    s = t // n_ex

    def body(val_ref, gat_ref, cb_ref, w_ref, lg_ref, lb_ref, dy_ref, dp_ref, dw_ref, dbb_ref, dlg_ref, dlb_ref,
             xpad, dpad, dglu_s, dw_acc):
        @pl.when(pl.program_id(0) == 0)
        def _():
            for r in (dw_ref, dbb_ref, dlg_ref, dlb_ref):
                r[...] = jnp.zeros_like(r)

        yl, xhat, rstd = _ln_fwd(cb_ref[...], lg_ref[...], lb_ref[...])
        sy = _sigmoid(yl)
        dyl = _ld(dy_ref) * sy * (1.0 + yl * (1.0 - sy))
        dlg_ref[...] += jnp.sum(dyl * xhat, axis=0, keepdims=True)
        dlb_ref[...] += jnp.sum(dyl, axis=0, keepdims=True)
        dcb = _ln_bwd(dyl, xhat, rstd, lg_ref[...])
        dbb_ref[...] += jnp.sum(dcb, axis=0, keepdims=True)

        val = _ld(val_ref)
        sg = _sigmoid(_ld(gat_ref))
        xpad[0:CONF_PAD, :] = jnp.zeros((CONF_PAD, MIX_W), _F32)
        xpad[CONF_PAD:, :] = val * sg
        dpad[0:s, :] = dcb
        dpad[s:, :] = jnp.zeros((CONF_PAD, MIX_W), _F32)
        dw_acc[...] = jnp.zeros_like(dw_acc)

        def chunk(c, carry):
            r0 = pl.multiple_of(c * CONF_ROWS, CONF_ROWS)
            for lanes in _CONF_LANES:
                d_win = dpad[pl.ds(r0, CONF_ROWS + CONF_PAD), lanes]
                d_rows = d_win[0:CONF_ROWS]
                acc = None
                for rot, sh, lo in _conf_taps(d_win, True):
                    term = w_ref[K_CONF - 1 - sh:K_CONF - sh, lanes] * rot[lo:lo + CONF_ROWS]
                    acc = term if acc is None else acc + term
                dglu_s[pl.ds(r0, CONF_ROWS), lanes] = acc
                for rot, sh, lo in _conf_taps(xpad[pl.ds(r0, CONF_ROWS + CONF_PAD), lanes], False):
                    prod = d_rows * rot[lo:lo + CONF_ROWS]
                    dw_acc[K_CONF - 1 - sh, :, lanes] += jnp.sum(prod.reshape(CONF_ROWS // 8, 8, 128), axis=0)
            return carry

        lax.fori_loop(0, s // CONF_ROWS, chunk, 0)
        dw_ref[...] += jnp.sum(dw_acc[...], axis=1)
        dglu = dglu_s[...]
        dp_ref[:, 0:MIX_W] = _bf(dglu * sg)
        dp_ref[:, MIX_W:2 * MIX_W] = _bf(dglu * val * sg * (1.0 - sg))

    return pl.pallas_call(
        body, name=name, grid=(n_ex,),
        in_specs=[_pcol(s, 3), _pcol(s, 4), pl.BlockSpec((s, MIX_W), lambda e: (e, 0)), _vec(K_CONF), _vec(), _vec(),
                  _pcol(s, 1)],
        out_specs=[pl.BlockSpec((s, 2 * MIX_W), lambda e: (e, 0)), _vec(K_CONF), _vec(), _vec(), _vec()],
        out_shape=[jax.ShapeDtypeStruct((t, 2 * MIX_W), _BF), jax.ShapeDtypeStruct((K_CONF, MIX_W), _F32),
                   jax.ShapeDtypeStruct((1, MIX_W), _F32), jax.ShapeDtypeStruct((1, MIX_W), _F32),
                   jax.ShapeDtypeStruct((1, MIX_W), _F32)],
        scratch_shapes=[pltpu.VMEM((CONF_PAD + s, MIX_W), _F32), pltpu.VMEM((s + CONF_PAD, MIX_W), _F32),
                        pltpu.VMEM((s, MIX_W), _F32), pltpu.VMEM((K_CONF, 8, MIX_W), _F32)],
        compiler_params=_params("arbitrary"),
    )(p, p, cb, wb, lg, lb, dmix)


_INV_SQRT2 = 0.7071067811865476
_INV_SQRT2PI = 0.3989422804014327


def _gelu(x):
    return 0.5 * x * (1.0 + lax.erf(x * _INV_SQRT2))


def _gelu_grad(x):
    return 0.5 * (1.0 + lax.erf(x * _INV_SQRT2)) + x * _INV_SQRT2PI * jnp.exp(-0.5 * x * x)


def _head_masks(width=MIX_W):
    lane = lax.broadcasted_iota(jnp.int32, (1, width), 1)
    return [(lane >= h * HEAD_DIM) & (lane < (h + 1) * HEAD_DIM) for h in range(N_HEADS)]


def _tril_mask():
    r = lax.broadcasted_iota(jnp.int32, (CHUNK, CHUNK), 0)
    c = lax.broadcasted_iota(jnp.int32, (CHUNK, CHUNK), 1)
    return c <= r


def _sgu_apply(ws_ref, x3, transpose):
    n = x3.shape[0]
    tril = _tril_mask()
    masks = _head_masks()
    xb = _bf(x3)
    out = jnp.zeros(x3.shape, _F32)
    for h in range(N_HEADS):
        w = _bf(jnp.where(tril, ws_ref[h], 0.0))
        wb = jnp.broadcast_to(w[None], (n, CHUNK, CHUNK))
        dims = (((1,), (1,)), ((0,), (0,))) if transpose else (((2,), (1,)), ((0,), (0,)))
        r = lax.dot_general(wb, xb, dims, preferred_element_type=_F32)
        out = out + jnp.where(masks[h][None], r, 0.0)
    return out


def _mix_c_fwd(p, lg, lb, ws, sb_full, n_ex, name):
    t = p.shape[0]
    s = t // n_ex
    nc = s // CHUNK

    def body(pu_ref, pv_ref, lg_ref, lb_ref, ws_ref, sb_ref, y_ref):
        u = _gelu(_ld(pu_ref))
        vl, _, _ = _ln_fwd(_gelu(_ld(pv_ref)), lg_ref[...], lb_ref[...])
        sp = _sgu_apply(ws_ref, vl.reshape(nc, CHUNK, MIX_W), False) + sb_ref[...][None]
        y_ref[...] = _bf(u * sp.reshape(s, MIX_W))

    return pl.pallas_call(
        body, name=name, grid=(n_ex,),
        in_specs=[_pcol(s, 5), _pcol(s, 6), _vec(), _vec(),
                  pl.BlockSpec((N_HEADS, CHUNK, CHUNK), lambda e: (0, 0, 0)), pl.BlockSpec((CHUNK, MIX_W), lambda e: (0, 0))],
        out_specs=pl.BlockSpec((s, MIX_W), lambda e: (e, 0)),
        out_shape=jax.ShapeDtypeStruct((t, MIX_W), _BF),
        compiler_params=_params("parallel"),
    )(p, p, lg, lb, ws, sb_full)


def _mix_c_bwd(p, lg, lb, ws, sb_full, dmix, n_ex, name):
    t = p.shape[0]
    s = t // n_ex
    nc = s // CHUNK

    def body(pu_ref, pv_ref, lg_ref, lb_ref, ws_ref, sb_ref, dy_ref, dp_ref, dlg_ref, dlb_ref, dws_ref, dsb_ref):
        @pl.when(pl.program_id(0) == 0)
        def _():
            for r in (dlg_ref, dlb_ref, dws_ref, dsb_ref):
                r[...] = jnp.zeros_like(r)

        pu = _ld(pu_ref)
        pv = _ld(pv_ref)
        u = _gelu(pu)
        vl, xhat, rstd = _ln_fwd(_gelu(pv), lg_ref[...], lb_ref[...])
        vl3 = vl.reshape(nc, CHUNK, MIX_W)
        sp = _sgu_apply(ws_ref, vl3, False) + sb_ref[...][None]
        dy = _ld(dy_ref)
        dp_ref[:, 0:MIX_W] = _bf(dy * sp.reshape(s, MIX_W) * _gelu_grad(pu))
        dsp3 = (dy * u).reshape(nc, CHUNK, MIX_W)
        dsb_full = jnp.sum(dsp3, axis=0)
        masks = _head_masks()
        tril = _tril_mask()
        dspb = _bf(dsp3)
        vlb = _bf(vl3)
        for h in range(N_HEADS):
            dsb_ref[:, h:h + 1] += jnp.sum(jnp.where(masks[h], dsb_full, 0.0), axis=1, keepdims=True)
            dm = jnp.where(masks[h][None], dspb, jnp.zeros_like(dspb))
            g3 = lax.dot_general(dm, vlb, (((2,), (2,)), ((0,), (0,))), preferred_element_type=_F32)
            dws_ref[h] += jnp.where(tril, jnp.sum(g3, axis=0), 0.0)
        dvl = _sgu_apply(ws_ref, dsp3, True).reshape(s, MIX_W)
        dlg_ref[...] += jnp.sum(dvl * xhat, axis=0, keepdims=True)
        dlb_ref[...] += jnp.sum(dvl, axis=0, keepdims=True)
        dp_ref[:, MIX_W:2 * MIX_W] = _bf(_ln_bwd(dvl, xhat, rstd, lg_ref[...]) * _gelu_grad(pv))

    return pl.pallas_call(
        body, name=name, grid=(n_ex,),
        in_specs=[_pcol(s, 5), _pcol(s, 6), _vec(), _vec(),
                  pl.BlockSpec((N_HEADS, CHUNK, CHUNK), lambda e: (0, 0, 0)), pl.BlockSpec((CHUNK, MIX_W), lambda e: (0, 0)),
                  _pcol(s, 2)],
        out_specs=[pl.BlockSpec((s, 2 * MIX_W), lambda e: (e, 0)), _vec(), _vec(),
                   pl.BlockSpec((N_HEADS, CHUNK, CHUNK), lambda e: (0, 0, 0)), pl.BlockSpec((CHUNK, N_HEADS), lambda e: (0, 0))],
        out_shape=[jax.ShapeDtypeStruct((t, 2 * MIX_W), _BF), jax.ShapeDtypeStruct((1, MIX_W), _F32),
                   jax.ShapeDtypeStruct((1, MIX_W), _F32), jax.ShapeDtypeStruct((N_HEADS, CHUNK, CHUNK), _F32),
                   jax.ShapeDtypeStruct((CHUNK, N_HEADS), _F32)],
        compiler_params=_params("arbitrary"),
    )(p, p, lg, lb, ws, sb_full, dmix)


D_QBLOCK = 256
HEAD_COLS = N_HEADS * KV_BLOCK


def _stack_heads(x3):
    return jnp.stack([_bf(jnp.where(m[None], x3, 0.0)) for m in _head_masks()], axis=1)


def _stack_heads_rows(x):
    return jnp.concatenate([_bf(jnp.where(m, x, 0.0)) for m in _head_masks()], axis=0)


def _cols_to_rows(x):
    return jnp.concatenate([x[:, h * KV_BLOCK:(h + 1) * KV_BLOCK] for h in range(N_HEADS)], axis=0)


def _head_sums(x):
    return [jnp.sum(x[:, h * KV_BLOCK:(h + 1) * KV_BLOCK], axis=1, keepdims=True) for h in range(N_HEADS)]


def _spread(cols):
    tq = cols[0].shape[0]
    return jnp.concatenate([jnp.broadcast_to(c, (tq, KV_BLOCK)) for c in cols], axis=1)


def _pair_dot(x, m2):
    half = 2 * KV_BLOCK
    xb = _bf(x)
    return jnp.concatenate([_dot(xb[:, :half], m2), _dot(xb[:, half:], m2)], axis=1)


def _tri2(lower):
    n = 2 * KV_BLOCK
    r = lax.broadcasted_iota(jnp.int32, (n, n), 0)
    c = lax.broadcasted_iota(jnp.int32, (n, n), 1)
    same = (r >= KV_BLOCK) == (c >= KV_BLOCK)
    return _bf(jnp.where(same & (r > c if lower else r < c), 1.0, 0.0))


def _sb_scores(qs, kc, j, t_idx, on_diagonal):
    z = _dot_nt(qs, kc)
    lb = jnp.minimum(z, 0.0) - jnp.log(1.0 + jnp.exp(-jnp.abs(z)))
    if not on_diagonal:
        return (lambda x: x), lb, lb - z
    lane = lax.broadcasted_iota(jnp.int32, (1, HEAD_COLS), 1)
    valid = (j * KV_BLOCK + (lane & (KV_BLOCK - 1))) < t_idx
    keep = lambda x: jnp.where(valid, x, 0.0)
    return keep, lb, keep(lb - z)


RUN_LANES = 128


def _run_lane(j, h):
    return lax.broadcasted_iota(jnp.int32, (1, RUN_LANES), 1) == j * N_HEADS + h


def _d_qblock(s):
    return D_QBLOCK if s % D_QBLOCK == 0 else KV_BLOCK


def _mix_d_fwd(p, n_ex, name):
    t = p.shape[0]
    s = t // n_ex
    tq = _d_qblock(s)
    nq = s // tq
    r = tq // KV_BLOCK
    nb = s // KV_BLOCK
    assert nb * N_HEADS <= RUN_LANES

    def body(q_ref, k_ref, v_ref, y_ref, runs_ref, kc, vc):
        i = pl.program_id(1)

        @pl.when(i == 0)
        def _():
            kc[...] = _stack_heads(k_ref[...].reshape(nb, KV_BLOCK, MIX_W))
            vc[...] = _stack_heads(v_ref[...].reshape(nb, KV_BLOCK, MIX_W))

        qs = _bf(_ld(q_ref) * (HEAD_DIM ** -0.5))
        t_idx = i * tq + lax.broadcasted_iota(jnp.int32, (tq, 1), 0)
        after_m = _tri2(True)
        nkb = (i + 1) * r

        runs_ref[...] = jnp.zeros_like(runs_ref)

        def one_block(j, runs, acc, on_diagonal):
            keep, lb, c = _sb_scores(qs, kc[j].reshape(HEAD_COLS, MIX_W), j, t_idx, on_diagonal)
            a = keep(jnp.exp(lb + _pair_dot(c, after_m) + _spread(runs)))
            acc = acc + _dot(_bf(a), vc[j].reshape(HEAD_COLS, MIX_W))
            kept = runs_ref[...]
            for h in range(N_HEADS):
                kept = jnp.where(_run_lane(j, h), runs[h], kept)
            runs_ref[...] = kept
            return tuple(ru + cs for ru, cs in zip(runs, _head_sums(c))), acc

        def trip(last, carry, on_diagonal):
            runs, acc = carry
            for sub in range(r):
                runs, acc = one_block(last - sub, runs, acc, on_diagonal)
            return runs, acc

        zero = jnp.zeros((tq, 1), _F32)
        carry = trip(nkb - 1, ((zero,) * N_HEADS, jnp.zeros((tq, MIX_W), _F32)), True)
        below = lambda m: nkb - 1 - (m + 1) * r
        carry = lax.fori_loop(0, i // 2, lambda m, carry: trip(below(2 * m + 1), trip(below(2 * m), carry, False), False), carry)
        _, acc = lax.fori_loop(0, i % 2, lambda m, carry: trip(below(i - 1), carry, False), carry)
        y_ref[...] = _bf(acc)

    return pl.pallas_call(
        body, name=name, grid=(n_ex, nq),
        in_specs=[pl.BlockSpec((tq, MIX_W), lambda e, i: (e * nq + i, 7)), pl.BlockSpec((s, MIX_W), lambda e, i: (e, 8)),
                  pl.BlockSpec((s, MIX_W), lambda e, i: (e, 9))],
        out_specs=[pl.BlockSpec((tq, MIX_W), lambda e, i: (e * nq + i, 0)),
                   pl.BlockSpec((tq, RUN_LANES), lambda e, i: (e * nq + i, 0))],
        out_shape=[jax.ShapeDtypeStruct((t, MIX_W), _BF), jax.ShapeDtypeStruct((t, RUN_LANES), _F32)],
        scratch_shapes=[pltpu.VMEM((nb, N_HEADS, KV_BLOCK, MIX_W), _BF), pltpu.VMEM((nb, N_HEADS, KV_BLOCK, MIX_W), _BF)],
        compiler_params=_params("parallel", "arbitrary"),
    )(p, p, p)


def _mix_d_bwd(p, kept_runs, dmix, n_ex, name):
    t = p.shape[0]
    s = t // n_ex
    tq = _d_qblock(s)
    nq = s // tq
    r = tq // KV_BLOCK
    nb = s // KV_BLOCK
    scale = HEAD_DIM ** -0.5

    def body(q_ref, k_ref, v_ref, runs_ref, do_ref, dq_ref, dk_ref, dv_ref, kc, vc):
        i = pl.program_id(1)

        @pl.when(i == 0)
        def _():
            kc[...] = _stack_heads(k_ref[...].reshape(nb, KV_BLOCK, MIX_W))
            vc[...] = _stack_heads(v_ref[...].reshape(nb, KV_BLOCK, MIX_W))
            dk_ref[...] = jnp.zeros_like(dk_ref)
            dv_ref[...] = jnp.zeros_like(dv_ref)

        q_scaled = _ld(q_ref) * scale
        qs = _bf(q_scaled)
        do = do_ref[...]
        dob = _bf(do)
        q_rows = _stack_heads_rows(q_scaled)
        do_rows = _stack_heads_rows(do)
        kept = runs_ref[...]
        t_idx = i * tq + lax.broadcasted_iota(jnp.int32, (tq, 1), 0)
        after_m = _tri2(True)
        before_m = _tri2(False)
        nkb = (i + 1) * r
        zero = jnp.zeros((tq, 1), _F32)

        def trip(first, carry, on_diagonal):
            for sub in range(r):
                carry = one_block(first + sub, carry, on_diagonal)
            return carry

        def one_block(j, carry, on_diagonal):
            pres, dq = carry
            rows = pl.ds(pl.multiple_of(j * KV_BLOCK, KV_BLOCK), KV_BLOCK)
            kj = kc[j].reshape(HEAD_COLS, MIX_W)
            keep, lb, c = _sb_scores(qs, kj, j, t_idx, on_diagonal)
            runs = [jnp.sum(jnp.where(_run_lane(j, h), kept, 0.0), axis=1, keepdims=True) for h in range(N_HEADS)]
            a = keep(jnp.exp(lb + _pair_dot(c, after_m) + _spread(runs)))
            g = a * _dot_nt(dob, vc[j].reshape(HEAD_COLS, MIX_W))
            before = _pair_dot(g, before_m) + _spread(pres)
            sig = jnp.exp(lb)
            dz = _bf(keep(g * (1.0 - sig) - sig * before))
            dk_ref[rows, :] += _dot_tn(_cols_to_rows(dz), q_rows)
            dv_ref[rows, :] += _dot_tn(_cols_to_rows(_bf(a)), do_rows)
            return tuple(pr + gs for pr, gs in zip(pres, _head_sums(g))), dq + _dot(dz, kj)

        init = ((zero,) * N_HEADS, jnp.zeros((tq, MIX_W), _F32))
        carry = lax.fori_loop(0, i // 2, lambda m, carry: trip((2 * m + 1) * r, trip(2 * m * r, carry, False), False), init)
        carry = lax.fori_loop(0, i % 2, lambda m, carry: trip((i - 1) * r, carry, False), carry)
        _, dq = trip(i * r, carry, True)
        dq_ref[...] = _bf(dq * scale)

    return pl.pallas_call(
        body, name=name, grid=(n_ex, nq),
        in_specs=[pl.BlockSpec((tq, MIX_W), lambda e, i: (e * nq + i, 7)), pl.BlockSpec((s, MIX_W), lambda e, i: (e, 8)),
                  pl.BlockSpec((s, MIX_W), lambda e, i: (e, 9)), pl.BlockSpec((tq, RUN_LANES), lambda e, i: (e * nq + i, 0)),
                  pl.BlockSpec((tq, MIX_W), lambda e, i: (e * nq + i, 3))],
        out_specs=[pl.BlockSpec((tq, MIX_W), lambda e, i: (e * nq + i, 0)), pl.BlockSpec((s, MIX_W), lambda e, i: (e, 0)),
                   pl.BlockSpec((s, MIX_W), lambda e, i: (e, 0))],
        out_shape=[jax.ShapeDtypeStruct((t, MIX_W), _BF), jax.ShapeDtypeStruct((t, MIX_W), _F32),
                   jax.ShapeDtypeStruct((t, MIX_W), _F32)],
        scratch_shapes=[pltpu.VMEM((nb, N_HEADS, KV_BLOCK, MIX_W), _BF), pltpu.VMEM((nb, N_HEADS, KV_BLOCK, MIX_W), _BF)],
        compiler_params=_params("parallel", "arbitrary"),
    )(p, p, p, kept_runs, dmix)


def _fwd_mix(x, w, l, n_ex):
    p, h1 = _norm_mm(x, w["norm1_g"][l], w["w_in_t"][l], "in_proj")
    y_a = _mix_a_fwd(p, w["conv_a_w"][l], n_ex, "mix_a_fwd")
    y_b, cb = _mix_b_fwd(p, w["conv_b_w"][l], w["conv_b_b"][l], w["ln_b_g"][l], w["ln_b_b"][l], n_ex, "mix_b_fwd")
    y_c = _mix_c_fwd(p, w["ln_c_g"][l], w["ln_c_b"][l], w["sgu_w"][l], w["sgu_b_full"][l], n_ex, "mix_c_fwd")
    y_d, runs_d = _mix_d_fwd(p, n_ex, "mix_d_fwd")
    return dict(x=x, h1=h1, p=p, cb=cb, runs_d=runs_d, mix=(y_a, y_b, y_c, y_d))


def _fwd_ffn(st, w, l, n_ex):
    x1 = _mm_res(st["mix"], w["w_out"][l], st["x"], "out_proj")
    up_pre, h2 = _norm_mm(x1, w["norm2_g"][l], w["w_up_t"][l], "up_proj")
    act, conv_g, conv_v = _ffn_mid_fwd(up_pre, w["conv_f_w"][l], n_ex, "ffn_mid_fwd")
    st.update(x1=x1, h2=h2, up_pre=up_pre, act=act, conv_g=conv_g, conv_v=conv_v)
    return _mm_res((act,), w["w_down"][l], x1, "down_proj")


def _bwd_ffn(st, w, l, dx, dxb, n_ex):
    g = {}
    dact = _mm_nt(dxb, w["w_down"][l], "down_proj_dx")
    g["w_down"] = _mm_tn(st["act"], dxb, "down_proj_dw", _BF)
    dup_g, dup_v, dwf_g, dwf_v = _ffn_mid_bwd(
        st["up_pre"], st["conv_g"], st["conv_v"], w["conv_f_w"][l], dact, n_ex, "ffn_mid_bwd")
    g["conv_f_w"] = jnp.concatenate([dwf_g, dwf_v], axis=1)
    dx, dxb, g["norm2_g"] = _mm_normbwd((dup_g, dup_v), w["w_up_t"][l], st["x1"], w["norm2_g"][l], dx, "up_proj_dx")
    g["w_up_t"] = _mm_tn_halves(dup_g, dup_v, st["h2"], "up_proj_dw")
    return dx, dxb, g


def _bwd_out_proj(st, w, l, dxb):
    return _mm_nt(dxb, w["w_out"][l], "out_proj_dx"), _mm_tn_parts(st["mix"], dxb, "out_proj_dw")


def _bwd_mixers(st, w, l, dx, dmix, n_ex):
    g = {}
    p = st["p"]
    dp_a, g["conv_a_w"] = _mix_a_bwd(p, w["conv_a_w"][l], dmix, n_ex, "mix_a_bwd")
    dp_b, g["conv_b_w"], g["conv_b_b"], g["ln_b_g"], g["ln_b_b"] = _mix_b_bwd(
        p, st["cb"], w["conv_b_w"][l], w["ln_b_g"][l], w["ln_b_b"][l], dmix, n_ex, "mix_b_bwd")
    dp_c, g["ln_c_g"], g["ln_c_b"], g["sgu_w"], g["sgu_b_t"] = _mix_c_bwd(
        p, w["ln_c_g"][l], w["ln_c_b"][l], w["sgu_w"][l], w["sgu_b_full"][l], dmix, n_ex, "mix_c_bwd")
    dq, dk, dv = _mix_d_bwd(p, st["runs_d"], dmix, n_ex, "mix_d_bwd")
    dp = (dp_a, dp_b, dp_c, dq, dk, dv)
    dx, dxb, g["norm1_g"] = _mm_normbwd(dp, w["w_in_t"][l], st["x"], w["norm1_g"][l], dx, "in_proj_dx")
    return dx, dxb, g, dp


def _bwd_mix(st, w, l, dx, dxb, n_ex):
    dmix, dw_out = _bwd_out_proj(st, w, l, dxb)
    dx, dxb, g, dp = _bwd_mixers(st, w, l, dx, dmix, n_ex)
    g["w_out"] = dw_out
    g["w_in_t"] = _mm_tn_parts(dp, st["h1"], "in_proj_dw")
    return dx, dxb, g


_MESH = pl.DeviceIdType.MESH
_ANY = pl.BlockSpec(memory_space=pl.ANY)


def _position():
    return lax.axis_index("x"), lax.axis_index("y"), lax.axis_index("c")


def _flat(px, py, pc):
    return 4 * px + 2 * py + pc


def _all_gather(shard, name, after):
    r, c_ = shard.shape

    def body(x_ref, after_ref, out_ref, send_sems, recv_sems, local_sem):
        x, y, c = _position()
        me, sibling = (x, y, c), (x, y, 1 - c)
        chips = [(1 - x, y), (x, 1 - y), (1 - x, 1 - y)]

        def copy(k, block, to, src=None):
            slab = out_ref.at[_flat(*block)]
            return pltpu.make_async_remote_copy(
                src_ref=slab if src is None else src, dst_ref=slab, send_sem=send_sems.at[k], recv_sem=recv_sems.at[k],
                device_id=to, device_id_type=_MESH)

        mine = pltpu.make_async_copy(x_ref, out_ref.at[_flat(*me)], local_sem)
        mine.start()
        first = [copy(0, me, sibling, src=x_ref)]
        first += [copy(1 + j, me, (*chip, c), src=x_ref) for j, chip in enumerate(chips)]
        for cp in first:
            cp.start()
        passed = [copy(4 + j, (*chip, c), sibling) for j, chip in enumerate(chips)]
        for j, chip in enumerate(chips):
            copy(1 + j, (*chip, c), me).wait_recv()
            passed[j].start()
        copy(0, sibling, me).wait_recv()
        for j, chip in enumerate(chips):
            copy(4 + j, (*chip, 1 - c), me).wait_recv()
        for cp in first + passed:
            cp.wait_send()
        mine.wait()

    return pl.pallas_call(
        body, name=name, out_shape=jax.ShapeDtypeStruct((N_DEV, r, c_), shard.dtype),
        in_specs=[_ANY, _ANY], out_specs=_ANY,
        scratch_shapes=[pltpu.SemaphoreType.DMA((7,)), pltpu.SemaphoreType.DMA((7,)), pltpu.SemaphoreType.DMA],
    )(shard, after)


_HBM = pl.BlockSpec(memory_space=pltpu.HBM)
_SEM = pl.BlockSpec(memory_space=pltpu.SEMAPHORE)
_DATAFLOW = pltpu.SideEffectType.DATAFLOW_SIDE_EFFECTING


def _peers(x, y, c):
    return [((1 - x) if (k + 1) & 4 else x, (1 - y) if (k + 1) & 2 else y, (1 - c) if (k + 1) & 1 else c)
            for k in range(N_DEV - 1)]


def _direct_copies(src_refs, land_refs, send_sems, recv_sems, to_all):
    x, y, c = _position()
    my = _flat(x, y, c)
    out, back = [], []
    for m, (src_ref, land_ref) in enumerate(zip(src_refs, land_refs)):
        for k, peer in enumerate(_peers(x, y, c)):
            src = src_ref if to_all else src_ref.at[_flat(*peer)]
            n = m * (N_DEV - 1) + k
            sems = dict(send_sem=send_sems.at[n], recv_sem=recv_sems.at[n], device_id=peer, device_id_type=_MESH)
            out.append(pltpu.make_async_remote_copy(src_ref=src, dst_ref=land_ref.at[my], **sems))
            back.append(pltpu.make_async_remote_copy(src_ref=src, dst_ref=land_ref.at[_flat(*peer)], **sems))
    return out, back


def _exchange_start(srcs, to_all, after, name):
    n = len(srcs)
    n_sems = n * (N_DEV - 1)
    land_shapes = [(N_DEV,) + tuple(a.shape[-2:]) for a in srcs]

    def body(*refs):
        src_refs, land_refs = refs[:n], refs[n:2 * n]
        send_sems, recv_sems = refs[2 * n + 1], refs[2 * n + 2]
        token = refs[-1]
        for cp in _direct_copies(src_refs, land_refs, send_sems, recv_sems, to_all)[0]:
            cp.start()
        token[...] = jnp.zeros_like(token)

    lands = [pltpu.with_memory_space_constraint(lax.empty(shp, a.dtype), pltpu.HBM) for shp, a in zip(land_shapes, srcs)]
    outs = pl.pallas_call(
        body, name=name,
        out_shape=(pltpu.SemaphoreType.DMA((n_sems,)), pltpu.SemaphoreType.DMA((n_sems,)),
                   *[pltpu.HBM(a.shape, a.dtype) for a in srcs], *[pltpu.HBM(shp, a.dtype) for shp, a in zip(land_shapes, srcs)],
                   jax.ShapeDtypeStruct((8, 128), _F32)),
        in_specs=(_HBM,) * (2 * n) + (_ANY,),
        out_specs=(_SEM, _SEM) + (_HBM,) * (2 * n) + (pl.BlockSpec(memory_space=pltpu.VMEM),),
        input_output_aliases={i: 2 + i for i in range(2 * n)},
        compiler_params=pltpu.CompilerParams(has_side_effects=_DATAFLOW),
    )(*[pltpu.with_memory_space_constraint(a, pltpu.HBM) for a in srcs], *lands, after)
    return (outs[0], outs[1], outs[2:2 + n], outs[2 + n:2 + 2 * n], to_all), outs[-1]


def _exchange_wait(handle, after, name):
    send_sems, recv_sems, srcs, lands, to_all = handle
    n = len(srcs)

    def body(*refs):
        out, back = _direct_copies(refs[:n], refs[n:2 * n], refs[2 * n], refs[2 * n + 1], to_all)
        for cp in out:
            cp.wait_send()
        for cp in back:
            cp.wait_recv()
        my = _flat(*_position())
        for src_ref, land_ref in zip(refs[:n], refs[n:2 * n]):
            pltpu.sync_copy(src_ref if to_all else src_ref.at[my], land_ref.at[my])

    outs = pl.pallas_call(
        body, name=name,
        out_shape=tuple(pltpu.HBM(a.shape, a.dtype) for a in (*srcs, *lands)),
        in_specs=(_HBM,) * (2 * n) + (_SEM, _SEM, _ANY), out_specs=(_HBM,) * (2 * n),
        input_output_aliases={i: i for i in range(2 * n)},
        compiler_params=pltpu.CompilerParams(has_side_effects=_DATAFLOW),
    )(*srcs, *lands, send_sems, recv_sems, after)
    return outs[n:]


def _sum_slabs(slabs, name):
    n, r, c_ = slabs.shape
    tr = _pick_tile(r, 16, max(16, (12 << 20) // (n * c_ * slabs.dtype.itemsize)))

    def body(x_ref, o_ref):
        acc = x_ref[0].astype(_F32)
        for k in range(1, n):
            acc = acc + x_ref[k].astype(_F32)
        o_ref[...] = acc

    return pl.pallas_call(
        body, name=name, grid=(r // tr,),
        in_specs=[pl.BlockSpec((n, tr, c_), lambda i: (0, i, 0))],
        out_specs=pl.BlockSpec((tr, c_), lambda i: (i, 0)),
        out_shape=jax.ShapeDtypeStruct((r, c_), _F32),
        compiler_params=_params("parallel"),
    )(slabs)


def _adamw(w, g, m, v, name):
    r, c_ = w.shape
    tr = _pick_tile(r, 8, 512)

    def body(w_ref, g_ref, m_ref, v_ref, d_ref, nm_ref, nv_ref):
        _adamw_refs(w_ref, g_ref, m_ref, v_ref, d_ref, nm_ref, nv_ref)

    spec = pl.BlockSpec((tr, c_), lambda i: (i, 0))
    shape = jax.ShapeDtypeStruct((r, c_), _F32)
    return pl.pallas_call(
        body, name=name, grid=(r // tr,), in_specs=[spec] * 4, out_specs=[spec] * 3, out_shape=[shape] * 3,
        compiler_params=_params("parallel"),
    )(w, g, m, v)


def _adamw_refs(w_ref, g_ref, m_ref, v_ref, d_ref, nm_ref, nv_ref):
    gv = g_ref[...]
    nm = ADAM_B1 * m_ref[...] + (1.0 - ADAM_B1) * gv
    nv = ADAM_B2 * v_ref[...] + (1.0 - ADAM_B2) * (gv * gv)
    m_hat = nm / (1.0 - ADAM_B1 ** ADAM_STEP)
    v_hat = nv / (1.0 - ADAM_B2 ** ADAM_STEP)
    d_ref[...] = -ADAM_LR * (m_hat / (jnp.sqrt(v_hat) + ADAM_EPS) + ADAM_WD * w_ref[...])
    nm_ref[...] = nm
    nv_ref[...] = nv


def _adamw_small(params, name):
    n = len(params)

    def body(*refs):
        for i in range(n):
            _adamw_refs(*refs[4 * i:4 * i + 4], *refs[4 * n + 3 * i:4 * n + 3 * i + 3])

    outs = pl.pallas_call(
        body, name=name,
        out_shape=[jax.ShapeDtypeStruct(p[0].shape, _F32) for p in params for _ in range(3)],
        compiler_params=pltpu.CompilerParams(vmem_limit_bytes=VMEM_LIMIT),
    )(*[a for p in params for a in p])
    return [tuple(outs[3 * i:3 * i + 3]) for i in range(n)]


_SMALL = ("norm1_g", "conv_a_w", "conv_b_w", "conv_b_b", "ln_b_g", "ln_b_b", "ln_c_g", "ln_c_b", "sgu_w", "sgu_b",
          "norm2_g", "conv_f_w", "final_g")
_CONV_SHARDED = ("conv_a_w", "conv_b_w", "conv_f_w")
_NAMES = ("norm1_g", "w_in", "conv_a_w", "conv_b_w", "conv_b_b", "ln_b_g", "ln_b_b", "ln_c_g", "ln_c_b", "sgu_w", "sgu_b",
          "w_out", "norm2_g", "w_up", "conv_f_w", "w_down", "final_g")


def _pack_rows(parts, lanes=128, row_multiple=8):
    flat = jnp.concatenate([a.reshape(-1) for a in parts])
    rows = -(-flat.shape[0] // lanes)
    rows = -(-rows // row_multiple) * row_multiple
    return jnp.pad(flat, (0, rows * lanes - flat.shape[0])).reshape(rows, lanes)


def _unpack_rows(packed, shapes):
    flat = packed.reshape(-1)
    out, off = [], 0
    for shp in shapes:
        size = 1
        for s in shp:
            size *= s
        out.append(flat[off:off + size].reshape(shp))
        off += size
    return out


def _gather_conv_weights(conv_a_w, conv_b_w, conv_f_w, after):
    shards = (conv_a_w, conv_b_w, conv_f_w)
    flat = _all_gather(_pack_rows(shards), "gather_conv_weights", after).reshape(N_DEV, -1)
    full, off = [], 0
    for s in shards:
        layers, taps, width = s.shape
        per_dev = flat[:, off:off + s.size].reshape(N_DEV, layers, taps, width)
        full.append(jnp.moveaxis(per_dev, 0, 2).reshape(layers, taps, N_DEV * width))
        off += s.size
    return full


def kernel(x, norm1_g, w_in, conv_a_w, conv_b_w, conv_b_b, ln_b_g, ln_b_b, ln_c_g, ln_c_b, sgu_w, sgu_b, w_out, norm2_g, w_up, conv_f_w, w_down, final_g, loss_target, m_norm1_g, m_w_in, m_conv_a_w, m_conv_b_w, m_conv_b_b, m_ln_b_g, m_ln_b_b, m_ln_c_g, m_ln_c_b, m_sgu_w, m_sgu_b, m_w_out, m_norm2_g, m_w_up, m_conv_f_w, m_w_down, m_final_g, v_norm1_g, v_w_in, v_conv_a_w, v_conv_b_w, v_conv_b_b, v_ln_b_g, v_ln_b_b, v_ln_c_g, v_ln_c_b, v_sgu_w, v_sgu_b, v_w_out, v_norm2_g, v_w_up, v_conv_f_w, v_w_down, v_final_g):
    weights = dict(norm1_g=norm1_g, w_in=w_in, conv_a_w=conv_a_w, conv_b_w=conv_b_w, conv_b_b=conv_b_b, ln_b_g=ln_b_g,
                   ln_b_b=ln_b_b, ln_c_g=ln_c_g, ln_c_b=ln_c_b, sgu_w=sgu_w, sgu_b=sgu_b, w_out=w_out, norm2_g=norm2_g,
                   w_up=w_up, conv_f_w=conv_f_w, w_down=w_down, final_g=final_g)
    mom1 = dict(norm1_g=m_norm1_g, w_in=m_w_in, conv_a_w=m_conv_a_w, conv_b_w=m_conv_b_w, conv_b_b=m_conv_b_b,
                ln_b_g=m_ln_b_g, ln_b_b=m_ln_b_b, ln_c_g=m_ln_c_g, ln_c_b=m_ln_c_b, sgu_w=m_sgu_w, sgu_b=m_sgu_b,
                w_out=m_w_out, norm2_g=m_norm2_g, w_up=m_w_up, conv_f_w=m_conv_f_w, w_down=m_w_down, final_g=m_final_g)
    mom2 = dict(norm1_g=v_norm1_g, w_in=v_w_in, conv_a_w=v_conv_a_w, conv_b_w=v_conv_b_w, conv_b_b=v_conv_b_b,
                ln_b_g=v_ln_b_g, ln_b_b=v_ln_b_b, ln_c_g=v_ln_c_g, ln_c_b=v_ln_c_b, sgu_w=v_sgu_w, sgu_b=v_sgu_b,
                w_out=v_w_out, norm2_g=v_norm2_g, w_up=v_w_up, conv_f_w=v_conv_f_w, w_down=v_w_down, final_g=v_final_g)
    n_ex, seq, d = x.shape
    depth = w_in.shape[0]
    assert depth == 2
    my = _flat(*_position())
    row = lambda a, l: a[l][None]
    tied = lambda a, token: a + token[0:1, 0:1]

    slab = {"w_in": [_bf(jnp.swapaxes(w_in, 1, 2)[l]) for l in range(depth)], "w_out": [_bf(w_out[l]) for l in range(depth)],
            "w_up": [_bf(jnp.swapaxes(w_up, 1, 2)[l]) for l in range(depth)], "w_down": [_bf(w_down[l]) for l in range(depth)]}
    rows = {name: parts[0].shape[0] for name, parts in slab.items()}
    key_of = {"w_in": "w_in_t", "w_out": "w_out", "w_up": "w_up_t", "w_down": "w_down"}
    rest_layer0 = [("w_out", 0), ("w_up", 0), ("w_down", 0)]
    all_layer1 = [("w_in", 1), ("w_out", 1), ("w_up", 1), ("w_down", 1)]

    w_in0 = _all_gather(slab["w_in"][0], "gather_w_in0", norm1_g)
    conv_a_full, conv_b_full, conv_f_full = _gather_conv_weights(conv_a_w, conv_b_w, conv_f_w, w_in0)
    gather0, token = _exchange_start([slab[n][l] for n, l in rest_layer0], True, conv_f_full, "gather_layer0_start")
    w = {
        "norm1_g": [row(norm1_g, l) for l in range(depth)], "w_in_t": [None] * depth,
        "conv_a_w": [conv_a_full[l] for l in range(depth)], "conv_b_w": [conv_b_full[l] for l in range(depth)],
        "conv_b_b": [row(conv_b_b, l) for l in range(depth)], "ln_b_g": [row(ln_b_g, l) for l in range(depth)],
        "ln_b_b": [row(ln_b_b, l) for l in range(depth)], "ln_c_g": [row(ln_c_g, l) for l in range(depth)],
        "ln_c_b": [row(ln_c_b, l) for l in range(depth)], "sgu_w": [sgu_w[l] for l in range(depth)],
        "sgu_b_full": [jnp.repeat(sgu_b[l].T, HEAD_DIM, axis=1) for l in range(depth)],
        "w_out": [None] * depth, "norm2_g": [row(norm2_g, l) for l in range(depth)], "w_up_t": [None] * depth,
        "conv_f_w": [conv_f_full[l] for l in range(depth)], "w_down": [None] * depth, "final_g": final_g[None],
    }
    w["w_in_t"][0] = w_in0.reshape(N_DEV * rows["w_in"], d)
    w["norm1_g"][0] = tied(row(norm1_g, 0), token)

    def land_weights(handle, after, which, name):
        landed = _exchange_wait(handle, after, name)
        for (n, l), got in zip(which, landed):
            w[key_of[n]][l] = got.reshape(N_DEV * rows[n], d)
        return landed[0]

    st0 = _fwd_mix(x.reshape(n_ex * seq, d), w, 0, n_ex)
    landed0 = land_weights(gather0, st0["mix"][3], rest_layer0, "gather_layer0_wait")
    gather1, token = _exchange_start([slab[n][l] for n, l in all_layer1], True, landed0, "gather_layer1_start")
    w["norm2_g"][0] = tied(row(norm2_g, 0), token)
    x_mid = _fwd_ffn(st0, w, 0, n_ex)
    land_weights(gather1, x_mid, all_layer1, "gather_layer1_wait")
    st1 = _fwd_mix(x_mid, w, 1, n_ex)
    x_out = _fwd_ffn(st1, w, 1, n_ex)
    dx, dxb, d_final_g, loss = _final_loss(x_out, w["final_g"], loss_target.reshape(n_ex * seq, d), "final_loss")
    loss = lax.psum(loss[0, 0], ("x", "y", "c"))

    def send_grads(g, which, after, name):
        return _exchange_start([g[key_of[n]].reshape(N_DEV, rows[n], d) for n, _ in which], False, after, name)

    dx, dxb, g_ffn1 = _bwd_ffn(st1, w, 1, dx, dxb, n_ex)
    dx, dxb, g_mix1 = _bwd_mix(st1, w, 1, dx, dxb, n_ex)
    grads1, token = send_grads({**g_ffn1, **g_mix1}, all_layer1, dx, "exchange_layer1_start")
    w["norm2_g"][0] = tied(row(norm2_g, 0), token)
    dx, dxb, g_ffn0 = _bwd_ffn(st0, w, 0, dx, dxb, n_ex)
    g_ffn0["w_out"] = _mm_tn_parts(st0["mix"], dxb, "out_proj_dw")
    ffn_layer0 = [("w_out", 0), ("w_up", 0), ("w_down", 0)]
    grads0a, token = send_grads(g_ffn0, ffn_layer0, dxb, "exchange_ffn0_start")
    dmix = _mm_nt(dxb, w["w_out"][0], "out_proj_dx", after=token)
    dx, dxb, g_mix0, dp0 = _bwd_mixers(st0, w, 0, dx, dmix, n_ex)
    grad_x = dx.reshape(n_ex, seq, d)
    g = {k: [{**g_ffn0, **g_mix0}[k], {**g_ffn1, **g_mix1}[k]] for k in g_mix0.keys() | g_ffn0.keys()}
    g["final_g"] = d_final_g

    small_local = {
        "norm1_g": jnp.stack([a[0] for a in g["norm1_g"]]), "conv_a_w": jnp.stack(g["conv_a_w"]),
        "conv_b_w": jnp.stack(g["conv_b_w"]), "conv_b_b": jnp.stack([a[0] for a in g["conv_b_b"]]),
        "ln_b_g": jnp.stack([a[0] for a in g["ln_b_g"]]), "ln_b_b": jnp.stack([a[0] for a in g["ln_b_b"]]),
        "ln_c_g": jnp.stack([a[0] for a in g["ln_c_g"]]), "ln_c_b": jnp.stack([a[0] for a in g["ln_c_b"]]),
        "sgu_w": jnp.stack(g["sgu_w"]), "sgu_b": jnp.stack([a.T for a in g["sgu_b_t"]]),
        "norm2_g": jnp.stack([a[0] for a in g["norm2_g"]]), "conv_f_w": jnp.stack(g["conv_f_w"]),
        "final_g": g["final_g"][0],
    }
    small, token = _exchange_start([_pack_rows([small_local[k] for k in _SMALL])], True, dx, "gather_small_start")
    g_mix0["w_in_t"] = _mm_tn_parts(dp0, st0["h1"], "in_proj_dw", after=token)
    mix_layer0 = [("w_in", 0)]
    grads0b, token = send_grads(g_mix0, mix_layer0, dx, "exchange_mix0_start")

    reduced = {}

    def land_grads(handle, after, which, name):
        for (n, l), got in zip(which, _exchange_wait(handle, after, name + "_wait")):
            reduced[(n, l)] = _sum_slabs(got, name + "_sum_" + n)
        return reduced[which[-1]]

    def stacked_grad(name):
        stacked = jnp.stack([reduced[(name, l)] for l in range(depth)])
        return jnp.swapaxes(stacked, 1, 2) if name in ("w_in", "w_up") else stacked

    done = land_grads(grads1, token, all_layer1, "exchange_layer1")
    land_grads(grads0a, done, ffn_layer0, "exchange_ffn0")
    grads = {name: stacked_grad(name) for name in ("w_out", "w_up", "w_down")}

    delta, new_m, new_v = {}, {}, {}

    def as_2d(name):
        shp = weights[name].shape
        two_d = (-1, shp[-1]) if len(shp) > 1 else (1, shp[0])
        return tuple(a.reshape(two_d) for a in (weights[name], grads[name], mom1[name], mom2[name]))

    def keep(name, outs):
        delta[name], new_m[name], new_v[name] = (o.reshape(weights[name].shape) for o in outs)

    for name in ("w_up", "w_down", "w_out"):
        keep(name, _adamw(*as_2d(name), "adamw_" + name))

    small_sum = _sum_slabs(_exchange_wait(small, new_v["w_out"], "gather_small_wait")[0], "sum_small_grads")
    for name, total in zip(_SMALL, _unpack_rows(small_sum, [small_local[k].shape for k in _SMALL])):
        if name in _CONV_SHARDED:
            width = weights[name].shape[-1]
            total = lax.dynamic_slice_in_dim(total, my * width, width, axis=-1)
        grads[name] = total
    at_least_2d = lambda a: a[None] if a.ndim == 1 else a
    small_params = [tuple(at_least_2d(a) for a in (weights[n], grads[n], mom1[n], mom2[n])) for n in _SMALL]
    for name, outs in zip(_SMALL, _adamw_small(small_params, "adamw_small")):
        keep(name, outs)

    land_grads(grads0b, new_v["final_g"], mix_layer0, "exchange_mix0")
    grads["w_in"] = stacked_grad("w_in")
    keep("w_in", _adamw(*as_2d("w_in"), "adamw_w_in"))

    return (loss, grad_x, *[grads[n] for n in _NAMES], *[delta[n] for n in _NAMES], *[new_m[n] for n in _NAMES],
            *[new_v[n] for n in _NAMES])
```

```python
import jax
import jax.numpy as jnp
from jax import lax
from jax.experimental import pallas as pl
from jax.experimental.pallas import tpu as pltpu

_F32 = jnp.float32
_BF = jnp.bfloat16

HEAD_DIM = 64
MIX_W = 256
N_HEADS = MIX_W // HEAD_DIM
CHUNK = 128
KV_BLOCK = 128
K_SHORT = 3
K_CONF = 31
K_FFN = 3
RMS_EPS = 1e-6
LN_EPS = 1e-5
ADAM_LR = 0.001
ADAM_B1 = 0.9
ADAM_B2 = 0.999
ADAM_EPS = 1e-08
ADAM_WD = 0.01
ADAM_STEP = 10
N_DEV = 8
VMEM_LIMIT = 56 * 1024 * 1024


def _bf(x):
    return x.astype(_BF)


def _ld(ref):
    return ref[...].astype(_F32)


_ANY_SPEC = pl.BlockSpec(memory_space=pl.ANY)


def _params(*sem):
    return pltpu.CompilerParams(dimension_semantics=sem, vmem_limit_bytes=VMEM_LIMIT)


def _dot(a, b):
    return jnp.dot(a, b, preferred_element_type=_F32)


def _dot_nt(a, b):
    return lax.dot_general(a, b, (((1,), (1,)), ((), ())), preferred_element_type=_F32)


def _dot_tn(a, b):
    return lax.dot_general(a, b, (((0,), (0,)), ((), ())), preferred_element_type=_F32)


def _row_tile(t, want):
    return want if t % want == 0 else t


def _pick_tile(rows, unit, max_rows):
    best = 0
    for cand in range(unit, min(rows, max_rows) + 1, unit):
        if rows % cand == 0:
            best = cand
    return best or rows


def _sigmoid(x):
    return 1.0 / (1.0 + jnp.exp(-x))


def _rms_rstd(x):
    return lax.rsqrt(jnp.mean(x * x, axis=-1, keepdims=True) + RMS_EPS)


def _norm_mm(x, g, w_t, name):
    t, d = x.shape
    n = w_t.shape[0]
    tm = _row_tile(t, 512)
    tn = _row_tile(n, 512)

    def body(x_ref, g_ref, w_ref, p_ref, h_ref):
        xv = x_ref[...]
        h = _bf(xv * _rms_rstd(xv) * g_ref[...])
        h_ref[...] = h
        for n0 in range(0, n, tn):
            p_ref[:, n0:n0 + tn] = _bf(_dot_nt(h, w_ref[n0:n0 + tn, :]))

    return pl.pallas_call(
        body, name=name, grid=(t // tm,),
        in_specs=[pl.BlockSpec((tm, d), lambda i: (i, 0)), pl.BlockSpec((1, d), lambda i: (0, 0)),
                  pl.BlockSpec((n, d), lambda i: (0, 0))],
        out_specs=[pl.BlockSpec((tm, n), lambda i: (i, 0)), pl.BlockSpec((tm, d), lambda i: (i, 0))],
        out_shape=[jax.ShapeDtypeStruct((t, n), _BF), jax.ShapeDtypeStruct((t, d), _BF)],
        compiler_params=_params("parallel"),
    )(x, g, w_t)


def _mm_nt(a, w_t, name, after=None):
    t, k = a.shape
    n = w_t.shape[0]
    tm = _row_tile(t, 512)
    tn = _row_tile(n, 512) if n % 512 == 0 else _row_tile(n, 256)

    def body(a_ref, w_ref, *rest):
        o_ref = rest[-1]
        av = a_ref[...]
        for n0 in range(0, n, tn):
            o_ref[:, n0:n0 + tn] = _bf(_dot_nt(av, w_ref[n0:n0 + tn, :]))

    extra = () if after is None else (after,)
    return pl.pallas_call(
        body, name=name, grid=(t // tm,),
        in_specs=[pl.BlockSpec((tm, k), lambda i: (i, 0)), pl.BlockSpec((n, k), lambda i: (0, 0))] + [_ANY_SPEC] * len(extra),
        out_specs=pl.BlockSpec((tm, n), lambda i: (i, 0)),
        out_shape=jax.ShapeDtypeStruct((t, n), _BF),
        compiler_params=_params("parallel"),
    )(a, w_t, *extra)


def _mm_res(parts, w, x, name):
    t = x.shape[0]
    k, d = w.shape
    tm = _row_tile(t, 512)
    widths = [a.shape[1] for a in parts]
    n_parts = len(parts)

    def body(*refs):
        w_ref, x_ref, o_ref = refs[n_parts:]
        acc, off = x_ref[...], 0
        for a_ref, width in zip(refs[:n_parts], widths):
            acc = acc + _dot(a_ref[...], w_ref[off:off + width, :])
            off += width
        o_ref[...] = acc

    return pl.pallas_call(
        body, name=name, grid=(t // tm,),
        in_specs=[pl.BlockSpec((tm, width), lambda i: (i, 0)) for width in widths] + [
            pl.BlockSpec((k, d), lambda i: (0, 0)), pl.BlockSpec((tm, d), lambda i: (i, 0))],
        out_specs=pl.BlockSpec((tm, d), lambda i: (i, 0)),
        out_shape=jax.ShapeDtypeStruct((t, d), _F32),
        compiler_params=_params("parallel"),
    )(*parts, w, x)


def _mm_normbwd(parts, w, x, g, dres, name):
    t = x.shape[0]
    k, d = w.shape
    tm = _row_tile(t, 512)
    widths = [a.shape[1] for a in parts]
    n_parts = len(parts)

    def body(*refs):
        a_refs = refs[:n_parts]
        w_ref, x_ref, g_ref, r_ref, dx_ref, dxb_ref, dg_ref = refs[n_parts:]
        dh, off = None, 0
        for a_ref, width in zip(a_refs, widths):
            term = _dot(_bf(a_ref[...]), w_ref[off:off + width, :])
            dh = term if dh is None else dh + term
            off += width
        xv = x_ref[...]
        rstd = _rms_rstd(xv)
        xn = xv * rstd
        u = dh * g_ref[...]
        dx = r_ref[...] + rstd * (u - xn * jnp.mean(u * xn, axis=-1, keepdims=True))
        dx_ref[...] = dx
        dxb_ref[...] = _bf(dx)

        @pl.when(pl.program_id(0) == 0)
        def _():
            dg_ref[...] = jnp.zeros_like(dg_ref)

        dg_ref[...] += jnp.sum(dh * xn, axis=0, keepdims=True)

    return pl.pallas_call(
        body, name=name, grid=(t // tm,),
        in_specs=[pl.BlockSpec((tm, width), lambda i: (i, 0)) for width in widths] + [
            pl.BlockSpec((k, d), lambda i: (0, 0)),
            pl.BlockSpec((tm, d), lambda i: (i, 0)), pl.BlockSpec((1, d), lambda i: (0, 0)),
            pl.BlockSpec((tm, d), lambda i: (i, 0))],
        out_specs=[pl.BlockSpec((tm, d), lambda i: (i, 0)), pl.BlockSpec((tm, d), lambda i: (i, 0)),
                   pl.BlockSpec((1, d), lambda i: (0, 0))],
        out_shape=[jax.ShapeDtypeStruct((t, d), _F32), jax.ShapeDtypeStruct((t, d), _BF),
                   jax.ShapeDtypeStruct((1, d), _F32)],
        compiler_params=_params("arbitrary"),
    )(*parts, w, x, g, dres)


def _mm_tn(a, b, name, out_dtype):
    t, m = a.shape
    n = b.shape[1]
    tm = _pick_tile(m, 128, 1408)
    tn = _pick_tile(n, 128, 1024)
    tk = _row_tile(t, 1024)
    nk = t // tk

    def body(a_ref, b_ref, o_ref, acc):
        kk = pl.program_id(2)

        @pl.when(kk == 0)
        def _():
            acc[...] = jnp.zeros_like(acc)

        acc[...] += _dot_tn(_bf(a_ref[...]), b_ref[...])

        @pl.when(kk == nk - 1)
        def _():
            o_ref[...] = acc[...].astype(o_ref.dtype)

    return pl.pallas_call(
        body, name=name, grid=(m // tm, n // tn, nk),
        in_specs=[pl.BlockSpec((tk, tm), lambda i, j, kk: (kk, i)), pl.BlockSpec((tk, tn), lambda i, j, kk: (kk, j))],
        out_specs=pl.BlockSpec((tm, tn), lambda i, j, kk: (i, j)),
        out_shape=jax.ShapeDtypeStruct((m, n), out_dtype),
        scratch_shapes=[pltpu.VMEM((tm, tn), _F32)],
        compiler_params=_params("parallel", "parallel", "arbitrary"),
    )(a, b)


def _mm_tn_halves(a0, a1, b, name):
    t, m = a0.shape
    n = b.shape[1]
    tm = _pick_tile(m, 128, 1408)
    tn = _pick_tile(n, 128, 1024)
    tk = _row_tile(t, 1024)
    nk = t // tk
    half = m // tm

    def body(a0_ref, a1_ref, b_ref, o_ref, acc):
        i = pl.program_id(0)
        kk = pl.program_id(2)

        @pl.when(kk == 0)
        def _():
            acc[...] = jnp.zeros_like(acc)

        @pl.when(i < half)
        def _():
            acc[...] += _dot_tn(a0_ref[...], b_ref[...])

        @pl.when(i >= half)
        def _():
            acc[...] += _dot_tn(a1_ref[...], b_ref[...])

        @pl.when(kk == nk - 1)
        def _():
            o_ref[...] = _bf(acc[...])

    return pl.pallas_call(
        body, name=name, grid=(2 * half, n // tn, nk),
        in_specs=[pl.BlockSpec((tk, tm), lambda i, j, kk: (jnp.where(i < half, kk, 0), jnp.minimum(i, half - 1))),
                  pl.BlockSpec((tk, tm), lambda i, j, kk: (jnp.where(i >= half, kk, 0), jnp.maximum(i - half, 0))),
                  pl.BlockSpec((tk, tn), lambda i, j, kk: (kk, j))],
        out_specs=pl.BlockSpec((tm, tn), lambda i, j, kk: (i, j)),
        out_shape=jax.ShapeDtypeStruct((2 * m, n), _BF),
        scratch_shapes=[pltpu.VMEM((tm, tn), _F32)],
        compiler_params=_params("parallel", "parallel", "arbitrary"),
    )(a0, a1, b)


def _mm_tn_parts(parts, b, name, after=None):
    t, n = b.shape
    widths = [a.shape[1] for a in parts]
    m = sum(widths)
    n_parts = len(parts)
    tk = _row_tile(t, 1024)
    nk = t // tk
    extra = () if after is None else (after,)

    def body(*refs):
        b_ref = refs[n_parts]
        o_ref, acc = refs[-2:]
        kk = pl.program_id(0)

        @pl.when(kk == 0)
        def _():
            acc[...] = jnp.zeros_like(acc)

        bv = b_ref[...]
        off = 0
        for a_ref, width in zip(refs[:n_parts], widths):
            acc[off:off + width, :] += _dot_tn(_bf(a_ref[...]), bv)
            off += width

        @pl.when(kk == nk - 1)
        def _():
            o_ref[...] = _bf(acc[...])

    return pl.pallas_call(
        body, name=name, grid=(nk,),
        in_specs=[pl.BlockSpec((tk, width), lambda kk: (kk, 0)) for width in widths] + [pl.BlockSpec((tk, n), lambda kk: (kk, 0))]
        + [_ANY_SPEC] * len(extra),
        out_specs=pl.BlockSpec((m, n), lambda kk: (0, 0)),
        out_shape=jax.ShapeDtypeStruct((m, n), _BF),
        scratch_shapes=[pltpu.VMEM((m, n), _F32)],
        compiler_params=_params("arbitrary"),
    )(*parts, b, *extra)


def _final_loss(x, g, target, name):
    t, d = x.shape
    tm = _row_tile(t, 256)

    def body(x_ref, g_ref, t_ref, dx_ref, dxb_ref, dg_ref, loss_ref):
        xv = x_ref[...]
        rstd = _rms_rstd(xv)
        xn = xv * rstd
        err = xn * g_ref[...] - t_ref[...]
        dy = err * (1.0 / d)
        u = dy * g_ref[...]
        dx = rstd * (u - xn * jnp.mean(u * xn, axis=-1, keepdims=True))
        dx_ref[...] = dx
        dxb_ref[...] = _bf(dx)

        @pl.when(pl.program_id(0) == 0)
        def _():
            dg_ref[...] = jnp.zeros_like(dg_ref)
            loss_ref[...] = jnp.zeros_like(loss_ref)

        dg_ref[...] += jnp.sum(dy * xn, axis=0, keepdims=True)
        loss_ref[...] += (0.5 / d) * jnp.sum(jnp.sum(err * err, axis=1, keepdims=True), axis=0, keepdims=True)

    return pl.pallas_call(
        body, name=name, grid=(t // tm,),
        in_specs=[pl.BlockSpec((tm, d), lambda i: (i, 0)), pl.BlockSpec((1, d), lambda i: (0, 0)),
                  pl.BlockSpec((tm, d), lambda i: (i, 0))],
        out_specs=[pl.BlockSpec((tm, d), lambda i: (i, 0)), pl.BlockSpec((tm, d), lambda i: (i, 0)),
                   pl.BlockSpec((1, d), lambda i: (0, 0)), pl.BlockSpec((1, 1), lambda i: (0, 0))],
        out_shape=[jax.ShapeDtypeStruct((t, d), _F32), jax.ShapeDtypeStruct((t, d), _BF),
                   jax.ShapeDtypeStruct((1, d), _F32), jax.ShapeDtypeStruct((1, 1), _F32)],
        compiler_params=_params("arbitrary"),
    )(x, g, target)


def _pad_rows(x, pad):
    return jnp.concatenate([x, jnp.zeros((pad, x.shape[1]), x.dtype)], axis=0)


def _shift_down(xp, s):
    return xp if s == 0 else pltpu.roll(xp, s, 0)


def _shift_up(xp, s):
    return xp if s == 0 else pltpu.roll(xp, xp.shape[0] - s, 0)


def _taps3(xp):
    one = _shift_down(xp, 1)
    return xp, one, _shift_down(one, 1)


def _conv3_taps(taps, w_ref):
    return w_ref[2:3, :] * taps[0] + w_ref[1:2, :] * taps[1] + w_ref[0:1, :] * taps[2]


def _conv3(xp, w_ref):
    return _conv3_taps(_taps3(xp), w_ref)


def _conv3_t(dyp, w_ref):
    one = _shift_up(dyp, 1)
    return w_ref[2:3, :] * dyp + w_ref[1:2, :] * one + w_ref[0:1, :] * _shift_up(one, 1)


def _conv3_dw(dyp, taps):
    return [jnp.sum(dyp * taps[2 - k], axis=0, keepdims=True) for k in range(3)]


def _ffn_mid_fwd(up_pre, wf, n_ex, name):
    t, f2 = up_pre.shape
    f = f2 // 2
    s = t // n_ex
    cb = MIX_W
    nb = f // cb

    def body(ug_ref, uv_ref, wg_ref, wv_ref, act_ref, gf_ref, vf_ref):
        gf = _conv3(_pad_rows(ug_ref[...].astype(_F32), 8), wg_ref)[:s]
        vf = _conv3(_pad_rows(uv_ref[...].astype(_F32), 8), wv_ref)[:s]
        act_ref[...] = _bf(gf * _sigmoid(gf) * vf)
        gf_ref[...] = _bf(gf)
        vf_ref[...] = _bf(vf)

    out = pl.BlockSpec((s, cb), lambda e, j: (e, j))
    return pl.pallas_call(
        body, name=name, grid=(n_ex, nb),
        in_specs=[pl.BlockSpec((s, cb), lambda e, j: (e, j)), pl.BlockSpec((s, cb), lambda e, j: (e, j + nb)),
                  pl.BlockSpec((K_FFN, cb), lambda e, j: (0, j)), pl.BlockSpec((K_FFN, cb), lambda e, j: (0, j + nb))],
        out_specs=[out, out, out],
        out_shape=[jax.ShapeDtypeStruct((t, f), _BF)] * 3,
        compiler_params=_params("parallel", "parallel"),
    )(up_pre, up_pre, wf, wf)


def _ffn_mid_bwd(up_pre, conv_g, conv_v, wf, dact, n_ex, name):
    t, f2 = up_pre.shape
    f = f2 // 2
    s = t // n_ex
    cb = MIX_W
    nb = f // cb

    def body(ug_ref, uv_ref, gf_ref, vf_ref, wg_ref, wv_ref, da_ref, dug_ref, duv_ref, dwg_ref, dwv_ref):
        gf = _ld(gf_ref)
        vf = _ld(vf_ref)
        sg = _sigmoid(gf)
        da = _ld(da_ref)

        @pl.when(pl.program_id(1) == 0)
        def _():
            dwg_ref[...] = jnp.zeros_like(dwg_ref)
            dwv_ref[...] = jnp.zeros_like(dwv_ref)

        def finish(dpost, w_ref, x_ref, du_ref, dw_ref):
            ahead = [_pad_rows(dpost, 8)]
            ahead.append(_shift_up(ahead[0], 1))
            ahead.append(_shift_up(ahead[1], 1))
            du_ref[...] = _bf((w_ref[2:3, :] * ahead[0] + w_ref[1:2, :] * ahead[1] + w_ref[0:1, :] * ahead[2])[:s])
            x = _ld(x_ref)
            for k in range(K_FFN):
                dw_ref[k:k + 1, :] += jnp.sum(ahead[2 - k][:s] * x, axis=0, keepdims=True)

        finish(da * vf * sg * (1.0 + gf * (1.0 - sg)), wg_ref, ug_ref, dug_ref, dwg_ref)
        finish(da * gf * sg, wv_ref, uv_ref, duv_ref, dwv_ref)

    return pl.pallas_call(
        body, name=name, grid=(nb, n_ex),
        in_specs=[pl.BlockSpec((s, cb), lambda j, e: (e, j)), pl.BlockSpec((s, cb), lambda j, e: (e, j + nb)),
                  pl.BlockSpec((s, cb), lambda j, e: (e, j)), pl.BlockSpec((s, cb), lambda j, e: (e, j)),
                  pl.BlockSpec((K_FFN, cb), lambda j, e: (0, j)), pl.BlockSpec((K_FFN, cb), lambda j, e: (0, j + nb)),
                  pl.BlockSpec((s, cb), lambda j, e: (e, j))],
        out_specs=[pl.BlockSpec((s, cb), lambda j, e: (e, j)), pl.BlockSpec((s, cb), lambda j, e: (e, j)),
                   pl.BlockSpec((K_FFN, cb), lambda j, e: (0, j)), pl.BlockSpec((K_FFN, cb), lambda j, e: (0, j))],
        out_shape=[jax.ShapeDtypeStruct((t, f), _BF), jax.ShapeDtypeStruct((t, f), _BF),
                   jax.ShapeDtypeStruct((K_FFN, f), _F32), jax.ShapeDtypeStruct((K_FFN, f), _F32)],
        compiler_params=_params("parallel", "arbitrary"),
    )(up_pre, up_pre, conv_g, conv_v, wf, wf, dact)


def _pcol(s, j):
    return pl.BlockSpec((s, MIX_W), lambda e, j=j: (e, j))


def _vec(rows=1):
    return pl.BlockSpec((rows, MIX_W), lambda e: (0, 0))


def _mix_a_fwd(p, wa, n_ex, name):
    t = p.shape[0]
    s = t // n_ex

    def body(gb_ref, gc_ref, ha_ref, w_ref, y_ref):
        cv = _conv3(_pad_rows(_ld(gc_ref) * _ld(ha_ref), 8), w_ref)[:s]
        y_ref[...] = _bf(_ld(gb_ref) * cv)

    return pl.pallas_call(
        body, name=name, grid=(n_ex,),
        in_specs=[_pcol(s, 0), _pcol(s, 1), _pcol(s, 2), _vec(K_SHORT)],
        out_specs=pl.BlockSpec((s, MIX_W), lambda e: (e, 0)),
        out_shape=jax.ShapeDtypeStruct((t, MIX_W), _BF),
        compiler_params=_params("parallel"),
    )(p, p, p, wa)


def _mix_a_bwd(p, wa, dmix, n_ex, name):
    t = p.shape[0]
    s = t // n_ex

    def body(gb_ref, gc_ref, ha_ref, w_ref, dy_ref, dp_ref, dw_ref):
        gc = _ld(gc_ref)
        ha = _ld(ha_ref)
        up = _taps3(_pad_rows(gc * ha, 8))
        cv = _conv3_taps(up, w_ref)[:s]
        dy = _ld(dy_ref)
        dcvp = _pad_rows(dy * _ld(gb_ref), 8)
        du = _conv3_t(dcvp, w_ref)[:s]
        dp_ref[:, 0:MIX_W] = _bf(dy * cv)
        dp_ref[:, MIX_W:2 * MIX_W] = _bf(du * ha)
        dp_ref[:, 2 * MIX_W:3 * MIX_W] = _bf(du * gc)

        @pl.when(pl.program_id(0) == 0)
        def _():
            dw_ref[...] = jnp.zeros_like(dw_ref)

        rows = _conv3_dw(dcvp, up)
        for k in range(3):
            dw_ref[k:k + 1, :] += rows[k]

    return pl.pallas_call(
        body, name=name, grid=(n_ex,),
        in_specs=[_pcol(s, 0), _pcol(s, 1), _pcol(s, 2), _vec(K_SHORT), _pcol(s, 0)],
        out_specs=[pl.BlockSpec((s, 3 * MIX_W), lambda e: (e, 0)), _vec(K_SHORT)],
        out_shape=[jax.ShapeDtypeStruct((t, 3 * MIX_W), _BF), jax.ShapeDtypeStruct((K_SHORT, MIX_W), _F32)],
        compiler_params=_params("arbitrary"),
    )(p, p, p, wa, dmix)


CONF_PAD = 32
CONF_ROWS = 64
_CONF_LANES = (slice(0, 128), slice(128, 256))


def _conf_taps(win, ahead):
    n = CONF_ROWS + CONF_PAD
    for b in range(8):
        rot = win if b == 0 else pltpu.roll(win, (n - b) if ahead else b, 0)
        for a in range(4):
            if 8 * a + b < K_CONF:
                yield rot, 8 * a + b, (8 * a) if ahead else (CONF_PAD - 8 * a)


def _ln_fwd(x, g, b):
    mu = jnp.mean(x, axis=-1, keepdims=True)
    xc = x - mu
    rstd = lax.rsqrt(jnp.mean(xc * xc, axis=-1, keepdims=True) + LN_EPS)
    xhat = xc * rstd
    return xhat * g + b, xhat, rstd


def _ln_bwd(dy, xhat, rstd, g):
    dxh = dy * g
    return rstd * (dxh - jnp.mean(dxh, axis=-1, keepdims=True) - xhat * jnp.mean(dxh * xhat, axis=-1, keepdims=True))


def _mix_b_fwd(p, wb, bb, lg, lb, n_ex, name):
    t = p.shape[0]
    s = t // n_ex

    def body(val_ref, gat_ref, w_ref, bb_ref, lg_ref, lb_ref, y_ref, cb_ref, xpad):
        xpad[0:CONF_PAD, :] = jnp.zeros((CONF_PAD, MIX_W), _F32)
        xpad[CONF_PAD:, :] = _ld(val_ref) * _sigmoid(_ld(gat_ref))

        def chunk(c, carry):
            r0 = pl.multiple_of(c * CONF_ROWS, CONF_ROWS)
            for lanes in _CONF_LANES:
                acc = None
                for rot, sh, lo in _conf_taps(xpad[pl.ds(r0, CONF_ROWS + CONF_PAD), lanes], False):
                    term = w_ref[K_CONF - 1 - sh:K_CONF - sh, lanes] * rot[lo:lo + CONF_ROWS]
                    acc = term if acc is None else acc + term
                cb_ref[pl.ds(r0, CONF_ROWS), lanes] = acc + bb_ref[:, lanes]
            return carry

        lax.fori_loop(0, s // CONF_ROWS, chunk, 0)
        yl, _, _ = _ln_fwd(cb_ref[...], lg_ref[...], lb_ref[...])
        y_ref[...] = _bf(yl * _sigmoid(yl))

    return pl.pallas_call(
        body, name=name, grid=(n_ex,),
        in_specs=[_pcol(s, 3), _pcol(s, 4), _vec(K_CONF), _vec(), _vec(), _vec()],
        out_specs=[pl.BlockSpec((s, MIX_W), lambda e: (e, 0)), pl.BlockSpec((s, MIX_W), lambda e: (e, 0))],
        out_shape=[jax.ShapeDtypeStruct((t, MIX_W), _BF), jax.ShapeDtypeStruct((t, MIX_W), _F32)],
        scratch_shapes=[pltpu.VMEM((CONF_PAD + s, MIX_W), _F32)],
        compiler_params=_params("parallel"),
    )(p, p, wb, bb, lg, lb)


def _mix_b_bwd(p, cb, wb, lg, lb, dmix, n_ex, name):
    t = p.shape[0]
    s = t // n_ex

    def body(val_ref, gat_ref, cb_ref, w_ref, lg_ref, lb_ref, dy_ref, dp_ref, dw_ref, dbb_ref, dlg_ref, dlb_ref,
             xpad, dpad, dglu_s, dw_acc):
        @pl.when(pl.program_id(0) == 0)
        def _():
            for r in (dw_ref, dbb_ref, dlg_ref, dlb_ref):
                r[...] = jnp.zeros_like(r)

        yl, xhat, rstd = _ln_fwd(cb_ref[...], lg_ref[...], lb_ref[...])
        sy = _sigmoid(yl)
        dyl = _ld(dy_ref) * sy * (1.0 + yl * (1.0 - sy))
        dlg_ref[...] += jnp.sum(dyl * xhat, axis=0, keepdims=True)
        dlb_ref[...] += jnp.sum(dyl, axis=0, keepdims=True)
        dcb = _ln_bwd(dyl, xhat, rstd, lg_ref[...])
        dbb_ref[...] += jnp.sum(dcb, axis=0, keepdims=True)

        val = _ld(val_ref)
        sg = _sigmoid(_ld(gat_ref))
        xpad[0:CONF_PAD, :] = jnp.zeros((CONF_PAD, MIX_W), _F32)
        xpad[CONF_PAD:, :] = val * sg
        dpad[0:s, :] = dcb
        dpad[s:, :] = jnp.zeros((CONF_PAD, MIX_W), _F32)
        dw_acc[...] = jnp.zeros_like(dw_acc)

        def chunk(c, carry):
            r0 = pl.multiple_of(c * CONF_ROWS, CONF_ROWS)
            for lanes in _CONF_LANES:
                d_win = dpad[pl.ds(r0, CONF_ROWS + CONF_PAD), lanes]
                d_rows = d_win[0:CONF_ROWS]
                acc = None
                for rot, sh, lo in _conf_taps(d_win, True):
                    term = w_ref[K_CONF - 1 - sh:K_CONF - sh, lanes] * rot[lo:lo + CONF_ROWS]
                    acc = term if acc is None else acc + term
                dglu_s[pl.ds(r0, CONF_ROWS), lanes] = acc
                for rot, sh, lo in _conf_taps(xpad[pl.ds(r0, CONF_ROWS + CONF_PAD), lanes], False):
                    prod = d_rows * rot[lo:lo + CONF_ROWS]
                    dw_acc[K_CONF - 1 - sh, :, lanes] += jnp.sum(prod.reshape(CONF_ROWS // 8, 8, 128), axis=0)
            return carry

        lax.fori_loop(0, s // CONF_ROWS, chunk, 0)
        dw_ref[...] += jnp.sum(dw_acc[...], axis=1)
        dglu = dglu_s[...]
        dp_ref[:, 0:MIX_W] = _bf(dglu * sg)
        dp_ref[:, MIX_W:2 * MIX_W] = _bf(dglu * val * sg * (1.0 - sg))

    return pl.pallas_call(
        body, name=name, grid=(n_ex,),
        in_specs=[_pcol(s, 3), _pcol(s, 4), pl.BlockSpec((s, MIX_W), lambda e: (e, 0)), _vec(K_CONF), _vec(), _vec(),
                  _pcol(s, 1)],
        out_specs=[pl.BlockSpec((s, 2 * MIX_W), lambda e: (e, 0)), _vec(K_CONF), _vec(), _vec(), _vec()],
        out_shape=[jax.ShapeDtypeStruct((t, 2 * MIX_W), _BF), jax.ShapeDtypeStruct((K_CONF, MIX_W), _F32),
                   jax.ShapeDtypeStruct((1, MIX_W), _F32), jax.ShapeDtypeStruct((1, MIX_W), _F32),
                   jax.ShapeDtypeStruct((1, MIX_W), _F32)],
        scratch_shapes=[pltpu.VMEM((CONF_PAD + s, MIX_W), _F32), pltpu.VMEM((s + CONF_PAD, MIX_W), _F32),
                        pltpu.VMEM((s, MIX_W), _F32), pltpu.VMEM((K_CONF, 8, MIX_W), _F32)],
        compiler_params=_params("arbitrary"),
    )(p, p, cb, wb, lg, lb, dmix)


_INV_SQRT2 = 0.7071067811865476
_INV_SQRT2PI = 0.3989422804014327


def _gelu(x):
    return 0.5 * x * (1.0 + lax.erf(x * _INV_SQRT2))


def _gelu_grad(x):
    return 0.5 * (1.0 + lax.erf(x * _INV_SQRT2)) + x * _INV_SQRT2PI * jnp.exp(-0.5 * x * x)


def _head_masks(width=MIX_W):
    lane = lax.broadcasted_iota(jnp.int32, (1, width), 1)
    return [(lane >= h * HEAD_DIM) & (lane < (h + 1) * HEAD_DIM) for h in range(N_HEADS)]


def _tril_mask():
    r = lax.broadcasted_iota(jnp.int32, (CHUNK, CHUNK), 0)
    c = lax.broadcasted_iota(jnp.int32, (CHUNK, CHUNK), 1)
    return c <= r


def _sgu_apply(ws_ref, x3, transpose):
    n = x3.shape[0]
    tril = _tril_mask()
    masks = _head_masks()
    xb = _bf(x3)
    out = jnp.zeros(x3.shape, _F32)
    for h in range(N_HEADS):
        w = _bf(jnp.where(tril, ws_ref[h], 0.0))
        wb = jnp.broadcast_to(w[None], (n, CHUNK, CHUNK))
        dims = (((1,), (1,)), ((0,), (0,))) if transpose else (((2,), (1,)), ((0,), (0,)))
        r = lax.dot_general(wb, xb, dims, preferred_element_type=_F32)
        out = out + jnp.where(masks[h][None], r, 0.0)
    return out


def _mix_c_fwd(p, lg, lb, ws, sb_full, n_ex, name):
    t = p.shape[0]
    s = t // n_ex
    nc = s // CHUNK

    def body(pu_ref, pv_ref, lg_ref, lb_ref, ws_ref, sb_ref, y_ref):
        u = _gelu(_ld(pu_ref))
        vl, _, _ = _ln_fwd(_gelu(_ld(pv_ref)), lg_ref[...], lb_ref[...])
        sp = _sgu_apply(ws_ref, vl.reshape(nc, CHUNK, MIX_W), False) + sb_ref[...][None]
        y_ref[...] = _bf(u * sp.reshape(s, MIX_W))

    return pl.pallas_call(
        body, name=name, grid=(n_ex,),
        in_specs=[_pcol(s, 5), _pcol(s, 6), _vec(), _vec(),
                  pl.BlockSpec((N_HEADS, CHUNK, CHUNK), lambda e: (0, 0, 0)), pl.BlockSpec((CHUNK, MIX_W), lambda e: (0, 0))],
        out_specs=pl.BlockSpec((s, MIX_W), lambda e: (e, 0)),
        out_shape=jax.ShapeDtypeStruct((t, MIX_W), _BF),
        compiler_params=_params("parallel"),
    )(p, p, lg, lb, ws, sb_full)


def _mix_c_bwd(p, lg, lb, ws, sb_full, dmix, n_ex, name):
    t = p.shape[0]
    s = t // n_ex
    nc = s // CHUNK

    def body(pu_ref, pv_ref, lg_ref, lb_ref, ws_ref, sb_ref, dy_ref, dp_ref, dlg_ref, dlb_ref, dws_ref, dsb_ref):
        @pl.when(pl.program_id(0) == 0)
        def _():
            for r in (dlg_ref, dlb_ref, dws_ref, dsb_ref):
                r[...] = jnp.zeros_like(r)

        pu = _ld(pu_ref)
        pv = _ld(pv_ref)
        u = _gelu(pu)
        vl, xhat, rstd = _ln_fwd(_gelu(pv), lg_ref[...], lb_ref[...])
        vl3 = vl.reshape(nc, CHUNK, MIX_W)
        sp = _sgu_apply(ws_ref, vl3, False) + sb_ref[...][None]
        dy = _ld(dy_ref)
        dp_ref[:, 0:MIX_W] = _bf(dy * sp.reshape(s, MIX_W) * _gelu_grad(pu))
        dsp3 = (dy * u).reshape(nc, CHUNK, MIX_W)
        dsb_full = jnp.sum(dsp3, axis=0)
        masks = _head_masks()
        tril = _tril_mask()
        dspb = _bf(dsp3)
        vlb = _bf(vl3)
        for h in range(N_HEADS):
            dsb_ref[:, h:h + 1] += jnp.sum(jnp.where(masks[h], dsb_full, 0.0), axis=1, keepdims=True)
            dm = jnp.where(masks[h][None], dspb, jnp.zeros_like(dspb))
            g3 = lax.dot_general(dm, vlb, (((2,), (2,)), ((0,), (0,))), preferred_element_type=_F32)
            dws_ref[h] += jnp.where(tril, jnp.sum(g3, axis=0), 0.0)
        dvl = _sgu_apply(ws_ref, dsp3, True).reshape(s, MIX_W)
        dlg_ref[...] += jnp.sum(dvl * xhat, axis=0, keepdims=True)
        dlb_ref[...] += jnp.sum(dvl, axis=0, keepdims=True)
        dp_ref[:, MIX_W:2 * MIX_W] = _bf(_ln_bwd(dvl, xhat, rstd, lg_ref[...]) * _gelu_grad(pv))

    return pl.pallas_call(
        body, name=name, grid=(n_ex,),
        in_specs=[_pcol(s, 5), _pcol(s, 6), _vec(), _vec(),
                  pl.BlockSpec((N_HEADS, CHUNK, CHUNK), lambda e: (0, 0, 0)), pl.BlockSpec((CHUNK, MIX_W), lambda e: (0, 0)),
                  _pcol(s, 2)],
        out_specs=[pl.BlockSpec((s, 2 * MIX_W), lambda e: (e, 0)), _vec(), _vec(),
                   pl.BlockSpec((N_HEADS, CHUNK, CHUNK), lambda e: (0, 0, 0)), pl.BlockSpec((CHUNK, N_HEADS), lambda e: (0, 0))],
        out_shape=[jax.ShapeDtypeStruct((t, 2 * MIX_W), _BF), jax.ShapeDtypeStruct((1, MIX_W), _F32),
                   jax.ShapeDtypeStruct((1, MIX_W), _F32), jax.ShapeDtypeStruct((N_HEADS, CHUNK, CHUNK), _F32),
                   jax.ShapeDtypeStruct((CHUNK, N_HEADS), _F32)],
        compiler_params=_params("arbitrary"),
    )(p, p, lg, lb, ws, sb_full, dmix)


D_QBLOCK = 256
HEAD_COLS = N_HEADS * KV_BLOCK


def _stack_heads(x3):
    return jnp.stack([_bf(jnp.where(m[None], x3, 0.0)) for m in _head_masks()], axis=1)


def _stack_heads_rows(x):
    return jnp.concatenate([_bf(jnp.where(m, x, 0.0)) for m in _head_masks()], axis=0)


def _cols_to_rows(x):
    return jnp.concatenate([x[:, h * KV_BLOCK:(h + 1) * KV_BLOCK] for h in range(N_HEADS)], axis=0)


def _head_sums(x):
    return [jnp.sum(x[:, h * KV_BLOCK:(h + 1) * KV_BLOCK], axis=1, keepdims=True) for h in range(N_HEADS)]


def _spread(cols):
    tq = cols[0].shape[0]
    return jnp.concatenate([jnp.broadcast_to(c, (tq, KV_BLOCK)) for c in cols], axis=1)


def _pair_dot(x, m2):
    half = 2 * KV_BLOCK
    xb = _bf(x)
    return jnp.concatenate([_dot(xb[:, :half], m2), _dot(xb[:, half:], m2)], axis=1)


def _tri2(lower):
    n = 2 * KV_BLOCK
    r = lax.broadcasted_iota(jnp.int32, (n, n), 0)
    c = lax.broadcasted_iota(jnp.int32, (n, n), 1)
    same = (r >= KV_BLOCK) == (c >= KV_BLOCK)
    return _bf(jnp.where(same & (r > c if lower else r < c), 1.0, 0.0))


def _sb_scores(qs, kc, j, t_idx, on_diagonal):
    z = _dot_nt(qs, kc)
    lb = jnp.minimum(z, 0.0) - jnp.log(1.0 + jnp.exp(-jnp.abs(z)))
    if not on_diagonal:
        return (lambda x: x), lb, lb - z
    lane = lax.broadcasted_iota(jnp.int32, (1, HEAD_COLS), 1)
    valid = (j * KV_BLOCK + (lane & (KV_BLOCK - 1))) < t_idx
    keep = lambda x: jnp.where(valid, x, 0.0)
    return keep, lb, keep(lb - z)


RUN_LANES = 128


def _run_lane(j, h):
    return lax.broadcasted_iota(jnp.int32, (1, RUN_LANES), 1) == j * N_HEADS + h


def _d_qblock(s):
    return D_QBLOCK if s % D_QBLOCK == 0 else KV_BLOCK


def _mix_d_fwd(p, n_ex, name):
    t = p.shape[0]
    s = t // n_ex
    tq = _d_qblock(s)
    nq = s // tq
    r = tq // KV_BLOCK
    nb = s // KV_BLOCK
    assert nb * N_HEADS <= RUN_LANES

    def body(q_ref, k_ref, v_ref, y_ref, runs_ref, kc, vc):
        i = pl.program_id(1)

        @pl.when(i == 0)
        def _():
            kc[...] = _stack_heads(k_ref[...].reshape(nb, KV_BLOCK, MIX_W))
            vc[...] = _stack_heads(v_ref[...].reshape(nb, KV_BLOCK, MIX_W))

        qs = _bf(_ld(q_ref) * (HEAD_DIM ** -0.5))
        t_idx = i * tq + lax.broadcasted_iota(jnp.int32, (tq, 1), 0)
        after_m = _tri2(True)
        nkb = (i + 1) * r

        runs_ref[...] = jnp.zeros_like(runs_ref)

        def one_block(j, runs, acc, on_diagonal):
            keep, lb, c = _sb_scores(qs, kc[j].reshape(HEAD_COLS, MIX_W), j, t_idx, on_diagonal)
            a = keep(jnp.exp(lb + _pair_dot(c, after_m) + _spread(runs)))
            acc = acc + _dot(_bf(a), vc[j].reshape(HEAD_COLS, MIX_W))
            kept = runs_ref[...]
            for h in range(N_HEADS):
                kept = jnp.where(_run_lane(j, h), runs[h], kept)
            runs_ref[...] = kept
            return tuple(ru + cs for ru, cs in zip(runs, _head_sums(c))), acc

        def trip(last, carry, on_diagonal):
            runs, acc = carry
            for sub in range(r):
                runs, acc = one_block(last - sub, runs, acc, on_diagonal)
            return runs, acc

        zero = jnp.zeros((tq, 1), _F32)
        carry = trip(nkb - 1, ((zero,) * N_HEADS, jnp.zeros((tq, MIX_W), _F32)), True)
        below = lambda m: nkb - 1 - (m + 1) * r
        carry = lax.fori_loop(0, i // 2, lambda m, carry: trip(below(2 * m + 1), trip(below(2 * m), carry, False), False), carry)
        _, acc = lax.fori_loop(0, i % 2, lambda m, carry: trip(below(i - 1), carry, False), carry)
        y_ref[...] = _bf(acc)

    return pl.pallas_call(
        body, name=name, grid=(n_ex, nq),
        in_specs=[pl.BlockSpec((tq, MIX_W), lambda e, i: (e * nq + i, 7)), pl.BlockSpec((s, MIX_W), lambda e, i: (e, 8)),
                  pl.BlockSpec((s, MIX_W), lambda e, i: (e, 9))],
        out_specs=[pl.BlockSpec((tq, MIX_W), lambda e, i: (e * nq + i, 0)),
                   pl.BlockSpec((tq, RUN_LANES), lambda e, i: (e * nq + i, 0))],
        out_shape=[jax.ShapeDtypeStruct((t, MIX_W), _BF), jax.ShapeDtypeStruct((t, RUN_LANES), _F32)],
        scratch_shapes=[pltpu.VMEM((nb, N_HEADS, KV_BLOCK, MIX_W), _BF), pltpu.VMEM((nb, N_HEADS, KV_BLOCK, MIX_W), _BF)],
        compiler_params=_params("parallel", "arbitrary"),
    )(p, p, p)


def _mix_d_bwd(p, kept_runs, dmix, n_ex, name):
    t = p.shape[0]
    s = t // n_ex
    tq = _d_qblock(s)
    nq = s // tq
    r = tq // KV_BLOCK
    nb = s // KV_BLOCK
    scale = HEAD_DIM ** -0.5

    def body(q_ref, k_ref, v_ref, runs_ref, do_ref, dq_ref, dk_ref, dv_ref, kc, vc):
        i = pl.program_id(1)

        @pl.when(i == 0)
        def _():
            kc[...] = _stack_heads(k_ref[...].reshape(nb, KV_BLOCK, MIX_W))
            vc[...] = _stack_heads(v_ref[...].reshape(nb, KV_BLOCK, MIX_W))
            dk_ref[...] = jnp.zeros_like(dk_ref)
            dv_ref[...] = jnp.zeros_like(dv_ref)

        q_scaled = _ld(q_ref) * scale
        qs = _bf(q_scaled)
        do = do_ref[...]
        dob = _bf(do)
        q_rows = _stack_heads_rows(q_scaled)
        do_rows = _stack_heads_rows(do)
        kept = runs_ref[...]
        t_idx = i * tq + lax.broadcasted_iota(jnp.int32, (tq, 1), 0)
        after_m = _tri2(True)
        before_m = _tri2(False)
        nkb = (i + 1) * r
        zero = jnp.zeros((tq, 1), _F32)

        def trip(first, carry, on_diagonal):
            for sub in range(r):
                carry = one_block(first + sub, carry, on_diagonal)
            return carry

        def one_block(j, carry, on_diagonal):
            pres, dq = carry
            rows = pl.ds(pl.multiple_of(j * KV_BLOCK, KV_BLOCK), KV_BLOCK)
            kj = kc[j].reshape(HEAD_COLS, MIX_W)
            keep, lb, c = _sb_scores(qs, kj, j, t_idx, on_diagonal)
            runs = [jnp.sum(jnp.where(_run_lane(j, h), kept, 0.0), axis=1, keepdims=True) for h in range(N_HEADS)]
            a = keep(jnp.exp(lb + _pair_dot(c, after_m) + _spread(runs)))
            g = a * _dot_nt(dob, vc[j].reshape(HEAD_COLS, MIX_W))
            before = _pair_dot(g, before_m) + _spread(pres)
            sig = jnp.exp(lb)
            dz = _bf(keep(g * (1.0 - sig) - sig * before))
            dk_ref[rows, :] += _dot_tn(_cols_to_rows(dz), q_rows)
            dv_ref[rows, :] += _dot_tn(_cols_to_rows(_bf(a)), do_rows)
            return tuple(pr + gs for pr, gs in zip(pres, _head_sums(g))), dq + _dot(dz, kj)

        init = ((zero,) * N_HEADS, jnp.zeros((tq, MIX_W), _F32))
        carry = lax.fori_loop(0, i // 2, lambda m, carry: trip((2 * m + 1) * r, trip(2 * m * r, carry, False), False), init)
        carry = lax.fori_loop(0, i % 2, lambda m, carry: trip((i - 1) * r, carry, False), carry)
        _, dq = trip(i * r, carry, True)
        dq_ref[...] = _bf(dq * scale)

    return pl.pallas_call(
        body, name=name, grid=(n_ex, nq),
        in_specs=[pl.BlockSpec((tq, MIX_W), lambda e, i: (e * nq + i, 7)), pl.BlockSpec((s, MIX_W), lambda e, i: (e, 8)),
                  pl.BlockSpec((s, MIX_W), lambda e, i: (e, 9)), pl.BlockSpec((tq, RUN_LANES), lambda e, i: (e * nq + i, 0)),
                  pl.BlockSpec((tq, MIX_W), lambda e, i: (e * nq + i, 3))],
        out_specs=[pl.BlockSpec((tq, MIX_W), lambda e, i: (e * nq + i, 0)), pl.BlockSpec((s, MIX_W), lambda e, i: (e, 0)),
                   pl.BlockSpec((s, MIX_W), lambda e, i: (e, 0))],
        out_shape=[jax.ShapeDtypeStruct((t, MIX_W), _BF), jax.ShapeDtypeStruct((t, MIX_W), _F32),
                   jax.ShapeDtypeStruct((t, MIX_W), _F32)],
        scratch_shapes=[pltpu.VMEM((nb, N_HEADS, KV_BLOCK, MIX_W), _BF), pltpu.VMEM((nb, N_HEADS, KV_BLOCK, MIX_W), _BF)],
        compiler_params=_params("parallel", "arbitrary"),
    )(p, p, p, kept_runs, dmix)


def _fwd_mix(x, w, l, n_ex):
    p, h1 = _norm_mm(x, w["norm1_g"][l], w["w_in_t"][l], "in_proj")
    y_a = _mix_a_fwd(p, w["conv_a_w"][l], n_ex, "mix_a_fwd")
    y_b, cb = _mix_b_fwd(p, w["conv_b_w"][l], w["conv_b_b"][l], w["ln_b_g"][l], w["ln_b_b"][l], n_ex, "mix_b_fwd")
    y_c = _mix_c_fwd(p, w["ln_c_g"][l], w["ln_c_b"][l], w["sgu_w"][l], w["sgu_b_full"][l], n_ex, "mix_c_fwd")
    y_d, runs_d = _mix_d_fwd(p, n_ex, "mix_d_fwd")
    return dict(x=x, h1=h1, p=p, cb=cb, runs_d=runs_d, mix=(y_a, y_b, y_c, y_d))


def _fwd_ffn(st, w, l, n_ex):
    x1 = _mm_res(st["mix"], w["w_out"][l], st["x"], "out_proj")
    up_pre, h2 = _norm_mm(x1, w["norm2_g"][l], w["w_up_t"][l], "up_proj")
    act, conv_g, conv_v = _ffn_mid_fwd(up_pre, w["conv_f_w"][l], n_ex, "ffn_mid_fwd")
    st.update(x1=x1, h2=h2, up_pre=up_pre, act=act, conv_g=conv_g, conv_v=conv_v)
    return _mm_res((act,), w["w_down"][l], x1, "down_proj")


def _bwd_ffn(st, w, l, dx, dxb, n_ex):
    g = {}
    dact = _mm_nt(dxb, w["w_down"][l], "down_proj_dx")
    g["w_down"] = _mm_tn(st["act"], dxb, "down_proj_dw", _BF)
    dup_g, dup_v, dwf_g, dwf_v = _ffn_mid_bwd(
        st["up_pre"], st["conv_g"], st["conv_v"], w["conv_f_w"][l], dact, n_ex, "ffn_mid_bwd")
    g["conv_f_w"] = jnp.concatenate([dwf_g, dwf_v], axis=1)
    dx, dxb, g["norm2_g"] = _mm_normbwd((dup_g, dup_v), w["w_up_t"][l], st["x1"], w["norm2_g"][l], dx, "up_proj_dx")
    g["w_up_t"] = _mm_tn_halves(dup_g, dup_v, st["h2"], "up_proj_dw")
    return dx, dxb, g


def _bwd_out_proj(st, w, l, dxb):
    return _mm_nt(dxb, w["w_out"][l], "out_proj_dx"), _mm_tn_parts(st["mix"], dxb, "out_proj_dw")


def _bwd_mixers(st, w, l, dx, dmix, n_ex):
    g = {}
    p = st["p"]
    dp_a, g["conv_a_w"] = _mix_a_bwd(p, w["conv_a_w"][l], dmix, n_ex, "mix_a_bwd")
    dp_b, g["conv_b_w"], g["conv_b_b"], g["ln_b_g"], g["ln_b_b"] = _mix_b_bwd(
        p, st["cb"], w["conv_b_w"][l], w["ln_b_g"][l], w["ln_b_b"][l], dmix, n_ex, "mix_b_bwd")
    dp_c, g["ln_c_g"], g["ln_c_b"], g["sgu_w"], g["sgu_b_t"] = _mix_c_bwd(
        p, w["ln_c_g"][l], w["ln_c_b"][l], w["sgu_w"][l], w["sgu_b_full"][l], dmix, n_ex, "mix_c_bwd")
    dq, dk, dv = _mix_d_bwd(p, st["runs_d"], dmix, n_ex, "mix_d_bwd")
    dp = (dp_a, dp_b, dp_c, dq, dk, dv)
    dx, dxb, g["norm1_g"] = _mm_normbwd(dp, w["w_in_t"][l], st["x"], w["norm1_g"][l], dx, "in_proj_dx")
    return dx, dxb, g, dp


def _bwd_mix(st, w, l, dx, dxb, n_ex):
    dmix, dw_out = _bwd_out_proj(st, w, l, dxb)
    dx, dxb, g, dp = _bwd_mixers(st, w, l, dx, dmix, n_ex)
    g["w_out"] = dw_out
    g["w_in_t"] = _mm_tn_parts(dp, st["h1"], "in_proj_dw")
    return dx, dxb, g


_MESH = pl.DeviceIdType.MESH
_ANY = pl.BlockSpec(memory_space=pl.ANY)


def _position():
    return lax.axis_index("x"), lax.axis_index("y"), lax.axis_index("c")


def _flat(px, py, pc):
    return 4 * px + 2 * py + pc


def _all_gather(shard, name, after):
    r, c_ = shard.shape

    def body(x_ref, after_ref, out_ref, send_sems, recv_sems, local_sem):
        x, y, c = _position()
        me, sibling = (x, y, c), (x, y, 1 - c)
        chips = [(1 - x, y), (x, 1 - y), (1 - x, 1 - y)]

        def copy(k, block, to, src=None):
            slab = out_ref.at[_flat(*block)]
            return pltpu.make_async_remote_copy(
                src_ref=slab if src is None else src, dst_ref=slab, send_sem=send_sems.at[k], recv_sem=recv_sems.at[k],
                device_id=to, device_id_type=_MESH)

        mine = pltpu.make_async_copy(x_ref, out_ref.at[_flat(*me)], local_sem)
        mine.start()
        first = [copy(0, me, sibling, src=x_ref)]
        first += [copy(1 + j, me, (*chip, c), src=x_ref) for j, chip in enumerate(chips)]
        for cp in first:
            cp.start()
        passed = [copy(4 + j, (*chip, c), sibling) for j, chip in enumerate(chips)]
        for j, chip in enumerate(chips):
            copy(1 + j, (*chip, c), me).wait_recv()
            passed[j].start()
        copy(0, sibling, me).wait_recv()
        for j, chip in enumerate(chips):
            copy(4 + j, (*chip, 1 - c), me).wait_recv()
        for cp in first + passed:
            cp.wait_send()
        mine.wait()

    return pl.pallas_call(
        body, name=name, out_shape=jax.ShapeDtypeStruct((N_DEV, r, c_), shard.dtype),
        in_specs=[_ANY, _ANY], out_specs=_ANY,
        scratch_shapes=[pltpu.SemaphoreType.DMA((7,)), pltpu.SemaphoreType.DMA((7,)), pltpu.SemaphoreType.DMA],
    )(shard, after)


_HBM = pl.BlockSpec(memory_space=pltpu.HBM)
_SEM = pl.BlockSpec(memory_space=pltpu.SEMAPHORE)
_DATAFLOW = pltpu.SideEffectType.DATAFLOW_SIDE_EFFECTING


def _peers(x, y, c):
    return [((1 - x) if (k + 1) & 4 else x, (1 - y) if (k + 1) & 2 else y, (1 - c) if (k + 1) & 1 else c)
            for k in range(N_DEV - 1)]


def _direct_copies(src_refs, land_refs, send_sems, recv_sems, to_all):
    x, y, c = _position()
    my = _flat(x, y, c)
    out, back = [], []
    for m, (src_ref, land_ref) in enumerate(zip(src_refs, land_refs)):
        for k, peer in enumerate(_peers(x, y, c)):
            src = src_ref if to_all else src_ref.at[_flat(*peer)]
            n = m * (N_DEV - 1) + k
            sems = dict(send_sem=send_sems.at[n], recv_sem=recv_sems.at[n], device_id=peer, device_id_type=_MESH)
            out.append(pltpu.make_async_remote_copy(src_ref=src, dst_ref=land_ref.at[my], **sems))
            back.append(pltpu.make_async_remote_copy(src_ref=src, dst_ref=land_ref.at[_flat(*peer)], **sems))
    return out, back


def _exchange_start(srcs, to_all, after, name):
    n = len(srcs)
    n_sems = n * (N_DEV - 1)
    land_shapes = [(N_DEV,) + tuple(a.shape[-2:]) for a in srcs]

    def body(*refs):
        src_refs, land_refs = refs[:n], refs[n:2 * n]
        send_sems, recv_sems = refs[2 * n + 1], refs[2 * n + 2]
        token = refs[-1]
        for cp in _direct_copies(src_refs, land_refs, send_sems, recv_sems, to_all)[0]:
            cp.start()
        token[...] = jnp.zeros_like(token)

    lands = [pltpu.with_memory_space_constraint(lax.empty(shp, a.dtype), pltpu.HBM) for shp, a in zip(land_shapes, srcs)]
    outs = pl.pallas_call(
        body, name=name,
        out_shape=(pltpu.SemaphoreType.DMA((n_sems,)), pltpu.SemaphoreType.DMA((n_sems,)),
                   *[pltpu.HBM(a.shape, a.dtype) for a in srcs], *[pltpu.HBM(shp, a.dtype) for shp, a in zip(land_shapes, srcs)],
                   jax.ShapeDtypeStruct((8, 128), _F32)),
        in_specs=(_HBM,) * (2 * n) + (_ANY,),
        out_specs=(_SEM, _SEM) + (_HBM,) * (2 * n) + (pl.BlockSpec(memory_space=pltpu.VMEM),),
        input_output_aliases={i: 2 + i for i in range(2 * n)},
        compiler_params=pltpu.CompilerParams(has_side_effects=_DATAFLOW),
    )(*[pltpu.with_memory_space_constraint(a, pltpu.HBM) for a in srcs], *lands, after)
    return (outs[0], outs[1], outs[2:2 + n], outs[2 + n:2 + 2 * n], to_all), outs[-1]


def _exchange_wait(handle, after, name):
    send_sems, recv_sems, srcs, lands, to_all = handle
    n = len(srcs)

    def body(*refs):
        out, back = _direct_copies(refs[:n], refs[n:2 * n], refs[2 * n], refs[2 * n + 1], to_all)
        for cp in out:
            cp.wait_send()
        for cp in back:
            cp.wait_recv()

    outs = pl.pallas_call(
        body, name=name,
        out_shape=tuple(pltpu.HBM(a.shape, a.dtype) for a in (*srcs, *lands)),
        in_specs=(_HBM,) * (2 * n) + (_SEM, _SEM, _ANY), out_specs=(_HBM,) * (2 * n),
        input_output_aliases={i: i for i in range(2 * n)},
        compiler_params=pltpu.CompilerParams(has_side_effects=_DATAFLOW),
    )(*srcs, *lands, send_sems, recv_sems, after)
    return outs[:n], outs[n:]


def _with_own(landed, own):
    my = _flat(*_position())
    return lax.dynamic_update_slice(landed, own[None], (my, 0, 0))


def _sum_slabs(slabs, own, name):
    n, r, c_ = slabs.shape
    tr = _pick_tile(r, 16, max(16, (12 << 20) // (n * c_ * slabs.dtype.itemsize)))

    def body(x_ref, own_ref, o_ref):
        my = _flat(*_position())
        acc = None
        for k in range(n):
            term = jnp.where(my == k, own_ref[...], x_ref[k]).astype(_F32)
            acc = term if acc is None else acc + term
        o_ref[...] = acc

    return pl.pallas_call(
        body, name=name, grid=(r // tr,),
        in_specs=[pl.BlockSpec((n, tr, c_), lambda i: (0, i, 0)), pl.BlockSpec((tr, c_), lambda i: (i, 0))],
        out_specs=pl.BlockSpec((tr, c_), lambda i: (i, 0)),
        out_shape=jax.ShapeDtypeStruct((r, c_), _F32),
        compiler_params=_params("parallel"),
    )(slabs, own)


def _adamw(w, g, m, v, name):
    r, c_ = w.shape
    tr = _pick_tile(r, 8, 512)

    def body(w_ref, g_ref, m_ref, v_ref, d_ref, nm_ref, nv_ref):
        _adamw_refs(w_ref, g_ref, m_ref, v_ref, d_ref, nm_ref, nv_ref)

    spec = pl.BlockSpec((tr, c_), lambda i: (i, 0))
    shape = jax.ShapeDtypeStruct((r, c_), _F32)
    return pl.pallas_call(
        body, name=name, grid=(r // tr,), in_specs=[spec] * 4, out_specs=[spec] * 3, out_shape=[shape] * 3,
        compiler_params=_params("parallel"),
    )(w, g, m, v)


def _adamw_refs(w_ref, g_ref, m_ref, v_ref, d_ref, nm_ref, nv_ref):
    gv = g_ref[...]
    nm = ADAM_B1 * m_ref[...] + (1.0 - ADAM_B1) * gv
    nv = ADAM_B2 * v_ref[...] + (1.0 - ADAM_B2) * (gv * gv)
    m_hat = nm / (1.0 - ADAM_B1 ** ADAM_STEP)
    v_hat = nv / (1.0 - ADAM_B2 ** ADAM_STEP)
    d_ref[...] = -ADAM_LR * (m_hat / (jnp.sqrt(v_hat) + ADAM_EPS) + ADAM_WD * w_ref[...])
    nm_ref[...] = nm
    nv_ref[...] = nv


def _adamw_small(params, name):
    n = len(params)

    def body(*refs):
        for i in range(n):
            _adamw_refs(*refs[4 * i:4 * i + 4], *refs[4 * n + 3 * i:4 * n + 3 * i + 3])

    outs = pl.pallas_call(
        body, name=name,
        out_shape=[jax.ShapeDtypeStruct(p[0].shape, _F32) for p in params for _ in range(3)],
        compiler_params=pltpu.CompilerParams(vmem_limit_bytes=VMEM_LIMIT),
    )(*[a for p in params for a in p])
    return [tuple(outs[3 * i:3 * i + 3]) for i in range(n)]


_SMALL = ("norm1_g", "conv_a_w", "conv_b_w", "conv_b_b", "ln_b_g", "ln_b_b", "ln_c_g", "ln_c_b", "sgu_w", "sgu_b",
          "norm2_g", "conv_f_w", "final_g")
_CONV_SHARDED = ("conv_a_w", "conv_b_w", "conv_f_w")
_NAMES = ("norm1_g", "w_in", "conv_a_w", "conv_b_w", "conv_b_b", "ln_b_g", "ln_b_b", "ln_c_g", "ln_c_b", "sgu_w", "sgu_b",
          "w_out", "norm2_g", "w_up", "conv_f_w", "w_down", "final_g")


def _pack_rows(parts, lanes=128, row_multiple=8):
    flat = jnp.concatenate([a.reshape(-1) for a in parts])
    rows = -(-flat.shape[0] // lanes)
    rows = -(-rows // row_multiple) * row_multiple
    return jnp.pad(flat, (0, rows * lanes - flat.shape[0])).reshape(rows, lanes)


def _unpack_rows(packed, shapes):
    flat = packed.reshape(-1)
    out, off = [], 0
    for shp in shapes:
        size = 1
        for s in shp:
            size *= s
        out.append(flat[off:off + size].reshape(shp))
        off += size
    return out


def _gather_conv_weights(conv_a_w, conv_b_w, conv_f_w, after):
    shards = (conv_a_w, conv_b_w, conv_f_w)
    flat = _all_gather(_pack_rows(shards), "gather_conv_weights", after).reshape(N_DEV, -1)
    full, off = [], 0
    for s in shards:
        layers, taps, width = s.shape
        per_dev = flat[:, off:off + s.size].reshape(N_DEV, layers, taps, width)
        full.append(jnp.moveaxis(per_dev, 0, 2).reshape(layers, taps, N_DEV * width))
        off += s.size
    return full


def kernel(x, norm1_g, w_in, conv_a_w, conv_b_w, conv_b_b, ln_b_g, ln_b_b, ln_c_g, ln_c_b, sgu_w, sgu_b, w_out, norm2_g, w_up, conv_f_w, w_down, final_g, loss_target, m_norm1_g, m_w_in, m_conv_a_w, m_conv_b_w, m_conv_b_b, m_ln_b_g, m_ln_b_b, m_ln_c_g, m_ln_c_b, m_sgu_w, m_sgu_b, m_w_out, m_norm2_g, m_w_up, m_conv_f_w, m_w_down, m_final_g, v_norm1_g, v_w_in, v_conv_a_w, v_conv_b_w, v_conv_b_b, v_ln_b_g, v_ln_b_b, v_ln_c_g, v_ln_c_b, v_sgu_w, v_sgu_b, v_w_out, v_norm2_g, v_w_up, v_conv_f_w, v_w_down, v_final_g):
    weights = dict(norm1_g=norm1_g, w_in=w_in, conv_a_w=conv_a_w, conv_b_w=conv_b_w, conv_b_b=conv_b_b, ln_b_g=ln_b_g,
                   ln_b_b=ln_b_b, ln_c_g=ln_c_g, ln_c_b=ln_c_b, sgu_w=sgu_w, sgu_b=sgu_b, w_out=w_out, norm2_g=norm2_g,
                   w_up=w_up, conv_f_w=conv_f_w, w_down=w_down, final_g=final_g)
    mom1 = dict(norm1_g=m_norm1_g, w_in=m_w_in, conv_a_w=m_conv_a_w, conv_b_w=m_conv_b_w, conv_b_b=m_conv_b_b,
                ln_b_g=m_ln_b_g, ln_b_b=m_ln_b_b, ln_c_g=m_ln_c_g, ln_c_b=m_ln_c_b, sgu_w=m_sgu_w, sgu_b=m_sgu_b,
                w_out=m_w_out, norm2_g=m_norm2_g, w_up=m_w_up, conv_f_w=m_conv_f_w, w_down=m_w_down, final_g=m_final_g)
    mom2 = dict(norm1_g=v_norm1_g, w_in=v_w_in, conv_a_w=v_conv_a_w, conv_b_w=v_conv_b_w, conv_b_b=v_conv_b_b,
                ln_b_g=v_ln_b_g, ln_b_b=v_ln_b_b, ln_c_g=v_ln_c_g, ln_c_b=v_ln_c_b, sgu_w=v_sgu_w, sgu_b=v_sgu_b,
                w_out=v_w_out, norm2_g=v_norm2_g, w_up=v_w_up, conv_f_w=v_conv_f_w, w_down=v_w_down, final_g=v_final_g)
    n_ex, seq, d = x.shape
    depth = w_in.shape[0]
    assert depth == 2
    my = _flat(*_position())
    row = lambda a, l: a[l][None]
    tied = lambda a, token: a + token[0:1, 0:1]

    slab = {"w_in": [_bf(jnp.swapaxes(w_in, 1, 2)[l]) for l in range(depth)], "w_out": [_bf(w_out[l]) for l in range(depth)],
            "w_up": [_bf(jnp.swapaxes(w_up, 1, 2)[l]) for l in range(depth)], "w_down": [_bf(w_down[l]) for l in range(depth)]}
    rows = {name: parts[0].shape[0] for name, parts in slab.items()}
    key_of = {"w_in": "w_in_t", "w_out": "w_out", "w_up": "w_up_t", "w_down": "w_down"}
    rest_layer0 = [("w_out", 0), ("w_up", 0), ("w_down", 0)]
    all_layer1 = [("w_in", 1), ("w_out", 1), ("w_up", 1), ("w_down", 1)]

    w_in0 = _all_gather(slab["w_in"][0], "gather_w_in0", norm1_g)
    conv_a_full, conv_b_full, conv_f_full = _gather_conv_weights(conv_a_w, conv_b_w, conv_f_w, w_in0)
    gather0, token = _exchange_start([slab[n][l] for n, l in rest_layer0], True, conv_f_full, "gather_layer0_start")
    w = {
        "norm1_g": [row(norm1_g, l) for l in range(depth)], "w_in_t": [None] * depth,
        "conv_a_w": [conv_a_full[l] for l in range(depth)], "conv_b_w": [conv_b_full[l] for l in range(depth)],
        "conv_b_b": [row(conv_b_b, l) for l in range(depth)], "ln_b_g": [row(ln_b_g, l) for l in range(depth)],
        "ln_b_b": [row(ln_b_b, l) for l in range(depth)], "ln_c_g": [row(ln_c_g, l) for l in range(depth)],
        "ln_c_b": [row(ln_c_b, l) for l in range(depth)], "sgu_w": [sgu_w[l] for l in range(depth)],
        "sgu_b_full": [jnp.repeat(sgu_b[l].T, HEAD_DIM, axis=1) for l in range(depth)],
        "w_out": [None] * depth, "norm2_g": [row(norm2_g, l) for l in range(depth)], "w_up_t": [None] * depth,
        "conv_f_w": [conv_f_full[l] for l in range(depth)], "w_down": [None] * depth, "final_g": final_g[None],
    }
    w["w_in_t"][0] = w_in0.reshape(N_DEV * rows["w_in"], d)
    w["norm1_g"][0] = tied(row(norm1_g, 0), token)

    def land_weights(handle, after, which, name):
        owns, landed = _exchange_wait(handle, after, name)
        for (n, l), own, got in zip(which, owns, landed):
            w[key_of[n]][l] = _with_own(got, own).reshape(N_DEV * rows[n], d)
        return landed[0]

    st0 = _fwd_mix(x.reshape(n_ex * seq, d), w, 0, n_ex)
    landed0 = land_weights(gather0, st0["mix"][3], rest_layer0, "gather_layer0_wait")
    gather1, token = _exchange_start([slab[n][l] for n, l in all_layer1], True, landed0, "gather_layer1_start")
    w["norm2_g"][0] = tied(row(norm2_g, 0), token)
    x_mid = _fwd_ffn(st0, w, 0, n_ex)
    land_weights(gather1, x_mid, all_layer1, "gather_layer1_wait")
    st1 = _fwd_mix(x_mid, w, 1, n_ex)
    x_out = _fwd_ffn(st1, w, 1, n_ex)
    dx, dxb, d_final_g, loss = _final_loss(x_out, w["final_g"], loss_target.reshape(n_ex * seq, d), "final_loss")
    loss = lax.psum(loss[0, 0], ("x", "y", "c"))

    def send_grads(g, which, after, name):
        return _exchange_start([g[key_of[n]].reshape(N_DEV, rows[n], d) for n, _ in which], False, after, name)

    dx, dxb, g_ffn1 = _bwd_ffn(st1, w, 1, dx, dxb, n_ex)
    dx, dxb, g_mix1 = _bwd_mix(st1, w, 1, dx, dxb, n_ex)
    grads1, token = send_grads({**g_ffn1, **g_mix1}, all_layer1, dx, "exchange_layer1_start")
    w["norm2_g"][0] = tied(row(norm2_g, 0), token)
    dx, dxb, g_ffn0 = _bwd_ffn(st0, w, 0, dx, dxb, n_ex)
    g_ffn0["w_out"] = _mm_tn_parts(st0["mix"], dxb, "out_proj_dw")
    ffn_layer0 = [("w_out", 0), ("w_up", 0), ("w_down", 0)]
    grads0a, token = send_grads(g_ffn0, ffn_layer0, dxb, "exchange_ffn0_start")
    dmix = _mm_nt(dxb, w["w_out"][0], "out_proj_dx", after=token)
    dx, dxb, g_mix0, dp0 = _bwd_mixers(st0, w, 0, dx, dmix, n_ex)
    grad_x = dx.reshape(n_ex, seq, d)
    g = {k: [{**g_ffn0, **g_mix0}[k], {**g_ffn1, **g_mix1}[k]] for k in g_mix0.keys() | g_ffn0.keys()}
    g["final_g"] = d_final_g

    small_local = {
        "norm1_g": jnp.stack([a[0] for a in g["norm1_g"]]), "conv_a_w": jnp.stack(g["conv_a_w"]),
        "conv_b_w": jnp.stack(g["conv_b_w"]), "conv_b_b": jnp.stack([a[0] for a in g["conv_b_b"]]),
        "ln_b_g": jnp.stack([a[0] for a in g["ln_b_g"]]), "ln_b_b": jnp.stack([a[0] for a in g["ln_b_b"]]),
        "ln_c_g": jnp.stack([a[0] for a in g["ln_c_g"]]), "ln_c_b": jnp.stack([a[0] for a in g["ln_c_b"]]),
        "sgu_w": jnp.stack(g["sgu_w"]), "sgu_b": jnp.stack([a.T for a in g["sgu_b_t"]]),
        "norm2_g": jnp.stack([a[0] for a in g["norm2_g"]]), "conv_f_w": jnp.stack(g["conv_f_w"]),
        "final_g": g["final_g"][0],
    }
    small, token = _exchange_start([_pack_rows([small_local[k] for k in _SMALL])], True, dx, "gather_small_start")
    g_mix0["w_in_t"] = _mm_tn_parts(dp0, st0["h1"], "in_proj_dw", after=token)
    mix_layer0 = [("w_in", 0)]
    grads0b, token = send_grads(g_mix0, mix_layer0, dx, "exchange_mix0_start")

    reduced = {}

    def land_grads(handle, after, which, name):
        sent, landed = _exchange_wait(handle, after, name + "_wait")
        for (n, l), src, got in zip(which, sent, landed):
            own = lax.dynamic_index_in_dim(src, my, 0, keepdims=False)
            reduced[(n, l)] = _sum_slabs(got, own, name + "_sum_" + n)
        return reduced[which[-1]]

    def stacked_grad(name):
        stacked = jnp.stack([reduced[(name, l)] for l in range(depth)])
        return jnp.swapaxes(stacked, 1, 2) if name in ("w_in", "w_up") else stacked

    done = land_grads(grads1, token, all_layer1, "exchange_layer1")
    land_grads(grads0a, done, ffn_layer0, "exchange_ffn0")
    grads = {name: stacked_grad(name) for name in ("w_out", "w_up", "w_down")}

    delta, new_m, new_v = {}, {}, {}

    def as_2d(name):
        shp = weights[name].shape
        two_d = (-1, shp[-1]) if len(shp) > 1 else (1, shp[0])
        return tuple(a.reshape(two_d) for a in (weights[name], grads[name], mom1[name], mom2[name]))

    def keep(name, outs):
        delta[name], new_m[name], new_v[name] = (o.reshape(weights[name].shape) for o in outs)

    for name in ("w_up", "w_down", "w_out"):
        keep(name, _adamw(*as_2d(name), "adamw_" + name))

    (own,), (landed,) = _exchange_wait(small, new_v["w_out"], "gather_small_wait")
    small_sum = _sum_slabs(landed, own, "sum_small_grads")
    for name, total in zip(_SMALL, _unpack_rows(small_sum, [small_local[k].shape for k in _SMALL])):
        if name in _CONV_SHARDED:
            width = weights[name].shape[-1]
            total = lax.dynamic_slice_in_dim(total, my * width, width, axis=-1)
        grads[name] = total
    at_least_2d = lambda a: a[None] if a.ndim == 1 else a
    small_params = [tuple(at_least_2d(a) for a in (weights[n], grads[n], mom1[n], mom2[n])) for n in _SMALL]
    for name, outs in zip(_SMALL, _adamw_small(small_params, "adamw_small")):
        keep(name, outs)

    land_grads(grads0b, new_v["final_g"], mix_layer0, "exchange_mix0")
    grads["w_in"] = stacked_grad("w_in")
    keep("w_in", _adamw(*as_2d("w_in"), "adamw_w_in"))

    return (loss, grad_x, *[grads[n] for n in _NAMES], *[delta[n] for n in _NAMES], *[new_m[n] for n in _NAMES],
            *[new_v[n] for n in _NAMES])
```

```python
import jax
import jax.numpy as jnp
from jax import lax
from jax.experimental import pallas as pl
from jax.experimental.pallas import tpu as pltpu

_F32 = jnp.float32
_BF = jnp.bfloat16

HEAD_DIM = 64
MIX_W = 256
N_HEADS = MIX_W // HEAD_DIM
CHUNK = 128
KV_BLOCK = 128
K_SHORT = 3
K_CONF = 31
K_FFN = 3
RMS_EPS = 1e-6
LN_EPS = 1e-5
ADAM_LR = 0.001
ADAM_B1 = 0.9
ADAM_B2 = 0.999
ADAM_EPS = 1e-08
ADAM_WD = 0.01
ADAM_STEP = 10
N_DEV = 8
VMEM_LIMIT = 56 * 1024 * 1024


def _bf(x):
    return x.astype(_BF)


def _ld(ref):
    return ref[...].astype(_F32)


_ANY_SPEC = pl.BlockSpec(memory_space=pl.ANY)


def _params(*sem):
    return pltpu.CompilerParams(dimension_semantics=sem, vmem_limit_bytes=VMEM_LIMIT)


def _dot(a, b):
    return jnp.dot(a, b, preferred_element_type=_F32)


def _dot_nt(a, b):
    return lax.dot_general(a, b, (((1,), (1,)), ((), ())), preferred_element_type=_F32)


def _dot_tn(a, b):
    return lax.dot_general(a, b, (((0,), (0,)), ((), ())), preferred_element_type=_F32)


def _row_tile(t, want):
    return want if t % want == 0 else t


def _pick_tile(rows, unit, max_rows):
    best = 0
    for cand in range(unit, min(rows, max_rows) + 1, unit):
        if rows % cand == 0:
            best = cand
    return best or rows


def _sigmoid(x):
    return 1.0 / (1.0 + jnp.exp(-x))


def _rms_rstd(x):
    return lax.rsqrt(jnp.mean(x * x, axis=-1, keepdims=True) + RMS_EPS)


def _norm_mm(x, g, w_t, name):
    t, d = x.shape
    n = w_t.shape[0]
    tm = _row_tile(t, 512)
    tn = _row_tile(n, 512)

    def body(x_ref, g_ref, w_ref, p_ref, h_ref):
        xv = x_ref[...]
        h = _bf(xv * _rms_rstd(xv) * g_ref[...])
        h_ref[...] = h
        for n0 in range(0, n, tn):
            p_ref[:, n0:n0 + tn] = _bf(_dot_nt(h, w_ref[n0:n0 + tn, :]))

    return pl.pallas_call(
        body, name=name, grid=(t // tm,),
        in_specs=[pl.BlockSpec((tm, d), lambda i: (i, 0)), pl.BlockSpec((1, d), lambda i: (0, 0)),
                  pl.BlockSpec((n, d), lambda i: (0, 0))],
        out_specs=[pl.BlockSpec((tm, n), lambda i: (i, 0)), pl.BlockSpec((tm, d), lambda i: (i, 0))],
        out_shape=[jax.ShapeDtypeStruct((t, n), _BF), jax.ShapeDtypeStruct((t, d), _BF)],
        compiler_params=_params("parallel"),
    )(x, g, w_t)


def _mm_nt(a, w_t, name, after=None):
    t, k = a.shape
    n = w_t.shape[0]
    tm = _row_tile(t, 512)
    tn = _row_tile(n, 512) if n % 512 == 0 else _row_tile(n, 256)

    def body(a_ref, w_ref, *rest):
        o_ref = rest[-1]
        av = a_ref[...]
        for n0 in range(0, n, tn):
            o_ref[:, n0:n0 + tn] = _bf(_dot_nt(av, w_ref[n0:n0 + tn, :]))

    extra = () if after is None else (after,)
    return pl.pallas_call(
        body, name=name, grid=(t // tm,),
        in_specs=[pl.BlockSpec((tm, k), lambda i: (i, 0)), pl.BlockSpec((n, k), lambda i: (0, 0))] + [_ANY_SPEC] * len(extra),
        out_specs=pl.BlockSpec((tm, n), lambda i: (i, 0)),
        out_shape=jax.ShapeDtypeStruct((t, n), _BF),
        compiler_params=_params("parallel"),
    )(a, w_t, *extra)


def _mm_res(parts, w, x, name):
    t = x.shape[0]
    k, d = w.shape
    tm = _row_tile(t, 512)
    widths = [a.shape[1] for a in parts]
    n_parts = len(parts)

    def body(*refs):
        w_ref, x_ref, o_ref = refs[n_parts:]
        acc, off = x_ref[...], 0
        for a_ref, width in zip(refs[:n_parts], widths):
            acc = acc + _dot(a_ref[...], w_ref[off:off + width, :])
            off += width
        o_ref[...] = acc

    return pl.pallas_call(
        body, name=name, grid=(t // tm,),
        in_specs=[pl.BlockSpec((tm, width), lambda i: (i, 0)) for width in widths] + [
            pl.BlockSpec((k, d), lambda i: (0, 0)), pl.BlockSpec((tm, d), lambda i: (i, 0))],
        out_specs=pl.BlockSpec((tm, d), lambda i: (i, 0)),
        out_shape=jax.ShapeDtypeStruct((t, d), _F32),
        compiler_params=_params("parallel"),
    )(*parts, w, x)


def _res_norm_mm(parts, w_res, x, g, w_t, name):
    t, d = x.shape
    k = w_res.shape[0]
    n = w_t.shape[0]
    tm = _row_tile(t, 512)
    tn = _row_tile(n, 512)
    widths = [a.shape[1] for a in parts]
    n_parts = len(parts)

    def body(*refs):
        wr_ref, x_ref, g_ref, wt_ref, xo_ref, p_ref, h_ref = refs[n_parts:]
        xv, off = x_ref[...], 0
        for a_ref, width in zip(refs[:n_parts], widths):
            xv = xv + _dot(a_ref[...], wr_ref[off:off + width, :])
            off += width
        xo_ref[...] = xv
        h = _bf(xv * _rms_rstd(xv) * g_ref[...])
        h_ref[...] = h
        for n0 in range(0, n, tn):
            p_ref[:, n0:n0 + tn] = _bf(_dot_nt(h, wt_ref[n0:n0 + tn, :]))

    row = lambda width: pl.BlockSpec((tm, width), lambda i: (i, 0))
    const = lambda shape: pl.BlockSpec(shape, lambda i: (0, 0))
    return pl.pallas_call(
        body, name=name, grid=(t // tm,),
        in_specs=[row(width) for width in widths] + [const((k, d)), row(d), const((1, d)), const((n, d))],
        out_specs=[row(d), row(n), row(d)],
        out_shape=[jax.ShapeDtypeStruct((t, d), _F32), jax.ShapeDtypeStruct((t, n), _BF), jax.ShapeDtypeStruct((t, d), _BF)],
        compiler_params=_params("parallel"),
    )(*parts, w_res, x, g, w_t)


def _res_final_loss(parts, w_res, x, g, target, name):
    t, d = x.shape
    k = w_res.shape[0]
    tm = _row_tile(t, 256)
    widths = [a.shape[1] for a in parts]
    n_parts = len(parts)

    def body(*refs):
        wr_ref, x_ref, g_ref, t_ref, dx_ref, dxb_ref, dg_ref, loss_ref = refs[n_parts:]
        xv, off = x_ref[...], 0
        for a_ref, width in zip(refs[:n_parts], widths):
            xv = xv + _dot(a_ref[...], wr_ref[off:off + width, :])
            off += width
        _loss_head(xv, g_ref, t_ref, dx_ref, dxb_ref, dg_ref, loss_ref, d)

    row = lambda width: pl.BlockSpec((tm, width), lambda i: (i, 0))
    const = lambda shape: pl.BlockSpec(shape, lambda i: (0, 0))
    return pl.pallas_call(
        body, name=name, grid=(t // tm,),
        in_specs=[row(width) for width in widths] + [const((k, d)), row(d), const((1, d)), row(d)],
        out_specs=[row(d), row(d), const((1, d)), const((1, 1))],
        out_shape=[jax.ShapeDtypeStruct((t, d), _F32), jax.ShapeDtypeStruct((t, d), _BF),
                   jax.ShapeDtypeStruct((1, d), _F32), jax.ShapeDtypeStruct((1, 1), _F32)],
        compiler_params=_params("arbitrary"),
    )(*parts, w_res, x, g, target)


def _loss_head(xv, g_ref, t_ref, dx_ref, dxb_ref, dg_ref, loss_ref, d):
    rstd = _rms_rstd(xv)
    xn = xv * rstd
    err = xn * g_ref[...] - t_ref[...]
    dy = err * (1.0 / d)
    u = dy * g_ref[...]
    dx = rstd * (u - xn * jnp.mean(u * xn, axis=-1, keepdims=True))
    dx_ref[...] = dx
    dxb_ref[...] = _bf(dx)

    @pl.when(pl.program_id(0) == 0)
    def _():
        dg_ref[...] = jnp.zeros_like(dg_ref)
        loss_ref[...] = jnp.zeros_like(loss_ref)

    dg_ref[...] += jnp.sum(dy * xn, axis=0, keepdims=True)
    loss_ref[...] += (0.5 / d) * jnp.sum(jnp.sum(err * err, axis=1, keepdims=True), axis=0, keepdims=True)


def _mm_normbwd(parts, w, x, g, dres, name):
    t = x.shape[0]
    k, d = w.shape
    tm = _row_tile(t, 512)
    widths = [a.shape[1] for a in parts]
    n_parts = len(parts)

    def body(*refs):
        a_refs = refs[:n_parts]
        w_ref, x_ref, g_ref, r_ref, dx_ref, dxb_ref, dg_ref = refs[n_parts:]
        dh, off = None, 0
        for a_ref, width in zip(a_refs, widths):
            term = _dot(_bf(a_ref[...]), w_ref[off:off + width, :])
            dh = term if dh is None else dh + term
            off += width
        xv = x_ref[...]
        rstd = _rms_rstd(xv)
        xn = xv * rstd
        u = dh * g_ref[...]
        dx = r_ref[...] + rstd * (u - xn * jnp.mean(u * xn, axis=-1, keepdims=True))
        dx_ref[...] = dx
        dxb_ref[...] = _bf(dx)

        @pl.when(pl.program_id(0) == 0)
        def _():
            dg_ref[...] = jnp.zeros_like(dg_ref)

        dg_ref[...] += jnp.sum(dh * xn, axis=0, keepdims=True)

    return pl.pallas_call(
        body, name=name, grid=(t // tm,),
        in_specs=[pl.BlockSpec((tm, width), lambda i: (i, 0)) for width in widths] + [
            pl.BlockSpec((k, d), lambda i: (0, 0)),
            pl.BlockSpec((tm, d), lambda i: (i, 0)), pl.BlockSpec((1, d), lambda i: (0, 0)),
            pl.BlockSpec((tm, d), lambda i: (i, 0))],
        out_specs=[pl.BlockSpec((tm, d), lambda i: (i, 0)), pl.BlockSpec((tm, d), lambda i: (i, 0)),
                   pl.BlockSpec((1, d), lambda i: (0, 0))],
        out_shape=[jax.ShapeDtypeStruct((t, d), _F32), jax.ShapeDtypeStruct((t, d), _BF),
                   jax.ShapeDtypeStruct((1, d), _F32)],
        compiler_params=_params("arbitrary"),
    )(*parts, w, x, g, dres)


def _mm_tn(a, b, name, out_dtype):
    t, m = a.shape
    n = b.shape[1]
    tm = _pick_tile(m, 128, 1408)
    tn = _pick_tile(n, 128, 1024)
    tk = _row_tile(t, 1024)
    nk = t // tk

    def body(a_ref, b_ref, o_ref, acc):
        kk = pl.program_id(2)

        @pl.when(kk == 0)
        def _():
            acc[...] = jnp.zeros_like(acc)

        acc[...] += _dot_tn(_bf(a_ref[...]), b_ref[...])

        @pl.when(kk == nk - 1)
        def _():
            o_ref[...] = acc[...].astype(o_ref.dtype)

    return pl.pallas_call(
        body, name=name, grid=(m // tm, n // tn, nk),
        in_specs=[pl.BlockSpec((tk, tm), lambda i, j, kk: (kk, i)), pl.BlockSpec((tk, tn), lambda i, j, kk: (kk, j))],
        out_specs=pl.BlockSpec((tm, tn), lambda i, j, kk: (i, j)),
        out_shape=jax.ShapeDtypeStruct((m, n), out_dtype),
        scratch_shapes=[pltpu.VMEM((tm, tn), _F32)],
        compiler_params=_params("parallel", "parallel", "arbitrary"),
    )(a, b)


def _mm_tn_halves(a0, a1, b, name):
    t, m = a0.shape
    n = b.shape[1]
    tm = _pick_tile(m, 128, 1408)
    tn = _pick_tile(n, 128, 1024)
    tk = _row_tile(t, 1024)
    nk = t // tk
    half = m // tm

    def body(a0_ref, a1_ref, b_ref, o_ref, acc):
        i = pl.program_id(0)
        kk = pl.program_id(2)

        @pl.when(kk == 0)
        def _():
            acc[...] = jnp.zeros_like(acc)

        @pl.when(i < half)
        def _():
            acc[...] += _dot_tn(a0_ref[...], b_ref[...])

        @pl.when(i >= half)
        def _():
            acc[...] += _dot_tn(a1_ref[...], b_ref[...])

        @pl.when(kk == nk - 1)
        def _():
            o_ref[...] = _bf(acc[...])

    return pl.pallas_call(
        body, name=name, grid=(2 * half, n // tn, nk),
        in_specs=[pl.BlockSpec((tk, tm), lambda i, j, kk: (jnp.where(i < half, kk, 0), jnp.minimum(i, half - 1))),
                  pl.BlockSpec((tk, tm), lambda i, j, kk: (jnp.where(i >= half, kk, 0), jnp.maximum(i - half, 0))),
                  pl.BlockSpec((tk, tn), lambda i, j, kk: (kk, j))],
        out_specs=pl.BlockSpec((tm, tn), lambda i, j, kk: (i, j)),
        out_shape=jax.ShapeDtypeStruct((2 * m, n), _BF),
        scratch_shapes=[pltpu.VMEM((tm, tn), _F32)],
        compiler_params=_params("parallel", "parallel", "arbitrary"),
    )(a0, a1, b)


def _mm_tn_parts(parts, b, name, after=None):
    t, n = b.shape
    widths = [a.shape[1] for a in parts]
    m = sum(widths)
    n_parts = len(parts)
    tk = _row_tile(t, 1024)
    nk = t // tk
    extra = () if after is None else (after,)

    def body(*refs):
        b_ref = refs[n_parts]
        o_ref, acc = refs[-2:]
        kk = pl.program_id(0)

        @pl.when(kk == 0)
        def _():
            acc[...] = jnp.zeros_like(acc)

        bv = b_ref[...]
        off = 0
        for a_ref, width in zip(refs[:n_parts], widths):
            acc[off:off + width, :] += _dot_tn(_bf(a_ref[...]), bv)
            off += width

        @pl.when(kk == nk - 1)
        def _():
            o_ref[...] = _bf(acc[...])

    return pl.pallas_call(
        body, name=name, grid=(nk,),
        in_specs=[pl.BlockSpec((tk, width), lambda kk: (kk, 0)) for width in widths] + [pl.BlockSpec((tk, n), lambda kk: (kk, 0))]
        + [_ANY_SPEC] * len(extra),
        out_specs=pl.BlockSpec((m, n), lambda kk: (0, 0)),
        out_shape=jax.ShapeDtypeStruct((m, n), _BF),
        scratch_shapes=[pltpu.VMEM((m, n), _F32)],
        compiler_params=_params("arbitrary"),
    )(*parts, b, *extra)


def _pad_rows(x, pad):
    return jnp.concatenate([x, jnp.zeros((pad, x.shape[1]), x.dtype)], axis=0)


def _shift_down(xp, s):
    return xp if s == 0 else pltpu.roll(xp, s, 0)


def _shift_up(xp, s):
    return xp if s == 0 else pltpu.roll(xp, xp.shape[0] - s, 0)


def _taps3(xp):
    one = _shift_down(xp, 1)
    return xp, one, _shift_down(one, 1)


def _conv3_taps(taps, w_ref):
    return w_ref[2:3, :] * taps[0] + w_ref[1:2, :] * taps[1] + w_ref[0:1, :] * taps[2]


def _conv3(xp, w_ref):
    return _conv3_taps(_taps3(xp), w_ref)


def _conv3_t(dyp, w_ref):
    one = _shift_up(dyp, 1)
    return w_ref[2:3, :] * dyp + w_ref[1:2, :] * one + w_ref[0:1, :] * _shift_up(one, 1)


def _conv3_dw(dyp, taps):
    return [jnp.sum(dyp * taps[2 - k], axis=0, keepdims=True) for k in range(3)]


def _ffn_mid_fwd(up_pre, wf, n_ex, name):
    t, f2 = up_pre.shape
    f = f2 // 2
    s = t // n_ex
    cb = MIX_W
    nb = f // cb

    def body(ug_ref, uv_ref, wg_ref, wv_ref, act_ref, gf_ref, vf_ref):
        gf = _conv3(_pad_rows(ug_ref[...].astype(_F32), 8), wg_ref)[:s]
        vf = _conv3(_pad_rows(uv_ref[...].astype(_F32), 8), wv_ref)[:s]
        act_ref[...] = _bf(gf * _sigmoid(gf) * vf)
        gf_ref[...] = _bf(gf)
        vf_ref[...] = _bf(vf)

    out = pl.BlockSpec((s, cb), lambda e, j: (e, j))
    return pl.pallas_call(
        body, name=name, grid=(n_ex, nb),
        in_specs=[pl.BlockSpec((s, cb), lambda e, j: (e, j)), pl.BlockSpec((s, cb), lambda e, j: (e, j + nb)),
                  pl.BlockSpec((K_FFN, cb), lambda e, j: (0, j)), pl.BlockSpec((K_FFN, cb), lambda e, j: (0, j + nb))],
        out_specs=[out, out, out],
        out_shape=[jax.ShapeDtypeStruct((t, f), _BF)] * 3,
        compiler_params=_params("parallel", "parallel"),
    )(up_pre, up_pre, wf, wf)


def _ffn_mid_bwd(up_pre, conv_g, conv_v, wf, dact, n_ex, name):
    t, f2 = up_pre.shape
    f = f2 // 2
    s = t // n_ex
    cb = MIX_W
    nb = f // cb

    def body(ug_ref, uv_ref, gf_ref, vf_ref, wg_ref, wv_ref, da_ref, dug_ref, duv_ref, dwg_ref, dwv_ref):
        gf = _ld(gf_ref)
        vf = _ld(vf_ref)
        sg = _sigmoid(gf)
        da = _ld(da_ref)

        @pl.when(pl.program_id(1) == 0)
        def _():
            dwg_ref[...] = jnp.zeros_like(dwg_ref)
            dwv_ref[...] = jnp.zeros_like(dwv_ref)

        def finish(dpost, w_ref, x_ref, du_ref, dw_ref):
            ahead = [_pad_rows(dpost, 8)]
            ahead.append(_shift_up(ahead[0], 1))
            ahead.append(_shift_up(ahead[1], 1))
            du_ref[...] = _bf((w_ref[2:3, :] * ahead[0] + w_ref[1:2, :] * ahead[1] + w_ref[0:1, :] * ahead[2])[:s])
            x = _ld(x_ref)
            for k in range(K_FFN):
                dw_ref[k:k + 1, :] += jnp.sum(ahead[2 - k][:s] * x, axis=0, keepdims=True)

        finish(da * vf * sg * (1.0 + gf * (1.0 - sg)), wg_ref, ug_ref, dug_ref, dwg_ref)
        finish(da * gf * sg, wv_ref, uv_ref, duv_ref, dwv_ref)

    return pl.pallas_call(
        body, name=name, grid=(nb, n_ex),
        in_specs=[pl.BlockSpec((s, cb), lambda j, e: (e, j)), pl.BlockSpec((s, cb), lambda j, e: (e, j + nb)),
                  pl.BlockSpec((s, cb), lambda j, e: (e, j)), pl.BlockSpec((s, cb), lambda j, e: (e, j)),
                  pl.BlockSpec((K_FFN, cb), lambda j, e: (0, j)), pl.BlockSpec((K_FFN, cb), lambda j, e: (0, j + nb)),
                  pl.BlockSpec((s, cb), lambda j, e: (e, j))],
        out_specs=[pl.BlockSpec((s, cb), lambda j, e: (e, j)), pl.BlockSpec((s, cb), lambda j, e: (e, j)),
                   pl.BlockSpec((K_FFN, cb), lambda j, e: (0, j)), pl.BlockSpec((K_FFN, cb), lambda j, e: (0, j))],
        out_shape=[jax.ShapeDtypeStruct((t, f), _BF), jax.ShapeDtypeStruct((t, f), _BF),
                   jax.ShapeDtypeStruct((K_FFN, f), _F32), jax.ShapeDtypeStruct((K_FFN, f), _F32)],
        compiler_params=_params("parallel", "arbitrary"),
    )(up_pre, up_pre, conv_g, conv_v, wf, wf, dact)


def _pcol(s, j):
    return pl.BlockSpec((s, MIX_W), lambda e, j=j: (e, j))


def _vec(rows=1):
    return pl.BlockSpec((rows, MIX_W), lambda e: (0, 0))


def _mix_a_fwd(p, wa, n_ex, name):
    t = p.shape[0]
    s = t // n_ex

    def body(gb_ref, gc_ref, ha_ref, w_ref, y_ref):
        cv = _conv3(_pad_rows(_ld(gc_ref) * _ld(ha_ref), 8), w_ref)[:s]
        y_ref[...] = _bf(_ld(gb_ref) * cv)

    return pl.pallas_call(
        body, name=name, grid=(n_ex,),
        in_specs=[_pcol(s, 0), _pcol(s, 1), _pcol(s, 2), _vec(K_SHORT)],
        out_specs=pl.BlockSpec((s, MIX_W), lambda e: (e, 0)),
        out_shape=jax.ShapeDtypeStruct((t, MIX_W), _BF),
        compiler_params=_params("parallel"),
    )(p, p, p, wa)


def _mix_a_bwd(p, wa, dmix, n_ex, name):
    t = p.shape[0]
    s = t // n_ex

    def body(gb_ref, gc_ref, ha_ref, w_ref, dy_ref, dp_ref, dw_ref):
        gc = _ld(gc_ref)
        ha = _ld(ha_ref)
        up = _taps3(_pad_rows(gc * ha, 8))
        cv = _conv3_taps(up, w_ref)[:s]
        dy = _ld(dy_ref)
        dcvp = _pad_rows(dy * _ld(gb_ref), 8)
        du = _conv3_t(dcvp, w_ref)[:s]
        dp_ref[:, 0:MIX_W] = _bf(dy * cv)
        dp_ref[:, MIX_W:2 * MIX_W] = _bf(du * ha)
        dp_ref[:, 2 * MIX_W:3 * MIX_W] = _bf(du * gc)

        @pl.when(pl.program_id(0) == 0)
        def _():
            dw_ref[...] = jnp.zeros_like(dw_ref)

        rows = _conv3_dw(dcvp, up)
        for k in range(3):
            dw_ref[k:k + 1, :] += rows[k]

    return pl.pallas_call(
        body, name=name, grid=(n_ex,),
        in_specs=[_pcol(s, 0), _pcol(s, 1), _pcol(s, 2), _vec(K_SHORT), _pcol(s, 0)],
        out_specs=[pl.BlockSpec((s, 3 * MIX_W), lambda e: (e, 0)), _vec(K_SHORT)],
        out_shape=[jax.ShapeDtypeStruct((t, 3 * MIX_W), _BF), jax.ShapeDtypeStruct((K_SHORT, MIX_W), _F32)],
        compiler_params=_params("arbitrary"),
    )(p, p, p, wa, dmix)


CONF_PAD = 32
CONF_ROWS = 64
_CONF_LANES = (slice(0, 128), slice(128, 256))


def _conf_taps(win, ahead):
    n = CONF_ROWS + CONF_PAD
    for b in range(8):
        rot = win if b == 0 else pltpu.roll(win, (n - b) if ahead else b, 0)
        for a in range(4):
            if 8 * a + b < K_CONF:
                yield rot, 8 * a + b, (8 * a) if ahead else (CONF_PAD - 8 * a)


def _ln_fwd(x, g, b):
    mu = jnp.mean(x, axis=-1, keepdims=True)
    xc = x - mu
    rstd = lax.rsqrt(jnp.mean(xc * xc, axis=-1, keepdims=True) + LN_EPS)
    xhat = xc * rstd
    return xhat * g + b, xhat, rstd


def _ln_bwd(dy, xhat, rstd, g):
    dxh = dy * g
    return rstd * (dxh - jnp.mean(dxh, axis=-1, keepdims=True) - xhat * jnp.mean(dxh * xhat, axis=-1, keepdims=True))


def _mix_b_fwd(p, wb, bb, lg, lb, n_ex, name):
    t = p.shape[0]
    s = t // n_ex

    def body(val_ref, gat_ref, w_ref, bb_ref, lg_ref, lb_ref, y_ref, cb_ref, xpad):
        xpad[0:CONF_PAD, :] = jnp.zeros((CONF_PAD, MIX_W), _F32)
        xpad[CONF_PAD:, :] = _ld(val_ref) * _sigmoid(_ld(gat_ref))

        def chunk(c, carry):
            r0 = pl.multiple_of(c * CONF_ROWS, CONF_ROWS)
            for lanes in _CONF_LANES:
                acc = None
                for rot, sh, lo in _conf_taps(xpad[pl.ds(r0, CONF_ROWS + CONF_PAD), lanes], False):
                    term = w_ref[K_CONF - 1 - sh:K_CONF - sh, lanes] * rot[lo:lo + CONF_ROWS]
                    acc = term if acc is None else acc + term
                cb_ref[pl.ds(r0, CONF_ROWS), lanes] = acc + bb_ref[:, lanes]
            return carry

        lax.fori_loop(0, s // CONF_ROWS, chunk, 0)
        yl, _, _ = _ln_fwd(cb_ref[...], lg_ref[...], lb_ref[...])
        y_ref[...] = _bf(yl * _sigmoid(yl))

    return pl.pallas_call(
        body, name=name, grid=(n_ex,),
        in_specs=[_pcol(s, 3), _pcol(s, 4), _vec(K_CONF), _vec(), _vec(), _vec()],
        out_specs=[pl.BlockSpec((s, MIX_W), lambda e: (e, 0)), pl.BlockSpec((s, MIX_W), lambda e: (e, 0))],
        out_shape=[jax.ShapeDtypeStruct((t, MIX_W), _BF), jax.ShapeDtypeStruct((t, MIX_W), _F32)],
        scratch_shapes=[pltpu.VMEM((CONF_PAD + s, MIX_W), _F32)],
        compiler_params=_params("parallel"),
    )(p, p, wb, bb, lg, lb)


def _mix_b_bwd(p, cb, wb, lg, lb, dmix, n_ex, name):
    t = p.shape[0]
    s = t // n_ex

    def body(val_ref, gat_ref, cb_ref, w_ref, lg_ref, lb_ref, dy_ref, dp_ref, dw_ref, dbb_ref, dlg_ref, dlb_ref,
             xpad, dpad, dglu_s, dw_acc):
        @pl.when(pl.program_id(0) == 0)
        def _():
            for r in (dw_ref, dbb_ref, dlg_ref, dlb_ref):
                r[...] = jnp.zeros_like(r)

        yl, xhat, rstd = _ln_fwd(cb_ref[...], lg_ref[...], lb_ref[...])
        sy = _sigmoid(yl)
        dyl = _ld(dy_ref) * sy * (1.0 + yl * (1.0 - sy))
        dlg_ref[...] += jnp.sum(dyl * xhat, axis=0, keepdims=True)
        dlb_ref[...] += jnp.sum(dyl, axis=0, keepdims=True)
        dcb = _ln_bwd(dyl, xhat, rstd, lg_ref[...])
        dbb_ref[...] += jnp.sum(dcb, axis=0, keepdims=True)

        val = _ld(val_ref)
        sg = _sigmoid(_ld(gat_ref))
        xpad[0:CONF_PAD, :] = jnp.zeros((CONF_PAD, MIX_W), _F32)
        xpad[CONF_PAD:, :] = val * sg
        dpad[0:s, :] = dcb
        dpad[s:, :] = jnp.zeros((CONF_PAD, MIX_W), _F32)
        dw_acc[...] = jnp.zeros_like(dw_acc)

        def chunk(c, carry):
            r0 = pl.multiple_of(c * CONF_ROWS, CONF_ROWS)
            for lanes in _CONF_LANES:
                d_win = dpad[pl.ds(r0, CONF_ROWS + CONF_PAD), lanes]
                d_rows = d_win[0:CONF_ROWS]
                acc = None
                for rot, sh, lo in _conf_taps(d_win, True):
                    term = w_ref[K_CONF - 1 - sh:K_CONF - sh, lanes] * rot[lo:lo + CONF_ROWS]
                    acc = term if acc is None else acc + term
                dglu_s[pl.ds(r0, CONF_ROWS), lanes] = acc
                for rot, sh, lo in _conf_taps(xpad[pl.ds(r0, CONF_ROWS + CONF_PAD), lanes], False):
                    prod = d_rows * rot[lo:lo + CONF_ROWS]
                    dw_acc[K_CONF - 1 - sh, :, lanes] += jnp.sum(prod.reshape(CONF_ROWS // 8, 8, 128), axis=0)
            return carry

        lax.fori_loop(0, s // CONF_ROWS, chunk, 0)
        dw_ref[...] += jnp.sum(dw_acc[...], axis=1)
        dglu = dglu_s[...]
        dp_ref[:, 0:MIX_W] = _bf(dglu * sg)
        dp_ref[:, MIX_W:2 * MIX_W] = _bf(dglu * val * sg * (1.0 - sg))

    return pl.pallas_call(
        body, name=name, grid=(n_ex,),
        in_specs=[_pcol(s, 3), _pcol(s, 4), pl.BlockSpec((s, MIX_W), lambda e: (e, 0)), _vec(K_CONF), _vec(), _vec(),
                  _pcol(s, 1)],
        out_specs=[pl.BlockSpec((s, 2 * MIX_W), lambda e: (e, 0)), _vec(K_CONF), _vec(), _vec(), _vec()],
        out_shape=[jax.ShapeDtypeStruct((t, 2 * MIX_W), _BF), jax.ShapeDtypeStruct((K_CONF, MIX_W), _F32),
                   jax.ShapeDtypeStruct((1, MIX_W), _F32), jax.ShapeDtypeStruct((1, MIX_W), _F32),
                   jax.ShapeDtypeStruct((1, MIX_W), _F32)],
        scratch_shapes=[pltpu.VMEM((CONF_PAD + s, MIX_W), _F32), pltpu.VMEM((s + CONF_PAD, MIX_W), _F32),
                        pltpu.VMEM((s, MIX_W), _F32), pltpu.VMEM((K_CONF, 8, MIX_W), _F32)],
        compiler_params=_params("arbitrary"),
    )(p, p, cb, wb, lg, lb, dmix)


_INV_SQRT2 = 0.7071067811865476
_INV_SQRT2PI = 0.3989422804014327


def _gelu(x):
    return 0.5 * x * (1.0 + lax.erf(x * _INV_SQRT2))


def _gelu_grad(x):
    return 0.5 * (1.0 + lax.erf(x * _INV_SQRT2)) + x * _INV_SQRT2PI * jnp.exp(-0.5 * x * x)


def _head_masks(width=MIX_W):
    lane = lax.broadcasted_iota(jnp.int32, (1, width), 1)
    return [(lane >= h * HEAD_DIM) & (lane < (h + 1) * HEAD_DIM) for h in range(N_HEADS)]


def _tril_mask():
    r = lax.broadcasted_iota(jnp.int32, (CHUNK, CHUNK), 0)
    c = lax.broadcasted_iota(jnp.int32, (CHUNK, CHUNK), 1)
    return c <= r


def _sgu_apply(ws_ref, x3, transpose):
    n = x3.shape[0]
    tril = _tril_mask()
    masks = _head_masks()
    xb = _bf(x3)
    out = jnp.zeros(x3.shape, _F32)
    for h in range(N_HEADS):
        w = _bf(jnp.where(tril, ws_ref[h], 0.0))
        wb = jnp.broadcast_to(w[None], (n, CHUNK, CHUNK))
        dims = (((1,), (1,)), ((0,), (0,))) if transpose else (((2,), (1,)), ((0,), (0,)))
        r = lax.dot_general(wb, xb, dims, preferred_element_type=_F32)
        out = out + jnp.where(masks[h][None], r, 0.0)
    return out


def _mix_c_fwd(p, lg, lb, ws, sb_full, n_ex, name):
    t = p.shape[0]
    s = t // n_ex
    nc = s // CHUNK

    def body(pu_ref, pv_ref, lg_ref, lb_ref, ws_ref, sb_ref, y_ref):
        u = _gelu(_ld(pu_ref))
        vl, _, _ = _ln_fwd(_gelu(_ld(pv_ref)), lg_ref[...], lb_ref[...])
        sp = _sgu_apply(ws_ref, vl.reshape(nc, CHUNK, MIX_W), False) + sb_ref[...][None]
        y_ref[...] = _bf(u * sp.reshape(s, MIX_W))

    return pl.pallas_call(
        body, name=name, grid=(n_ex,),
        in_specs=[_pcol(s, 5), _pcol(s, 6), _vec(), _vec(),
                  pl.BlockSpec((N_HEADS, CHUNK, CHUNK), lambda e: (0, 0, 0)), pl.BlockSpec((CHUNK, MIX_W), lambda e: (0, 0))],
        out_specs=pl.BlockSpec((s, MIX_W), lambda e: (e, 0)),
        out_shape=jax.ShapeDtypeStruct((t, MIX_W), _BF),
        compiler_params=_params("parallel"),
    )(p, p, lg, lb, ws, sb_full)


def _mix_c_bwd(p, lg, lb, ws, sb_full, dmix, n_ex, name):
    t = p.shape[0]
    s = t // n_ex
    nc = s // CHUNK

    def body(pu_ref, pv_ref, lg_ref, lb_ref, ws_ref, sb_ref, dy_ref, dp_ref, dlg_ref, dlb_ref, dws_ref, dsb_ref):
        @pl.when(pl.program_id(0) == 0)
        def _():
            for r in (dlg_ref, dlb_ref, dws_ref, dsb_ref):
                r[...] = jnp.zeros_like(r)

        pu = _ld(pu_ref)
        pv = _ld(pv_ref)
        u = _gelu(pu)
        vl, xhat, rstd = _ln_fwd(_gelu(pv), lg_ref[...], lb_ref[...])
        vl3 = vl.reshape(nc, CHUNK, MIX_W)
        sp = _sgu_apply(ws_ref, vl3, False) + sb_ref[...][None]
        dy = _ld(dy_ref)
        dp_ref[:, 0:MIX_W] = _bf(dy * sp.reshape(s, MIX_W) * _gelu_grad(pu))
        dsp3 = (dy * u).reshape(nc, CHUNK, MIX_W)
        dsb_full = jnp.sum(dsp3, axis=0)
        masks = _head_masks()
        tril = _tril_mask()
        dspb = _bf(dsp3)
        vlb = _bf(vl3)
        for h in range(N_HEADS):
            dsb_ref[:, h:h + 1] += jnp.sum(jnp.where(masks[h], dsb_full, 0.0), axis=1, keepdims=True)
            dm = jnp.where(masks[h][None], dspb, jnp.zeros_like(dspb))
            g3 = lax.dot_general(dm, vlb, (((2,), (2,)), ((0,), (0,))), preferred_element_type=_F32)
            dws_ref[h] += jnp.where(tril, jnp.sum(g3, axis=0), 0.0)
        dvl = _sgu_apply(ws_ref, dsp3, True).reshape(s, MIX_W)
        dlg_ref[...] += jnp.sum(dvl * xhat, axis=0, keepdims=True)
        dlb_ref[...] += jnp.sum(dvl, axis=0, keepdims=True)
        dp_ref[:, MIX_W:2 * MIX_W] = _bf(_ln_bwd(dvl, xhat, rstd, lg_ref[...]) * _gelu_grad(pv))

    return pl.pallas_call(
        body, name=name, grid=(n_ex,),
        in_specs=[_pcol(s, 5), _pcol(s, 6), _vec(), _vec(),
                  pl.BlockSpec((N_HEADS, CHUNK, CHUNK), lambda e: (0, 0, 0)), pl.BlockSpec((CHUNK, MIX_W), lambda e: (0, 0)),
                  _pcol(s, 2)],
        out_specs=[pl.BlockSpec((s, 2 * MIX_W), lambda e: (e, 0)), _vec(), _vec(),
                   pl.BlockSpec((N_HEADS, CHUNK, CHUNK), lambda e: (0, 0, 0)), pl.BlockSpec((CHUNK, N_HEADS), lambda e: (0, 0))],
        out_shape=[jax.ShapeDtypeStruct((t, 2 * MIX_W), _BF), jax.ShapeDtypeStruct((1, MIX_W), _F32),
                   jax.ShapeDtypeStruct((1, MIX_W), _F32), jax.ShapeDtypeStruct((N_HEADS, CHUNK, CHUNK), _F32),
                   jax.ShapeDtypeStruct((CHUNK, N_HEADS), _F32)],
        compiler_params=_params("arbitrary"),
    )(p, p, lg, lb, ws, sb_full, dmix)


D_QBLOCK = 256
HEAD_COLS = N_HEADS * KV_BLOCK


def _stack_heads(x3):
    return jnp.stack([_bf(jnp.where(m[None], x3, 0.0)) for m in _head_masks()], axis=1)


def _stack_heads_rows(x):
    return jnp.concatenate([_bf(jnp.where(m, x, 0.0)) for m in _head_masks()], axis=0)


def _cols_to_rows(x):
    return jnp.concatenate([x[:, h * KV_BLOCK:(h + 1) * KV_BLOCK] for h in range(N_HEADS)], axis=0)


def _head_sums(x):
    return [jnp.sum(x[:, h * KV_BLOCK:(h + 1) * KV_BLOCK], axis=1, keepdims=True) for h in range(N_HEADS)]


def _spread(cols):
    tq = cols[0].shape[0]
    return jnp.concatenate([jnp.broadcast_to(c, (tq, KV_BLOCK)) for c in cols], axis=1)


def _pair_dot(x, m2):
    half = 2 * KV_BLOCK
    xb = _bf(x)
    return jnp.concatenate([_dot(xb[:, :half], m2), _dot(xb[:, half:], m2)], axis=1)


def _tri2(lower):
    n = 2 * KV_BLOCK
    r = lax.broadcasted_iota(jnp.int32, (n, n), 0)
    c = lax.broadcasted_iota(jnp.int32, (n, n), 1)
    same = (r >= KV_BLOCK) == (c >= KV_BLOCK)
    return _bf(jnp.where(same & (r > c if lower else r < c), 1.0, 0.0))


def _sb_scores(qs, kc, j, t_idx, on_diagonal):
    z = _dot_nt(qs, kc)
    lb = jnp.minimum(z, 0.0) - jnp.log(1.0 + jnp.exp(-jnp.abs(z)))
    if not on_diagonal:
        return (lambda x: x), lb, lb - z
    lane = lax.broadcasted_iota(jnp.int32, (1, HEAD_COLS), 1)
    valid = (j * KV_BLOCK + (lane & (KV_BLOCK - 1))) < t_idx
    keep = lambda x: jnp.where(valid, x, 0.0)
    return keep, lb, keep(lb - z)


RUN_LANES = 128


def _run_lane(j, h):
    return lax.broadcasted_iota(jnp.int32, (1, RUN_LANES), 1) == j * N_HEADS + h


def _d_qblock(s):
    return D_QBLOCK if s % D_QBLOCK == 0 else KV_BLOCK


def _mix_d_fwd(p, n_ex, name):
    t = p.shape[0]
    s = t // n_ex
    tq = _d_qblock(s)
    nq = s // tq
    r = tq // KV_BLOCK
    nb = s // KV_BLOCK
    assert nb * N_HEADS <= RUN_LANES

    def body(q_ref, k_ref, v_ref, y_ref, runs_ref, kc, vc):
        i = pl.program_id(1)

        @pl.when(i == 0)
        def _():
            kc[...] = _stack_heads(k_ref[...].reshape(nb, KV_BLOCK, MIX_W))
            vc[...] = _stack_heads(v_ref[...].reshape(nb, KV_BLOCK, MIX_W))

        qs = _bf(_ld(q_ref) * (HEAD_DIM ** -0.5))
        t_idx = i * tq + lax.broadcasted_iota(jnp.int32, (tq, 1), 0)
        after_m = _tri2(True)
        nkb = (i + 1) * r

        runs_ref[...] = jnp.zeros_like(runs_ref)

        def one_block(j, runs, acc, on_diagonal):
            keep, lb, c = _sb_scores(qs, kc[j].reshape(HEAD_COLS, MIX_W), j, t_idx, on_diagonal)
            a = keep(jnp.exp(lb + _pair_dot(c, after_m) + _spread(runs)))
            acc = acc + _dot(_bf(a), vc[j].reshape(HEAD_COLS, MIX_W))
            kept = runs_ref[...]
            for h in range(N_HEADS):
                kept = jnp.where(_run_lane(j, h), runs[h], kept)
            runs_ref[...] = kept
            return tuple(ru + cs for ru, cs in zip(runs, _head_sums(c))), acc

        def trip(last, carry, on_diagonal):
            runs, acc = carry
            for sub in range(r):
                runs, acc = one_block(last - sub, runs, acc, on_diagonal)
            return runs, acc

        zero = jnp.zeros((tq, 1), _F32)
        carry = trip(nkb - 1, ((zero,) * N_HEADS, jnp.zeros((tq, MIX_W), _F32)), True)
        below = lambda m: nkb - 1 - (m + 1) * r
        carry = lax.fori_loop(0, i // 2, lambda m, carry: trip(below(2 * m + 1), trip(below(2 * m), carry, False), False), carry)
        _, acc = lax.fori_loop(0, i % 2, lambda m, carry: trip(below(i - 1), carry, False), carry)
        y_ref[...] = _bf(acc)

    return pl.pallas_call(
        body, name=name, grid=(n_ex, nq),
        in_specs=[pl.BlockSpec((tq, MIX_W), lambda e, i: (e * nq + i, 7)), pl.BlockSpec((s, MIX_W), lambda e, i: (e, 8)),
                  pl.BlockSpec((s, MIX_W), lambda e, i: (e, 9))],
        out_specs=[pl.BlockSpec((tq, MIX_W), lambda e, i: (e * nq + i, 0)),
                   pl.BlockSpec((tq, RUN_LANES), lambda e, i: (e * nq + i, 0))],
        out_shape=[jax.ShapeDtypeStruct((t, MIX_W), _BF), jax.ShapeDtypeStruct((t, RUN_LANES), _F32)],
        scratch_shapes=[pltpu.VMEM((nb, N_HEADS, KV_BLOCK, MIX_W), _BF), pltpu.VMEM((nb, N_HEADS, KV_BLOCK, MIX_W), _BF)],
        compiler_params=_params("parallel", "arbitrary"),
    )(p, p, p)


def _mix_d_bwd(p, kept_runs, dmix, n_ex, name):
    t = p.shape[0]
    s = t // n_ex
    tq = _d_qblock(s)
    nq = s // tq
    r = tq // KV_BLOCK
    nb = s // KV_BLOCK
    scale = HEAD_DIM ** -0.5

    def body(q_ref, k_ref, v_ref, runs_ref, do_ref, dq_ref, dk_ref, dv_ref, kc, vc):
        i = pl.program_id(1)

        @pl.when(i == 0)
        def _():
            kc[...] = _stack_heads(k_ref[...].reshape(nb, KV_BLOCK, MIX_W))
            vc[...] = _stack_heads(v_ref[...].reshape(nb, KV_BLOCK, MIX_W))
            dk_ref[...] = jnp.zeros_like(dk_ref)
            dv_ref[...] = jnp.zeros_like(dv_ref)

        q_scaled = _ld(q_ref) * scale
        qs = _bf(q_scaled)
        do = do_ref[...]
        dob = _bf(do)
        q_rows = _stack_heads_rows(q_scaled)
        do_rows = _stack_heads_rows(do)
        kept = runs_ref[...]
        t_idx = i * tq + lax.broadcasted_iota(jnp.int32, (tq, 1), 0)
        after_m = _tri2(True)
        before_m = _tri2(False)
        nkb = (i + 1) * r
        zero = jnp.zeros((tq, 1), _F32)

        def trip(first, carry, on_diagonal):
            for sub in range(r):
                carry = one_block(first + sub, carry, on_diagonal)
            return carry

        def one_block(j, carry, on_diagonal):
            pres, dq = carry
            rows = pl.ds(pl.multiple_of(j * KV_BLOCK, KV_BLOCK), KV_BLOCK)
            kj = kc[j].reshape(HEAD_COLS, MIX_W)
            keep, lb, c = _sb_scores(qs, kj, j, t_idx, on_diagonal)
            runs = [jnp.sum(jnp.where(_run_lane(j, h), kept, 0.0), axis=1, keepdims=True) for h in range(N_HEADS)]
            a = keep(jnp.exp(lb + _pair_dot(c, after_m) + _spread(runs)))
            g = a * _dot_nt(dob, vc[j].reshape(HEAD_COLS, MIX_W))
            before = _pair_dot(g, before_m) + _spread(pres)
            sig = jnp.exp(lb)
            dz = _bf(keep(g * (1.0 - sig) - sig * before))
            dk_ref[rows, :] += _dot_tn(_cols_to_rows(dz), q_rows)
            dv_ref[rows, :] += _dot_tn(_cols_to_rows(_bf(a)), do_rows)
            return tuple(pr + gs for pr, gs in zip(pres, _head_sums(g))), dq + _dot(dz, kj)

        init = ((zero,) * N_HEADS, jnp.zeros((tq, MIX_W), _F32))
        carry = lax.fori_loop(0, i // 2, lambda m, carry: trip((2 * m + 1) * r, trip(2 * m * r, carry, False), False), init)
        carry = lax.fori_loop(0, i % 2, lambda m, carry: trip((i - 1) * r, carry, False), carry)
        _, dq = trip(i * r, carry, True)
        dq_ref[...] = _bf(dq * scale)

    return pl.pallas_call(
        body, name=name, grid=(n_ex, nq),
        in_specs=[pl.BlockSpec((tq, MIX_W), lambda e, i: (e * nq + i, 7)), pl.BlockSpec((s, MIX_W), lambda e, i: (e, 8)),
                  pl.BlockSpec((s, MIX_W), lambda e, i: (e, 9)), pl.BlockSpec((tq, RUN_LANES), lambda e, i: (e * nq + i, 0)),
                  pl.BlockSpec((tq, MIX_W), lambda e, i: (e * nq + i, 3))],
        out_specs=[pl.BlockSpec((tq, MIX_W), lambda e, i: (e * nq + i, 0)), pl.BlockSpec((s, MIX_W), lambda e, i: (e, 0)),
                   pl.BlockSpec((s, MIX_W), lambda e, i: (e, 0))],
        out_shape=[jax.ShapeDtypeStruct((t, MIX_W), _BF), jax.ShapeDtypeStruct((t, MIX_W), _F32),
                   jax.ShapeDtypeStruct((t, MIX_W), _F32)],
        scratch_shapes=[pltpu.VMEM((nb, N_HEADS, KV_BLOCK, MIX_W), _BF), pltpu.VMEM((nb, N_HEADS, KV_BLOCK, MIX_W), _BF)],
        compiler_params=_params("parallel", "arbitrary"),
    )(p, p, p, kept_runs, dmix)


def _fwd_mix(x, w, l, n_ex):
    p, h1 = _norm_mm(x, w["norm1_g"][l], w["w_in_t"][l], "in_proj")
    y_a = _mix_a_fwd(p, w["conv_a_w"][l], n_ex, "mix_a_fwd")
    y_b, cb = _mix_b_fwd(p, w["conv_b_w"][l], w["conv_b_b"][l], w["ln_b_g"][l], w["ln_b_b"][l], n_ex, "mix_b_fwd")
    y_c = _mix_c_fwd(p, w["ln_c_g"][l], w["ln_c_b"][l], w["sgu_w"][l], w["sgu_b_full"][l], n_ex, "mix_c_fwd")
    y_d, runs_d = _mix_d_fwd(p, n_ex, "mix_d_fwd")
    return dict(x=x, h1=h1, p=p, cb=cb, runs_d=runs_d, mix=(y_a, y_b, y_c, y_d))


def _fwd_ffn(st, w, l, n_ex):
    x1, up_pre, h2 = _res_norm_mm(st["mix"], w["w_out"][l], st["x"], w["norm2_g"][l], w["w_up_t"][l], "out_up_proj")
    act, conv_g, conv_v = _ffn_mid_fwd(up_pre, w["conv_f_w"][l], n_ex, "ffn_mid_fwd")
    st.update(x1=x1, h2=h2, up_pre=up_pre, act=act, conv_g=conv_g, conv_v=conv_v)


def _down_proj(st, w, l):
    return _mm_res((st["act"],), w["w_down"][l], st["x1"], "down_proj")


def _down_proj_loss(st, w, l, target):
    return _res_final_loss((st["act"],), w["w_down"][l], st["x1"], w["final_g"], target, "down_proj_loss")


def _bwd_ffn(st, w, l, dx, dxb, n_ex):
    g = {}
    dact = _mm_nt(dxb, w["w_down"][l], "down_proj_dx")
    g["w_down"] = _mm_tn(st["act"], dxb, "down_proj_dw", _BF)
    dup_g, dup_v, dwf_g, dwf_v = _ffn_mid_bwd(
        st["up_pre"], st["conv_g"], st["conv_v"], w["conv_f_w"][l], dact, n_ex, "ffn_mid_bwd")
    g["conv_f_w"] = jnp.concatenate([dwf_g, dwf_v], axis=1)
    dx, dxb, g["norm2_g"] = _mm_normbwd((dup_g, dup_v), w["w_up_t"][l], st["x1"], w["norm2_g"][l], dx, "up_proj_dx")
    g["w_up_t"] = _mm_tn_halves(dup_g, dup_v, st["h2"], "up_proj_dw")
    return dx, dxb, g


def _bwd_out_proj(st, w, l, dxb):
    return _mm_nt(dxb, w["w_out"][l], "out_proj_dx"), _mm_tn_parts(st["mix"], dxb, "out_proj_dw")


def _bwd_mixers(st, w, l, dx, dmix, n_ex):
    g = {}
    p = st["p"]
    dp_a, g["conv_a_w"] = _mix_a_bwd(p, w["conv_a_w"][l], dmix, n_ex, "mix_a_bwd")
    dp_b, g["conv_b_w"], g["conv_b_b"], g["ln_b_g"], g["ln_b_b"] = _mix_b_bwd(
        p, st["cb"], w["conv_b_w"][l], w["ln_b_g"][l], w["ln_b_b"][l], dmix, n_ex, "mix_b_bwd")
    dp_c, g["ln_c_g"], g["ln_c_b"], g["sgu_w"], g["sgu_b_t"] = _mix_c_bwd(
        p, w["ln_c_g"][l], w["ln_c_b"][l], w["sgu_w"][l], w["sgu_b_full"][l], dmix, n_ex, "mix_c_bwd")
    dq, dk, dv = _mix_d_bwd(p, st["runs_d"], dmix, n_ex, "mix_d_bwd")
    dp = (dp_a, dp_b, dp_c, dq, dk, dv)
    dx, dxb, g["norm1_g"] = _mm_normbwd(dp, w["w_in_t"][l], st["x"], w["norm1_g"][l], dx, "in_proj_dx")
    return dx, dxb, g, dp


def _bwd_mix(st, w, l, dx, dxb, n_ex):
    dmix, dw_out = _bwd_out_proj(st, w, l, dxb)
    dx, dxb, g, dp = _bwd_mixers(st, w, l, dx, dmix, n_ex)
    g["w_out"] = dw_out
    g["w_in_t"] = _mm_tn_parts(dp, st["h1"], "in_proj_dw")
    return dx, dxb, g


_MESH = pl.DeviceIdType.MESH
_ANY = pl.BlockSpec(memory_space=pl.ANY)


def _position():
    return lax.axis_index("x"), lax.axis_index("y"), lax.axis_index("c")


def _flat(px, py, pc):
    return 4 * px + 2 * py + pc


def _all_gather(shard, name, after):
    r, c_ = shard.shape

    def body(x_ref, after_ref, out_ref, send_sems, recv_sems, local_sem):
        x, y, c = _position()
        me, sibling = (x, y, c), (x, y, 1 - c)
        chips = [(1 - x, y), (x, 1 - y), (1 - x, 1 - y)]

        def copy(k, block, to, src=None):
            slab = out_ref.at[_flat(*block)]
            return pltpu.make_async_remote_copy(
                src_ref=slab if src is None else src, dst_ref=slab, send_sem=send_sems.at[k], recv_sem=recv_sems.at[k],
                device_id=to, device_id_type=_MESH)

        mine = pltpu.make_async_copy(x_ref, out_ref.at[_flat(*me)], local_sem)
        mine.start()
        first = [copy(0, me, sibling, src=x_ref)]
        first += [copy(1 + j, me, (*chip, c), src=x_ref) for j, chip in enumerate(chips)]
        for cp in first:
            cp.start()
        passed = [copy(4 + j, (*chip, c), sibling) for j, chip in enumerate(chips)]
        for j, chip in enumerate(chips):
            copy(1 + j, (*chip, c), me).wait_recv()
            passed[j].start()
        copy(0, sibling, me).wait_recv()
        for j, chip in enumerate(chips):
            copy(4 + j, (*chip, 1 - c), me).wait_recv()
        for cp in first + passed:
            cp.wait_send()
        mine.wait()

    return pl.pallas_call(
        body, name=name, out_shape=jax.ShapeDtypeStruct((N_DEV, r, c_), shard.dtype),
        in_specs=[_ANY, _ANY], out_specs=_ANY,
        scratch_shapes=[pltpu.SemaphoreType.DMA((7,)), pltpu.SemaphoreType.DMA((7,)), pltpu.SemaphoreType.DMA],
    )(shard, after)


_HBM = pl.BlockSpec(memory_space=pltpu.HBM)
_SEM = pl.BlockSpec(memory_space=pltpu.SEMAPHORE)
_DATAFLOW = pltpu.SideEffectType.DATAFLOW_SIDE_EFFECTING


def _peers(x, y, c):
    return [((1 - x) if (k + 1) & 4 else x, (1 - y) if (k + 1) & 2 else y, (1 - c) if (k + 1) & 1 else c)
            for k in range(N_DEV - 1)]


def _direct_copies(src_refs, land_refs, send_sems, recv_sems, to_all):
    x, y, c = _position()
    my = _flat(x, y, c)
    out, back = [], []
    for m, (src_ref, land_ref) in enumerate(zip(src_refs, land_refs)):
        for k, peer in enumerate(_peers(x, y, c)):
            src = src_ref if to_all else src_ref.at[_flat(*peer)]
            n = m * (N_DEV - 1) + k
            sems = dict(send_sem=send_sems.at[n], recv_sem=recv_sems.at[n], device_id=peer, device_id_type=_MESH)
            out.append(pltpu.make_async_remote_copy(src_ref=src, dst_ref=land_ref.at[my], **sems))
            back.append(pltpu.make_async_remote_copy(src_ref=src, dst_ref=land_ref.at[_flat(*peer)], **sems))
    return out, back


def _exchange_start(srcs, to_all, after, name):
    n = len(srcs)
    n_sems = n * (N_DEV - 1)
    land_shapes = [(N_DEV,) + tuple(a.shape[-2:]) for a in srcs]

    def body(*refs):
        src_refs, land_refs = refs[:n], refs[n:2 * n]
        send_sems, recv_sems = refs[2 * n + 1], refs[2 * n + 2]
        token = refs[-1]
        for cp in _direct_copies(src_refs, land_refs, send_sems, recv_sems, to_all)[0]:
            cp.start()
        token[...] = jnp.zeros_like(token)

    lands = [pltpu.with_memory_space_constraint(lax.empty(shp, a.dtype), pltpu.HBM) for shp, a in zip(land_shapes, srcs)]
    outs = pl.pallas_call(
        body, name=name,
        out_shape=(pltpu.SemaphoreType.DMA((n_sems,)), pltpu.SemaphoreType.DMA((n_sems,)),
                   *[pltpu.HBM(a.shape, a.dtype) for a in srcs], *[pltpu.HBM(shp, a.dtype) for shp, a in zip(land_shapes, srcs)],
                   jax.ShapeDtypeStruct((8, 128), _F32)),
        in_specs=(_HBM,) * (2 * n) + (_ANY,),
        out_specs=(_SEM, _SEM) + (_HBM,) * (2 * n) + (pl.BlockSpec(memory_space=pltpu.VMEM),),
        input_output_aliases={i: 2 + i for i in range(2 * n)},
        compiler_params=pltpu.CompilerParams(has_side_effects=_DATAFLOW),
    )(*[pltpu.with_memory_space_constraint(a, pltpu.HBM) for a in srcs], *lands, after)
    return (outs[0], outs[1], outs[2:2 + n], outs[2 + n:2 + 2 * n], to_all), outs[-1]


def _exchange_wait(handle, after, name):
    send_sems, recv_sems, srcs, lands, to_all = handle
    n = len(srcs)

    def body(*refs):
        out, back = _direct_copies(refs[:n], refs[n:2 * n], refs[2 * n], refs[2 * n + 1], to_all)
        for cp in out:
            cp.wait_send()
        for cp in back:
            cp.wait_recv()

    outs = pl.pallas_call(
        body, name=name,
        out_shape=tuple(pltpu.HBM(a.shape, a.dtype) for a in (*srcs, *lands)),
        in_specs=(_HBM,) * (2 * n) + (_SEM, _SEM, _ANY), out_specs=(_HBM,) * (2 * n),
        input_output_aliases={i: i for i in range(2 * n)},
        compiler_params=pltpu.CompilerParams(has_side_effects=_DATAFLOW),
    )(*srcs, *lands, send_sems, recv_sems, after)
    return outs[:n], outs[n:]


def _with_own(landed, own):
    my = _flat(*_position())
    return lax.dynamic_update_slice(landed, own[None], (my, 0, 0))


def _sum_slabs(slabs, own, name):
    n, r, c_ = slabs.shape
    tr = _pick_tile(r, 16, max(16, (12 << 20) // (n * c_ * slabs.dtype.itemsize)))

    def body(x_ref, own_ref, o_ref):
        my = _flat(*_position())
        acc = None
        for k in range(n):
            term = jnp.where(my == k, own_ref[...], x_ref[k]).astype(_F32)
            acc = term if acc is None else acc + term
        o_ref[...] = acc

    return pl.pallas_call(
        body, name=name, grid=(r // tr,),
        in_specs=[pl.BlockSpec((n, tr, c_), lambda i: (0, i, 0)), pl.BlockSpec((tr, c_), lambda i: (i, 0))],
        out_specs=pl.BlockSpec((tr, c_), lambda i: (i, 0)),
        out_shape=jax.ShapeDtypeStruct((r, c_), _F32),
        compiler_params=_params("parallel"),
    )(slabs, own)


def _adamw(w, g, m, v, name):
    r, c_ = w.shape
    tr = _pick_tile(r, 8, 512)

    def body(w_ref, g_ref, m_ref, v_ref, d_ref, nm_ref, nv_ref):
        _adamw_refs(w_ref, g_ref, m_ref, v_ref, d_ref, nm_ref, nv_ref)

    spec = pl.BlockSpec((tr, c_), lambda i: (i, 0))
    shape = jax.ShapeDtypeStruct((r, c_), _F32)
    return pl.pallas_call(
        body, name=name, grid=(r // tr,), in_specs=[spec] * 4, out_specs=[spec] * 3, out_shape=[shape] * 3,
        compiler_params=_params("parallel"),
    )(w, g, m, v)


def _adamw_refs(w_ref, g_ref, m_ref, v_ref, d_ref, nm_ref, nv_ref):
    gv = g_ref[...]
    nm = ADAM_B1 * m_ref[...] + (1.0 - ADAM_B1) * gv
    nv = ADAM_B2 * v_ref[...] + (1.0 - ADAM_B2) * (gv * gv)
    m_hat = nm / (1.0 - ADAM_B1 ** ADAM_STEP)
    v_hat = nv / (1.0 - ADAM_B2 ** ADAM_STEP)
    d_ref[...] = -ADAM_LR * (m_hat / (jnp.sqrt(v_hat) + ADAM_EPS) + ADAM_WD * w_ref[...])
    nm_ref[...] = nm
    nv_ref[...] = nv


def _adamw_small(params, name):
    n = len(params)

    def body(*refs):
        for i in range(n):
            _adamw_refs(*refs[4 * i:4 * i + 4], *refs[4 * n + 3 * i:4 * n + 3 * i + 3])

    outs = pl.pallas_call(
        body, name=name,
        out_shape=[jax.ShapeDtypeStruct(p[0].shape, _F32) for p in params for _ in range(3)],
        compiler_params=pltpu.CompilerParams(vmem_limit_bytes=VMEM_LIMIT),
    )(*[a for p in params for a in p])
    return [tuple(outs[3 * i:3 * i + 3]) for i in range(n)]


_SMALL = ("norm1_g", "conv_a_w", "conv_b_w", "conv_b_b", "ln_b_g", "ln_b_b", "ln_c_g", "ln_c_b", "sgu_w", "sgu_b",
          "norm2_g", "conv_f_w", "final_g")
_CONV_SHARDED = ("conv_a_w", "conv_b_w", "conv_f_w")
_NAMES = ("norm1_g", "w_in", "conv_a_w", "conv_b_w", "conv_b_b", "ln_b_g", "ln_b_b", "ln_c_g", "ln_c_b", "sgu_w", "sgu_b",
          "w_out", "norm2_g", "w_up", "conv_f_w", "w_down", "final_g")


def _pack_rows(parts, lanes=128, row_multiple=8):
    flat = jnp.concatenate([a.reshape(-1) for a in parts])
    rows = -(-flat.shape[0] // lanes)
    rows = -(-rows // row_multiple) * row_multiple
    return jnp.pad(flat, (0, rows * lanes - flat.shape[0])).reshape(rows, lanes)


def _unpack_rows(packed, shapes):
    flat = packed.reshape(-1)
    out, off = [], 0
    for shp in shapes:
        size = 1
        for s in shp:
            size *= s
        out.append(flat[off:off + size].reshape(shp))
        off += size
    return out


def _gather_conv_weights(conv_a_w, conv_b_w, conv_f_w, after):
    shards = (conv_a_w, conv_b_w, conv_f_w)
    flat = _all_gather(_pack_rows(shards), "gather_conv_weights", after).reshape(N_DEV, -1)
    full, off = [], 0
    for s in shards:
        layers, taps, width = s.shape
        per_dev = flat[:, off:off + s.size].reshape(N_DEV, layers, taps, width)
        full.append(jnp.moveaxis(per_dev, 0, 2).reshape(layers, taps, N_DEV * width))
        off += s.size
    return full


def kernel(x, norm1_g, w_in, conv_a_w, conv_b_w, conv_b_b, ln_b_g, ln_b_b, ln_c_g, ln_c_b, sgu_w, sgu_b, w_out, norm2_g, w_up, conv_f_w, w_down, final_g, loss_target, m_norm1_g, m_w_in, m_conv_a_w, m_conv_b_w, m_conv_b_b, m_ln_b_g, m_ln_b_b, m_ln_c_g, m_ln_c_b, m_sgu_w, m_sgu_b, m_w_out, m_norm2_g, m_w_up, m_conv_f_w, m_w_down, m_final_g, v_norm1_g, v_w_in, v_conv_a_w, v_conv_b_w, v_conv_b_b, v_ln_b_g, v_ln_b_b, v_ln_c_g, v_ln_c_b, v_sgu_w, v_sgu_b, v_w_out, v_norm2_g, v_w_up, v_conv_f_w, v_w_down, v_final_g):
    weights = dict(norm1_g=norm1_g, w_in=w_in, conv_a_w=conv_a_w, conv_b_w=conv_b_w, conv_b_b=conv_b_b, ln_b_g=ln_b_g,
                   ln_b_b=ln_b_b, ln_c_g=ln_c_g, ln_c_b=ln_c_b, sgu_w=sgu_w, sgu_b=sgu_b, w_out=w_out, norm2_g=norm2_g,
                   w_up=w_up, conv_f_w=conv_f_w, w_down=w_down, final_g=final_g)
    mom1 = dict(norm1_g=m_norm1_g, w_in=m_w_in, conv_a_w=m_conv_a_w, conv_b_w=m_conv_b_w, conv_b_b=m_conv_b_b,
                ln_b_g=m_ln_b_g, ln_b_b=m_ln_b_b, ln_c_g=m_ln_c_g, ln_c_b=m_ln_c_b, sgu_w=m_sgu_w, sgu_b=m_sgu_b,
                w_out=m_w_out, norm2_g=m_norm2_g, w_up=m_w_up, conv_f_w=m_conv_f_w, w_down=m_w_down, final_g=m_final_g)
    mom2 = dict(norm1_g=v_norm1_g, w_in=v_w_in, conv_a_w=v_conv_a_w, conv_b_w=v_conv_b_w, conv_b_b=v_conv_b_b,
                ln_b_g=v_ln_b_g, ln_b_b=v_ln_b_b, ln_c_g=v_ln_c_g, ln_c_b=v_ln_c_b, sgu_w=v_sgu_w, sgu_b=v_sgu_b,
                w_out=v_w_out, norm2_g=v_norm2_g, w_up=v_w_up, conv_f_w=v_conv_f_w, w_down=v_w_down, final_g=v_final_g)
    n_ex, seq, d = x.shape
    depth = w_in.shape[0]
    assert depth == 2
    my = _flat(*_position())
    row = lambda a, l: a[l][None]
    tied = lambda a, token: a + token[0:1, 0:1]

    slab = {"w_in": [_bf(jnp.swapaxes(w_in, 1, 2)[l]) for l in range(depth)], "w_out": [_bf(w_out[l]) for l in range(depth)],
            "w_up": [_bf(jnp.swapaxes(w_up, 1, 2)[l]) for l in range(depth)], "w_down": [_bf(w_down[l]) for l in range(depth)]}
    rows = {name: parts[0].shape[0] for name, parts in slab.items()}
    key_of = {"w_in": "w_in_t", "w_out": "w_out", "w_up": "w_up_t", "w_down": "w_down"}
    rest_layer0 = [("w_out", 0), ("w_up", 0), ("w_down", 0)]
    all_layer1 = [("w_in", 1), ("w_out", 1), ("w_up", 1), ("w_down", 1)]

    w_in0 = _all_gather(slab["w_in"][0], "gather_w_in0", norm1_g)
    conv_a_full, conv_b_full, conv_f_full = _gather_conv_weights(conv_a_w, conv_b_w, conv_f_w, w_in0)
    gather0, token = _exchange_start([slab[n][l] for n, l in rest_layer0], True, conv_f_full, "gather_layer0_start")
    w = {
        "norm1_g": [row(norm1_g, l) for l in range(depth)], "w_in_t": [None] * depth,
        "conv_a_w": [conv_a_full[l] for l in range(depth)], "conv_b_w": [conv_b_full[l] for l in range(depth)],
        "conv_b_b": [row(conv_b_b, l) for l in range(depth)], "ln_b_g": [row(ln_b_g, l) for l in range(depth)],
        "ln_b_b": [row(ln_b_b, l) for l in range(depth)], "ln_c_g": [row(ln_c_g, l) for l in range(depth)],
        "ln_c_b": [row(ln_c_b, l) for l in range(depth)], "sgu_w": [sgu_w[l] for l in range(depth)],
        "sgu_b_full": [jnp.repeat(sgu_b[l].T, HEAD_DIM, axis=1) for l in range(depth)],
        "w_out": [None] * depth, "norm2_g": [row(norm2_g, l) for l in range(depth)], "w_up_t": [None] * depth,
        "conv_f_w": [conv_f_full[l] for l in range(depth)], "w_down": [None] * depth, "final_g": final_g[None],
    }
    w["w_in_t"][0] = w_in0.reshape(N_DEV * rows["w_in"], d)
    w["norm1_g"][0] = tied(row(norm1_g, 0), token)

    def land_weights(handle, after, which, name):
        owns, landed = _exchange_wait(handle, after, name)
        for (n, l), own, got in zip(which, owns, landed):
            w[key_of[n]][l] = _with_own(got, own).reshape(N_DEV * rows[n], d)
        return landed[0]

    st0 = _fwd_mix(x.reshape(n_ex * seq, d), w, 0, n_ex)
    landed0 = land_weights(gather0, st0["mix"][3], rest_layer0, "gather_layer0_wait")
    gather1, token = _exchange_start([slab[n][l] for n, l in all_layer1], True, landed0, "gather_layer1_start")
    w["norm2_g"][0] = tied(row(norm2_g, 0), token)
    _fwd_ffn(st0, w, 0, n_ex)
    x_mid = _down_proj(st0, w, 0)
    land_weights(gather1, x_mid, all_layer1, "gather_layer1_wait")
    st1 = _fwd_mix(x_mid, w, 1, n_ex)
    _fwd_ffn(st1, w, 1, n_ex)
    dx, dxb, d_final_g, loss = _down_proj_loss(st1, w, 1, loss_target.reshape(n_ex * seq, d))
    loss = lax.psum(loss[0, 0], ("x", "y", "c"))

    def send_grads(g, which, after, name):
        return _exchange_start([g[key_of[n]].reshape(N_DEV, rows[n], d) for n, _ in which], False, after, name)

    dx, dxb, g_ffn1 = _bwd_ffn(st1, w, 1, dx, dxb, n_ex)
    dx, dxb, g_mix1 = _bwd_mix(st1, w, 1, dx, dxb, n_ex)
    grads1, token = send_grads({**g_ffn1, **g_mix1}, all_layer1, dx, "exchange_layer1_start")
    w["norm2_g"][0] = tied(row(norm2_g, 0), token)
    dx, dxb, g_ffn0 = _bwd_ffn(st0, w, 0, dx, dxb, n_ex)
    g_ffn0["w_out"] = _mm_tn_parts(st0["mix"], dxb, "out_proj_dw")
    ffn_layer0 = [("w_out", 0), ("w_up", 0), ("w_down", 0)]
    grads0a, token = send_grads(g_ffn0, ffn_layer0, dxb, "exchange_ffn0_start")
    dmix = _mm_nt(dxb, w["w_out"][0], "out_proj_dx", after=token)
    dx, dxb, g_mix0, dp0 = _bwd_mixers(st0, w, 0, dx, dmix, n_ex)
    grad_x = dx.reshape(n_ex, seq, d)
    g = {k: [{**g_ffn0, **g_mix0}[k], {**g_ffn1, **g_mix1}[k]] for k in g_mix0.keys() | g_ffn0.keys()}
    g["final_g"] = d_final_g

    small_local = {
        "norm1_g": jnp.stack([a[0] for a in g["norm1_g"]]), "conv_a_w": jnp.stack(g["conv_a_w"]),
        "conv_b_w": jnp.stack(g["conv_b_w"]), "conv_b_b": jnp.stack([a[0] for a in g["conv_b_b"]]),
        "ln_b_g": jnp.stack([a[0] for a in g["ln_b_g"]]), "ln_b_b": jnp.stack([a[0] for a in g["ln_b_b"]]),
        "ln_c_g": jnp.stack([a[0] for a in g["ln_c_g"]]), "ln_c_b": jnp.stack([a[0] for a in g["ln_c_b"]]),
        "sgu_w": jnp.stack(g["sgu_w"]), "sgu_b": jnp.stack([a.T for a in g["sgu_b_t"]]),
        "norm2_g": jnp.stack([a[0] for a in g["norm2_g"]]), "conv_f_w": jnp.stack(g["conv_f_w"]),
        "final_g": g["final_g"][0],
    }
    small, token = _exchange_start([_pack_rows([small_local[k] for k in _SMALL])], True, dx, "gather_small_start")
    g_mix0["w_in_t"] = _mm_tn_parts(dp0, st0["h1"], "in_proj_dw", after=token)
    mix_layer0 = [("w_in", 0)]
    grads0b, token = send_grads(g_mix0, mix_layer0, dx, "exchange_mix0_start")

    reduced = {}

    def land_grads(handle, after, which, name):
        sent, landed = _exchange_wait(handle, after, name + "_wait")
        for (n, l), src, got in zip(which, sent, landed):
            own = lax.dynamic_index_in_dim(src, my, 0, keepdims=False)
            reduced[(n, l)] = _sum_slabs(got, own, name + "_sum_" + n)
        return reduced[which[-1]]

    def stacked_grad(name):
        stacked = jnp.stack([reduced[(name, l)] for l in range(depth)])
        return jnp.swapaxes(stacked, 1, 2) if name in ("w_in", "w_up") else stacked

    done = land_grads(grads1, token, all_layer1, "exchange_layer1")
    land_grads(grads0a, done, ffn_layer0, "exchange_ffn0")
    grads = {name: stacked_grad(name) for name in ("w_out", "w_up", "w_down")}

    delta, new_m, new_v = {}, {}, {}

    def as_2d(name):
        shp = weights[name].shape
        two_d = (-1, shp[-1]) if len(shp) > 1 else (1, shp[0])
        return tuple(a.reshape(two_d) for a in (weights[name], grads[name], mom1[name], mom2[name]))

    def keep(name, outs):
        delta[name], new_m[name], new_v[name] = (o.reshape(weights[name].shape) for o in outs)

    for name in ("w_up", "w_down", "w_out"):
        keep(name, _adamw(*as_2d(name), "adamw_" + name))

    (own,), (landed,) = _exchange_wait(small, new_v["w_out"], "gather_small_wait")
    small_sum = _sum_slabs(landed, own, "sum_small_grads")
    for name, total in zip(_SMALL, _unpack_rows(small_sum, [small_local[k].shape for k in _SMALL])):
        if name in _CONV_SHARDED:
            width = weights[name].shape[-1]
            total = lax.dynamic_slice_in_dim(total, my * width, width, axis=-1)
        grads[name] = total
    at_least_2d = lambda a: a[None] if a.ndim == 1 else a
    small_params = [tuple(at_least_2d(a) for a in (weights[n], grads[n], mom1[n], mom2[n])) for n in _SMALL]
    for name, outs in zip(_SMALL, _adamw_small(small_params, "adamw_small")):
        keep(name, outs)

    land_grads(grads0b, new_v["final_g"], mix_layer0, "exchange_mix0")
    grads["w_in"] = stacked_grad("w_in")
    keep("w_in", _adamw(*as_2d("w_in"), "adamw_w_in"))

    return (loss, grad_x, *[grads[n] for n in _NAMES], *[delta[n] for n in _NAMES], *[new_m[n] for n in _NAMES],
            *[new_v[n] for n in _NAMES])
```

```python
import jax
import jax.numpy as jnp
from jax import lax
from jax.experimental import pallas as pl
from jax.experimental.pallas import tpu as pltpu

_F32 = jnp.float32
_BF = jnp.bfloat16

HEAD_DIM = 64
MIX_W = 256
N_HEADS = MIX_W // HEAD_DIM
CHUNK = 128
KV_BLOCK = 128
K_SHORT = 3
K_CONF = 31
K_FFN = 3
RMS_EPS = 1e-6
LN_EPS = 1e-5
ADAM_LR = 0.001
ADAM_B1 = 0.9
ADAM_B2 = 0.999
ADAM_EPS = 1e-08
ADAM_WD = 0.01
ADAM_STEP = 10
N_DEV = 8
VMEM_LIMIT = 56 * 1024 * 1024


def _bf(x):
    return x.astype(_BF)


def _ld(ref):
    return ref[...].astype(_F32)


_ANY_SPEC = pl.BlockSpec(memory_space=pl.ANY)


def _params(*sem):
    return pltpu.CompilerParams(dimension_semantics=sem, vmem_limit_bytes=VMEM_LIMIT)


def _dot(a, b):
    return jnp.dot(a, b, preferred_element_type=_F32)


def _dot_nt(a, b):
    return lax.dot_general(a, b, (((1,), (1,)), ((), ())), preferred_element_type=_F32)


def _dot_tn(a, b):
    return lax.dot_general(a, b, (((0,), (0,)), ((), ())), preferred_element_type=_F32)


def _row_tile(t, want):
    return want if t % want == 0 else t


def _pick_tile(rows, unit, max_rows):
    best = 0
    for cand in range(unit, min(rows, max_rows) + 1, unit):
        if rows % cand == 0:
            best = cand
    return best or rows


def _sigmoid(x):
    return 1.0 / (1.0 + jnp.exp(-x))


def _rms_rstd(x):
    return lax.rsqrt(jnp.mean(x * x, axis=-1, keepdims=True) + RMS_EPS)


def _norm_mm(x, g, w_t, name):
    t, d = x.shape
    n = w_t.shape[0]
    tm = _row_tile(t, 512)
    tn = _row_tile(n, 512)

    def body(x_ref, g_ref, w_ref, p_ref, h_ref):
        xv = x_ref[...]
        h = _bf(xv * _rms_rstd(xv) * g_ref[...])
        h_ref[...] = h
        for n0 in range(0, n, tn):
            p_ref[:, n0:n0 + tn] = _bf(_dot_nt(h, w_ref[n0:n0 + tn, :]))

    return pl.pallas_call(
        body, name=name, grid=(t // tm,),
        in_specs=[pl.BlockSpec((tm, d), lambda i: (i, 0)), pl.BlockSpec((1, d), lambda i: (0, 0)),
                  pl.BlockSpec((n, d), lambda i: (0, 0))],
        out_specs=[pl.BlockSpec((tm, n), lambda i: (i, 0)), pl.BlockSpec((tm, d), lambda i: (i, 0))],
        out_shape=[jax.ShapeDtypeStruct((t, n), _BF), jax.ShapeDtypeStruct((t, d), _BF)],
        compiler_params=_params("parallel"),
    )(x, g, w_t)


def _mm_nt(a, w_t, name, after=None):
    t, k = a.shape
    n = w_t.shape[0]
    tm = _row_tile(t, 512)
    tn = _row_tile(n, 512) if n % 512 == 0 else _row_tile(n, 256)

    def body(a_ref, w_ref, *rest):
        o_ref = rest[-1]
        av = a_ref[...]
        for n0 in range(0, n, tn):
            o_ref[:, n0:n0 + tn] = _bf(_dot_nt(av, w_ref[n0:n0 + tn, :]))

    extra = () if after is None else (after,)
    return pl.pallas_call(
        body, name=name, grid=(t // tm,),
        in_specs=[pl.BlockSpec((tm, k), lambda i: (i, 0)), pl.BlockSpec((n, k), lambda i: (0, 0))] + [_ANY_SPEC] * len(extra),
        out_specs=pl.BlockSpec((tm, n), lambda i: (i, 0)),
        out_shape=jax.ShapeDtypeStruct((t, n), _BF),
        compiler_params=_params("parallel"),
    )(a, w_t, *extra)


def _mm_res(parts, w, x, name):
    t = x.shape[0]
    k, d = w.shape
    tm = _row_tile(t, 512)
    widths = [a.shape[1] for a in parts]
    n_parts = len(parts)

    def body(*refs):
        w_ref, x_ref, o_ref = refs[n_parts:]
        acc, off = x_ref[...], 0
        for a_ref, width in zip(refs[:n_parts], widths):
            acc = acc + _dot(a_ref[...], w_ref[off:off + width, :])
            off += width
        o_ref[...] = acc

    return pl.pallas_call(
        body, name=name, grid=(t // tm,),
        in_specs=[pl.BlockSpec((tm, width), lambda i: (i, 0)) for width in widths] + [
            pl.BlockSpec((k, d), lambda i: (0, 0)), pl.BlockSpec((tm, d), lambda i: (i, 0))],
        out_specs=pl.BlockSpec((tm, d), lambda i: (i, 0)),
        out_shape=jax.ShapeDtypeStruct((t, d), _F32),
        compiler_params=_params("parallel"),
    )(*parts, w, x)


def _res_norm_mm(parts, w_res, x, g, w_t, name):
    t, d = x.shape
    k = w_res.shape[0]
    n = w_t.shape[0]
    tm = _row_tile(t, 512)
    tn = _row_tile(n, 512)
    widths = [a.shape[1] for a in parts]
    n_parts = len(parts)

    def body(*refs):
        wr_ref, x_ref, g_ref, wt_ref, xo_ref, p_ref, h_ref = refs[n_parts:]
        xv, off = x_ref[...], 0
        for a_ref, width in zip(refs[:n_parts], widths):
            xv = xv + _dot(a_ref[...], wr_ref[off:off + width, :])
            off += width
        xo_ref[...] = xv
        h = _bf(xv * _rms_rstd(xv) * g_ref[...])
        h_ref[...] = h
        for n0 in range(0, n, tn):
            p_ref[:, n0:n0 + tn] = _bf(_dot_nt(h, wt_ref[n0:n0 + tn, :]))

    row = lambda width: pl.BlockSpec((tm, width), lambda i: (i, 0))
    const = lambda shape: pl.BlockSpec(shape, lambda i: (0, 0))
    return pl.pallas_call(
        body, name=name, grid=(t // tm,),
        in_specs=[row(width) for width in widths] + [const((k, d)), row(d), const((1, d)), const((n, d))],
        out_specs=[row(d), row(n), row(d)],
        out_shape=[jax.ShapeDtypeStruct((t, d), _F32), jax.ShapeDtypeStruct((t, n), _BF), jax.ShapeDtypeStruct((t, d), _BF)],
        compiler_params=_params("parallel"),
    )(*parts, w_res, x, g, w_t)


def _res_final_loss(parts, w_res, x, g, target, name):
    t, d = x.shape
    k = w_res.shape[0]
    tm = _row_tile(t, 256)
    widths = [a.shape[1] for a in parts]
    n_parts = len(parts)

    def body(*refs):
        wr_ref, x_ref, g_ref, t_ref, dx_ref, dxb_ref, dg_ref, loss_ref = refs[n_parts:]
        xv, off = x_ref[...], 0
        for a_ref, width in zip(refs[:n_parts], widths):
            xv = xv + _dot(a_ref[...], wr_ref[off:off + width, :])
            off += width
        _loss_head(xv, g_ref, t_ref, dx_ref, dxb_ref, dg_ref, loss_ref, d)

    row = lambda width: pl.BlockSpec((tm, width), lambda i: (i, 0))
    const = lambda shape: pl.BlockSpec(shape, lambda i: (0, 0))
    return pl.pallas_call(
        body, name=name, grid=(t // tm,),
        in_specs=[row(width) for width in widths] + [const((k, d)), row(d), const((1, d)), row(d)],
        out_specs=[row(d), row(d), const((1, d)), const((1, 1))],
        out_shape=[jax.ShapeDtypeStruct((t, d), _F32), jax.ShapeDtypeStruct((t, d), _BF),
                   jax.ShapeDtypeStruct((1, d), _F32), jax.ShapeDtypeStruct((1, 1), _F32)],
        compiler_params=_params("arbitrary"),
    )(*parts, w_res, x, g, target)


def _loss_head(xv, g_ref, t_ref, dx_ref, dxb_ref, dg_ref, loss_ref, d):
    rstd = _rms_rstd(xv)
    xn = xv * rstd
    err = xn * g_ref[...] - t_ref[...]
    dy = err * (1.0 / d)
    u = dy * g_ref[...]
    dx = rstd * (u - xn * jnp.mean(u * xn, axis=-1, keepdims=True))
    dx_ref[...] = dx
    dxb_ref[...] = _bf(dx)

    @pl.when(pl.program_id(0) == 0)
    def _():
        dg_ref[...] = jnp.zeros_like(dg_ref)
        loss_ref[...] = jnp.zeros_like(loss_ref)

    dg_ref[...] += jnp.sum(dy * xn, axis=0, keepdims=True)
    loss_ref[...] += (0.5 / d) * jnp.sum(jnp.sum(err * err, axis=1, keepdims=True), axis=0, keepdims=True)


def _mm_normbwd(parts, w, x, g, dres, name):
    t = x.shape[0]
    k, d = w.shape
    tm = _row_tile(t, 512)
    widths = [a.shape[1] for a in parts]
    n_parts = len(parts)

    def body(*refs):
        a_refs = refs[:n_parts]
        w_ref, x_ref, g_ref, r_ref, dx_ref, dxb_ref, dg_ref = refs[n_parts:]
        dh, off = None, 0
        for a_ref, width in zip(a_refs, widths):
            term = _dot(_bf(a_ref[...]), w_ref[off:off + width, :])
            dh = term if dh is None else dh + term
            off += width
        xv = x_ref[...]
        rstd = _rms_rstd(xv)
        xn = xv * rstd
        u = dh * g_ref[...]
        dx = r_ref[...] + rstd * (u - xn * jnp.mean(u * xn, axis=-1, keepdims=True))
        dx_ref[...] = dx
        dxb_ref[...] = _bf(dx)

        @pl.when(pl.program_id(0) == 0)
        def _():
            dg_ref[...] = jnp.zeros_like(dg_ref)

        dg_ref[...] += jnp.sum(dh * xn, axis=0, keepdims=True)

    return pl.pallas_call(
        body, name=name, grid=(t // tm,),
        in_specs=[pl.BlockSpec((tm, width), lambda i: (i, 0)) for width in widths] + [
            pl.BlockSpec((k, d), lambda i: (0, 0)),
            pl.BlockSpec((tm, d), lambda i: (i, 0)), pl.BlockSpec((1, d), lambda i: (0, 0)),
            pl.BlockSpec((tm, d), lambda i: (i, 0))],
        out_specs=[pl.BlockSpec((tm, d), lambda i: (i, 0)), pl.BlockSpec((tm, d), lambda i: (i, 0)),
                   pl.BlockSpec((1, d), lambda i: (0, 0))],
        out_shape=[jax.ShapeDtypeStruct((t, d), _F32), jax.ShapeDtypeStruct((t, d), _BF),
                   jax.ShapeDtypeStruct((1, d), _F32)],
        compiler_params=_params("arbitrary"),
    )(*parts, w, x, g, dres)


def _mm_tn(a, b, name, out_dtype):
    t, m = a.shape
    n = b.shape[1]
    tm = _pick_tile(m, 128, 1408)
    tn = _pick_tile(n, 128, 1024)
    tk = _row_tile(t, 1024)
    nk = t // tk

    def body(a_ref, b_ref, o_ref, acc):
        kk = pl.program_id(2)

        @pl.when(kk == 0)
        def _():
            acc[...] = jnp.zeros_like(acc)

        acc[...] += _dot_tn(_bf(a_ref[...]), b_ref[...])

        @pl.when(kk == nk - 1)
        def _():
            o_ref[...] = acc[...].astype(o_ref.dtype)

    return pl.pallas_call(
        body, name=name, grid=(m // tm, n // tn, nk),
        in_specs=[pl.BlockSpec((tk, tm), lambda i, j, kk: (kk, i)), pl.BlockSpec((tk, tn), lambda i, j, kk: (kk, j))],
        out_specs=pl.BlockSpec((tm, tn), lambda i, j, kk: (i, j)),
        out_shape=jax.ShapeDtypeStruct((m, n), out_dtype),
        scratch_shapes=[pltpu.VMEM((tm, tn), _F32)],
        compiler_params=_params("parallel", "parallel", "arbitrary"),
    )(a, b)


def _mm_tn_halves(a0, a1, b, name):
    t, m = a0.shape
    n = b.shape[1]
    tm = _pick_tile(m, 128, 1408)
    tn = _pick_tile(n, 128, 1024)
    tk = _row_tile(t, 1024)
    nk = t // tk
    half = m // tm

    def body(a0_ref, a1_ref, b_ref, o_ref, acc):
        i = pl.program_id(0)
        kk = pl.program_id(2)

        @pl.when(kk == 0)
        def _():
            acc[...] = jnp.zeros_like(acc)

        @pl.when(i < half)
        def _():
            acc[...] += _dot_tn(a0_ref[...], b_ref[...])

        @pl.when(i >= half)
        def _():
            acc[...] += _dot_tn(a1_ref[...], b_ref[...])

        @pl.when(kk == nk - 1)
        def _():
            o_ref[...] = _bf(acc[...])

    return pl.pallas_call(
        body, name=name, grid=(2 * half, n // tn, nk),
        in_specs=[pl.BlockSpec((tk, tm), lambda i, j, kk: (jnp.where(i < half, kk, 0), jnp.minimum(i, half - 1))),
                  pl.BlockSpec((tk, tm), lambda i, j, kk: (jnp.where(i >= half, kk, 0), jnp.maximum(i - half, 0))),
                  pl.BlockSpec((tk, tn), lambda i, j, kk: (kk, j))],
        out_specs=pl.BlockSpec((tm, tn), lambda i, j, kk: (i, j)),
        out_shape=jax.ShapeDtypeStruct((2 * m, n), _BF),
        scratch_shapes=[pltpu.VMEM((tm, tn), _F32)],
        compiler_params=_params("parallel", "parallel", "arbitrary"),
    )(a0, a1, b)


def _mm_tn_parts(parts, b, name, after=None):
    t, n = b.shape
    widths = [a.shape[1] for a in parts]
    m = sum(widths)
    n_parts = len(parts)
    tk = _row_tile(t, 1024)
    nk = t // tk
    extra = () if after is None else (after,)

    def body(*refs):
        b_ref = refs[n_parts]
        o_ref, acc = refs[-2:]
        kk = pl.program_id(0)

        @pl.when(kk == 0)
        def _():
            acc[...] = jnp.zeros_like(acc)

        bv = b_ref[...]
        off = 0
        for a_ref, width in zip(refs[:n_parts], widths):
            acc[off:off + width, :] += _dot_tn(_bf(a_ref[...]), bv)
            off += width

        @pl.when(kk == nk - 1)
        def _():
            o_ref[...] = _bf(acc[...])

    return pl.pallas_call(
        body, name=name, grid=(nk,),
        in_specs=[pl.BlockSpec((tk, width), lambda kk: (kk, 0)) for width in widths] + [pl.BlockSpec((tk, n), lambda kk: (kk, 0))]
        + [_ANY_SPEC] * len(extra),
        out_specs=pl.BlockSpec((m, n), lambda kk: (0, 0)),
        out_shape=jax.ShapeDtypeStruct((m, n), _BF),
        scratch_shapes=[pltpu.VMEM((m, n), _F32)],
        compiler_params=_params("arbitrary"),
    )(*parts, b, *extra)


def _pad_rows(x, pad):
    return jnp.concatenate([x, jnp.zeros((pad, x.shape[1]), x.dtype)], axis=0)


def _shift_down(xp, s):
    return xp if s == 0 else pltpu.roll(xp, s, 0)


def _shift_up(xp, s):
    return xp if s == 0 else pltpu.roll(xp, xp.shape[0] - s, 0)


def _taps3(xp):
    one = _shift_down(xp, 1)
    return xp, one, _shift_down(one, 1)


def _conv3_taps(taps, w_ref):
    return w_ref[2:3, :] * taps[0] + w_ref[1:2, :] * taps[1] + w_ref[0:1, :] * taps[2]


def _conv3(xp, w_ref):
    return _conv3_taps(_taps3(xp), w_ref)


def _conv3_t(dyp, w_ref):
    one = _shift_up(dyp, 1)
    return w_ref[2:3, :] * dyp + w_ref[1:2, :] * one + w_ref[0:1, :] * _shift_up(one, 1)


def _conv3_dw(dyp, taps):
    return [jnp.sum(dyp * taps[2 - k], axis=0, keepdims=True) for k in range(3)]


def _ffn_mid_fwd(up_pre, wf, n_ex, name):
    t, f2 = up_pre.shape
    f = f2 // 2
    s = t // n_ex
    cb = MIX_W
    nb = f // cb

    def body(ug_ref, uv_ref, wg_ref, wv_ref, act_ref, gf_ref, vf_ref):
        gf = _conv3(_pad_rows(ug_ref[...].astype(_F32), 8), wg_ref)[:s]
        vf = _conv3(_pad_rows(uv_ref[...].astype(_F32), 8), wv_ref)[:s]
        act_ref[...] = _bf(gf * _sigmoid(gf) * vf)
        gf_ref[...] = _bf(gf)
        vf_ref[...] = _bf(vf)

    out = pl.BlockSpec((s, cb), lambda e, j: (e, j))
    return pl.pallas_call(
        body, name=name, grid=(n_ex, nb),
        in_specs=[pl.BlockSpec((s, cb), lambda e, j: (e, j)), pl.BlockSpec((s, cb), lambda e, j: (e, j + nb)),
                  pl.BlockSpec((K_FFN, cb), lambda e, j: (0, j)), pl.BlockSpec((K_FFN, cb), lambda e, j: (0, j + nb))],
        out_specs=[out, out, out],
        out_shape=[jax.ShapeDtypeStruct((t, f), _BF)] * 3,
        compiler_params=_params("parallel", "parallel"),
    )(up_pre, up_pre, wf, wf)


def _ffn_mid_bwd(up_pre, conv_g, conv_v, wf, dact, n_ex, name):
    t, f2 = up_pre.shape
    f = f2 // 2
    s = t // n_ex
    cb = MIX_W
    nb = f // cb

    def body(ug_ref, uv_ref, gf_ref, vf_ref, wg_ref, wv_ref, da_ref, dug_ref, duv_ref, dwg_ref, dwv_ref):
        gf = _ld(gf_ref)
        vf = _ld(vf_ref)
        sg = _sigmoid(gf)
        da = _ld(da_ref)

        @pl.when(pl.program_id(1) == 0)
        def _():
            dwg_ref[...] = jnp.zeros_like(dwg_ref)
            dwv_ref[...] = jnp.zeros_like(dwv_ref)

        def finish(dpost, w_ref, x_ref, du_ref, dw_ref):
            ahead = [_pad_rows(dpost, 8)]
            ahead.append(_shift_up(ahead[0], 1))
            ahead.append(_shift_up(ahead[1], 1))
            du_ref[...] = _bf((w_ref[2:3, :] * ahead[0] + w_ref[1:2, :] * ahead[1] + w_ref[0:1, :] * ahead[2])[:s])
            x = _ld(x_ref)
            for k in range(K_FFN):
                dw_ref[k:k + 1, :] += jnp.sum(ahead[2 - k][:s] * x, axis=0, keepdims=True)

        finish(da * vf * sg * (1.0 + gf * (1.0 - sg)), wg_ref, ug_ref, dug_ref, dwg_ref)
        finish(da * gf * sg, wv_ref, uv_ref, duv_ref, dwv_ref)

    return pl.pallas_call(
        body, name=name, grid=(nb, n_ex),
        in_specs=[pl.BlockSpec((s, cb), lambda j, e: (e, j)), pl.BlockSpec((s, cb), lambda j, e: (e, j + nb)),
                  pl.BlockSpec((s, cb), lambda j, e: (e, j)), pl.BlockSpec((s, cb), lambda j, e: (e, j)),
                  pl.BlockSpec((K_FFN, cb), lambda j, e: (0, j)), pl.BlockSpec((K_FFN, cb), lambda j, e: (0, j + nb)),
                  pl.BlockSpec((s, cb), lambda j, e: (e, j))],
        out_specs=[pl.BlockSpec((s, cb), lambda j, e: (e, j)), pl.BlockSpec((s, cb), lambda j, e: (e, j)),
                   pl.BlockSpec((K_FFN, cb), lambda j, e: (0, j)), pl.BlockSpec((K_FFN, cb), lambda j, e: (0, j))],
        out_shape=[jax.ShapeDtypeStruct((t, f), _BF), jax.ShapeDtypeStruct((t, f), _BF),
                   jax.ShapeDtypeStruct((K_FFN, f), _F32), jax.ShapeDtypeStruct((K_FFN, f), _F32)],
        compiler_params=_params("parallel", "arbitrary"),
    )(up_pre, up_pre, conv_g, conv_v, wf, wf, dact)


def _pcol(s, j):
    return pl.BlockSpec((s, MIX_W), lambda e, j=j: (e, j))


def _vec(rows=1):
    return pl.BlockSpec((rows, MIX_W), lambda e: (0, 0))


def _mix_a_fwd(p, wa, n_ex, name):
    t = p.shape[0]
    s = t // n_ex

    def body(gb_ref, gc_ref, ha_ref, w_ref, y_ref):
        cv = _conv3(_pad_rows(_ld(gc_ref) * _ld(ha_ref), 8), w_ref)[:s]
        y_ref[...] = _bf(_ld(gb_ref) * cv)

    return pl.pallas_call(
        body, name=name, grid=(n_ex,),
        in_specs=[_pcol(s, 0), _pcol(s, 1), _pcol(s, 2), _vec(K_SHORT)],
        out_specs=pl.BlockSpec((s, MIX_W), lambda e: (e, 0)),
        out_shape=jax.ShapeDtypeStruct((t, MIX_W), _BF),
        compiler_params=_params("parallel"),
    )(p, p, p, wa)


def _mix_a_bwd(p, wa, dmix, n_ex, name):
    t = p.shape[0]
    s = t // n_ex

    def body(gb_ref, gc_ref, ha_ref, w_ref, dy_ref, dp_ref, dw_ref):
        gc = _ld(gc_ref)
        ha = _ld(ha_ref)
        up = _taps3(_pad_rows(gc * ha, 8))
        cv = _conv3_taps(up, w_ref)[:s]
        dy = _ld(dy_ref)
        dcvp = _pad_rows(dy * _ld(gb_ref), 8)
        du = _conv3_t(dcvp, w_ref)[:s]
        dp_ref[:, 0:MIX_W] = _bf(dy * cv)
        dp_ref[:, MIX_W:2 * MIX_W] = _bf(du * ha)
        dp_ref[:, 2 * MIX_W:3 * MIX_W] = _bf(du * gc)

        @pl.when(pl.program_id(0) == 0)
        def _():
            dw_ref[...] = jnp.zeros_like(dw_ref)

        rows = _conv3_dw(dcvp, up)
        for k in range(3):
            dw_ref[k:k + 1, :] += rows[k]

    return pl.pallas_call(
        body, name=name, grid=(n_ex,),
        in_specs=[_pcol(s, 0), _pcol(s, 1), _pcol(s, 2), _vec(K_SHORT), _pcol(s, 0)],
        out_specs=[pl.BlockSpec((s, 3 * MIX_W), lambda e: (e, 0)), _vec(K_SHORT)],
        out_shape=[jax.ShapeDtypeStruct((t, 3 * MIX_W), _BF), jax.ShapeDtypeStruct((K_SHORT, MIX_W), _F32)],
        compiler_params=_params("arbitrary"),
    )(p, p, p, wa, dmix)


CONF_PAD = 32
CONF_ROWS = 64
_CONF_LANES = (slice(0, 128), slice(128, 256))


def _conf_taps(win, ahead):
    n = CONF_ROWS + CONF_PAD
    for b in range(8):
        rot = win if b == 0 else pltpu.roll(win, (n - b) if ahead else b, 0)
        for a in range(4):
            if 8 * a + b < K_CONF:
                yield rot, 8 * a + b, (8 * a) if ahead else (CONF_PAD - 8 * a)


def _ln_fwd(x, g, b):
    mu = jnp.mean(x, axis=-1, keepdims=True)
    xc = x - mu
    rstd = lax.rsqrt(jnp.mean(xc * xc, axis=-1, keepdims=True) + LN_EPS)
    xhat = xc * rstd
    return xhat * g + b, xhat, rstd


def _ln_bwd(dy, xhat, rstd, g):
    dxh = dy * g
    return rstd * (dxh - jnp.mean(dxh, axis=-1, keepdims=True) - xhat * jnp.mean(dxh * xhat, axis=-1, keepdims=True))


def _mix_b_fwd(p, wb, bb, lg, lb, n_ex, name):
    t = p.shape[0]
    s = t // n_ex

    def body(val_ref, gat_ref, w_ref, bb_ref, lg_ref, lb_ref, y_ref, cb_ref, xpad):
        xpad[0:CONF_PAD, :] = jnp.zeros((CONF_PAD, MIX_W), _F32)
        xpad[CONF_PAD:, :] = _ld(val_ref) * _sigmoid(_ld(gat_ref))

        def chunk(c, carry):
            r0 = pl.multiple_of(c * CONF_ROWS, CONF_ROWS)
            for lanes in _CONF_LANES:
                acc = None
                for rot, sh, lo in _conf_taps(xpad[pl.ds(r0, CONF_ROWS + CONF_PAD), lanes], False):
                    term = w_ref[K_CONF - 1 - sh:K_CONF - sh, lanes] * rot[lo:lo + CONF_ROWS]
                    acc = term if acc is None else acc + term
                cb_ref[pl.ds(r0, CONF_ROWS), lanes] = acc + bb_ref[:, lanes]
            return carry

        lax.fori_loop(0, s // CONF_ROWS, chunk, 0)
        yl, _, _ = _ln_fwd(cb_ref[...], lg_ref[...], lb_ref[...])
        y_ref[...] = _bf(yl * _sigmoid(yl))

    return pl.pallas_call(
        body, name=name, grid=(n_ex,),
        in_specs=[_pcol(s, 3), _pcol(s, 4), _vec(K_CONF), _vec(), _vec(), _vec()],
        out_specs=[pl.BlockSpec((s, MIX_W), lambda e: (e, 0)), pl.BlockSpec((s, MIX_W), lambda e: (e, 0))],
        out_shape=[jax.ShapeDtypeStruct((t, MIX_W), _BF), jax.ShapeDtypeStruct((t, MIX_W), _F32)],
        scratch_shapes=[pltpu.VMEM((CONF_PAD + s, MIX_W), _F32)],
        compiler_params=_params("parallel"),
    )(p, p, wb, bb, lg, lb)


def _mix_b_bwd(p, cb, wb, lg, lb, dmix, n_ex, name):
    t = p.shape[0]
    s = t // n_ex

    def body(val_ref, gat_ref, cb_ref, w_ref, lg_ref, lb_ref, dy_ref, dp_ref, dw_ref, dbb_ref, dlg_ref, dlb_ref,
             xpad, dpad, dglu_s, dw_acc):
        @pl.when(pl.program_id(0) == 0)
        def _():
            for r in (dw_ref, dbb_ref, dlg_ref, dlb_ref):
                r[...] = jnp.zeros_like(r)

        yl, xhat, rstd = _ln_fwd(cb_ref[...], lg_ref[...], lb_ref[...])
        sy = _sigmoid(yl)
        dyl = _ld(dy_ref) * sy * (1.0 + yl * (1.0 - sy))
        dlg_ref[...] += jnp.sum(dyl * xhat, axis=0, keepdims=True)
        dlb_ref[...] += jnp.sum(dyl, axis=0, keepdims=True)
        dcb = _ln_bwd(dyl, xhat, rstd, lg_ref[...])
        dbb_ref[...] += jnp.sum(dcb, axis=0, keepdims=True)

        val = _ld(val_ref)
        sg = _sigmoid(_ld(gat_ref))
        xpad[0:CONF_PAD, :] = jnp.zeros((CONF_PAD, MIX_W), _F32)
        xpad[CONF_PAD:, :] = val * sg
        dpad[0:s, :] = dcb
        dpad[s:, :] = jnp.zeros((CONF_PAD, MIX_W), _F32)
        dw_acc[...] = jnp.zeros_like(dw_acc)

        def chunk(c, carry):
            r0 = pl.multiple_of(c * CONF_ROWS, CONF_ROWS)
            for lanes in _CONF_LANES:
                d_win = dpad[pl.ds(r0, CONF_ROWS + CONF_PAD), lanes]
                d_rows = d_win[0:CONF_ROWS]
                acc = None
                for rot, sh, lo in _conf_taps(d_win, True):
                    term = w_ref[K_CONF - 1 - sh:K_CONF - sh, lanes] * rot[lo:lo + CONF_ROWS]
                    acc = term if acc is None else acc + term
                dglu_s[pl.ds(r0, CONF_ROWS), lanes] = acc
                for rot, sh, lo in _conf_taps(xpad[pl.ds(r0, CONF_ROWS + CONF_PAD), lanes], False):
                    prod = d_rows * rot[lo:lo + CONF_ROWS]
                    dw_acc[K_CONF - 1 - sh, :, lanes] += jnp.sum(prod.reshape(CONF_ROWS // 8, 8, 128), axis=0)
            return carry

        lax.fori_loop(0, s // CONF_ROWS, chunk, 0)
        dw_ref[...] += jnp.sum(dw_acc[...], axis=1)
        dglu = dglu_s[...]
        dp_ref[:, 0:MIX_W] = _bf(dglu * sg)
        dp_ref[:, MIX_W:2 * MIX_W] = _bf(dglu * val * sg * (1.0 - sg))

    return pl.pallas_call(
        body, name=name, grid=(n_ex,),
        in_specs=[_pcol(s, 3), _pcol(s, 4), pl.BlockSpec((s, MIX_W), lambda e: (e, 0)), _vec(K_CONF), _vec(), _vec(),
                  _pcol(s, 1)],
        out_specs=[pl.BlockSpec((s, 2 * MIX_W), lambda e: (e, 0)), _vec(K_CONF), _vec(), _vec(), _vec()],
        out_shape=[jax.ShapeDtypeStruct((t, 2 * MIX_W), _BF), jax.ShapeDtypeStruct((K_CONF, MIX_W), _F32),
                   jax.ShapeDtypeStruct((1, MIX_W), _F32), jax.ShapeDtypeStruct((1, MIX_W), _F32),
                   jax.ShapeDtypeStruct((1, MIX_W), _F32)],
        scratch_shapes=[pltpu.VMEM((CONF_PAD + s, MIX_W), _F32), pltpu.VMEM((s + CONF_PAD, MIX_W), _F32),
                        pltpu.VMEM((s, MIX_W), _F32), pltpu.VMEM((K_CONF, 8, MIX_W), _F32)],
        compiler_params=_params("arbitrary"),
    )(p, p, cb, wb, lg, lb, dmix)


_INV_SQRT2 = 0.7071067811865476
_INV_SQRT2PI = 0.3989422804014327


def _gelu(x):
    return 0.5 * x * (1.0 + lax.erf(x * _INV_SQRT2))


def _gelu_grad(x):
    return 0.5 * (1.0 + lax.erf(x * _INV_SQRT2)) + x * _INV_SQRT2PI * jnp.exp(-0.5 * x * x)


def _head_masks(width=MIX_W):
    lane = lax.broadcasted_iota(jnp.int32, (1, width), 1)
    return [(lane >= h * HEAD_DIM) & (lane < (h + 1) * HEAD_DIM) for h in range(N_HEADS)]


def _tril_mask():
    r = lax.broadcasted_iota(jnp.int32, (CHUNK, CHUNK), 0)
    c = lax.broadcasted_iota(jnp.int32, (CHUNK, CHUNK), 1)
    return c <= r


def _sgu_apply(ws_ref, x3, transpose):
    n = x3.shape[0]
    tril = _tril_mask()
    masks = _head_masks()
    xb = _bf(x3)
    out = jnp.zeros(x3.shape, _F32)
    for h in range(N_HEADS):
        w = _bf(jnp.where(tril, ws_ref[h], 0.0))
        wb = jnp.broadcast_to(w[None], (n, CHUNK, CHUNK))
        dims = (((1,), (1,)), ((0,), (0,))) if transpose else (((2,), (1,)), ((0,), (0,)))
        r = lax.dot_general(wb, xb, dims, preferred_element_type=_F32)
        out = out + jnp.where(masks[h][None], r, 0.0)
    return out


def _mix_c_fwd(p, lg, lb, ws, sb_full, n_ex, name):
    t = p.shape[0]
    s = t // n_ex
    nc = s // CHUNK

    def body(pu_ref, pv_ref, lg_ref, lb_ref, ws_ref, sb_ref, y_ref):
        u = _gelu(_ld(pu_ref))
        vl, _, _ = _ln_fwd(_gelu(_ld(pv_ref)), lg_ref[...], lb_ref[...])
        sp = _sgu_apply(ws_ref, vl.reshape(nc, CHUNK, MIX_W), False) + sb_ref[...][None]
        y_ref[...] = _bf(u * sp.reshape(s, MIX_W))

    return pl.pallas_call(
        body, name=name, grid=(n_ex,),
        in_specs=[_pcol(s, 5), _pcol(s, 6), _vec(), _vec(),
                  pl.BlockSpec((N_HEADS, CHUNK, CHUNK), lambda e: (0, 0, 0)), pl.BlockSpec((CHUNK, MIX_W), lambda e: (0, 0))],
        out_specs=pl.BlockSpec((s, MIX_W), lambda e: (e, 0)),
        out_shape=jax.ShapeDtypeStruct((t, MIX_W), _BF),
        compiler_params=_params("parallel"),
    )(p, p, lg, lb, ws, sb_full)


def _mix_c_bwd(p, lg, lb, ws, sb_full, dmix, n_ex, name):
    t = p.shape[0]
    s = t // n_ex
    nc = s // CHUNK

    def body(pu_ref, pv_ref, lg_ref, lb_ref, ws_ref, sb_ref, dy_ref, dp_ref, dlg_ref, dlb_ref, dws_ref, dsb_ref):
        @pl.when(pl.program_id(0) == 0)
        def _():
            for r in (dlg_ref, dlb_ref, dws_ref, dsb_ref):
                r[...] = jnp.zeros_like(r)

        pu = _ld(pu_ref)
        pv = _ld(pv_ref)
        u = _gelu(pu)
        vl, xhat, rstd = _ln_fwd(_gelu(pv), lg_ref[...], lb_ref[...])
        vl3 = vl.reshape(nc, CHUNK, MIX_W)
        sp = _sgu_apply(ws_ref, vl3, False) + sb_ref[...][None]
        dy = _ld(dy_ref)
        dp_ref[:, 0:MIX_W] = _bf(dy * sp.reshape(s, MIX_W) * _gelu_grad(pu))
        dsp3 = (dy * u).reshape(nc, CHUNK, MIX_W)
        dsb_full = jnp.sum(dsp3, axis=0)
        masks = _head_masks()
        tril = _tril_mask()
        dspb = _bf(dsp3)
        vlb = _bf(vl3)
        for h in range(N_HEADS):
            dsb_ref[:, h:h + 1] += jnp.sum(jnp.where(masks[h], dsb_full, 0.0), axis=1, keepdims=True)
            dm = jnp.where(masks[h][None], dspb, jnp.zeros_like(dspb))
            g3 = lax.dot_general(dm, vlb, (((2,), (2,)), ((0,), (0,))), preferred_element_type=_F32)
            dws_ref[h] += jnp.where(tril, jnp.sum(g3, axis=0), 0.0)
        dvl = _sgu_apply(ws_ref, dsp3, True).reshape(s, MIX_W)
        dlg_ref[...] += jnp.sum(dvl * xhat, axis=0, keepdims=True)
        dlb_ref[...] += jnp.sum(dvl, axis=0, keepdims=True)
        dp_ref[:, MIX_W:2 * MIX_W] = _bf(_ln_bwd(dvl, xhat, rstd, lg_ref[...]) * _gelu_grad(pv))

    return pl.pallas_call(
        body, name=name, grid=(n_ex,),
        in_specs=[_pcol(s, 5), _pcol(s, 6), _vec(), _vec(),
                  pl.BlockSpec((N_HEADS, CHUNK, CHUNK), lambda e: (0, 0, 0)), pl.BlockSpec((CHUNK, MIX_W), lambda e: (0, 0)),
                  _pcol(s, 2)],
        out_specs=[pl.BlockSpec((s, 2 * MIX_W), lambda e: (e, 0)), _vec(), _vec(),
                   pl.BlockSpec((N_HEADS, CHUNK, CHUNK), lambda e: (0, 0, 0)), pl.BlockSpec((CHUNK, N_HEADS), lambda e: (0, 0))],
        out_shape=[jax.ShapeDtypeStruct((t, 2 * MIX_W), _BF), jax.ShapeDtypeStruct((1, MIX_W), _F32),
                   jax.ShapeDtypeStruct((1, MIX_W), _F32), jax.ShapeDtypeStruct((N_HEADS, CHUNK, CHUNK), _F32),
                   jax.ShapeDtypeStruct((CHUNK, N_HEADS), _F32)],
        compiler_params=_params("arbitrary"),
    )(p, p, lg, lb, ws, sb_full, dmix)


D_QBLOCK = 512
HEAD_COLS = N_HEADS * KV_BLOCK


def _stack_heads(x3):
    return jnp.stack([_bf(jnp.where(m[None], x3, 0.0)) for m in _head_masks()], axis=1)


def _stack_heads_rows(x):
    return jnp.concatenate([_bf(jnp.where(m, x, 0.0)) for m in _head_masks()], axis=0)


def _cols_to_rows(x):
    return jnp.concatenate([x[:, h * KV_BLOCK:(h + 1) * KV_BLOCK] for h in range(N_HEADS)], axis=0)


def _head_sums(x):
    return [jnp.sum(x[:, h * KV_BLOCK:(h + 1) * KV_BLOCK], axis=1, keepdims=True) for h in range(N_HEADS)]


def _spread(cols):
    tq = cols[0].shape[0]
    return jnp.concatenate([jnp.broadcast_to(c, (tq, KV_BLOCK)) for c in cols], axis=1)


def _pair_dot(x, m2):
    half = 2 * KV_BLOCK
    xb = _bf(x)
    return jnp.concatenate([_dot(xb[:, :half], m2), _dot(xb[:, half:], m2)], axis=1)


def _tri2(lower):
    n = 2 * KV_BLOCK
    r = lax.broadcasted_iota(jnp.int32, (n, n), 0)
    c = lax.broadcasted_iota(jnp.int32, (n, n), 1)
    same = (r >= KV_BLOCK) == (c >= KV_BLOCK)
    return _bf(jnp.where(same & (r > c if lower else r < c), 1.0, 0.0))


def _sb_scores(qs, kc, j, t_idx, on_diagonal):
    z = _dot_nt(qs, kc)
    lb = jnp.minimum(z, 0.0) - jnp.log(1.0 + jnp.exp(-jnp.abs(z)))
    if not on_diagonal:
        return (lambda x: x), lb, lb - z
    lane = lax.broadcasted_iota(jnp.int32, (1, HEAD_COLS), 1)
    valid = (j * KV_BLOCK + (lane & (KV_BLOCK - 1))) < t_idx
    keep = lambda x: jnp.where(valid, x, 0.0)
    return keep, lb, keep(lb - z)


RUN_LANES = 128


def _run_lane(j, h):
    return lax.broadcasted_iota(jnp.int32, (1, RUN_LANES), 1) == j * N_HEADS + h


def _d_qblock(s):
    return D_QBLOCK if s % D_QBLOCK == 0 else KV_BLOCK


def _mix_d_fwd(p, n_ex, name):
    t = p.shape[0]
    s = t // n_ex
    tq = _d_qblock(s)
    nq = s // tq
    r = tq // KV_BLOCK
    nb = s // KV_BLOCK
    assert nb * N_HEADS <= RUN_LANES

    def body(q_ref, k_ref, v_ref, y_ref, runs_ref, kc, vc):
        i = pl.program_id(1)

        @pl.when(i == 0)
        def _():
            kc[...] = _stack_heads(k_ref[...].reshape(nb, KV_BLOCK, MIX_W))
            vc[...] = _stack_heads(v_ref[...].reshape(nb, KV_BLOCK, MIX_W))

        qs = _bf(_ld(q_ref) * (HEAD_DIM ** -0.5))
        t_idx = i * tq + lax.broadcasted_iota(jnp.int32, (tq, 1), 0)
        after_m = _tri2(True)
        nkb = (i + 1) * r

        runs_ref[...] = jnp.zeros_like(runs_ref)

        def one_block(j, runs, acc, on_diagonal):
            keep, lb, c = _sb_scores(qs, kc[j].reshape(HEAD_COLS, MIX_W), j, t_idx, on_diagonal)
            a = keep(jnp.exp(lb + _pair_dot(c, after_m) + _spread(runs)))
            acc = acc + _dot(_bf(a), vc[j].reshape(HEAD_COLS, MIX_W))
            kept = runs_ref[...]
            for h in range(N_HEADS):
                kept = jnp.where(_run_lane(j, h), runs[h], kept)
            runs_ref[...] = kept
            return tuple(ru + cs for ru, cs in zip(runs, _head_sums(c))), acc

        def trip(last, carry, on_diagonal):
            runs, acc = carry
            for sub in range(r):
                runs, acc = one_block(last - sub, runs, acc, on_diagonal)
            return runs, acc

        zero = jnp.zeros((tq, 1), _F32)
        carry = trip(nkb - 1, ((zero,) * N_HEADS, jnp.zeros((tq, MIX_W), _F32)), True)
        below = lambda m: nkb - 1 - (m + 1) * r
        carry = lax.fori_loop(0, i // 2, lambda m, carry: trip(below(2 * m + 1), trip(below(2 * m), carry, False), False), carry)
        _, acc = lax.fori_loop(0, i % 2, lambda m, carry: trip(below(i - 1), carry, False), carry)
        y_ref[...] = _bf(acc)

    return pl.pallas_call(
        body, name=name, grid=(n_ex, nq),
        in_specs=[pl.BlockSpec((tq, MIX_W), lambda e, i: (e * nq + i, 7)), pl.BlockSpec((s, MIX_W), lambda e, i: (e, 8)),
                  pl.BlockSpec((s, MIX_W), lambda e, i: (e, 9))],
        out_specs=[pl.BlockSpec((tq, MIX_W), lambda e, i: (e * nq + i, 0)),
                   pl.BlockSpec((tq, RUN_LANES), lambda e, i: (e * nq + i, 0))],
        out_shape=[jax.ShapeDtypeStruct((t, MIX_W), _BF), jax.ShapeDtypeStruct((t, RUN_LANES), _F32)],
        scratch_shapes=[pltpu.VMEM((nb, N_HEADS, KV_BLOCK, MIX_W), _BF), pltpu.VMEM((nb, N_HEADS, KV_BLOCK, MIX_W), _BF)],
        compiler_params=_params("parallel", "arbitrary"),
    )(p, p, p)


def _mix_d_bwd(p, kept_runs, dmix, n_ex, name):
    t = p.shape[0]
    s = t // n_ex
    tq = _d_qblock(s)
    nq = s // tq
    r = tq // KV_BLOCK
    nb = s // KV_BLOCK
    scale = HEAD_DIM ** -0.5

    def body(q_ref, k_ref, v_ref, runs_ref, do_ref, dq_ref, dk_ref, dv_ref, kc, vc):
        i = pl.program_id(1)

        @pl.when(i == 0)
        def _():
            kc[...] = _stack_heads(k_ref[...].reshape(nb, KV_BLOCK, MIX_W))
            vc[...] = _stack_heads(v_ref[...].reshape(nb, KV_BLOCK, MIX_W))
            dk_ref[...] = jnp.zeros_like(dk_ref)
            dv_ref[...] = jnp.zeros_like(dv_ref)

        q_scaled = _ld(q_ref) * scale
        qs = _bf(q_scaled)
        do = do_ref[...]
        dob = _bf(do)
        q_rows = _stack_heads_rows(q_scaled)
        do_rows = _stack_heads_rows(do)
        kept = runs_ref[...]
        t_idx = i * tq + lax.broadcasted_iota(jnp.int32, (tq, 1), 0)
        after_m = _tri2(True)
        before_m = _tri2(False)
        nkb = (i + 1) * r
        zero = jnp.zeros((tq, 1), _F32)

        def trip(first, carry, on_diagonal):
            for sub in range(r):
                carry = one_block(first + sub, carry, on_diagonal)
            return carry

        def one_block(j, carry, on_diagonal):
            pres, dq = carry
            rows = pl.ds(pl.multiple_of(j * KV_BLOCK, KV_BLOCK), KV_BLOCK)
            kj = kc[j].reshape(HEAD_COLS, MIX_W)
            keep, lb, c = _sb_scores(qs, kj, j, t_idx, on_diagonal)
            runs = [jnp.sum(jnp.where(_run_lane(j, h), kept, 0.0), axis=1, keepdims=True) for h in range(N_HEADS)]
            a = keep(jnp.exp(lb + _pair_dot(c, after_m) + _spread(runs)))
            g = a * _dot_nt(dob, vc[j].reshape(HEAD_COLS, MIX_W))
            before = _pair_dot(g, before_m) + _spread(pres)
            sig = jnp.exp(lb)
            dz = _bf(keep(g * (1.0 - sig) - sig * before))
            dk_ref[rows, :] += _dot_tn(_cols_to_rows(dz), q_rows)
            dv_ref[rows, :] += _dot_tn(_cols_to_rows(_bf(a)), do_rows)
            return tuple(pr + gs for pr, gs in zip(pres, _head_sums(g))), dq + _dot(dz, kj)

        init = ((zero,) * N_HEADS, jnp.zeros((tq, MIX_W), _F32))
        carry = lax.fori_loop(0, i // 2, lambda m, carry: trip((2 * m + 1) * r, trip(2 * m * r, carry, False), False), init)
        carry = lax.fori_loop(0, i % 2, lambda m, carry: trip((i - 1) * r, carry, False), carry)
        _, dq = trip(i * r, carry, True)
        dq_ref[...] = _bf(dq * scale)

    return pl.pallas_call(
        body, name=name, grid=(n_ex, nq),
        in_specs=[pl.BlockSpec((tq, MIX_W), lambda e, i: (e * nq + i, 7)), pl.BlockSpec((s, MIX_W), lambda e, i: (e, 8)),
                  pl.BlockSpec((s, MIX_W), lambda e, i: (e, 9)), pl.BlockSpec((tq, RUN_LANES), lambda e, i: (e * nq + i, 0)),
                  pl.BlockSpec((tq, MIX_W), lambda e, i: (e * nq + i, 3))],
        out_specs=[pl.BlockSpec((tq, MIX_W), lambda e, i: (e * nq + i, 0)), pl.BlockSpec((s, MIX_W), lambda e, i: (e, 0)),
                   pl.BlockSpec((s, MIX_W), lambda e, i: (e, 0))],
        out_shape=[jax.ShapeDtypeStruct((t, MIX_W), _BF), jax.ShapeDtypeStruct((t, MIX_W), _F32),
                   jax.ShapeDtypeStruct((t, MIX_W), _F32)],
        scratch_shapes=[pltpu.VMEM((nb, N_HEADS, KV_BLOCK, MIX_W), _BF), pltpu.VMEM((nb, N_HEADS, KV_BLOCK, MIX_W), _BF)],
        compiler_params=_params("parallel", "arbitrary"),
    )(p, p, p, kept_runs, dmix)


def _fwd_mix(x, w, l, n_ex):
    p, h1 = _norm_mm(x, w["norm1_g"][l], w["w_in_t"][l], "in_proj")
    y_a = _mix_a_fwd(p, w["conv_a_w"][l], n_ex, "mix_a_fwd")
    y_b, cb = _mix_b_fwd(p, w["conv_b_w"][l], w["conv_b_b"][l], w["ln_b_g"][l], w["ln_b_b"][l], n_ex, "mix_b_fwd")
    y_c = _mix_c_fwd(p, w["ln_c_g"][l], w["ln_c_b"][l], w["sgu_w"][l], w["sgu_b_full"][l], n_ex, "mix_c_fwd")
    y_d, runs_d = _mix_d_fwd(p, n_ex, "mix_d_fwd")
    return dict(x=x, h1=h1, p=p, cb=cb, runs_d=runs_d, mix=(y_a, y_b, y_c, y_d))


def _fwd_ffn(st, w, l, n_ex):
    x1, up_pre, h2 = _res_norm_mm(st["mix"], w["w_out"][l], st["x"], w["norm2_g"][l], w["w_up_t"][l], "out_up_proj")
    act, conv_g, conv_v = _ffn_mid_fwd(up_pre, w["conv_f_w"][l], n_ex, "ffn_mid_fwd")
    st.update(x1=x1, h2=h2, up_pre=up_pre, act=act, conv_g=conv_g, conv_v=conv_v)


def _down_proj(st, w, l):
    return _mm_res((st["act"],), w["w_down"][l], st["x1"], "down_proj")


def _down_proj_loss(st, w, l, target):
    return _res_final_loss((st["act"],), w["w_down"][l], st["x1"], w["final_g"], target, "down_proj_loss")


def _bwd_ffn(st, w, l, dx, dxb, n_ex):
    g = {}
    dact = _mm_nt(dxb, w["w_down"][l], "down_proj_dx")
    g["w_down"] = _mm_tn(st["act"], dxb, "down_proj_dw", _BF)
    dup_g, dup_v, dwf_g, dwf_v = _ffn_mid_bwd(
        st["up_pre"], st["conv_g"], st["conv_v"], w["conv_f_w"][l], dact, n_ex, "ffn_mid_bwd")
    g["conv_f_w"] = jnp.concatenate([dwf_g, dwf_v], axis=1)
    dx, dxb, g["norm2_g"] = _mm_normbwd((dup_g, dup_v), w["w_up_t"][l], st["x1"], w["norm2_g"][l], dx, "up_proj_dx")
    g["w_up_t"] = _mm_tn_halves(dup_g, dup_v, st["h2"], "up_proj_dw")
    return dx, dxb, g


def _bwd_out_proj(st, w, l, dxb):
    return _mm_nt(dxb, w["w_out"][l], "out_proj_dx"), _mm_tn_parts(st["mix"], dxb, "out_proj_dw")


def _bwd_mixers(st, w, l, dx, dmix, n_ex):
    g = {}
    p = st["p"]
    dp_a, g["conv_a_w"] = _mix_a_bwd(p, w["conv_a_w"][l], dmix, n_ex, "mix_a_bwd")
    dp_b, g["conv_b_w"], g["conv_b_b"], g["ln_b_g"], g["ln_b_b"] = _mix_b_bwd(
        p, st["cb"], w["conv_b_w"][l], w["ln_b_g"][l], w["ln_b_b"][l], dmix, n_ex, "mix_b_bwd")
    dp_c, g["ln_c_g"], g["ln_c_b"], g["sgu_w"], g["sgu_b_t"] = _mix_c_bwd(
        p, w["ln_c_g"][l], w["ln_c_b"][l], w["sgu_w"][l], w["sgu_b_full"][l], dmix, n_ex, "mix_c_bwd")
    dq, dk, dv = _mix_d_bwd(p, st["runs_d"], dmix, n_ex, "mix_d_bwd")
    dp = (dp_a, dp_b, dp_c, dq, dk, dv)
    dx, dxb, g["norm1_g"] = _mm_normbwd(dp, w["w_in_t"][l], st["x"], w["norm1_g"][l], dx, "in_proj_dx")
    return dx, dxb, g, dp


def _bwd_mix(st, w, l, dx, dxb, n_ex):
    dmix, dw_out = _bwd_out_proj(st, w, l, dxb)
    dx, dxb, g, dp = _bwd_mixers(st, w, l, dx, dmix, n_ex)
    g["w_out"] = dw_out
    g["w_in_t"] = _mm_tn_parts(dp, st["h1"], "in_proj_dw")
    return dx, dxb, g


_MESH = pl.DeviceIdType.MESH
_ANY = pl.BlockSpec(memory_space=pl.ANY)


def _position():
    return lax.axis_index("x"), lax.axis_index("y"), lax.axis_index("c")


def _flat(px, py, pc):
    return 4 * px + 2 * py + pc


def _all_gather(shard, name, after):
    r, c_ = shard.shape

    def body(x_ref, after_ref, out_ref, send_sems, recv_sems, local_sem):
        x, y, c = _position()
        me, sibling = (x, y, c), (x, y, 1 - c)
        chips = [(1 - x, y), (x, 1 - y), (1 - x, 1 - y)]

        def copy(k, block, to, src=None):
            slab = out_ref.at[_flat(*block)]
            return pltpu.make_async_remote_copy(
                src_ref=slab if src is None else src, dst_ref=slab, send_sem=send_sems.at[k], recv_sem=recv_sems.at[k],
                device_id=to, device_id_type=_MESH)

        mine = pltpu.make_async_copy(x_ref, out_ref.at[_flat(*me)], local_sem)
        mine.start()
        first = [copy(0, me, sibling, src=x_ref)]
        first += [copy(1 + j, me, (*chip, c), src=x_ref) for j, chip in enumerate(chips)]
        for cp in first:
            cp.start()
        passed = [copy(4 + j, (*chip, c), sibling) for j, chip in enumerate(chips)]
        for j, chip in enumerate(chips):
            copy(1 + j, (*chip, c), me).wait_recv()
            passed[j].start()
        copy(0, sibling, me).wait_recv()
        for j, chip in enumerate(chips):
            copy(4 + j, (*chip, 1 - c), me).wait_recv()
        for cp in first + passed:
            cp.wait_send()
        mine.wait()

    return pl.pallas_call(
        body, name=name, out_shape=jax.ShapeDtypeStruct((N_DEV, r, c_), shard.dtype),
        in_specs=[_ANY, _ANY], out_specs=_ANY,
        scratch_shapes=[pltpu.SemaphoreType.DMA((7,)), pltpu.SemaphoreType.DMA((7,)), pltpu.SemaphoreType.DMA],
    )(shard, after)


_HBM = pl.BlockSpec(memory_space=pltpu.HBM)
_SEM = pl.BlockSpec(memory_space=pltpu.SEMAPHORE)
_DATAFLOW = pltpu.SideEffectType.DATAFLOW_SIDE_EFFECTING


def _peers(x, y, c):
    return [((1 - x) if (k + 1) & 4 else x, (1 - y) if (k + 1) & 2 else y, (1 - c) if (k + 1) & 1 else c)
            for k in range(N_DEV - 1)]


def _direct_copies(src_refs, land_refs, send_sems, recv_sems, to_all):
    x, y, c = _position()
    my = _flat(x, y, c)
    out, back = [], []
    for m, (src_ref, land_ref) in enumerate(zip(src_refs, land_refs)):
        for k, peer in enumerate(_peers(x, y, c)):
            src = src_ref if to_all else src_ref.at[_flat(*peer)]
            n = m * (N_DEV - 1) + k
            sems = dict(send_sem=send_sems.at[n], recv_sem=recv_sems.at[n], device_id=peer, device_id_type=_MESH)
            out.append(pltpu.make_async_remote_copy(src_ref=src, dst_ref=land_ref.at[my], **sems))
            back.append(pltpu.make_async_remote_copy(src_ref=src, dst_ref=land_ref.at[_flat(*peer)], **sems))
    return out, back


def _exchange_start(srcs, to_all, after, name):
    n = len(srcs)
    n_sems = n * (N_DEV - 1)
    land_shapes = [(N_DEV,) + tuple(a.shape[-2:]) for a in srcs]

    def body(*refs):
        src_refs, land_refs = refs[:n], refs[n:2 * n]
        send_sems, recv_sems = refs[2 * n + 1], refs[2 * n + 2]
        token = refs[-1]
        for cp in _direct_copies(src_refs, land_refs, send_sems, recv_sems, to_all)[0]:
            cp.start()
        token[...] = jnp.zeros_like(token)

    lands = [pltpu.with_memory_space_constraint(lax.empty(shp, a.dtype), pltpu.HBM) for shp, a in zip(land_shapes, srcs)]
    outs = pl.pallas_call(
        body, name=name,
        out_shape=(pltpu.SemaphoreType.DMA((n_sems,)), pltpu.SemaphoreType.DMA((n_sems,)),
                   *[pltpu.HBM(a.shape, a.dtype) for a in srcs], *[pltpu.HBM(shp, a.dtype) for shp, a in zip(land_shapes, srcs)],
                   jax.ShapeDtypeStruct((8, 128), _F32)),
        in_specs=(_HBM,) * (2 * n) + (_ANY,),
        out_specs=(_SEM, _SEM) + (_HBM,) * (2 * n) + (pl.BlockSpec(memory_space=pltpu.VMEM),),
        input_output_aliases={i: 2 + i for i in range(2 * n)},
        compiler_params=pltpu.CompilerParams(has_side_effects=_DATAFLOW),
    )(*[pltpu.with_memory_space_constraint(a, pltpu.HBM) for a in srcs], *lands, after)
    return (outs[0], outs[1], outs[2:2 + n], outs[2 + n:2 + 2 * n], to_all), outs[-1]


def _exchange_wait(handle, after, name):
    send_sems, recv_sems, srcs, lands, to_all = handle
    n = len(srcs)

    def body(*refs):
        out, back = _direct_copies(refs[:n], refs[n:2 * n], refs[2 * n], refs[2 * n + 1], to_all)
        for cp in out:
            cp.wait_send()
        for cp in back:
            cp.wait_recv()

    outs = pl.pallas_call(
        body, name=name,
        out_shape=tuple(pltpu.HBM(a.shape, a.dtype) for a in (*srcs, *lands)),
        in_specs=(_HBM,) * (2 * n) + (_SEM, _SEM, _ANY), out_specs=(_HBM,) * (2 * n),
        input_output_aliases={i: i for i in range(2 * n)},
        compiler_params=pltpu.CompilerParams(has_side_effects=_DATAFLOW),
    )(*srcs, *lands, send_sems, recv_sems, after)
    return outs[:n], outs[n:]


def _with_own(landed, own):
    my = _flat(*_position())
    return lax.dynamic_update_slice(landed, own[None], (my, 0, 0))


def _sum_slabs(slabs, own, name):
    n, r, c_ = slabs.shape
    tr = _pick_tile(r, 16, max(16, (12 << 20) // (n * c_ * slabs.dtype.itemsize)))

    def body(x_ref, own_ref, o_ref):
        my = _flat(*_position())
        acc = None
        for k in range(n):
            term = jnp.where(my == k, own_ref[...], x_ref[k]).astype(_F32)
            acc = term if acc is None else acc + term
        o_ref[...] = acc

    return pl.pallas_call(
        body, name=name, grid=(r // tr,),
        in_specs=[pl.BlockSpec((n, tr, c_), lambda i: (0, i, 0)), pl.BlockSpec((tr, c_), lambda i: (i, 0))],
        out_specs=pl.BlockSpec((tr, c_), lambda i: (i, 0)),
        out_shape=jax.ShapeDtypeStruct((r, c_), _F32),
        compiler_params=_params("parallel"),
    )(slabs, own)


def _adamw(w, g, m, v, name):
    r, c_ = w.shape
    tr = _pick_tile(r, 8, 512)

    def body(w_ref, g_ref, m_ref, v_ref, d_ref, nm_ref, nv_ref):
        _adamw_refs(w_ref, g_ref, m_ref, v_ref, d_ref, nm_ref, nv_ref)

    spec = pl.BlockSpec((tr, c_), lambda i: (i, 0))
    shape = jax.ShapeDtypeStruct((r, c_), _F32)
    return pl.pallas_call(
        body, name=name, grid=(r // tr,), in_specs=[spec] * 4, out_specs=[spec] * 3, out_shape=[shape] * 3,
        compiler_params=_params("parallel"),
    )(w, g, m, v)


def _adamw_refs(w_ref, g_ref, m_ref, v_ref, d_ref, nm_ref, nv_ref):
    gv = g_ref[...]
    nm = ADAM_B1 * m_ref[...] + (1.0 - ADAM_B1) * gv
    nv = ADAM_B2 * v_ref[...] + (1.0 - ADAM_B2) * (gv * gv)
    m_hat = nm / (1.0 - ADAM_B1 ** ADAM_STEP)
    v_hat = nv / (1.0 - ADAM_B2 ** ADAM_STEP)
    d_ref[...] = -ADAM_LR * (m_hat / (jnp.sqrt(v_hat) + ADAM_EPS) + ADAM_WD * w_ref[...])
    nm_ref[...] = nm
    nv_ref[...] = nv


def _adamw_small(params, name):
    n = len(params)

    def body(*refs):
        for i in range(n):
            _adamw_refs(*refs[4 * i:4 * i + 4], *refs[4 * n + 3 * i:4 * n + 3 * i + 3])

    outs = pl.pallas_call(
        body, name=name,
        out_shape=[jax.ShapeDtypeStruct(p[0].shape, _F32) for p in params for _ in range(3)],
        compiler_params=pltpu.CompilerParams(vmem_limit_bytes=VMEM_LIMIT),
    )(*[a for p in params for a in p])
    return [tuple(outs[3 * i:3 * i + 3]) for i in range(n)]


_SMALL = ("norm1_g", "conv_a_w", "conv_b_w", "conv_b_b", "ln_b_g", "ln_b_b", "ln_c_g", "ln_c_b", "sgu_w", "sgu_b",
          "norm2_g", "conv_f_w", "final_g")
_CONV_SHARDED = ("conv_a_w", "conv_b_w", "conv_f_w")
_NAMES = ("norm1_g", "w_in", "conv_a_w", "conv_b_w", "conv_b_b", "ln_b_g", "ln_b_b", "ln_c_g", "ln_c_b", "sgu_w", "sgu_b",
          "w_out", "norm2_g", "w_up", "conv_f_w", "w_down", "final_g")


def _pack_rows(parts, lanes=128, row_multiple=8):
    flat = jnp.concatenate([a.reshape(-1) for a in parts])
    rows = -(-flat.shape[0] // lanes)
    rows = -(-rows // row_multiple) * row_multiple
    return jnp.pad(flat, (0, rows * lanes - flat.shape[0])).reshape(rows, lanes)


def _unpack_rows(packed, shapes):
    flat = packed.reshape(-1)
    out, off = [], 0
    for shp in shapes:
        size = 1
        for s in shp:
            size *= s
        out.append(flat[off:off + size].reshape(shp))
        off += size
    return out


def _gather_conv_weights(conv_a_w, conv_b_w, conv_f_w, after):
    shards = (conv_a_w, conv_b_w, conv_f_w)
    flat = _all_gather(_pack_rows(shards), "gather_conv_weights", after).reshape(N_DEV, -1)
    full, off = [], 0
    for s in shards:
        layers, taps, width = s.shape
        per_dev = flat[:, off:off + s.size].reshape(N_DEV, layers, taps, width)
        full.append(jnp.moveaxis(per_dev, 0, 2).reshape(layers, taps, N_DEV * width))
        off += s.size
    return full


def kernel(x, norm1_g, w_in, conv_a_w, conv_b_w, conv_b_b, ln_b_g, ln_b_b, ln_c_g, ln_c_b, sgu_w, sgu_b, w_out, norm2_g, w_up, conv_f_w, w_down, final_g, loss_target, m_norm1_g, m_w_in, m_conv_a_w, m_conv_b_w, m_conv_b_b, m_ln_b_g, m_ln_b_b, m_ln_c_g, m_ln_c_b, m_sgu_w, m_sgu_b, m_w_out, m_norm2_g, m_w_up, m_conv_f_w, m_w_down, m_final_g, v_norm1_g, v_w_in, v_conv_a_w, v_conv_b_w, v_conv_b_b, v_ln_b_g, v_ln_b_b, v_ln_c_g, v_ln_c_b, v_sgu_w, v_sgu_b, v_w_out, v_norm2_g, v_w_up, v_conv_f_w, v_w_down, v_final_g):
    weights = dict(norm1_g=norm1_g, w_in=w_in, conv_a_w=conv_a_w, conv_b_w=conv_b_w, conv_b_b=conv_b_b, ln_b_g=ln_b_g,
                   ln_b_b=ln_b_b, ln_c_g=ln_c_g, ln_c_b=ln_c_b, sgu_w=sgu_w, sgu_b=sgu_b, w_out=w_out, norm2_g=norm2_g,
                   w_up=w_up, conv_f_w=conv_f_w, w_down=w_down, final_g=final_g)
    mom1 = dict(norm1_g=m_norm1_g, w_in=m_w_in, conv_a_w=m_conv_a_w, conv_b_w=m_conv_b_w, conv_b_b=m_conv_b_b,
                ln_b_g=m_ln_b_g, ln_b_b=m_ln_b_b, ln_c_g=m_ln_c_g, ln_c_b=m_ln_c_b, sgu_w=m_sgu_w, sgu_b=m_sgu_b,
                w_out=m_w_out, norm2_g=m_norm2_g, w_up=m_w_up, conv_f_w=m_conv_f_w, w_down=m_w_down, final_g=m_final_g)
    mom2 = dict(norm1_g=v_norm1_g, w_in=v_w_in, conv_a_w=v_conv_a_w, conv_b_w=v_conv_b_w, conv_b_b=v_conv_b_b,
                ln_b_g=v_ln_b_g, ln_b_b=v_ln_b_b, ln_c_g=v_ln_c_g, ln_c_b=v_ln_c_b, sgu_w=v_sgu_w, sgu_b=v_sgu_b,
                w_out=v_w_out, norm2_g=v_norm2_g, w_up=v_w_up, conv_f_w=v_conv_f_w, w_down=v_w_down, final_g=v_final_g)
    n_ex, seq, d = x.shape
    depth = w_in.shape[0]
    assert depth == 2
    my = _flat(*_position())
    row = lambda a, l: a[l][None]
    tied = lambda a, token: a + token[0:1, 0:1]

    slab = {"w_in": [_bf(jnp.swapaxes(w_in, 1, 2)[l]) for l in range(depth)], "w_out": [_bf(w_out[l]) for l in range(depth)],
            "w_up": [_bf(jnp.swapaxes(w_up, 1, 2)[l]) for l in range(depth)], "w_down": [_bf(w_down[l]) for l in range(depth)]}
    rows = {name: parts[0].shape[0] for name, parts in slab.items()}
    key_of = {"w_in": "w_in_t", "w_out": "w_out", "w_up": "w_up_t", "w_down": "w_down"}
    rest_layer0 = [("w_out", 0), ("w_up", 0), ("w_down", 0)]
    all_layer1 = [("w_in", 1), ("w_out", 1), ("w_up", 1), ("w_down", 1)]

    w_in0 = _all_gather(slab["w_in"][0], "gather_w_in0", norm1_g)
    conv_a_full, conv_b_full, conv_f_full = _gather_conv_weights(conv_a_w, conv_b_w, conv_f_w, w_in0)
    gather0, token = _exchange_start([slab[n][l] for n, l in rest_layer0], True, conv_f_full, "gather_layer0_start")
    w = {
        "norm1_g": [row(norm1_g, l) for l in range(depth)], "w_in_t": [None] * depth,
        "conv_a_w": [conv_a_full[l] for l in range(depth)], "conv_b_w": [conv_b_full[l] for l in range(depth)],
        "conv_b_b": [row(conv_b_b, l) for l in range(depth)], "ln_b_g": [row(ln_b_g, l) for l in range(depth)],
        "ln_b_b": [row(ln_b_b, l) for l in range(depth)], "ln_c_g": [row(ln_c_g, l) for l in range(depth)],
        "ln_c_b": [row(ln_c_b, l) for l in range(depth)], "sgu_w": [sgu_w[l] for l in range(depth)],
        "sgu_b_full": [jnp.repeat(sgu_b[l].T, HEAD_DIM, axis=1) for l in range(depth)],
        "w_out": [None] * depth, "norm2_g": [row(norm2_g, l) for l in range(depth)], "w_up_t": [None] * depth,
        "conv_f_w": [conv_f_full[l] for l in range(depth)], "w_down": [None] * depth, "final_g": final_g[None],
    }
    w["w_in_t"][0] = w_in0.reshape(N_DEV * rows["w_in"], d)
    w["norm1_g"][0] = tied(row(norm1_g, 0), token)

    def land_weights(handle, after, which, name):
        owns, landed = _exchange_wait(handle, after, name)
        for (n, l), own, got in zip(which, owns, landed):
            w[key_of[n]][l] = _with_own(got, own).reshape(N_DEV * rows[n], d)
        return landed[0]

    st0 = _fwd_mix(x.reshape(n_ex * seq, d), w, 0, n_ex)
    landed0 = land_weights(gather0, st0["mix"][3], rest_layer0, "gather_layer0_wait")
    gather1, token = _exchange_start([slab[n][l] for n, l in all_layer1], True, landed0, "gather_layer1_start")
    w["norm2_g"][0] = tied(row(norm2_g, 0), token)
    _fwd_ffn(st0, w, 0, n_ex)
    x_mid = _down_proj(st0, w, 0)
    land_weights(gather1, x_mid, all_layer1, "gather_layer1_wait")
    st1 = _fwd_mix(x_mid, w, 1, n_ex)
    _fwd_ffn(st1, w, 1, n_ex)
    dx, dxb, d_final_g, loss = _down_proj_loss(st1, w, 1, loss_target.reshape(n_ex * seq, d))
    loss = lax.psum(loss[0, 0], ("x", "y", "c"))

    def send_grads(g, which, after, name):
        return _exchange_start([g[key_of[n]].reshape(N_DEV, rows[n], d) for n, _ in which], False, after, name)

    dx, dxb, g_ffn1 = _bwd_ffn(st1, w, 1, dx, dxb, n_ex)
    dx, dxb, g_mix1 = _bwd_mix(st1, w, 1, dx, dxb, n_ex)
    grads1, token = send_grads({**g_ffn1, **g_mix1}, all_layer1, dx, "exchange_layer1_start")
    w["norm2_g"][0] = tied(row(norm2_g, 0), token)
    dx, dxb, g_ffn0 = _bwd_ffn(st0, w, 0, dx, dxb, n_ex)
    g_ffn0["w_out"] = _mm_tn_parts(st0["mix"], dxb, "out_proj_dw")
    ffn_layer0 = [("w_out", 0), ("w_up", 0), ("w_down", 0)]
    grads0a, token = send_grads(g_ffn0, ffn_layer0, dxb, "exchange_ffn0_start")
    dmix = _mm_nt(dxb, w["w_out"][0], "out_proj_dx", after=token)
    dx, dxb, g_mix0, dp0 = _bwd_mixers(st0, w, 0, dx, dmix, n_ex)
    grad_x = dx.reshape(n_ex, seq, d)
    g = {k: [{**g_ffn0, **g_mix0}[k], {**g_ffn1, **g_mix1}[k]] for k in g_mix0.keys() | g_ffn0.keys()}
    g["final_g"] = d_final_g

    small_local = {
        "norm1_g": jnp.stack([a[0] for a in g["norm1_g"]]), "conv_a_w": jnp.stack(g["conv_a_w"]),
        "conv_b_w": jnp.stack(g["conv_b_w"]), "conv_b_b": jnp.stack([a[0] for a in g["conv_b_b"]]),
        "ln_b_g": jnp.stack([a[0] for a in g["ln_b_g"]]), "ln_b_b": jnp.stack([a[0] for a in g["ln_b_b"]]),
        "ln_c_g": jnp.stack([a[0] for a in g["ln_c_g"]]), "ln_c_b": jnp.stack([a[0] for a in g["ln_c_b"]]),
        "sgu_w": jnp.stack(g["sgu_w"]), "sgu_b": jnp.stack([a.T for a in g["sgu_b_t"]]),
        "norm2_g": jnp.stack([a[0] for a in g["norm2_g"]]), "conv_f_w": jnp.stack(g["conv_f_w"]),
        "final_g": g["final_g"][0],
    }
    small, token = _exchange_start([_pack_rows([small_local[k] for k in _SMALL])], True, dx, "gather_small_start")
    g_mix0["w_in_t"] = _mm_tn_parts(dp0, st0["h1"], "in_proj_dw", after=token)
    mix_layer0 = [("w_in", 0)]
    grads0b, token = send_grads(g_mix0, mix_layer0, dx, "exchange_mix0_start")

    reduced = {}

    def land_grads(handle, after, which, name):
        sent, landed = _exchange_wait(handle, after, name + "_wait")
        for (n, l), src, got in zip(which, sent, landed):
            own = lax.dynamic_index_in_dim(src, my, 0, keepdims=False)
            reduced[(n, l)] = _sum_slabs(got, own, name + "_sum_" + n)
        return reduced[which[-1]]

    def stacked_grad(name):
        stacked = jnp.stack([reduced[(name, l)] for l in range(depth)])
        return jnp.swapaxes(stacked, 1, 2) if name in ("w_in", "w_up") else stacked

    done = land_grads(grads1, token, all_layer1, "exchange_layer1")
    land_grads(grads0a, done, ffn_layer0, "exchange_ffn0")
    grads = {name: stacked_grad(name) for name in ("w_out", "w_up", "w_down")}

    delta, new_m, new_v = {}, {}, {}

    def as_2d(name):
        shp = weights[name].shape
        two_d = (-1, shp[-1]) if len(shp) > 1 else (1, shp[0])
        return tuple(a.reshape(two_d) for a in (weights[name], grads[name], mom1[name], mom2[name]))

    def keep(name, outs):
        delta[name], new_m[name], new_v[name] = (o.reshape(weights[name].shape) for o in outs)

    for name in ("w_up", "w_down", "w_out"):
        keep(name, _adamw(*as_2d(name), "adamw_" + name))

    (own,), (landed,) = _exchange_wait(small, new_v["w_out"], "gather_small_wait")
    small_sum = _sum_slabs(landed, own, "sum_small_grads")
    for name, total in zip(_SMALL, _unpack_rows(small_sum, [small_local[k].shape for k in _SMALL])):
        if name in _CONV_SHARDED:
            width = weights[name].shape[-1]
            total = lax.dynamic_slice_in_dim(total, my * width, width, axis=-1)
        grads[name] = total
    at_least_2d = lambda a: a[None] if a.ndim == 1 else a
    small_params = [tuple(at_least_2d(a) for a in (weights[n], grads[n], mom1[n], mom2[n])) for n in _SMALL]
    for name, outs in zip(_SMALL, _adamw_small(small_params, "adamw_small")):
        keep(name, outs)

    land_grads(grads0b, new_v["final_g"], mix_layer0, "exchange_mix0")
    grads["w_in"] = stacked_grad("w_in")
    keep("w_in", _adamw(*as_2d("w_in"), "adamw_w_in"))

    return (loss, grad_x, *[grads[n] for n in _NAMES], *[delta[n] for n in _NAMES], *[new_m[n] for n in _NAMES],
            *[new_v[n] for n in _NAMES])
```

```python
import jax
import jax.numpy as jnp
from jax import lax
from jax.experimental import pallas as pl
from jax.experimental.pallas import tpu as pltpu

_F32 = jnp.float32
_BF = jnp.bfloat16

HEAD_DIM = 64
MIX_W = 256
N_HEADS = MIX_W // HEAD_DIM
CHUNK = 128
KV_BLOCK = 128
K_SHORT = 3
K_CONF = 31
K_FFN = 3
RMS_EPS = 1e-6
LN_EPS = 1e-5
ADAM_LR = 0.001
ADAM_B1 = 0.9
ADAM_B2 = 0.999
ADAM_EPS = 1e-08
ADAM_WD = 0.01
ADAM_STEP = 10
N_DEV = 8
VMEM_LIMIT = 56 * 1024 * 1024


def _bf(x):
    return x.astype(_BF)


def _ld(ref):
    return ref[...].astype(_F32)


_ANY_SPEC = pl.BlockSpec(memory_space=pl.ANY)


def _params(*sem):
    return pltpu.CompilerParams(dimension_semantics=sem, vmem_limit_bytes=VMEM_LIMIT)


def _dot(a, b):
    return jnp.dot(a, b, preferred_element_type=_F32)


def _dot_nt(a, b):
    return lax.dot_general(a, b, (((1,), (1,)), ((), ())), preferred_element_type=_F32)


def _dot_tn(a, b):
    return lax.dot_general(a, b, (((0,), (0,)), ((), ())), preferred_element_type=_F32)


def _row_tile(t, want):
    return want if t % want == 0 else t


def _pick_tile(rows, unit, max_rows):
    best = 0
    for cand in range(unit, min(rows, max_rows) + 1, unit):
        if rows % cand == 0:
            best = cand
    return best or rows


def _sigmoid(x):
    return 1.0 / (1.0 + jnp.exp(-x))


def _rms_rstd(x):
    return lax.rsqrt(jnp.mean(x * x, axis=-1, keepdims=True) + RMS_EPS)


def _norm_mm(x, g, w_t, name):
    t, d = x.shape
    n = w_t.shape[0]
    tm = _row_tile(t, 512)
    tn = _row_tile(n, 512)

    def body(x_ref, g_ref, w_ref, p_ref, h_ref):
        xv = x_ref[...]
        h = _bf(xv * _rms_rstd(xv) * g_ref[...])
        h_ref[...] = h
        for n0 in range(0, n, tn):
            p_ref[:, n0:n0 + tn] = _bf(_dot_nt(h, w_ref[n0:n0 + tn, :]))

    return pl.pallas_call(
        body, name=name, grid=(t // tm,),
        in_specs=[pl.BlockSpec((tm, d), lambda i: (i, 0)), pl.BlockSpec((1, d), lambda i: (0, 0)),
                  pl.BlockSpec((n, d), lambda i: (0, 0))],
        out_specs=[pl.BlockSpec((tm, n), lambda i: (i, 0)), pl.BlockSpec((tm, d), lambda i: (i, 0))],
        out_shape=[jax.ShapeDtypeStruct((t, n), _BF), jax.ShapeDtypeStruct((t, d), _BF)],
        compiler_params=_params("parallel"),
    )(x, g, w_t)


def _mm_nt(a, w_t, name, after=None):
    t, k = a.shape
    n = w_t.shape[0]
    tm = _row_tile(t, 512)
    tn = _row_tile(n, 512) if n % 512 == 0 else _row_tile(n, 256)

    def body(a_ref, w_ref, *rest):
        o_ref = rest[-1]
        av = a_ref[...]
        for n0 in range(0, n, tn):
            o_ref[:, n0:n0 + tn] = _bf(_dot_nt(av, w_ref[n0:n0 + tn, :]))

    extra = () if after is None else (after,)
    return pl.pallas_call(
        body, name=name, grid=(t // tm,),
        in_specs=[pl.BlockSpec((tm, k), lambda i: (i, 0)), pl.BlockSpec((n, k), lambda i: (0, 0))] + [_ANY_SPEC] * len(extra),
        out_specs=pl.BlockSpec((tm, n), lambda i: (i, 0)),
        out_shape=jax.ShapeDtypeStruct((t, n), _BF),
        compiler_params=_params("parallel"),
    )(a, w_t, *extra)


def _mm_res(parts, w, x, name):
    t = x.shape[0]
    k, d = w.shape
    tm = _row_tile(t, 512)
    widths = [a.shape[1] for a in parts]
    n_parts = len(parts)

    def body(*refs):
        w_ref, x_ref, o_ref = refs[n_parts:]
        acc, off = x_ref[...], 0
        for a_ref, width in zip(refs[:n_parts], widths):
            acc = acc + _dot(a_ref[...], w_ref[off:off + width, :])
            off += width
        o_ref[...] = acc

    return pl.pallas_call(
        body, name=name, grid=(t // tm,),
        in_specs=[pl.BlockSpec((tm, width), lambda i: (i, 0)) for width in widths] + [
            pl.BlockSpec((k, d), lambda i: (0, 0)), pl.BlockSpec((tm, d), lambda i: (i, 0))],
        out_specs=pl.BlockSpec((tm, d), lambda i: (i, 0)),
        out_shape=jax.ShapeDtypeStruct((t, d), _F32),
        compiler_params=_params("parallel"),
    )(*parts, w, x)


def _res_norm_mm(parts, w_res, x, g, w_t, name):
    t, d = x.shape
    k = w_res.shape[0]
    n = w_t.shape[0]
    tm = _row_tile(t, 512)
    tn = _row_tile(n, 512)
    widths = [a.shape[1] for a in parts]
    n_parts = len(parts)

    def body(*refs):
        wr_ref, x_ref, g_ref, wt_ref, xo_ref, p_ref, h_ref = refs[n_parts:]
        xv, off = x_ref[...], 0
        for a_ref, width in zip(refs[:n_parts], widths):
            xv = xv + _dot(a_ref[...], wr_ref[off:off + width, :])
            off += width
        xo_ref[...] = xv
        h = _bf(xv * _rms_rstd(xv) * g_ref[...])
        h_ref[...] = h
        for n0 in range(0, n, tn):
            p_ref[:, n0:n0 + tn] = _bf(_dot_nt(h, wt_ref[n0:n0 + tn, :]))

    row = lambda width: pl.BlockSpec((tm, width), lambda i: (i, 0))
    const = lambda shape: pl.BlockSpec(shape, lambda i: (0, 0))
    return pl.pallas_call(
        body, name=name, grid=(t // tm,),
        in_specs=[row(width) for width in widths] + [const((k, d)), row(d), const((1, d)), const((n, d))],
        out_specs=[row(d), row(n), row(d)],
        out_shape=[jax.ShapeDtypeStruct((t, d), _F32), jax.ShapeDtypeStruct((t, n), _BF), jax.ShapeDtypeStruct((t, d), _BF)],
        compiler_params=_params("parallel"),
    )(*parts, w_res, x, g, w_t)


def _res_final_loss(parts, w_res, x, g, target, name):
    t, d = x.shape
    k = w_res.shape[0]
    tm = _row_tile(t, 512)
    widths = [a.shape[1] for a in parts]
    n_parts = len(parts)

    def body(*refs):
        wr_ref, x_ref, g_ref, t_ref, dx_ref, dxb_ref, dg_ref, loss_ref = refs[n_parts:]
        xv, off = x_ref[...], 0
        for a_ref, width in zip(refs[:n_parts], widths):
            xv = xv + _dot(a_ref[...], wr_ref[off:off + width, :])
            off += width
        _loss_head(xv, g_ref, t_ref, dx_ref, dxb_ref, dg_ref, loss_ref, d)

    row = lambda width: pl.BlockSpec((tm, width), lambda i: (i, 0))
    const = lambda shape: pl.BlockSpec(shape, lambda i: (0, 0))
    return pl.pallas_call(
        body, name=name, grid=(t // tm,),
        in_specs=[row(width) for width in widths] + [const((k, d)), row(d), const((1, d)), row(d)],
        out_specs=[row(d), row(d), const((1, d)), const((1, 1))],
        out_shape=[jax.ShapeDtypeStruct((t, d), _F32), jax.ShapeDtypeStruct((t, d), _BF),
                   jax.ShapeDtypeStruct((1, d), _F32), jax.ShapeDtypeStruct((1, 1), _F32)],
        compiler_params=_params("arbitrary"),
    )(*parts, w_res, x, g, target)


def _loss_head(xv, g_ref, t_ref, dx_ref, dxb_ref, dg_ref, loss_ref, d):
    rstd = _rms_rstd(xv)
    xn = xv * rstd
    err = xn * g_ref[...] - t_ref[...]
    dy = err * (1.0 / d)
    u = dy * g_ref[...]
    dx = rstd * (u - xn * jnp.mean(u * xn, axis=-1, keepdims=True))
    dx_ref[...] = dx
    dxb_ref[...] = _bf(dx)

    @pl.when(pl.program_id(0) == 0)
    def _():
        dg_ref[...] = jnp.zeros_like(dg_ref)
        loss_ref[...] = jnp.zeros_like(loss_ref)

    dg_ref[...] += jnp.sum(dy * xn, axis=0, keepdims=True)
    loss_ref[...] += (0.5 / d) * jnp.sum(jnp.sum(err * err, axis=1, keepdims=True), axis=0, keepdims=True)


def _mm_normbwd(parts, w, x, g, dres, name):
    t = x.shape[0]
    k, d = w.shape
    tm = _row_tile(t, 512)
    widths = [a.shape[1] for a in parts]
    n_parts = len(parts)

    def body(*refs):
        a_refs = refs[:n_parts]
        w_ref, x_ref, g_ref, r_ref, dx_ref, dxb_ref, dg_ref = refs[n_parts:]
        dh, off = None, 0
        for a_ref, width in zip(a_refs, widths):
            term = _dot(_bf(a_ref[...]), w_ref[off:off + width, :])
            dh = term if dh is None else dh + term
            off += width
        xv = x_ref[...]
        rstd = _rms_rstd(xv)
        xn = xv * rstd
        u = dh * g_ref[...]
        dx = r_ref[...] + rstd * (u - xn * jnp.mean(u * xn, axis=-1, keepdims=True))
        dx_ref[...] = dx
        dxb_ref[...] = _bf(dx)

        @pl.when(pl.program_id(0) == 0)
        def _():
            dg_ref[...] = jnp.zeros_like(dg_ref)

        dg_ref[...] += jnp.sum(dh * xn, axis=0, keepdims=True)

    return pl.pallas_call(
        body, name=name, grid=(t // tm,),
        in_specs=[pl.BlockSpec((tm, width), lambda i: (i, 0)) for width in widths] + [
            pl.BlockSpec((k, d), lambda i: (0, 0)),
            pl.BlockSpec((tm, d), lambda i: (i, 0)), pl.BlockSpec((1, d), lambda i: (0, 0)),
            pl.BlockSpec((tm, d), lambda i: (i, 0))],
        out_specs=[pl.BlockSpec((tm, d), lambda i: (i, 0)), pl.BlockSpec((tm, d), lambda i: (i, 0)),
                   pl.BlockSpec((1, d), lambda i: (0, 0))],
        out_shape=[jax.ShapeDtypeStruct((t, d), _F32), jax.ShapeDtypeStruct((t, d), _BF),
                   jax.ShapeDtypeStruct((1, d), _F32)],
        compiler_params=_params("arbitrary"),
    )(*parts, w, x, g, dres)


def _mm_tn(a, b, name, out_dtype):
    t, m = a.shape
    n = b.shape[1]
    tm = _pick_tile(m, 128, 1408)
    tn = _pick_tile(n, 128, 1024)
    tk = _row_tile(t, 1024)
    nk = t // tk

    def body(a_ref, b_ref, o_ref, acc):
        kk = pl.program_id(2)

        @pl.when(kk == 0)
        def _():
            acc[...] = jnp.zeros_like(acc)

        acc[...] += _dot_tn(_bf(a_ref[...]), b_ref[...])

        @pl.when(kk == nk - 1)
        def _():
            o_ref[...] = acc[...].astype(o_ref.dtype)

    return pl.pallas_call(
        body, name=name, grid=(m // tm, n // tn, nk),
        in_specs=[pl.BlockSpec((tk, tm), lambda i, j, kk: (kk, i)), pl.BlockSpec((tk, tn), lambda i, j, kk: (kk, j))],
        out_specs=pl.BlockSpec((tm, tn), lambda i, j, kk: (i, j)),
        out_shape=jax.ShapeDtypeStruct((m, n), out_dtype),
        scratch_shapes=[pltpu.VMEM((tm, tn), _F32)],
        compiler_params=_params("parallel", "parallel", "arbitrary"),
    )(a, b)


def _mm_tn_halves(a0, a1, b, name):
    t, m = a0.shape
    n = b.shape[1]
    tm = _pick_tile(m, 128, 1408)
    tn = _pick_tile(n, 128, 1024)
    tk = _row_tile(t, 1024)
    nk = t // tk
    half = m // tm

    def body(a0_ref, a1_ref, b_ref, o_ref, acc):
        i = pl.program_id(0)
        kk = pl.program_id(2)

        @pl.when(kk == 0)
        def _():
            acc[...] = jnp.zeros_like(acc)

        @pl.when(i < half)
        def _():
            acc[...] += _dot_tn(a0_ref[...], b_ref[...])

        @pl.when(i >= half)
        def _():
            acc[...] += _dot_tn(a1_ref[...], b_ref[...])

        @pl.when(kk == nk - 1)
        def _():
            o_ref[...] = _bf(acc[...])

    return pl.pallas_call(
        body, name=name, grid=(2 * half, n // tn, nk),
        in_specs=[pl.BlockSpec((tk, tm), lambda i, j, kk: (jnp.where(i < half, kk, 0), jnp.minimum(i, half - 1))),
                  pl.BlockSpec((tk, tm), lambda i, j, kk: (jnp.where(i >= half, kk, 0), jnp.maximum(i - half, 0))),
                  pl.BlockSpec((tk, tn), lambda i, j, kk: (kk, j))],
        out_specs=pl.BlockSpec((tm, tn), lambda i, j, kk: (i, j)),
        out_shape=jax.ShapeDtypeStruct((2 * m, n), _BF),
        scratch_shapes=[pltpu.VMEM((tm, tn), _F32)],
        compiler_params=_params("parallel", "parallel", "arbitrary"),
    )(a0, a1, b)


def _mm_tn_parts(parts, b, name, after=None):
    t, n = b.shape
    widths = [a.shape[1] for a in parts]
    m = sum(widths)
    n_parts = len(parts)
    tk = _row_tile(t, 1024)
    nk = t // tk
    extra = () if after is None else (after,)

    def body(*refs):
        b_ref = refs[n_parts]
        o_ref, acc = refs[-2:]
        kk = pl.program_id(0)

        @pl.when(kk == 0)
        def _():
            acc[...] = jnp.zeros_like(acc)

        bv = b_ref[...]
        off = 0
        for a_ref, width in zip(refs[:n_parts], widths):
            acc[off:off + width, :] += _dot_tn(_bf(a_ref[...]), bv)
            off += width

        @pl.when(kk == nk - 1)
        def _():
            o_ref[...] = _bf(acc[...])

    return pl.pallas_call(
        body, name=name, grid=(nk,),
        in_specs=[pl.BlockSpec((tk, width), lambda kk: (kk, 0)) for width in widths] + [pl.BlockSpec((tk, n), lambda kk: (kk, 0))]
        + [_ANY_SPEC] * len(extra),
        out_specs=pl.BlockSpec((m, n), lambda kk: (0, 0)),
        out_shape=jax.ShapeDtypeStruct((m, n), _BF),
        scratch_shapes=[pltpu.VMEM((m, n), _F32)],
        compiler_params=_params("arbitrary"),
    )(*parts, b, *extra)


def _pad_rows(x, pad):
    return jnp.concatenate([x, jnp.zeros((pad, x.shape[1]), x.dtype)], axis=0)


def _shift_down(xp, s):
    return xp if s == 0 else pltpu.roll(xp, s, 0)


def _shift_up(xp, s):
    return xp if s == 0 else pltpu.roll(xp, xp.shape[0] - s, 0)


def _taps3(xp):
    one = _shift_down(xp, 1)
    return xp, one, _shift_down(one, 1)


def _conv3_taps(taps, w_ref):
    return w_ref[2:3, :] * taps[0] + w_ref[1:2, :] * taps[1] + w_ref[0:1, :] * taps[2]


def _conv3(xp, w_ref):
    return _conv3_taps(_taps3(xp), w_ref)


def _conv3_t(dyp, w_ref):
    one = _shift_up(dyp, 1)
    return w_ref[2:3, :] * dyp + w_ref[1:2, :] * one + w_ref[0:1, :] * _shift_up(one, 1)


def _conv3_dw(dyp, taps):
    return [jnp.sum(dyp * taps[2 - k], axis=0, keepdims=True) for k in range(3)]


def _ffn_mid_fwd(up_pre, wf, n_ex, name):
    t, f2 = up_pre.shape
    f = f2 // 2
    s = t // n_ex
    cb = MIX_W
    nb = f // cb

    def body(ug_ref, uv_ref, wg_ref, wv_ref, act_ref, gf_ref, vf_ref):
        gf = _conv3(_pad_rows(ug_ref[...].astype(_F32), 8), wg_ref)[:s]
        vf = _conv3(_pad_rows(uv_ref[...].astype(_F32), 8), wv_ref)[:s]
        act_ref[...] = _bf(gf * _sigmoid(gf) * vf)
        gf_ref[...] = _bf(gf)
        vf_ref[...] = _bf(vf)

    out = pl.BlockSpec((s, cb), lambda e, j: (e, j))
    return pl.pallas_call(
        body, name=name, grid=(n_ex, nb),
        in_specs=[pl.BlockSpec((s, cb), lambda e, j: (e, j)), pl.BlockSpec((s, cb), lambda e, j: (e, j + nb)),
                  pl.BlockSpec((K_FFN, cb), lambda e, j: (0, j)), pl.BlockSpec((K_FFN, cb), lambda e, j: (0, j + nb))],
        out_specs=[out, out, out],
        out_shape=[jax.ShapeDtypeStruct((t, f), _BF)] * 3,
        compiler_params=_params("parallel", "parallel"),
    )(up_pre, up_pre, wf, wf)


def _ffn_mid_bwd(up_pre, conv_g, conv_v, wf, dact, n_ex, name):
    t, f2 = up_pre.shape
    f = f2 // 2
    s = t // n_ex
    cb = MIX_W
    nb = f // cb

    def body(ug_ref, uv_ref, gf_ref, vf_ref, wg_ref, wv_ref, da_ref, dug_ref, duv_ref, dwg_ref, dwv_ref):
        gf = _ld(gf_ref)
        vf = _ld(vf_ref)
        sg = _sigmoid(gf)
        da = _ld(da_ref)

        @pl.when(pl.program_id(1) == 0)
        def _():
            dwg_ref[...] = jnp.zeros_like(dwg_ref)
            dwv_ref[...] = jnp.zeros_like(dwv_ref)

        def finish(dpost, w_ref, x_ref, du_ref, dw_ref):
            ahead = [_pad_rows(dpost, 8)]
            ahead.append(_shift_up(ahead[0], 1))
            ahead.append(_shift_up(ahead[1], 1))
            du_ref[...] = _bf((w_ref[2:3, :] * ahead[0] + w_ref[1:2, :] * ahead[1] + w_ref[0:1, :] * ahead[2])[:s])
            x = _ld(x_ref)
            for k in range(K_FFN):
                dw_ref[k:k + 1, :] += jnp.sum(ahead[2 - k][:s] * x, axis=0, keepdims=True)

        finish(da * vf * sg * (1.0 + gf * (1.0 - sg)), wg_ref, ug_ref, dug_ref, dwg_ref)
        finish(da * gf * sg, wv_ref, uv_ref, duv_ref, dwv_ref)

    return pl.pallas_call(
        body, name=name, grid=(nb, n_ex),
        in_specs=[pl.BlockSpec((s, cb), lambda j, e: (e, j)), pl.BlockSpec((s, cb), lambda j, e: (e, j + nb)),
                  pl.BlockSpec((s, cb), lambda j, e: (e, j)), pl.BlockSpec((s, cb), lambda j, e: (e, j)),
                  pl.BlockSpec((K_FFN, cb), lambda j, e: (0, j)), pl.BlockSpec((K_FFN, cb), lambda j, e: (0, j + nb)),
                  pl.BlockSpec((s, cb), lambda j, e: (e, j))],
        out_specs=[pl.BlockSpec((s, cb), lambda j, e: (e, j)), pl.BlockSpec((s, cb), lambda j, e: (e, j)),
                   pl.BlockSpec((K_FFN, cb), lambda j, e: (0, j)), pl.BlockSpec((K_FFN, cb), lambda j, e: (0, j))],
        out_shape=[jax.ShapeDtypeStruct((t, f), _BF), jax.ShapeDtypeStruct((t, f), _BF),
                   jax.ShapeDtypeStruct((K_FFN, f), _F32), jax.ShapeDtypeStruct((K_FFN, f), _F32)],
        compiler_params=_params("parallel", "arbitrary"),
    )(up_pre, up_pre, conv_g, conv_v, wf, wf, dact)


def _pcol(s, j):
    return pl.BlockSpec((s, MIX_W), lambda e, j=j: (e, j))


def _vec(rows=1):
    return pl.BlockSpec((rows, MIX_W), lambda e: (0, 0))


def _mix_a_fwd(p, wa, n_ex, name):
    t = p.shape[0]
    s = t // n_ex

    def body(gb_ref, gc_ref, ha_ref, w_ref, y_ref):
        cv = _conv3(_pad_rows(_ld(gc_ref) * _ld(ha_ref), 8), w_ref)[:s]
        y_ref[...] = _bf(_ld(gb_ref) * cv)

    return pl.pallas_call(
        body, name=name, grid=(n_ex,),
        in_specs=[_pcol(s, 0), _pcol(s, 1), _pcol(s, 2), _vec(K_SHORT)],
        out_specs=pl.BlockSpec((s, MIX_W), lambda e: (e, 0)),
        out_shape=jax.ShapeDtypeStruct((t, MIX_W), _BF),
        compiler_params=_params("parallel"),
    )(p, p, p, wa)


def _mix_a_bwd(p, wa, dmix, n_ex, name):
    t = p.shape[0]
    s = t // n_ex

    def body(gb_ref, gc_ref, ha_ref, w_ref, dy_ref, dp_ref, dw_ref):
        gc = _ld(gc_ref)
        ha = _ld(ha_ref)
        up = _taps3(_pad_rows(gc * ha, 8))
        cv = _conv3_taps(up, w_ref)[:s]
        dy = _ld(dy_ref)
        dcvp = _pad_rows(dy * _ld(gb_ref), 8)
        du = _conv3_t(dcvp, w_ref)[:s]
        dp_ref[:, 0:MIX_W] = _bf(dy * cv)
        dp_ref[:, MIX_W:2 * MIX_W] = _bf(du * ha)
        dp_ref[:, 2 * MIX_W:3 * MIX_W] = _bf(du * gc)

        @pl.when(pl.program_id(0) == 0)
        def _():
            dw_ref[...] = jnp.zeros_like(dw_ref)

        rows = _conv3_dw(dcvp, up)
        for k in range(3):
            dw_ref[k:k + 1, :] += rows[k]

    return pl.pallas_call(
        body, name=name, grid=(n_ex,),
        in_specs=[_pcol(s, 0), _pcol(s, 1), _pcol(s, 2), _vec(K_SHORT), _pcol(s, 0)],
        out_specs=[pl.BlockSpec((s, 3 * MIX_W), lambda e: (e, 0)), _vec(K_SHORT)],
        out_shape=[jax.ShapeDtypeStruct((t, 3 * MIX_W), _BF), jax.ShapeDtypeStruct((K_SHORT, MIX_W), _F32)],
        compiler_params=_params("arbitrary"),
    )(p, p, p, wa, dmix)


CONF_PAD = 32
CONF_ROWS = 64
_CONF_LANES = (slice(0, 128), slice(128, 256))


def _conf_taps(win, ahead):
    n = CONF_ROWS + CONF_PAD
    for b in range(8):
        rot = win if b == 0 else pltpu.roll(win, (n - b) if ahead else b, 0)
        for a in range(4):
            if 8 * a + b < K_CONF:
                yield rot, 8 * a + b, (8 * a) if ahead else (CONF_PAD - 8 * a)


def _ln_fwd(x, g, b):
    mu = jnp.mean(x, axis=-1, keepdims=True)
    xc = x - mu
    rstd = lax.rsqrt(jnp.mean(xc * xc, axis=-1, keepdims=True) + LN_EPS)
    xhat = xc * rstd
    return xhat * g + b, xhat, rstd


def _ln_bwd(dy, xhat, rstd, g):
    dxh = dy * g
    return rstd * (dxh - jnp.mean(dxh, axis=-1, keepdims=True) - xhat * jnp.mean(dxh * xhat, axis=-1, keepdims=True))


def _mix_b_fwd(p, wb, bb, lg, lb, n_ex, name):
    t = p.shape[0]
    s = t // n_ex

    def body(val_ref, gat_ref, w_ref, bb_ref, lg_ref, lb_ref, y_ref, cb_ref, xpad):
        xpad[0:CONF_PAD, :] = jnp.zeros((CONF_PAD, MIX_W), _F32)
        xpad[CONF_PAD:, :] = _ld(val_ref) * _sigmoid(_ld(gat_ref))

        def chunk(c, carry):
            r0 = pl.multiple_of(c * CONF_ROWS, CONF_ROWS)
            for lanes in _CONF_LANES:
                acc = None
                for rot, sh, lo in _conf_taps(xpad[pl.ds(r0, CONF_ROWS + CONF_PAD), lanes], False):
                    term = w_ref[K_CONF - 1 - sh:K_CONF - sh, lanes] * rot[lo:lo + CONF_ROWS]
                    acc = term if acc is None else acc + term
                cb_ref[pl.ds(r0, CONF_ROWS), lanes] = acc + bb_ref[:, lanes]
            return carry

        lax.fori_loop(0, s // CONF_ROWS, chunk, 0)
        yl, _, _ = _ln_fwd(cb_ref[...], lg_ref[...], lb_ref[...])
        y_ref[...] = _bf(yl * _sigmoid(yl))

    return pl.pallas_call(
        body, name=name, grid=(n_ex,),
        in_specs=[_pcol(s, 3), _pcol(s, 4), _vec(K_CONF), _vec(), _vec(), _vec()],
        out_specs=[pl.BlockSpec((s, MIX_W), lambda e: (e, 0)), pl.BlockSpec((s, MIX_W), lambda e: (e, 0))],
        out_shape=[jax.ShapeDtypeStruct((t, MIX_W), _BF), jax.ShapeDtypeStruct((t, MIX_W), _F32)],
        scratch_shapes=[pltpu.VMEM((CONF_PAD + s, MIX_W), _F32)],
        compiler_params=_params("parallel"),
    )(p, p, wb, bb, lg, lb)


def _mix_b_bwd(p, cb, wb, lg, lb, dmix, n_ex, name):
    t = p.shape[0]
    s = t // n_ex

    def body(val_ref, gat_ref, cb_ref, w_ref, lg_ref, lb_ref, dy_ref, dp_ref, dw_ref, dbb_ref, dlg_ref, dlb_ref,
             xpad, dpad, dglu_s, dw_acc):
        @pl.when(pl.program_id(0) == 0)
        def _():
            for r in (dw_ref, dbb_ref, dlg_ref, dlb_ref):
                r[...] = jnp.zeros_like(r)

        yl, xhat, rstd = _ln_fwd(cb_ref[...], lg_ref[...], lb_ref[...])
        sy = _sigmoid(yl)
        dyl = _ld(dy_ref) * sy * (1.0 + yl * (1.0 - sy))
        dlg_ref[...] += jnp.sum(dyl * xhat, axis=0, keepdims=True)
        dlb_ref[...] += jnp.sum(dyl, axis=0, keepdims=True)
        dcb = _ln_bwd(dyl, xhat, rstd, lg_ref[...])
        dbb_ref[...] += jnp.sum(dcb, axis=0, keepdims=True)

        val = _ld(val_ref)
        sg = _sigmoid(_ld(gat_ref))
        xpad[0:CONF_PAD, :] = jnp.zeros((CONF_PAD, MIX_W), _F32)
        xpad[CONF_PAD:, :] = val * sg
        dpad[0:s, :] = dcb
        dpad[s:, :] = jnp.zeros((CONF_PAD, MIX_W), _F32)
        dw_acc[...] = jnp.zeros_like(dw_acc)

        def chunk(c, carry):
            r0 = pl.multiple_of(c * CONF_ROWS, CONF_ROWS)
            for lanes in _CONF_LANES:
                d_win = dpad[pl.ds(r0, CONF_ROWS + CONF_PAD), lanes]
                d_rows = d_win[0:CONF_ROWS]
                acc = None
                for rot, sh, lo in _conf_taps(d_win, True):
                    term = w_ref[K_CONF - 1 - sh:K_CONF - sh, lanes] * rot[lo:lo + CONF_ROWS]
                    acc = term if acc is None else acc + term
                dglu_s[pl.ds(r0, CONF_ROWS), lanes] = acc
                for rot, sh, lo in _conf_taps(xpad[pl.ds(r0, CONF_ROWS + CONF_PAD), lanes], False):
                    prod = d_rows * rot[lo:lo + CONF_ROWS]
                    dw_acc[K_CONF - 1 - sh, :, lanes] += jnp.sum(prod.reshape(CONF_ROWS // 8, 8, 128), axis=0)
            return carry

        lax.fori_loop(0, s // CONF_ROWS, chunk, 0)
        dw_ref[...] += jnp.sum(dw_acc[...], axis=1)
        dglu = dglu_s[...]
        dp_ref[:, 0:MIX_W] = _bf(dglu * sg)
        dp_ref[:, MIX_W:2 * MIX_W] = _bf(dglu * val * sg * (1.0 - sg))

    return pl.pallas_call(
        body, name=name, grid=(n_ex,),
        in_specs=[_pcol(s, 3), _pcol(s, 4), pl.BlockSpec((s, MIX_W), lambda e: (e, 0)), _vec(K_CONF), _vec(), _vec(),
                  _pcol(s, 1)],
        out_specs=[pl.BlockSpec((s, 2 * MIX_W), lambda e: (e, 0)), _vec(K_CONF), _vec(), _vec(), _vec()],
        out_shape=[jax.ShapeDtypeStruct((t, 2 * MIX_W), _BF), jax.ShapeDtypeStruct((K_CONF, MIX_W), _F32),
                   jax.ShapeDtypeStruct((1, MIX_W), _F32), jax.ShapeDtypeStruct((1, MIX_W), _F32),
                   jax.ShapeDtypeStruct((1, MIX_W), _F32)],
        scratch_shapes=[pltpu.VMEM((CONF_PAD + s, MIX_W), _F32), pltpu.VMEM((s + CONF_PAD, MIX_W), _F32),
                        pltpu.VMEM((s, MIX_W), _F32), pltpu.VMEM((K_CONF, 8, MIX_W), _F32)],
        compiler_params=_params("arbitrary"),
    )(p, p, cb, wb, lg, lb, dmix)


_INV_SQRT2 = 0.7071067811865476
_INV_SQRT2PI = 0.3989422804014327


def _gelu(x):
    return 0.5 * x * (1.0 + lax.erf(x * _INV_SQRT2))


def _gelu_grad(x):
    return 0.5 * (1.0 + lax.erf(x * _INV_SQRT2)) + x * _INV_SQRT2PI * jnp.exp(-0.5 * x * x)


def _head_masks(width=MIX_W):
    lane = lax.broadcasted_iota(jnp.int32, (1, width), 1)
    return [(lane >= h * HEAD_DIM) & (lane < (h + 1) * HEAD_DIM) for h in range(N_HEADS)]


def _tril_mask():
    r = lax.broadcasted_iota(jnp.int32, (CHUNK, CHUNK), 0)
    c = lax.broadcasted_iota(jnp.int32, (CHUNK, CHUNK), 1)
    return c <= r


def _sgu_apply(ws_ref, x3, transpose):
    n = x3.shape[0]
    tril = _tril_mask()
    masks = _head_masks()
    xb = _bf(x3)
    out = jnp.zeros(x3.shape, _F32)
    for h in range(N_HEADS):
        w = _bf(jnp.where(tril, ws_ref[h], 0.0))
        wb = jnp.broadcast_to(w[None], (n, CHUNK, CHUNK))
        dims = (((1,), (1,)), ((0,), (0,))) if transpose else (((2,), (1,)), ((0,), (0,)))
        r = lax.dot_general(wb, xb, dims, preferred_element_type=_F32)
        out = out + jnp.where(masks[h][None], r, 0.0)
    return out


def _mix_c_fwd(p, lg, lb, ws, sb_full, n_ex, name):
    t = p.shape[0]
    s = t // n_ex
    nc = s // CHUNK

    def body(pu_ref, pv_ref, lg_ref, lb_ref, ws_ref, sb_ref, y_ref):
        u = _gelu(_ld(pu_ref))
        vl, _, _ = _ln_fwd(_gelu(_ld(pv_ref)), lg_ref[...], lb_ref[...])
        sp = _sgu_apply(ws_ref, vl.reshape(nc, CHUNK, MIX_W), False) + sb_ref[...][None]
        y_ref[...] = _bf(u * sp.reshape(s, MIX_W))

    return pl.pallas_call(
        body, name=name, grid=(n_ex,),
        in_specs=[_pcol(s, 5), _pcol(s, 6), _vec(), _vec(),
                  pl.BlockSpec((N_HEADS, CHUNK, CHUNK), lambda e: (0, 0, 0)), pl.BlockSpec((CHUNK, MIX_W), lambda e: (0, 0))],
        out_specs=pl.BlockSpec((s, MIX_W), lambda e: (e, 0)),
        out_shape=jax.ShapeDtypeStruct((t, MIX_W), _BF),
        compiler_params=_params("parallel"),
    )(p, p, lg, lb, ws, sb_full)


def _mix_c_bwd(p, lg, lb, ws, sb_full, dmix, n_ex, name):
    t = p.shape[0]
    s = t // n_ex
    nc = s // CHUNK

    def body(pu_ref, pv_ref, lg_ref, lb_ref, ws_ref, sb_ref, dy_ref, dp_ref, dlg_ref, dlb_ref, dws_ref, dsb_ref):
        @pl.when(pl.program_id(0) == 0)
        def _():
            for r in (dlg_ref, dlb_ref, dws_ref, dsb_ref):
                r[...] = jnp.zeros_like(r)

        pu = _ld(pu_ref)
        pv = _ld(pv_ref)
        u = _gelu(pu)
        vl, xhat, rstd = _ln_fwd(_gelu(pv), lg_ref[...], lb_ref[...])
        vl3 = vl.reshape(nc, CHUNK, MIX_W)
        sp = _sgu_apply(ws_ref, vl3, False) + sb_ref[...][None]
        dy = _ld(dy_ref)
        dp_ref[:, 0:MIX_W] = _bf(dy * sp.reshape(s, MIX_W) * _gelu_grad(pu))
        dsp3 = (dy * u).reshape(nc, CHUNK, MIX_W)
        dsb_full = jnp.sum(dsp3, axis=0)
        masks = _head_masks()
        tril = _tril_mask()
        dspb = _bf(dsp3)
        vlb = _bf(vl3)
        for h in range(N_HEADS):
            dsb_ref[:, h:h + 1] += jnp.sum(jnp.where(masks[h], dsb_full, 0.0), axis=1, keepdims=True)
            dm = jnp.where(masks[h][None], dspb, jnp.zeros_like(dspb))
            g3 = lax.dot_general(dm, vlb, (((2,), (2,)), ((0,), (0,))), preferred_element_type=_F32)
            dws_ref[h] += jnp.where(tril, jnp.sum(g3, axis=0), 0.0)
        dvl = _sgu_apply(ws_ref, dsp3, True).reshape(s, MIX_W)
        dlg_ref[...] += jnp.sum(dvl * xhat, axis=0, keepdims=True)
        dlb_ref[...] += jnp.sum(dvl, axis=0, keepdims=True)
        dp_ref[:, MIX_W:2 * MIX_W] = _bf(_ln_bwd(dvl, xhat, rstd, lg_ref[...]) * _gelu_grad(pv))

    return pl.pallas_call(
        body, name=name, grid=(n_ex,),
        in_specs=[_pcol(s, 5), _pcol(s, 6), _vec(), _vec(),
                  pl.BlockSpec((N_HEADS, CHUNK, CHUNK), lambda e: (0, 0, 0)), pl.BlockSpec((CHUNK, MIX_W), lambda e: (0, 0)),
                  _pcol(s, 2)],
        out_specs=[pl.BlockSpec((s, 2 * MIX_W), lambda e: (e, 0)), _vec(), _vec(),
                   pl.BlockSpec((N_HEADS, CHUNK, CHUNK), lambda e: (0, 0, 0)), pl.BlockSpec((CHUNK, N_HEADS), lambda e: (0, 0))],
        out_shape=[jax.ShapeDtypeStruct((t, 2 * MIX_W), _BF), jax.ShapeDtypeStruct((1, MIX_W), _F32),
                   jax.ShapeDtypeStruct((1, MIX_W), _F32), jax.ShapeDtypeStruct((N_HEADS, CHUNK, CHUNK), _F32),
                   jax.ShapeDtypeStruct((CHUNK, N_HEADS), _F32)],
        compiler_params=_params("arbitrary"),
    )(p, p, lg, lb, ws, sb_full, dmix)


D_QBLOCK = 512
HEAD_COLS = N_HEADS * KV_BLOCK


def _stack_heads(x3):
    return jnp.stack([_bf(jnp.where(m[None], x3, 0.0)) for m in _head_masks()], axis=1)


def _stack_heads_rows(x):
    return jnp.concatenate([_bf(jnp.where(m, x, 0.0)) for m in _head_masks()], axis=0)


def _cols_to_rows(x):
    return jnp.concatenate([x[:, h * KV_BLOCK:(h + 1) * KV_BLOCK] for h in range(N_HEADS)], axis=0)


def _head_sums(x):
    return [jnp.sum(x[:, h * KV_BLOCK:(h + 1) * KV_BLOCK], axis=1, keepdims=True) for h in range(N_HEADS)]


def _spread(cols):
    tq = cols[0].shape[0]
    return jnp.concatenate([jnp.broadcast_to(c, (tq, KV_BLOCK)) for c in cols], axis=1)


def _pair_dot(x, m2):
    half = 2 * KV_BLOCK
    xb = _bf(x)
    return jnp.concatenate([_dot(xb[:, :half], m2), _dot(xb[:, half:], m2)], axis=1)


def _tri2(lower):
    n = 2 * KV_BLOCK
    r = lax.broadcasted_iota(jnp.int32, (n, n), 0)
    c = lax.broadcasted_iota(jnp.int32, (n, n), 1)
    same = (r >= KV_BLOCK) == (c >= KV_BLOCK)
    return _bf(jnp.where(same & (r > c if lower else r < c), 1.0, 0.0))


def _sb_scores(qs, kc, j, t_idx, on_diagonal):
    z = _dot_nt(qs, kc)
    lb = jnp.minimum(z, 0.0) - jnp.log(1.0 + jnp.exp(-jnp.abs(z)))
    if not on_diagonal:
        return (lambda x: x), lb, lb - z
    lane = lax.broadcasted_iota(jnp.int32, (1, HEAD_COLS), 1)
    valid = (j * KV_BLOCK + (lane & (KV_BLOCK - 1))) < t_idx
    keep = lambda x: jnp.where(valid, x, 0.0)
    return keep, lb, keep(lb - z)


RUN_LANES = 128


def _run_lane(j, h):
    return lax.broadcasted_iota(jnp.int32, (1, RUN_LANES), 1) == j * N_HEADS + h


def _d_qblock(s):
    return D_QBLOCK if s % D_QBLOCK == 0 else KV_BLOCK


def _mix_d_fwd(p, n_ex, name):
    t = p.shape[0]
    s = t // n_ex
    tq = _d_qblock(s)
    nq = s // tq
    r = tq // KV_BLOCK
    nb = s // KV_BLOCK
    assert nb * N_HEADS <= RUN_LANES

    def body(q_ref, k_ref, v_ref, y_ref, runs_ref, kc, vc):
        i = pl.program_id(1)

        @pl.when(i == 0)
        def _():
            kc[...] = _stack_heads(k_ref[...].reshape(nb, KV_BLOCK, MIX_W))
            vc[...] = _stack_heads(v_ref[...].reshape(nb, KV_BLOCK, MIX_W))

        qs = _bf(_ld(q_ref) * (HEAD_DIM ** -0.5))
        t_idx = i * tq + lax.broadcasted_iota(jnp.int32, (tq, 1), 0)
        after_m = _tri2(True)
        nkb = (i + 1) * r

        runs_ref[...] = jnp.zeros_like(runs_ref)

        def one_block(j, runs, acc, on_diagonal):
            keep, lb, c = _sb_scores(qs, kc[j].reshape(HEAD_COLS, MIX_W), j, t_idx, on_diagonal)
            a = keep(jnp.exp(lb + _pair_dot(c, after_m) + _spread(runs)))
            acc = acc + _dot(_bf(a), vc[j].reshape(HEAD_COLS, MIX_W))
            kept = runs_ref[...]
            for h in range(N_HEADS):
                kept = jnp.where(_run_lane(j, h), runs[h], kept)
            runs_ref[...] = kept
            return tuple(ru + cs for ru, cs in zip(runs, _head_sums(c))), acc

        def trip(last, carry, on_diagonal):
            runs, acc = carry
            for sub in range(r):
                runs, acc = one_block(last - sub, runs, acc, on_diagonal)
            return runs, acc

        zero = jnp.zeros((tq, 1), _F32)
        carry = trip(nkb - 1, ((zero,) * N_HEADS, jnp.zeros((tq, MIX_W), _F32)), True)
        below = lambda m: nkb - 1 - (m + 1) * r
        carry = lax.fori_loop(0, i // 2, lambda m, carry: trip(below(2 * m + 1), trip(below(2 * m), carry, False), False), carry)
        _, acc = lax.fori_loop(0, i % 2, lambda m, carry: trip(below(i - 1), carry, False), carry)
        y_ref[...] = _bf(acc)

    return pl.pallas_call(
        body, name=name, grid=(n_ex, nq),
        in_specs=[pl.BlockSpec((tq, MIX_W), lambda e, i: (e * nq + i, 7)), pl.BlockSpec((s, MIX_W), lambda e, i: (e, 8)),
                  pl.BlockSpec((s, MIX_W), lambda e, i: (e, 9))],
        out_specs=[pl.BlockSpec((tq, MIX_W), lambda e, i: (e * nq + i, 0)),
                   pl.BlockSpec((tq, RUN_LANES), lambda e, i: (e * nq + i, 0))],
        out_shape=[jax.ShapeDtypeStruct((t, MIX_W), _BF), jax.ShapeDtypeStruct((t, RUN_LANES), _F32)],
        scratch_shapes=[pltpu.VMEM((nb, N_HEADS, KV_BLOCK, MIX_W), _BF), pltpu.VMEM((nb, N_HEADS, KV_BLOCK, MIX_W), _BF)],
        compiler_params=_params("parallel", "arbitrary"),
    )(p, p, p)


def _mix_d_bwd(p, kept_runs, dmix, n_ex, name):
    t = p.shape[0]
    s = t // n_ex
    tq = _d_qblock(s)
    nq = s // tq
    r = tq // KV_BLOCK
    nb = s // KV_BLOCK
    scale = HEAD_DIM ** -0.5

    def body(q_ref, k_ref, v_ref, runs_ref, do_ref, dq_ref, dk_ref, dv_ref, kc, vc):
        i = pl.program_id(1)

        @pl.when(i == 0)
        def _():
            kc[...] = _stack_heads(k_ref[...].reshape(nb, KV_BLOCK, MIX_W))
            vc[...] = _stack_heads(v_ref[...].reshape(nb, KV_BLOCK, MIX_W))
            dk_ref[...] = jnp.zeros_like(dk_ref)
            dv_ref[...] = jnp.zeros_like(dv_ref)

        q_scaled = _ld(q_ref) * scale
        qs = _bf(q_scaled)
        do = do_ref[...]
        dob = _bf(do)
        q_rows = _stack_heads_rows(q_scaled)
        do_rows = _stack_heads_rows(do)
        kept = runs_ref[...]
        t_idx = i * tq + lax.broadcasted_iota(jnp.int32, (tq, 1), 0)
        after_m = _tri2(True)
        before_m = _tri2(False)
        nkb = (i + 1) * r
        zero = jnp.zeros((tq, 1), _F32)

        def trip(first, carry, on_diagonal):
            for sub in range(r):
                carry = one_block(first + sub, carry, on_diagonal)
            return carry

        def one_block(j, carry, on_diagonal):
            pres, dq = carry
            rows = pl.ds(pl.multiple_of(j * KV_BLOCK, KV_BLOCK), KV_BLOCK)
            kj = kc[j].reshape(HEAD_COLS, MIX_W)
            keep, lb, c = _sb_scores(qs, kj, j, t_idx, on_diagonal)
            runs = [jnp.sum(jnp.where(_run_lane(j, h), kept, 0.0), axis=1, keepdims=True) for h in range(N_HEADS)]
            a = keep(jnp.exp(lb + _pair_dot(c, after_m) + _spread(runs)))
            g = a * _dot_nt(dob, vc[j].reshape(HEAD_COLS, MIX_W))
            before = _pair_dot(g, before_m) + _spread(pres)
            sig = jnp.exp(lb)
            dz = _bf(keep(g * (1.0 - sig) - sig * before))
            dk_ref[rows, :] += _dot_tn(_cols_to_rows(dz), q_rows)
            dv_ref[rows, :] += _dot_tn(_cols_to_rows(_bf(a)), do_rows)
            return tuple(pr + gs for pr, gs in zip(pres, _head_sums(g))), dq + _dot(dz, kj)

        init = ((zero,) * N_HEADS, jnp.zeros((tq, MIX_W), _F32))
        carry = lax.fori_loop(0, i // 2, lambda m, carry: trip((2 * m + 1) * r, trip(2 * m * r, carry, False), False), init)
        carry = lax.fori_loop(0, i % 2, lambda m, carry: trip((i - 1) * r, carry, False), carry)
        _, dq = trip(i * r, carry, True)
        dq_ref[...] = _bf(dq * scale)

    return pl.pallas_call(
        body, name=name, grid=(n_ex, nq),
        in_specs=[pl.BlockSpec((tq, MIX_W), lambda e, i: (e * nq + i, 7)), pl.BlockSpec((s, MIX_W), lambda e, i: (e, 8)),
                  pl.BlockSpec((s, MIX_W), lambda e, i: (e, 9)), pl.BlockSpec((tq, RUN_LANES), lambda e, i: (e * nq + i, 0)),
                  pl.BlockSpec((tq, MIX_W), lambda e, i: (e * nq + i, 3))],
        out_specs=[pl.BlockSpec((tq, MIX_W), lambda e, i: (e * nq + i, 0)), pl.BlockSpec((s, MIX_W), lambda e, i: (e, 0)),
                   pl.BlockSpec((s, MIX_W), lambda e, i: (e, 0))],
        out_shape=[jax.ShapeDtypeStruct((t, MIX_W), _BF), jax.ShapeDtypeStruct((t, MIX_W), _F32),
                   jax.ShapeDtypeStruct((t, MIX_W), _F32)],
        scratch_shapes=[pltpu.VMEM((nb, N_HEADS, KV_BLOCK, MIX_W), _BF), pltpu.VMEM((nb, N_HEADS, KV_BLOCK, MIX_W), _BF)],
        compiler_params=_params("parallel", "arbitrary"),
    )(p, p, p, kept_runs, dmix)


def _fwd_mix(x, w, l, n_ex):
    p, h1 = _norm_mm(x, w["norm1_g"][l], w["w_in_t"][l], "in_proj")
    y_a = _mix_a_fwd(p, w["conv_a_w"][l], n_ex, "mix_a_fwd")
    y_b, cb = _mix_b_fwd(p, w["conv_b_w"][l], w["conv_b_b"][l], w["ln_b_g"][l], w["ln_b_b"][l], n_ex, "mix_b_fwd")
    y_c = _mix_c_fwd(p, w["ln_c_g"][l], w["ln_c_b"][l], w["sgu_w"][l], w["sgu_b_full"][l], n_ex, "mix_c_fwd")
    y_d, runs_d = _mix_d_fwd(p, n_ex, "mix_d_fwd")
    return dict(x=x, h1=h1, p=p, cb=cb, runs_d=runs_d, mix=(y_a, y_b, y_c, y_d))


def _fwd_ffn(st, w, l, n_ex):
    x1, up_pre, h2 = _res_norm_mm(st["mix"], w["w_out"][l], st["x"], w["norm2_g"][l], w["w_up_t"][l], "out_up_proj")
    act, conv_g, conv_v = _ffn_mid_fwd(up_pre, w["conv_f_w"][l], n_ex, "ffn_mid_fwd")
    st.update(x1=x1, h2=h2, up_pre=up_pre, act=act, conv_g=conv_g, conv_v=conv_v)


def _down_proj(st, w, l):
    return _mm_res((st["act"],), w["w_down"][l], st["x1"], "down_proj")


def _down_proj_loss(st, w, l, target):
    return _res_final_loss((st["act"],), w["w_down"][l], st["x1"], w["final_g"], target, "down_proj_loss")


def _bwd_ffn(st, w, l, dx, dxb, n_ex):
    g = {}
    dact = _mm_nt(dxb, w["w_down"][l], "down_proj_dx")
    g["w_down"] = _mm_tn(st["act"], dxb, "down_proj_dw", _BF)
    dup_g, dup_v, dwf_g, dwf_v = _ffn_mid_bwd(
        st["up_pre"], st["conv_g"], st["conv_v"], w["conv_f_w"][l], dact, n_ex, "ffn_mid_bwd")
    g["conv_f_w"] = jnp.concatenate([dwf_g, dwf_v], axis=1)
    dx, dxb, g["norm2_g"] = _mm_normbwd((dup_g, dup_v), w["w_up_t"][l], st["x1"], w["norm2_g"][l], dx, "up_proj_dx")
    g["w_up_t"] = _mm_tn_halves(dup_g, dup_v, st["h2"], "up_proj_dw")
    return dx, dxb, g


def _bwd_out_proj(st, w, l, dxb):
    return _mm_nt(dxb, w["w_out"][l], "out_proj_dx"), _mm_tn_parts(st["mix"], dxb, "out_proj_dw")


def _bwd_mixers(st, w, l, dx, dmix, n_ex):
    g = {}
    p = st["p"]
    dp_a, g["conv_a_w"] = _mix_a_bwd(p, w["conv_a_w"][l], dmix, n_ex, "mix_a_bwd")
    dp_b, g["conv_b_w"], g["conv_b_b"], g["ln_b_g"], g["ln_b_b"] = _mix_b_bwd(
        p, st["cb"], w["conv_b_w"][l], w["ln_b_g"][l], w["ln_b_b"][l], dmix, n_ex, "mix_b_bwd")
    dp_c, g["ln_c_g"], g["ln_c_b"], g["sgu_w"], g["sgu_b_t"] = _mix_c_bwd(
        p, w["ln_c_g"][l], w["ln_c_b"][l], w["sgu_w"][l], w["sgu_b_full"][l], dmix, n_ex, "mix_c_bwd")
    dq, dk, dv = _mix_d_bwd(p, st["runs_d"], dmix, n_ex, "mix_d_bwd")
    dp = (dp_a, dp_b, dp_c, dq, dk, dv)
    dx, dxb, g["norm1_g"] = _mm_normbwd(dp, w["w_in_t"][l], st["x"], w["norm1_g"][l], dx, "in_proj_dx")
    return dx, dxb, g, dp


def _bwd_mix(st, w, l, dx, dxb, n_ex):
    dmix, dw_out = _bwd_out_proj(st, w, l, dxb)
    dx, dxb, g, dp = _bwd_mixers(st, w, l, dx, dmix, n_ex)
    g["w_out"] = dw_out
    g["w_in_t"] = _mm_tn_parts(dp, st["h1"], "in_proj_dw")
    return dx, dxb, g


_MESH = pl.DeviceIdType.MESH
_ANY = pl.BlockSpec(memory_space=pl.ANY)


def _position():
    return lax.axis_index("x"), lax.axis_index("y"), lax.axis_index("c")


def _flat(px, py, pc):
    return 4 * px + 2 * py + pc


def _all_gather(shard, name, after):
    r, c_ = shard.shape

    def body(x_ref, after_ref, out_ref, send_sems, recv_sems, local_sem):
        x, y, c = _position()
        me, sibling = (x, y, c), (x, y, 1 - c)
        chips = [(1 - x, y), (x, 1 - y), (1 - x, 1 - y)]

        def copy(k, block, to, src=None):
            slab = out_ref.at[_flat(*block)]
            return pltpu.make_async_remote_copy(
                src_ref=slab if src is None else src, dst_ref=slab, send_sem=send_sems.at[k], recv_sem=recv_sems.at[k],
                device_id=to, device_id_type=_MESH)

        mine = pltpu.make_async_copy(x_ref, out_ref.at[_flat(*me)], local_sem)
        mine.start()
        first = [copy(0, me, sibling, src=x_ref)]
        first += [copy(1 + j, me, (*chip, c), src=x_ref) for j, chip in enumerate(chips)]
        for cp in first:
            cp.start()
        passed = [copy(4 + j, (*chip, c), sibling) for j, chip in enumerate(chips)]
        for j, chip in enumerate(chips):
            copy(1 + j, (*chip, c), me).wait_recv()
            passed[j].start()
        copy(0, sibling, me).wait_recv()
        for j, chip in enumerate(chips):
            copy(4 + j, (*chip, 1 - c), me).wait_recv()
        for cp in first + passed:
            cp.wait_send()
        mine.wait()

    return pl.pallas_call(
        body, name=name, out_shape=jax.ShapeDtypeStruct((N_DEV, r, c_), shard.dtype),
        in_specs=[_ANY, _ANY], out_specs=_ANY,
        scratch_shapes=[pltpu.SemaphoreType.DMA((7,)), pltpu.SemaphoreType.DMA((7,)), pltpu.SemaphoreType.DMA],
    )(shard, after)


_HBM = pl.BlockSpec(memory_space=pltpu.HBM)
_SEM = pl.BlockSpec(memory_space=pltpu.SEMAPHORE)
_DATAFLOW = pltpu.SideEffectType.DATAFLOW_SIDE_EFFECTING


def _peers(x, y, c):
    return [((1 - x) if (k + 1) & 4 else x, (1 - y) if (k + 1) & 2 else y, (1 - c) if (k + 1) & 1 else c)
            for k in range(N_DEV - 1)]


def _direct_copies(src_refs, land_refs, send_sems, recv_sems, to_all):
    x, y, c = _position()
    my = _flat(x, y, c)
    out, back = [], []
    for m, (src_ref, land_ref) in enumerate(zip(src_refs, land_refs)):
        for k, peer in enumerate(_peers(x, y, c)):
            src = src_ref if to_all else src_ref.at[_flat(*peer)]
            n = m * (N_DEV - 1) + k
            sems = dict(send_sem=send_sems.at[n], recv_sem=recv_sems.at[n], device_id=peer, device_id_type=_MESH)
            out.append(pltpu.make_async_remote_copy(src_ref=src, dst_ref=land_ref.at[my], **sems))
            back.append(pltpu.make_async_remote_copy(src_ref=src, dst_ref=land_ref.at[_flat(*peer)], **sems))
    return out, back


def _exchange_start(srcs, to_all, after, name):
    n = len(srcs)
    n_sems = n * (N_DEV - 1)
    land_shapes = [(N_DEV,) + tuple(a.shape[-2:]) for a in srcs]

    def body(*refs):
        src_refs, land_refs = refs[:n], refs[n:2 * n]
        send_sems, recv_sems = refs[2 * n + 1], refs[2 * n + 2]
        token = refs[-1]
        for cp in _direct_copies(src_refs, land_refs, send_sems, recv_sems, to_all)[0]:
            cp.start()
        token[...] = jnp.zeros_like(token)

    lands = [pltpu.with_memory_space_constraint(lax.empty(shp, a.dtype), pltpu.HBM) for shp, a in zip(land_shapes, srcs)]
    outs = pl.pallas_call(
        body, name=name,
        out_shape=(pltpu.SemaphoreType.DMA((n_sems,)), pltpu.SemaphoreType.DMA((n_sems,)),
                   *[pltpu.HBM(a.shape, a.dtype) for a in srcs], *[pltpu.HBM(shp, a.dtype) for shp, a in zip(land_shapes, srcs)],
                   jax.ShapeDtypeStruct((8, 128), _F32)),
        in_specs=(_HBM,) * (2 * n) + (_ANY,),
        out_specs=(_SEM, _SEM) + (_HBM,) * (2 * n) + (pl.BlockSpec(memory_space=pltpu.VMEM),),
        input_output_aliases={i: 2 + i for i in range(2 * n)},
        compiler_params=pltpu.CompilerParams(has_side_effects=_DATAFLOW),
    )(*[pltpu.with_memory_space_constraint(a, pltpu.HBM) for a in srcs], *lands, after)
    return (outs[0], outs[1], outs[2:2 + n], outs[2 + n:2 + 2 * n], to_all), outs[-1]


def _exchange_wait(handle, after, name):
    send_sems, recv_sems, srcs, lands, to_all = handle
    n = len(srcs)

    def body(*refs):
        out, back = _direct_copies(refs[:n], refs[n:2 * n], refs[2 * n], refs[2 * n + 1], to_all)
        for cp in out:
            cp.wait_send()
        for cp in back:
            cp.wait_recv()

    outs = pl.pallas_call(
        body, name=name,
        out_shape=tuple(pltpu.HBM(a.shape, a.dtype) for a in (*srcs, *lands)),
        in_specs=(_HBM,) * (2 * n) + (_SEM, _SEM, _ANY), out_specs=(_HBM,) * (2 * n),
        input_output_aliases={i: i for i in range(2 * n)},
        compiler_params=pltpu.CompilerParams(has_side_effects=_DATAFLOW),
    )(*srcs, *lands, send_sems, recv_sems, after)
    return outs[:n], outs[n:]


def _with_own(landed, own):
    my = _flat(*_position())
    return lax.dynamic_update_slice(landed, own[None], (my, 0, 0))


def _sum_slabs(slabs, own, name):
    n, r, c_ = slabs.shape
    tr = _pick_tile(r, 16, max(16, (12 << 20) // (n * c_ * slabs.dtype.itemsize)))

    def body(x_ref, own_ref, o_ref):
        my = _flat(*_position())
        acc = None
        for k in range(n):
            term = jnp.where(my == k, own_ref[...], x_ref[k]).astype(_F32)
            acc = term if acc is None else acc + term
        o_ref[...] = acc

    return pl.pallas_call(
        body, name=name, grid=(r // tr,),
        in_specs=[pl.BlockSpec((n, tr, c_), lambda i: (0, i, 0)), pl.BlockSpec((tr, c_), lambda i: (i, 0))],
        out_specs=pl.BlockSpec((tr, c_), lambda i: (i, 0)),
        out_shape=jax.ShapeDtypeStruct((r, c_), _F32),
        compiler_params=_params("parallel"),
    )(slabs, own)


def _adamw(w, g, m, v, name):
    r, c_ = w.shape
    tr = _pick_tile(r, 8, 512)

    def body(w_ref, g_ref, m_ref, v_ref, d_ref, nm_ref, nv_ref):
        _adamw_refs(w_ref, g_ref, m_ref, v_ref, d_ref, nm_ref, nv_ref)

    spec = pl.BlockSpec((tr, c_), lambda i: (i, 0))
    shape = jax.ShapeDtypeStruct((r, c_), _F32)
    return pl.pallas_call(
        body, name=name, grid=(r // tr,), in_specs=[spec] * 4, out_specs=[spec] * 3, out_shape=[shape] * 3,
        compiler_params=_params("parallel"),
    )(w, g, m, v)


def _adamw_refs(w_ref, g_ref, m_ref, v_ref, d_ref, nm_ref, nv_ref):
    gv = g_ref[...]
    nm = ADAM_B1 * m_ref[...] + (1.0 - ADAM_B1) * gv
    nv = ADAM_B2 * v_ref[...] + (1.0 - ADAM_B2) * (gv * gv)
    m_hat = nm / (1.0 - ADAM_B1 ** ADAM_STEP)
    v_hat = nv / (1.0 - ADAM_B2 ** ADAM_STEP)
    d_ref[...] = -ADAM_LR * (m_hat / (jnp.sqrt(v_hat) + ADAM_EPS) + ADAM_WD * w_ref[...])
    nm_ref[...] = nm
    nv_ref[...] = nv


def _adamw_small(params, name):
    n = len(params)

    def body(*refs):
        for i in range(n):
            _adamw_refs(*refs[4 * i:4 * i + 4], *refs[4 * n + 3 * i:4 * n + 3 * i + 3])

    outs = pl.pallas_call(
        body, name=name,
        out_shape=[jax.ShapeDtypeStruct(p[0].shape, _F32) for p in params for _ in range(3)],
        compiler_params=pltpu.CompilerParams(vmem_limit_bytes=VMEM_LIMIT),
    )(*[a for p in params for a in p])
    return [tuple(outs[3 * i:3 * i + 3]) for i in range(n)]


_SMALL = ("norm1_g", "conv_a_w", "conv_b_w", "conv_b_b", "ln_b_g", "ln_b_b", "ln_c_g", "ln_c_b", "sgu_w", "sgu_b",
          "norm2_g", "conv_f_w", "final_g")
_CONV_SHARDED = ("conv_a_w", "conv_b_w", "conv_f_w")
_NAMES = ("norm1_g", "w_in", "conv_a_w", "conv_b_w", "conv_b_b", "ln_b_g", "ln_b_b", "ln_c_g", "ln_c_b", "sgu_w", "sgu_b",
          "w_out", "norm2_g", "w_up", "conv_f_w", "w_down", "final_g")


def _pack_rows(parts, lanes=128, row_multiple=8):
    flat = jnp.concatenate([a.reshape(-1) for a in parts])
    rows = -(-flat.shape[0] // lanes)
    rows = -(-rows // row_multiple) * row_multiple
    return jnp.pad(flat, (0, rows * lanes - flat.shape[0])).reshape(rows, lanes)


def _unpack_rows(packed, shapes):
    flat = packed.reshape(-1)
    out, off = [], 0
    for shp in shapes:
        size = 1
        for s in shp:
            size *= s
        out.append(flat[off:off + size].reshape(shp))
        off += size
    return out


def _gather_conv_weights(conv_a_w, conv_b_w, conv_f_w, after):
    shards = (conv_a_w, conv_b_w, conv_f_w)
    flat = _all_gather(_pack_rows(shards), "gather_conv_weights", after).reshape(N_DEV, -1)
    full, off = [], 0
    for s in shards:
        layers, taps, width = s.shape
        per_dev = flat[:, off:off + s.size].reshape(N_DEV, layers, taps, width)
        full.append(jnp.moveaxis(per_dev, 0, 2).reshape(layers, taps, N_DEV * width))
        off += s.size
    return full


def kernel(x, norm1_g, w_in, conv_a_w, conv_b_w, conv_b_b, ln_b_g, ln_b_b, ln_c_g, ln_c_b, sgu_w, sgu_b, w_out, norm2_g, w_up, conv_f_w, w_down, final_g, loss_target, m_norm1_g, m_w_in, m_conv_a_w, m_conv_b_w, m_conv_b_b, m_ln_b_g, m_ln_b_b, m_ln_c_g, m_ln_c_b, m_sgu_w, m_sgu_b, m_w_out, m_norm2_g, m_w_up, m_conv_f_w, m_w_down, m_final_g, v_norm1_g, v_w_in, v_conv_a_w, v_conv_b_w, v_conv_b_b, v_ln_b_g, v_ln_b_b, v_ln_c_g, v_ln_c_b, v_sgu_w, v_sgu_b, v_w_out, v_norm2_g, v_w_up, v_conv_f_w, v_w_down, v_final_g):
    weights = dict(norm1_g=norm1_g, w_in=w_in, conv_a_w=conv_a_w, conv_b_w=conv_b_w, conv_b_b=conv_b_b, ln_b_g=ln_b_g,
                   ln_b_b=ln_b_b, ln_c_g=ln_c_g, ln_c_b=ln_c_b, sgu_w=sgu_w, sgu_b=sgu_b, w_out=w_out, norm2_g=norm2_g,
                   w_up=w_up, conv_f_w=conv_f_w, w_down=w_down, final_g=final_g)
    mom1 = dict(norm1_g=m_norm1_g, w_in=m_w_in, conv_a_w=m_conv_a_w, conv_b_w=m_conv_b_w, conv_b_b=m_conv_b_b,
                ln_b_g=m_ln_b_g, ln_b_b=m_ln_b_b, ln_c_g=m_ln_c_g, ln_c_b=m_ln_c_b, sgu_w=m_sgu_w, sgu_b=m_sgu_b,
                w_out=m_w_out, norm2_g=m_norm2_g, w_up=m_w_up, conv_f_w=m_conv_f_w, w_down=m_w_down, final_g=m_final_g)
    mom2 = dict(norm1_g=v_norm1_g, w_in=v_w_in, conv_a_w=v_conv_a_w, conv_b_w=v_conv_b_w, conv_b_b=v_conv_b_b,
                ln_b_g=v_ln_b_g, ln_b_b=v_ln_b_b, ln_c_g=v_ln_c_g, ln_c_b=v_ln_c_b, sgu_w=v_sgu_w, sgu_b=v_sgu_b,
                w_out=v_w_out, norm2_g=v_norm2_g, w_up=v_w_up, conv_f_w=v_conv_f_w, w_down=v_w_down, final_g=v_final_g)
    n_ex, seq, d = x.shape
    depth = w_in.shape[0]
    assert depth == 2
    my = _flat(*_position())
    row = lambda a, l: a[l][None]
    tied = lambda a, token: a + token[0:1, 0:1]

    slab = {"w_in": [_bf(jnp.swapaxes(w_in, 1, 2)[l]) for l in range(depth)], "w_out": [_bf(w_out[l]) for l in range(depth)],
            "w_up": [_bf(jnp.swapaxes(w_up, 1, 2)[l]) for l in range(depth)], "w_down": [_bf(w_down[l]) for l in range(depth)]}
    rows = {name: parts[0].shape[0] for name, parts in slab.items()}
    key_of = {"w_in": "w_in_t", "w_out": "w_out", "w_up": "w_up_t", "w_down": "w_down"}
    rest_layer0 = [("w_out", 0), ("w_up", 0), ("w_down", 0)]
    all_layer1 = [("w_in", 1), ("w_out", 1), ("w_up", 1), ("w_down", 1)]

    w_in0 = _all_gather(slab["w_in"][0], "gather_w_in0", norm1_g)
    conv_a_full, conv_b_full, conv_f_full = _gather_conv_weights(conv_a_w, conv_b_w, conv_f_w, w_in0)
    gather0, token = _exchange_start([slab[n][l] for n, l in rest_layer0], True, conv_f_full, "gather_layer0_start")
    w = {
        "norm1_g": [row(norm1_g, l) for l in range(depth)], "w_in_t": [None] * depth,
        "conv_a_w": [conv_a_full[l] for l in range(depth)], "conv_b_w": [conv_b_full[l] for l in range(depth)],
        "conv_b_b": [row(conv_b_b, l) for l in range(depth)], "ln_b_g": [row(ln_b_g, l) for l in range(depth)],
        "ln_b_b": [row(ln_b_b, l) for l in range(depth)], "ln_c_g": [row(ln_c_g, l) for l in range(depth)],
        "ln_c_b": [row(ln_c_b, l) for l in range(depth)], "sgu_w": [sgu_w[l] for l in range(depth)],
        "sgu_b_full": [jnp.repeat(sgu_b[l].T, HEAD_DIM, axis=1) for l in range(depth)],
        "w_out": [None] * depth, "norm2_g": [row(norm2_g, l) for l in range(depth)], "w_up_t": [None] * depth,
        "conv_f_w": [conv_f_full[l] for l in range(depth)], "w_down": [None] * depth, "final_g": final_g[None],
    }
    w["w_in_t"][0] = w_in0.reshape(N_DEV * rows["w_in"], d)
    w["norm1_g"][0] = tied(row(norm1_g, 0), token)

    def land_weights(handle, after, which, name):
        owns, landed = _exchange_wait(handle, after, name)
        for (n, l), own, got in zip(which, owns, landed):
            w[key_of[n]][l] = _with_own(got, own).reshape(N_DEV * rows[n], d)
        return landed[0]

    st0 = _fwd_mix(x.reshape(n_ex * seq, d), w, 0, n_ex)
    landed0 = land_weights(gather0, st0["mix"][3], rest_layer0, "gather_layer0_wait")
    gather1, token = _exchange_start([slab[n][l] for n, l in all_layer1], True, landed0, "gather_layer1_start")
    w["norm2_g"][0] = tied(row(norm2_g, 0), token)
    _fwd_ffn(st0, w, 0, n_ex)
    x_mid = _down_proj(st0, w, 0)
    land_weights(gather1, x_mid, all_layer1, "gather_layer1_wait")
    st1 = _fwd_mix(x_mid, w, 1, n_ex)
    _fwd_ffn(st1, w, 1, n_ex)
    dx, dxb, d_final_g, loss_part = _down_proj_loss(st1, w, 1, loss_target.reshape(n_ex * seq, d))

    def send_grads(g, which, after, name):
        return _exchange_start([g[key_of[n]].reshape(N_DEV, rows[n], d) for n, _ in which], False, after, name)

    dx, dxb, g_ffn1 = _bwd_ffn(st1, w, 1, dx, dxb, n_ex)
    dx, dxb, g_mix1 = _bwd_mix(st1, w, 1, dx, dxb, n_ex)
    grads1, token = send_grads({**g_ffn1, **g_mix1}, all_layer1, dx, "exchange_layer1_start")
    w["norm2_g"][0] = tied(row(norm2_g, 0), token)
    dx, dxb, g_ffn0 = _bwd_ffn(st0, w, 0, dx, dxb, n_ex)
    g_ffn0["w_out"] = _mm_tn_parts(st0["mix"], dxb, "out_proj_dw")
    ffn_layer0 = [("w_out", 0), ("w_up", 0), ("w_down", 0)]
    grads0a, token = send_grads(g_ffn0, ffn_layer0, dxb, "exchange_ffn0_start")
    dmix = _mm_nt(dxb, w["w_out"][0], "out_proj_dx", after=token)
    dx, dxb, g_mix0, dp0 = _bwd_mixers(st0, w, 0, dx, dmix, n_ex)
    grad_x = dx.reshape(n_ex, seq, d)
    g = {k: [{**g_ffn0, **g_mix0}[k], {**g_ffn1, **g_mix1}[k]] for k in g_mix0.keys() | g_ffn0.keys()}
    g["final_g"] = d_final_g

    small_local = {
        "norm1_g": jnp.stack([a[0] for a in g["norm1_g"]]), "conv_a_w": jnp.stack(g["conv_a_w"]),
        "conv_b_w": jnp.stack(g["conv_b_w"]), "conv_b_b": jnp.stack([a[0] for a in g["conv_b_b"]]),
        "ln_b_g": jnp.stack([a[0] for a in g["ln_b_g"]]), "ln_b_b": jnp.stack([a[0] for a in g["ln_b_b"]]),
        "ln_c_g": jnp.stack([a[0] for a in g["ln_c_g"]]), "ln_c_b": jnp.stack([a[0] for a in g["ln_c_b"]]),
        "sgu_w": jnp.stack(g["sgu_w"]), "sgu_b": jnp.stack([a.T for a in g["sgu_b_t"]]),
        "norm2_g": jnp.stack([a[0] for a in g["norm2_g"]]), "conv_f_w": jnp.stack(g["conv_f_w"]),
        "final_g": g["final_g"][0],
    }
    small_parts = [small_local[k] for k in _SMALL] + [loss_part.reshape(1)]
    small, token = _exchange_start([_pack_rows(small_parts)], True, dx, "gather_small_start")
    g_mix0["w_in_t"] = _mm_tn_parts(dp0, st0["h1"], "in_proj_dw", after=token)
    mix_layer0 = [("w_in", 0)]
    grads0b, token = send_grads(g_mix0, mix_layer0, dx, "exchange_mix0_start")

    reduced = {}

    def land_grads(handle, after, which, name):
        sent, landed = _exchange_wait(handle, after, name + "_wait")
        for (n, l), src, got in zip(which, sent, landed):
            own = lax.dynamic_index_in_dim(src, my, 0, keepdims=False)
            reduced[(n, l)] = _sum_slabs(got, own, name + "_sum_" + n)
        return reduced[which[-1]]

    def stacked_grad(name):
        stacked = jnp.stack([reduced[(name, l)] for l in range(depth)])
        return jnp.swapaxes(stacked, 1, 2) if name in ("w_in", "w_up") else stacked

    done = land_grads(grads1, token, all_layer1, "exchange_layer1")
    land_grads(grads0a, done, ffn_layer0, "exchange_ffn0")
    grads = {name: stacked_grad(name) for name in ("w_out", "w_up", "w_down")}

    delta, new_m, new_v = {}, {}, {}

    def as_2d(name):
        shp = weights[name].shape
        two_d = (-1, shp[-1]) if len(shp) > 1 else (1, shp[0])
        return tuple(a.reshape(two_d) for a in (weights[name], grads[name], mom1[name], mom2[name]))

    def keep(name, outs):
        delta[name], new_m[name], new_v[name] = (o.reshape(weights[name].shape) for o in outs)

    for name in ("w_up", "w_down", "w_out"):
        keep(name, _adamw(*as_2d(name), "adamw_" + name))

    (own,), (landed,) = _exchange_wait(small, new_v["w_out"], "gather_small_wait")
    small_sum = _sum_slabs(landed, own, "sum_small_grads")
    *small_totals, loss_total = _unpack_rows(small_sum, [a.shape for a in small_parts])
    loss = loss_total[0]
    for name, total in zip(_SMALL, small_totals):
        if name in _CONV_SHARDED:
            width = weights[name].shape[-1]
            total = lax.dynamic_slice_in_dim(total, my * width, width, axis=-1)
        grads[name] = total
    at_least_2d = lambda a: a[None] if a.ndim == 1 else a
    small_params = [tuple(at_least_2d(a) for a in (weights[n], grads[n], mom1[n], mom2[n])) for n in _SMALL]
    for name, outs in zip(_SMALL, _adamw_small(small_params, "adamw_small")):
        keep(name, outs)

    land_grads(grads0b, new_v["final_g"], mix_layer0, "exchange_mix0")
    grads["w_in"] = stacked_grad("w_in")
    keep("w_in", _adamw(*as_2d("w_in"), "adamw_w_in"))

    return (loss, grad_x, *[grads[n] for n in _NAMES], *[delta[n] for n in _NAMES], *[new_m[n] for n in _NAMES],
            *[new_v[n] for n in _NAMES])
```

```python
import jax
import jax.numpy as jnp
from jax import lax
from jax.experimental import pallas as pl
from jax.experimental.pallas import tpu as pltpu

_F32 = jnp.float32
_BF = jnp.bfloat16

HEAD_DIM = 64
MIX_W = 256
N_HEADS = MIX_W // HEAD_DIM
CHUNK = 128
KV_BLOCK = 128
K_SHORT = 3
K_CONF = 31
K_FFN = 3
RMS_EPS = 1e-6
LN_EPS = 1e-5
ADAM_LR = 0.001
ADAM_B1 = 0.9
ADAM_B2 = 0.999
ADAM_EPS = 1e-08
ADAM_WD = 0.01
ADAM_STEP = 10
N_DEV = 8
VMEM_LIMIT = 56 * 1024 * 1024


def _bf(x):
    return x.astype(_BF)


def _ld(ref):
    return ref[...].astype(_F32)


_ANY_SPEC = pl.BlockSpec(memory_space=pl.ANY)


def _params(*sem):
    return pltpu.CompilerParams(dimension_semantics=sem, vmem_limit_bytes=VMEM_LIMIT)


def _dot(a, b):
    return jnp.dot(a, b, preferred_element_type=_F32)


def _dot_nt(a, b):
    return lax.dot_general(a, b, (((1,), (1,)), ((), ())), preferred_element_type=_F32)


def _dot_tn(a, b):
    return lax.dot_general(a, b, (((0,), (0,)), ((), ())), preferred_element_type=_F32)


def _row_tile(t, want):
    return want if t % want == 0 else t


def _pick_tile(rows, unit, max_rows):
    best = 0
    for cand in range(unit, min(rows, max_rows) + 1, unit):
        if rows % cand == 0:
            best = cand
    return best or rows


def _sigmoid(x):
    return 1.0 / (1.0 + jnp.exp(-x))


def _rms_rstd(x):
    return lax.rsqrt(jnp.mean(x * x, axis=-1, keepdims=True) + RMS_EPS)


def _norm_mm(x, g, w_t, name):
    t, d = x.shape
    n = w_t.shape[0]
    tm = _row_tile(t, 512)
    tn = _row_tile(n, 512)

    def body(x_ref, g_ref, w_ref, p_ref, h_ref):
        xv = x_ref[...]
        h = _bf(xv * _rms_rstd(xv) * g_ref[...])
        h_ref[...] = h
        for n0 in range(0, n, tn):
            p_ref[:, n0:n0 + tn] = _bf(_dot_nt(h, w_ref[n0:n0 + tn, :]))

    return pl.pallas_call(
        body, name=name, grid=(t // tm,),
        in_specs=[pl.BlockSpec((tm, d), lambda i: (i, 0)), pl.BlockSpec((1, d), lambda i: (0, 0)),
                  pl.BlockSpec((n, d), lambda i: (0, 0))],
        out_specs=[pl.BlockSpec((tm, n), lambda i: (i, 0)), pl.BlockSpec((tm, d), lambda i: (i, 0))],
        out_shape=[jax.ShapeDtypeStruct((t, n), _BF), jax.ShapeDtypeStruct((t, d), _BF)],
        compiler_params=_params("parallel"),
    )(x, g, w_t)


def _mm_nt(a, w_t, name, after=None):
    t, k = a.shape
    n = w_t.shape[0]
    tm = _row_tile(t, 512)
    tn = _row_tile(n, 512) if n % 512 == 0 else _row_tile(n, 256)

    def body(a_ref, w_ref, *rest):
        o_ref = rest[-1]
        av = a_ref[...]
        for n0 in range(0, n, tn):
            o_ref[:, n0:n0 + tn] = _bf(_dot_nt(av, w_ref[n0:n0 + tn, :]))

    extra = () if after is None else (after,)
    return pl.pallas_call(
        body, name=name, grid=(t // tm,),
        in_specs=[pl.BlockSpec((tm, k), lambda i: (i, 0)), pl.BlockSpec((n, k), lambda i: (0, 0))] + [_ANY_SPEC] * len(extra),
        out_specs=pl.BlockSpec((tm, n), lambda i: (i, 0)),
        out_shape=jax.ShapeDtypeStruct((t, n), _BF),
        compiler_params=_params("parallel"),
    )(a, w_t, *extra)


def _mm_res(parts, w, x, name):
    t = x.shape[0]
    k, d = w.shape
    tm = _row_tile(t, 512)
    widths = [a.shape[1] for a in parts]
    n_parts = len(parts)

    def body(*refs):
        w_ref, x_ref, o_ref = refs[n_parts:]
        acc, off = x_ref[...], 0
        for a_ref, width in zip(refs[:n_parts], widths):
            acc = acc + _dot(a_ref[...], w_ref[off:off + width, :])
            off += width
        o_ref[...] = acc

    return pl.pallas_call(
        body, name=name, grid=(t // tm,),
        in_specs=[pl.BlockSpec((tm, width), lambda i: (i, 0)) for width in widths] + [
            pl.BlockSpec((k, d), lambda i: (0, 0)), pl.BlockSpec((tm, d), lambda i: (i, 0))],
        out_specs=pl.BlockSpec((tm, d), lambda i: (i, 0)),
        out_shape=jax.ShapeDtypeStruct((t, d), _F32),
        compiler_params=_params("parallel"),
    )(*parts, w, x)


def _res_norm_mm(parts, w_res, x, g, w_t, name):
    t, d = x.shape
    k = w_res.shape[0]
    n = w_t.shape[0]
    tm = _row_tile(t, 512)
    tn = _row_tile(n, 512)
    widths = [a.shape[1] for a in parts]
    n_parts = len(parts)

    def body(*refs):
        wr_ref, x_ref, g_ref, wt_ref, xo_ref, p_ref, h_ref = refs[n_parts:]
        xv, off = x_ref[...], 0
        for a_ref, width in zip(refs[:n_parts], widths):
            xv = xv + _dot(a_ref[...], wr_ref[off:off + width, :])
            off += width
        xo_ref[...] = xv
        h = _bf(xv * _rms_rstd(xv) * g_ref[...])
        h_ref[...] = h
        for n0 in range(0, n, tn):
            p_ref[:, n0:n0 + tn] = _bf(_dot_nt(h, wt_ref[n0:n0 + tn, :]))

    row = lambda width: pl.BlockSpec((tm, width), lambda i: (i, 0))
    const = lambda shape: pl.BlockSpec(shape, lambda i: (0, 0))
    return pl.pallas_call(
        body, name=name, grid=(t // tm,),
        in_specs=[row(width) for width in widths] + [const((k, d)), row(d), const((1, d)), const((n, d))],
        out_specs=[row(d), row(n), row(d)],
        out_shape=[jax.ShapeDtypeStruct((t, d), _F32), jax.ShapeDtypeStruct((t, n), _BF), jax.ShapeDtypeStruct((t, d), _BF)],
        compiler_params=_params("parallel"),
    )(*parts, w_res, x, g, w_t)


def _res_final_loss(parts, w_res, x, g, target, name):
    t, d = x.shape
    k = w_res.shape[0]
    tm = _row_tile(t, 512)
    widths = [a.shape[1] for a in parts]
    n_parts = len(parts)

    def body(*refs):
        wr_ref, x_ref, g_ref, t_ref, dx_ref, dxb_ref, dg_ref, loss_ref = refs[n_parts:]
        xv, off = x_ref[...], 0
        for a_ref, width in zip(refs[:n_parts], widths):
            xv = xv + _dot(a_ref[...], wr_ref[off:off + width, :])
            off += width
        _loss_head(xv, g_ref, t_ref, dx_ref, dxb_ref, dg_ref, loss_ref, d)

    row = lambda width: pl.BlockSpec((tm, width), lambda i: (i, 0))
    const = lambda shape: pl.BlockSpec(shape, lambda i: (0, 0))
    return pl.pallas_call(
        body, name=name, grid=(t // tm,),
        in_specs=[row(width) for width in widths] + [const((k, d)), row(d), const((1, d)), row(d)],
        out_specs=[row(d), row(d), const((1, d)), const((1, 1))],
        out_shape=[jax.ShapeDtypeStruct((t, d), _F32), jax.ShapeDtypeStruct((t, d), _BF),
                   jax.ShapeDtypeStruct((1, d), _F32), jax.ShapeDtypeStruct((1, 1), _F32)],
        compiler_params=_params("arbitrary"),
    )(*parts, w_res, x, g, target)


def _loss_head(xv, g_ref, t_ref, dx_ref, dxb_ref, dg_ref, loss_ref, d):
    rstd = _rms_rstd(xv)
    xn = xv * rstd
    err = xn * g_ref[...] - t_ref[...]
    dy = err * (1.0 / d)
    u = dy * g_ref[...]
    dx = rstd * (u - xn * jnp.mean(u * xn, axis=-1, keepdims=True))
    dx_ref[...] = dx
    dxb_ref[...] = _bf(dx)

    @pl.when(pl.program_id(0) == 0)
    def _():
        dg_ref[...] = jnp.zeros_like(dg_ref)
        loss_ref[...] = jnp.zeros_like(loss_ref)

    dg_ref[...] += jnp.sum(dy * xn, axis=0, keepdims=True)
    loss_ref[...] += (0.5 / d) * jnp.sum(jnp.sum(err * err, axis=1, keepdims=True), axis=0, keepdims=True)


def _mm_normbwd(parts, w, x, g, dres, name):
    t = x.shape[0]
    k, d = w.shape
    tm = _row_tile(t, 512)
    widths = [a.shape[1] for a in parts]
    n_parts = len(parts)

    def body(*refs):
        a_refs = refs[:n_parts]
        w_ref, x_ref, g_ref, r_ref, dx_ref, dxb_ref, dg_ref = refs[n_parts:]
        dh, off = None, 0
        for a_ref, width in zip(a_refs, widths):
            term = _dot(_bf(a_ref[...]), w_ref[off:off + width, :])
            dh = term if dh is None else dh + term
            off += width
        xv = x_ref[...]
        rstd = _rms_rstd(xv)
        xn = xv * rstd
        u = dh * g_ref[...]
        dx = r_ref[...] + rstd * (u - xn * jnp.mean(u * xn, axis=-1, keepdims=True))
        dx_ref[...] = dx
        dxb_ref[...] = _bf(dx)

        @pl.when(pl.program_id(0) == 0)
        def _():
            dg_ref[...] = jnp.zeros_like(dg_ref)

        dg_ref[...] += jnp.sum(dh * xn, axis=0, keepdims=True)

    return pl.pallas_call(
        body, name=name, grid=(t // tm,),
        in_specs=[pl.BlockSpec((tm, width), lambda i: (i, 0)) for width in widths] + [
            pl.BlockSpec((k, d), lambda i: (0, 0)),
            pl.BlockSpec((tm, d), lambda i: (i, 0)), pl.BlockSpec((1, d), lambda i: (0, 0)),
            pl.BlockSpec((tm, d), lambda i: (i, 0))],
        out_specs=[pl.BlockSpec((tm, d), lambda i: (i, 0)), pl.BlockSpec((tm, d), lambda i: (i, 0)),
                   pl.BlockSpec((1, d), lambda i: (0, 0))],
        out_shape=[jax.ShapeDtypeStruct((t, d), _F32), jax.ShapeDtypeStruct((t, d), _BF),
                   jax.ShapeDtypeStruct((1, d), _F32)],
        compiler_params=_params("arbitrary"),
    )(*parts, w, x, g, dres)


def _mm_tn(a, b, name, out_dtype):
    t, m = a.shape
    n = b.shape[1]
    tm = _pick_tile(m, 128, 1408)
    tn = _pick_tile(n, 128, 1024)
    tk = _row_tile(t, 1024)
    nk = t // tk

    def body(a_ref, b_ref, o_ref, acc):
        kk = pl.program_id(2)

        @pl.when(kk == 0)
        def _():
            acc[...] = jnp.zeros_like(acc)

        acc[...] += _dot_tn(_bf(a_ref[...]), b_ref[...])

        @pl.when(kk == nk - 1)
        def _():
            o_ref[...] = acc[...].astype(o_ref.dtype)

    return pl.pallas_call(
        body, name=name, grid=(m // tm, n // tn, nk),
        in_specs=[pl.BlockSpec((tk, tm), lambda i, j, kk: (kk, i)), pl.BlockSpec((tk, tn), lambda i, j, kk: (kk, j))],
        out_specs=pl.BlockSpec((tm, tn), lambda i, j, kk: (i, j)),
        out_shape=jax.ShapeDtypeStruct((m, n), out_dtype),
        scratch_shapes=[pltpu.VMEM((tm, tn), _F32)],
        compiler_params=_params("parallel", "parallel", "arbitrary"),
    )(a, b)


def _mm_tn_halves(a0, a1, b, name):
    t, m = a0.shape
    n = b.shape[1]
    tm = _pick_tile(m, 128, 1408)
    tn = _pick_tile(n, 128, 1024)
    tk = _row_tile(t, 1024)
    nk = t // tk
    half = m // tm

    def body(a0_ref, a1_ref, b_ref, o_ref, acc):
        i = pl.program_id(0)
        kk = pl.program_id(2)

        @pl.when(kk == 0)
        def _():
            acc[...] = jnp.zeros_like(acc)

        @pl.when(i < half)
        def _():
            acc[...] += _dot_tn(a0_ref[...], b_ref[...])

        @pl.when(i >= half)
        def _():
            acc[...] += _dot_tn(a1_ref[...], b_ref[...])

        @pl.when(kk == nk - 1)
        def _():
            o_ref[...] = _bf(acc[...])

    return pl.pallas_call(
        body, name=name, grid=(2 * half, n // tn, nk),
        in_specs=[pl.BlockSpec((tk, tm), lambda i, j, kk: (jnp.where(i < half, kk, 0), jnp.minimum(i, half - 1))),
                  pl.BlockSpec((tk, tm), lambda i, j, kk: (jnp.where(i >= half, kk, 0), jnp.maximum(i - half, 0))),
                  pl.BlockSpec((tk, tn), lambda i, j, kk: (kk, j))],
        out_specs=pl.BlockSpec((tm, tn), lambda i, j, kk: (i, j)),
        out_shape=jax.ShapeDtypeStruct((2 * m, n), _BF),
        scratch_shapes=[pltpu.VMEM((tm, tn), _F32)],
        compiler_params=_params("parallel", "parallel", "arbitrary"),
    )(a0, a1, b)


def _mm_tn_parts(parts, b, name, after=None):
    t, n = b.shape
    widths = [a.shape[1] for a in parts]
    m = sum(widths)
    n_parts = len(parts)
    tk = _row_tile(t, 1024)
    nk = t // tk
    extra = () if after is None else (after,)

    def body(*refs):
        b_ref = refs[n_parts]
        o_ref, acc = refs[-2:]
        kk = pl.program_id(0)

        @pl.when(kk == 0)
        def _():
            acc[...] = jnp.zeros_like(acc)

        bv = b_ref[...]
        off = 0
        for a_ref, width in zip(refs[:n_parts], widths):
            acc[off:off + width, :] += _dot_tn(_bf(a_ref[...]), bv)
            off += width

        @pl.when(kk == nk - 1)
        def _():
            o_ref[...] = _bf(acc[...])

    return pl.pallas_call(
        body, name=name, grid=(nk,),
        in_specs=[pl.BlockSpec((tk, width), lambda kk: (kk, 0)) for width in widths] + [pl.BlockSpec((tk, n), lambda kk: (kk, 0))]
        + [_ANY_SPEC] * len(extra),
        out_specs=pl.BlockSpec((m, n), lambda kk: (0, 0)),
        out_shape=jax.ShapeDtypeStruct((m, n), _BF),
        scratch_shapes=[pltpu.VMEM((m, n), _F32)],
        compiler_params=_params("arbitrary"),
    )(*parts, b, *extra)


def _pad_rows(x, pad):
    return jnp.concatenate([x, jnp.zeros((pad, x.shape[1]), x.dtype)], axis=0)


def _shift_down(xp, s):
    return xp if s == 0 else pltpu.roll(xp, s, 0)


def _shift_up(xp, s):
    return xp if s == 0 else pltpu.roll(xp, xp.shape[0] - s, 0)


def _taps3(xp):
    one = _shift_down(xp, 1)
    return xp, one, _shift_down(one, 1)


def _conv3_taps(taps, w_ref):
    return w_ref[2:3, :] * taps[0] + w_ref[1:2, :] * taps[1] + w_ref[0:1, :] * taps[2]


def _conv3(xp, w_ref):
    return _conv3_taps(_taps3(xp), w_ref)


def _conv3_t(dyp, w_ref):
    one = _shift_up(dyp, 1)
    return w_ref[2:3, :] * dyp + w_ref[1:2, :] * one + w_ref[0:1, :] * _shift_up(one, 1)


def _conv3_dw(dyp, taps):
    return [jnp.sum(dyp * taps[2 - k], axis=0, keepdims=True) for k in range(3)]


def _examples_per_step(n_ex):
    return 2 if n_ex % 2 == 0 else 1


def _ffn_mid_fwd(up_pre, wf, n_ex, name):
    t, f2 = up_pre.shape
    f = f2 // 2
    s = t // n_ex
    cb = MIX_W
    nb = f // cb

    per = _examples_per_step(n_ex)

    def body(ug_ref, uv_ref, wg_ref, wv_ref, act_ref, gf_ref, vf_ref):
        for ex in range(per):
            rows = slice(ex * s, (ex + 1) * s)
            gf = _conv3(_pad_rows(ug_ref[rows, :].astype(_F32), 8), wg_ref)[:s]
            vf = _conv3(_pad_rows(uv_ref[rows, :].astype(_F32), 8), wv_ref)[:s]
            act_ref[rows, :] = _bf(gf * _sigmoid(gf) * vf)
            gf_ref[rows, :] = _bf(gf)
            vf_ref[rows, :] = _bf(vf)

    out = pl.BlockSpec((per * s, cb), lambda e, j: (e, j))
    return pl.pallas_call(
        body, name=name, grid=(n_ex // per, nb),
        in_specs=[pl.BlockSpec((per * s, cb), lambda e, j: (e, j)), pl.BlockSpec((per * s, cb), lambda e, j: (e, j + nb)),
                  pl.BlockSpec((K_FFN, cb), lambda e, j: (0, j)), pl.BlockSpec((K_FFN, cb), lambda e, j: (0, j + nb))],
        out_specs=[out, out, out],
        out_shape=[jax.ShapeDtypeStruct((t, f), _BF)] * 3,
        compiler_params=_params("parallel", "parallel"),
    )(up_pre, up_pre, wf, wf)


def _ffn_mid_bwd(up_pre, conv_g, conv_v, wf, dact, n_ex, name):
    t, f2 = up_pre.shape
    f = f2 // 2
    s = t // n_ex
    cb = MIX_W
    nb = f // cb

    per = _examples_per_step(n_ex)

    def body(ug_ref, uv_ref, gf_ref, vf_ref, wg_ref, wv_ref, da_ref, dug_ref, duv_ref, dwg_ref, dwv_ref):
        @pl.when(pl.program_id(1) == 0)
        def _():
            dwg_ref[...] = jnp.zeros_like(dwg_ref)
            dwv_ref[...] = jnp.zeros_like(dwv_ref)

        def finish(dpost, w_ref, x_ref, du_ref, dw_ref, rows):
            ahead = [_pad_rows(dpost, 8)]
            ahead.append(_shift_up(ahead[0], 1))
            ahead.append(_shift_up(ahead[1], 1))
            du_ref[rows, :] = _bf((w_ref[2:3, :] * ahead[0] + w_ref[1:2, :] * ahead[1] + w_ref[0:1, :] * ahead[2])[:s])
            x = x_ref[rows, :].astype(_F32)
            for k in range(K_FFN):
                dw_ref[k:k + 1, :] += jnp.sum(ahead[2 - k][:s] * x, axis=0, keepdims=True)

        for ex in range(per):
            rows = slice(ex * s, (ex + 1) * s)
            gf = gf_ref[rows, :].astype(_F32)
            vf = vf_ref[rows, :].astype(_F32)
            sg = _sigmoid(gf)
            da = da_ref[rows, :].astype(_F32)
            finish(da * vf * sg * (1.0 + gf * (1.0 - sg)), wg_ref, ug_ref, dug_ref, dwg_ref, rows)
            finish(da * gf * sg, wv_ref, uv_ref, duv_ref, dwv_ref, rows)

    blk = per * s
    return pl.pallas_call(
        body, name=name, grid=(nb, n_ex // per),
        in_specs=[pl.BlockSpec((blk, cb), lambda j, e: (e, j)), pl.BlockSpec((blk, cb), lambda j, e: (e, j + nb)),
                  pl.BlockSpec((blk, cb), lambda j, e: (e, j)), pl.BlockSpec((blk, cb), lambda j, e: (e, j)),
                  pl.BlockSpec((K_FFN, cb), lambda j, e: (0, j)), pl.BlockSpec((K_FFN, cb), lambda j, e: (0, j + nb)),
                  pl.BlockSpec((blk, cb), lambda j, e: (e, j))],
        out_specs=[pl.BlockSpec((blk, cb), lambda j, e: (e, j)), pl.BlockSpec((blk, cb), lambda j, e: (e, j)),
                   pl.BlockSpec((K_FFN, cb), lambda j, e: (0, j)), pl.BlockSpec((K_FFN, cb), lambda j, e: (0, j))],
        out_shape=[jax.ShapeDtypeStruct((t, f), _BF), jax.ShapeDtypeStruct((t, f), _BF),
                   jax.ShapeDtypeStruct((K_FFN, f), _F32), jax.ShapeDtypeStruct((K_FFN, f), _F32)],
        compiler_params=_params("parallel", "arbitrary"),
    )(up_pre, up_pre, conv_g, conv_v, wf, wf, dact)


def _pcol(s, j):
    return pl.BlockSpec((s, MIX_W), lambda e, j=j: (e, j))


def _vec(rows=1):
    return pl.BlockSpec((rows, MIX_W), lambda e: (0, 0))


def _mix_a_fwd(p, wa, n_ex, name):
    t = p.shape[0]
    s = t // n_ex

    def body(gb_ref, gc_ref, ha_ref, w_ref, y_ref):
        cv = _conv3(_pad_rows(_ld(gc_ref) * _ld(ha_ref), 8), w_ref)[:s]
        y_ref[...] = _bf(_ld(gb_ref) * cv)

    return pl.pallas_call(
        body, name=name, grid=(n_ex,),
        in_specs=[_pcol(s, 0), _pcol(s, 1), _pcol(s, 2), _vec(K_SHORT)],
        out_specs=pl.BlockSpec((s, MIX_W), lambda e: (e, 0)),
        out_shape=jax.ShapeDtypeStruct((t, MIX_W), _BF),
        compiler_params=_params("parallel"),
    )(p, p, p, wa)


def _mix_a_bwd(p, wa, dmix, n_ex, name):
    t = p.shape[0]
    s = t // n_ex

    def body(gb_ref, gc_ref, ha_ref, w_ref, dy_ref, dp_ref, dw_ref):
        gc = _ld(gc_ref)
        ha = _ld(ha_ref)
        up = _taps3(_pad_rows(gc * ha, 8))
        cv = _conv3_taps(up, w_ref)[:s]
        dy = _ld(dy_ref)
        dcvp = _pad_rows(dy * _ld(gb_ref), 8)
        du = _conv3_t(dcvp, w_ref)[:s]
        dp_ref[:, 0:MIX_W] = _bf(dy * cv)
        dp_ref[:, MIX_W:2 * MIX_W] = _bf(du * ha)
        dp_ref[:, 2 * MIX_W:3 * MIX_W] = _bf(du * gc)

        @pl.when(pl.program_id(0) == 0)
        def _():
            dw_ref[...] = jnp.zeros_like(dw_ref)

        rows = _conv3_dw(dcvp, up)
        for k in range(3):
            dw_ref[k:k + 1, :] += rows[k]

    return pl.pallas_call(
        body, name=name, grid=(n_ex,),
        in_specs=[_pcol(s, 0), _pcol(s, 1), _pcol(s, 2), _vec(K_SHORT), _pcol(s, 0)],
        out_specs=[pl.BlockSpec((s, 3 * MIX_W), lambda e: (e, 0)), _vec(K_SHORT)],
        out_shape=[jax.ShapeDtypeStruct((t, 3 * MIX_W), _BF), jax.ShapeDtypeStruct((K_SHORT, MIX_W), _F32)],
        compiler_params=_params("arbitrary"),
    )(p, p, p, wa, dmix)


CONF_PAD = 32
CONF_ROWS = 64
_CONF_LANES = (slice(0, 128), slice(128, 256))


def _conf_taps(win, ahead):
    n = CONF_ROWS + CONF_PAD
    for b in range(8):
        rot = win if b == 0 else pltpu.roll(win, (n - b) if ahead else b, 0)
        for a in range(4):
            if 8 * a + b < K_CONF:
                yield rot, 8 * a + b, (8 * a) if ahead else (CONF_PAD - 8 * a)


def _ln_fwd(x, g, b):
    mu = jnp.mean(x, axis=-1, keepdims=True)
    xc = x - mu
    rstd = lax.rsqrt(jnp.mean(xc * xc, axis=-1, keepdims=True) + LN_EPS)
    xhat = xc * rstd
    return xhat * g + b, xhat, rstd


def _ln_bwd(dy, xhat, rstd, g):
    dxh = dy * g
    return rstd * (dxh - jnp.mean(dxh, axis=-1, keepdims=True) - xhat * jnp.mean(dxh * xhat, axis=-1, keepdims=True))


def _mix_b_fwd(p, wb, bb, lg, lb, n_ex, name):
    t = p.shape[0]
    s = t // n_ex

    def body(val_ref, gat_ref, w_ref, bb_ref, lg_ref, lb_ref, y_ref, cb_ref, xpad):
        xpad[0:CONF_PAD, :] = jnp.zeros((CONF_PAD, MIX_W), _F32)
        xpad[CONF_PAD:, :] = _ld(val_ref) * _sigmoid(_ld(gat_ref))

        def chunk(c, carry):
            r0 = pl.multiple_of(c * CONF_ROWS, CONF_ROWS)
            for lanes in _CONF_LANES:
                acc = None
                for rot, sh, lo in _conf_taps(xpad[pl.ds(r0, CONF_ROWS + CONF_PAD), lanes], False):
                    term = w_ref[K_CONF - 1 - sh:K_CONF - sh, lanes] * rot[lo:lo + CONF_ROWS]
                    acc = term if acc is None else acc + term
                cb_ref[pl.ds(r0, CONF_ROWS), lanes] = acc + bb_ref[:, lanes]
            return carry

        lax.fori_loop(0, s // CONF_ROWS, chunk, 0)
        yl, _, _ = _ln_fwd(cb_ref[...], lg_ref[...], lb_ref[...])
        y_ref[...] = _bf(yl * _sigmoid(yl))

    return pl.pallas_call(
        body, name=name, grid=(n_ex,),
        in_specs=[_pcol(s, 3), _pcol(s, 4), _vec(K_CONF), _vec(), _vec(), _vec()],
        out_specs=[pl.BlockSpec((s, MIX_W), lambda e: (e, 0)), pl.BlockSpec((s, MIX_W), lambda e: (e, 0))],
        out_shape=[jax.ShapeDtypeStruct((t, MIX_W), _BF), jax.ShapeDtypeStruct((t, MIX_W), _F32)],
        scratch_shapes=[pltpu.VMEM((CONF_PAD + s, MIX_W), _F32)],
        compiler_params=_params("parallel"),
    )(p, p, wb, bb, lg, lb)


def _mix_b_bwd(p, cb, wb, lg, lb, dmix, n_ex, name):
    t = p.shape[0]
    s = t // n_ex

    def body(val_ref, gat_ref, cb_ref, w_ref, lg_ref, lb_ref, dy_ref, dp_ref, dw_ref, dbb_ref, dlg_ref, dlb_ref,
             xpad, dpad, dglu_s, dw_acc):
        @pl.when(pl.program_id(0) == 0)
        def _():
            for r in (dw_ref, dbb_ref, dlg_ref, dlb_ref):
                r[...] = jnp.zeros_like(r)

        yl, xhat, rstd = _ln_fwd(cb_ref[...], lg_ref[...], lb_ref[...])
        sy = _sigmoid(yl)
        dyl = _ld(dy_ref) * sy * (1.0 + yl * (1.0 - sy))
        dlg_ref[...] += jnp.sum(dyl * xhat, axis=0, keepdims=True)
        dlb_ref[...] += jnp.sum(dyl, axis=0, keepdims=True)
        dcb = _ln_bwd(dyl, xhat, rstd, lg_ref[...])
        dbb_ref[...] += jnp.sum(dcb, axis=0, keepdims=True)

        val = _ld(val_ref)
        sg = _sigmoid(_ld(gat_ref))
        xpad[0:CONF_PAD, :] = jnp.zeros((CONF_PAD, MIX_W), _F32)
        xpad[CONF_PAD:, :] = val * sg
        dpad[0:s, :] = dcb
        dpad[s:, :] = jnp.zeros((CONF_PAD, MIX_W), _F32)
        dw_acc[...] = jnp.zeros_like(dw_acc)

        def chunk(c, carry):
            r0 = pl.multiple_of(c * CONF_ROWS, CONF_ROWS)
            for lanes in _CONF_LANES:
                d_win = dpad[pl.ds(r0, CONF_ROWS + CONF_PAD), lanes]
                d_rows = d_win[0:CONF_ROWS]
                acc = None
                for rot, sh, lo in _conf_taps(d_win, True):
                    term = w_ref[K_CONF - 1 - sh:K_CONF - sh, lanes] * rot[lo:lo + CONF_ROWS]
                    acc = term if acc is None else acc + term
                dglu_s[pl.ds(r0, CONF_ROWS), lanes] = acc
                for rot, sh, lo in _conf_taps(xpad[pl.ds(r0, CONF_ROWS + CONF_PAD), lanes], False):
                    prod = d_rows * rot[lo:lo + CONF_ROWS]
                    dw_acc[K_CONF - 1 - sh, :, lanes] += jnp.sum(prod.reshape(CONF_ROWS // 8, 8, 128), axis=0)
            return carry

        lax.fori_loop(0, s // CONF_ROWS, chunk, 0)
        dw_ref[...] += jnp.sum(dw_acc[...], axis=1)
        dglu = dglu_s[...]
        dp_ref[:, 0:MIX_W] = _bf(dglu * sg)
        dp_ref[:, MIX_W:2 * MIX_W] = _bf(dglu * val * sg * (1.0 - sg))

    return pl.pallas_call(
        body, name=name, grid=(n_ex,),
        in_specs=[_pcol(s, 3), _pcol(s, 4), pl.BlockSpec((s, MIX_W), lambda e: (e, 0)), _vec(K_CONF), _vec(), _vec(),
                  _pcol(s, 1)],
        out_specs=[pl.BlockSpec((s, 2 * MIX_W), lambda e: (e, 0)), _vec(K_CONF), _vec(), _vec(), _vec()],
        out_shape=[jax.ShapeDtypeStruct((t, 2 * MIX_W), _BF), jax.ShapeDtypeStruct((K_CONF, MIX_W), _F32),
                   jax.ShapeDtypeStruct((1, MIX_W), _F32), jax.ShapeDtypeStruct((1, MIX_W), _F32),
                   jax.ShapeDtypeStruct((1, MIX_W), _F32)],
        scratch_shapes=[pltpu.VMEM((CONF_PAD + s, MIX_W), _F32), pltpu.VMEM((s + CONF_PAD, MIX_W), _F32),
                        pltpu.VMEM((s, MIX_W), _F32), pltpu.VMEM((K_CONF, 8, MIX_W), _F32)],
        compiler_params=_params("arbitrary"),
    )(p, p, cb, wb, lg, lb, dmix)


_INV_SQRT2 = 0.7071067811865476
_INV_SQRT2PI = 0.3989422804014327


def _gelu(x):
    return 0.5 * x * (1.0 + lax.erf(x * _INV_SQRT2))


def _gelu_grad(x):
    return 0.5 * (1.0 + lax.erf(x * _INV_SQRT2)) + x * _INV_SQRT2PI * jnp.exp(-0.5 * x * x)


def _head_masks(width=MIX_W):
    lane = lax.broadcasted_iota(jnp.int32, (1, width), 1)
    return [(lane >= h * HEAD_DIM) & (lane < (h + 1) * HEAD_DIM) for h in range(N_HEADS)]


def _tril_mask():
    r = lax.broadcasted_iota(jnp.int32, (CHUNK, CHUNK), 0)
    c = lax.broadcasted_iota(jnp.int32, (CHUNK, CHUNK), 1)
    return c <= r


def _sgu_apply(ws_ref, x3, transpose):
    n = x3.shape[0]
    tril = _tril_mask()
    masks = _head_masks()
    xb = _bf(x3)
    out = jnp.zeros(x3.shape, _F32)
    for h in range(N_HEADS):
        w = _bf(jnp.where(tril, ws_ref[h], 0.0))
        wb = jnp.broadcast_to(w[None], (n, CHUNK, CHUNK))
        dims = (((1,), (1,)), ((0,), (0,))) if transpose else (((2,), (1,)), ((0,), (0,)))
        r = lax.dot_general(wb, xb, dims, preferred_element_type=_F32)
        out = out + jnp.where(masks[h][None], r, 0.0)
    return out


def _mix_c_fwd(p, lg, lb, ws, sb_full, n_ex, name):
    t = p.shape[0]
    s = t // n_ex
    nc = s // CHUNK

    def body(pu_ref, pv_ref, lg_ref, lb_ref, ws_ref, sb_ref, y_ref):
        u = _gelu(_ld(pu_ref))
        vl, _, _ = _ln_fwd(_gelu(_ld(pv_ref)), lg_ref[...], lb_ref[...])
        sp = _sgu_apply(ws_ref, vl.reshape(nc, CHUNK, MIX_W), False) + sb_ref[...][None]
        y_ref[...] = _bf(u * sp.reshape(s, MIX_W))

    return pl.pallas_call(
        body, name=name, grid=(n_ex,),
        in_specs=[_pcol(s, 5), _pcol(s, 6), _vec(), _vec(),
                  pl.BlockSpec((N_HEADS, CHUNK, CHUNK), lambda e: (0, 0, 0)), pl.BlockSpec((CHUNK, MIX_W), lambda e: (0, 0))],
        out_specs=pl.BlockSpec((s, MIX_W), lambda e: (e, 0)),
        out_shape=jax.ShapeDtypeStruct((t, MIX_W), _BF),
        compiler_params=_params("parallel"),
    )(p, p, lg, lb, ws, sb_full)


def _mix_c_bwd(p, lg, lb, ws, sb_full, dmix, n_ex, name):
    t = p.shape[0]
    s = t // n_ex
    nc = s // CHUNK

    def body(pu_ref, pv_ref, lg_ref, lb_ref, ws_ref, sb_ref, dy_ref, dp_ref, dlg_ref, dlb_ref, dws_ref, dsb_ref):
        @pl.when(pl.program_id(0) == 0)
        def _():
            for r in (dlg_ref, dlb_ref, dws_ref, dsb_ref):
                r[...] = jnp.zeros_like(r)

        pu = _ld(pu_ref)
        pv = _ld(pv_ref)
        u = _gelu(pu)
        vl, xhat, rstd = _ln_fwd(_gelu(pv), lg_ref[...], lb_ref[...])
        vl3 = vl.reshape(nc, CHUNK, MIX_W)
        sp = _sgu_apply(ws_ref, vl3, False) + sb_ref[...][None]
        dy = _ld(dy_ref)
        dp_ref[:, 0:MIX_W] = _bf(dy * sp.reshape(s, MIX_W) * _gelu_grad(pu))
        dsp3 = (dy * u).reshape(nc, CHUNK, MIX_W)
        dsb_full = jnp.sum(dsp3, axis=0)
        masks = _head_masks()
        tril = _tril_mask()
        dspb = _bf(dsp3)
        vlb = _bf(vl3)
        for h in range(N_HEADS):
            dsb_ref[:, h:h + 1] += jnp.sum(jnp.where(masks[h], dsb_full, 0.0), axis=1, keepdims=True)
            dm = jnp.where(masks[h][None], dspb, jnp.zeros_like(dspb))
            g3 = lax.dot_general(dm, vlb, (((2,), (2,)), ((0,), (0,))), preferred_element_type=_F32)
            dws_ref[h] += jnp.where(tril, jnp.sum(g3, axis=0), 0.0)
        dvl = _sgu_apply(ws_ref, dsp3, True).reshape(s, MIX_W)
        dlg_ref[...] += jnp.sum(dvl * xhat, axis=0, keepdims=True)
        dlb_ref[...] += jnp.sum(dvl, axis=0, keepdims=True)
        dp_ref[:, MIX_W:2 * MIX_W] = _bf(_ln_bwd(dvl, xhat, rstd, lg_ref[...]) * _gelu_grad(pv))

    return pl.pallas_call(
        body, name=name, grid=(n_ex,),
        in_specs=[_pcol(s, 5), _pcol(s, 6), _vec(), _vec(),
                  pl.BlockSpec((N_HEADS, CHUNK, CHUNK), lambda e: (0, 0, 0)), pl.BlockSpec((CHUNK, MIX_W), lambda e: (0, 0)),
                  _pcol(s, 2)],
        out_specs=[pl.BlockSpec((s, 2 * MIX_W), lambda e: (e, 0)), _vec(), _vec(),
                   pl.BlockSpec((N_HEADS, CHUNK, CHUNK), lambda e: (0, 0, 0)), pl.BlockSpec((CHUNK, N_HEADS), lambda e: (0, 0))],
        out_shape=[jax.ShapeDtypeStruct((t, 2 * MIX_W), _BF), jax.ShapeDtypeStruct((1, MIX_W), _F32),
                   jax.ShapeDtypeStruct((1, MIX_W), _F32), jax.ShapeDtypeStruct((N_HEADS, CHUNK, CHUNK), _F32),
                   jax.ShapeDtypeStruct((CHUNK, N_HEADS), _F32)],
        compiler_params=_params("arbitrary"),
    )(p, p, lg, lb, ws, sb_full, dmix)


D_QBLOCK = 512
HEAD_COLS = N_HEADS * KV_BLOCK


def _stack_heads(x3):
    return jnp.stack([_bf(jnp.where(m[None], x3, 0.0)) for m in _head_masks()], axis=1)


def _stack_heads_rows(x):
    return jnp.concatenate([_bf(jnp.where(m, x, 0.0)) for m in _head_masks()], axis=0)


def _cols_to_rows(x):
    return jnp.concatenate([x[:, h * KV_BLOCK:(h + 1) * KV_BLOCK] for h in range(N_HEADS)], axis=0)


def _head_sums(x):
    return [jnp.sum(x[:, h * KV_BLOCK:(h + 1) * KV_BLOCK], axis=1, keepdims=True) for h in range(N_HEADS)]


def _spread(cols):
    tq = cols[0].shape[0]
    return jnp.concatenate([jnp.broadcast_to(c, (tq, KV_BLOCK)) for c in cols], axis=1)


def _pair_dot(x, m2):
    half = 2 * KV_BLOCK
    xb = _bf(x)
    return jnp.concatenate([_dot(xb[:, :half], m2), _dot(xb[:, half:], m2)], axis=1)


def _tri2(lower):
    n = 2 * KV_BLOCK
    r = lax.broadcasted_iota(jnp.int32, (n, n), 0)
    c = lax.broadcasted_iota(jnp.int32, (n, n), 1)
    same = (r >= KV_BLOCK) == (c >= KV_BLOCK)
    return _bf(jnp.where(same & (r > c if lower else r < c), 1.0, 0.0))


def _sb_scores(qs, kc, j, t_idx, on_diagonal):
    z = _dot_nt(qs, kc)
    lb = jnp.minimum(z, 0.0) - jnp.log(1.0 + jnp.exp(-jnp.abs(z)))
    if not on_diagonal:
        return (lambda x: x), lb, lb - z
    lane = lax.broadcasted_iota(jnp.int32, (1, HEAD_COLS), 1)
    valid = (j * KV_BLOCK + (lane & (KV_BLOCK - 1))) < t_idx
    keep = lambda x: jnp.where(valid, x, 0.0)
    return keep, lb, keep(lb - z)


RUN_LANES = 128


def _run_lane(j, h):
    return lax.broadcasted_iota(jnp.int32, (1, RUN_LANES), 1) == j * N_HEADS + h


def _d_qblock(s):
    return D_QBLOCK if s % D_QBLOCK == 0 else KV_BLOCK


def _mix_d_fwd(p, n_ex, name):
    t = p.shape[0]
    s = t // n_ex
    tq = _d_qblock(s)
    nq = s // tq
    r = tq // KV_BLOCK
    nb = s // KV_BLOCK
    assert nb * N_HEADS <= RUN_LANES

    def body(q_ref, k_ref, v_ref, y_ref, runs_ref, kc, vc):
        i = pl.program_id(1)

        @pl.when(i == 0)
        def _():
            kc[...] = _stack_heads(k_ref[...].reshape(nb, KV_BLOCK, MIX_W))
            vc[...] = _stack_heads(v_ref[...].reshape(nb, KV_BLOCK, MIX_W))

        qs = _bf(_ld(q_ref) * (HEAD_DIM ** -0.5))
        t_idx = i * tq + lax.broadcasted_iota(jnp.int32, (tq, 1), 0)
        after_m = _tri2(True)
        nkb = (i + 1) * r

        runs_ref[...] = jnp.zeros_like(runs_ref)

        def one_block(j, runs, acc, on_diagonal):
            keep, lb, c = _sb_scores(qs, kc[j].reshape(HEAD_COLS, MIX_W), j, t_idx, on_diagonal)
            a = keep(jnp.exp(lb + _pair_dot(c, after_m) + _spread(runs)))
            acc = acc + _dot(_bf(a), vc[j].reshape(HEAD_COLS, MIX_W))
            kept = runs_ref[...]
            for h in range(N_HEADS):
                kept = jnp.where(_run_lane(j, h), runs[h], kept)
            runs_ref[...] = kept
            return tuple(ru + cs for ru, cs in zip(runs, _head_sums(c))), acc

        def trip(last, carry, on_diagonal):
            runs, acc = carry
            for sub in range(r):
                runs, acc = one_block(last - sub, runs, acc, on_diagonal)
            return runs, acc

        zero = jnp.zeros((tq, 1), _F32)
        carry = trip(nkb - 1, ((zero,) * N_HEADS, jnp.zeros((tq, MIX_W), _F32)), True)
        below = lambda m: nkb - 1 - (m + 1) * r
        carry = lax.fori_loop(0, i // 2, lambda m, carry: trip(below(2 * m + 1), trip(below(2 * m), carry, False), False), carry)
        _, acc = lax.fori_loop(0, i % 2, lambda m, carry: trip(below(i - 1), carry, False), carry)
        y_ref[...] = _bf(acc)

    return pl.pallas_call(
        body, name=name, grid=(n_ex, nq),
        in_specs=[pl.BlockSpec((tq, MIX_W), lambda e, i: (e * nq + i, 7)), pl.BlockSpec((s, MIX_W), lambda e, i: (e, 8)),
                  pl.BlockSpec((s, MIX_W), lambda e, i: (e, 9))],
        out_specs=[pl.BlockSpec((tq, MIX_W), lambda e, i: (e * nq + i, 0)),
                   pl.BlockSpec((tq, RUN_LANES), lambda e, i: (e * nq + i, 0))],
        out_shape=[jax.ShapeDtypeStruct((t, MIX_W), _BF), jax.ShapeDtypeStruct((t, RUN_LANES), _F32)],
        scratch_shapes=[pltpu.VMEM((nb, N_HEADS, KV_BLOCK, MIX_W), _BF), pltpu.VMEM((nb, N_HEADS, KV_BLOCK, MIX_W), _BF)],
        compiler_params=_params("parallel", "arbitrary"),
    )(p, p, p)


def _mix_d_bwd(p, kept_runs, dmix, n_ex, name):
    t = p.shape[0]
    s = t // n_ex
    tq = _d_qblock(s)
    nq = s // tq
    r = tq // KV_BLOCK
    nb = s // KV_BLOCK
    scale = HEAD_DIM ** -0.5

    def body(q_ref, k_ref, v_ref, runs_ref, do_ref, dq_ref, dk_ref, dv_ref, kc, vc):
        i = pl.program_id(1)

        @pl.when(i == 0)
        def _():
            kc[...] = _stack_heads(k_ref[...].reshape(nb, KV_BLOCK, MIX_W))
            vc[...] = _stack_heads(v_ref[...].reshape(nb, KV_BLOCK, MIX_W))
            dk_ref[...] = jnp.zeros_like(dk_ref)
            dv_ref[...] = jnp.zeros_like(dv_ref)

        q_scaled = _ld(q_ref) * scale
        qs = _bf(q_scaled)
        do = do_ref[...]
        dob = _bf(do)
        q_rows = _stack_heads_rows(q_scaled)
        do_rows = _stack_heads_rows(do)
        kept = runs_ref[...]
        t_idx = i * tq + lax.broadcasted_iota(jnp.int32, (tq, 1), 0)
        after_m = _tri2(True)
        before_m = _tri2(False)
        nkb = (i + 1) * r
        zero = jnp.zeros((tq, 1), _F32)

        def trip(first, carry, on_diagonal):
            for sub in range(r):
                carry = one_block(first + sub, carry, on_diagonal)
            return carry

        def one_block(j, carry, on_diagonal):
            pres, dq = carry
            rows = pl.ds(pl.multiple_of(j * KV_BLOCK, KV_BLOCK), KV_BLOCK)
            kj = kc[j].reshape(HEAD_COLS, MIX_W)
            keep, lb, c = _sb_scores(qs, kj, j, t_idx, on_diagonal)
            runs = [jnp.sum(jnp.where(_run_lane(j, h), kept, 0.0), axis=1, keepdims=True) for h in range(N_HEADS)]
            a = keep(jnp.exp(lb + _pair_dot(c, after_m) + _spread(runs)))
            g = a * _dot_nt(dob, vc[j].reshape(HEAD_COLS, MIX_W))
            before = _pair_dot(g, before_m) + _spread(pres)
            sig = jnp.exp(lb)
            dz = _bf(keep(g * (1.0 - sig) - sig * before))
            dk_ref[rows, :] += _dot_tn(_cols_to_rows(dz), q_rows)
            dv_ref[rows, :] += _dot_tn(_cols_to_rows(_bf(a)), do_rows)
            return tuple(pr + gs for pr, gs in zip(pres, _head_sums(g))), dq + _dot(dz, kj)

        init = ((zero,) * N_HEADS, jnp.zeros((tq, MIX_W), _F32))
        carry = lax.fori_loop(0, i // 2, lambda m, carry: trip((2 * m + 1) * r, trip(2 * m * r, carry, False), False), init)
        carry = lax.fori_loop(0, i % 2, lambda m, carry: trip((i - 1) * r, carry, False), carry)
        _, dq = trip(i * r, carry, True)
        dq_ref[...] = _bf(dq * scale)

    return pl.pallas_call(
        body, name=name, grid=(n_ex, nq),
        in_specs=[pl.BlockSpec((tq, MIX_W), lambda e, i: (e * nq + i, 7)), pl.BlockSpec((s, MIX_W), lambda e, i: (e, 8)),
                  pl.BlockSpec((s, MIX_W), lambda e, i: (e, 9)), pl.BlockSpec((tq, RUN_LANES), lambda e, i: (e * nq + i, 0)),
                  pl.BlockSpec((tq, MIX_W), lambda e, i: (e * nq + i, 3))],
        out_specs=[pl.BlockSpec((tq, MIX_W), lambda e, i: (e * nq + i, 0)), pl.BlockSpec((s, MIX_W), lambda e, i: (e, 0)),
                   pl.BlockSpec((s, MIX_W), lambda e, i: (e, 0))],
        out_shape=[jax.ShapeDtypeStruct((t, MIX_W), _BF), jax.ShapeDtypeStruct((t, MIX_W), _F32),
                   jax.ShapeDtypeStruct((t, MIX_W), _F32)],
        scratch_shapes=[pltpu.VMEM((nb, N_HEADS, KV_BLOCK, MIX_W), _BF), pltpu.VMEM((nb, N_HEADS, KV_BLOCK, MIX_W), _BF)],
        compiler_params=_params("parallel", "arbitrary"),
    )(p, p, p, kept_runs, dmix)


def _fwd_mix(x, w, l, n_ex):
    p, h1 = _norm_mm(x, w["norm1_g"][l], w["w_in_t"][l], "in_proj")
    y_a = _mix_a_fwd(p, w["conv_a_w"][l], n_ex, "mix_a_fwd")
    y_b, cb = _mix_b_fwd(p, w["conv_b_w"][l], w["conv_b_b"][l], w["ln_b_g"][l], w["ln_b_b"][l], n_ex, "mix_b_fwd")
    y_c = _mix_c_fwd(p, w["ln_c_g"][l], w["ln_c_b"][l], w["sgu_w"][l], w["sgu_b_full"][l], n_ex, "mix_c_fwd")
    y_d, runs_d = _mix_d_fwd(p, n_ex, "mix_d_fwd")
    return dict(x=x, h1=h1, p=p, cb=cb, runs_d=runs_d, mix=(y_a, y_b, y_c, y_d))


def _fwd_ffn(st, w, l, n_ex):
    x1, up_pre, h2 = _res_norm_mm(st["mix"], w["w_out"][l], st["x"], w["norm2_g"][l], w["w_up_t"][l], "out_up_proj")
    act, conv_g, conv_v = _ffn_mid_fwd(up_pre, w["conv_f_w"][l], n_ex, "ffn_mid_fwd")
    st.update(x1=x1, h2=h2, up_pre=up_pre, act=act, conv_g=conv_g, conv_v=conv_v)


def _down_proj(st, w, l):
    return _mm_res((st["act"],), w["w_down"][l], st["x1"], "down_proj")


def _down_proj_loss(st, w, l, target):
    return _res_final_loss((st["act"],), w["w_down"][l], st["x1"], w["final_g"], target, "down_proj_loss")


def _bwd_ffn(st, w, l, dx, dxb, n_ex):
    g = {}
    dact = _mm_nt(dxb, w["w_down"][l], "down_proj_dx")
    g["w_down"] = _mm_tn(st["act"], dxb, "down_proj_dw", _BF)
    dup_g, dup_v, dwf_g, dwf_v = _ffn_mid_bwd(
        st["up_pre"], st["conv_g"], st["conv_v"], w["conv_f_w"][l], dact, n_ex, "ffn_mid_bwd")
    g["conv_f_w"] = jnp.concatenate([dwf_g, dwf_v], axis=1)
    dx, dxb, g["norm2_g"] = _mm_normbwd((dup_g, dup_v), w["w_up_t"][l], st["x1"], w["norm2_g"][l], dx, "up_proj_dx")
    g["w_up_t"] = _mm_tn_halves(dup_g, dup_v, st["h2"], "up_proj_dw")
    return dx, dxb, g


def _bwd_out_proj(st, w, l, dxb):
    return _mm_nt(dxb, w["w_out"][l], "out_proj_dx"), _mm_tn_parts(st["mix"], dxb, "out_proj_dw")


def _bwd_mixers(st, w, l, dx, dmix, n_ex):
    g = {}
    p = st["p"]
    dp_a, g["conv_a_w"] = _mix_a_bwd(p, w["conv_a_w"][l], dmix, n_ex, "mix_a_bwd")
    dp_b, g["conv_b_w"], g["conv_b_b"], g["ln_b_g"], g["ln_b_b"] = _mix_b_bwd(
        p, st["cb"], w["conv_b_w"][l], w["ln_b_g"][l], w["ln_b_b"][l], dmix, n_ex, "mix_b_bwd")
    dp_c, g["ln_c_g"], g["ln_c_b"], g["sgu_w"], g["sgu_b_t"] = _mix_c_bwd(
        p, w["ln_c_g"][l], w["ln_c_b"][l], w["sgu_w"][l], w["sgu_b_full"][l], dmix, n_ex, "mix_c_bwd")
    dq, dk, dv = _mix_d_bwd(p, st["runs_d"], dmix, n_ex, "mix_d_bwd")
    dp = (dp_a, dp_b, dp_c, dq, dk, dv)
    dx, dxb, g["norm1_g"] = _mm_normbwd(dp, w["w_in_t"][l], st["x"], w["norm1_g"][l], dx, "in_proj_dx")
    return dx, dxb, g, dp


def _bwd_mix(st, w, l, dx, dxb, n_ex):
    dmix, dw_out = _bwd_out_proj(st, w, l, dxb)
    dx, dxb, g, dp = _bwd_mixers(st, w, l, dx, dmix, n_ex)
    g["w_out"] = dw_out
    g["w_in_t"] = _mm_tn_parts(dp, st["h1"], "in_proj_dw")
    return dx, dxb, g


_MESH = pl.DeviceIdType.MESH
_ANY = pl.BlockSpec(memory_space=pl.ANY)


def _position():
    return lax.axis_index("x"), lax.axis_index("y"), lax.axis_index("c")


def _flat(px, py, pc):
    return 4 * px + 2 * py + pc


def _all_gather(shard, name, after):
    r, c_ = shard.shape

    def body(x_ref, after_ref, out_ref, send_sems, recv_sems, local_sem):
        x, y, c = _position()
        me, sibling = (x, y, c), (x, y, 1 - c)
        chips = [(1 - x, y), (x, 1 - y), (1 - x, 1 - y)]

        def copy(k, block, to, src=None):
            slab = out_ref.at[_flat(*block)]
            return pltpu.make_async_remote_copy(
                src_ref=slab if src is None else src, dst_ref=slab, send_sem=send_sems.at[k], recv_sem=recv_sems.at[k],
                device_id=to, device_id_type=_MESH)

        mine = pltpu.make_async_copy(x_ref, out_ref.at[_flat(*me)], local_sem)
        mine.start()
        first = [copy(0, me, sibling, src=x_ref)]
        first += [copy(1 + j, me, (*chip, c), src=x_ref) for j, chip in enumerate(chips)]
        for cp in first:
            cp.start()
        passed = [copy(4 + j, (*chip, c), sibling) for j, chip in enumerate(chips)]
        for j, chip in enumerate(chips):
            copy(1 + j, (*chip, c), me).wait_recv()
            passed[j].start()
        copy(0, sibling, me).wait_recv()
        for j, chip in enumerate(chips):
            copy(4 + j, (*chip, 1 - c), me).wait_recv()
        for cp in first + passed:
            cp.wait_send()
        mine.wait()

    return pl.pallas_call(
        body, name=name, out_shape=jax.ShapeDtypeStruct((N_DEV, r, c_), shard.dtype),
        in_specs=[_ANY, _ANY], out_specs=_ANY,
        scratch_shapes=[pltpu.SemaphoreType.DMA((7,)), pltpu.SemaphoreType.DMA((7,)), pltpu.SemaphoreType.DMA],
    )(shard, after)


_HBM = pl.BlockSpec(memory_space=pltpu.HBM)
_SEM = pl.BlockSpec(memory_space=pltpu.SEMAPHORE)
_DATAFLOW = pltpu.SideEffectType.DATAFLOW_SIDE_EFFECTING


def _peers(x, y, c):
    return [((1 - x) if (k + 1) & 4 else x, (1 - y) if (k + 1) & 2 else y, (1 - c) if (k + 1) & 1 else c)
            for k in range(N_DEV - 1)]


def _direct_copies(src_refs, land_refs, send_sems, recv_sems, to_all):
    x, y, c = _position()
    my = _flat(x, y, c)
    out, back = [], []
    for m, (src_ref, land_ref) in enumerate(zip(src_refs, land_refs)):
        for k, peer in enumerate(_peers(x, y, c)):
            src = src_ref if to_all else src_ref.at[_flat(*peer)]
            n = m * (N_DEV - 1) + k
            sems = dict(send_sem=send_sems.at[n], recv_sem=recv_sems.at[n], device_id=peer, device_id_type=_MESH)
            out.append(pltpu.make_async_remote_copy(src_ref=src, dst_ref=land_ref.at[my], **sems))
            back.append(pltpu.make_async_remote_copy(src_ref=src, dst_ref=land_ref.at[_flat(*peer)], **sems))
    return out, back


def _exchange_start(srcs, to_all, after, name):
    n = len(srcs)
    n_sems = n * (N_DEV - 1)
    land_shapes = [(N_DEV,) + tuple(a.shape[-2:]) for a in srcs]

    def body(*refs):
        src_refs, land_refs = refs[:n], refs[n:2 * n]
        send_sems, recv_sems = refs[2 * n + 1], refs[2 * n + 2]
        token = refs[-1]
        for cp in _direct_copies(src_refs, land_refs, send_sems, recv_sems, to_all)[0]:
            cp.start()
        token[...] = jnp.zeros_like(token)

    lands = [pltpu.with_memory_space_constraint(lax.empty(shp, a.dtype), pltpu.HBM) for shp, a in zip(land_shapes, srcs)]
    outs = pl.pallas_call(
        body, name=name,
        out_shape=(pltpu.SemaphoreType.DMA((n_sems,)), pltpu.SemaphoreType.DMA((n_sems,)),
                   *[pltpu.HBM(a.shape, a.dtype) for a in srcs], *[pltpu.HBM(shp, a.dtype) for shp, a in zip(land_shapes, srcs)],
                   jax.ShapeDtypeStruct((8, 128), _F32)),
        in_specs=(_HBM,) * (2 * n) + (_ANY,),
        out_specs=(_SEM, _SEM) + (_HBM,) * (2 * n) + (pl.BlockSpec(memory_space=pltpu.VMEM),),
        input_output_aliases={i: 2 + i for i in range(2 * n)},
        compiler_params=pltpu.CompilerParams(has_side_effects=_DATAFLOW),
    )(*[pltpu.with_memory_space_constraint(a, pltpu.HBM) for a in srcs], *lands, after)
    return (outs[0], outs[1], outs[2:2 + n], outs[2 + n:2 + 2 * n], to_all), outs[-1]


def _exchange_wait(handle, after, name):
    send_sems, recv_sems, srcs, lands, to_all = handle
    n = len(srcs)

    def body(*refs):
        out, back = _direct_copies(refs[:n], refs[n:2 * n], refs[2 * n], refs[2 * n + 1], to_all)
        for cp in out:
            cp.wait_send()
        for cp in back:
            cp.wait_recv()

    outs = pl.pallas_call(
        body, name=name,
        out_shape=tuple(pltpu.HBM(a.shape, a.dtype) for a in (*srcs, *lands)),
        in_specs=(_HBM,) * (2 * n) + (_SEM, _SEM, _ANY), out_specs=(_HBM,) * (2 * n),
        input_output_aliases={i: i for i in range(2 * n)},
        compiler_params=pltpu.CompilerParams(has_side_effects=_DATAFLOW),
    )(*srcs, *lands, send_sems, recv_sems, after)
    return outs[:n], outs[n:]


def _with_own(landed, own):
    my = _flat(*_position())
    return lax.dynamic_update_slice(landed, own[None], (my, 0, 0))


def _sum_slabs(slabs, own, name):
    n, r, c_ = slabs.shape
    tr = _pick_tile(r, 16, max(16, (12 << 20) // (n * c_ * slabs.dtype.itemsize)))

    def body(x_ref, own_ref, o_ref):
        my = _flat(*_position())
        acc = None
        for k in range(n):
            term = jnp.where(my == k, own_ref[...], x_ref[k]).astype(_F32)
            acc = term if acc is None else acc + term
        o_ref[...] = acc

    return pl.pallas_call(
        body, name=name, grid=(r // tr,),
        in_specs=[pl.BlockSpec((n, tr, c_), lambda i: (0, i, 0)), pl.BlockSpec((tr, c_), lambda i: (i, 0))],
        out_specs=pl.BlockSpec((tr, c_), lambda i: (i, 0)),
        out_shape=jax.ShapeDtypeStruct((r, c_), _F32),
        compiler_params=_params("parallel"),
    )(slabs, own)


def _adamw(w, g, m, v, name):
    r, c_ = w.shape
    tr = _pick_tile(r, 8, 512)

    def body(w_ref, g_ref, m_ref, v_ref, d_ref, nm_ref, nv_ref):
        _adamw_refs(w_ref, g_ref, m_ref, v_ref, d_ref, nm_ref, nv_ref)

    spec = pl.BlockSpec((tr, c_), lambda i: (i, 0))
    shape = jax.ShapeDtypeStruct((r, c_), _F32)
    return pl.pallas_call(
        body, name=name, grid=(r // tr,), in_specs=[spec] * 4, out_specs=[spec] * 3, out_shape=[shape] * 3,
        compiler_params=_params("parallel"),
    )(w, g, m, v)


def _adamw_refs(w_ref, g_ref, m_ref, v_ref, d_ref, nm_ref, nv_ref):
    gv = g_ref[...]
    nm = ADAM_B1 * m_ref[...] + (1.0 - ADAM_B1) * gv
    nv = ADAM_B2 * v_ref[...] + (1.0 - ADAM_B2) * (gv * gv)
    m_hat = nm / (1.0 - ADAM_B1 ** ADAM_STEP)
    v_hat = nv / (1.0 - ADAM_B2 ** ADAM_STEP)
    d_ref[...] = -ADAM_LR * (m_hat / (jnp.sqrt(v_hat) + ADAM_EPS) + ADAM_WD * w_ref[...])
    nm_ref[...] = nm
    nv_ref[...] = nv


def _adamw_small(params, name):
    n = len(params)

    def body(*refs):
        for i in range(n):
            _adamw_refs(*refs[4 * i:4 * i + 4], *refs[4 * n + 3 * i:4 * n + 3 * i + 3])

    outs = pl.pallas_call(
        body, name=name,
        out_shape=[jax.ShapeDtypeStruct(p[0].shape, _F32) for p in params for _ in range(3)],
        compiler_params=pltpu.CompilerParams(vmem_limit_bytes=VMEM_LIMIT),
    )(*[a for p in params for a in p])
    return [tuple(outs[3 * i:3 * i + 3]) for i in range(n)]


_SMALL = ("norm1_g", "conv_a_w", "conv_b_w", "conv_b_b", "ln_b_g", "ln_b_b", "ln_c_g", "ln_c_b", "sgu_w", "sgu_b",
          "norm2_g", "conv_f_w", "final_g")
_CONV_SHARDED = ("conv_a_w", "conv_b_w", "conv_f_w")
_NAMES = ("norm1_g", "w_in", "conv_a_w", "conv_b_w", "conv_b_b", "ln_b_g", "ln_b_b", "ln_c_g", "ln_c_b", "sgu_w", "sgu_b",
          "w_out", "norm2_g", "w_up", "conv_f_w", "w_down", "final_g")


def _pack_rows(parts, lanes=128, row_multiple=8):
    flat = jnp.concatenate([a.reshape(-1) for a in parts])
    rows = -(-flat.shape[0] // lanes)
    rows = -(-rows // row_multiple) * row_multiple
    return jnp.pad(flat, (0, rows * lanes - flat.shape[0])).reshape(rows, lanes)


def _unpack_rows(packed, shapes):
    flat = packed.reshape(-1)
    out, off = [], 0
    for shp in shapes:
        size = 1
        for s in shp:
            size *= s
        out.append(flat[off:off + size].reshape(shp))
        off += size
    return out


def _gather_conv_weights(conv_a_w, conv_b_w, conv_f_w, after):
    shards = (conv_a_w, conv_b_w, conv_f_w)
    flat = _all_gather(_pack_rows(shards), "gather_conv_weights", after).reshape(N_DEV, -1)
    full, off = [], 0
    for s in shards:
        layers, taps, width = s.shape
        per_dev = flat[:, off:off + s.size].reshape(N_DEV, layers, taps, width)
        full.append(jnp.moveaxis(per_dev, 0, 2).reshape(layers, taps, N_DEV * width))
        off += s.size
    return full


def kernel(x, norm1_g, w_in, conv_a_w, conv_b_w, conv_b_b, ln_b_g, ln_b_b, ln_c_g, ln_c_b, sgu_w, sgu_b, w_out, norm2_g, w_up, conv_f_w, w_down, final_g, loss_target, m_norm1_g, m_w_in, m_conv_a_w, m_conv_b_w, m_conv_b_b, m_ln_b_g, m_ln_b_b, m_ln_c_g, m_ln_c_b, m_sgu_w, m_sgu_b, m_w_out, m_norm2_g, m_w_up, m_conv_f_w, m_w_down, m_final_g, v_norm1_g, v_w_in, v_conv_a_w, v_conv_b_w, v_conv_b_b, v_ln_b_g, v_ln_b_b, v_ln_c_g, v_ln_c_b, v_sgu_w, v_sgu_b, v_w_out, v_norm2_g, v_w_up, v_conv_f_w, v_w_down, v_final_g):
    weights = dict(norm1_g=norm1_g, w_in=w_in, conv_a_w=conv_a_w, conv_b_w=conv_b_w, conv_b_b=conv_b_b, ln_b_g=ln_b_g,
                   ln_b_b=ln_b_b, ln_c_g=ln_c_g, ln_c_b=ln_c_b, sgu_w=sgu_w, sgu_b=sgu_b, w_out=w_out, norm2_g=norm2_g,
                   w_up=w_up, conv_f_w=conv_f_w, w_down=w_down, final_g=final_g)
    mom1 = dict(norm1_g=m_norm1_g, w_in=m_w_in, conv_a_w=m_conv_a_w, conv_b_w=m_conv_b_w, conv_b_b=m_conv_b_b,
                ln_b_g=m_ln_b_g, ln_b_b=m_ln_b_b, ln_c_g=m_ln_c_g, ln_c_b=m_ln_c_b, sgu_w=m_sgu_w, sgu_b=m_sgu_b,
                w_out=m_w_out, norm2_g=m_norm2_g, w_up=m_w_up, conv_f_w=m_conv_f_w, w_down=m_w_down, final_g=m_final_g)
    mom2 = dict(norm1_g=v_norm1_g, w_in=v_w_in, conv_a_w=v_conv_a_w, conv_b_w=v_conv_b_w, conv_b_b=v_conv_b_b,
                ln_b_g=v_ln_b_g, ln_b_b=v_ln_b_b, ln_c_g=v_ln_c_g, ln_c_b=v_ln_c_b, sgu_w=v_sgu_w, sgu_b=v_sgu_b,
                w_out=v_w_out, norm2_g=v_norm2_g, w_up=v_w_up, conv_f_w=v_conv_f_w, w_down=v_w_down, final_g=v_final_g)
    n_ex, seq, d = x.shape
    depth = w_in.shape[0]
    assert depth == 2
    my = _flat(*_position())
    row = lambda a, l: a[l][None]
    tied = lambda a, token: a + token[0:1, 0:1]

    slab = {"w_in": [_bf(jnp.swapaxes(w_in, 1, 2)[l]) for l in range(depth)], "w_out": [_bf(w_out[l]) for l in range(depth)],
            "w_up": [_bf(jnp.swapaxes(w_up, 1, 2)[l]) for l in range(depth)], "w_down": [_bf(w_down[l]) for l in range(depth)]}
    rows = {name: parts[0].shape[0] for name, parts in slab.items()}
    key_of = {"w_in": "w_in_t", "w_out": "w_out", "w_up": "w_up_t", "w_down": "w_down"}
    rest_layer0 = [("w_out", 0), ("w_up", 0), ("w_down", 0)]
    all_layer1 = [("w_in", 1), ("w_out", 1), ("w_up", 1), ("w_down", 1)]

    w_in0 = _all_gather(slab["w_in"][0], "gather_w_in0", norm1_g)
    conv_a_full, conv_b_full, conv_f_full = _gather_conv_weights(conv_a_w, conv_b_w, conv_f_w, w_in0)
    gather0, token = _exchange_start([slab[n][l] for n, l in rest_layer0], True, conv_f_full, "gather_layer0_start")
    w = {
        "norm1_g": [row(norm1_g, l) for l in range(depth)], "w_in_t": [None] * depth,
        "conv_a_w": [conv_a_full[l] for l in range(depth)], "conv_b_w": [conv_b_full[l] for l in range(depth)],
        "conv_b_b": [row(conv_b_b, l) for l in range(depth)], "ln_b_g": [row(ln_b_g, l) for l in range(depth)],
        "ln_b_b": [row(ln_b_b, l) for l in range(depth)], "ln_c_g": [row(ln_c_g, l) for l in range(depth)],
        "ln_c_b": [row(ln_c_b, l) for l in range(depth)], "sgu_w": [sgu_w[l] for l in range(depth)],
        "sgu_b_full": [jnp.repeat(sgu_b[l].T, HEAD_DIM, axis=1) for l in range(depth)],
        "w_out": [None] * depth, "norm2_g": [row(norm2_g, l) for l in range(depth)], "w_up_t": [None] * depth,
        "conv_f_w": [conv_f_full[l] for l in range(depth)], "w_down": [None] * depth, "final_g": final_g[None],
    }
    w["w_in_t"][0] = w_in0.reshape(N_DEV * rows["w_in"], d)
    w["norm1_g"][0] = tied(row(norm1_g, 0), token)

    def land_weights(handle, after, which, name):
        owns, landed = _exchange_wait(handle, after, name)
        for (n, l), own, got in zip(which, owns, landed):
            w[key_of[n]][l] = _with_own(got, own).reshape(N_DEV * rows[n], d)
        return landed[0]

    st0 = _fwd_mix(x.reshape(n_ex * seq, d), w, 0, n_ex)
    landed0 = land_weights(gather0, st0["mix"][3], rest_layer0, "gather_layer0_wait")
    gather1, token = _exchange_start([slab[n][l] for n, l in all_layer1], True, landed0, "gather_layer1_start")
    w["norm2_g"][0] = tied(row(norm2_g, 0), token)
    _fwd_ffn(st0, w, 0, n_ex)
    x_mid = _down_proj(st0, w, 0)
    land_weights(gather1, x_mid, all_layer1, "gather_layer1_wait")
    st1 = _fwd_mix(x_mid, w, 1, n_ex)
    _fwd_ffn(st1, w, 1, n_ex)
    dx, dxb, d_final_g, loss_part = _down_proj_loss(st1, w, 1, loss_target.reshape(n_ex * seq, d))

    def send_grads(g, which, after, name):
        return _exchange_start([g[key_of[n]].reshape(N_DEV, rows[n], d) for n, _ in which], False, after, name)

    dx, dxb, g_ffn1 = _bwd_ffn(st1, w, 1, dx, dxb, n_ex)
    dx, dxb, g_mix1 = _bwd_mix(st1, w, 1, dx, dxb, n_ex)
    grads1, token = send_grads({**g_ffn1, **g_mix1}, all_layer1, dx, "exchange_layer1_start")
    w["norm2_g"][0] = tied(row(norm2_g, 0), token)
    dx, dxb, g_ffn0 = _bwd_ffn(st0, w, 0, dx, dxb, n_ex)
    g_ffn0["w_out"] = _mm_tn_parts(st0["mix"], dxb, "out_proj_dw")
    ffn_layer0 = [("w_out", 0), ("w_up", 0), ("w_down", 0)]
    grads0a, token = send_grads(g_ffn0, ffn_layer0, dxb, "exchange_ffn0_start")
    dmix = _mm_nt(dxb, w["w_out"][0], "out_proj_dx", after=token)
    dx, dxb, g_mix0, dp0 = _bwd_mixers(st0, w, 0, dx, dmix, n_ex)
    grad_x = dx.reshape(n_ex, seq, d)
    g = {k: [{**g_ffn0, **g_mix0}[k], {**g_ffn1, **g_mix1}[k]] for k in g_mix0.keys() | g_ffn0.keys()}
    g["final_g"] = d_final_g

    small_local = {
        "norm1_g": jnp.stack([a[0] for a in g["norm1_g"]]), "conv_a_w": jnp.stack(g["conv_a_w"]),
        "conv_b_w": jnp.stack(g["conv_b_w"]), "conv_b_b": jnp.stack([a[0] for a in g["conv_b_b"]]),
        "ln_b_g": jnp.stack([a[0] for a in g["ln_b_g"]]), "ln_b_b": jnp.stack([a[0] for a in g["ln_b_b"]]),
        "ln_c_g": jnp.stack([a[0] for a in g["ln_c_g"]]), "ln_c_b": jnp.stack([a[0] for a in g["ln_c_b"]]),
        "sgu_w": jnp.stack(g["sgu_w"]), "sgu_b": jnp.stack([a.T for a in g["sgu_b_t"]]),
        "norm2_g": jnp.stack([a[0] for a in g["norm2_g"]]), "conv_f_w": jnp.stack(g["conv_f_w"]),
        "final_g": g["final_g"][0],
    }
    small_parts = [small_local[k] for k in _SMALL] + [loss_part.reshape(1)]
    small, token = _exchange_start([_pack_rows(small_parts)], True, dx, "gather_small_start")
    g_mix0["w_in_t"] = _mm_tn_parts(dp0, st0["h1"], "in_proj_dw", after=token)
    mix_layer0 = [("w_in", 0)]
    grads0b, token = send_grads(g_mix0, mix_layer0, dx, "exchange_mix0_start")

    reduced = {}

    def land_grads(handle, after, which, name):
        sent, landed = _exchange_wait(handle, after, name + "_wait")
        for (n, l), src, got in zip(which, sent, landed):
            own = lax.dynamic_index_in_dim(src, my, 0, keepdims=False)
            reduced[(n, l)] = _sum_slabs(got, own, name + "_sum_" + n)
        return reduced[which[-1]]

    def stacked_grad(name):
        stacked = jnp.stack([reduced[(name, l)] for l in range(depth)])
        return jnp.swapaxes(stacked, 1, 2) if name in ("w_in", "w_up") else stacked

    done = land_grads(grads1, token, all_layer1, "exchange_layer1")
    land_grads(grads0a, done, ffn_layer0, "exchange_ffn0")
    grads = {name: stacked_grad(name) for name in ("w_out", "w_up", "w_down")}

    delta, new_m, new_v = {}, {}, {}

    def as_2d(name):
        shp = weights[name].shape
        two_d = (-1, shp[-1]) if len(shp) > 1 else (1, shp[0])
        return tuple(a.reshape(two_d) for a in (weights[name], grads[name], mom1[name], mom2[name]))

    def keep(name, outs):
        delta[name], new_m[name], new_v[name] = (o.reshape(weights[name].shape) for o in outs)

    for name in ("w_up", "w_down", "w_out"):
        keep(name, _adamw(*as_2d(name), "adamw_" + name))

    (own,), (landed,) = _exchange_wait(small, new_v["w_out"], "gather_small_wait")
    small_sum = _sum_slabs(landed, own, "sum_small_grads")
    *small_totals, loss_total = _unpack_rows(small_sum, [a.shape for a in small_parts])
    loss = loss_total[0]
    for name, total in zip(_SMALL, small_totals):
        if name in _CONV_SHARDED:
            width = weights[name].shape[-1]
            total = lax.dynamic_slice_in_dim(total, my * width, width, axis=-1)
        grads[name] = total
    at_least_2d = lambda a: a[None] if a.ndim == 1 else a
    small_params = [tuple(at_least_2d(a) for a in (weights[n], grads[n], mom1[n], mom2[n])) for n in _SMALL]
    for name, outs in zip(_SMALL, _adamw_small(small_params, "adamw_small")):
        keep(name, outs)

    land_grads(grads0b, new_v["final_g"], mix_layer0, "exchange_mix0")
    grads["w_in"] = stacked_grad("w_in")
    keep("w_in", _adamw(*as_2d("w_in"), "adamw_w_in"))

    return (loss, grad_x, *[grads[n] for n in _NAMES], *[delta[n] for n in _NAMES], *[new_m[n] for n in _NAMES],
            *[new_v[n] for n in _NAMES])
```

```python
import jax
import jax.numpy as jnp
from jax import lax
from jax.experimental import pallas as pl
from jax.experimental.pallas import tpu as pltpu

_F32 = jnp.float32
_BF = jnp.bfloat16

HEAD_DIM = 64
MIX_W = 256
N_HEADS = MIX_W // HEAD_DIM
CHUNK = 128
KV_BLOCK = 128
K_SHORT = 3
K_CONF = 31
K_FFN = 3
RMS_EPS = 1e-6
LN_EPS = 1e-5
ADAM_LR = 0.001
ADAM_B1 = 0.9
ADAM_B2 = 0.999
ADAM_EPS = 1e-08
ADAM_WD = 0.01
ADAM_STEP = 10
N_DEV = 8
VMEM_LIMIT = 56 * 1024 * 1024


def _bf(x):
    return x.astype(_BF)


def _ld(ref):
    return ref[...].astype(_F32)


_ANY_SPEC = pl.BlockSpec(memory_space=pl.ANY)


def _params(*sem):
    return pltpu.CompilerParams(dimension_semantics=sem, vmem_limit_bytes=VMEM_LIMIT)


def _dot(a, b):
    return jnp.dot(a, b, preferred_element_type=_F32)


def _dot_nt(a, b):
    return lax.dot_general(a, b, (((1,), (1,)), ((), ())), preferred_element_type=_F32)


def _dot_tn(a, b):
    return lax.dot_general(a, b, (((0,), (0,)), ((), ())), preferred_element_type=_F32)


def _row_tile(t, want):
    return want if t % want == 0 else t


def _pick_tile(rows, unit, max_rows):
    best = 0
    for cand in range(unit, min(rows, max_rows) + 1, unit):
        if rows % cand == 0:
            best = cand
    return best or rows


def _sigmoid(x):
    return 1.0 / (1.0 + jnp.exp(-x))


def _rms_rstd(x):
    return lax.rsqrt(jnp.mean(x * x, axis=-1, keepdims=True) + RMS_EPS)


def _norm_mm(x, g, w_t, name):
    t, d = x.shape
    n = w_t.shape[0]
    tm = _row_tile(t, 512)
    tn = _row_tile(n, 512)

    def body(x_ref, g_ref, w_ref, p_ref, h_ref):
        xv = x_ref[...]
        h = _bf(xv * _rms_rstd(xv) * g_ref[...])
        h_ref[...] = h
        for n0 in range(0, n, tn):
            p_ref[:, n0:n0 + tn] = _bf(_dot_nt(h, w_ref[n0:n0 + tn, :]))

    return pl.pallas_call(
        body, name=name, grid=(t // tm,),
        in_specs=[pl.BlockSpec((tm, d), lambda i: (i, 0)), pl.BlockSpec((1, d), lambda i: (0, 0)),
                  pl.BlockSpec((n, d), lambda i: (0, 0))],
        out_specs=[pl.BlockSpec((tm, n), lambda i: (i, 0)), pl.BlockSpec((tm, d), lambda i: (i, 0))],
        out_shape=[jax.ShapeDtypeStruct((t, n), _BF), jax.ShapeDtypeStruct((t, d), _BF)],
        compiler_params=_params("parallel"),
    )(x, g, w_t)


def _mm_nt(a, w_t, name, after=None):
    t, k = a.shape
    n = w_t.shape[0]
    tm = _row_tile(t, 512)
    tn = _row_tile(n, 512) if n % 512 == 0 else _row_tile(n, 256)

    def body(a_ref, w_ref, *rest):
        o_ref = rest[-1]
        av = a_ref[...]
        for n0 in range(0, n, tn):
            o_ref[:, n0:n0 + tn] = _bf(_dot_nt(av, w_ref[n0:n0 + tn, :]))

    extra = () if after is None else (after,)
    return pl.pallas_call(
        body, name=name, grid=(t // tm,),
        in_specs=[pl.BlockSpec((tm, k), lambda i: (i, 0)), pl.BlockSpec((n, k), lambda i: (0, 0))] + [_ANY_SPEC] * len(extra),
        out_specs=pl.BlockSpec((tm, n), lambda i: (i, 0)),
        out_shape=jax.ShapeDtypeStruct((t, n), _BF),
        compiler_params=_params("parallel"),
    )(a, w_t, *extra)


def _mm_res(parts, w, x, name):
    t = x.shape[0]
    k, d = w.shape
    tm = _row_tile(t, 512)
    widths = [a.shape[1] for a in parts]
    n_parts = len(parts)

    def body(*refs):
        w_ref, x_ref, o_ref = refs[n_parts:]
        acc, off = x_ref[...], 0
        for a_ref, width in zip(refs[:n_parts], widths):
            acc = acc + _dot(a_ref[...], w_ref[off:off + width, :])
            off += width
        o_ref[...] = acc

    return pl.pallas_call(
        body, name=name, grid=(t // tm,),
        in_specs=[pl.BlockSpec((tm, width), lambda i: (i, 0)) for width in widths] + [
            pl.BlockSpec((k, d), lambda i: (0, 0)), pl.BlockSpec((tm, d), lambda i: (i, 0))],
        out_specs=pl.BlockSpec((tm, d), lambda i: (i, 0)),
        out_shape=jax.ShapeDtypeStruct((t, d), _F32),
        compiler_params=_params("parallel"),
    )(*parts, w, x)


def _res_norm_mm(parts, w_res, x, g, w_t, name):
    t, d = x.shape
    k = w_res.shape[0]
    n = w_t.shape[0]
    tm = _row_tile(t, 512)
    tn = _row_tile(n, 512)
    widths = [a.shape[1] for a in parts]
    n_parts = len(parts)

    def body(*refs):
        wr_ref, x_ref, g_ref, wt_ref, xo_ref, p_ref, h_ref = refs[n_parts:]
        xv, off = x_ref[...], 0
        for a_ref, width in zip(refs[:n_parts], widths):
            xv = xv + _dot(a_ref[...], wr_ref[off:off + width, :])
            off += width
        xo_ref[...] = xv
        h = _bf(xv * _rms_rstd(xv) * g_ref[...])
        h_ref[...] = h
        for n0 in range(0, n, tn):
            p_ref[:, n0:n0 + tn] = _bf(_dot_nt(h, wt_ref[n0:n0 + tn, :]))

    row = lambda width: pl.BlockSpec((tm, width), lambda i: (i, 0))
    const = lambda shape: pl.BlockSpec(shape, lambda i: (0, 0))
    return pl.pallas_call(
        body, name=name, grid=(t // tm,),
        in_specs=[row(width) for width in widths] + [const((k, d)), row(d), const((1, d)), const((n, d))],
        out_specs=[row(d), row(n), row(d)],
        out_shape=[jax.ShapeDtypeStruct((t, d), _F32), jax.ShapeDtypeStruct((t, n), _BF), jax.ShapeDtypeStruct((t, d), _BF)],
        compiler_params=_params("parallel"),
    )(*parts, w_res, x, g, w_t)


def _res_final_loss(parts, w_res, x, g, target, name):
    t, d = x.shape
    k = w_res.shape[0]
    tm = _row_tile(t, 512)
    widths = [a.shape[1] for a in parts]
    n_parts = len(parts)

    def body(*refs):
        wr_ref, x_ref, g_ref, t_ref, dx_ref, dxb_ref, dg_ref, loss_ref = refs[n_parts:]
        xv, off = x_ref[...], 0
        for a_ref, width in zip(refs[:n_parts], widths):
            xv = xv + _dot(a_ref[...], wr_ref[off:off + width, :])
            off += width
        _loss_head(xv, g_ref, t_ref, dx_ref, dxb_ref, dg_ref, loss_ref, d)

    row = lambda width: pl.BlockSpec((tm, width), lambda i: (i, 0))
    const = lambda shape: pl.BlockSpec(shape, lambda i: (0, 0))
    return pl.pallas_call(
        body, name=name, grid=(t // tm,),
        in_specs=[row(width) for width in widths] + [const((k, d)), row(d), const((1, d)), row(d)],
        out_specs=[row(d), row(d), const((1, d)), const((1, 1))],
        out_shape=[jax.ShapeDtypeStruct((t, d), _F32), jax.ShapeDtypeStruct((t, d), _BF),
                   jax.ShapeDtypeStruct((1, d), _F32), jax.ShapeDtypeStruct((1, 1), _F32)],
        compiler_params=_params("arbitrary"),
    )(*parts, w_res, x, g, target)


def _loss_head(xv, g_ref, t_ref, dx_ref, dxb_ref, dg_ref, loss_ref, d):
    rstd = _rms_rstd(xv)
    xn = xv * rstd
    err = xn * g_ref[...] - t_ref[...]
    dy = err * (1.0 / d)
    u = dy * g_ref[...]
    dx = rstd * (u - xn * jnp.mean(u * xn, axis=-1, keepdims=True))
    dx_ref[...] = dx
    dxb_ref[...] = _bf(dx)

    @pl.when(pl.program_id(0) == 0)
    def _():
        dg_ref[...] = jnp.zeros_like(dg_ref)
        loss_ref[...] = jnp.zeros_like(loss_ref)

    dg_ref[...] += jnp.sum(dy * xn, axis=0, keepdims=True)
    loss_ref[...] += (0.5 / d) * jnp.sum(jnp.sum(err * err, axis=1, keepdims=True), axis=0, keepdims=True)


def _mm_normbwd(parts, w, x, g, dres, name):
    t = x.shape[0]
    k, d = w.shape
    tm = _row_tile(t, 512)
    widths = [a.shape[1] for a in parts]
    n_parts = len(parts)

    def body(*refs):
        a_refs = refs[:n_parts]
        w_ref, x_ref, g_ref, r_ref, dx_ref, dxb_ref, dg_ref = refs[n_parts:]
        dh, off = None, 0
        for a_ref, width in zip(a_refs, widths):
            term = _dot(_bf(a_ref[...]), w_ref[off:off + width, :])
            dh = term if dh is None else dh + term
            off += width
        xv = x_ref[...]
        rstd = _rms_rstd(xv)
        xn = xv * rstd
        u = dh * g_ref[...]
        dx = r_ref[...] + rstd * (u - xn * jnp.mean(u * xn, axis=-1, keepdims=True))
        dx_ref[...] = dx
        dxb_ref[...] = _bf(dx)

        @pl.when(pl.program_id(0) == 0)
        def _():
            dg_ref[...] = jnp.zeros_like(dg_ref)

        dg_ref[...] += jnp.sum(dh * xn, axis=0, keepdims=True)

    return pl.pallas_call(
        body, name=name, grid=(t // tm,),
        in_specs=[pl.BlockSpec((tm, width), lambda i: (i, 0)) for width in widths] + [
            pl.BlockSpec((k, d), lambda i: (0, 0)),
            pl.BlockSpec((tm, d), lambda i: (i, 0)), pl.BlockSpec((1, d), lambda i: (0, 0)),
            pl.BlockSpec((tm, d), lambda i: (i, 0))],
        out_specs=[pl.BlockSpec((tm, d), lambda i: (i, 0)), pl.BlockSpec((tm, d), lambda i: (i, 0)),
                   pl.BlockSpec((1, d), lambda i: (0, 0))],
        out_shape=[jax.ShapeDtypeStruct((t, d), _F32), jax.ShapeDtypeStruct((t, d), _BF),
                   jax.ShapeDtypeStruct((1, d), _F32)],
        compiler_params=_params("arbitrary"),
    )(*parts, w, x, g, dres)


def _mm_tn(a, b, name, out_dtype):
    t, m = a.shape
    n = b.shape[1]
    tm = _pick_tile(m, 128, 1408)
    tn = _pick_tile(n, 128, 1024)
    tk = _row_tile(t, 1024)
    nk = t // tk

    def body(a_ref, b_ref, o_ref, acc):
        kk = pl.program_id(2)

        @pl.when(kk == 0)
        def _():
            acc[...] = jnp.zeros_like(acc)

        acc[...] += _dot_tn(_bf(a_ref[...]), b_ref[...])

        @pl.when(kk == nk - 1)
        def _():
            o_ref[...] = acc[...].astype(o_ref.dtype)

    return pl.pallas_call(
        body, name=name, grid=(m // tm, n // tn, nk),
        in_specs=[pl.BlockSpec((tk, tm), lambda i, j, kk: (kk, i)), pl.BlockSpec((tk, tn), lambda i, j, kk: (kk, j))],
        out_specs=pl.BlockSpec((tm, tn), lambda i, j, kk: (i, j)),
        out_shape=jax.ShapeDtypeStruct((m, n), out_dtype),
        scratch_shapes=[pltpu.VMEM((tm, tn), _F32)],
        compiler_params=_params("parallel", "parallel", "arbitrary"),
    )(a, b)


def _mm_tn_halves(a0, a1, b, name):
    t, m = a0.shape
    n = b.shape[1]
    tm = _pick_tile(m, 128, 1408)
    tn = _pick_tile(n, 128, 1024)
    tk = _row_tile(t, 1024)
    nk = t // tk
    half = m // tm

    def body(a0_ref, a1_ref, b_ref, o_ref, acc):
        i = pl.program_id(0)
        kk = pl.program_id(2)

        @pl.when(kk == 0)
        def _():
            acc[...] = jnp.zeros_like(acc)

        @pl.when(i < half)
        def _():
            acc[...] += _dot_tn(a0_ref[...], b_ref[...])

        @pl.when(i >= half)
        def _():
            acc[...] += _dot_tn(a1_ref[...], b_ref[...])

        @pl.when(kk == nk - 1)
        def _():
            o_ref[...] = _bf(acc[...])

    return pl.pallas_call(
        body, name=name, grid=(2 * half, n // tn, nk),
        in_specs=[pl.BlockSpec((tk, tm), lambda i, j, kk: (jnp.where(i < half, kk, 0), jnp.minimum(i, half - 1))),
                  pl.BlockSpec((tk, tm), lambda i, j, kk: (jnp.where(i >= half, kk, 0), jnp.maximum(i - half, 0))),
                  pl.BlockSpec((tk, tn), lambda i, j, kk: (kk, j))],
        out_specs=pl.BlockSpec((tm, tn), lambda i, j, kk: (i, j)),
        out_shape=jax.ShapeDtypeStruct((2 * m, n), _BF),
        scratch_shapes=[pltpu.VMEM((tm, tn), _F32)],
        compiler_params=_params("parallel", "parallel", "arbitrary"),
    )(a0, a1, b)


def _mm_tn_parts(parts, b, name, after=None):
    t, n = b.shape
    widths = [a.shape[1] for a in parts]
    m = sum(widths)
    n_parts = len(parts)
    tk = _row_tile(t, 1024)
    nk = t // tk
    extra = () if after is None else (after,)

    def body(*refs):
        b_ref = refs[n_parts]
        o_ref, acc = refs[-2:]
        kk = pl.program_id(0)

        @pl.when(kk == 0)
        def _():
            acc[...] = jnp.zeros_like(acc)

        bv = b_ref[...]
        off = 0
        for a_ref, width in zip(refs[:n_parts], widths):
            acc[off:off + width, :] += _dot_tn(_bf(a_ref[...]), bv)
            off += width

        @pl.when(kk == nk - 1)
        def _():
            o_ref[...] = _bf(acc[...])

    return pl.pallas_call(
        body, name=name, grid=(nk,),
        in_specs=[pl.BlockSpec((tk, width), lambda kk: (kk, 0)) for width in widths] + [pl.BlockSpec((tk, n), lambda kk: (kk, 0))]
        + [_ANY_SPEC] * len(extra),
        out_specs=pl.BlockSpec((m, n), lambda kk: (0, 0)),
        out_shape=jax.ShapeDtypeStruct((m, n), _BF),
        scratch_shapes=[pltpu.VMEM((m, n), _F32)],
        compiler_params=_params("arbitrary"),
    )(*parts, b, *extra)


def _pad_rows(x, pad):
    return jnp.concatenate([x, jnp.zeros((pad, x.shape[1]), x.dtype)], axis=0)


def _shift_down(xp, s):
    return xp if s == 0 else pltpu.roll(xp, s, 0)


def _shift_up(xp, s):
    return xp if s == 0 else pltpu.roll(xp, xp.shape[0] - s, 0)


def _taps3(xp):
    one = _shift_down(xp, 1)
    return xp, one, _shift_down(one, 1)


def _conv3_taps(taps, w_ref):
    return w_ref[2:3, :] * taps[0] + w_ref[1:2, :] * taps[1] + w_ref[0:1, :] * taps[2]


def _conv3(xp, w_ref):
    return _conv3_taps(_taps3(xp), w_ref)


def _conv3_t(dyp, w_ref):
    one = _shift_up(dyp, 1)
    return w_ref[2:3, :] * dyp + w_ref[1:2, :] * one + w_ref[0:1, :] * _shift_up(one, 1)


def _conv3_dw(dyp, taps):
    return [jnp.sum(dyp * taps[2 - k], axis=0, keepdims=True) for k in range(3)]


def _examples_per_step(n_ex):
    return 2 if n_ex % 2 == 0 else 1


def _ffn_mid_fwd(up_pre, wf, n_ex, name):
    t, f2 = up_pre.shape
    f = f2 // 2
    s = t // n_ex
    cb = MIX_W
    nb = f // cb

    per = _examples_per_step(n_ex)

    def body(ug_ref, uv_ref, wg_ref, wv_ref, act_ref, gf_ref, vf_ref):
        for ex in range(per):
            rows = slice(ex * s, (ex + 1) * s)
            gf = _conv3(_pad_rows(ug_ref[rows, :].astype(_F32), 8), wg_ref)[:s]
            vf = _conv3(_pad_rows(uv_ref[rows, :].astype(_F32), 8), wv_ref)[:s]
            act_ref[rows, :] = _bf(gf * _sigmoid(gf) * vf)
            gf_ref[rows, :] = _bf(gf)
            vf_ref[rows, :] = _bf(vf)

    out = pl.BlockSpec((per * s, cb), lambda e, j: (e, j))
    return pl.pallas_call(
        body, name=name, grid=(n_ex // per, nb),
        in_specs=[pl.BlockSpec((per * s, cb), lambda e, j: (e, j)), pl.BlockSpec((per * s, cb), lambda e, j: (e, j + nb)),
                  pl.BlockSpec((K_FFN, cb), lambda e, j: (0, j)), pl.BlockSpec((K_FFN, cb), lambda e, j: (0, j + nb))],
        out_specs=[out, out, out],
        out_shape=[jax.ShapeDtypeStruct((t, f), _BF)] * 3,
        compiler_params=_params("parallel", "parallel"),
    )(up_pre, up_pre, wf, wf)


def _ffn_mid_bwd(up_pre, conv_g, conv_v, wf, dact, n_ex, name):
    t, f2 = up_pre.shape
    f = f2 // 2
    s = t // n_ex
    cb = MIX_W
    nb = f // cb
    per = 1

    def body(ug_ref, uv_ref, gf_ref, vf_ref, wg_ref, wv_ref, da_ref, dug_ref, duv_ref, dwg_ref, dwv_ref):
        @pl.when(pl.program_id(1) == 0)
        def _():
            dwg_ref[...] = jnp.zeros_like(dwg_ref)
            dwv_ref[...] = jnp.zeros_like(dwv_ref)

        def finish(dpost, w_ref, x_ref, du_ref, dw_ref, rows):
            ahead = [_pad_rows(dpost, 8)]
            ahead.append(_shift_up(ahead[0], 1))
            ahead.append(_shift_up(ahead[1], 1))
            du_ref[rows, :] = _bf((w_ref[2:3, :] * ahead[0] + w_ref[1:2, :] * ahead[1] + w_ref[0:1, :] * ahead[2])[:s])
            x = x_ref[rows, :].astype(_F32)
            for k in range(K_FFN):
                dw_ref[k:k + 1, :] += jnp.sum(ahead[2 - k][:s] * x, axis=0, keepdims=True)

        for ex in range(per):
            rows = slice(ex * s, (ex + 1) * s)
            gf = gf_ref[rows, :].astype(_F32)
            vf = vf_ref[rows, :].astype(_F32)
            sg = _sigmoid(gf)
            da = da_ref[rows, :].astype(_F32)
            finish(da * vf * sg * (1.0 + gf * (1.0 - sg)), wg_ref, ug_ref, dug_ref, dwg_ref, rows)
            finish(da * gf * sg, wv_ref, uv_ref, duv_ref, dwv_ref, rows)

    blk = per * s
    return pl.pallas_call(
        body, name=name, grid=(nb, n_ex // per),
        in_specs=[pl.BlockSpec((blk, cb), lambda j, e: (e, j)), pl.BlockSpec((blk, cb), lambda j, e: (e, j + nb)),
                  pl.BlockSpec((blk, cb), lambda j, e: (e, j)), pl.BlockSpec((blk, cb), lambda j, e: (e, j)),
                  pl.BlockSpec((K_FFN, cb), lambda j, e: (0, j)), pl.BlockSpec((K_FFN, cb), lambda j, e: (0, j + nb)),
                  pl.BlockSpec((blk, cb), lambda j, e: (e, j))],
        out_specs=[pl.BlockSpec((blk, cb), lambda j, e: (e, j)), pl.BlockSpec((blk, cb), lambda j, e: (e, j)),
                   pl.BlockSpec((K_FFN, cb), lambda j, e: (0, j)), pl.BlockSpec((K_FFN, cb), lambda j, e: (0, j))],
        out_shape=[jax.ShapeDtypeStruct((t, f), _BF), jax.ShapeDtypeStruct((t, f), _BF),
                   jax.ShapeDtypeStruct((K_FFN, f), _F32), jax.ShapeDtypeStruct((K_FFN, f), _F32)],
        compiler_params=_params("parallel", "arbitrary"),
    )(up_pre, up_pre, conv_g, conv_v, wf, wf, dact)


def _pcol(s, j):
    return pl.BlockSpec((s, MIX_W), lambda e, j=j: (e, j))


def _vec(rows=1):
    return pl.BlockSpec((rows, MIX_W), lambda e: (0, 0))


def _mix_a_fwd(p, wa, n_ex, name):
    t = p.shape[0]
    s = t // n_ex

    def body(gb_ref, gc_ref, ha_ref, w_ref, y_ref):
        cv = _conv3(_pad_rows(_ld(gc_ref) * _ld(ha_ref), 8), w_ref)[:s]
        y_ref[...] = _bf(_ld(gb_ref) * cv)

    return pl.pallas_call(
        body, name=name, grid=(n_ex,),
        in_specs=[_pcol(s, 0), _pcol(s, 1), _pcol(s, 2), _vec(K_SHORT)],
        out_specs=pl.BlockSpec((s, MIX_W), lambda e: (e, 0)),
        out_shape=jax.ShapeDtypeStruct((t, MIX_W), _BF),
        compiler_params=_params("parallel"),
    )(p, p, p, wa)


def _mix_a_bwd(p, wa, dmix, n_ex, name):
    t = p.shape[0]
    s = t // n_ex

    def body(gb_ref, gc_ref, ha_ref, w_ref, dy_ref, dp_ref, dw_ref):
        gc = _ld(gc_ref)
        ha = _ld(ha_ref)
        up = _taps3(_pad_rows(gc * ha, 8))
        cv = _conv3_taps(up, w_ref)[:s]
        dy = _ld(dy_ref)
        dcvp = _pad_rows(dy * _ld(gb_ref), 8)
        du = _conv3_t(dcvp, w_ref)[:s]
        dp_ref[:, 0:MIX_W] = _bf(dy * cv)
        dp_ref[:, MIX_W:2 * MIX_W] = _bf(du * ha)
        dp_ref[:, 2 * MIX_W:3 * MIX_W] = _bf(du * gc)

        @pl.when(pl.program_id(0) == 0)
        def _():
            dw_ref[...] = jnp.zeros_like(dw_ref)

        rows = _conv3_dw(dcvp, up)
        for k in range(3):
            dw_ref[k:k + 1, :] += rows[k]

    return pl.pallas_call(
        body, name=name, grid=(n_ex,),
        in_specs=[_pcol(s, 0), _pcol(s, 1), _pcol(s, 2), _vec(K_SHORT), _pcol(s, 0)],
        out_specs=[pl.BlockSpec((s, 3 * MIX_W), lambda e: (e, 0)), _vec(K_SHORT)],
        out_shape=[jax.ShapeDtypeStruct((t, 3 * MIX_W), _BF), jax.ShapeDtypeStruct((K_SHORT, MIX_W), _F32)],
        compiler_params=_params("arbitrary"),
    )(p, p, p, wa, dmix)


CONF_PAD = 32
CONF_ROWS = 64
_CONF_LANES = (slice(0, 128), slice(128, 256))


def _conf_taps(win, ahead):
    n = CONF_ROWS + CONF_PAD
    for b in range(8):
        rot = win if b == 0 else pltpu.roll(win, (n - b) if ahead else b, 0)
        for a in range(4):
            if 8 * a + b < K_CONF:
                yield rot, 8 * a + b, (8 * a) if ahead else (CONF_PAD - 8 * a)


def _ln_fwd(x, g, b):
    mu = jnp.mean(x, axis=-1, keepdims=True)
    xc = x - mu
    rstd = lax.rsqrt(jnp.mean(xc * xc, axis=-1, keepdims=True) + LN_EPS)
    xhat = xc * rstd
    return xhat * g + b, xhat, rstd


def _ln_bwd(dy, xhat, rstd, g):
    dxh = dy * g
    return rstd * (dxh - jnp.mean(dxh, axis=-1, keepdims=True) - xhat * jnp.mean(dxh * xhat, axis=-1, keepdims=True))


def _mix_b_fwd(p, wb, bb, lg, lb, n_ex, name):
    t = p.shape[0]
    s = t // n_ex

    def body(val_ref, gat_ref, w_ref, bb_ref, lg_ref, lb_ref, y_ref, cb_ref, xpad):
        xpad[0:CONF_PAD, :] = jnp.zeros((CONF_PAD, MIX_W), _F32)
        xpad[CONF_PAD:, :] = _ld(val_ref) * _sigmoid(_ld(gat_ref))

        def chunk(c, carry):
            r0 = pl.multiple_of(c * CONF_ROWS, CONF_ROWS)
            for lanes in _CONF_LANES:
                acc = None
                for rot, sh, lo in _conf_taps(xpad[pl.ds(r0, CONF_ROWS + CONF_PAD), lanes], False):
                    term = w_ref[K_CONF - 1 - sh:K_CONF - sh, lanes] * rot[lo:lo + CONF_ROWS]
                    acc = term if acc is None else acc + term
                cb_ref[pl.ds(r0, CONF_ROWS), lanes] = acc + bb_ref[:, lanes]
            return carry

        lax.fori_loop(0, s // CONF_ROWS, chunk, 0)
        yl, _, _ = _ln_fwd(cb_ref[...], lg_ref[...], lb_ref[...])
        y_ref[...] = _bf(yl * _sigmoid(yl))

    return pl.pallas_call(
        body, name=name, grid=(n_ex,),
        in_specs=[_pcol(s, 3), _pcol(s, 4), _vec(K_CONF), _vec(), _vec(), _vec()],
        out_specs=[pl.BlockSpec((s, MIX_W), lambda e: (e, 0)), pl.BlockSpec((s, MIX_W), lambda e: (e, 0))],
        out_shape=[jax.ShapeDtypeStruct((t, MIX_W), _BF), jax.ShapeDtypeStruct((t, MIX_W), _F32)],
        scratch_shapes=[pltpu.VMEM((CONF_PAD + s, MIX_W), _F32)],
        compiler_params=_params("parallel"),
    )(p, p, wb, bb, lg, lb)


def _mix_b_bwd(p, cb, wb, lg, lb, dmix, n_ex, name):
    t = p.shape[0]
    s = t // n_ex

    def body(val_ref, gat_ref, cb_ref, w_ref, lg_ref, lb_ref, dy_ref, dp_ref, dw_ref, dbb_ref, dlg_ref, dlb_ref,
             xpad, dpad, dglu_s, dw_acc):
        @pl.when(pl.program_id(0) == 0)
        def _():
            for r in (dw_ref, dbb_ref, dlg_ref, dlb_ref):
                r[...] = jnp.zeros_like(r)

        yl, xhat, rstd = _ln_fwd(cb_ref[...], lg_ref[...], lb_ref[...])
        sy = _sigmoid(yl)
        dyl = _ld(dy_ref) * sy * (1.0 + yl * (1.0 - sy))
        dlg_ref[...] += jnp.sum(dyl * xhat, axis=0, keepdims=True)
        dlb_ref[...] += jnp.sum(dyl, axis=0, keepdims=True)
        dcb = _ln_bwd(dyl, xhat, rstd, lg_ref[...])
        dbb_ref[...] += jnp.sum(dcb, axis=0, keepdims=True)

        val = _ld(val_ref)
        sg = _sigmoid(_ld(gat_ref))
        xpad[0:CONF_PAD, :] = jnp.zeros((CONF_PAD, MIX_W), _F32)
        xpad[CONF_PAD:, :] = val * sg
        dpad[0:s, :] = dcb
        dpad[s:, :] = jnp.zeros((CONF_PAD, MIX_W), _F32)
        dw_acc[...] = jnp.zeros_like(dw_acc)

        def chunk(c, carry):
            r0 = pl.multiple_of(c * CONF_ROWS, CONF_ROWS)
            for lanes in _CONF_LANES:
                d_win = dpad[pl.ds(r0, CONF_ROWS + CONF_PAD), lanes]
                d_rows = d_win[0:CONF_ROWS]
                acc = None
                for rot, sh, lo in _conf_taps(d_win, True):
                    term = w_ref[K_CONF - 1 - sh:K_CONF - sh, lanes] * rot[lo:lo + CONF_ROWS]
                    acc = term if acc is None else acc + term
                dglu_s[pl.ds(r0, CONF_ROWS), lanes] = acc
                for rot, sh, lo in _conf_taps(xpad[pl.ds(r0, CONF_ROWS + CONF_PAD), lanes], False):
                    prod = d_rows * rot[lo:lo + CONF_ROWS]
                    dw_acc[K_CONF - 1 - sh, :, lanes] += jnp.sum(prod.reshape(CONF_ROWS // 8, 8, 128), axis=0)
            return carry

        lax.fori_loop(0, s // CONF_ROWS, chunk, 0)
        dw_ref[...] += jnp.sum(dw_acc[...], axis=1)
        dglu = dglu_s[...]
        dp_ref[:, 0:MIX_W] = _bf(dglu * sg)
        dp_ref[:, MIX_W:2 * MIX_W] = _bf(dglu * val * sg * (1.0 - sg))

    return pl.pallas_call(
        body, name=name, grid=(n_ex,),
        in_specs=[_pcol(s, 3), _pcol(s, 4), pl.BlockSpec((s, MIX_W), lambda e: (e, 0)), _vec(K_CONF), _vec(), _vec(),
                  _pcol(s, 1)],
        out_specs=[pl.BlockSpec((s, 2 * MIX_W), lambda e: (e, 0)), _vec(K_CONF), _vec(), _vec(), _vec()],
        out_shape=[jax.ShapeDtypeStruct((t, 2 * MIX_W), _BF), jax.ShapeDtypeStruct((K_CONF, MIX_W), _F32),
                   jax.ShapeDtypeStruct((1, MIX_W), _F32), jax.ShapeDtypeStruct((1, MIX_W), _F32),
                   jax.ShapeDtypeStruct((1, MIX_W), _F32)],
        scratch_shapes=[pltpu.VMEM((CONF_PAD + s, MIX_W), _F32), pltpu.VMEM((s + CONF_PAD, MIX_W), _F32),
                        pltpu.VMEM((s, MIX_W), _F32), pltpu.VMEM((K_CONF, 8, MIX_W), _F32)],
        compiler_params=_params("arbitrary"),
    )(p, p, cb, wb, lg, lb, dmix)


_INV_SQRT2 = 0.7071067811865476
_INV_SQRT2PI = 0.3989422804014327


def _gelu(x):
    return 0.5 * x * (1.0 + lax.erf(x * _INV_SQRT2))


def _gelu_grad(x):
    return 0.5 * (1.0 + lax.erf(x * _INV_SQRT2)) + x * _INV_SQRT2PI * jnp.exp(-0.5 * x * x)


def _head_masks(width=MIX_W):
    lane = lax.broadcasted_iota(jnp.int32, (1, width), 1)
    return [(lane >= h * HEAD_DIM) & (lane < (h + 1) * HEAD_DIM) for h in range(N_HEADS)]


def _tril_mask():
    r = lax.broadcasted_iota(jnp.int32, (CHUNK, CHUNK), 0)
    c = lax.broadcasted_iota(jnp.int32, (CHUNK, CHUNK), 1)
    return c <= r


def _sgu_apply(ws_ref, x3, transpose):
    n = x3.shape[0]
    tril = _tril_mask()
    masks = _head_masks()
    xb = _bf(x3)
    out = jnp.zeros(x3.shape, _F32)
    for h in range(N_HEADS):
        w = _bf(jnp.where(tril, ws_ref[h], 0.0))
        wb = jnp.broadcast_to(w[None], (n, CHUNK, CHUNK))
        dims = (((1,), (1,)), ((0,), (0,))) if transpose else (((2,), (1,)), ((0,), (0,)))
        r = lax.dot_general(wb, xb, dims, preferred_element_type=_F32)
        out = out + jnp.where(masks[h][None], r, 0.0)
    return out


def _mix_c_fwd(p, lg, lb, ws, sb_full, n_ex, name):
    t = p.shape[0]
    s = t // n_ex
    nc = s // CHUNK

    def body(pu_ref, pv_ref, lg_ref, lb_ref, ws_ref, sb_ref, y_ref):
        u = _gelu(_ld(pu_ref))
        vl, _, _ = _ln_fwd(_gelu(_ld(pv_ref)), lg_ref[...], lb_ref[...])
        sp = _sgu_apply(ws_ref, vl.reshape(nc, CHUNK, MIX_W), False) + sb_ref[...][None]
        y_ref[...] = _bf(u * sp.reshape(s, MIX_W))

    return pl.pallas_call(
        body, name=name, grid=(n_ex,),
        in_specs=[_pcol(s, 5), _pcol(s, 6), _vec(), _vec(),
                  pl.BlockSpec((N_HEADS, CHUNK, CHUNK), lambda e: (0, 0, 0)), pl.BlockSpec((CHUNK, MIX_W), lambda e: (0, 0))],
        out_specs=pl.BlockSpec((s, MIX_W), lambda e: (e, 0)),
        out_shape=jax.ShapeDtypeStruct((t, MIX_W), _BF),
        compiler_params=_params("parallel"),
    )(p, p, lg, lb, ws, sb_full)


def _mix_c_bwd(p, lg, lb, ws, sb_full, dmix, n_ex, name):
    t = p.shape[0]
    s = t // n_ex
    nc = s // CHUNK

    def body(pu_ref, pv_ref, lg_ref, lb_ref, ws_ref, sb_ref, dy_ref, dp_ref, dlg_ref, dlb_ref, dws_ref, dsb_ref):
        @pl.when(pl.program_id(0) == 0)
        def _():
            for r in (dlg_ref, dlb_ref, dws_ref, dsb_ref):
                r[...] = jnp.zeros_like(r)

        pu = _ld(pu_ref)
        pv = _ld(pv_ref)
        u = _gelu(pu)
        vl, xhat, rstd = _ln_fwd(_gelu(pv), lg_ref[...], lb_ref[...])
        vl3 = vl.reshape(nc, CHUNK, MIX_W)
        sp = _sgu_apply(ws_ref, vl3, False) + sb_ref[...][None]
        dy = _ld(dy_ref)
        dp_ref[:, 0:MIX_W] = _bf(dy * sp.reshape(s, MIX_W) * _gelu_grad(pu))
        dsp3 = (dy * u).reshape(nc, CHUNK, MIX_W)
        dsb_full = jnp.sum(dsp3, axis=0)
        masks = _head_masks()
        tril = _tril_mask()
        dspb = _bf(dsp3)
        vlb = _bf(vl3)
        for h in range(N_HEADS):
            dsb_ref[:, h:h + 1] += jnp.sum(jnp.where(masks[h], dsb_full, 0.0), axis=1, keepdims=True)
            dm = jnp.where(masks[h][None], dspb, jnp.zeros_like(dspb))
            g3 = lax.dot_general(dm, vlb, (((2,), (2,)), ((0,), (0,))), preferred_element_type=_F32)
            dws_ref[h] += jnp.where(tril, jnp.sum(g3, axis=0), 0.0)
        dvl = _sgu_apply(ws_ref, dsp3, True).reshape(s, MIX_W)
        dlg_ref[...] += jnp.sum(dvl * xhat, axis=0, keepdims=True)
        dlb_ref[...] += jnp.sum(dvl, axis=0, keepdims=True)
        dp_ref[:, MIX_W:2 * MIX_W] = _bf(_ln_bwd(dvl, xhat, rstd, lg_ref[...]) * _gelu_grad(pv))

    return pl.pallas_call(
        body, name=name, grid=(n_ex,),
        in_specs=[_pcol(s, 5), _pcol(s, 6), _vec(), _vec(),
                  pl.BlockSpec((N_HEADS, CHUNK, CHUNK), lambda e: (0, 0, 0)), pl.BlockSpec((CHUNK, MIX_W), lambda e: (0, 0)),
                  _pcol(s, 2)],
        out_specs=[pl.BlockSpec((s, 2 * MIX_W), lambda e: (e, 0)), _vec(), _vec(),
                   pl.BlockSpec((N_HEADS, CHUNK, CHUNK), lambda e: (0, 0, 0)), pl.BlockSpec((CHUNK, N_HEADS), lambda e: (0, 0))],
        out_shape=[jax.ShapeDtypeStruct((t, 2 * MIX_W), _BF), jax.ShapeDtypeStruct((1, MIX_W), _F32),
                   jax.ShapeDtypeStruct((1, MIX_W), _F32), jax.ShapeDtypeStruct((N_HEADS, CHUNK, CHUNK), _F32),
                   jax.ShapeDtypeStruct((CHUNK, N_HEADS), _F32)],
        compiler_params=_params("arbitrary"),
    )(p, p, lg, lb, ws, sb_full, dmix)


D_QBLOCK = 512
HEAD_COLS = N_HEADS * KV_BLOCK


def _stack_heads(x3):
    return jnp.stack([_bf(jnp.where(m[None], x3, 0.0)) for m in _head_masks()], axis=1)


def _stack_heads_rows(x):
    return jnp.concatenate([_bf(jnp.where(m, x, 0.0)) for m in _head_masks()], axis=0)


def _cols_to_rows(x):
    return jnp.concatenate([x[:, h * KV_BLOCK:(h + 1) * KV_BLOCK] for h in range(N_HEADS)], axis=0)


def _head_sums(x):
    return [jnp.sum(x[:, h * KV_BLOCK:(h + 1) * KV_BLOCK], axis=1, keepdims=True) for h in range(N_HEADS)]


def _spread(cols):
    tq = cols[0].shape[0]
    return jnp.concatenate([jnp.broadcast_to(c, (tq, KV_BLOCK)) for c in cols], axis=1)


def _pair_dot(x, m2):
    half = 2 * KV_BLOCK
    xb = _bf(x)
    return jnp.concatenate([_dot(xb[:, :half], m2), _dot(xb[:, half:], m2)], axis=1)


def _tri2(lower):
    n = 2 * KV_BLOCK
    r = lax.broadcasted_iota(jnp.int32, (n, n), 0)
    c = lax.broadcasted_iota(jnp.int32, (n, n), 1)
    same = (r >= KV_BLOCK) == (c >= KV_BLOCK)
    return _bf(jnp.where(same & (r > c if lower else r < c), 1.0, 0.0))


def _sb_scores(qs, kc, j, t_idx, on_diagonal):
    z = _dot_nt(qs, kc)
    lb = jnp.minimum(z, 0.0) - jnp.log(1.0 + jnp.exp(-jnp.abs(z)))
    if not on_diagonal:
        return (lambda x: x), lb, lb - z
    lane = lax.broadcasted_iota(jnp.int32, (1, HEAD_COLS), 1)
    valid = (j * KV_BLOCK + (lane & (KV_BLOCK - 1))) < t_idx
    keep = lambda x: jnp.where(valid, x, 0.0)
    return keep, lb, keep(lb - z)


RUN_LANES = 128


def _run_lane(j, h):
    return lax.broadcasted_iota(jnp.int32, (1, RUN_LANES), 1) == j * N_HEADS + h


def _d_qblock(s):
    return D_QBLOCK if s % D_QBLOCK == 0 else KV_BLOCK


def _mix_d_fwd(p, n_ex, name):
    t = p.shape[0]
    s = t // n_ex
    tq = _d_qblock(s)
    nq = s // tq
    r = tq // KV_BLOCK
    nb = s // KV_BLOCK
    assert nb * N_HEADS <= RUN_LANES

    def body(q_ref, k_ref, v_ref, y_ref, runs_ref, kc, vc):
        i = pl.program_id(1)

        @pl.when(i == 0)
        def _():
            kc[...] = _stack_heads(k_ref[...].reshape(nb, KV_BLOCK, MIX_W))
            vc[...] = _stack_heads(v_ref[...].reshape(nb, KV_BLOCK, MIX_W))

        qs = _bf(_ld(q_ref) * (HEAD_DIM ** -0.5))
        t_idx = i * tq + lax.broadcasted_iota(jnp.int32, (tq, 1), 0)
        after_m = _tri2(True)
        nkb = (i + 1) * r

        runs_ref[...] = jnp.zeros_like(runs_ref)

        def one_block(j, runs, acc, on_diagonal):
            keep, lb, c = _sb_scores(qs, kc[j].reshape(HEAD_COLS, MIX_W), j, t_idx, on_diagonal)
            a = keep(jnp.exp(lb + _pair_dot(c, after_m) + _spread(runs)))
            acc = acc + _dot(_bf(a), vc[j].reshape(HEAD_COLS, MIX_W))
            kept = runs_ref[...]
            for h in range(N_HEADS):
                kept = jnp.where(_run_lane(j, h), runs[h], kept)
            runs_ref[...] = kept
            return tuple(ru + cs for ru, cs in zip(runs, _head_sums(c))), acc

        def trip(last, carry, on_diagonal):
            runs, acc = carry
            for sub in range(r):
                runs, acc = one_block(last - sub, runs, acc, on_diagonal)
            return runs, acc

        zero = jnp.zeros((tq, 1), _F32)
        carry = trip(nkb - 1, ((zero,) * N_HEADS, jnp.zeros((tq, MIX_W), _F32)), True)
        below = lambda m: nkb - 1 - (m + 1) * r
        carry = lax.fori_loop(0, i // 2, lambda m, carry: trip(below(2 * m + 1), trip(below(2 * m), carry, False), False), carry)
        _, acc = lax.fori_loop(0, i % 2, lambda m, carry: trip(below(i - 1), carry, False), carry)
        y_ref[...] = _bf(acc)

    return pl.pallas_call(
        body, name=name, grid=(n_ex, nq),
        in_specs=[pl.BlockSpec((tq, MIX_W), lambda e, i: (e * nq + i, 7)), pl.BlockSpec((s, MIX_W), lambda e, i: (e, 8)),
                  pl.BlockSpec((s, MIX_W), lambda e, i: (e, 9))],
        out_specs=[pl.BlockSpec((tq, MIX_W), lambda e, i: (e * nq + i, 0)),
                   pl.BlockSpec((tq, RUN_LANES), lambda e, i: (e * nq + i, 0))],
        out_shape=[jax.ShapeDtypeStruct((t, MIX_W), _BF), jax.ShapeDtypeStruct((t, RUN_LANES), _F32)],
        scratch_shapes=[pltpu.VMEM((nb, N_HEADS, KV_BLOCK, MIX_W), _BF), pltpu.VMEM((nb, N_HEADS, KV_BLOCK, MIX_W), _BF)],
        compiler_params=_params("parallel", "arbitrary"),
    )(p, p, p)


def _mix_d_bwd(p, kept_runs, dmix, n_ex, name):
    t = p.shape[0]
    s = t // n_ex
    tq = _d_qblock(s)
    nq = s // tq
    r = tq // KV_BLOCK
    nb = s // KV_BLOCK
    scale = HEAD_DIM ** -0.5

    def body(q_ref, k_ref, v_ref, runs_ref, do_ref, dq_ref, dk_ref, dv_ref, kc, vc):
        i = pl.program_id(1)

        @pl.when(i == 0)
        def _():
            kc[...] = _stack_heads(k_ref[...].reshape(nb, KV_BLOCK, MIX_W))
            vc[...] = _stack_heads(v_ref[...].reshape(nb, KV_BLOCK, MIX_W))
            dk_ref[...] = jnp.zeros_like(dk_ref)
            dv_ref[...] = jnp.zeros_like(dv_ref)

        q_scaled = _ld(q_ref) * scale
        qs = _bf(q_scaled)
        do = do_ref[...]
        dob = _bf(do)
        q_rows = _stack_heads_rows(q_scaled)
        do_rows = _stack_heads_rows(do)
        kept = runs_ref[...]
        t_idx = i * tq + lax.broadcasted_iota(jnp.int32, (tq, 1), 0)
        after_m = _tri2(True)
        before_m = _tri2(False)
        nkb = (i + 1) * r
        zero = jnp.zeros((tq, 1), _F32)

        def trip(first, carry, on_diagonal):
            for sub in range(r):
                carry = one_block(first + sub, carry, on_diagonal)
            return carry

        def one_block(j, carry, on_diagonal):
            pres, dq = carry
            rows = pl.ds(pl.multiple_of(j * KV_BLOCK, KV_BLOCK), KV_BLOCK)
            kj = kc[j].reshape(HEAD_COLS, MIX_W)
            keep, lb, c = _sb_scores(qs, kj, j, t_idx, on_diagonal)
            runs = [jnp.sum(jnp.where(_run_lane(j, h), kept, 0.0), axis=1, keepdims=True) for h in range(N_HEADS)]
            a = keep(jnp.exp(lb + _pair_dot(c, after_m) + _spread(runs)))
            g = a * _dot_nt(dob, vc[j].reshape(HEAD_COLS, MIX_W))
            before = _pair_dot(g, before_m) + _spread(pres)
            sig = jnp.exp(lb)
            dz = _bf(keep(g * (1.0 - sig) - sig * before))
            dk_ref[rows, :] += _dot_tn(_cols_to_rows(dz), q_rows)
            dv_ref[rows, :] += _dot_tn(_cols_to_rows(_bf(a)), do_rows)
            return tuple(pr + gs for pr, gs in zip(pres, _head_sums(g))), dq + _dot(dz, kj)

        init = ((zero,) * N_HEADS, jnp.zeros((tq, MIX_W), _F32))
        carry = lax.fori_loop(0, i // 2, lambda m, carry: trip((2 * m + 1) * r, trip(2 * m * r, carry, False), False), init)
        carry = lax.fori_loop(0, i % 2, lambda m, carry: trip((i - 1) * r, carry, False), carry)
        _, dq = trip(i * r, carry, True)
        dq_ref[...] = _bf(dq * scale)

    return pl.pallas_call(
        body, name=name, grid=(n_ex, nq),
        in_specs=[pl.BlockSpec((tq, MIX_W), lambda e, i: (e * nq + i, 7)), pl.BlockSpec((s, MIX_W), lambda e, i: (e, 8)),
                  pl.BlockSpec((s, MIX_W), lambda e, i: (e, 9)), pl.BlockSpec((tq, RUN_LANES), lambda e, i: (e * nq + i, 0)),
                  pl.BlockSpec((tq, MIX_W), lambda e, i: (e * nq + i, 3))],
        out_specs=[pl.BlockSpec((tq, MIX_W), lambda e, i: (e * nq + i, 0)), pl.BlockSpec((s, MIX_W), lambda e, i: (e, 0)),
                   pl.BlockSpec((s, MIX_W), lambda e, i: (e, 0))],
        out_shape=[jax.ShapeDtypeStruct((t, MIX_W), _BF), jax.ShapeDtypeStruct((t, MIX_W), _F32),
                   jax.ShapeDtypeStruct((t, MIX_W), _F32)],
        scratch_shapes=[pltpu.VMEM((nb, N_HEADS, KV_BLOCK, MIX_W), _BF), pltpu.VMEM((nb, N_HEADS, KV_BLOCK, MIX_W), _BF)],
        compiler_params=_params("parallel", "arbitrary"),
    )(p, p, p, kept_runs, dmix)


def _fwd_mix(x, w, l, n_ex):
    p, h1 = _norm_mm(x, w["norm1_g"][l], w["w_in_t"][l], "in_proj")
    y_a = _mix_a_fwd(p, w["conv_a_w"][l], n_ex, "mix_a_fwd")
    y_b, cb = _mix_b_fwd(p, w["conv_b_w"][l], w["conv_b_b"][l], w["ln_b_g"][l], w["ln_b_b"][l], n_ex, "mix_b_fwd")
    y_c = _mix_c_fwd(p, w["ln_c_g"][l], w["ln_c_b"][l], w["sgu_w"][l], w["sgu_b_full"][l], n_ex, "mix_c_fwd")
    y_d, runs_d = _mix_d_fwd(p, n_ex, "mix_d_fwd")
    return dict(x=x, h1=h1, p=p, cb=cb, runs_d=runs_d, mix=(y_a, y_b, y_c, y_d))


def _fwd_ffn(st, w, l, n_ex):
    x1, up_pre, h2 = _res_norm_mm(st["mix"], w["w_out"][l], st["x"], w["norm2_g"][l], w["w_up_t"][l], "out_up_proj")
    act, conv_g, conv_v = _ffn_mid_fwd(up_pre, w["conv_f_w"][l], n_ex, "ffn_mid_fwd")
    st.update(x1=x1, h2=h2, up_pre=up_pre, act=act, conv_g=conv_g, conv_v=conv_v)


def _down_proj(st, w, l):
    return _mm_res((st["act"],), w["w_down"][l], st["x1"], "down_proj")


def _down_proj_loss(st, w, l, target):
    return _res_final_loss((st["act"],), w["w_down"][l], st["x1"], w["final_g"], target, "down_proj_loss")


def _bwd_ffn(st, w, l, dx, dxb, n_ex):
    g = {}
    dact = _mm_nt(dxb, w["w_down"][l], "down_proj_dx")
    g["w_down"] = _mm_tn(st["act"], dxb, "down_proj_dw", _BF)
    dup_g, dup_v, dwf_g, dwf_v = _ffn_mid_bwd(
        st["up_pre"], st["conv_g"], st["conv_v"], w["conv_f_w"][l], dact, n_ex, "ffn_mid_bwd")
    g["conv_f_w"] = jnp.concatenate([dwf_g, dwf_v], axis=1)
    dx, dxb, g["norm2_g"] = _mm_normbwd((dup_g, dup_v), w["w_up_t"][l], st["x1"], w["norm2_g"][l], dx, "up_proj_dx")
    g["w_up_t"] = _mm_tn_halves(dup_g, dup_v, st["h2"], "up_proj_dw")
    return dx, dxb, g


def _bwd_out_proj(st, w, l, dxb):
    return _mm_nt(dxb, w["w_out"][l], "out_proj_dx"), _mm_tn_parts(st["mix"], dxb, "out_proj_dw")


def _bwd_mixers(st, w, l, dx, dmix, n_ex):
    g = {}
    p = st["p"]
    dp_a, g["conv_a_w"] = _mix_a_bwd(p, w["conv_a_w"][l], dmix, n_ex, "mix_a_bwd")
    dp_b, g["conv_b_w"], g["conv_b_b"], g["ln_b_g"], g["ln_b_b"] = _mix_b_bwd(
        p, st["cb"], w["conv_b_w"][l], w["ln_b_g"][l], w["ln_b_b"][l], dmix, n_ex, "mix_b_bwd")
    dp_c, g["ln_c_g"], g["ln_c_b"], g["sgu_w"], g["sgu_b_t"] = _mix_c_bwd(
        p, w["ln_c_g"][l], w["ln_c_b"][l], w["sgu_w"][l], w["sgu_b_full"][l], dmix, n_ex, "mix_c_bwd")
    dq, dk, dv = _mix_d_bwd(p, st["runs_d"], dmix, n_ex, "mix_d_bwd")
    dp = (dp_a, dp_b, dp_c, dq, dk, dv)
    dx, dxb, g["norm1_g"] = _mm_normbwd(dp, w["w_in_t"][l], st["x"], w["norm1_g"][l], dx, "in_proj_dx")
    return dx, dxb, g, dp


def _bwd_mix(st, w, l, dx, dxb, n_ex):
    dmix, dw_out = _bwd_out_proj(st, w, l, dxb)
    dx, dxb, g, dp = _bwd_mixers(st, w, l, dx, dmix, n_ex)
    g["w_out"] = dw_out
    g["w_in_t"] = _mm_tn_parts(dp, st["h1"], "in_proj_dw")
    return dx, dxb, g


_MESH = pl.DeviceIdType.MESH
_ANY = pl.BlockSpec(memory_space=pl.ANY)


def _position():
    return lax.axis_index("x"), lax.axis_index("y"), lax.axis_index("c")


def _flat(px, py, pc):
    return 4 * px + 2 * py + pc


def _all_gather(shard, name, after):
    r, c_ = shard.shape

    def body(x_ref, after_ref, out_ref, send_sems, recv_sems, local_sem):
        x, y, c = _position()
        me, sibling = (x, y, c), (x, y, 1 - c)
        chips = [(1 - x, y), (x, 1 - y), (1 - x, 1 - y)]

        def copy(k, block, to, src=None):
            slab = out_ref.at[_flat(*block)]
            return pltpu.make_async_remote_copy(
                src_ref=slab if src is None else src, dst_ref=slab, send_sem=send_sems.at[k], recv_sem=recv_sems.at[k],
                device_id=to, device_id_type=_MESH)

        mine = pltpu.make_async_copy(x_ref, out_ref.at[_flat(*me)], local_sem)
        mine.start()
        first = [copy(0, me, sibling, src=x_ref)]
        first += [copy(1 + j, me, (*chip, c), src=x_ref) for j, chip in enumerate(chips)]
        for cp in first:
            cp.start()
        passed = [copy(4 + j, (*chip, c), sibling) for j, chip in enumerate(chips)]
        for j, chip in enumerate(chips):
            copy(1 + j, (*chip, c), me).wait_recv()
            passed[j].start()
        copy(0, sibling, me).wait_recv()
        for j, chip in enumerate(chips):
            copy(4 + j, (*chip, 1 - c), me).wait_recv()
        for cp in first + passed:
            cp.wait_send()
        mine.wait()

    return pl.pallas_call(
        body, name=name, out_shape=jax.ShapeDtypeStruct((N_DEV, r, c_), shard.dtype),
        in_specs=[_ANY, _ANY], out_specs=_ANY,
        scratch_shapes=[pltpu.SemaphoreType.DMA((7,)), pltpu.SemaphoreType.DMA((7,)), pltpu.SemaphoreType.DMA],
    )(shard, after)


_HBM = pl.BlockSpec(memory_space=pltpu.HBM)
_SEM = pl.BlockSpec(memory_space=pltpu.SEMAPHORE)
_DATAFLOW = pltpu.SideEffectType.DATAFLOW_SIDE_EFFECTING


def _peers(x, y, c):
    return [((1 - x) if (k + 1) & 4 else x, (1 - y) if (k + 1) & 2 else y, (1 - c) if (k + 1) & 1 else c)
            for k in range(N_DEV - 1)]


def _direct_copies(src_refs, land_refs, send_sems, recv_sems, to_all):
    x, y, c = _position()
    my = _flat(x, y, c)
    out, back = [], []
    for m, (src_ref, land_ref) in enumerate(zip(src_refs, land_refs)):
        for k, peer in enumerate(_peers(x, y, c)):
            src = src_ref if to_all else src_ref.at[_flat(*peer)]
            n = m * (N_DEV - 1) + k
            sems = dict(send_sem=send_sems.at[n], recv_sem=recv_sems.at[n], device_id=peer, device_id_type=_MESH)
            out.append(pltpu.make_async_remote_copy(src_ref=src, dst_ref=land_ref.at[my], **sems))
            back.append(pltpu.make_async_remote_copy(src_ref=src, dst_ref=land_ref.at[_flat(*peer)], **sems))
    return out, back


def _exchange_start(srcs, to_all, after, name):
    n = len(srcs)
    n_sems = n * (N_DEV - 1)
    land_shapes = [(N_DEV,) + tuple(a.shape[-2:]) for a in srcs]

    def body(*refs):
        src_refs, land_refs = refs[:n], refs[n:2 * n]
        send_sems, recv_sems = refs[2 * n + 1], refs[2 * n + 2]
        token = refs[-1]
        for cp in _direct_copies(src_refs, land_refs, send_sems, recv_sems, to_all)[0]:
            cp.start()
        token[...] = jnp.zeros_like(token)

    lands = [pltpu.with_memory_space_constraint(lax.empty(shp, a.dtype), pltpu.HBM) for shp, a in zip(land_shapes, srcs)]
    outs = pl.pallas_call(
        body, name=name,
        out_shape=(pltpu.SemaphoreType.DMA((n_sems,)), pltpu.SemaphoreType.DMA((n_sems,)),
                   *[pltpu.HBM(a.shape, a.dtype) for a in srcs], *[pltpu.HBM(shp, a.dtype) for shp, a in zip(land_shapes, srcs)],
                   jax.ShapeDtypeStruct((8, 128), _F32)),
        in_specs=(_HBM,) * (2 * n) + (_ANY,),
        out_specs=(_SEM, _SEM) + (_HBM,) * (2 * n) + (pl.BlockSpec(memory_space=pltpu.VMEM),),
        input_output_aliases={i: 2 + i for i in range(2 * n)},
        compiler_params=pltpu.CompilerParams(has_side_effects=_DATAFLOW),
    )(*[pltpu.with_memory_space_constraint(a, pltpu.HBM) for a in srcs], *lands, after)
    return (outs[0], outs[1], outs[2:2 + n], outs[2 + n:2 + 2 * n], to_all), outs[-1]


def _exchange_wait(handle, after, name):
    send_sems, recv_sems, srcs, lands, to_all = handle
    n = len(srcs)

    def body(*refs):
        out, back = _direct_copies(refs[:n], refs[n:2 * n], refs[2 * n], refs[2 * n + 1], to_all)
        for cp in out:
            cp.wait_send()
        for cp in back:
            cp.wait_recv()

    outs = pl.pallas_call(
        body, name=name,
        out_shape=tuple(pltpu.HBM(a.shape, a.dtype) for a in (*srcs, *lands)),
        in_specs=(_HBM,) * (2 * n) + (_SEM, _SEM, _ANY), out_specs=(_HBM,) * (2 * n),
        input_output_aliases={i: i for i in range(2 * n)},
        compiler_params=pltpu.CompilerParams(has_side_effects=_DATAFLOW),
    )(*srcs, *lands, send_sems, recv_sems, after)
    return outs[:n], outs[n:]


def _with_own(landed, own):
    my = _flat(*_position())
    return lax.dynamic_update_slice(landed, own[None], (my, 0, 0))


def _sum_slabs(slabs, own, name):
    n, r, c_ = slabs.shape
    tr = _pick_tile(r, 16, max(16, (12 << 20) // (n * c_ * slabs.dtype.itemsize)))

    def body(x_ref, own_ref, o_ref):
        my = _flat(*_position())
        acc = None
        for k in range(n):
            term = jnp.where(my == k, own_ref[...], x_ref[k]).astype(_F32)
            acc = term if acc is None else acc + term
        o_ref[...] = acc

    return pl.pallas_call(
        body, name=name, grid=(r // tr,),
        in_specs=[pl.BlockSpec((n, tr, c_), lambda i: (0, i, 0)), pl.BlockSpec((tr, c_), lambda i: (i, 0))],
        out_specs=pl.BlockSpec((tr, c_), lambda i: (i, 0)),
        out_shape=jax.ShapeDtypeStruct((r, c_), _F32),
        compiler_params=_params("parallel"),
    )(slabs, own)


def _adamw(w, g, m, v, name):
    r, c_ = w.shape
    tr = _pick_tile(r, 8, 512)

    def body(w_ref, g_ref, m_ref, v_ref, d_ref, nm_ref, nv_ref):
        _adamw_refs(w_ref, g_ref, m_ref, v_ref, d_ref, nm_ref, nv_ref)

    spec = pl.BlockSpec((tr, c_), lambda i: (i, 0))
    shape = jax.ShapeDtypeStruct((r, c_), _F32)
    return pl.pallas_call(
        body, name=name, grid=(r // tr,), in_specs=[spec] * 4, out_specs=[spec] * 3, out_shape=[shape] * 3,
        compiler_params=_params("parallel"),
    )(w, g, m, v)


def _adamw_refs(w_ref, g_ref, m_ref, v_ref, d_ref, nm_ref, nv_ref):
    gv = g_ref[...]
    nm = ADAM_B1 * m_ref[...] + (1.0 - ADAM_B1) * gv
    nv = ADAM_B2 * v_ref[...] + (1.0 - ADAM_B2) * (gv * gv)
    m_hat = nm / (1.0 - ADAM_B1 ** ADAM_STEP)
    v_hat = nv / (1.0 - ADAM_B2 ** ADAM_STEP)
    d_ref[...] = -ADAM_LR * (m_hat / (jnp.sqrt(v_hat) + ADAM_EPS) + ADAM_WD * w_ref[...])
    nm_ref[...] = nm
    nv_ref[...] = nv


def _adamw_small(params, name):
    n = len(params)

    def body(*refs):
        for i in range(n):
            _adamw_refs(*refs[4 * i:4 * i + 4], *refs[4 * n + 3 * i:4 * n + 3 * i + 3])

    outs = pl.pallas_call(
        body, name=name,
        out_shape=[jax.ShapeDtypeStruct(p[0].shape, _F32) for p in params for _ in range(3)],
        compiler_params=pltpu.CompilerParams(vmem_limit_bytes=VMEM_LIMIT),
    )(*[a for p in params for a in p])
    return [tuple(outs[3 * i:3 * i + 3]) for i in range(n)]


_SMALL = ("norm1_g", "conv_a_w", "conv_b_w", "conv_b_b", "ln_b_g", "ln_b_b", "ln_c_g", "ln_c_b", "sgu_w", "sgu_b",
          "norm2_g", "conv_f_w", "final_g")
_CONV_SHARDED = ("conv_a_w", "conv_b_w", "conv_f_w")
_NAMES = ("norm1_g", "w_in", "conv_a_w", "conv_b_w", "conv_b_b", "ln_b_g", "ln_b_b", "ln_c_g", "ln_c_b", "sgu_w", "sgu_b",
          "w_out", "norm2_g", "w_up", "conv_f_w", "w_down", "final_g")


def _pack_rows(parts, lanes=128, row_multiple=8):
    flat = jnp.concatenate([a.reshape(-1) for a in parts])
    rows = -(-flat.shape[0] // lanes)
    rows = -(-rows // row_multiple) * row_multiple
    return jnp.pad(flat, (0, rows * lanes - flat.shape[0])).reshape(rows, lanes)


def _unpack_rows(packed, shapes):
    flat = packed.reshape(-1)
    out, off = [], 0
    for shp in shapes:
        size = 1
        for s in shp:
            size *= s
        out.append(flat[off:off + size].reshape(shp))
        off += size
    return out


def _gather_conv_weights(conv_a_w, conv_b_w, conv_f_w, after):
    shards = (conv_a_w, conv_b_w, conv_f_w)
    flat = _all_gather(_pack_rows(shards), "gather_conv_weights", after).reshape(N_DEV, -1)
    full, off = [], 0
    for s in shards:
        layers, taps, width = s.shape
        per_dev = flat[:, off:off + s.size].reshape(N_DEV, layers, taps, width)
        full.append(jnp.moveaxis(per_dev, 0, 2).reshape(layers, taps, N_DEV * width))
        off += s.size
    return full


def kernel(x, norm1_g, w_in, conv_a_w, conv_b_w, conv_b_b, ln_b_g, ln_b_b, ln_c_g, ln_c_b, sgu_w, sgu_b, w_out, norm2_g, w_up, conv_f_w, w_down, final_g, loss_target, m_norm1_g, m_w_in, m_conv_a_w, m_conv_b_w, m_conv_b_b, m_ln_b_g, m_ln_b_b, m_ln_c_g, m_ln_c_b, m_sgu_w, m_sgu_b, m_w_out, m_norm2_g, m_w_up, m_conv_f_w, m_w_down, m_final_g, v_norm1_g, v_w_in, v_conv_a_w, v_conv_b_w, v_conv_b_b, v_ln_b_g, v_ln_b_b, v_ln_c_g, v_ln_c_b, v_sgu_w, v_sgu_b, v_w_out, v_norm2_g, v_w_up, v_conv_f_w, v_w_down, v_final_g):
    weights = dict(norm1_g=norm1_g, w_in=w_in, conv_a_w=conv_a_w, conv_b_w=conv_b_w, conv_b_b=conv_b_b, ln_b_g=ln_b_g,
                   ln_b_b=ln_b_b, ln_c_g=ln_c_g, ln_c_b=ln_c_b, sgu_w=sgu_w, sgu_b=sgu_b, w_out=w_out, norm2_g=norm2_g,
                   w_up=w_up, conv_f_w=conv_f_w, w_down=w_down, final_g=final_g)
    mom1 = dict(norm1_g=m_norm1_g, w_in=m_w_in, conv_a_w=m_conv_a_w, conv_b_w=m_conv_b_w, conv_b_b=m_conv_b_b,
                ln_b_g=m_ln_b_g, ln_b_b=m_ln_b_b, ln_c_g=m_ln_c_g, ln_c_b=m_ln_c_b, sgu_w=m_sgu_w, sgu_b=m_sgu_b,
                w_out=m_w_out, norm2_g=m_norm2_g, w_up=m_w_up, conv_f_w=m_conv_f_w, w_down=m_w_down, final_g=m_final_g)
    mom2 = dict(norm1_g=v_norm1_g, w_in=v_w_in, conv_a_w=v_conv_a_w, conv_b_w=v_conv_b_w, conv_b_b=v_conv_b_b,
                ln_b_g=v_ln_b_g, ln_b_b=v_ln_b_b, ln_c_g=v_ln_c_g, ln_c_b=v_ln_c_b, sgu_w=v_sgu_w, sgu_b=v_sgu_b,
                w_out=v_w_out, norm2_g=v_norm2_g, w_up=v_w_up, conv_f_w=v_conv_f_w, w_down=v_w_down, final_g=v_final_g)
    n_ex, seq, d = x.shape
    depth = w_in.shape[0]
    assert depth == 2
    my = _flat(*_position())
    row = lambda a, l: a[l][None]
    tied = lambda a, token: a + token[0:1, 0:1]

    slab = {"w_in": [_bf(jnp.swapaxes(w_in, 1, 2)[l]) for l in range(depth)], "w_out": [_bf(w_out[l]) for l in range(depth)],
            "w_up": [_bf(jnp.swapaxes(w_up, 1, 2)[l]) for l in range(depth)], "w_down": [_bf(w_down[l]) for l in range(depth)]}
    rows = {name: parts[0].shape[0] for name, parts in slab.items()}
    key_of = {"w_in": "w_in_t", "w_out": "w_out", "w_up": "w_up_t", "w_down": "w_down"}
    rest_layer0 = [("w_out", 0), ("w_up", 0), ("w_down", 0)]
    all_layer1 = [("w_in", 1), ("w_out", 1), ("w_up", 1), ("w_down", 1)]

    w_in0 = _all_gather(slab["w_in"][0], "gather_w_in0", norm1_g)
    conv_a_full, conv_b_full, conv_f_full = _gather_conv_weights(conv_a_w, conv_b_w, conv_f_w, w_in0)
    gather0, token = _exchange_start([slab[n][l] for n, l in rest_layer0], True, conv_f_full, "gather_layer0_start")
    w = {
        "norm1_g": [row(norm1_g, l) for l in range(depth)], "w_in_t": [None] * depth,
        "conv_a_w": [conv_a_full[l] for l in range(depth)], "conv_b_w": [conv_b_full[l] for l in range(depth)],
        "conv_b_b": [row(conv_b_b, l) for l in range(depth)], "ln_b_g": [row(ln_b_g, l) for l in range(depth)],
        "ln_b_b": [row(ln_b_b, l) for l in range(depth)], "ln_c_g": [row(ln_c_g, l) for l in range(depth)],
        "ln_c_b": [row(ln_c_b, l) for l in range(depth)], "sgu_w": [sgu_w[l] for l in range(depth)],
        "sgu_b_full": [jnp.repeat(sgu_b[l].T, HEAD_DIM, axis=1) for l in range(depth)],
        "w_out": [None] * depth, "norm2_g": [row(norm2_g, l) for l in range(depth)], "w_up_t": [None] * depth,
        "conv_f_w": [conv_f_full[l] for l in range(depth)], "w_down": [None] * depth, "final_g": final_g[None],
    }
    w["w_in_t"][0] = w_in0.reshape(N_DEV * rows["w_in"], d)
    w["norm1_g"][0] = tied(row(norm1_g, 0), token)

    def land_weights(handle, after, which, name):
        owns, landed = _exchange_wait(handle, after, name)
        for (n, l), own, got in zip(which, owns, landed):
            w[key_of[n]][l] = _with_own(got, own).reshape(N_DEV * rows[n], d)
        return landed[0]

    st0 = _fwd_mix(x.reshape(n_ex * seq, d), w, 0, n_ex)
    landed0 = land_weights(gather0, st0["mix"][3], rest_layer0, "gather_layer0_wait")
    gather1, token = _exchange_start([slab[n][l] for n, l in all_layer1], True, landed0, "gather_layer1_start")
    w["norm2_g"][0] = tied(row(norm2_g, 0), token)
    _fwd_ffn(st0, w, 0, n_ex)
    x_mid = _down_proj(st0, w, 0)
    land_weights(gather1, x_mid, all_layer1, "gather_layer1_wait")
    st1 = _fwd_mix(x_mid, w, 1, n_ex)
    _fwd_ffn(st1, w, 1, n_ex)
    dx, dxb, d_final_g, loss_part = _down_proj_loss(st1, w, 1, loss_target.reshape(n_ex * seq, d))

    def send_grads(g, which, after, name):
        return _exchange_start([g[key_of[n]].reshape(N_DEV, rows[n], d) for n, _ in which], False, after, name)

    dx, dxb, g_ffn1 = _bwd_ffn(st1, w, 1, dx, dxb, n_ex)
    dx, dxb, g_mix1 = _bwd_mix(st1, w, 1, dx, dxb, n_ex)
    grads1, token = send_grads({**g_ffn1, **g_mix1}, all_layer1, dx, "exchange_layer1_start")
    w["norm2_g"][0] = tied(row(norm2_g, 0), token)
    dx, dxb, g_ffn0 = _bwd_ffn(st0, w, 0, dx, dxb, n_ex)
    g_ffn0["w_out"] = _mm_tn_parts(st0["mix"], dxb, "out_proj_dw")
    ffn_layer0 = [("w_out", 0), ("w_up", 0), ("w_down", 0)]
    grads0a, token = send_grads(g_ffn0, ffn_layer0, dxb, "exchange_ffn0_start")
    dmix = _mm_nt(dxb, w["w_out"][0], "out_proj_dx", after=token)
    dx, dxb, g_mix0, dp0 = _bwd_mixers(st0, w, 0, dx, dmix, n_ex)
    grad_x = dx.reshape(n_ex, seq, d)
    g = {k: [{**g_ffn0, **g_mix0}[k], {**g_ffn1, **g_mix1}[k]] for k in g_mix0.keys() | g_ffn0.keys()}
    g["final_g"] = d_final_g

    small_local = {
        "norm1_g": jnp.stack([a[0] for a in g["norm1_g"]]), "conv_a_w": jnp.stack(g["conv_a_w"]),
        "conv_b_w": jnp.stack(g["conv_b_w"]), "conv_b_b": jnp.stack([a[0] for a in g["conv_b_b"]]),
        "ln_b_g": jnp.stack([a[0] for a in g["ln_b_g"]]), "ln_b_b": jnp.stack([a[0] for a in g["ln_b_b"]]),
        "ln_c_g": jnp.stack([a[0] for a in g["ln_c_g"]]), "ln_c_b": jnp.stack([a[0] for a in g["ln_c_b"]]),
        "sgu_w": jnp.stack(g["sgu_w"]), "sgu_b": jnp.stack([a.T for a in g["sgu_b_t"]]),
        "norm2_g": jnp.stack([a[0] for a in g["norm2_g"]]), "conv_f_w": jnp.stack(g["conv_f_w"]),
        "final_g": g["final_g"][0],
    }
    small_parts = [small_local[k] for k in _SMALL] + [loss_part.reshape(1)]
    small, token = _exchange_start([_pack_rows(small_parts)], True, dx, "gather_small_start")
    g_mix0["w_in_t"] = _mm_tn_parts(dp0, st0["h1"], "in_proj_dw", after=token)
    mix_layer0 = [("w_in", 0)]
    grads0b, token = send_grads(g_mix0, mix_layer0, dx, "exchange_mix0_start")

    reduced = {}

    def land_grads(handle, after, which, name):
        sent, landed = _exchange_wait(handle, after, name + "_wait")
        for (n, l), src, got in zip(which, sent, landed):
            own = lax.dynamic_index_in_dim(src, my, 0, keepdims=False)
            reduced[(n, l)] = _sum_slabs(got, own, name + "_sum_" + n)
        return reduced[which[-1]]

    def stacked_grad(name):
        stacked = jnp.stack([reduced[(name, l)] for l in range(depth)])
        return jnp.swapaxes(stacked, 1, 2) if name in ("w_in", "w_up") else stacked

    done = land_grads(grads1, token, all_layer1, "exchange_layer1")
    land_grads(grads0a, done, ffn_layer0, "exchange_ffn0")
    grads = {name: stacked_grad(name) for name in ("w_out", "w_up", "w_down")}

    delta, new_m, new_v = {}, {}, {}

    def as_2d(name):
        shp = weights[name].shape
        two_d = (-1, shp[-1]) if len(shp) > 1 else (1, shp[0])
        return tuple(a.reshape(two_d) for a in (weights[name], grads[name], mom1[name], mom2[name]))

    def keep(name, outs):
        delta[name], new_m[name], new_v[name] = (o.reshape(weights[name].shape) for o in outs)

    for name in ("w_up", "w_down", "w_out"):
        keep(name, _adamw(*as_2d(name), "adamw_" + name))

    (own,), (landed,) = _exchange_wait(small, new_v["w_out"], "gather_small_wait")
    small_sum = _sum_slabs(landed, own, "sum_small_grads")
    *small_totals, loss_total = _unpack_rows(small_sum, [a.shape for a in small_parts])
    loss = loss_total[0]
    for name, total in zip(_SMALL, small_totals):
        if name in _CONV_SHARDED:
            width = weights[name].shape[-1]
            total = lax.dynamic_slice_in_dim(total, my * width, width, axis=-1)
        grads[name] = total
    at_least_2d = lambda a: a[None] if a.ndim == 1 else a
    small_params = [tuple(at_least_2d(a) for a in (weights[n], grads[n], mom1[n], mom2[n])) for n in _SMALL]
    for name, outs in zip(_SMALL, _adamw_small(small_params, "adamw_small")):
        keep(name, outs)

    land_grads(grads0b, new_v["final_g"], mix_layer0, "exchange_mix0")
    grads["w_in"] = stacked_grad("w_in")
    keep("w_in", _adamw(*as_2d("w_in"), "adamw_w_in"))

    return (loss, grad_x, *[grads[n] for n in _NAMES], *[delta[n] for n in _NAMES], *[new_m[n] for n in _NAMES],
            *[new_v[n] for n in _NAMES])
```

```python
import jax
import jax.numpy as jnp
from jax import lax
from jax.experimental import pallas as pl
from jax.experimental.pallas import tpu as pltpu

_F32 = jnp.float32
_BF = jnp.bfloat16

HEAD_DIM = 64
MIX_W = 256
N_HEADS = MIX_W // HEAD_DIM
CHUNK = 128
KV_BLOCK = 128
K_SHORT = 3
K_CONF = 31
K_FFN = 3
RMS_EPS = 1e-6
LN_EPS = 1e-5
ADAM_LR = 0.001
ADAM_B1 = 0.9
ADAM_B2 = 0.999
ADAM_EPS = 1e-08
ADAM_WD = 0.01
ADAM_STEP = 10
N_DEV = 8
VMEM_LIMIT = 56 * 1024 * 1024


def _bf(x):
    return x.astype(_BF)


def _ld(ref):
    return ref[...].astype(_F32)


_ANY_SPEC = pl.BlockSpec(memory_space=pl.ANY)


def _params(*sem):
    return pltpu.CompilerParams(dimension_semantics=sem, vmem_limit_bytes=VMEM_LIMIT)


def _dot(a, b):
    return jnp.dot(a, b, preferred_element_type=_F32)


def _dot_nt(a, b):
    return lax.dot_general(a, b, (((1,), (1,)), ((), ())), preferred_element_type=_F32)


def _dot_tn(a, b):
    return lax.dot_general(a, b, (((0,), (0,)), ((), ())), preferred_element_type=_F32)


def _row_tile(t, want):
    return want if t % want == 0 else t


def _pick_tile(rows, unit, max_rows):
    best = 0
    for cand in range(unit, min(rows, max_rows) + 1, unit):
        if rows % cand == 0:
            best = cand
    return best or rows


def _sigmoid(x):
    return 1.0 / (1.0 + jnp.exp(-x))


def _rms_rstd(x):
    return lax.rsqrt(jnp.mean(x * x, axis=-1, keepdims=True) + RMS_EPS)


def _norm_mm(x, g, w_t, name):
    t, d = x.shape
    n = w_t.shape[0]
    tm = _row_tile(t, 512)
    tn = _row_tile(n, 512)

    def body(x_ref, g_ref, w_ref, p_ref, h_ref):
        xv = x_ref[...]
        h = _bf(xv * _rms_rstd(xv) * g_ref[...])
        h_ref[...] = h
        for n0 in range(0, n, tn):
            p_ref[:, n0:n0 + tn] = _bf(_dot_nt(h, w_ref[n0:n0 + tn, :]))

    return pl.pallas_call(
        body, name=name, grid=(t // tm,),
        in_specs=[pl.BlockSpec((tm, d), lambda i: (i, 0)), pl.BlockSpec((1, d), lambda i: (0, 0)),
                  pl.BlockSpec((n, d), lambda i: (0, 0))],
        out_specs=[pl.BlockSpec((tm, n), lambda i: (i, 0)), pl.BlockSpec((tm, d), lambda i: (i, 0))],
        out_shape=[jax.ShapeDtypeStruct((t, n), _BF), jax.ShapeDtypeStruct((t, d), _BF)],
        compiler_params=_params("parallel"),
    )(x, g, w_t)


def _mm_nt(a, w_t, name, after=None):
    t, k = a.shape
    n = w_t.shape[0]
    tm = _row_tile(t, 512)
    tn = _row_tile(n, 512) if n % 512 == 0 else _row_tile(n, 256)

    def body(a_ref, w_ref, *rest):
        o_ref = rest[-1]
        av = a_ref[...]
        for n0 in range(0, n, tn):
            o_ref[:, n0:n0 + tn] = _bf(_dot_nt(av, w_ref[n0:n0 + tn, :]))

    extra = () if after is None else (after,)
    return pl.pallas_call(
        body, name=name, grid=(t // tm,),
        in_specs=[pl.BlockSpec((tm, k), lambda i: (i, 0)), pl.BlockSpec((n, k), lambda i: (0, 0))] + [_ANY_SPEC] * len(extra),
        out_specs=pl.BlockSpec((tm, n), lambda i: (i, 0)),
        out_shape=jax.ShapeDtypeStruct((t, n), _BF),
        compiler_params=_params("parallel"),
    )(a, w_t, *extra)


def _mm_res(parts, w, x, name):
    t = x.shape[0]
    k, d = w.shape
    tm = _row_tile(t, 512)
    widths = [a.shape[1] for a in parts]
    n_parts = len(parts)

    def body(*refs):
        w_ref, x_ref, o_ref = refs[n_parts:]
        acc, off = x_ref[...], 0
        for a_ref, width in zip(refs[:n_parts], widths):
            acc = acc + _dot(a_ref[...], w_ref[off:off + width, :])
            off += width
        o_ref[...] = acc

    return pl.pallas_call(
        body, name=name, grid=(t // tm,),
        in_specs=[pl.BlockSpec((tm, width), lambda i: (i, 0)) for width in widths] + [
            pl.BlockSpec((k, d), lambda i: (0, 0)), pl.BlockSpec((tm, d), lambda i: (i, 0))],
        out_specs=pl.BlockSpec((tm, d), lambda i: (i, 0)),
        out_shape=jax.ShapeDtypeStruct((t, d), _F32),
        compiler_params=_params("parallel"),
    )(*parts, w, x)


def _res_norm_mm(parts, w_res, x, g, w_t, name):
    t, d = x.shape
    k = w_res.shape[0]
    n = w_t.shape[0]
    tm = _row_tile(t, 512)
    tn = _row_tile(n, 512)
    widths = [a.shape[1] for a in parts]
    n_parts = len(parts)

    def body(*refs):
        wr_ref, x_ref, g_ref, wt_ref, xo_ref, p_ref, h_ref = refs[n_parts:]
        xv, off = x_ref[...], 0
        for a_ref, width in zip(refs[:n_parts], widths):
            xv = xv + _dot(a_ref[...], wr_ref[off:off + width, :])
            off += width
        xo_ref[...] = xv
        h = _bf(xv * _rms_rstd(xv) * g_ref[...])
        h_ref[...] = h
        for n0 in range(0, n, tn):
            p_ref[:, n0:n0 + tn] = _bf(_dot_nt(h, wt_ref[n0:n0 + tn, :]))

    row = lambda width: pl.BlockSpec((tm, width), lambda i: (i, 0))
    const = lambda shape: pl.BlockSpec(shape, lambda i: (0, 0))
    return pl.pallas_call(
        body, name=name, grid=(t // tm,),
        in_specs=[row(width) for width in widths] + [const((k, d)), row(d), const((1, d)), const((n, d))],
        out_specs=[row(d), row(n), row(d)],
        out_shape=[jax.ShapeDtypeStruct((t, d), _F32), jax.ShapeDtypeStruct((t, n), _BF), jax.ShapeDtypeStruct((t, d), _BF)],
        compiler_params=_params("parallel"),
    )(*parts, w_res, x, g, w_t)


def _res_final_loss(parts, w_res, x, g, target, name):
    t, d = x.shape
    k = w_res.shape[0]
    tm = _row_tile(t, 512)
    widths = [a.shape[1] for a in parts]
    n_parts = len(parts)

    def body(*refs):
        wr_ref, x_ref, g_ref, t_ref, dx_ref, dxb_ref, dg_ref, loss_ref = refs[n_parts:]
        xv, off = x_ref[...], 0
        for a_ref, width in zip(refs[:n_parts], widths):
            xv = xv + _dot(a_ref[...], wr_ref[off:off + width, :])
            off += width
        _loss_head(xv, g_ref, t_ref, dx_ref, dxb_ref, dg_ref, loss_ref, d)

    row = lambda width: pl.BlockSpec((tm, width), lambda i: (i, 0))
    const = lambda shape: pl.BlockSpec(shape, lambda i: (0, 0))
    return pl.pallas_call(
        body, name=name, grid=(t // tm,),
        in_specs=[row(width) for width in widths] + [const((k, d)), row(d), const((1, d)), row(d)],
        out_specs=[row(d), row(d), const((1, d)), const((1, 1))],
        out_shape=[jax.ShapeDtypeStruct((t, d), _F32), jax.ShapeDtypeStruct((t, d), _BF),
                   jax.ShapeDtypeStruct((1, d), _F32), jax.ShapeDtypeStruct((1, 1), _F32)],
        compiler_params=_params("arbitrary"),
    )(*parts, w_res, x, g, target)


def _loss_head(xv, g_ref, t_ref, dx_ref, dxb_ref, dg_ref, loss_ref, d):
    rstd = _rms_rstd(xv)
    xn = xv * rstd
    err = xn * g_ref[...] - t_ref[...]
    dy = err * (1.0 / d)
    u = dy * g_ref[...]
    dx = rstd * (u - xn * jnp.mean(u * xn, axis=-1, keepdims=True))
    dx_ref[...] = dx
    dxb_ref[...] = _bf(dx)

    @pl.when(pl.program_id(0) == 0)
    def _():
        dg_ref[...] = jnp.zeros_like(dg_ref)
        loss_ref[...] = jnp.zeros_like(loss_ref)

    dg_ref[...] += jnp.sum(dy * xn, axis=0, keepdims=True)
    loss_ref[...] += (0.5 / d) * jnp.sum(jnp.sum(err * err, axis=1, keepdims=True), axis=0, keepdims=True)


def _mm_normbwd(parts, w, x, g, dres, name):
    t = x.shape[0]
    k, d = w.shape
    tm = _row_tile(t, 512)
    widths = [a.shape[1] for a in parts]
    n_parts = len(parts)

    def body(*refs):
        a_refs = refs[:n_parts]
        w_ref, x_ref, g_ref, r_ref, dx_ref, dxb_ref, dg_ref = refs[n_parts:]
        dh, off = None, 0
        for a_ref, width in zip(a_refs, widths):
            term = _dot(_bf(a_ref[...]), w_ref[off:off + width, :])
            dh = term if dh is None else dh + term
            off += width
        xv = x_ref[...]
        rstd = _rms_rstd(xv)
        xn = xv * rstd
        u = dh * g_ref[...]
        dx = r_ref[...] + rstd * (u - xn * jnp.mean(u * xn, axis=-1, keepdims=True))
        dx_ref[...] = dx
        dxb_ref[...] = _bf(dx)

        @pl.when(pl.program_id(0) == 0)
        def _():
            dg_ref[...] = jnp.zeros_like(dg_ref)

        dg_ref[...] += jnp.sum(dh * xn, axis=0, keepdims=True)

    return pl.pallas_call(
        body, name=name, grid=(t // tm,),
        in_specs=[pl.BlockSpec((tm, width), lambda i: (i, 0)) for width in widths] + [
            pl.BlockSpec((k, d), lambda i: (0, 0)),
            pl.BlockSpec((tm, d), lambda i: (i, 0)), pl.BlockSpec((1, d), lambda i: (0, 0)),
            pl.BlockSpec((tm, d), lambda i: (i, 0))],
        out_specs=[pl.BlockSpec((tm, d), lambda i: (i, 0)), pl.BlockSpec((tm, d), lambda i: (i, 0)),
                   pl.BlockSpec((1, d), lambda i: (0, 0))],
        out_shape=[jax.ShapeDtypeStruct((t, d), _F32), jax.ShapeDtypeStruct((t, d), _BF),
                   jax.ShapeDtypeStruct((1, d), _F32)],
        compiler_params=_params("arbitrary"),
    )(*parts, w, x, g, dres)


def _mm_tn(a, b, name, out_dtype):
    t, m = a.shape
    n = b.shape[1]
    tm = _pick_tile(m, 128, 1408)
    tn = _pick_tile(n, 128, 1024)
    tk = _row_tile(t, 1024)
    nk = t // tk

    def body(a_ref, b_ref, o_ref, acc):
        kk = pl.program_id(2)

        @pl.when(kk == 0)
        def _():
            acc[...] = jnp.zeros_like(acc)

        acc[...] += _dot_tn(_bf(a_ref[...]), b_ref[...])

        @pl.when(kk == nk - 1)
        def _():
            o_ref[...] = acc[...].astype(o_ref.dtype)

    return pl.pallas_call(
        body, name=name, grid=(m // tm, n // tn, nk),
        in_specs=[pl.BlockSpec((tk, tm), lambda i, j, kk: (kk, i)), pl.BlockSpec((tk, tn), lambda i, j, kk: (kk, j))],
        out_specs=pl.BlockSpec((tm, tn), lambda i, j, kk: (i, j)),
        out_shape=jax.ShapeDtypeStruct((m, n), out_dtype),
        scratch_shapes=[pltpu.VMEM((tm, tn), _F32)],
        compiler_params=_params("parallel", "parallel", "arbitrary"),
    )(a, b)


def _mm_tn_halves(a0, a1, b, name):
    t, m = a0.shape
    n = b.shape[1]
    tm = _pick_tile(m, 128, 1408)
    tn = _pick_tile(n, 128, 1024)
    tk = _row_tile(t, 1024)
    nk = t // tk
    half = m // tm

    def body(a0_ref, a1_ref, b_ref, o_ref, acc):
        i = pl.program_id(0)
        kk = pl.program_id(2)

        @pl.when(kk == 0)
        def _():
            acc[...] = jnp.zeros_like(acc)

        @pl.when(i < half)
        def _():
            acc[...] += _dot_tn(a0_ref[...], b_ref[...])

        @pl.when(i >= half)
        def _():
            acc[...] += _dot_tn(a1_ref[...], b_ref[...])

        @pl.when(kk == nk - 1)
        def _():
            o_ref[...] = _bf(acc[...])

    return pl.pallas_call(
        body, name=name, grid=(2 * half, n // tn, nk),
        in_specs=[pl.BlockSpec((tk, tm), lambda i, j, kk: (jnp.where(i < half, kk, 0), jnp.minimum(i, half - 1))),
                  pl.BlockSpec((tk, tm), lambda i, j, kk: (jnp.where(i >= half, kk, 0), jnp.maximum(i - half, 0))),
                  pl.BlockSpec((tk, tn), lambda i, j, kk: (kk, j))],
        out_specs=pl.BlockSpec((tm, tn), lambda i, j, kk: (i, j)),
        out_shape=jax.ShapeDtypeStruct((2 * m, n), _BF),
        scratch_shapes=[pltpu.VMEM((tm, tn), _F32)],
        compiler_params=_params("parallel", "parallel", "arbitrary"),
    )(a0, a1, b)


def _mm_tn_parts(parts, b, name, after=None):
    t, n = b.shape
    widths = [a.shape[1] for a in parts]
    m = sum(widths)
    n_parts = len(parts)
    tk = _row_tile(t, 1024)
    nk = t // tk
    extra = () if after is None else (after,)

    def body(*refs):
        b_ref = refs[n_parts]
        o_ref, acc = refs[-2:]
        kk = pl.program_id(0)

        @pl.when(kk == 0)
        def _():
            acc[...] = jnp.zeros_like(acc)

        bv = b_ref[...]
        off = 0
        for a_ref, width in zip(refs[:n_parts], widths):
            acc[off:off + width, :] += _dot_tn(_bf(a_ref[...]), bv)
            off += width

        @pl.when(kk == nk - 1)
        def _():
            o_ref[...] = _bf(acc[...])

    return pl.pallas_call(
        body, name=name, grid=(nk,),
        in_specs=[pl.BlockSpec((tk, width), lambda kk: (kk, 0)) for width in widths] + [pl.BlockSpec((tk, n), lambda kk: (kk, 0))]
        + [_ANY_SPEC] * len(extra),
        out_specs=pl.BlockSpec((m, n), lambda kk: (0, 0)),
        out_shape=jax.ShapeDtypeStruct((m, n), _BF),
        scratch_shapes=[pltpu.VMEM((m, n), _F32)],
        compiler_params=_params("arbitrary"),
    )(*parts, b, *extra)


def _pad_rows(x, pad):
    return jnp.concatenate([x, jnp.zeros((pad, x.shape[1]), x.dtype)], axis=0)


def _shift_down(xp, s):
    return xp if s == 0 else pltpu.roll(xp, s, 0)


def _shift_up(xp, s):
    return xp if s == 0 else pltpu.roll(xp, xp.shape[0] - s, 0)


def _taps3(xp):
    one = _shift_down(xp, 1)
    return xp, one, _shift_down(one, 1)


def _conv3_taps(taps, w_ref):
    return w_ref[2:3, :] * taps[0] + w_ref[1:2, :] * taps[1] + w_ref[0:1, :] * taps[2]


def _conv3(xp, w_ref):
    return _conv3_taps(_taps3(xp), w_ref)


def _conv3_t(dyp, w_ref):
    one = _shift_up(dyp, 1)
    return w_ref[2:3, :] * dyp + w_ref[1:2, :] * one + w_ref[0:1, :] * _shift_up(one, 1)


def _conv3_dw(dyp, taps):
    return [jnp.sum(dyp * taps[2 - k], axis=0, keepdims=True) for k in range(3)]


def _examples_per_step(n_ex):
    return 4 if n_ex % 4 == 0 else 2 if n_ex % 2 == 0 else 1


def _ffn_mid_fwd(up_pre, wf, n_ex, name):
    t, f2 = up_pre.shape
    f = f2 // 2
    s = t // n_ex
    cb = MIX_W
    nb = f // cb

    per = _examples_per_step(n_ex)

    def body(ug_ref, uv_ref, wg_ref, wv_ref, act_ref, gf_ref, vf_ref):
        for ex in range(per):
            rows = slice(ex * s, (ex + 1) * s)
            gf = _conv3(_pad_rows(ug_ref[rows, :].astype(_F32), 8), wg_ref)[:s]
            vf = _conv3(_pad_rows(uv_ref[rows, :].astype(_F32), 8), wv_ref)[:s]
            act_ref[rows, :] = _bf(gf * _sigmoid(gf) * vf)
            gf_ref[rows, :] = _bf(gf)
            vf_ref[rows, :] = _bf(vf)

    out = pl.BlockSpec((per * s, cb), lambda e, j: (e, j))
    return pl.pallas_call(
        body, name=name, grid=(n_ex // per, nb),
        in_specs=[pl.BlockSpec((per * s, cb), lambda e, j: (e, j)), pl.BlockSpec((per * s, cb), lambda e, j: (e, j + nb)),
                  pl.BlockSpec((K_FFN, cb), lambda e, j: (0, j)), pl.BlockSpec((K_FFN, cb), lambda e, j: (0, j + nb))],
        out_specs=[out, out, out],
        out_shape=[jax.ShapeDtypeStruct((t, f), _BF)] * 3,
        compiler_params=_params("parallel", "parallel"),
    )(up_pre, up_pre, wf, wf)


def _ffn_mid_bwd(up_pre, conv_g, conv_v, wf, dact, n_ex, name):
    t, f2 = up_pre.shape
    f = f2 // 2
    s = t // n_ex
    cb = MIX_W
    nb = f // cb
    per = 1

    def body(ug_ref, uv_ref, gf_ref, vf_ref, wg_ref, wv_ref, da_ref, dug_ref, duv_ref, dwg_ref, dwv_ref):
        @pl.when(pl.program_id(1) == 0)
        def _():
            dwg_ref[...] = jnp.zeros_like(dwg_ref)
            dwv_ref[...] = jnp.zeros_like(dwv_ref)

        def finish(dpost, w_ref, x_ref, du_ref, dw_ref, rows):
            ahead = [_pad_rows(dpost, 8)]
            ahead.append(_shift_up(ahead[0], 1))
            ahead.append(_shift_up(ahead[1], 1))
            du_ref[rows, :] = _bf((w_ref[2:3, :] * ahead[0] + w_ref[1:2, :] * ahead[1] + w_ref[0:1, :] * ahead[2])[:s])
            x = x_ref[rows, :].astype(_F32)
            for k in range(K_FFN):
                dw_ref[k:k + 1, :] += jnp.sum(ahead[2 - k][:s] * x, axis=0, keepdims=True)

        for ex in range(per):
            rows = slice(ex * s, (ex + 1) * s)
            gf = gf_ref[rows, :].astype(_F32)
            vf = vf_ref[rows, :].astype(_F32)
            sg = _sigmoid(gf)
            da = da_ref[rows, :].astype(_F32)
            finish(da * vf * sg * (1.0 + gf * (1.0 - sg)), wg_ref, ug_ref, dug_ref, dwg_ref, rows)
            finish(da * gf * sg, wv_ref, uv_ref, duv_ref, dwv_ref, rows)

    blk = per * s
    return pl.pallas_call(
        body, name=name, grid=(nb, n_ex // per),
        in_specs=[pl.BlockSpec((blk, cb), lambda j, e: (e, j)), pl.BlockSpec((blk, cb), lambda j, e: (e, j + nb)),
                  pl.BlockSpec((blk, cb), lambda j, e: (e, j)), pl.BlockSpec((blk, cb), lambda j, e: (e, j)),
                  pl.BlockSpec((K_FFN, cb), lambda j, e: (0, j)), pl.BlockSpec((K_FFN, cb), lambda j, e: (0, j + nb)),
                  pl.BlockSpec((blk, cb), lambda j, e: (e, j))],
        out_specs=[pl.BlockSpec((blk, cb), lambda j, e: (e, j)), pl.BlockSpec((blk, cb), lambda j, e: (e, j)),
                   pl.BlockSpec((K_FFN, cb), lambda j, e: (0, j)), pl.BlockSpec((K_FFN, cb), lambda j, e: (0, j))],
        out_shape=[jax.ShapeDtypeStruct((t, f), _BF), jax.ShapeDtypeStruct((t, f), _BF),
                   jax.ShapeDtypeStruct((K_FFN, f), _F32), jax.ShapeDtypeStruct((K_FFN, f), _F32)],
        compiler_params=_params("parallel", "arbitrary"),
    )(up_pre, up_pre, conv_g, conv_v, wf, wf, dact)


def _pcol(s, j):
    return pl.BlockSpec((s, MIX_W), lambda e, j=j: (e, j))


def _vec(rows=1):
    return pl.BlockSpec((rows, MIX_W), lambda e: (0, 0))


def _mix_a_fwd(p, wa, n_ex, name):
    t = p.shape[0]
    s = t // n_ex

    def body(gb_ref, gc_ref, ha_ref, w_ref, y_ref):
        cv = _conv3(_pad_rows(_ld(gc_ref) * _ld(ha_ref), 8), w_ref)[:s]
        y_ref[...] = _bf(_ld(gb_ref) * cv)

    return pl.pallas_call(
        body, name=name, grid=(n_ex,),
        in_specs=[_pcol(s, 0), _pcol(s, 1), _pcol(s, 2), _vec(K_SHORT)],
        out_specs=pl.BlockSpec((s, MIX_W), lambda e: (e, 0)),
        out_shape=jax.ShapeDtypeStruct((t, MIX_W), _BF),
        compiler_params=_params("parallel"),
    )(p, p, p, wa)


def _mix_a_bwd(p, wa, dmix, n_ex, name):
    t = p.shape[0]
    s = t // n_ex

    def body(gb_ref, gc_ref, ha_ref, w_ref, dy_ref, dp_ref, dw_ref):
        gc = _ld(gc_ref)
        ha = _ld(ha_ref)
        up = _taps3(_pad_rows(gc * ha, 8))
        cv = _conv3_taps(up, w_ref)[:s]
        dy = _ld(dy_ref)
        dcvp = _pad_rows(dy * _ld(gb_ref), 8)
        du = _conv3_t(dcvp, w_ref)[:s]
        dp_ref[:, 0:MIX_W] = _bf(dy * cv)
        dp_ref[:, MIX_W:2 * MIX_W] = _bf(du * ha)
        dp_ref[:, 2 * MIX_W:3 * MIX_W] = _bf(du * gc)

        @pl.when(pl.program_id(0) == 0)
        def _():
            dw_ref[...] = jnp.zeros_like(dw_ref)

        rows = _conv3_dw(dcvp, up)
        for k in range(3):
            dw_ref[k:k + 1, :] += rows[k]

    return pl.pallas_call(
        body, name=name, grid=(n_ex,),
        in_specs=[_pcol(s, 0), _pcol(s, 1), _pcol(s, 2), _vec(K_SHORT), _pcol(s, 0)],
        out_specs=[pl.BlockSpec((s, 3 * MIX_W), lambda e: (e, 0)), _vec(K_SHORT)],
        out_shape=[jax.ShapeDtypeStruct((t, 3 * MIX_W), _BF), jax.ShapeDtypeStruct((K_SHORT, MIX_W), _F32)],
        compiler_params=_params("arbitrary"),
    )(p, p, p, wa, dmix)


CONF_PAD = 32
CONF_ROWS = 64
_CONF_LANES = (slice(0, 128), slice(128, 256))


def _conf_taps(win, ahead):
    n = CONF_ROWS + CONF_PAD
    for b in range(8):
        rot = win if b == 0 else pltpu.roll(win, (n - b) if ahead else b, 0)
        for a in range(4):
            if 8 * a + b < K_CONF:
                yield rot, 8 * a + b, (8 * a) if ahead else (CONF_PAD - 8 * a)


def _ln_fwd(x, g, b):
    mu = jnp.mean(x, axis=-1, keepdims=True)
    xc = x - mu
    rstd = lax.rsqrt(jnp.mean(xc * xc, axis=-1, keepdims=True) + LN_EPS)
    xhat = xc * rstd
    return xhat * g + b, xhat, rstd


def _ln_bwd(dy, xhat, rstd, g):
    dxh = dy * g
    return rstd * (dxh - jnp.mean(dxh, axis=-1, keepdims=True) - xhat * jnp.mean(dxh * xhat, axis=-1, keepdims=True))


def _mix_b_fwd(p, wb, bb, lg, lb, n_ex, name):
    t = p.shape[0]
    s = t // n_ex

    def body(val_ref, gat_ref, w_ref, bb_ref, lg_ref, lb_ref, y_ref, cb_ref, xpad):
        xpad[0:CONF_PAD, :] = jnp.zeros((CONF_PAD, MIX_W), _F32)
        xpad[CONF_PAD:, :] = _ld(val_ref) * _sigmoid(_ld(gat_ref))

        def chunk(c, carry):
            r0 = pl.multiple_of(c * CONF_ROWS, CONF_ROWS)
            for lanes in _CONF_LANES:
                acc = None
                for rot, sh, lo in _conf_taps(xpad[pl.ds(r0, CONF_ROWS + CONF_PAD), lanes], False):
                    term = w_ref[K_CONF - 1 - sh:K_CONF - sh, lanes] * rot[lo:lo + CONF_ROWS]
                    acc = term if acc is None else acc + term
                cb_ref[pl.ds(r0, CONF_ROWS), lanes] = acc + bb_ref[:, lanes]
            return carry

        lax.fori_loop(0, s // CONF_ROWS, chunk, 0)
        yl, _, _ = _ln_fwd(cb_ref[...], lg_ref[...], lb_ref[...])
        y_ref[...] = _bf(yl * _sigmoid(yl))

    return pl.pallas_call(
        body, name=name, grid=(n_ex,),
        in_specs=[_pcol(s, 3), _pcol(s, 4), _vec(K_CONF), _vec(), _vec(), _vec()],
        out_specs=[pl.BlockSpec((s, MIX_W), lambda e: (e, 0)), pl.BlockSpec((s, MIX_W), lambda e: (e, 0))],
        out_shape=[jax.ShapeDtypeStruct((t, MIX_W), _BF), jax.ShapeDtypeStruct((t, MIX_W), _F32)],
        scratch_shapes=[pltpu.VMEM((CONF_PAD + s, MIX_W), _F32)],
        compiler_params=_params("parallel"),
    )(p, p, wb, bb, lg, lb)


def _mix_b_bwd(p, cb, wb, lg, lb, dmix, n_ex, name):
    t = p.shape[0]
    s = t // n_ex

    def body(val_ref, gat_ref, cb_ref, w_ref, lg_ref, lb_ref, dy_ref, dp_ref, dw_ref, dbb_ref, dlg_ref, dlb_ref,
             xpad, dpad, dglu_s, dw_acc):
        @pl.when(pl.program_id(0) == 0)
        def _():
            for r in (dw_ref, dbb_ref, dlg_ref, dlb_ref):
                r[...] = jnp.zeros_like(r)

        yl, xhat, rstd = _ln_fwd(cb_ref[...], lg_ref[...], lb_ref[...])
        sy = _sigmoid(yl)
        dyl = _ld(dy_ref) * sy * (1.0 + yl * (1.0 - sy))
        dlg_ref[...] += jnp.sum(dyl * xhat, axis=0, keepdims=True)
        dlb_ref[...] += jnp.sum(dyl, axis=0, keepdims=True)
        dcb = _ln_bwd(dyl, xhat, rstd, lg_ref[...])
        dbb_ref[...] += jnp.sum(dcb, axis=0, keepdims=True)

        val = _ld(val_ref)
        sg = _sigmoid(_ld(gat_ref))
        xpad[0:CONF_PAD, :] = jnp.zeros((CONF_PAD, MIX_W), _F32)
        xpad[CONF_PAD:, :] = val * sg
        dpad[0:s, :] = dcb
        dpad[s:, :] = jnp.zeros((CONF_PAD, MIX_W), _F32)
        dw_acc[...] = jnp.zeros_like(dw_acc)

        def chunk(c, carry):
            r0 = pl.multiple_of(c * CONF_ROWS, CONF_ROWS)
            for lanes in _CONF_LANES:
                d_win = dpad[pl.ds(r0, CONF_ROWS + CONF_PAD), lanes]
                d_rows = d_win[0:CONF_ROWS]
                acc = None
                for rot, sh, lo in _conf_taps(d_win, True):
                    term = w_ref[K_CONF - 1 - sh:K_CONF - sh, lanes] * rot[lo:lo + CONF_ROWS]
                    acc = term if acc is None else acc + term
                dglu_s[pl.ds(r0, CONF_ROWS), lanes] = acc
                for rot, sh, lo in _conf_taps(xpad[pl.ds(r0, CONF_ROWS + CONF_PAD), lanes], False):
                    prod = d_rows * rot[lo:lo + CONF_ROWS]
                    dw_acc[K_CONF - 1 - sh, :, lanes] += jnp.sum(prod.reshape(CONF_ROWS // 8, 8, 128), axis=0)
            return carry

        lax.fori_loop(0, s // CONF_ROWS, chunk, 0)
        dw_ref[...] += jnp.sum(dw_acc[...], axis=1)
        dglu = dglu_s[...]
        dp_ref[:, 0:MIX_W] = _bf(dglu * sg)
        dp_ref[:, MIX_W:2 * MIX_W] = _bf(dglu * val * sg * (1.0 - sg))

    return pl.pallas_call(
        body, name=name, grid=(n_ex,),
        in_specs=[_pcol(s, 3), _pcol(s, 4), pl.BlockSpec((s, MIX_W), lambda e: (e, 0)), _vec(K_CONF), _vec(), _vec(),
                  _pcol(s, 1)],
        out_specs=[pl.BlockSpec((s, 2 * MIX_W), lambda e: (e, 0)), _vec(K_CONF), _vec(), _vec(), _vec()],
        out_shape=[jax.ShapeDtypeStruct((t, 2 * MIX_W), _BF), jax.ShapeDtypeStruct((K_CONF, MIX_W), _F32),
                   jax.ShapeDtypeStruct((1, MIX_W), _F32), jax.ShapeDtypeStruct((1, MIX_W), _F32),
                   jax.ShapeDtypeStruct((1, MIX_W), _F32)],
        scratch_shapes=[pltpu.VMEM((CONF_PAD + s, MIX_W), _F32), pltpu.VMEM((s + CONF_PAD, MIX_W), _F32),
                        pltpu.VMEM((s, MIX_W), _F32), pltpu.VMEM((K_CONF, 8, MIX_W), _F32)],
        compiler_params=_params("arbitrary"),
    )(p, p, cb, wb, lg, lb, dmix)


_INV_SQRT2 = 0.7071067811865476
_INV_SQRT2PI = 0.3989422804014327


def _gelu(x):
    return 0.5 * x * (1.0 + lax.erf(x * _INV_SQRT2))


def _gelu_grad(x):
    return 0.5 * (1.0 + lax.erf(x * _INV_SQRT2)) + x * _INV_SQRT2PI * jnp.exp(-0.5 * x * x)


def _head_masks(width=MIX_W):
    lane = lax.broadcasted_iota(jnp.int32, (1, width), 1)
    return [(lane >= h * HEAD_DIM) & (lane < (h + 1) * HEAD_DIM) for h in range(N_HEADS)]


def _tril_mask():
    r = lax.broadcasted_iota(jnp.int32, (CHUNK, CHUNK), 0)
    c = lax.broadcasted_iota(jnp.int32, (CHUNK, CHUNK), 1)
    return c <= r


def _sgu_apply(ws_ref, x3, transpose):
    n = x3.shape[0]
    tril = _tril_mask()
    masks = _head_masks()
    xb = _bf(x3)
    out = jnp.zeros(x3.shape, _F32)
    for h in range(N_HEADS):
        w = _bf(jnp.where(tril, ws_ref[h], 0.0))
        wb = jnp.broadcast_to(w[None], (n, CHUNK, CHUNK))
        dims = (((1,), (1,)), ((0,), (0,))) if transpose else (((2,), (1,)), ((0,), (0,)))
        r = lax.dot_general(wb, xb, dims, preferred_element_type=_F32)
        out = out + jnp.where(masks[h][None], r, 0.0)
    return out


def _mix_c_fwd(p, lg, lb, ws, sb_full, n_ex, name):
    t = p.shape[0]
    s = t // n_ex
    nc = s // CHUNK

    def body(pu_ref, pv_ref, lg_ref, lb_ref, ws_ref, sb_ref, y_ref):
        u = _gelu(_ld(pu_ref))
        vl, _, _ = _ln_fwd(_gelu(_ld(pv_ref)), lg_ref[...], lb_ref[...])
        sp = _sgu_apply(ws_ref, vl.reshape(nc, CHUNK, MIX_W), False) + sb_ref[...][None]
        y_ref[...] = _bf(u * sp.reshape(s, MIX_W))

    return pl.pallas_call(
        body, name=name, grid=(n_ex,),
        in_specs=[_pcol(s, 5), _pcol(s, 6), _vec(), _vec(),
                  pl.BlockSpec((N_HEADS, CHUNK, CHUNK), lambda e: (0, 0, 0)), pl.BlockSpec((CHUNK, MIX_W), lambda e: (0, 0))],
        out_specs=pl.BlockSpec((s, MIX_W), lambda e: (e, 0)),
        out_shape=jax.ShapeDtypeStruct((t, MIX_W), _BF),
        compiler_params=_params("parallel"),
    )(p, p, lg, lb, ws, sb_full)


def _mix_c_bwd(p, lg, lb, ws, sb_full, dmix, n_ex, name):
    t = p.shape[0]
    s = t // n_ex
    nc = s // CHUNK

    def body(pu_ref, pv_ref, lg_ref, lb_ref, ws_ref, sb_ref, dy_ref, dp_ref, dlg_ref, dlb_ref, dws_ref, dsb_ref):
        @pl.when(pl.program_id(0) == 0)
        def _():
            for r in (dlg_ref, dlb_ref, dws_ref, dsb_ref):
                r[...] = jnp.zeros_like(r)

        pu = _ld(pu_ref)
        pv = _ld(pv_ref)
        u = _gelu(pu)
        vl, xhat, rstd = _ln_fwd(_gelu(pv), lg_ref[...], lb_ref[...])
        vl3 = vl.reshape(nc, CHUNK, MIX_W)
        sp = _sgu_apply(ws_ref, vl3, False) + sb_ref[...][None]
        dy = _ld(dy_ref)
        dp_ref[:, 0:MIX_W] = _bf(dy * sp.reshape(s, MIX_W) * _gelu_grad(pu))
        dsp3 = (dy * u).reshape(nc, CHUNK, MIX_W)
        dsb_full = jnp.sum(dsp3, axis=0)
        masks = _head_masks()
        tril = _tril_mask()
        dspb = _bf(dsp3)
        vlb = _bf(vl3)
        for h in range(N_HEADS):
            dsb_ref[:, h:h + 1] += jnp.sum(jnp.where(masks[h], dsb_full, 0.0), axis=1, keepdims=True)
            dm = jnp.where(masks[h][None], dspb, jnp.zeros_like(dspb))
            g3 = lax.dot_general(dm, vlb, (((2,), (2,)), ((0,), (0,))), preferred_element_type=_F32)
            dws_ref[h] += jnp.where(tril, jnp.sum(g3, axis=0), 0.0)
        dvl = _sgu_apply(ws_ref, dsp3, True).reshape(s, MIX_W)
        dlg_ref[...] += jnp.sum(dvl * xhat, axis=0, keepdims=True)
        dlb_ref[...] += jnp.sum(dvl, axis=0, keepdims=True)
        dp_ref[:, MIX_W:2 * MIX_W] = _bf(_ln_bwd(dvl, xhat, rstd, lg_ref[...]) * _gelu_grad(pv))

    return pl.pallas_call(
        body, name=name, grid=(n_ex,),
        in_specs=[_pcol(s, 5), _pcol(s, 6), _vec(), _vec(),
                  pl.BlockSpec((N_HEADS, CHUNK, CHUNK), lambda e: (0, 0, 0)), pl.BlockSpec((CHUNK, MIX_W), lambda e: (0, 0)),
                  _pcol(s, 2)],
        out_specs=[pl.BlockSpec((s, 2 * MIX_W), lambda e: (e, 0)), _vec(), _vec(),
                   pl.BlockSpec((N_HEADS, CHUNK, CHUNK), lambda e: (0, 0, 0)), pl.BlockSpec((CHUNK, N_HEADS), lambda e: (0, 0))],
        out_shape=[jax.ShapeDtypeStruct((t, 2 * MIX_W), _BF), jax.ShapeDtypeStruct((1, MIX_W), _F32),
                   jax.ShapeDtypeStruct((1, MIX_W), _F32), jax.ShapeDtypeStruct((N_HEADS, CHUNK, CHUNK), _F32),
                   jax.ShapeDtypeStruct((CHUNK, N_HEADS), _F32)],
        compiler_params=_params("arbitrary"),
    )(p, p, lg, lb, ws, sb_full, dmix)


D_QBLOCK = 512
HEAD_COLS = N_HEADS * KV_BLOCK


def _stack_heads(x3):
    return jnp.stack([_bf(jnp.where(m[None], x3, 0.0)) for m in _head_masks()], axis=1)


def _stack_heads_rows(x):
    return jnp.concatenate([_bf(jnp.where(m, x, 0.0)) for m in _head_masks()], axis=0)


def _cols_to_rows(x):
    return jnp.concatenate([x[:, h * KV_BLOCK:(h + 1) * KV_BLOCK] for h in range(N_HEADS)], axis=0)


def _head_sums(x):
    return [jnp.sum(x[:, h * KV_BLOCK:(h + 1) * KV_BLOCK], axis=1, keepdims=True) for h in range(N_HEADS)]


def _spread(cols):
    tq = cols[0].shape[0]
    return jnp.concatenate([jnp.broadcast_to(c, (tq, KV_BLOCK)) for c in cols], axis=1)


def _pair_dot(x, m2):
    half = 2 * KV_BLOCK
    xb = _bf(x)
    return jnp.concatenate([_dot(xb[:, :half], m2), _dot(xb[:, half:], m2)], axis=1)


def _tri2(lower):
    n = 2 * KV_BLOCK
    r = lax.broadcasted_iota(jnp.int32, (n, n), 0)
    c = lax.broadcasted_iota(jnp.int32, (n, n), 1)
    same = (r >= KV_BLOCK) == (c >= KV_BLOCK)
    return _bf(jnp.where(same & (r > c if lower else r < c), 1.0, 0.0))


def _sb_scores(qs, kc, j, t_idx, on_diagonal):
    z = _dot_nt(qs, kc)
    lb = jnp.minimum(z, 0.0) - jnp.log(1.0 + jnp.exp(-jnp.abs(z)))
    if not on_diagonal:
        return (lambda x: x), lb, lb - z
    lane = lax.broadcasted_iota(jnp.int32, (1, HEAD_COLS), 1)
    valid = (j * KV_BLOCK + (lane & (KV_BLOCK - 1))) < t_idx
    keep = lambda x: jnp.where(valid, x, 0.0)
    return keep, lb, keep(lb - z)


RUN_LANES = 128


def _run_lane(j, h):
    return lax.broadcasted_iota(jnp.int32, (1, RUN_LANES), 1) == j * N_HEADS + h


def _d_qblock(s):
    return D_QBLOCK if s % D_QBLOCK == 0 else KV_BLOCK


def _mix_d_fwd(p, n_ex, name):
    t = p.shape[0]
    s = t // n_ex
    tq = _d_qblock(s)
    nq = s // tq
    r = tq // KV_BLOCK
    nb = s // KV_BLOCK
    assert nb * N_HEADS <= RUN_LANES

    def body(q_ref, k_ref, v_ref, y_ref, runs_ref, kc, vc):
        i = pl.program_id(1)

        @pl.when(i == 0)
        def _():
            kc[...] = _stack_heads(k_ref[...].reshape(nb, KV_BLOCK, MIX_W))
            vc[...] = _stack_heads(v_ref[...].reshape(nb, KV_BLOCK, MIX_W))

        qs = _bf(_ld(q_ref) * (HEAD_DIM ** -0.5))
        t_idx = i * tq + lax.broadcasted_iota(jnp.int32, (tq, 1), 0)
        after_m = _tri2(True)
        nkb = (i + 1) * r

        runs_ref[...] = jnp.zeros_like(runs_ref)

        def one_block(j, runs, acc, on_diagonal):
            keep, lb, c = _sb_scores(qs, kc[j].reshape(HEAD_COLS, MIX_W), j, t_idx, on_diagonal)
            a = keep(jnp.exp(lb + _pair_dot(c, after_m) + _spread(runs)))
            acc = acc + _dot(_bf(a), vc[j].reshape(HEAD_COLS, MIX_W))
            kept = runs_ref[...]
            for h in range(N_HEADS):
                kept = jnp.where(_run_lane(j, h), runs[h], kept)
            runs_ref[...] = kept
            return tuple(ru + cs for ru, cs in zip(runs, _head_sums(c))), acc

        def trip(last, carry, on_diagonal):
            runs, acc = carry
            for sub in range(r):
                runs, acc = one_block(last - sub, runs, acc, on_diagonal)
            return runs, acc

        zero = jnp.zeros((tq, 1), _F32)
        carry = trip(nkb - 1, ((zero,) * N_HEADS, jnp.zeros((tq, MIX_W), _F32)), True)
        below = lambda m: nkb - 1 - (m + 1) * r
        carry = lax.fori_loop(0, i // 2, lambda m, carry: trip(below(2 * m + 1), trip(below(2 * m), carry, False), False), carry)
        _, acc = lax.fori_loop(0, i % 2, lambda m, carry: trip(below(i - 1), carry, False), carry)
        y_ref[...] = _bf(acc)

    return pl.pallas_call(
        body, name=name, grid=(n_ex, nq),
        in_specs=[pl.BlockSpec((tq, MIX_W), lambda e, i: (e * nq + i, 7)), pl.BlockSpec((s, MIX_W), lambda e, i: (e, 8)),
                  pl.BlockSpec((s, MIX_W), lambda e, i: (e, 9))],
        out_specs=[pl.BlockSpec((tq, MIX_W), lambda e, i: (e * nq + i, 0)),
                   pl.BlockSpec((tq, RUN_LANES), lambda e, i: (e * nq + i, 0))],
        out_shape=[jax.ShapeDtypeStruct((t, MIX_W), _BF), jax.ShapeDtypeStruct((t, RUN_LANES), _F32)],
        scratch_shapes=[pltpu.VMEM((nb, N_HEADS, KV_BLOCK, MIX_W), _BF), pltpu.VMEM((nb, N_HEADS, KV_BLOCK, MIX_W), _BF)],
        compiler_params=_params("parallel", "arbitrary"),
    )(p, p, p)


def _mix_d_bwd(p, kept_runs, dmix, n_ex, name):
    t = p.shape[0]
    s = t // n_ex
    tq = _d_qblock(s)
    nq = s // tq
    r = tq // KV_BLOCK
    nb = s // KV_BLOCK
    scale = HEAD_DIM ** -0.5

    def body(q_ref, k_ref, v_ref, runs_ref, do_ref, dq_ref, dk_ref, dv_ref, kc, vc):
        i = pl.program_id(1)

        @pl.when(i == 0)
        def _():
            kc[...] = _stack_heads(k_ref[...].reshape(nb, KV_BLOCK, MIX_W))
            vc[...] = _stack_heads(v_ref[...].reshape(nb, KV_BLOCK, MIX_W))
            dk_ref[...] = jnp.zeros_like(dk_ref)
            dv_ref[...] = jnp.zeros_like(dv_ref)

        q_scaled = _ld(q_ref) * scale
        qs = _bf(q_scaled)
        do = do_ref[...]
        dob = _bf(do)
        q_rows = _stack_heads_rows(q_scaled)
        do_rows = _stack_heads_rows(do)
        kept = runs_ref[...]
        t_idx = i * tq + lax.broadcasted_iota(jnp.int32, (tq, 1), 0)
        after_m = _tri2(True)
        before_m = _tri2(False)
        nkb = (i + 1) * r
        zero = jnp.zeros((tq, 1), _F32)

        def trip(first, carry, on_diagonal):
            for sub in range(r):
                carry = one_block(first + sub, carry, on_diagonal)
            return carry

        def one_block(j, carry, on_diagonal):
            pres, dq = carry
            rows = pl.ds(pl.multiple_of(j * KV_BLOCK, KV_BLOCK), KV_BLOCK)
            kj = kc[j].reshape(HEAD_COLS, MIX_W)
            keep, lb, c = _sb_scores(qs, kj, j, t_idx, on_diagonal)
            runs = [jnp.sum(jnp.where(_run_lane(j, h), kept, 0.0), axis=1, keepdims=True) for h in range(N_HEADS)]
            a = keep(jnp.exp(lb + _pair_dot(c, after_m) + _spread(runs)))
            g = a * _dot_nt(dob, vc[j].reshape(HEAD_COLS, MIX_W))
            before = _pair_dot(g, before_m) + _spread(pres)
            sig = jnp.exp(lb)
            dz = _bf(keep(g * (1.0 - sig) - sig * before))
            dk_ref[rows, :] += _dot_tn(_cols_to_rows(dz), q_rows)
            dv_ref[rows, :] += _dot_tn(_cols_to_rows(_bf(a)), do_rows)
            return tuple(pr + gs for pr, gs in zip(pres, _head_sums(g))), dq + _dot(dz, kj)

        init = ((zero,) * N_HEADS, jnp.zeros((tq, MIX_W), _F32))
        carry = lax.fori_loop(0, i // 2, lambda m, carry: trip((2 * m + 1) * r, trip(2 * m * r, carry, False), False), init)
        carry = lax.fori_loop(0, i % 2, lambda m, carry: trip((i - 1) * r, carry, False), carry)
        _, dq = trip(i * r, carry, True)
        dq_ref[...] = _bf(dq * scale)

    return pl.pallas_call(
        body, name=name, grid=(n_ex, nq),
        in_specs=[pl.BlockSpec((tq, MIX_W), lambda e, i: (e * nq + i, 7)), pl.BlockSpec((s, MIX_W), lambda e, i: (e, 8)),
                  pl.BlockSpec((s, MIX_W), lambda e, i: (e, 9)), pl.BlockSpec((tq, RUN_LANES), lambda e, i: (e * nq + i, 0)),
                  pl.BlockSpec((tq, MIX_W), lambda e, i: (e * nq + i, 3))],
        out_specs=[pl.BlockSpec((tq, MIX_W), lambda e, i: (e * nq + i, 0)), pl.BlockSpec((s, MIX_W), lambda e, i: (e, 0)),
                   pl.BlockSpec((s, MIX_W), lambda e, i: (e, 0))],
        out_shape=[jax.ShapeDtypeStruct((t, MIX_W), _BF), jax.ShapeDtypeStruct((t, MIX_W), _F32),
                   jax.ShapeDtypeStruct((t, MIX_W), _F32)],
        scratch_shapes=[pltpu.VMEM((nb, N_HEADS, KV_BLOCK, MIX_W), _BF), pltpu.VMEM((nb, N_HEADS, KV_BLOCK, MIX_W), _BF)],
        compiler_params=_params("parallel", "arbitrary"),
    )(p, p, p, kept_runs, dmix)


def _fwd_mix(x, w, l, n_ex):
    p, h1 = _norm_mm(x, w["norm1_g"][l], w["w_in_t"][l], "in_proj")
    y_a = _mix_a_fwd(p, w["conv_a_w"][l], n_ex, "mix_a_fwd")
    y_b, cb = _mix_b_fwd(p, w["conv_b_w"][l], w["conv_b_b"][l], w["ln_b_g"][l], w["ln_b_b"][l], n_ex, "mix_b_fwd")
    y_c = _mix_c_fwd(p, w["ln_c_g"][l], w["ln_c_b"][l], w["sgu_w"][l], w["sgu_b_full"][l], n_ex, "mix_c_fwd")
    y_d, runs_d = _mix_d_fwd(p, n_ex, "mix_d_fwd")
    return dict(x=x, h1=h1, p=p, cb=cb, runs_d=runs_d, mix=(y_a, y_b, y_c, y_d))


def _fwd_ffn(st, w, l, n_ex):
    x1, up_pre, h2 = _res_norm_mm(st["mix"], w["w_out"][l], st["x"], w["norm2_g"][l], w["w_up_t"][l], "out_up_proj")
    act, conv_g, conv_v = _ffn_mid_fwd(up_pre, w["conv_f_w"][l], n_ex, "ffn_mid_fwd")
    st.update(x1=x1, h2=h2, up_pre=up_pre, act=act, conv_g=conv_g, conv_v=conv_v)


def _down_proj(st, w, l):
    return _mm_res((st["act"],), w["w_down"][l], st["x1"], "down_proj")


def _down_proj_loss(st, w, l, target):
    return _res_final_loss((st["act"],), w["w_down"][l], st["x1"], w["final_g"], target, "down_proj_loss")


def _bwd_ffn(st, w, l, dx, dxb, n_ex):
    g = {}
    dact = _mm_nt(dxb, w["w_down"][l], "down_proj_dx")
    g["w_down"] = _mm_tn(st["act"], dxb, "down_proj_dw", _BF)
    dup_g, dup_v, dwf_g, dwf_v = _ffn_mid_bwd(
        st["up_pre"], st["conv_g"], st["conv_v"], w["conv_f_w"][l], dact, n_ex, "ffn_mid_bwd")
    g["conv_f_w"] = jnp.concatenate([dwf_g, dwf_v], axis=1)
    dx, dxb, g["norm2_g"] = _mm_normbwd((dup_g, dup_v), w["w_up_t"][l], st["x1"], w["norm2_g"][l], dx, "up_proj_dx")
    g["w_up_t"] = _mm_tn_halves(dup_g, dup_v, st["h2"], "up_proj_dw")
    return dx, dxb, g


def _bwd_out_proj(st, w, l, dxb):
    return _mm_nt(dxb, w["w_out"][l], "out_proj_dx"), _mm_tn_parts(st["mix"], dxb, "out_proj_dw")


def _bwd_mixers(st, w, l, dx, dmix, n_ex):
    g = {}
    p = st["p"]
    dp_a, g["conv_a_w"] = _mix_a_bwd(p, w["conv_a_w"][l], dmix, n_ex, "mix_a_bwd")
    dp_b, g["conv_b_w"], g["conv_b_b"], g["ln_b_g"], g["ln_b_b"] = _mix_b_bwd(
        p, st["cb"], w["conv_b_w"][l], w["ln_b_g"][l], w["ln_b_b"][l], dmix, n_ex, "mix_b_bwd")
    dp_c, g["ln_c_g"], g["ln_c_b"], g["sgu_w"], g["sgu_b_t"] = _mix_c_bwd(
        p, w["ln_c_g"][l], w["ln_c_b"][l], w["sgu_w"][l], w["sgu_b_full"][l], dmix, n_ex, "mix_c_bwd")
    dq, dk, dv = _mix_d_bwd(p, st["runs_d"], dmix, n_ex, "mix_d_bwd")
    dp = (dp_a, dp_b, dp_c, dq, dk, dv)
    dx, dxb, g["norm1_g"] = _mm_normbwd(dp, w["w_in_t"][l], st["x"], w["norm1_g"][l], dx, "in_proj_dx")
    return dx, dxb, g, dp


def _bwd_mix(st, w, l, dx, dxb, n_ex):
    dmix, dw_out = _bwd_out_proj(st, w, l, dxb)
    dx, dxb, g, dp = _bwd_mixers(st, w, l, dx, dmix, n_ex)
    g["w_out"] = dw_out
    g["w_in_t"] = _mm_tn_parts(dp, st["h1"], "in_proj_dw")
    return dx, dxb, g


_MESH = pl.DeviceIdType.MESH
_ANY = pl.BlockSpec(memory_space=pl.ANY)


def _position():
    return lax.axis_index("x"), lax.axis_index("y"), lax.axis_index("c")


def _flat(px, py, pc):
    return 4 * px + 2 * py + pc


def _all_gather(shard, name, after):
    r, c_ = shard.shape

    def body(x_ref, after_ref, out_ref, send_sems, recv_sems, local_sem):
        x, y, c = _position()
        me, sibling = (x, y, c), (x, y, 1 - c)
        chips = [(1 - x, y), (x, 1 - y), (1 - x, 1 - y)]

        def copy(k, block, to, src=None):
            slab = out_ref.at[_flat(*block)]
            return pltpu.make_async_remote_copy(
                src_ref=slab if src is None else src, dst_ref=slab, send_sem=send_sems.at[k], recv_sem=recv_sems.at[k],
                device_id=to, device_id_type=_MESH)

        mine = pltpu.make_async_copy(x_ref, out_ref.at[_flat(*me)], local_sem)
        mine.start()
        first = [copy(0, me, sibling, src=x_ref)]
        first += [copy(1 + j, me, (*chip, c), src=x_ref) for j, chip in enumerate(chips)]
        for cp in first:
            cp.start()
        passed = [copy(4 + j, (*chip, c), sibling) for j, chip in enumerate(chips)]
        for j, chip in enumerate(chips):
            copy(1 + j, (*chip, c), me).wait_recv()
            passed[j].start()
        copy(0, sibling, me).wait_recv()
        for j, chip in enumerate(chips):
            copy(4 + j, (*chip, 1 - c), me).wait_recv()
        for cp in first + passed:
            cp.wait_send()
        mine.wait()

    return pl.pallas_call(
        body, name=name, out_shape=jax.ShapeDtypeStruct((N_DEV, r, c_), shard.dtype),
        in_specs=[_ANY, _ANY], out_specs=_ANY,
        scratch_shapes=[pltpu.SemaphoreType.DMA((7,)), pltpu.SemaphoreType.DMA((7,)), pltpu.SemaphoreType.DMA],
    )(shard, after)


_HBM = pl.BlockSpec(memory_space=pltpu.HBM)
_SEM = pl.BlockSpec(memory_space=pltpu.SEMAPHORE)
_DATAFLOW = pltpu.SideEffectType.DATAFLOW_SIDE_EFFECTING


def _peers(x, y, c):
    return [((1 - x) if (k + 1) & 4 else x, (1 - y) if (k + 1) & 2 else y, (1 - c) if (k + 1) & 1 else c)
            for k in range(N_DEV - 1)]


def _direct_copies(src_refs, land_refs, send_sems, recv_sems, to_all):
    x, y, c = _position()
    my = _flat(x, y, c)
    out, back = [], []
    for m, (src_ref, land_ref) in enumerate(zip(src_refs, land_refs)):
        for k, peer in enumerate(_peers(x, y, c)):
            src = src_ref if to_all else src_ref.at[_flat(*peer)]
            n = m * (N_DEV - 1) + k
            sems = dict(send_sem=send_sems.at[n], recv_sem=recv_sems.at[n], device_id=peer, device_id_type=_MESH)
            out.append(pltpu.make_async_remote_copy(src_ref=src, dst_ref=land_ref.at[my], **sems))
            back.append(pltpu.make_async_remote_copy(src_ref=src, dst_ref=land_ref.at[_flat(*peer)], **sems))
    return out, back


def _exchange_start(srcs, to_all, after, name):
    n = len(srcs)
    n_sems = n * (N_DEV - 1)
    land_shapes = [(N_DEV,) + tuple(a.shape[-2:]) for a in srcs]

    def body(*refs):
        src_refs, land_refs = refs[:n], refs[n:2 * n]
        send_sems, recv_sems = refs[2 * n + 1], refs[2 * n + 2]
        token = refs[-1]
        for cp in _direct_copies(src_refs, land_refs, send_sems, recv_sems, to_all)[0]:
            cp.start()
        token[...] = jnp.zeros_like(token)

    lands = [pltpu.with_memory_space_constraint(lax.empty(shp, a.dtype), pltpu.HBM) for shp, a in zip(land_shapes, srcs)]
    outs = pl.pallas_call(
        body, name=name,
        out_shape=(pltpu.SemaphoreType.DMA((n_sems,)), pltpu.SemaphoreType.DMA((n_sems,)),
                   *[pltpu.HBM(a.shape, a.dtype) for a in srcs], *[pltpu.HBM(shp, a.dtype) for shp, a in zip(land_shapes, srcs)],
                   jax.ShapeDtypeStruct((8, 128), _F32)),
        in_specs=(_HBM,) * (2 * n) + (_ANY,),
        out_specs=(_SEM, _SEM) + (_HBM,) * (2 * n) + (pl.BlockSpec(memory_space=pltpu.VMEM),),
        input_output_aliases={i: 2 + i for i in range(2 * n)},
        compiler_params=pltpu.CompilerParams(has_side_effects=_DATAFLOW),
    )(*[pltpu.with_memory_space_constraint(a, pltpu.HBM) for a in srcs], *lands, after)
    return (outs[0], outs[1], outs[2:2 + n], outs[2 + n:2 + 2 * n], to_all), outs[-1]


def _exchange_wait(handle, after, name):
    send_sems, recv_sems, srcs, lands, to_all = handle
    n = len(srcs)

    def body(*refs):
        out, back = _direct_copies(refs[:n], refs[n:2 * n], refs[2 * n], refs[2 * n + 1], to_all)
        for cp in out:
            cp.wait_send()
        for cp in back:
            cp.wait_recv()

    outs = pl.pallas_call(
        body, name=name,
        out_shape=tuple(pltpu.HBM(a.shape, a.dtype) for a in (*srcs, *lands)),
        in_specs=(_HBM,) * (2 * n) + (_SEM, _SEM, _ANY), out_specs=(_HBM,) * (2 * n),
        input_output_aliases={i: i for i in range(2 * n)},
        compiler_params=pltpu.CompilerParams(has_side_effects=_DATAFLOW),
    )(*srcs, *lands, send_sems, recv_sems, after)
    return outs[:n], outs[n:]


def _with_own(landed, own):
    my = _flat(*_position())
    return lax.dynamic_update_slice(landed, own[None], (my, 0, 0))


def _sum_slabs(slabs, own, name):
    n, r, c_ = slabs.shape
    tr = _pick_tile(r, 16, max(16, (12 << 20) // (n * c_ * slabs.dtype.itemsize)))

    def body(x_ref, own_ref, o_ref):
        my = _flat(*_position())
        acc = None
        for k in range(n):
            term = jnp.where(my == k, own_ref[...], x_ref[k]).astype(_F32)
            acc = term if acc is None else acc + term
        o_ref[...] = acc

    return pl.pallas_call(
        body, name=name, grid=(r // tr,),
        in_specs=[pl.BlockSpec((n, tr, c_), lambda i: (0, i, 0)), pl.BlockSpec((tr, c_), lambda i: (i, 0))],
        out_specs=pl.BlockSpec((tr, c_), lambda i: (i, 0)),
        out_shape=jax.ShapeDtypeStruct((r, c_), _F32),
        compiler_params=_params("parallel"),
    )(slabs, own)


def _adamw(w, g, m, v, name):
    r, c_ = w.shape
    tr = _pick_tile(r, 8, 512)

    def body(w_ref, g_ref, m_ref, v_ref, d_ref, nm_ref, nv_ref):
        _adamw_refs(w_ref, g_ref, m_ref, v_ref, d_ref, nm_ref, nv_ref)

    spec = pl.BlockSpec((tr, c_), lambda i: (i, 0))
    shape = jax.ShapeDtypeStruct((r, c_), _F32)
    return pl.pallas_call(
        body, name=name, grid=(r // tr,), in_specs=[spec] * 4, out_specs=[spec] * 3, out_shape=[shape] * 3,
        compiler_params=_params("parallel"),
    )(w, g, m, v)


def _adamw_refs(w_ref, g_ref, m_ref, v_ref, d_ref, nm_ref, nv_ref):
    gv = g_ref[...]
    nm = ADAM_B1 * m_ref[...] + (1.0 - ADAM_B1) * gv
    nv = ADAM_B2 * v_ref[...] + (1.0 - ADAM_B2) * (gv * gv)
    m_hat = nm / (1.0 - ADAM_B1 ** ADAM_STEP)
    v_hat = nv / (1.0 - ADAM_B2 ** ADAM_STEP)
    d_ref[...] = -ADAM_LR * (m_hat / (jnp.sqrt(v_hat) + ADAM_EPS) + ADAM_WD * w_ref[...])
    nm_ref[...] = nm
    nv_ref[...] = nv


def _adamw_small(params, name):
    n = len(params)

    def body(*refs):
        for i in range(n):
            _adamw_refs(*refs[4 * i:4 * i + 4], *refs[4 * n + 3 * i:4 * n + 3 * i + 3])

    outs = pl.pallas_call(
        body, name=name,
        out_shape=[jax.ShapeDtypeStruct(p[0].shape, _F32) for p in params for _ in range(3)],
        compiler_params=pltpu.CompilerParams(vmem_limit_bytes=VMEM_LIMIT),
    )(*[a for p in params for a in p])
    return [tuple(outs[3 * i:3 * i + 3]) for i in range(n)]


_SMALL = ("norm1_g", "conv_a_w", "conv_b_w", "conv_b_b", "ln_b_g", "ln_b_b", "ln_c_g", "ln_c_b", "sgu_w", "sgu_b",
          "norm2_g", "conv_f_w", "final_g")
_CONV_SHARDED = ("conv_a_w", "conv_b_w", "conv_f_w")
_NAMES = ("norm1_g", "w_in", "conv_a_w", "conv_b_w", "conv_b_b", "ln_b_g", "ln_b_b", "ln_c_g", "ln_c_b", "sgu_w", "sgu_b",
          "w_out", "norm2_g", "w_up", "conv_f_w", "w_down", "final_g")


def _pack_rows(parts, lanes=128, row_multiple=8):
    flat = jnp.concatenate([a.reshape(-1) for a in parts])
    rows = -(-flat.shape[0] // lanes)
    rows = -(-rows // row_multiple) * row_multiple
    return jnp.pad(flat, (0, rows * lanes - flat.shape[0])).reshape(rows, lanes)


def _unpack_rows(packed, shapes):
    flat = packed.reshape(-1)
    out, off = [], 0
    for shp in shapes:
        size = 1
        for s in shp:
            size *= s
        out.append(flat[off:off + size].reshape(shp))
        off += size
    return out


def _gather_conv_weights(conv_a_w, conv_b_w, conv_f_w, after):
    shards = (conv_a_w, conv_b_w, conv_f_w)
    flat = _all_gather(_pack_rows(shards), "gather_conv_weights", after).reshape(N_DEV, -1)
    full, off = [], 0
    for s in shards:
        layers, taps, width = s.shape
        per_dev = flat[:, off:off + s.size].reshape(N_DEV, layers, taps, width)
        full.append(jnp.moveaxis(per_dev, 0, 2).reshape(layers, taps, N_DEV * width))
        off += s.size
    return full


def kernel(x, norm1_g, w_in, conv_a_w, conv_b_w, conv_b_b, ln_b_g, ln_b_b, ln_c_g, ln_c_b, sgu_w, sgu_b, w_out, norm2_g, w_up, conv_f_w, w_down, final_g, loss_target, m_norm1_g, m_w_in, m_conv_a_w, m_conv_b_w, m_conv_b_b, m_ln_b_g, m_ln_b_b, m_ln_c_g, m_ln_c_b, m_sgu_w, m_sgu_b, m_w_out, m_norm2_g, m_w_up, m_conv_f_w, m_w_down, m_final_g, v_norm1_g, v_w_in, v_conv_a_w, v_conv_b_w, v_conv_b_b, v_ln_b_g, v_ln_b_b, v_ln_c_g, v_ln_c_b, v_sgu_w, v_sgu_b, v_w_out, v_norm2_g, v_w_up, v_conv_f_w, v_w_down, v_final_g):
    weights = dict(norm1_g=norm1_g, w_in=w_in, conv_a_w=conv_a_w, conv_b_w=conv_b_w, conv_b_b=conv_b_b, ln_b_g=ln_b_g,
                   ln_b_b=ln_b_b, ln_c_g=ln_c_g, ln_c_b=ln_c_b, sgu_w=sgu_w, sgu_b=sgu_b, w_out=w_out, norm2_g=norm2_g,
                   w_up=w_up, conv_f_w=conv_f_w, w_down=w_down, final_g=final_g)
    mom1 = dict(norm1_g=m_norm1_g, w_in=m_w_in, conv_a_w=m_conv_a_w, conv_b_w=m_conv_b_w, conv_b_b=m_conv_b_b,
                ln_b_g=m_ln_b_g, ln_b_b=m_ln_b_b, ln_c_g=m_ln_c_g, ln_c_b=m_ln_c_b, sgu_w=m_sgu_w, sgu_b=m_sgu_b,
                w_out=m_w_out, norm2_g=m_norm2_g, w_up=m_w_up, conv_f_w=m_conv_f_w, w_down=m_w_down, final_g=m_final_g)
    mom2 = dict(norm1_g=v_norm1_g, w_in=v_w_in, conv_a_w=v_conv_a_w, conv_b_w=v_conv_b_w, conv_b_b=v_conv_b_b,
                ln_b_g=v_ln_b_g, ln_b_b=v_ln_b_b, ln_c_g=v_ln_c_g, ln_c_b=v_ln_c_b, sgu_w=v_sgu_w, sgu_b=v_sgu_b,
                w_out=v_w_out, norm2_g=v_norm2_g, w_up=v_w_up, conv_f_w=v_conv_f_w, w_down=v_w_down, final_g=v_final_g)
    n_ex, seq, d = x.shape
    depth = w_in.shape[0]
    assert depth == 2
    my = _flat(*_position())
    row = lambda a, l: a[l][None]
    tied = lambda a, token: a + token[0:1, 0:1]

    slab = {"w_in": [_bf(jnp.swapaxes(w_in, 1, 2)[l]) for l in range(depth)], "w_out": [_bf(w_out[l]) for l in range(depth)],
            "w_up": [_bf(jnp.swapaxes(w_up, 1, 2)[l]) for l in range(depth)], "w_down": [_bf(w_down[l]) for l in range(depth)]}
    rows = {name: parts[0].shape[0] for name, parts in slab.items()}
    key_of = {"w_in": "w_in_t", "w_out": "w_out", "w_up": "w_up_t", "w_down": "w_down"}
    rest_layer0 = [("w_out", 0), ("w_up", 0), ("w_down", 0)]
    all_layer1 = [("w_in", 1), ("w_out", 1), ("w_up", 1), ("w_down", 1)]

    w_in0 = _all_gather(slab["w_in"][0], "gather_w_in0", norm1_g)
    conv_a_full, conv_b_full, conv_f_full = _gather_conv_weights(conv_a_w, conv_b_w, conv_f_w, w_in0)
    gather0, token = _exchange_start([slab[n][l] for n, l in rest_layer0], True, conv_f_full, "gather_layer0_start")
    w = {
        "norm1_g": [row(norm1_g, l) for l in range(depth)], "w_in_t": [None] * depth,
        "conv_a_w": [conv_a_full[l] for l in range(depth)], "conv_b_w": [conv_b_full[l] for l in range(depth)],
        "conv_b_b": [row(conv_b_b, l) for l in range(depth)], "ln_b_g": [row(ln_b_g, l) for l in range(depth)],
        "ln_b_b": [row(ln_b_b, l) for l in range(depth)], "ln_c_g": [row(ln_c_g, l) for l in range(depth)],
        "ln_c_b": [row(ln_c_b, l) for l in range(depth)], "sgu_w": [sgu_w[l] for l in range(depth)],
        "sgu_b_full": [jnp.repeat(sgu_b[l].T, HEAD_DIM, axis=1) for l in range(depth)],
        "w_out": [None] * depth, "norm2_g": [row(norm2_g, l) for l in range(depth)], "w_up_t": [None] * depth,
        "conv_f_w": [conv_f_full[l] for l in range(depth)], "w_down": [None] * depth, "final_g": final_g[None],
    }
    w["w_in_t"][0] = w_in0.reshape(N_DEV * rows["w_in"], d)
    w["norm1_g"][0] = tied(row(norm1_g, 0), token)

    def land_weights(handle, after, which, name):
        owns, landed = _exchange_wait(handle, after, name)
        for (n, l), own, got in zip(which, owns, landed):
            w[key_of[n]][l] = _with_own(got, own).reshape(N_DEV * rows[n], d)
        return landed[0]

    st0 = _fwd_mix(x.reshape(n_ex * seq, d), w, 0, n_ex)
    landed0 = land_weights(gather0, st0["mix"][3], rest_layer0, "gather_layer0_wait")
    gather1, token = _exchange_start([slab[n][l] for n, l in all_layer1], True, landed0, "gather_layer1_start")
    w["norm2_g"][0] = tied(row(norm2_g, 0), token)
    _fwd_ffn(st0, w, 0, n_ex)
    x_mid = _down_proj(st0, w, 0)
    land_weights(gather1, x_mid, all_layer1, "gather_layer1_wait")
    st1 = _fwd_mix(x_mid, w, 1, n_ex)
    _fwd_ffn(st1, w, 1, n_ex)
    dx, dxb, d_final_g, loss_part = _down_proj_loss(st1, w, 1, loss_target.reshape(n_ex * seq, d))

    def send_grads(g, which, after, name):
        return _exchange_start([g[key_of[n]].reshape(N_DEV, rows[n], d) for n, _ in which], False, after, name)

    dx, dxb, g_ffn1 = _bwd_ffn(st1, w, 1, dx, dxb, n_ex)
    dx, dxb, g_mix1 = _bwd_mix(st1, w, 1, dx, dxb, n_ex)
    grads1, token = send_grads({**g_ffn1, **g_mix1}, all_layer1, dx, "exchange_layer1_start")
    w["norm2_g"][0] = tied(row(norm2_g, 0), token)
    dx, dxb, g_ffn0 = _bwd_ffn(st0, w, 0, dx, dxb, n_ex)
    g_ffn0["w_out"] = _mm_tn_parts(st0["mix"], dxb, "out_proj_dw")
    ffn_layer0 = [("w_out", 0), ("w_up", 0), ("w_down", 0)]
    grads0a, token = send_grads(g_ffn0, ffn_layer0, dxb, "exchange_ffn0_start")
    dmix = _mm_nt(dxb, w["w_out"][0], "out_proj_dx", after=token)
    dx, dxb, g_mix0, dp0 = _bwd_mixers(st0, w, 0, dx, dmix, n_ex)
    grad_x = dx.reshape(n_ex, seq, d)
    g = {k: [{**g_ffn0, **g_mix0}[k], {**g_ffn1, **g_mix1}[k]] for k in g_mix0.keys() | g_ffn0.keys()}
    g["final_g"] = d_final_g

    small_local = {
        "norm1_g": jnp.stack([a[0] for a in g["norm1_g"]]), "conv_a_w": jnp.stack(g["conv_a_w"]),
        "conv_b_w": jnp.stack(g["conv_b_w"]), "conv_b_b": jnp.stack([a[0] for a in g["conv_b_b"]]),
        "ln_b_g": jnp.stack([a[0] for a in g["ln_b_g"]]), "ln_b_b": jnp.stack([a[0] for a in g["ln_b_b"]]),
        "ln_c_g": jnp.stack([a[0] for a in g["ln_c_g"]]), "ln_c_b": jnp.stack([a[0] for a in g["ln_c_b"]]),
        "sgu_w": jnp.stack(g["sgu_w"]), "sgu_b": jnp.stack([a.T for a in g["sgu_b_t"]]),
        "norm2_g": jnp.stack([a[0] for a in g["norm2_g"]]), "conv_f_w": jnp.stack(g["conv_f_w"]),
        "final_g": g["final_g"][0],
    }
    small_parts = [small_local[k] for k in _SMALL] + [loss_part.reshape(1)]
    small, token = _exchange_start([_pack_rows(small_parts)], True, dx, "gather_small_start")
    g_mix0["w_in_t"] = _mm_tn_parts(dp0, st0["h1"], "in_proj_dw", after=token)
    mix_layer0 = [("w_in", 0)]
    grads0b, token = send_grads(g_mix0, mix_layer0, dx, "exchange_mix0_start")

    reduced = {}

    def land_grads(handle, after, which, name):
        sent, landed = _exchange_wait(handle, after, name + "_wait")
        for (n, l), src, got in zip(which, sent, landed):
            own = lax.dynamic_index_in_dim(src, my, 0, keepdims=False)
            reduced[(n, l)] = _sum_slabs(got, own, name + "_sum_" + n)
        return reduced[which[-1]]

    def stacked_grad(name):
        stacked = jnp.stack([reduced[(name, l)] for l in range(depth)])
        return jnp.swapaxes(stacked, 1, 2) if name in ("w_in", "w_up") else stacked

    done = land_grads(grads1, token, all_layer1, "exchange_layer1")
    land_grads(grads0a, done, ffn_layer0, "exchange_ffn0")
    grads = {name: stacked_grad(name) for name in ("w_out", "w_up", "w_down")}

    delta, new_m, new_v = {}, {}, {}

    def as_2d(name):
        shp = weights[name].shape
        two_d = (-1, shp[-1]) if len(shp) > 1 else (1, shp[0])
        return tuple(a.reshape(two_d) for a in (weights[name], grads[name], mom1[name], mom2[name]))

    def keep(name, outs):
        delta[name], new_m[name], new_v[name] = (o.reshape(weights[name].shape) for o in outs)

    for name in ("w_up", "w_down", "w_out"):
        keep(name, _adamw(*as_2d(name), "adamw_" + name))

    (own,), (landed,) = _exchange_wait(small, new_v["w_out"], "gather_small_wait")
    small_sum = _sum_slabs(landed, own, "sum_small_grads")
    *small_totals, loss_total = _unpack_rows(small_sum, [a.shape for a in small_parts])
    loss = loss_total[0]
    for name, total in zip(_SMALL, small_totals):
        if name in _CONV_SHARDED:
            width = weights[name].shape[-1]
            total = lax.dynamic_slice_in_dim(total, my * width, width, axis=-1)
        grads[name] = total
    at_least_2d = lambda a: a[None] if a.ndim == 1 else a
    small_params = [tuple(at_least_2d(a) for a in (weights[n], grads[n], mom1[n], mom2[n])) for n in _SMALL]
    for name, outs in zip(_SMALL, _adamw_small(small_params, "adamw_small")):
        keep(name, outs)

    land_grads(grads0b, new_v["final_g"], mix_layer0, "exchange_mix0")
    grads["w_in"] = stacked_grad("w_in")
    keep("w_in", _adamw(*as_2d("w_in"), "adamw_w_in"))

    return (loss, grad_x, *[grads[n] for n in _NAMES], *[delta[n] for n in _NAMES], *[new_m[n] for n in _NAMES],
            *[new_v[n] for n in _NAMES])
```
